```python
import math
import jax, jax.numpy as jnp
from jax import lax
import numpy as np

D_MODEL = 1024
BATCH = 4
SEQ = 4096
DEPTH = 2
DEC_BATCH = 128
DEC_SEQ = 8
PAST_LEN = 2048
PAGE_SIZE = 128

H_A = 4
DK_A = 64
DV_A = 2 * DK_A
H_B = 4
G_B = 2
R_B = H_B // G_B
DH_B = 128
CMP_LEN = 32
CMP_STRIDE = 16
SEL_BLOCK = 64
TOP_N = 16
WINDOW = 512
H_C = 4
DK_C = D_MODEL // 2 // H_C
DV_C = D_MODEL // H_C
GATE_RANK = 16
GATE_TAU = 16.0
GLA_CHUNK = 64
D_FF = 4 * D_MODEL
ROPE_THETA = 10000.0
Q_BLOCK = 128
EPS = 1e-6
N_EVEN = (DEPTH + 1) // 2
N_ODD = DEPTH // 2
EVEN_SIZES = [H_A * 2 * DK_A, H_A * 2 * DK_A, H_A * DV_A, H_B * DH_B] + [G_B * DH_B] * 6 + [H_B * 3]
EVEN_IN = sum(EVEN_SIZES)
EVEN_MIX = H_A * DV_A + H_B * DH_B
ODD_SIZES = [H_C * DK_C, H_C * DK_C, H_C * DV_C, H_C * DV_C, GATE_RANK]
ODD_IN = sum(ODD_SIZES)

kernel_name = "hybrid_diffattn_nsa_gla_adaln_step"


def rms_norm(x, w):
    xf = x.astype(jnp.float32)
    y = xf * lax.rsqrt(jnp.mean(xf * xf, axis=-1, keepdims=True) + EPS)
    return (y * w.astype(jnp.float32)).astype(x.dtype)


def masked_softmax(s, mask):
    s = jnp.where(mask, s.astype(jnp.float32), -jnp.inf)
    m = jnp.max(s, axis=-1, keepdims=True)
    m = jnp.where(jnp.isfinite(m), m, 0.0)
    p = jnp.exp(s - m)
    return p / jnp.maximum(jnp.sum(p, axis=-1, keepdims=True), 1e-30)


def rope(x, pos):
    d = x.shape[-1]
    inv = ROPE_THETA ** (-jnp.arange(0, d, 2, dtype=jnp.float32) / d)
    ang = pos.astype(jnp.float32)[:, None] * inv[None, :]
    ang = ang.reshape(ang.shape[:1] + (1,) * (x.ndim - 3) + ang.shape[1:])
    cos, sin = jnp.cos(ang), jnp.sin(ang)
    xf = x.astype(jnp.float32)
    x1, x2 = xf[..., : d // 2], xf[..., d // 2:]
    return jnp.concatenate([x1 * cos - x2 * sin, x2 * cos + x1 * sin], axis=-1).astype(x.dtype)


def adaln(c, w, b):
    mod = (jax.nn.silu(c) @ w + b).reshape(c.shape[0], 6, D_MODEL)
    return [mod[:, i, None, :] for i in range(6)]


def modulate(h, shift, scale):
    return h * (1.0 + scale) + shift


def sqrelu_mlp(h, w1, w2):
    return jnp.square(jax.nn.relu(h @ w1)) @ w2


def residual_layer(y, mod, n1, n2, mixer, w1, w2):
    h = modulate(rms_norm(y, n1), mod[0], mod[1])
    out, state = mixer(h)
    y = y + (mod[2] * out).astype(y.dtype)
    h = modulate(rms_norm(y, n2), mod[3], mod[4])
    y = y + (mod[5] * sqrelu_mlp(h, w1, w2)).astype(y.dtype)
    return y, state


def diff_lambda(lw, lam_init):
    lw = lw.astype(jnp.float32)
    return jnp.exp(jnp.sum(lw[0] * lw[1])) - jnp.exp(jnp.sum(lw[2] * lw[3])) + lam_init


def diff_attention(q, q_pos, k, v, k_pos, lam):
    N, Q = q.shape[0], q.shape[1]
    qb = math.gcd(Q, Q_BLOCK)
    nb = Q // qb
    qs = q.reshape((N, nb, qb) + q.shape[2:]).swapaxes(0, 1)
    ps = q_pos.reshape(nb, qb)

    def block(args):
        qi, pi = args
        s = jnp.einsum('nqhmd,nkhmd->nhmqk', qi, k) * DK_A ** -0.5
        p = masked_softmax(s, k_pos[None, :] <= pi[:, None])
        a = p[:, :, 0] - lam * p[:, :, 1]
        return jnp.einsum('nhqk,nkhe->nqhe', a.astype(v.dtype), v)

    o = lax.map(block, (qs, ps))
    return o.swapaxes(0, 1).reshape(N, Q, H_A, DV_A)


def compress(x, pe, w1, w2):
    N, L = x.shape[0], x.shape[1]
    n_cmp = (L - CMP_LEN) // CMP_STRIDE + 1
    idx = CMP_STRIDE * jnp.arange(n_cmp)[:, None] + jnp.arange(CMP_LEN)[None, :]
    blk = x[:, idx] + pe[:, None, :]
    blk = blk.transpose(0, 1, 3, 2, 4).reshape(N, n_cmp, G_B, CMP_LEN * DH_B)
    return jax.nn.silu(blk @ w1) @ w2


def block_cover(n_cmp, n_sel):
    cs = CMP_STRIDE * jnp.arange(n_cmp)[:, None]
    ss = SEL_BLOCK * jnp.arange(n_sel)[None, :]
    return ((cs < ss + SEL_BLOCK) & (cs + CMP_LEN > ss)).astype(jnp.float32)


def sel_attend(q, q_pos, idx, kb, vb):
    N, Qc = q.shape[0], q.shape[1]
    K = idx.shape[-1]
    n_i = jnp.arange(N)[:, None, None, None]
    g_i = jnp.arange(G_B)[None, :, None, None]
    ksel = kb[n_i, g_i, idx].reshape(N, G_B, Qc, K * SEL_BLOCK, DH_B)
    vsel = vb[n_i, g_i, idx].reshape(N, G_B, Qc, K * SEL_BLOCK, DH_B)
    s = jnp.einsum('nqgrd,ngqmd->ngrqm', q, ksel) * DH_B ** -0.5
    tok = (idx[..., None] * SEL_BLOCK + jnp.arange(SEL_BLOCK)).reshape(N, G_B, Qc, K * SEL_BLOCK)
    mask = (tok <= q_pos[None, None, :, None])[:, :, None]
    p = masked_softmax(s, mask)
    return jnp.einsum('ngrqm,ngqmd->nqgrd', p.astype(vsel.dtype), vsel)


def nsa_cmp_sel(q, q_pos, kc, vc, ks, vs, pe, w1, w2, sel_chunk):
    N, L = kc.shape[0], kc.shape[1]
    Q = q.shape[1]
    kcc = compress(kc, pe[0], w1[0], w2[0])
    vcc = compress(vc, pe[1], w1[1], w2[1])
    n_cmp = kcc.shape[1]
    s = jnp.einsum('nqgrd,nkgd->ngrqk', q, kcc) * DH_B ** -0.5
    cmp_end = CMP_STRIDE * jnp.arange(n_cmp) + (CMP_LEN - 1)
    p_cmp = masked_softmax(s, cmp_end[None, :] <= q_pos[:, None])
    o_cmp = jnp.einsum('ngrqk,nkgd->nqgrd', p_cmp.astype(vcc.dtype), vcc)
    n_sel = -(-L // SEL_BLOCK)
    imp = jnp.einsum('ngrqk,ks->ngqs', p_cmp, block_cover(n_cmp, n_sel))
    blk = jnp.arange(n_sel)[None, :]
    valid = blk * SEL_BLOCK <= q_pos[:, None]
    forced = (blk == 0) | (blk == (q_pos // SEL_BLOCK)[:, None])
    imp = jnp.where(forced, jnp.inf, jnp.where(valid, imp, -jnp.inf))
    K = min(TOP_N, n_sel)
    _, idx = lax.top_k(imp, K)
    pad = n_sel * SEL_BLOCK - L

    def to_blocks(x):
        x = jnp.pad(x, ((0, 0), (0, pad), (0, 0), (0, 0)))
        return x.reshape(N, n_sel, SEL_BLOCK, G_B, DH_B).transpose(0, 3, 1, 2, 4)

    kb, vb = to_blocks(ks), to_blocks(vs)
    nc = Q // sel_chunk
    qs = q.reshape(N, nc, sel_chunk, G_B, R_B, DH_B).swapaxes(0, 1)
    ids = idx.reshape(N, G_B, nc, sel_chunk, K).transpose(2, 0, 1, 3, 4)
    ps = q_pos.reshape(nc, sel_chunk)
    o_sel = lax.map(lambda a: sel_attend(a[0], a[1], a[2], kb, vb), (qs, ps, ids))
    o_sel = o_sel.swapaxes(0, 1).reshape(N, Q, G_B, R_B, DH_B)
    return o_cmp, o_sel


def window_attend(q, q_pos, k, v, k_pos):
    s = jnp.einsum('nqgrd,nkgd->ngrqk', q, k) * DH_B ** -0.5
    kp, qp = k_pos[None, :], q_pos[:, None]
    mask = (kp >= 0) & (kp <= qp) & (kp > qp - WINDOW)
    p = masked_softmax(s, mask)
    return jnp.einsum('ngrqk,nkgd->nqgrd', p.astype(v.dtype), v)


def even_project(h, pos, w_in):
    N, T = h.shape[0], h.shape[1]
    parts = jnp.split(h @ w_in, np.cumsum(EVEN_SIZES)[:-1].tolist(), axis=-1)
    qa = rope(parts[0].reshape(N, T, H_A, 2, DK_A), pos)
    ka = rope(parts[1].reshape(N, T, H_A, 2, DK_A), pos)
    va = parts[2].reshape(N, T, H_A, DV_A)
    qb = rope(parts[3].reshape(N, T, G_B, R_B, DH_B), pos)
    kc, vc, ks, vs, kw, vw = [p.reshape(N, T, G_B, DH_B) for p in parts[4:10]]
    kc, ks, kw = rope(kc, pos), rope(ks, pos), rope(kw, pos)
    gates = jax.nn.sigmoid(parts[10].astype(jnp.float32)).reshape(N, T, G_B, R_B, 3)
    return qa, ka, va, qb, kc, vc, ks, vs, kw, vw, gates


def even_output(oa, o_cmp, o_sel, o_win, gates, lam_init, subln_w, w_out):
    N, T = oa.shape[0], oa.shape[1]
    oa = rms_norm(oa, subln_w) * (1.0 - lam_init)
    ob = gates[..., 0:1] * o_cmp + gates[..., 1:2] * o_sel + gates[..., 2:3] * o_win
    o = jnp.concatenate([oa.reshape(N, T, -1), ob.reshape(N, T, -1).astype(oa.dtype)], axis=-1)
    return o @ w_out


def even_mixer_prompt(h, w_in, w_out, lam_w, subln_w, pe, w1, w2, lam_init):
    N, T = h.shape[0], h.shape[1]
    pos = jnp.arange(T)
    qa, ka, va, qb, kc, vc, ks, vs, kw, vw, gates = even_project(h, pos, w_in)
    oa = diff_attention(qa, pos, ka, va, pos, diff_lambda(lam_w, lam_init))
    qblk = math.gcd(T, Q_BLOCK)
    o_cmp, o_sel = nsa_cmp_sel(qb, pos, kc, vc, ks, vs, pe, w1, w2, qblk)
    nb = T // qblk
    band = (jnp.arange(nb) * qblk)[:, None] + jnp.arange(WINDOW + qblk)[None, :]
    padw = ((0, 0), (WINDOW, 0), (0, 0), (0, 0))
    kband = jnp.pad(kw, padw)[:, band]
    vband = jnp.pad(vw, padw)[:, band]
    o_win = jax.vmap(window_attend, in_axes=(1, 0, 1, 1, 0), out_axes=1)(
        qb.reshape(N, nb, qblk, G_B, R_B, DH_B), pos.reshape(nb, qblk), kband, vband, band - WINDOW)
    o_win = o_win.reshape(N, T, G_B, R_B, DH_B)
    y = even_output(oa, o_cmp, o_sel, o_win, gates, lam_init, subln_w, w_out)
    wl = min(WINDOW, T)
    return y, (ka, va, jnp.stack([kc, vc], axis=2), jnp.stack([ks, vs], axis=2),
               jnp.stack([kw, vw], axis=2)[:, T - wl:])


def even_mixer_sample(h, c_dk, c_dv, c_cmp, c_sel, win_buf, page_table, w_in, w_out, lam_w, subln_w, pe, w1, w2, lam_init):
    N, S = h.shape[0], h.shape[1]
    P = page_table.shape[1] * c_dk.shape[1]
    pos = P + jnp.arange(S)
    qa, ka, va, qb, kc, vc, ks, vs, kw, vw, gates = even_project(h, pos, w_in)

    def past(pool):
        return pool[page_table].reshape((N, P) + pool.shape[2:])

    k_pos = jnp.arange(P + S)
    oa = diff_attention(qa, pos, jnp.concatenate([past(c_dk), ka], axis=1),
                        jnp.concatenate([past(c_dv), va], axis=1), k_pos, diff_lambda(lam_w, lam_init))
    cmp_new = jnp.stack([kc, vc], axis=2)
    sel_new = jnp.stack([ks, vs], axis=2)
    cmp_all = jnp.concatenate([past(c_cmp), cmp_new], axis=1)
    sel_all = jnp.concatenate([past(c_sel), sel_new], axis=1)
    o_cmp, o_sel = nsa_cmp_sel(qb, pos, cmp_all[:, :, 0], cmp_all[:, :, 1],
                               sel_all[:, :, 0], sel_all[:, :, 1], pe, w1, w2, 1)
    wb = win_buf.shape[1]
    win_all = jnp.concatenate([win_buf, jnp.stack([kw, vw], axis=2)], axis=1)
    o_win = window_attend(qb, pos, win_all[:, :, 0], win_all[:, :, 1], P - wb + jnp.arange(wb + S))
    y = even_output(oa, o_cmp, o_sel, o_win, gates, lam_init, subln_w, w_out)
    return y, (ka, va, cmp_new, sel_new, win_all[:, S:])


def gla_recurrence(q, k, v, log_a, s0):
    N, T = q.shape[0], q.shape[1]
    c = math.gcd(T, GLA_CHUNK)
    nc = T // c

    def chunks(x):
        return x.astype(jnp.float32).reshape(N, nc, c, H_C, x.shape[-1]).transpose(1, 0, 3, 2, 4)

    causal = jnp.tril(jnp.ones((c, c), dtype=bool))

    def step(S, inp):
        qc, kc, vc, gc = inp
        b = jnp.cumsum(gc, axis=2)
        o_inter = jnp.einsum('nhid,nhde->nhie', qc * jnp.exp(b), S)
        dec = jnp.exp(jnp.where(causal[:, :, None], b[:, :, :, None, :] - b[:, :, None, :, :], -jnp.inf))
        att = jnp.einsum('nhid,nhjd,nhijd->nhij', qc, kc, dec)
        o = o_inter + jnp.einsum('nhij,nhje->nhie', att, vc)
        b_last = b[:, :, -1:, :]
        S = jnp.exp(b_last[:, :, 0, :])[..., None] * S + jnp.einsum('nhjd,nhje->nhde', kc * jnp.exp(b_last - b), vc)
        return S, o

    s_fin, o = lax.scan(step, s0.astype(jnp.float32), (chunks(q), chunks(k), chunks(v), chunks(log_a)))
    return o.transpose(1, 0, 3, 2, 4).reshape(N, T, H_C, DV_C), s_fin


def gla_mixer(h, s0, w_in, w_gate, b_gate, norm_w, w_out):
    N, T = h.shape[0], h.shape[1]
    q, k, v, r, gl = jnp.split(h @ w_in, np.cumsum(ODD_SIZES)[:-1].tolist(), axis=-1)
    q = q.reshape(N, T, H_C, DK_C) * DK_C ** -0.5
    k = k.reshape(N, T, H_C, DK_C)
    v = v.reshape(N, T, H_C, DV_C)
    log_a = jax.nn.log_sigmoid((gl @ w_gate + b_gate).astype(jnp.float32)) / GATE_TAU
    o, s_new = gla_recurrence(q, k, v, log_a.reshape(N, T, H_C, DK_C), s0)
    o = rms_norm(o, norm_w) * jax.nn.silu(r.reshape(N, T, H_C, DV_C).astype(jnp.float32))
    return o.reshape(N, T, H_C * DV_C).astype(h.dtype) @ w_out, (s_new,)


def setup_inputs(seed: int = 0) -> dict:
    key = jax.random.key(seed)
    keys = iter(jax.random.split(key, 40))

    def nrm(shape, scale=1.0):
        return jax.random.normal(next(keys), shape, jnp.float32) * scale

    n_pages = PAST_LEN // PAGE_SIZE
    n_used = DEC_BATCH * n_pages
    n_phys = n_used + n_used // 4
    win_buf = min(WINDOW, PAST_LEN)
    return {
        "x_prompt": nrm((BATCH, SEQ, D_MODEL)),
        "x_sample": nrm((DEC_BATCH, DEC_SEQ, D_MODEL)),
        "c_prompt": nrm((BATCH, D_MODEL)),
        "c_sample": nrm((DEC_BATCH, D_MODEL)),
        "cache_diff_k": nrm((N_EVEN, n_phys, PAGE_SIZE, H_A, 2, DK_A)),
        "cache_diff_v": nrm((N_EVEN, n_phys, PAGE_SIZE, H_A, DV_A)),
        "cache_cmp_kv": nrm((N_EVEN, n_phys, PAGE_SIZE, 2, G_B, DH_B)),
        "cache_sel_kv": nrm((N_EVEN, n_phys, PAGE_SIZE, 2, G_B, DH_B)),
        "state_win_kv": nrm((N_EVEN, DEC_BATCH, win_buf, 2, G_B, DH_B)),
        "state_gla": nrm((N_ODD, DEC_BATCH, H_C, DK_C, DV_C)),
        "page_table": jax.random.permutation(next(keys), n_phys)[:n_used].reshape(DEC_BATCH, n_pages).astype(jnp.int32),
        "norm1_w": 1.0 + nrm((DEPTH, D_MODEL), 0.05),
        "norm2_w": 1.0 + nrm((DEPTH, D_MODEL), 0.05),
        "ada_w": nrm((DEPTH, D_MODEL, 6 * D_MODEL), D_MODEL ** -0.5),
        "ada_b": nrm((DEPTH, 6 * D_MODEL), 0.02),
        "even_w_in": nrm((N_EVEN, D_MODEL, EVEN_IN), D_MODEL ** -0.5),
        "even_w_out": nrm((N_EVEN, EVEN_MIX, D_MODEL), EVEN_MIX ** -0.5),
        "diff_lambda_w": nrm((N_EVEN, 4, DK_A), 0.1),
        "diff_subln_w": 1.0 + nrm((N_EVEN, DV_A), 0.05),
        "cmp_pe": nrm((N_EVEN, 2, CMP_LEN, DH_B), 0.2),
        "cmp_w1": nrm((N_EVEN, 2, CMP_LEN * DH_B, DH_B), (CMP_LEN * DH_B) ** -0.5),
        "cmp_w2": nrm((N_EVEN, 2, DH_B, DH_B), DH_B ** -0.5),
        "gla_w_in": nrm((N_ODD, D_MODEL, ODD_IN), D_MODEL ** -0.5),
        "gla_w_gate": nrm((N_ODD, GATE_RANK, H_C * DK_C), GATE_RANK ** -0.5),
        "gla_b_gate": nrm((N_ODD, H_C * DK_C), 0.1),
        "gla_norm_w": 1.0 + nrm((N_ODD, DV_C), 0.05),
        "gla_w_out": nrm((N_ODD, H_C * DV_C, D_MODEL), (H_C * DV_C) ** -0.5),
        "mlp_w1": nrm((DEPTH, D_MODEL, D_FF), D_MODEL ** -0.5),
        "mlp_w2": nrm((DEPTH, D_FF, D_MODEL), D_FF ** -0.5),
        "final_norm_w": 1.0 + nrm((D_MODEL,), 0.05),
    }


def reference(x_prompt, x_sample, c_prompt, c_sample, cache_diff_k, cache_diff_v, cache_cmp_kv, cache_sel_kv,
              state_win_kv, state_gla, page_table, norm1_w, norm2_w, ada_w, ada_b, even_w_in, even_w_out,
              diff_lambda_w, diff_subln_w, cmp_pe, cmp_w1, cmp_w2, gla_w_in, gla_w_gate, gla_b_gate, gla_norm_w,
              gla_w_out, mlp_w1, mlp_w2, final_norm_w):
    yp, ys = x_prompt, x_sample
    st_p = [[] for _ in range(6)]
    st_s = [[] for _ in range(6)]
    for l in range(DEPTH):
        mp = adaln(c_prompt, ada_w[l], ada_b[l])
        ms = adaln(c_sample, ada_w[l], ada_b[l])
        if l % 2 == 0:
            e = l // 2
            lam_init = 0.8 - 0.6 * math.exp(-0.3 * l)
            ew = (even_w_in[e], even_w_out[e], diff_lambda_w[e], diff_subln_w[e], cmp_pe[e], cmp_w1[e], cmp_w2[e], lam_init)
            mix_p = lambda h, ew=ew: even_mixer_prompt(h, *ew)
            mix_s = lambda h, ew=ew, e=e: even_mixer_sample(h, cache_diff_k[e], cache_diff_v[e], cache_cmp_kv[e],
                                                            cache_sel_kv[e], state_win_kv[e], page_table, *ew)
            slots = (0, 1, 2, 3, 4)
        else:
            o = l // 2
            gw = (gla_w_in[o], gla_w_gate[o], gla_b_gate[o], gla_norm_w[o], gla_w_out[o])
            mix_p = lambda h, gw=gw: gla_mixer(h, jnp.zeros((h.shape[0], H_C, DK_C, DV_C), jnp.float32), *gw)
            mix_s = lambda h, gw=gw, o=o: gla_mixer(h, state_gla[o], *gw)
            slots = (5,)
        yp, sp = residual_layer(yp, mp, norm1_w[l], norm2_w[l], mix_p, mlp_w1[l], mlp_w2[l])
        ys, ss = residual_layer(ys, ms, norm1_w[l], norm2_w[l], mix_s, mlp_w1[l], mlp_w2[l])
        for i, a, b in zip(slots, sp, ss):
            st_p[i].append(a)
            st_s[i].append(b)
    diff_k_p, diff_v_p, cmp_kv_p, sel_kv_p, win_kv_p, gla_p = [jnp.stack(s, axis=0) for s in st_p]
    diff_k_s, diff_v_s, cmp_kv_s, sel_kv_s, win_kv_s, gla_s = [jnp.stack(s, axis=0) for s in st_s]
    y_prompt = rms_norm(yp, final_norm_w)
    y_sample = rms_norm(ys, final_norm_w)
    return (y_prompt, y_sample, diff_k_p, diff_v_p, cmp_kv_p, sel_kv_p, win_kv_p, gla_p,
            diff_k_s, diff_v_s, cmp_kv_s, sel_kv_s, win_kv_s, gla_s)
```

```python
import functools
import math

import jax
import jax.numpy as jnp
from jax import lax
from jax.experimental import pallas as pl
from jax.experimental.pallas import tpu as pltpu

F32 = jnp.float32
BF16 = jnp.bfloat16

D_MODEL = 1024
PAGE_SIZE = 128
H_A = 4
DK_A = 64
DV_A = 128
H_B = 4
G_B = 2
R_B = 2
DH_B = 128
CMP_LEN = 32
CMP_STRIDE = 16
SEL_BLOCK = 64
TOP_N = 16
WINDOW = 512
H_C = 4
DK_C = 128
DV_C = 256
GATE_RANK = 16
GATE_TAU = 16.0
GLA_CHUNK = 64
GLA_SUB = 16
D_FF = 4 * D_MODEL
ROPE_THETA = 10000.0
EPS = 1e-6

LANES = 128
MASK_NEG = -1e30
SEL_NEG = 32768.0
VMEM_LIMIT_MB = 56


def _cparams(sem, vmem_mb=VMEM_LIMIT_MB):
    return pltpu.CompilerParams(dimension_semantics=sem, vmem_limit_bytes=vmem_mb * 1024 * 1024)


def _dot(a, b):
    return jnp.dot(a, b, preferred_element_type=F32)


def _dot_nt(a, b):
    return lax.dot_general(a, b, (((1,), (1,)), ((), ())), preferred_element_type=F32)


def _dot_tn(a, b):
    return lax.dot_general(a, b, (((0,), (0,)), ((), ())), preferred_element_type=F32)


def _iota(shape, dim):
    return lax.broadcasted_iota(jnp.int32, shape, dim)


def _rms(x, w):
    ms = jnp.mean(x * x, axis=-1, keepdims=True)
    return x * lax.rsqrt(ms + EPS) * w


def _norm_mod(x, nw, shift, scale):
    return _rms(x, nw) * (1.0 + scale) + shift


class _Tiling:
    def __init__(self, nb, r, b, rt):
        assert nb % b == 0 and r % rt == 0 and (b == 1 or rt == r)
        self.nb, self.r, self.b, self.rt = nb, r, b, rt
        self.grid = (nb // b, r // rt)
        self.rows = b * rt
        self.nrb = r // rt

    def x_spec(self, d):
        return pl.BlockSpec((self.b, self.rt, d), lambda i, j: (i, j, 0))

    def mod_spec(self, k):
        return pl.BlockSpec((1, self.b, 1, D_MODEL), lambda i, j: (k, i, 0, 0))

    def flat_spec(self, c):
        nrb = self.nrb
        return pl.BlockSpec((self.rows, c), lambda i, j: (i * nrb + j, 0))

    def tab_spec(self):
        return pl.BlockSpec((self.rows, LANES), lambda i, j: (j, 0))


def _full_spec(shape):
    nd = len(shape)
    return pl.BlockSpec(shape, lambda *_: (0,) * nd)


def _adaln_kernel(c_ref, w_ref, b_ref, o_ref):
    c = c_ref[...]
    a = (c * jax.nn.sigmoid(c)).astype(BF16)
    o_ref[0] = _dot(a, w_ref[0].astype(BF16)) + b_ref[0]


def _adaln(c_all, ada_w, ada_b):
    depth, d, n6 = ada_w.shape
    rows = c_all.shape[0]
    tn = 1536
    return pl.pallas_call(
        _adaln_kernel,
        grid=(depth, n6 // tn),
        in_specs=[pl.BlockSpec((rows, d), lambda l, j: (0, 0)),
                  pl.BlockSpec((1, d, tn), lambda l, j: (l, 0, j)),
                  pl.BlockSpec((1, 1, tn), lambda l, j: (l, 0, j))],
        out_specs=pl.BlockSpec((1, rows, tn), lambda l, j: (l, 0, j)),
        out_shape=jax.ShapeDtypeStruct((depth, rows, n6), F32),
        compiler_params=_cparams(("parallel", "parallel")),
        name="adaln",
    )(c_all, ada_w, ada_b.reshape(depth, 1, n6))


def _swap_half(x, half):
    if 2 * half == LANES:
        return pltpu.roll(x, half, 1)
    lane = _iota((1, LANES), 1)
    lo = (lane % (2 * half)) < half
    return jnp.where(lo, pltpu.roll(x, LANES - half, 1), pltpu.roll(x, half, 1))


def _even_inproj_kernel(x_ref, sh_ref, sc_ref, nw_ref, w_ref, wg_ref, c64_ref, s64_ref, c128_ref, s128_ref,
                        qa_ref, ka_ref, va_ref, qb_ref, cmp_ref, sel_ref, win_ref, gt_ref):
    h = _norm_mod(x_ref[...], nw_ref[...], sh_ref[0], sc_ref[0])
    h = h.reshape(-1, D_MODEL).astype(BF16)
    c64, s64, c128, s128 = c64_ref[...], s64_ref[...], c128_ref[...], s128_ref[...]

    def proj(off):
        return _dot(h, w_ref[:, off:off + LANES])

    def rope64(p):
        return p * c64 + _swap_half(p, DK_A // 2) * s64

    def rope128(p):
        return p * c128 + _swap_half(p, DH_B // 2) * s128

    for j in range(4):
        sl = slice(j * LANES, (j + 1) * LANES)
        qa_ref[:, sl] = rope64(proj(j * LANES))
        ka_ref[:, sl] = rope64(proj(512 + j * LANES))
        va_ref[:, sl] = proj(1024 + j * LANES)
        qb_ref[:, sl] = rope128(proj(1536 + j * LANES))
    for t, ref in enumerate((cmp_ref, sel_ref, win_ref)):
        base = 2048 + t * 512
        for j in range(2):
            sl = slice(j * LANES, (j + 1) * LANES)
            ref[:, sl] = rope128(proj(base + j * LANES))
            sl = slice(256 + j * LANES, 256 + (j + 1) * LANES)
            ref[:, sl] = proj(base + 256 + j * LANES)
    gt_ref[...] = jax.nn.sigmoid(_dot(h, wg_ref[...]))


def _even_inproj(til, x, mods, nw, w_main, w_gate, tabs):
    rows_total = til.nb * til.r
    widths = (512, 512, 512, 512, 512, 512, 512, LANES)
    return pl.pallas_call(
        _even_inproj_kernel,
        grid=til.grid,
        in_specs=[til.x_spec(D_MODEL), til.mod_spec(0), til.mod_spec(1), _full_spec((1, D_MODEL)),
                  _full_spec(w_main.shape), _full_spec(w_gate.shape)] + [til.tab_spec()] * 4,
        out_specs=[til.flat_spec(c) for c in widths],
        out_shape=[jax.ShapeDtypeStruct((rows_total, c), F32) for c in widths],
        compiler_params=_cparams(("parallel", "parallel")),
        name="even_inproj",
    )(x, mods, mods, nw, w_main, w_gate, *tabs)


def _softmax_update(s, v, m_ref, l_ref, acc_ref):
    m_old = m_ref[...]
    m_new = jnp.maximum(m_old, jnp.max(s, axis=-1, keepdims=True))
    alpha = jnp.exp(m_old - m_new)
    p = jnp.exp(s - m_new)
    l_ref[...] = alpha * l_ref[...] + jnp.sum(p, axis=-1, keepdims=True)
    acc_ref[...] = alpha * acc_ref[...] + _dot(p.astype(BF16), v)
    m_ref[...] = m_new


def _init_softmax(m_ref, l_ref, acc_ref):
    m_ref[...] = jnp.full(m_ref.shape, MASK_NEG, F32)
    l_ref[...] = jnp.zeros(l_ref.shape, F32)
    acc_ref[...] = jnp.zeros(acc_ref.shape, F32)


def _diff_lambda(lw, lam_init):
    a = jnp.sum(lw[0:1] * lw[1:2], axis=-1, keepdims=True)
    b = jnp.sum(lw[2:3] * lw[3:4], axis=-1, keepdims=True)
    return jnp.exp(a) - jnp.exp(b) + lam_init


def _diff_finalize(o2, tq, lam, sub_w, lam_init):
    od = o2[:tq] - lam * o2[tq:]
    return _rms(od, sub_w) * (1.0 - lam_init)


def _diff_flash_kernel(q_ref, k_ref, v_ref, lamw_ref, sub_ref, o_ref, q2_ref, m_ref, l_ref, acc_ref, *,
                       tq, lam_init):
    i = pl.program_id(2)
    q = q_ref[0] * (DK_A ** -0.5)
    lane = _iota((1, LANES), 1)
    q2_ref[0:tq] = jnp.where(lane < DK_A, q, 0.0).astype(BF16)
    q2_ref[tq:2 * tq] = jnp.where(lane >= DK_A, q, 0.0).astype(BF16)
    _init_softmax(m_ref, l_ref, acc_ref)

    def tile(kb, masked):
        start = pl.multiple_of(kb * tq, tq)
        k = k_ref[0, pl.ds(start, tq), :].astype(BF16)
        v = v_ref[0, pl.ds(start, tq), :].astype(BF16)
        s = _dot_nt(q2_ref[...], k)
        if masked:
            r = _iota((2 * tq, tq), 0)
            r = jnp.where(r >= tq, r - tq, r)
            s = jnp.where(_iota((2 * tq, tq), 1) <= r, s, MASK_NEG)
        _softmax_update(s, v, m_ref, l_ref, acc_ref)

    def body(kb, carry):
        tile(kb, False)
        return carry

    lax.fori_loop(0, i, body, 0)
    tile(i, True)
    o2 = acc_ref[...] / l_ref[...]
    o_ref[0] = _diff_finalize(o2, tq, _diff_lambda(lamw_ref[...], lam_init), sub_ref[...], lam_init)


def _diff_attn_prompt(qa, ka, va, lam_w, sub_w, lam_init, tq):
    n, t, _ = qa.shape
    kern = functools.partial(_diff_flash_kernel, tq=tq, lam_init=lam_init)
    return pl.pallas_call(
        kern,
        grid=(n, H_A, t // tq),
        in_specs=[pl.BlockSpec((1, tq, LANES), lambda b, h, i: (b, i, h)),
                  pl.BlockSpec((1, t, LANES), lambda b, h, i: (b, 0, h)),
                  pl.BlockSpec((1, t, LANES), lambda b, h, i: (b, 0, h)),
                  _full_spec(lam_w.shape), _full_spec(sub_w.shape)],
        out_specs=pl.BlockSpec((1, tq, LANES), lambda b, h, i: (b, i, h)),
        out_shape=jax.ShapeDtypeStruct((n, t, H_A * DV_A), F32),
        scratch_shapes=[pltpu.VMEM((2 * tq, LANES), BF16), pltpu.VMEM((2 * tq, 1), F32),
                        pltpu.VMEM((2 * tq, 1), F32), pltpu.VMEM((2 * tq, DV_A), F32)],
        compiler_params=_cparams(("parallel", "parallel", "parallel")),
        name="diff_attn_prompt",
    )(qa, ka, va, lam_w, sub_w)


def _sel_onehot(rows, first_block):
    blk = _iota((rows, LANES), 0) // SEL_BLOCK + first_block
    return jnp.where(blk == _iota((rows, LANES), 1), 1.0, 0.0).astype(BF16)


def _nsa_flash_kernel(*refs, tq, use_sel):
    if use_sel:
        q_ref, selm_ref, k_ref, v_ref, o_ref, q2_ref, m_ref, l_ref, acc_ref = refs
    else:
        q_ref, k_ref, v_ref, o_ref, q2_ref, m_ref, l_ref, acc_ref = refs
    i = pl.program_id(2)
    q = q_ref[0] * (DH_B ** -0.5)
    for r in range(R_B):
        q2_ref[r * tq:(r + 1) * tq, 0:LANES] = q[:, r * LANES:(r + 1) * LANES].astype(BF16)
        if use_sel:
            q2_ref[r * tq:(r + 1) * tq, LANES:2 * LANES] = selm_ref[0, 0]
    _init_softmax(m_ref, l_ref, acc_ref)

    def tile(kb, masked):
        start = pl.multiple_of(kb * tq, tq)
        k = k_ref[0, pl.ds(start, tq), :].astype(BF16)
        v = v_ref[0, pl.ds(start, tq), :].astype(BF16)
        if use_sel:
            k = jnp.concatenate([k, _sel_onehot(tq, kb * (tq // SEL_BLOCK))], axis=-1)
        s = _dot_nt(q2_ref[...], k)
        if masked:
            r = _iota((R_B * tq, tq), 0)
            qp = i * tq + jnp.where(r >= tq, r - tq, r)
            kp = kb * tq + _iota((R_B * tq, tq), 1)
            ok = kp <= qp
            if not use_sel:
                ok = ok & (kp > qp - WINDOW)
            s = jnp.where(ok, s, MASK_NEG)
        _softmax_update(s, v, m_ref, l_ref, acc_ref)

    if use_sel:
        def body(kb, carry):
            tile(kb, False)
            return carry
        lax.fori_loop(0, i, body, 0)
        tile(i, True)
    else:
        def body(kb, carry):
            tile(kb, True)
            return carry
        lax.fori_loop(jnp.maximum(i - WINDOW // tq, 0), i + 1, body, 0)
    o = acc_ref[...] / l_ref[...]
    for r in range(R_B):
        o_ref[0, :, r * LANES:(r + 1) * LANES] = o[r * tq:(r + 1) * tq]


def _nsa_attn_prompt(qb, kv, selm, tq):
    n, t, _ = qb.shape
    use_sel = selm is not None
    kd = 2 * LANES if use_sel else LANES
    kern = functools.partial(_nsa_flash_kernel, tq=tq, use_sel=use_sel)
    in_specs = [pl.BlockSpec((1, tq, R_B * LANES), lambda b, g, i: (b, i, g))]
    args = [qb]
    if use_sel:
        in_specs.append(pl.BlockSpec((1, 1, tq, LANES), lambda b, g, i: (b, g, i, 0)))
        args.append(selm)
    in_specs += [pl.BlockSpec((1, t, LANES), lambda b, g, i: (b, 0, g)),
                 pl.BlockSpec((1, t, LANES), lambda b, g, i: (b, 0, G_B + g))]
    args += [kv, kv]
    return pl.pallas_call(
        kern,
        grid=(n, G_B, t // tq),
        in_specs=in_specs,
        out_specs=pl.BlockSpec((1, tq, R_B * LANES), lambda b, g, i: (b, i, g)),
        out_shape=jax.ShapeDtypeStruct((n, t, H_B * DH_B), F32),
        scratch_shapes=[pltpu.VMEM((R_B * tq, kd), BF16), pltpu.VMEM((R_B * tq, 1), F32),
                        pltpu.VMEM((R_B * tq, 1), F32), pltpu.VMEM((R_B * tq, DH_B), F32)],
        compiler_params=_cparams(("parallel", "parallel", "parallel")),
        name="nsa_sel_prompt" if use_sel else "nsa_win_prompt",
    )(*args)


def _compress_kernel(*refs, n_src, chunks_per_src):
    x_refs = refs[:G_B * n_src]
    wc_ref, pe_ref, w2_ref, o_ref = refs[G_B * n_src:]
    nchunk = n_src * chunks_per_src
    rows = G_B * nchunk
    acc = jnp.zeros((rows, 2 * LANES), F32)
    pew = jnp.zeros((16, 2 * LANES), F32)
    for u in range(CMP_STRIDE // 2):
        halves = []
        for tok in (2 * u, 2 * u + 1):
            pieces = [xr[0, pl.ds(tok, chunks_per_src, stride=CMP_STRIDE), :] for xr in x_refs]
            halves.append(jnp.concatenate(pieces, axis=0))
        lhs = jnp.concatenate(halves, axis=-1).astype(BF16)
        w = wc_ref[0, u]
        acc = acc + _dot(lhs, w)
        pew = pew + _dot(pe_ref[0, u], w)
    first = acc[:, :LANES]
    second = pltpu.roll(acc[:, LANES:], rows - 1, 0)
    hid = first + second + pew[0:1, :LANES] + pew[8:9, LANES:]
    hid = hid * jax.nn.sigmoid(hid)
    out = _dot(hid.astype(BF16), w2_ref[0])
    for g in range(G_B):
        o_ref[0, 0, g] = out[g * nchunk:(g + 1) * nchunk]


def _compress_weights(cmp_pe, cmp_w1, cmp_w2):
    w1 = cmp_w1.reshape(2, CMP_LEN, DH_B, DH_B)
    wab = jnp.concatenate([w1[:, :CMP_STRIDE], w1[:, CMP_STRIDE:]], axis=-1)
    wc = wab.reshape(2, CMP_STRIDE // 2, 2 * DH_B, 2 * DH_B).astype(BF16)
    pa = cmp_pe[:, :CMP_STRIDE].reshape(2, CMP_STRIDE // 2, 1, 2 * DH_B)
    pb = cmp_pe[:, CMP_STRIDE:].reshape(2, CMP_STRIDE // 2, 1, 2 * DH_B)
    z = jnp.zeros((2, CMP_STRIDE // 2, 7, 2 * DH_B), F32)
    pe = jnp.concatenate([pa, z, pb, z], axis=2).astype(BF16)
    return wc, pe, cmp_w2.astype(BF16)


def _compress_prompt(cmp_kv, wc, pe, w2):
    n, t, _ = cmp_kv.shape
    nchunk = t // CMP_STRIDE
    kern = functools.partial(_compress_kernel, n_src=1, chunks_per_src=nchunk)
    return pl.pallas_call(
        kern,
        grid=(n, 2),
        in_specs=[pl.BlockSpec((1, t, DH_B), functools.partial(lambda b, kv, g: (b, 0, kv * G_B + g), g=g))
                  for g in range(G_B)] + [
                  pl.BlockSpec((1,) + wc.shape[1:], lambda b, kv: (kv, 0, 0, 0)),
                  pl.BlockSpec((1,) + pe.shape[1:], lambda b, kv: (kv, 0, 0, 0)),
                  pl.BlockSpec((1, DH_B, DH_B), lambda b, kv: (kv, 0, 0))],
        out_specs=pl.BlockSpec((1, 1, G_B, nchunk, DH_B), lambda b, kv: (b, kv, 0, 0, 0)),
        out_shape=jax.ShapeDtypeStruct((n, 2, G_B, nchunk, DH_B), F32),
        compiler_params=_cparams(("parallel", "parallel")),
        name="compress_prompt",
    )(*([cmp_kv] * G_B), wc, pe, w2)


def _compress_sample(cache, page_table, wc, pe, w2):
    n, n_pages = page_table.shape
    cps = PAGE_SIZE // CMP_STRIDE
    nchunk = n_pages * cps
    kern = functools.partial(_compress_kernel, n_src=n_pages, chunks_per_src=cps)

    def body(pt_ref, *refs):
        kern(*refs)

    page_specs = [pl.BlockSpec((1, PAGE_SIZE, DH_B),
                               functools.partial(lambda b, kv, pt, j, g: (pt[b, j], 0, kv * G_B + g), j=j, g=g))
                  for g in range(G_B) for j in range(n_pages)]
    grid_spec = pltpu.PrefetchScalarGridSpec(
        num_scalar_prefetch=1,
        grid=(n, 2),
        in_specs=page_specs + [
            pl.BlockSpec((1,) + wc.shape[1:], lambda b, kv, pt: (kv, 0, 0, 0)),
            pl.BlockSpec((1,) + pe.shape[1:], lambda b, kv, pt: (kv, 0, 0, 0)),
            pl.BlockSpec((1, DH_B, DH_B), lambda b, kv, pt: (kv, 0, 0))],
        out_specs=pl.BlockSpec((1, 1, G_B, nchunk, DH_B), lambda b, kv, pt: (b, kv, 0, 0, 0)),
    )
    return pl.pallas_call(
        body,
        grid_spec=grid_spec,
        out_shape=jax.ShapeDtypeStruct((n, 2, G_B, nchunk, DH_B), F32),
        compiler_params=_cparams(("parallel", "parallel")),
        name="compress_sample",
    )(page_table, *([cache] * (G_B * n_pages)), wc, pe, w2)


def _cmp_topk_kernel(q_ref, kc_ref, vc_ref, ocmp_ref, selm_ref, p_scr, v_scr, *,
                     nb, rq, nck, n_cmp, n_sel, nsp, pos_base):
    tq = nb * rq
    j = pl.program_id(2)
    scale = DH_B ** -0.5
    kidx = _iota((1, nck), 1)
    qpos_c = pos_base + j * rq + _iota((rq, 1), 0)
    valid = (kidx < n_cmp) & (CMP_STRIDE * kidx + (CMP_LEN - 1) <= qpos_c)
    for b in range(nb):
        kcc = kc_ref[b, 0, 0].astype(BF16)
        vcc = vc_ref[b, 0, 0].astype(BF16)
        q = q_ref[b] * scale
        psum = jnp.zeros((rq, nck), F32)
        for r in range(R_B):
            s = _dot_nt(q[:, r * LANES:(r + 1) * LANES].astype(BF16), kcc)
            s = jnp.where(valid, s, MASK_NEG)
            m = jnp.max(s, axis=-1, keepdims=True)
            p = jnp.where(valid, jnp.exp(s - m), 0.0)
            p = p / jnp.maximum(jnp.sum(p, axis=-1, keepdims=True), 1e-30)
            ocmp_ref[b, :, r * LANES:(r + 1) * LANES] = _dot(p.astype(BF16), vcc)
            psum = psum + p
        p_scr[b * rq:(b + 1) * rq, :] = psum
    psum = p_scr[...]
    p_hi = psum.astype(BF16)
    p_lo = (psum - p_hi.astype(F32)).astype(BF16)
    srow = _iota((LANES, nck), 0)
    kcol = _iota((LANES, nck), 1)
    cov = (CMP_STRIDE * kcol < SEL_BLOCK * srow + SEL_BLOCK) & (CMP_STRIDE * kcol + CMP_LEN > SEL_BLOCK * srow)
    cov = cov & (srow < n_sel) & (kcol < n_cmp)
    cov_t = jnp.where(cov, 1.0, 0.0).astype(BF16)
    imp_t = _dot_nt(cov_t, p_hi) + _dot_nt(cov_t, p_lo)
    qpos_l = pos_base + j * rq + _iota((1, tq), 1) % rq
    sb = _iota((nsp, 1), 0)
    valid_b = (sb < n_sel) & (sb * SEL_BLOCK <= qpos_l)
    forced = (sb == 0) | (sb == qpos_l // SEL_BLOCK)
    vals = jnp.where(forced, jnp.inf, jnp.where(valid_b, imp_t[:nsp], -jnp.inf))
    v_scr[...] = vals

    def body(jj, cnt):
        vj = v_scr[pl.ds(jj, 1), :]
        tie = jnp.where(sb > jj, 1.0, 0.0)
        return cnt + jnp.where(vj > vals, 1.0, jnp.where(vj == vals, tie, 0.0))

    cnt = lax.fori_loop(0, nsp, body, jnp.zeros((nsp, tq), F32))
    keep = valid_b & (cnt < float(min(TOP_N, n_sel)))
    selm_t = jnp.where(keep, 0.0, -SEL_NEG)
    if nsp < LANES:
        selm_t = jnp.concatenate([selm_t, jnp.zeros((LANES - nsp, tq), F32)], axis=0)
    selm = selm_t.T.astype(BF16)
    for b in range(nb):
        selm_ref[b, 0] = selm[b * rq:(b + 1) * rq]


def _cmp_topk(qb, ccmp, *, nb, rq, n_cmp, n_sel, pos_base):
    n, r, _ = qb.shape
    nck = ccmp.shape[3]
    nsp = -(-n_sel // 8) * 8
    tq = nb * rq
    kern = functools.partial(_cmp_topk_kernel, nb=nb, rq=rq, nck=nck, n_cmp=n_cmp, n_sel=n_sel, nsp=nsp,
                             pos_base=pos_base)
    return pl.pallas_call(
        kern,
        grid=(n // nb, G_B, r // rq),
        in_specs=[pl.BlockSpec((nb, rq, R_B * LANES), lambda a, g, j: (a, j, g)),
                  pl.BlockSpec((nb, 1, 1, nck, DH_B), lambda a, g, j: (a, 0, g, 0, 0)),
                  pl.BlockSpec((nb, 1, 1, nck, DH_B), lambda a, g, j: (a, 1, g, 0, 0))],
        out_specs=[pl.BlockSpec((nb, rq, R_B * LANES), lambda a, g, j: (a, j, g)),
                   pl.BlockSpec((nb, 1, rq, LANES), lambda a, g, j: (a, g, j, 0))],
        out_shape=[jax.ShapeDtypeStruct((n, r, H_B * DH_B), F32),
                   jax.ShapeDtypeStruct((n, G_B, r, LANES), BF16)],
        scratch_shapes=[pltpu.VMEM((tq, nck), F32), pltpu.VMEM((nsp, tq), F32)],
        compiler_params=_cparams(("parallel", "parallel", "parallel")),
        name="cmp_topk",
    )(qb, ccmp, ccmp)


def _attend_pieces(q2, pieces):
    scores = []
    m = None
    for k, _, mask in pieces:
        s = _dot_nt(q2, k)
        if mask is not None:
            s = jnp.where(mask, s, MASK_NEG)
        scores.append(s)
        ms = jnp.max(s, axis=-1, keepdims=True)
        m = ms if m is None else jnp.maximum(m, ms)
    l = jnp.zeros_like(m)
    acc = jnp.zeros((q2.shape[0], LANES), F32)
    for s, (_, v, _) in zip(scores, pieces):
        p = jnp.exp(s - m)
        l = l + jnp.sum(p, axis=-1, keepdims=True)
        acc = acc + _dot(p.astype(BF16), v)
    return acc / l


def _pad_rows(x, rows):
    return jnp.concatenate([x, jnp.zeros((rows - x.shape[0], x.shape[1]), x.dtype)], axis=0)


def _new_token_mask(m_rows, s):
    r = _iota((m_rows, LANES), 0) % s
    return _iota((m_rows, LANES), 1) <= r


def _diff_sample_kernel(pt_ref, *refs, n_pages, s, lam_init):
    kp_refs = refs[:n_pages]
    vp_refs = refs[n_pages:2 * n_pages]
    q_ref, kn_ref, vn_ref, lamw_ref, sub_ref, o_ref = refs[2 * n_pages:]
    lane = _iota((1, LANES), 1)
    lam = _diff_lambda(lamw_ref[...], lam_init)
    new_mask = _new_token_mask(2 * s, s)
    for h in range(H_A):
        cols = slice(h * LANES, (h + 1) * LANES)
        q = q_ref[0, :, cols] * (DK_A ** -0.5)
        q2 = jnp.concatenate([jnp.where(lane < DK_A, q, 0.0), jnp.where(lane >= DK_A, q, 0.0)], axis=0)
        kpast = jnp.concatenate([r[0, :, cols].astype(BF16) for r in kp_refs], axis=0)
        vpast = jnp.concatenate([r[0, :, cols].astype(BF16) for r in vp_refs], axis=0)
        knew = _pad_rows(kn_ref[0, :, cols], LANES).astype(BF16)
        vnew = _pad_rows(vn_ref[0, :, cols], LANES).astype(BF16)
        o2 = _attend_pieces(q2.astype(BF16), [(kpast, vpast, None), (knew, vnew, new_mask)])
        o_ref[0, :, cols] = _diff_finalize(o2, s, lam, sub_ref[...], lam_init)


def _page_specs(n_pages, width, col):
    return [pl.BlockSpec((1, PAGE_SIZE, width), functools.partial(lambda b, pt, j: (pt[b, j], 0, col), j=j))
            for j in range(n_pages)]


def _diff_attn_sample(qa, ka, va, cache_k, cache_v, page_table, lam_w, sub_w, lam_init):
    n, s, _ = qa.shape
    n_pages = page_table.shape[1]
    kern = functools.partial(_diff_sample_kernel, n_pages=n_pages, s=s, lam_init=lam_init)
    row_spec = pl.BlockSpec((1, s, 512), lambda b, pt: (b, 0, 0))
    grid_spec = pltpu.PrefetchScalarGridSpec(
        num_scalar_prefetch=1,
        grid=(n,),
        in_specs=_page_specs(n_pages, 512, 0) + _page_specs(n_pages, 512, 0)
        + [row_spec, row_spec, row_spec,
           pl.BlockSpec(lam_w.shape, lambda b, pt: (0, 0)), pl.BlockSpec(sub_w.shape, lambda b, pt: (0, 0))],
        out_specs=row_spec,
    )
    return pl.pallas_call(
        kern,
        grid_spec=grid_spec,
        out_shape=jax.ShapeDtypeStruct((n, s, 512), F32),
        compiler_params=_cparams(("parallel",)),
        name="diff_attn_sample",
    )(page_table, *([cache_k] * n_pages), *([cache_v] * n_pages), qa, ka, va, lam_w, sub_w)


def _sel_sample_kernel(pt_ref, *refs, n_pages, s):
    pg_refs = refs[:n_pages]
    q_ref, selm_ref, new_ref, o_ref = refs[n_pages:]
    past = n_pages * PAGE_SIZE
    new_mask = _new_token_mask(R_B * s, s)
    oh_past = _sel_onehot(past, 0)
    new_blk = past // SEL_BLOCK
    oh_new = jnp.where((_iota((LANES, LANES), 1) == new_blk) & (_iota((LANES, LANES), 0) < s), 1.0, 0.0)
    oh_new = oh_new.astype(BF16)
    scale = DH_B ** -0.5
    for g in range(G_B):
        kcols = slice(g * LANES, (g + 1) * LANES)
        vcols = slice((G_B + g) * LANES, (G_B + g + 1) * LANES)
        selm = selm_ref[0, g]
        q2 = jnp.concatenate(
            [jnp.concatenate([(q_ref[0, :, (g * R_B + r) * LANES:(g * R_B + r + 1) * LANES] * scale).astype(BF16),
                              selm], axis=-1) for r in range(R_B)], axis=0)
        kpast = jnp.concatenate([r[0, :, kcols].astype(BF16) for r in pg_refs], axis=0)
        vpast = jnp.concatenate([r[0, :, vcols].astype(BF16) for r in pg_refs], axis=0)
        kpast = jnp.concatenate([kpast, oh_past], axis=-1)
        knew = jnp.concatenate([_pad_rows(new_ref[0, :, kcols], LANES).astype(BF16), oh_new], axis=-1)
        vnew = _pad_rows(new_ref[0, :, vcols], LANES).astype(BF16)
        o2 = _attend_pieces(q2, [(kpast, vpast, None), (knew, vnew, new_mask)])
        for r in range(R_B):
            o_ref[0, :, (g * R_B + r) * LANES:(g * R_B + r + 1) * LANES] = o2[r * s:(r + 1) * s]


def _sel_attn_sample(qb, selm, sel_new, cache, page_table):
    n, s, _ = qb.shape
    n_pages = page_table.shape[1]
    assert s <= SEL_BLOCK and (n_pages * PAGE_SIZE) % SEL_BLOCK == 0
    kern = functools.partial(_sel_sample_kernel, n_pages=n_pages, s=s)
    row_spec = pl.BlockSpec((1, s, 512), lambda b, pt: (b, 0, 0))
    grid_spec = pltpu.PrefetchScalarGridSpec(
        num_scalar_prefetch=1,
        grid=(n,),
        in_specs=_page_specs(n_pages, 512, 0)
        + [row_spec, pl.BlockSpec((1, G_B, s, LANES), lambda b, pt: (b, 0, 0, 0)), row_spec],
        out_specs=row_spec,
    )
    return pl.pallas_call(
        kern,
        grid_spec=grid_spec,
        out_shape=jax.ShapeDtypeStruct((n, s, 512), F32),
        compiler_params=_cparams(("parallel",)),
        name="nsa_sel_sample",
    )(page_table, *([cache] * n_pages), qb, selm, sel_new)


def _win_sample_kernel(q_ref, buf_ref, new_ref, o_ref, wout_ref, *, s, wb):
    new_mask = _new_token_mask(R_B * s, s)
    r = _iota((R_B * s, wb), 0) % s
    buf_mask = _iota((R_B * s, wb), 1) > r + (wb - WINDOW)
    scale = DH_B ** -0.5
    for g in range(G_B):
        kcols = slice(g * LANES, (g + 1) * LANES)
        vcols = slice((G_B + g) * LANES, (G_B + g + 1) * LANES)
        q2 = jnp.concatenate(
            [(q_ref[0, :, (g * R_B + r_) * LANES:(g * R_B + r_ + 1) * LANES] * scale).astype(BF16)
             for r_ in range(R_B)], axis=0)
        kbuf = buf_ref[0, :, kcols].astype(BF16)
        vbuf = buf_ref[0, :, vcols].astype(BF16)
        knew = _pad_rows(new_ref[0, :, kcols], LANES).astype(BF16)
        vnew = _pad_rows(new_ref[0, :, vcols], LANES).astype(BF16)
        o2 = _attend_pieces(q2, [(kbuf, vbuf, buf_mask), (knew, vnew, new_mask)])
        for r_ in range(R_B):
            o_ref[0, :, (g * R_B + r_) * LANES:(g * R_B + r_ + 1) * LANES] = o2[r_ * s:(r_ + 1) * s]
    wout_ref[0, 0:wb - s] = buf_ref[0, s:wb]
    wout_ref[0, wb - s:wb] = new_ref[0]


def _win_attn_sample(qb, win_buf, win_new, past):
    n, s, _ = qb.shape
    wb = win_buf.shape[1]
    assert past >= wb and wb % 8 == 0 and s % 8 == 0
    kern = functools.partial(_win_sample_kernel, s=s, wb=wb)
    row_spec = pl.BlockSpec((1, s, 512), lambda b: (b, 0, 0))
    buf_spec = pl.BlockSpec((1, wb, 512), lambda b: (b, 0, 0))
    return pl.pallas_call(
        kern,
        grid=(n,),
        in_specs=[row_spec, buf_spec, row_spec],
        out_specs=[row_spec, buf_spec],
        out_shape=[jax.ShapeDtypeStruct((n, s, 512), F32), jax.ShapeDtypeStruct((n, wb, 512), F32)],
        compiler_params=_cparams(("parallel",)),
        name="nsa_win_sample",
    )(qb, win_buf, win_new)


def _even_out_kernel(y_ref, g1_ref, oa_ref, oc_ref, os_ref, ow_ref, gt_ref, w_ref, o_ref):
    gt = gt_ref[...]
    parts = [oa_ref[...].astype(BF16)]
    for hb in range(H_B):
        sl = slice(hb * LANES, (hb + 1) * LANES)
        ob = (gt[:, 3 * hb:3 * hb + 1] * oc_ref[:, sl] + gt[:, 3 * hb + 1:3 * hb + 2] * os_ref[:, sl]
              + gt[:, 3 * hb + 2:3 * hb + 3] * ow_ref[:, sl])
        parts.append(ob.astype(BF16))
    out = _dot(jnp.concatenate(parts, axis=-1), w_ref[...])
    y = y_ref[...]
    o_ref[...] = y + g1_ref[0] * out.reshape(y.shape)


def _even_out(til, y, mods, oa, oc, os_, ow, gt, w_out):
    return pl.pallas_call(
        _even_out_kernel,
        grid=til.grid,
        in_specs=[til.x_spec(D_MODEL), til.mod_spec(2), til.flat_spec(512), til.flat_spec(512),
                  til.flat_spec(512), til.flat_spec(512), til.flat_spec(LANES), _full_spec(w_out.shape)],
        out_specs=til.x_spec(D_MODEL),
        out_shape=jax.ShapeDtypeStruct(y.shape, F32),
        compiler_params=_cparams(("parallel", "parallel")),
        name="even_out",
    )(y, mods, oa, oc, os_, ow, gt, w_out)


def _mlp_kernel(y_ref, sh_ref, sc_ref, g_ref, nw_ref, w1_ref, w2_ref, fw_ref, o_ref, *, final):
    y = y_ref[...]
    h = _norm_mod(y, nw_ref[...], sh_ref[0], sc_ref[0]).reshape(-1, D_MODEL).astype(BF16)
    a = jnp.maximum(_dot(h, w1_ref[...]), 0.0)
    out = _dot((a * a).astype(BF16), w2_ref[...])
    y2 = y + g_ref[0] * out.reshape(y.shape)
    if final:
        y2 = _rms(y2, fw_ref[...])
    o_ref[...] = y2


def _mlp(til, y, mods, nw, w1, w2, fw, final):
    kern = functools.partial(_mlp_kernel, final=final)
    return pl.pallas_call(
        kern,
        grid=til.grid,
        in_specs=[til.x_spec(D_MODEL), til.mod_spec(3), til.mod_spec(4), til.mod_spec(5),
                  _full_spec((1, D_MODEL)), _full_spec(w1.shape), _full_spec(w2.shape), _full_spec((1, D_MODEL))],
        out_specs=til.x_spec(D_MODEL),
        out_shape=jax.ShapeDtypeStruct(y.shape, F32),
        compiler_params=_cparams(("parallel", "parallel")),
        name="mlp",
    )(y, mods, mods, mods, nw, w1, w2, fw)


def _gla_inproj_kernel(x_ref, sh_ref, sc_ref, nw_ref, w_ref, wgl_ref, wgate_ref, bgate_ref,
                       q_ref, k_ref, v_ref, r_ref, la_ref):
    h = _norm_mod(x_ref[...], nw_ref[...], sh_ref[0], sc_ref[0]).reshape(-1, D_MODEL).astype(BF16)
    nk = H_C * DK_C
    nv = H_C * DV_C
    q_ref[...] = _dot(h, w_ref[:, 0:nk]) * (DK_C ** -0.5)
    k_ref[...] = _dot(h, w_ref[:, nk:2 * nk])
    v_ref[...] = _dot(h, w_ref[:, 2 * nk:2 * nk + nv])
    r_ref[...] = _dot(h, w_ref[:, 2 * nk + nv:2 * nk + 2 * nv])
    gl = _dot(h, wgl_ref[...])
    x = _dot(gl.astype(BF16), wgate_ref[...]) + bgate_ref[...]
    log_sig = jnp.minimum(x, 0.0) - jnp.log1p(jnp.exp(-jnp.abs(x)))
    la_ref[...] = log_sig / GATE_TAU


def _gla_inproj(til, x, mods, nw, w_main, w_gl, w_gate, b_gate):
    rows_total = til.nb * til.r
    widths = (H_C * DK_C, H_C * DK_C, H_C * DV_C, H_C * DV_C, H_C * DK_C)
    return pl.pallas_call(
        _gla_inproj_kernel,
        grid=til.grid,
        in_specs=[til.x_spec(D_MODEL), til.mod_spec(0), til.mod_spec(1), _full_spec((1, D_MODEL)),
                  _full_spec(w_main.shape), _full_spec(w_gl.shape), _full_spec(w_gate.shape),
                  _full_spec(b_gate.shape)],
        out_specs=[til.flat_spec(c) for c in widths],
        out_shape=[jax.ShapeDtypeStruct((rows_total, c), F32) for c in widths],
        compiler_params=_cparams(("parallel", "parallel")),
        name="gla_inproj",
    )(x, mods, mods, nw, w_main, w_gl, w_gate, b_gate)


def _gla_chunk(q, k, v, g, s_ref, c, sub):
    tri = jnp.where(_iota((c, c), 0) >= _iota((c, c), 1), 1.0, 0.0)
    b = jnp.dot(tri, g, preferred_element_type=F32, precision=lax.Precision.HIGHEST)
    state = s_ref[...]
    o = _dot((q * jnp.exp(b)).astype(BF16), state.astype(BF16))
    lane_c = _iota((sub, c), 1)
    row_s = _iota((sub, c), 0)
    att_rows = []
    for blk in range(c // sub):
        lo = blk * sub
        qi, ki, bi = q[lo:lo + sub], k[lo:lo + sub], b[lo:lo + sub]
        att = jnp.zeros((sub, c), F32)
        if blk > 0:
            bs = b[lo - 1:lo]
            qe = qi * jnp.exp(bi - bs)
            ke = k * jnp.exp(jnp.minimum(bs - b, 0.0))
            att = jnp.where(lane_c < lo, _dot_nt(qe.astype(BF16), ke.astype(BF16)), 0.0)
        for jj in range(sub):
            e = jnp.exp(jnp.minimum(bi - bi[jj:jj + 1], 0.0))
            col = jnp.sum(qi * ki[jj:jj + 1] * e, axis=-1, keepdims=True)
            att = jnp.where((lane_c == lo + jj) & (row_s >= jj), col, att)
        att_rows.append(att)
    att = att_rows[0] if len(att_rows) == 1 else jnp.concatenate(att_rows, axis=0)
    o = o + _dot(att, v)
    bl = b[c - 1:c]
    kd = k * jnp.exp(bl - b)
    eye = _iota((DK_C, DK_C), 0) == _iota((DK_C, DK_C), 1)
    decay = jnp.sum(jnp.where(eye, jnp.exp(bl), 0.0), axis=-1, keepdims=True)
    s_ref[...] = decay * state + _dot_tn(kd.astype(BF16), v.astype(BF16))
    return o


def _gla_rec_kernel(*refs, tt, c, sub, has_s0):
    if has_s0:
        q_ref, k_ref, v_ref, g_ref, s0_ref, o_ref, sfin_ref, s_ref = refs
    else:
        q_ref, k_ref, v_ref, g_ref, o_ref, sfin_ref, s_ref = refs
    t = pl.program_id(2)

    @pl.when(t == 0)
    def _():
        s_ref[...] = s0_ref[0, 0] if has_s0 else jnp.zeros(s_ref.shape, F32)

    if tt < c:
        pad = lambda x: _pad_rows(x, c)
        o = _gla_chunk(pad(q_ref[0]), pad(k_ref[0]), pad(v_ref[0]), pad(g_ref[0]), s_ref, c, sub)
        o_ref[0] = o[:tt]
    else:
        def body(ci, carry):
            rows = pl.ds(pl.multiple_of(ci * c, c), c)
            o_ref[0, rows, :] = _gla_chunk(q_ref[0, rows, :], k_ref[0, rows, :], v_ref[0, rows, :],
                                           g_ref[0, rows, :], s_ref, c, sub)
            return carry
        lax.fori_loop(0, tt // c, body, 0)

    @pl.when(t == pl.num_programs(2) - 1)
    def _():
        sfin_ref[0, 0] = s_ref[...]


def _gla_recurrence(q, k, v, g, s0, tt, c, sub):
    n, t, _ = q.shape
    kern = functools.partial(_gla_rec_kernel, tt=tt, c=c, sub=sub, has_s0=s0 is not None)
    kspec = pl.BlockSpec((1, tt, DK_C), lambda b, h, i: (b, i, h))
    vspec = pl.BlockSpec((1, tt, DV_C), lambda b, h, i: (b, i, h))
    sspec = pl.BlockSpec((1, 1, DK_C, DV_C), lambda b, h, i: (b, h, 0, 0))
    in_specs = [kspec, kspec, vspec, kspec]
    args = [q, k, v, g]
    if s0 is not None:
        in_specs.append(sspec)
        args.append(s0)
    return pl.pallas_call(
        kern,
        grid=(n, H_C, t // tt),
        in_specs=in_specs,
        out_specs=[vspec, sspec],
        out_shape=[jax.ShapeDtypeStruct((n, t, H_C * DV_C), F32),
                   jax.ShapeDtypeStruct((n, H_C, DK_C, DV_C), F32)],
        scratch_shapes=[pltpu.VMEM((DK_C, DV_C), F32)],
        compiler_params=_cparams(("parallel", "parallel", "arbitrary")),
        name="gla_recurrence",
    )(*args)


def _gla_out_kernel(y_ref, g1_ref, o_ref_in, r_ref, nw_ref, w_ref, out_ref):
    parts = []
    for h in range(H_C):
        sl = slice(h * DV_C, (h + 1) * DV_C)
        r = r_ref[:, sl]
        parts.append((_rms(o_ref_in[:, sl], nw_ref[...]) * (r * jax.nn.sigmoid(r))).astype(BF16))
    out = _dot(jnp.concatenate(parts, axis=-1), w_ref[...])
    y = y_ref[...]
    out_ref[...] = y + g1_ref[0] * out.reshape(y.shape)


def _gla_out(til, y, mods, o, r, nw, w_out):
    return pl.pallas_call(
        _gla_out_kernel,
        grid=til.grid,
        in_specs=[til.x_spec(D_MODEL), til.mod_spec(2), til.flat_spec(H_C * DV_C), til.flat_spec(H_C * DV_C),
                  _full_spec(nw.shape), _full_spec(w_out.shape)],
        out_specs=til.x_spec(D_MODEL),
        out_shape=jax.ShapeDtypeStruct(y.shape, F32),
        compiler_params=_cparams(("parallel", "parallel")),
        name="gla_out",
    )(y, mods, o, r, nw, w_out)


def _rope_tables(pos, d):
    inv = ROPE_THETA ** (-jnp.arange(0, d, 2, dtype=F32) / d)
    ang = pos.astype(F32)[:, None] * inv[None, :]
    cos, sin = jnp.cos(ang), jnp.sin(ang)
    rep = LANES // d
    c = jnp.tile(jnp.concatenate([cos, cos], axis=-1), (1, rep))
    s = jnp.tile(jnp.concatenate([-sin, sin], axis=-1), (1, rep))
    return c, s


def _mods_for(mod_l, lo, hi):
    nb = hi - lo
    return mod_l[lo:hi].reshape(nb, 6, D_MODEL).transpose(1, 0, 2).reshape(6, nb, 1, D_MODEL)


def _prompt_tile_rows(t):
    return math.gcd(t, 256)


def _even_layer(yp, ys, mods_p, mods_s, caches, page_table, wts, lam_init, til_p, til_s):
    (c_dk, c_dv, c_cmp, c_sel, win_buf) = caches
    n, t, _ = yp.shape
    ns, s, _ = ys.shape
    n_pages = page_table.shape[1]
    past = n_pages * PAGE_SIZE
    w_in = wts["w_in"]
    n_main = w_in.shape[1] - 3 * H_B
    w_main = w_in[:, :n_main].astype(BF16)
    w_gate = jnp.pad(w_in[:, n_main:], ((0, 0), (0, LANES - 3 * H_B))).astype(BF16)
    nw1 = wts["norm1"].reshape(1, D_MODEL)
    lam_w = wts["lam_w"]
    sub_w = wts["sub_w"].reshape(1, DV_A)
    wc, pe, w2c = _compress_weights(wts["cmp_pe"], wts["cmp_w1"], wts["cmp_w2"])
    w_out = wts["w_out"].astype(BF16)

    tabs_p = _rope_tables(jnp.arange(t), DK_A) + _rope_tables(jnp.arange(t), DH_B)
    qa, ka, va, qb, cmp_kv, sel_kv, win_kv, gt = _even_inproj(til_p, yp, mods_p, nw1, w_main, w_gate, tabs_p)
    r3 = lambda a: a.reshape(n, t, a.shape[-1])
    tq = _prompt_tile_rows(t)
    oa = _diff_attn_prompt(r3(qa), r3(ka), r3(va), lam_w, sub_w, lam_init, tq)
    ccmp = _compress_prompt(r3(cmp_kv), wc, pe, w2c)
    n_cmp = (t - CMP_LEN) // CMP_STRIDE + 1
    n_sel = -(-t // SEL_BLOCK)
    o_cmp, selm = _cmp_topk(r3(qb), ccmp, nb=1, rq=tq, n_cmp=n_cmp, n_sel=n_sel, pos_base=0)
    o_sel = _nsa_attn_prompt(r3(qb), r3(sel_kv), selm, tq)
    o_win = _nsa_attn_prompt(r3(qb), r3(win_kv), None, tq)
    f2 = lambda a: a.reshape(n * t, a.shape[-1])
    yp = _even_out(til_p, yp, mods_p, f2(oa), f2(o_cmp), f2(o_sel), f2(o_win), gt, w_out)
    wl = min(WINDOW, t)
    st_p = (r3(ka).reshape(n, t, H_A, 2, DK_A), r3(va).reshape(n, t, H_A, DV_A),
            r3(cmp_kv).reshape(n, t, 2, G_B, DH_B), r3(sel_kv).reshape(n, t, 2, G_B, DH_B),
            r3(win_kv)[:, t - wl:].reshape(n, wl, 2, G_B, DH_B))

    pos_s = past + jnp.arange(s)
    tabs_s = tuple(jnp.tile(x, (til_s.b, 1)) for x in _rope_tables(pos_s, DK_A) + _rope_tables(pos_s, DH_B))
    qa, ka, va, qb, cmp_kv, sel_kv, win_kv, gt = _even_inproj(til_s, ys, mods_s, nw1, w_main, w_gate, tabs_s)
    r3 = lambda a: a.reshape(ns, s, a.shape[-1])
    flat_cache = lambda c: c.reshape(c.shape[0], PAGE_SIZE, 512)
    oa = _diff_attn_sample(r3(qa), r3(ka), r3(va), flat_cache(c_dk), flat_cache(c_dv), page_table,
                           lam_w, sub_w, lam_init)
    ccmp = _compress_sample(flat_cache(c_cmp), page_table, wc, pe, w2c)
    total = past + s
    n_cmp = (total - CMP_LEN) // CMP_STRIDE + 1
    n_sel = -(-total // SEL_BLOCK)
    assert n_cmp <= ccmp.shape[3] and n_sel <= LANES
    nb = math.gcd(ns, LANES // s)
    o_cmp, selm = _cmp_topk(r3(qb), ccmp, nb=nb, rq=s, n_cmp=n_cmp, n_sel=n_sel, pos_base=past)
    o_sel = _sel_attn_sample(r3(qb), selm, r3(sel_kv), flat_cache(c_sel), page_table)
    wb = win_buf.shape[1]
    o_win, win_out = _win_attn_sample(r3(qb), win_buf.reshape(ns, wb, 512), r3(win_kv), past)
    f2 = lambda a: a.reshape(ns * s, a.shape[-1])
    ys = _even_out(til_s, ys, mods_s, f2(oa), f2(o_cmp), f2(o_sel), f2(o_win), gt, w_out)
    st_s = (r3(ka).reshape(ns, s, H_A, 2, DK_A), r3(va).reshape(ns, s, H_A, DV_A),
            r3(cmp_kv).reshape(ns, s, 2, G_B, DH_B), r3(sel_kv).reshape(ns, s, 2, G_B, DH_B),
            win_out.reshape(ns, wb, 2, G_B, DH_B))
    return yp, ys, st_p, st_s


def _odd_layer(yp, ys, mods_p, mods_s, s0, wts, til_p, til_s):
    w_in = wts["w_in"]
    n_main = w_in.shape[1] - GATE_RANK
    w_main = w_in[:, :n_main].astype(BF16)
    w_gl = jnp.pad(w_in[:, n_main:], ((0, 0), (0, LANES - GATE_RANK))).astype(BF16)
    w_gate = jnp.pad(wts["w_gate"], ((0, LANES - GATE_RANK), (0, 0))).astype(BF16)
    b_gate = wts["b_gate"].reshape(1, -1)
    nw1 = wts["norm1"].reshape(1, D_MODEL)
    gnw = wts["gnorm"].reshape(1, DV_C)
    w_out = wts["w_out"].astype(BF16)
    outs = []
    for y, mods, til, state in ((yp, mods_p, til_p, None), (ys, mods_s, til_s, s0)):
        n, t, _ = y.shape
        q, k, v, r, la = _gla_inproj(til, y, mods, nw1, w_main, w_gl, w_gate, b_gate)
        r3 = lambda a: a.reshape(n, t, a.shape[-1])
        c = math.gcd(t, GLA_CHUNK)
        if c >= GLA_SUB:
            tt, cc = math.gcd(t, 8 * c), c
        else:
            tt, cc = t, GLA_SUB
        o, s_fin = _gla_recurrence(r3(q), r3(k), r3(v), r3(la), state, tt, cc, GLA_SUB)
        y = _gla_out(til, y, mods, o.reshape(n * t, -1), r, gnw, w_out)
        outs.append((y, s_fin))
    return outs[0][0], outs[1][0], outs[0][1], outs[1][1]


def kernel(x_prompt, x_sample, c_prompt, c_sample, cache_diff_k, cache_diff_v, cache_cmp_kv, cache_sel_kv,
           state_win_kv, state_gla, page_table, norm1_w, norm2_w, ada_w, ada_b, even_w_in, even_w_out,
           diff_lambda_w, diff_subln_w, cmp_pe, cmp_w1, cmp_w2, gla_w_in, gla_w_gate, gla_b_gate, gla_norm_w,
           gla_w_out, mlp_w1, mlp_w2, final_norm_w):
    depth = ada_w.shape[0]
    n, t, _ = x_prompt.shape
    ns, s, _ = x_sample.shape
    til_p = _Tiling(n, t, 1, _prompt_tile_rows(t))
    til_s = _Tiling(ns, s, math.gcd(ns, 256 // s), s)

    pad = (-(n + ns)) % 8
    c_all = jnp.concatenate([c_prompt, c_sample, jnp.zeros((pad, D_MODEL), F32)], axis=0)
    mod = _adaln(c_all, ada_w, ada_b)

    yp, ys = x_prompt, x_sample
    st_p = [[] for _ in range(6)]
    st_s = [[] for _ in range(6)]
    fw = final_norm_w.reshape(1, D_MODEL)
    for l in range(depth):
        mods_p = _mods_for(mod[l], 0, n)
        mods_s = _mods_for(mod[l], n, n + ns)
        if l % 2 == 0:
            e = l // 2
            lam_init = 0.8 - 0.6 * math.exp(-0.3 * l)
            wts = dict(w_in=even_w_in[e], w_out=even_w_out[e], lam_w=diff_lambda_w[e], sub_w=diff_subln_w[e],
                       cmp_pe=cmp_pe[e], cmp_w1=cmp_w1[e], cmp_w2=cmp_w2[e], norm1=norm1_w[l])
            caches = (cache_diff_k[e], cache_diff_v[e], cache_cmp_kv[e], cache_sel_kv[e], state_win_kv[e])
            yp, ys, sp, ss = _even_layer(yp, ys, mods_p, mods_s, caches, page_table, wts, lam_init, til_p, til_s)
            for i in range(5):
                st_p[i].append(sp[i])
                st_s[i].append(ss[i])
        else:
            o = l // 2
            wts = dict(w_in=gla_w_in[o], w_gate=gla_w_gate[o], b_gate=gla_b_gate[o], gnorm=gla_norm_w[o],
                       w_out=gla_w_out[o], norm1=norm1_w[l])
            yp, ys, gp, gs = _odd_layer(yp, ys, mods_p, mods_s, state_gla[o], wts, til_p, til_s)
            st_p[5].append(gp)
            st_s[5].append(gs)
        final = l == depth - 1
        w1 = mlp_w1[l].astype(BF16)
        w2 = mlp_w2[l].astype(BF16)
        nw2 = norm2_w[l].reshape(1, D_MODEL)
        yp = _mlp(til_p, yp, mods_p, nw2, w1, w2, fw, final)
        ys = _mlp(til_s, ys, mods_s, nw2, w1, w2, fw, final)
    outs_p = [jnp.stack(x, axis=0) for x in st_p]
    outs_s = [jnp.stack(x, axis=0) for x in st_s]
    return (yp, ys, *outs_p, *outs_s)
```

```python
import functools
import math

import jax
import jax.numpy as jnp
from jax import lax
from jax.experimental import pallas as pl
from jax.experimental.pallas import tpu as pltpu

F32 = jnp.float32
BF16 = jnp.bfloat16

D_MODEL = 1024
PAGE_SIZE = 128
H_A = 4
DK_A = 64
DV_A = 128
H_B = 4
G_B = 2
R_B = 2
DH_B = 128
CMP_LEN = 32
CMP_STRIDE = 16
SEL_BLOCK = 64
TOP_N = 16
WINDOW = 512
H_C = 4
DK_C = 128
DV_C = 256
GATE_RANK = 16
GATE_TAU = 16.0
GLA_CHUNK = 64
GLA_SUB = 16
D_FF = 4 * D_MODEL
ROPE_THETA = 10000.0
EPS = 1e-6

LANES = 128
SLOTS = 4
MASK_NEG = -1e30
LOG2E = 1.4426950408889634
SEL_NEG = 32768.0
VMEM_LIMIT_MB = 56


def _cparams(sem, vmem_mb=VMEM_LIMIT_MB):
    return pltpu.CompilerParams(dimension_semantics=sem, vmem_limit_bytes=vmem_mb * 1024 * 1024)


def _dot(a, b):
    return jnp.dot(a, b, preferred_element_type=F32)


def _dot_nt(a, b):
    return lax.dot_general(a, b, (((1,), (1,)), ((), ())), preferred_element_type=F32)


def _dot_tn(a, b):
    return lax.dot_general(a, b, (((0,), (0,)), ((), ())), preferred_element_type=F32)


def _iota(shape, dim):
    return lax.broadcasted_iota(jnp.int32, shape, dim)


def _rms(x, w):
    ms = jnp.mean(x * x, axis=-1, keepdims=True)
    return x * lax.rsqrt(ms + EPS) * w


def _norm_mod(x, nw, shift, scale):
    return _rms(x, nw) * (1.0 + scale) + shift


class _Tiling:
    def __init__(self, nb, r, b, rt):
        assert nb % b == 0 and r % rt == 0 and (b == 1 or rt == r)
        self.nb, self.r, self.b, self.rt = nb, r, b, rt
        self.grid = (nb // b, r // rt)
        self.rows = b * rt
        self.nrb = r // rt

    def x_spec(self, d):
        return pl.BlockSpec((self.b, self.rt, d), lambda i, j: (i, j, 0))

    def mod_spec(self, k):
        return pl.BlockSpec((1, self.b, 1, D_MODEL), lambda i, j: (k, i, 0, 0))

    def flat_spec(self, c):
        nrb = self.nrb
        return pl.BlockSpec((self.rows, c), lambda i, j: (i * nrb + j, 0))

    def tab_spec(self):
        return pl.BlockSpec((self.rows, LANES), lambda i, j: (j, 0))


def _full_spec(shape):
    nd = len(shape)
    return pl.BlockSpec(shape, lambda *_: (0,) * nd)


def _adaln_kernel(c_ref, w_ref, b_ref, o_ref):
    c = c_ref[...]
    a = (c * jax.nn.sigmoid(c)).astype(BF16)
    o_ref[0] = _dot(a, w_ref[0].astype(BF16)) + b_ref[0]


def _adaln(c_all, ada_w, ada_b):
    depth, d, n6 = ada_w.shape
    rows = c_all.shape[0]
    tn = 1536
    return pl.pallas_call(
        _adaln_kernel,
        grid=(depth, n6 // tn),
        in_specs=[pl.BlockSpec((rows, d), lambda l, j: (0, 0)),
                  pl.BlockSpec((1, d, tn), lambda l, j: (l, 0, j)),
                  pl.BlockSpec((1, 1, tn), lambda l, j: (l, 0, j))],
        out_specs=pl.BlockSpec((1, rows, tn), lambda l, j: (l, 0, j)),
        out_shape=jax.ShapeDtypeStruct((depth, rows, n6), F32),
        compiler_params=_cparams(("parallel", "parallel")),
        name="adaln",
    )(c_all, ada_w, ada_b.reshape(depth, 1, n6))


def _swap_half(x, half):
    if 2 * half == LANES:
        return pltpu.roll(x, half, 1)
    lane = _iota((1, LANES), 1)
    lo = (lane % (2 * half)) < half
    return jnp.where(lo, pltpu.roll(x, LANES - half, 1), pltpu.roll(x, half, 1))


def _even_inproj_kernel(x_ref, sh_ref, sc_ref, nw_ref, w_ref, wg_ref, c64_ref, s64_ref, c128_ref, s128_ref,
                        qa_ref, ka_ref, va_ref, qb_ref, cmp_ref, sel_ref, win_ref, gt_ref):
    h = _norm_mod(x_ref[...], nw_ref[...], sh_ref[0], sc_ref[0])
    h = h.reshape(-1, D_MODEL).astype(BF16)
    c64, s64, c128, s128 = c64_ref[...], s64_ref[...], c128_ref[...], s128_ref[...]

    def proj(off):
        return _dot(h, w_ref[:, off:off + LANES])

    def rope64(p):
        return p * c64 + _swap_half(p, DK_A // 2) * s64

    def rope128(p):
        return p * c128 + _swap_half(p, DH_B // 2) * s128

    for j in range(4):
        sl = slice(j * LANES, (j + 1) * LANES)
        qa_ref[:, sl] = rope64(proj(j * LANES))
        ka_ref[:, sl] = rope64(proj(512 + j * LANES))
        va_ref[:, sl] = proj(1024 + j * LANES)
        qb_ref[:, sl] = rope128(proj(1536 + j * LANES))
    for t, ref in enumerate((cmp_ref, sel_ref, win_ref)):
        base = 2048 + t * 512
        for j in range(2):
            sl = slice(j * LANES, (j + 1) * LANES)
            ref[:, sl] = rope128(proj(base + j * LANES))
            sl = slice(256 + j * LANES, 256 + (j + 1) * LANES)
            ref[:, sl] = proj(base + 256 + j * LANES)
    gt_ref[...] = jax.nn.sigmoid(_dot(h, wg_ref[...]))


def _even_inproj(til, x, mods, nw, w_main, w_gate, tabs):
    rows_total = til.nb * til.r
    widths = (512, 512, 512, 512, 512, 512, 512, LANES)
    return pl.pallas_call(
        _even_inproj_kernel,
        grid=til.grid,
        in_specs=[til.x_spec(D_MODEL), til.mod_spec(0), til.mod_spec(1), _full_spec((1, D_MODEL)),
                  _full_spec(w_main.shape), _full_spec(w_gate.shape)] + [til.tab_spec()] * 4,
        out_specs=[til.flat_spec(c) for c in widths],
        out_shape=[jax.ShapeDtypeStruct((rows_total, c), F32) for c in widths],
        compiler_params=_cparams(("parallel", "parallel")),
        name="even_inproj",
    )(x, mods, mods, nw, w_main, w_gate, *tabs)


def _for_tiles(lo, hi, fn):
    n = hi - lo

    def pair(j, carry):
        fn(lo + 2 * j)
        fn(lo + 2 * j + 1)
        return carry

    lax.fori_loop(0, n // 2, pair, 0)

    @pl.when(n % 2 == 1)
    def _():
        fn(hi - 1)


def _two_pass_attention(mx_ref, acc_ref, lo, hi, last, scores, values, mask_body=False):
    def lane_max(s):
        m = s[:, 0:LANES]
        for c in range(1, s.shape[1] // LANES):
            m = jnp.maximum(m, s[:, c * LANES:(c + 1) * LANES])
        return m

    mx_ref[...] = jnp.full(mx_ref.shape, MASK_NEG, F32)

    def pass1(kb, masked):
        mx_ref[...] = jnp.maximum(mx_ref[...], lane_max(scores(kb, masked)))

    _for_tiles(lo, hi, lambda kb: pass1(kb, mask_body))
    pass1(last, True)
    mx_ref[...] = jnp.broadcast_to(jnp.max(mx_ref[...], axis=-1, keepdims=True), mx_ref.shape)
    acc_ref[...] = jnp.zeros(acc_ref.shape, F32)

    def pass2(kb, masked):
        s = scores(kb, masked)
        m = mx_ref[...]
        p = jnp.concatenate([jnp.exp2(s[:, c * LANES:(c + 1) * LANES] - m) for c in range(s.shape[1] // LANES)],
                            axis=-1).astype(BF16)
        v = values(kb)
        v1 = jnp.concatenate([v, jnp.ones(v.shape, BF16)], axis=-1)
        acc_ref[...] += _dot(p, v1)

    _for_tiles(lo, hi, lambda kb: pass2(kb, mask_body))
    pass2(last, True)
    return acc_ref[:, 0:LANES] / acc_ref[:, LANES:2 * LANES]


def _diff_lambda(lw, lam_init):
    a = jnp.sum(lw[0:1] * lw[1:2], axis=-1, keepdims=True)
    b = jnp.sum(lw[2:3] * lw[3:4], axis=-1, keepdims=True)
    return jnp.exp(a) - jnp.exp(b) + lam_init


def _diff_finalize(o2, tq, lam, sub_w, lam_init):
    od = o2[:tq] - lam * o2[tq:]
    return _rms(od, sub_w) * (1.0 - lam_init)


def _diff_flash_kernel(q_ref, k_ref, v_ref, lamw_ref, sub_ref, o_ref, q2_ref, mx_ref, acc_ref, *,
                       tq, lam_init):
    i = pl.program_id(2)
    q = q_ref[0] * (DK_A ** -0.5 * LOG2E)
    lane = _iota((1, LANES), 1)
    q2_ref[0:tq] = jnp.where(lane < DK_A, q, 0.0).astype(BF16)
    q2_ref[tq:2 * tq] = jnp.where(lane >= DK_A, q, 0.0).astype(BF16)

    def scores(kb, masked):
        k = k_ref[0, pl.ds(pl.multiple_of(kb * tq, tq), tq), :].astype(BF16)
        s = _dot_nt(q2_ref[...], k)
        if masked:
            r = _iota((2 * tq, tq), 0)
            r = jnp.where(r >= tq, r - tq, r)
            s = jnp.where(_iota((2 * tq, tq), 1) <= r, s, MASK_NEG)
        return s

    def values(kb):
        return v_ref[0, pl.ds(pl.multiple_of(kb * tq, tq), tq), :].astype(BF16)

    o2 = _two_pass_attention(mx_ref, acc_ref, 0, i, i, scores, values)
    o_ref[0] = _diff_finalize(o2, tq, _diff_lambda(lamw_ref[...], lam_init), sub_ref[...], lam_init)


def _diff_attn_prompt(qa, ka, va, lam_w, sub_w, lam_init, tq):
    n, t, _ = qa.shape
    kern = functools.partial(_diff_flash_kernel, tq=tq, lam_init=lam_init)
    return pl.pallas_call(
        kern,
        grid=(n, H_A, t // tq),
        in_specs=[pl.BlockSpec((1, tq, LANES), lambda b, h, i: (b, i, h)),
                  pl.BlockSpec((1, t, LANES), lambda b, h, i: (b, 0, h)),
                  pl.BlockSpec((1, t, LANES), lambda b, h, i: (b, 0, h)),
                  _full_spec(lam_w.shape), _full_spec(sub_w.shape)],
        out_specs=pl.BlockSpec((1, tq, LANES), lambda b, h, i: (b, i, h)),
        out_shape=jax.ShapeDtypeStruct((n, t, H_A * DV_A), F32),
        scratch_shapes=[pltpu.VMEM((2 * tq, LANES), BF16), pltpu.VMEM((2 * tq, LANES), F32),
                        pltpu.VMEM((2 * tq, 2 * LANES), F32)],
        compiler_params=_cparams(("parallel", "parallel", "parallel")),
        name="diff_attn_prompt",
    )(qa, ka, va, lam_w, sub_w)


def _sel_onehot(rows, first_block):
    blk = _iota((rows, LANES), 0) // SEL_BLOCK + first_block
    return jnp.where(blk == _iota((rows, LANES), 1), 1.0, 0.0).astype(BF16)


def _nsa_flash_kernel(*refs, tq, use_sel):
    if use_sel:
        q_ref, selm_ref, k_ref, v_ref, o_ref, q2_ref, mx_ref, acc_ref = refs
    else:
        q_ref, k_ref, v_ref, o_ref, q2_ref, mx_ref, acc_ref = refs
    i = pl.program_id(2)
    q = q_ref[0] * (DH_B ** -0.5 * LOG2E)
    for r in range(R_B):
        q2_ref[r * tq:(r + 1) * tq, 0:LANES] = q[:, r * LANES:(r + 1) * LANES].astype(BF16)
        if use_sel:
            q2_ref[r * tq:(r + 1) * tq, LANES:2 * LANES] = selm_ref[0, 0]

    def scores(kb, masked):
        k = k_ref[0, pl.ds(pl.multiple_of(kb * tq, tq), tq), :].astype(BF16)
        if use_sel:
            k = jnp.concatenate([k, _sel_onehot(tq, kb * (tq // SEL_BLOCK))], axis=-1)
        s = _dot_nt(q2_ref[...], k)
        if masked:
            r = _iota((R_B * tq, tq), 0)
            qp = i * tq + jnp.where(r >= tq, r - tq, r)
            kp = kb * tq + _iota((R_B * tq, tq), 1)
            ok = kp <= qp
            if not use_sel:
                ok = ok & (kp > qp - WINDOW)
            s = jnp.where(ok, s, MASK_NEG)
        return s

    def values(kb):
        return v_ref[0, pl.ds(pl.multiple_of(kb * tq, tq), tq), :].astype(BF16)

    lo = 0 if use_sel else jnp.maximum(i - WINDOW // tq, 0)
    o = _two_pass_attention(mx_ref, acc_ref, lo, i, i, scores, values, mask_body=not use_sel)
    for r in range(R_B):
        o_ref[0, :, r * LANES:(r + 1) * LANES] = o[r * tq:(r + 1) * tq]


def _nsa_attn_prompt(qb, kv, selm, tq):
    n, t, _ = qb.shape
    use_sel = selm is not None
    kd = 2 * LANES if use_sel else LANES
    kern = functools.partial(_nsa_flash_kernel, tq=tq, use_sel=use_sel)
    in_specs = [pl.BlockSpec((1, tq, R_B * LANES), lambda b, g, i: (b, i, g))]
    args = [qb]
    if use_sel:
        in_specs.append(pl.BlockSpec((1, 1, tq, LANES), lambda b, g, i: (b, g, i, 0)))
        args.append(selm)
    in_specs += [pl.BlockSpec((1, t, LANES), lambda b, g, i: (b, 0, g)),
                 pl.BlockSpec((1, t, LANES), lambda b, g, i: (b, 0, G_B + g))]
    args += [kv, kv]
    return pl.pallas_call(
        kern,
        grid=(n, G_B, t // tq),
        in_specs=in_specs,
        out_specs=pl.BlockSpec((1, tq, R_B * LANES), lambda b, g, i: (b, i, g)),
        out_shape=jax.ShapeDtypeStruct((n, t, H_B * DH_B), F32),
        scratch_shapes=[pltpu.VMEM((R_B * tq, kd), BF16), pltpu.VMEM((R_B * tq, LANES), F32),
                        pltpu.VMEM((R_B * tq, 2 * LANES), F32)],
        compiler_params=_cparams(("parallel", "parallel", "parallel")),
        name="nsa_sel_prompt" if use_sel else "nsa_win_prompt",
    )(*args)


def _compress_core(load, wc_ref, pe_ref, w2_ref, wi, nchunk):
    rows = G_B * nchunk
    acc = jnp.zeros((rows, 2 * LANES), F32)
    pew = jnp.zeros((16, 2 * LANES), F32)
    for u in range(CMP_STRIDE // 2):
        lhs = jnp.concatenate([load(2 * u), load(2 * u + 1)], axis=-1).astype(BF16)
        w = wc_ref[wi, u]
        acc = acc + _dot(lhs, w)
        pew = pew + _dot(pe_ref[wi, u], w)
    first = acc[:, :LANES]
    second = pltpu.roll(acc[:, LANES:], rows - 1, 0)
    hid = first + second + pew[0:1, :LANES] + pew[8:9, LANES:]
    hid = hid * jax.nn.sigmoid(hid)
    return _dot(hid.astype(BF16), w2_ref[wi])


def _compress_prompt_kernel(*refs, nchunk):
    x_refs = refs[:G_B]
    wc_ref, pe_ref, w2_ref, o_ref = refs[G_B:]

    def load(tok):
        return jnp.concatenate([xr[0, pl.ds(tok, nchunk, stride=CMP_STRIDE), :] for xr in x_refs], axis=0)

    out = _compress_core(load, wc_ref, pe_ref, w2_ref, 0, nchunk)
    for g in range(G_B):
        o_ref[0, 0, g] = out[g * nchunk:(g + 1) * nchunk]


def _compress_sample_kernel(pt_ref, *refs, n_pages):
    pg_refs = refs[:n_pages]
    wc_ref, pe_ref, w2_ref, o_ref = refs[n_pages:]
    cps = PAGE_SIZE // CMP_STRIDE
    nchunk = n_pages * cps
    for kv in range(2):
        def load(tok):
            return jnp.concatenate(
                [pg[0, pl.ds(tok * 2 * G_B + kv * G_B + g, cps, stride=CMP_STRIDE * 2 * G_B), :]
                 for g in range(G_B) for pg in pg_refs], axis=0)

        out = _compress_core(load, wc_ref, pe_ref, w2_ref, kv, nchunk)
        for g in range(G_B):
            o_ref[0, kv, g] = out[g * nchunk:(g + 1) * nchunk]


def _compress_weights(cmp_pe, cmp_w1, cmp_w2):
    w1 = cmp_w1.reshape(2, CMP_LEN, DH_B, DH_B)
    wab = jnp.concatenate([w1[:, :CMP_STRIDE], w1[:, CMP_STRIDE:]], axis=-1)
    wc = wab.reshape(2, CMP_STRIDE // 2, 2 * DH_B, 2 * DH_B).astype(BF16)
    pa = cmp_pe[:, :CMP_STRIDE].reshape(2, CMP_STRIDE // 2, 1, 2 * DH_B)
    pb = cmp_pe[:, CMP_STRIDE:].reshape(2, CMP_STRIDE // 2, 1, 2 * DH_B)
    z = jnp.zeros((2, CMP_STRIDE // 2, 7, 2 * DH_B), F32)
    pe = jnp.concatenate([pa, z, pb, z], axis=2).astype(BF16)
    return wc, pe, cmp_w2.astype(BF16)


def _compress_prompt(cmp_kv, wc, pe, w2):
    n, t, _ = cmp_kv.shape
    nchunk = t // CMP_STRIDE
    kern = functools.partial(_compress_prompt_kernel, nchunk=nchunk)
    return pl.pallas_call(
        kern,
        grid=(n, 2),
        in_specs=[pl.BlockSpec((1, t, DH_B), functools.partial(lambda b, kv, g: (b, 0, kv * G_B + g), g=g))
                  for g in range(G_B)] + [
                  pl.BlockSpec((1,) + wc.shape[1:], lambda b, kv: (kv, 0, 0, 0)),
                  pl.BlockSpec((1,) + pe.shape[1:], lambda b, kv: (kv, 0, 0, 0)),
                  pl.BlockSpec((1, DH_B, DH_B), lambda b, kv: (kv, 0, 0))],
        out_specs=pl.BlockSpec((1, 1, G_B, nchunk, DH_B), lambda b, kv: (b, kv, 0, 0, 0)),
        out_shape=jax.ShapeDtypeStruct((n, 2, G_B, nchunk, DH_B), F32),
        compiler_params=_cparams(("parallel", "parallel")),
        name="compress_prompt",
    )(*([cmp_kv] * G_B), wc, pe, w2)


def _compress_sample(cache, page_table, wc, pe, w2):
    n, n_pages = page_table.shape
    nchunk = n_pages * (PAGE_SIZE // CMP_STRIDE)
    kern = functools.partial(_compress_sample_kernel, n_pages=n_pages)
    grid_spec = pltpu.PrefetchScalarGridSpec(
        num_scalar_prefetch=1,
        grid=(n,),
        in_specs=_page_specs(n_pages, PAGE_SIZE * SLOTS, LANES) + [
            pl.BlockSpec(wc.shape, lambda b, pt: (0, 0, 0, 0)),
            pl.BlockSpec(pe.shape, lambda b, pt: (0, 0, 0, 0)),
            pl.BlockSpec(w2.shape, lambda b, pt: (0, 0, 0))],
        out_specs=pl.BlockSpec((1, 2, G_B, nchunk, DH_B), lambda b, pt: (b, 0, 0, 0, 0)),
    )
    return pl.pallas_call(
        kern,
        grid_spec=grid_spec,
        out_shape=jax.ShapeDtypeStruct((n, 2, G_B, nchunk, DH_B), F32),
        compiler_params=_cparams(("parallel",)),
        name="compress_sample",
    )(page_table, *([cache] * n_pages), wc, pe, w2)


def _cmp_topk_kernel(q_ref, kc_ref, vc_ref, ocmp_ref, selm_ref, p_scr, v_scr, *,
                     nb, rq, nck, n_cmp, n_sel, nsp, pos_base):
    tq = nb * rq
    j = pl.program_id(2)
    scale = DH_B ** -0.5
    kidx = _iota((1, nck), 1)
    qpos_c = pos_base + j * rq + _iota((rq, 1), 0)
    valid = (kidx < n_cmp) & (CMP_STRIDE * kidx + (CMP_LEN - 1) <= qpos_c)
    for b in range(nb):
        kcc = kc_ref[b, 0, 0].astype(BF16)
        vcc = vc_ref[b, 0, 0].astype(BF16)
        q = q_ref[b] * scale
        psum = jnp.zeros((rq, nck), F32)
        for r in range(R_B):
            s = _dot_nt(q[:, r * LANES:(r + 1) * LANES].astype(BF16), kcc)
            s = jnp.where(valid, s, MASK_NEG)
            m = jnp.max(s, axis=-1, keepdims=True)
            p = jnp.where(valid, jnp.exp(s - m), 0.0)
            p = p / jnp.maximum(jnp.sum(p, axis=-1, keepdims=True), 1e-30)
            ocmp_ref[b, :, r * LANES:(r + 1) * LANES] = _dot(p.astype(BF16), vcc)
            psum = psum + p
        p_scr[b * rq:(b + 1) * rq, :] = psum
    psum = p_scr[...]
    p_hi = psum.astype(BF16)
    p_lo = (psum - p_hi.astype(F32)).astype(BF16)
    srow = _iota((LANES, nck), 0)
    kcol = _iota((LANES, nck), 1)
    cov = (CMP_STRIDE * kcol < SEL_BLOCK * srow + SEL_BLOCK) & (CMP_STRIDE * kcol + CMP_LEN > SEL_BLOCK * srow)
    cov = cov & (srow < n_sel) & (kcol < n_cmp)
    cov_t = jnp.where(cov, 1.0, 0.0).astype(BF16)
    imp_t = _dot_nt(cov_t, p_hi) + _dot_nt(cov_t, p_lo)
    qpos_l = pos_base + j * rq + _iota((1, tq), 1) % rq
    sb = _iota((nsp, 1), 0)
    valid_b = (sb < n_sel) & (sb * SEL_BLOCK <= qpos_l)
    forced = (sb == 0) | (sb == qpos_l // SEL_BLOCK)
    vals = jnp.where(forced, jnp.inf, jnp.where(valid_b, imp_t[:nsp], -jnp.inf))
    v_scr[...] = vals

    def body(jj, cnt):
        vj = v_scr[pl.ds(jj, 1), :]
        tie = jnp.where(sb > jj, 1.0, 0.0)
        return cnt + jnp.where(vj > vals, 1.0, jnp.where(vj == vals, tie, 0.0))

    cnt = lax.fori_loop(0, nsp, body, jnp.zeros((nsp, tq), F32))
    keep = valid_b & (cnt < float(min(TOP_N, n_sel)))
    selm_t = jnp.where(keep, 0.0, -SEL_NEG)
    if nsp < LANES:
        selm_t = jnp.concatenate([selm_t, jnp.zeros((LANES - nsp, tq), F32)], axis=0)
    selm = selm_t.T.astype(BF16)
    for b in range(nb):
        selm_ref[b, 0] = selm[b * rq:(b + 1) * rq]


def _cmp_topk(qb, ccmp, *, nb, rq, n_cmp, n_sel, pos_base):
    n, r, _ = qb.shape
    nck = ccmp.shape[3]
    nsp = -(-n_sel // 8) * 8
    tq = nb * rq
    kern = functools.partial(_cmp_topk_kernel, nb=nb, rq=rq, nck=nck, n_cmp=n_cmp, n_sel=n_sel, nsp=nsp,
                             pos_base=pos_base)
    return pl.pallas_call(
        kern,
        grid=(n // nb, G_B, r // rq),
        in_specs=[pl.BlockSpec((nb, rq, R_B * LANES), lambda a, g, j: (a, j, g)),
                  pl.BlockSpec((nb, 1, 1, nck, DH_B), lambda a, g, j: (a, 0, g, 0, 0)),
                  pl.BlockSpec((nb, 1, 1, nck, DH_B), lambda a, g, j: (a, 1, g, 0, 0))],
        out_specs=[pl.BlockSpec((nb, rq, R_B * LANES), lambda a, g, j: (a, j, g)),
                   pl.BlockSpec((nb, 1, rq, LANES), lambda a, g, j: (a, g, j, 0))],
        out_shape=[jax.ShapeDtypeStruct((n, r, H_B * DH_B), F32),
                   jax.ShapeDtypeStruct((n, G_B, r, LANES), BF16)],
        scratch_shapes=[pltpu.VMEM((tq, nck), F32), pltpu.VMEM((nsp, tq), F32)],
        compiler_params=_cparams(("parallel", "parallel", "parallel")),
        name="cmp_topk",
    )(qb, ccmp, ccmp)


def _attend_pieces(q2, pieces):
    scores = []
    m = None
    for k, _, mask in pieces:
        s = _dot_nt(q2, k)
        if mask is not None:
            s = jnp.where(mask, s, MASK_NEG)
        scores.append(s)
        ms = jnp.max(s, axis=-1, keepdims=True)
        m = ms if m is None else jnp.maximum(m, ms)
    l = jnp.zeros_like(m)
    acc = jnp.zeros((q2.shape[0], LANES), F32)
    for s, (_, v, _) in zip(scores, pieces):
        p = jnp.exp(s - m)
        l = l + jnp.sum(p, axis=-1, keepdims=True)
        acc = acc + _dot(p.astype(BF16), v)
    return acc / l


def _pad_rows(x, rows):
    return jnp.concatenate([x, jnp.zeros((rows - x.shape[0], x.shape[1]), x.dtype)], axis=0)


def _new_token_mask(m_rows, s):
    r = _iota((m_rows, LANES), 0) % s
    return _iota((m_rows, LANES), 1) <= r


def _diff_sample_kernel(pt_ref, *refs, n_pages, s, lam_init):
    kp_refs = refs[:n_pages]
    vp_refs = refs[n_pages:2 * n_pages]
    q_ref, kn_ref, vn_ref, lamw_ref, sub_ref, o_ref = refs[2 * n_pages:]
    lane = _iota((1, LANES), 1)
    lam = _diff_lambda(lamw_ref[...], lam_init)
    new_mask = _new_token_mask(2 * s, s)
    for h in range(H_A):
        cols = slice(h * LANES, (h + 1) * LANES)
        q = q_ref[0, :, cols] * (DK_A ** -0.5)
        q2 = jnp.concatenate([jnp.where(lane < DK_A, q, 0.0), jnp.where(lane >= DK_A, q, 0.0)], axis=0)
        kpast = jnp.concatenate([r[0, :, cols].astype(BF16) for r in kp_refs], axis=0)
        vpast = jnp.concatenate([_slot_rows(r, h, PAGE_SIZE).astype(BF16) for r in vp_refs], axis=0)
        knew = _pad_rows(kn_ref[0, :, cols], LANES).astype(BF16)
        vnew = _pad_rows(vn_ref[0, :, cols], LANES).astype(BF16)
        o2 = _attend_pieces(q2.astype(BF16), [(kpast, vpast, None), (knew, vnew, new_mask)])
        o_ref[0, :, cols] = _diff_finalize(o2, s, lam, sub_ref[...], lam_init)


def _page_specs(n_pages, rows, width):
    return [pl.BlockSpec((1, rows, width), functools.partial(lambda b, pt, j: (pt[b, j], 0, 0), j=j))
            for j in range(n_pages)]


def _slot_rows(ref, slot, n_tok):
    return ref[0, pl.ds(slot, n_tok, stride=SLOTS), :]


def _diff_attn_sample(qa, ka, va, cache_k, cache_v, page_table, lam_w, sub_w, lam_init):
    n, s, _ = qa.shape
    n_pages = page_table.shape[1]
    kern = functools.partial(_diff_sample_kernel, n_pages=n_pages, s=s, lam_init=lam_init)
    row_spec = pl.BlockSpec((1, s, 512), lambda b, pt: (b, 0, 0))
    grid_spec = pltpu.PrefetchScalarGridSpec(
        num_scalar_prefetch=1,
        grid=(n,),
        in_specs=_page_specs(n_pages, PAGE_SIZE, 512) + _page_specs(n_pages, PAGE_SIZE * SLOTS, LANES)
        + [row_spec, row_spec, row_spec,
           pl.BlockSpec(lam_w.shape, lambda b, pt: (0, 0)), pl.BlockSpec(sub_w.shape, lambda b, pt: (0, 0))],
        out_specs=row_spec,
    )
    return pl.pallas_call(
        kern,
        grid_spec=grid_spec,
        out_shape=jax.ShapeDtypeStruct((n, s, 512), F32),
        compiler_params=_cparams(("parallel",)),
        name="diff_attn_sample",
    )(page_table, *([cache_k] * n_pages), *([cache_v] * n_pages), qa, ka, va, lam_w, sub_w)


def _sel_sample_kernel(pt_ref, *refs, n_pages, s):
    pg_refs = refs[:n_pages]
    q_ref, selm_ref, new_ref, o_ref = refs[n_pages:]
    past = n_pages * PAGE_SIZE
    new_mask = _new_token_mask(R_B * s, s)
    oh_past = _sel_onehot(past, 0)
    new_blk = past // SEL_BLOCK
    oh_new = jnp.where((_iota((LANES, LANES), 1) == new_blk) & (_iota((LANES, LANES), 0) < s), 1.0, 0.0)
    oh_new = oh_new.astype(BF16)
    scale = DH_B ** -0.5
    for g in range(G_B):
        kcols = slice(g * LANES, (g + 1) * LANES)
        vcols = slice((G_B + g) * LANES, (G_B + g + 1) * LANES)
        selm = selm_ref[0, g]
        q2 = jnp.concatenate(
            [jnp.concatenate([(q_ref[0, :, (g * R_B + r) * LANES:(g * R_B + r + 1) * LANES] * scale).astype(BF16),
                              selm], axis=-1) for r in range(R_B)], axis=0)
        kpast = jnp.concatenate([_slot_rows(r, g, PAGE_SIZE).astype(BF16) for r in pg_refs], axis=0)
        vpast = jnp.concatenate([_slot_rows(r, G_B + g, PAGE_SIZE).astype(BF16) for r in pg_refs], axis=0)
        kpast = jnp.concatenate([kpast, oh_past], axis=-1)
        knew = jnp.concatenate([_pad_rows(new_ref[0, :, kcols], LANES).astype(BF16), oh_new], axis=-1)
        vnew = _pad_rows(new_ref[0, :, vcols], LANES).astype(BF16)
        o2 = _attend_pieces(q2, [(kpast, vpast, None), (knew, vnew, new_mask)])
        for r in range(R_B):
            o_ref[0, :, (g * R_B + r) * LANES:(g * R_B + r + 1) * LANES] = o2[r * s:(r + 1) * s]


def _sel_attn_sample(qb, selm, sel_new, cache, page_table):
    n, s, _ = qb.shape
    n_pages = page_table.shape[1]
    assert s <= SEL_BLOCK and (n_pages * PAGE_SIZE) % SEL_BLOCK == 0
    kern = functools.partial(_sel_sample_kernel, n_pages=n_pages, s=s)
    row_spec = pl.BlockSpec((1, s, 512), lambda b, pt: (b, 0, 0))
    grid_spec = pltpu.PrefetchScalarGridSpec(
        num_scalar_prefetch=1,
        grid=(n,),
        in_specs=_page_specs(n_pages, PAGE_SIZE * SLOTS, LANES)
        + [row_spec, pl.BlockSpec((1, G_B, s, LANES), lambda b, pt: (b, 0, 0, 0)), row_spec],
        out_specs=row_spec,
    )
    return pl.pallas_call(
        kern,
        grid_spec=grid_spec,
        out_shape=jax.ShapeDtypeStruct((n, s, 512), F32),
        compiler_params=_cparams(("parallel",)),
        name="nsa_sel_sample",
    )(page_table, *([cache] * n_pages), qb, selm, sel_new)


def _win_sample_kernel(q_ref, buf_ref, new_ref, o_ref, wout_ref, *, s, wb):
    new_mask = _new_token_mask(R_B * s, s)
    r = _iota((R_B * s, wb), 0) % s
    buf_mask = _iota((R_B * s, wb), 1) > r + (wb - WINDOW)
    scale = DH_B ** -0.5
    for g in range(G_B):
        kcols = slice(g * LANES, (g + 1) * LANES)
        vcols = slice((G_B + g) * LANES, (G_B + g + 1) * LANES)
        q2 = jnp.concatenate(
            [(q_ref[0, :, (g * R_B + r_) * LANES:(g * R_B + r_ + 1) * LANES] * scale).astype(BF16)
             for r_ in range(R_B)], axis=0)
        kbuf = _slot_rows(buf_ref, g, wb).astype(BF16)
        vbuf = _slot_rows(buf_ref, G_B + g, wb).astype(BF16)
        knew = _pad_rows(new_ref[0, :, kcols], LANES).astype(BF16)
        vnew = _pad_rows(new_ref[0, :, vcols], LANES).astype(BF16)
        o2 = _attend_pieces(q2, [(kbuf, vbuf, buf_mask), (knew, vnew, new_mask)])
        for r_ in range(R_B):
            o_ref[0, :, (g * R_B + r_) * LANES:(g * R_B + r_ + 1) * LANES] = o2[r_ * s:(r_ + 1) * s]
    keep = (wb - s) * SLOTS
    wout_ref[0, 0:keep] = buf_ref[0, s * SLOTS:wb * SLOTS]
    for slot in range(SLOTS):
        wout_ref[0, pl.ds(keep + slot, s, stride=SLOTS), :] = new_ref[0, :, slot * LANES:(slot + 1) * LANES]


def _win_attn_sample(qb, win_buf, win_new, past):
    n, s, _ = qb.shape
    wb = win_buf.shape[1] // SLOTS
    assert past >= wb and wb % 8 == 0 and s % 8 == 0
    kern = functools.partial(_win_sample_kernel, s=s, wb=wb)
    row_spec = pl.BlockSpec((1, s, 512), lambda b: (b, 0, 0))
    buf_spec = pl.BlockSpec((1, wb * SLOTS, LANES), lambda b: (b, 0, 0))
    return pl.pallas_call(
        kern,
        grid=(n,),
        in_specs=[row_spec, buf_spec, row_spec],
        out_specs=[row_spec, buf_spec],
        out_shape=[jax.ShapeDtypeStruct((n, s, 512), F32), jax.ShapeDtypeStruct((n, wb * SLOTS, LANES), F32)],
        compiler_params=_cparams(("parallel",)),
        name="nsa_win_sample",
    )(qb, win_buf, win_new)


def _even_out_kernel(y_ref, g1_ref, oa_ref, oc_ref, os_ref, ow_ref, gt_ref, w_ref, o_ref):
    gt = gt_ref[...]
    parts = [oa_ref[...].astype(BF16)]
    for hb in range(H_B):
        sl = slice(hb * LANES, (hb + 1) * LANES)
        ob = (gt[:, 3 * hb:3 * hb + 1] * oc_ref[:, sl] + gt[:, 3 * hb + 1:3 * hb + 2] * os_ref[:, sl]
              + gt[:, 3 * hb + 2:3 * hb + 3] * ow_ref[:, sl])
        parts.append(ob.astype(BF16))
    out = _dot(jnp.concatenate(parts, axis=-1), w_ref[...])
    y = y_ref[...]
    o_ref[...] = y + g1_ref[0] * out.reshape(y.shape)


def _even_out(til, y, mods, oa, oc, os_, ow, gt, w_out):
    return pl.pallas_call(
        _even_out_kernel,
        grid=til.grid,
        in_specs=[til.x_spec(D_MODEL), til.mod_spec(2), til.flat_spec(512), til.flat_spec(512),
                  til.flat_spec(512), til.flat_spec(512), til.flat_spec(LANES), _full_spec(w_out.shape)],
        out_specs=til.x_spec(D_MODEL),
        out_shape=jax.ShapeDtypeStruct(y.shape, F32),
        compiler_params=_cparams(("parallel", "parallel")),
        name="even_out",
    )(y, mods, oa, oc, os_, ow, gt, w_out)


def _mlp_kernel(y_ref, sh_ref, sc_ref, g_ref, nw_ref, w1_ref, w2_ref, fw_ref, o_ref, *, final):
    y = y_ref[...]
    h = _norm_mod(y, nw_ref[...], sh_ref[0], sc_ref[0]).reshape(-1, D_MODEL).astype(BF16)
    a = jnp.maximum(_dot(h, w1_ref[...]), 0.0)
    out = _dot((a * a).astype(BF16), w2_ref[...])
    y2 = y + g_ref[0] * out.reshape(y.shape)
    if final:
        y2 = _rms(y2, fw_ref[...])
    o_ref[...] = y2


def _mlp(til, y, mods, nw, w1, w2, fw, final):
    kern = functools.partial(_mlp_kernel, final=final)
    return pl.pallas_call(
        kern,
        grid=til.grid,
        in_specs=[til.x_spec(D_MODEL), til.mod_spec(3), til.mod_spec(4), til.mod_spec(5),
                  _full_spec((1, D_MODEL)), _full_spec(w1.shape), _full_spec(w2.shape), _full_spec((1, D_MODEL))],
        out_specs=til.x_spec(D_MODEL),
        out_shape=jax.ShapeDtypeStruct(y.shape, F32),
        compiler_params=_cparams(("parallel", "parallel")),
        name="mlp",
    )(y, mods, mods, mods, nw, w1, w2, fw)


def _gla_inproj_kernel(x_ref, sh_ref, sc_ref, nw_ref, w_ref, wgl_ref, wgate_ref, bgate_ref,
                       q_ref, k_ref, v_ref, r_ref, la_ref):
    h = _norm_mod(x_ref[...], nw_ref[...], sh_ref[0], sc_ref[0]).reshape(-1, D_MODEL).astype(BF16)
    nk = H_C * DK_C
    nv = H_C * DV_C
    q_ref[...] = _dot(h, w_ref[:, 0:nk]) * (DK_C ** -0.5)
    k_ref[...] = _dot(h, w_ref[:, nk:2 * nk])
    v_ref[...] = _dot(h, w_ref[:, 2 * nk:2 * nk + nv])
    r_ref[...] = _dot(h, w_ref[:, 2 * nk + nv:2 * nk + 2 * nv])
    gl = _dot(h, wgl_ref[...])
    x = _dot(gl.astype(BF16), wgate_ref[...]) + bgate_ref[...]
    log_sig = jnp.minimum(x, 0.0) - jnp.log1p(jnp.exp(-jnp.abs(x)))
    la_ref[...] = log_sig / GATE_TAU


def _gla_inproj(til, x, mods, nw, w_main, w_gl, w_gate, b_gate):
    rows_total = til.nb * til.r
    widths = (H_C * DK_C, H_C * DK_C, H_C * DV_C, H_C * DV_C, H_C * DK_C)
    return pl.pallas_call(
        _gla_inproj_kernel,
        grid=til.grid,
        in_specs=[til.x_spec(D_MODEL), til.mod_spec(0), til.mod_spec(1), _full_spec((1, D_MODEL)),
                  _full_spec(w_main.shape), _full_spec(w_gl.shape), _full_spec(w_gate.shape),
                  _full_spec(b_gate.shape)],
        out_specs=[til.flat_spec(c) for c in widths],
        out_shape=[jax.ShapeDtypeStruct((rows_total, c), F32) for c in widths],
        compiler_params=_cparams(("parallel", "parallel")),
        name="gla_inproj",
    )(x, mods, mods, nw, w_main, w_gl, w_gate, b_gate)


def _gla_chunk(q, k, v, g, s_ref, c, sub):
    tri = jnp.where(_iota((c, c), 0) >= _iota((c, c), 1), 1.0, 0.0)
    b = jnp.dot(tri, g, preferred_element_type=F32, precision=lax.Precision.HIGHEST)
    state = s_ref[...]
    o = _dot((q * jnp.exp(b)).astype(BF16), state.astype(BF16))
    lane_c = _iota((sub, c), 1)
    row_s = _iota((sub, c), 0)
    att_rows = []
    for blk in range(c // sub):
        lo = blk * sub
        qi, ki, bi = q[lo:lo + sub], k[lo:lo + sub], b[lo:lo + sub]
        att = jnp.zeros((sub, c), F32)
        if blk > 0:
            bs = b[lo - 1:lo]
            qe = qi * jnp.exp(bi - bs)
            ke = k * jnp.exp(jnp.minimum(bs - b, 0.0))
            att = jnp.where(lane_c < lo, _dot_nt(qe.astype(BF16), ke.astype(BF16)), 0.0)
        for jj in range(sub):
            e = jnp.exp(jnp.minimum(bi - bi[jj:jj + 1], 0.0))
            col = jnp.sum(qi * ki[jj:jj + 1] * e, axis=-1, keepdims=True)
            att = jnp.where((lane_c == lo + jj) & (row_s >= jj), col, att)
        att_rows.append(att)
    att = att_rows[0] if len(att_rows) == 1 else jnp.concatenate(att_rows, axis=0)
    o = o + _dot(att, v)
    bl = b[c - 1:c]
    kd = k * jnp.exp(bl - b)
    eye = _iota((DK_C, DK_C), 0) == _iota((DK_C, DK_C), 1)
    decay = jnp.sum(jnp.where(eye, jnp.exp(bl), 0.0), axis=-1, keepdims=True)
    s_ref[...] = decay * state + _dot_tn(kd.astype(BF16), v.astype(BF16))
    return o


def _gla_rec_kernel(*refs, tt, c, sub, has_s0):
    if has_s0:
        q_ref, k_ref, v_ref, g_ref, s0_ref, o_ref, sfin_ref, s_ref = refs
    else:
        q_ref, k_ref, v_ref, g_ref, o_ref, sfin_ref, s_ref = refs
    t = pl.program_id(2)

    @pl.when(t == 0)
    def _():
        s_ref[...] = s0_ref[0, 0] if has_s0 else jnp.zeros(s_ref.shape, F32)

    if tt < c:
        pad = lambda x: _pad_rows(x, c)
        o = _gla_chunk(pad(q_ref[0]), pad(k_ref[0]), pad(v_ref[0]), pad(g_ref[0]), s_ref, c, sub)
        o_ref[0] = o[:tt]
    else:
        def body(ci, carry):
            rows = pl.ds(pl.multiple_of(ci * c, c), c)
            o_ref[0, rows, :] = _gla_chunk(q_ref[0, rows, :], k_ref[0, rows, :], v_ref[0, rows, :],
                                           g_ref[0, rows, :], s_ref, c, sub)
            return carry
        lax.fori_loop(0, tt // c, body, 0)

    @pl.when(t == pl.num_programs(2) - 1)
    def _():
        sfin_ref[0, 0] = s_ref[...]


def _gla_recurrence(q, k, v, g, s0, tt, c, sub):
    n, t, _ = q.shape
    kern = functools.partial(_gla_rec_kernel, tt=tt, c=c, sub=sub, has_s0=s0 is not None)
    kspec = pl.BlockSpec((1, tt, DK_C), lambda b, h, i: (b, i, h))
    vspec = pl.BlockSpec((1, tt, DV_C), lambda b, h, i: (b, i, h))
    sspec = pl.BlockSpec((1, 1, DK_C, DV_C), lambda b, h, i: (b, h, 0, 0))
    in_specs = [kspec, kspec, vspec, kspec]
    args = [q, k, v, g]
    if s0 is not None:
        in_specs.append(sspec)
        args.append(s0)
    return pl.pallas_call(
        kern,
        grid=(n, H_C, t // tt),
        in_specs=in_specs,
        out_specs=[vspec, sspec],
        out_shape=[jax.ShapeDtypeStruct((n, t, H_C * DV_C), F32),
                   jax.ShapeDtypeStruct((n, H_C, DK_C, DV_C), F32)],
        scratch_shapes=[pltpu.VMEM((DK_C, DV_C), F32)],
        compiler_params=_cparams(("parallel", "parallel", "arbitrary")),
        name="gla_recurrence",
    )(*args)


def _gla_out_kernel(y_ref, g1_ref, o_ref_in, r_ref, nw_ref, w_ref, out_ref):
    parts = []
    for h in range(H_C):
        sl = slice(h * DV_C, (h + 1) * DV_C)
        r = r_ref[:, sl]
        parts.append((_rms(o_ref_in[:, sl], nw_ref[...]) * (r * jax.nn.sigmoid(r))).astype(BF16))
    out = _dot(jnp.concatenate(parts, axis=-1), w_ref[...])
    y = y_ref[...]
    out_ref[...] = y + g1_ref[0] * out.reshape(y.shape)


def _gla_out(til, y, mods, o, r, nw, w_out):
    return pl.pallas_call(
        _gla_out_kernel,
        grid=til.grid,
        in_specs=[til.x_spec(D_MODEL), til.mod_spec(2), til.flat_spec(H_C * DV_C), til.flat_spec(H_C * DV_C),
                  _full_spec(nw.shape), _full_spec(w_out.shape)],
        out_specs=til.x_spec(D_MODEL),
        out_shape=jax.ShapeDtypeStruct(y.shape, F32),
        compiler_params=_cparams(("parallel", "parallel")),
        name="gla_out",
    )(y, mods, o, r, nw, w_out)


def _rope_tables(pos, d):
    inv = ROPE_THETA ** (-jnp.arange(0, d, 2, dtype=F32) / d)
    ang = pos.astype(F32)[:, None] * inv[None, :]
    cos, sin = jnp.cos(ang), jnp.sin(ang)
    rep = LANES // d
    c = jnp.tile(jnp.concatenate([cos, cos], axis=-1), (1, rep))
    s = jnp.tile(jnp.concatenate([-sin, sin], axis=-1), (1, rep))
    return c, s


def _mods_for(mod_l, lo, hi):
    nb = hi - lo
    return mod_l[lo:hi].reshape(nb, 6, D_MODEL).transpose(1, 0, 2).reshape(6, nb, 1, D_MODEL)


def _prompt_tile_rows(t):
    return math.gcd(t, 256)


def _even_layer(yp, ys, mods_p, mods_s, caches, page_table, wts, lam_init, til_p, til_s):
    (c_dk, c_dv, c_cmp, c_sel, win_buf) = caches
    n, t, _ = yp.shape
    ns, s, _ = ys.shape
    n_pages = page_table.shape[1]
    past = n_pages * PAGE_SIZE
    w_in = wts["w_in"]
    n_main = w_in.shape[1] - 3 * H_B
    w_main = w_in[:, :n_main].astype(BF16)
    w_gate = jnp.pad(w_in[:, n_main:], ((0, 0), (0, LANES - 3 * H_B))).astype(BF16)
    nw1 = wts["norm1"].reshape(1, D_MODEL)
    lam_w = wts["lam_w"]
    sub_w = wts["sub_w"].reshape(1, DV_A)
    wc, pe, w2c = _compress_weights(wts["cmp_pe"], wts["cmp_w1"], wts["cmp_w2"])
    w_out = wts["w_out"].astype(BF16)

    tabs_p = _rope_tables(jnp.arange(t), DK_A) + _rope_tables(jnp.arange(t), DH_B)
    qa, ka, va, qb, cmp_kv, sel_kv, win_kv, gt = _even_inproj(til_p, yp, mods_p, nw1, w_main, w_gate, tabs_p)
    r3 = lambda a: a.reshape(n, t, a.shape[-1])
    tq = _prompt_tile_rows(t)
    oa = _diff_attn_prompt(r3(qa), r3(ka), r3(va), lam_w, sub_w, lam_init, tq)
    ccmp = _compress_prompt(r3(cmp_kv), wc, pe, w2c)
    n_cmp = (t - CMP_LEN) // CMP_STRIDE + 1
    n_sel = -(-t // SEL_BLOCK)
    o_cmp, selm = _cmp_topk(r3(qb), ccmp, nb=1, rq=tq, n_cmp=n_cmp, n_sel=n_sel, pos_base=0)
    o_sel = _nsa_attn_prompt(r3(qb), r3(sel_kv), selm, tq)
    o_win = _nsa_attn_prompt(r3(qb), r3(win_kv), None, tq)
    f2 = lambda a: a.reshape(n * t, a.shape[-1])
    yp = _even_out(til_p, yp, mods_p, f2(oa), f2(o_cmp), f2(o_sel), f2(o_win), gt, w_out)
    wl = min(WINDOW, t)
    st_p = (r3(ka).reshape(n, t, H_A, 2, DK_A), r3(va).reshape(n, t, H_A, DV_A),
            r3(cmp_kv).reshape(n, t, 2, G_B, DH_B), r3(sel_kv).reshape(n, t, 2, G_B, DH_B),
            r3(win_kv)[:, t - wl:].reshape(n, wl, 2, G_B, DH_B))

    pos_s = past + jnp.arange(s)
    tabs_s = tuple(jnp.tile(x, (til_s.b, 1)) for x in _rope_tables(pos_s, DK_A) + _rope_tables(pos_s, DH_B))
    qa, ka, va, qb, cmp_kv, sel_kv, win_kv, gt = _even_inproj(til_s, ys, mods_s, nw1, w_main, w_gate, tabs_s)
    r3 = lambda a: a.reshape(ns, s, a.shape[-1])
    slot_cache = lambda c: c.reshape(c.shape[0], PAGE_SIZE * SLOTS, LANES)
    oa = _diff_attn_sample(r3(qa), r3(ka), r3(va), c_dk.reshape(c_dk.shape[0], PAGE_SIZE, 512), slot_cache(c_dv),
                           page_table, lam_w, sub_w, lam_init)
    ccmp = _compress_sample(slot_cache(c_cmp), page_table, wc, pe, w2c)
    total = past + s
    n_cmp = (total - CMP_LEN) // CMP_STRIDE + 1
    n_sel = -(-total // SEL_BLOCK)
    assert n_cmp <= ccmp.shape[3] and n_sel <= LANES
    nb = math.gcd(ns, LANES // s)
    o_cmp, selm = _cmp_topk(r3(qb), ccmp, nb=nb, rq=s, n_cmp=n_cmp, n_sel=n_sel, pos_base=past)
    o_sel = _sel_attn_sample(r3(qb), selm, r3(sel_kv), slot_cache(c_sel), page_table)
    wb = win_buf.shape[1]
    o_win, win_out = _win_attn_sample(r3(qb), win_buf.reshape(ns, wb * SLOTS, LANES), r3(win_kv), past)
    f2 = lambda a: a.reshape(ns * s, a.shape[-1])
    ys = _even_out(til_s, ys, mods_s, f2(oa), f2(o_cmp), f2(o_sel), f2(o_win), gt, w_out)
    st_s = (r3(ka).reshape(ns, s, H_A, 2, DK_A), r3(va).reshape(ns, s, H_A, DV_A),
            r3(cmp_kv).reshape(ns, s, 2, G_B, DH_B), r3(sel_kv).reshape(ns, s, 2, G_B, DH_B),
            win_out.reshape(ns, wb, 2, G_B, DH_B))
    return yp, ys, st_p, st_s


def _odd_layer(yp, ys, mods_p, mods_s, s0, wts, til_p, til_s):
    w_in = wts["w_in"]
    n_main = w_in.shape[1] - GATE_RANK
    w_main = w_in[:, :n_main].astype(BF16)
    w_gl = jnp.pad(w_in[:, n_main:], ((0, 0), (0, LANES - GATE_RANK))).astype(BF16)
    w_gate = jnp.pad(wts["w_gate"], ((0, LANES - GATE_RANK), (0, 0))).astype(BF16)
    b_gate = wts["b_gate"].reshape(1, -1)
    nw1 = wts["norm1"].reshape(1, D_MODEL)
    gnw = wts["gnorm"].reshape(1, DV_C)
    w_out = wts["w_out"].astype(BF16)
    outs = []
    for y, mods, til, state in ((yp, mods_p, til_p, None), (ys, mods_s, til_s, s0)):
        n, t, _ = y.shape
        q, k, v, r, la = _gla_inproj(til, y, mods, nw1, w_main, w_gl, w_gate, b_gate)
        r3 = lambda a: a.reshape(n, t, a.shape[-1])
        c = math.gcd(t, GLA_CHUNK)
        if c >= GLA_SUB:
            tt, cc = math.gcd(t, 8 * c), c
        else:
            tt, cc = t, GLA_SUB
        o, s_fin = _gla_recurrence(r3(q), r3(k), r3(v), r3(la), state, tt, cc, GLA_SUB)
        y = _gla_out(til, y, mods, o.reshape(n * t, -1), r, gnw, w_out)
        outs.append((y, s_fin))
    return outs[0][0], outs[1][0], outs[0][1], outs[1][1]


def kernel(x_prompt, x_sample, c_prompt, c_sample, cache_diff_k, cache_diff_v, cache_cmp_kv, cache_sel_kv,
           state_win_kv, state_gla, page_table, norm1_w, norm2_w, ada_w, ada_b, even_w_in, even_w_out,
           diff_lambda_w, diff_subln_w, cmp_pe, cmp_w1, cmp_w2, gla_w_in, gla_w_gate, gla_b_gate, gla_norm_w,
           gla_w_out, mlp_w1, mlp_w2, final_norm_w):
    depth = ada_w.shape[0]
    n, t, _ = x_prompt.shape
    ns, s, _ = x_sample.shape
    til_p = _Tiling(n, t, 1, _prompt_tile_rows(t))
    til_s = _Tiling(ns, s, math.gcd(ns, 256 // s), s)

    pad = (-(n + ns)) % 8
    c_all = jnp.concatenate([c_prompt, c_sample, jnp.zeros((pad, D_MODEL), F32)], axis=0)
    mod = _adaln(c_all, ada_w, ada_b)

    yp, ys = x_prompt, x_sample
    st_p = [[] for _ in range(6)]
    st_s = [[] for _ in range(6)]
    fw = final_norm_w.reshape(1, D_MODEL)
    for l in range(depth):
        mods_p = _mods_for(mod[l], 0, n)
        mods_s = _mods_for(mod[l], n, n + ns)
        if l % 2 == 0:
            e = l // 2
            lam_init = 0.8 - 0.6 * math.exp(-0.3 * l)
            wts = dict(w_in=even_w_in[e], w_out=even_w_out[e], lam_w=diff_lambda_w[e], sub_w=diff_subln_w[e],
                       cmp_pe=cmp_pe[e], cmp_w1=cmp_w1[e], cmp_w2=cmp_w2[e], norm1=norm1_w[l])
            caches = (cache_diff_k[e], cache_diff_v[e], cache_cmp_kv[e], cache_sel_kv[e], state_win_kv[e])
            yp, ys, sp, ss = _even_layer(yp, ys, mods_p, mods_s, caches, page_table, wts, lam_init, til_p, til_s)
            for i in range(5):
                st_p[i].append(sp[i])
                st_s[i].append(ss[i])
        else:
            o = l // 2
            wts = dict(w_in=gla_w_in[o], w_gate=gla_w_gate[o], b_gate=gla_b_gate[o], gnorm=gla_norm_w[o],
                       w_out=gla_w_out[o], norm1=norm1_w[l])
            yp, ys, gp, gs = _odd_layer(yp, ys, mods_p, mods_s, state_gla[o], wts, til_p, til_s)
            st_p[5].append(gp)
            st_s[5].append(gs)
        final = l == depth - 1
        w1 = mlp_w1[l].astype(BF16)
        w2 = mlp_w2[l].astype(BF16)
        nw2 = norm2_w[l].reshape(1, D_MODEL)
        yp = _mlp(til_p, yp, mods_p, nw2, w1, w2, fw, final)
        ys = _mlp(til_s, ys, mods_s, nw2, w1, w2, fw, final)
    outs_p = [jnp.stack(x, axis=0) for x in st_p]
    outs_s = [jnp.stack(x, axis=0) for x in st_s]
    return (yp, ys, *outs_p, *outs_s)
```

```python
import functools
import math

import jax
import jax.numpy as jnp
from jax import lax
from jax.experimental import pallas as pl
from jax.experimental.pallas import tpu as pltpu

F32 = jnp.float32
BF16 = jnp.bfloat16

D_MODEL = 1024
PAGE_SIZE = 128
H_A = 4
DK_A = 64
DV_A = 128
H_B = 4
G_B = 2
R_B = 2
DH_B = 128
CMP_LEN = 32
CMP_STRIDE = 16
SEL_BLOCK = 64
TOP_N = 16
WINDOW = 512
H_C = 4
DK_C = 128
DV_C = 256
GATE_RANK = 16
GATE_TAU = 16.0
GLA_CHUNK = 64
GLA_SUB = 16
D_FF = 4 * D_MODEL
ROPE_THETA = 10000.0
EPS = 1e-6

LANES = 128
SLOTS = 4
MASK_NEG = -1e30
LOG2E = 1.4426950408889634
SEL_NEG = 32768.0
VMEM_LIMIT_MB = 56


def _cparams(sem, vmem_mb=VMEM_LIMIT_MB):
    return pltpu.CompilerParams(dimension_semantics=sem, vmem_limit_bytes=vmem_mb * 1024 * 1024)


def _dot(a, b):
    return jnp.dot(a, b, preferred_element_type=F32)


def _dot_nt(a, b):
    return lax.dot_general(a, b, (((1,), (1,)), ((), ())), preferred_element_type=F32)


def _dot_tn(a, b):
    return lax.dot_general(a, b, (((0,), (0,)), ((), ())), preferred_element_type=F32)


def _iota(shape, dim):
    return lax.broadcasted_iota(jnp.int32, shape, dim)


def _rms(x, w):
    ms = jnp.mean(x * x, axis=-1, keepdims=True)
    return x * lax.rsqrt(ms + EPS) * w


def _norm_mod(x, nw, shift, scale):
    return _rms(x, nw) * (1.0 + scale) + shift


class _Tiling:
    def __init__(self, nb, r, b, rt):
        assert nb % b == 0 and r % rt == 0 and (b == 1 or rt == r)
        self.nb, self.r, self.b, self.rt = nb, r, b, rt
        self.grid = (nb // b, r // rt)
        self.rows = b * rt
        self.nrb = r // rt

    def x_spec(self, d):
        return pl.BlockSpec((self.b, self.rt, d), lambda i, j: (i, j, 0))

    def mod_spec(self, k):
        return pl.BlockSpec((1, self.b, 1, D_MODEL), lambda i, j: (k, i, 0, 0))

    def flat_spec(self, c):
        nrb = self.nrb
        return pl.BlockSpec((self.rows, c), lambda i, j: (i * nrb + j, 0))

    def tab_spec(self):
        return pl.BlockSpec((self.rows, LANES), lambda i, j: (j, 0))


def _full_spec(shape):
    nd = len(shape)
    return pl.BlockSpec(shape, lambda *_: (0,) * nd)


def _adaln_kernel(c_ref, w_ref, b_ref, o_ref):
    c = c_ref[...]
    a = (c * jax.nn.sigmoid(c)).astype(BF16)
    o_ref[0] = _dot(a, w_ref[0].astype(BF16)) + b_ref[0]


def _adaln(c_all, ada_w, ada_b):
    depth, d, n6 = ada_w.shape
    rows = c_all.shape[0]
    tn = 1536
    return pl.pallas_call(
        _adaln_kernel,
        grid=(depth, n6 // tn),
        in_specs=[pl.BlockSpec((rows, d), lambda l, j: (0, 0)),
                  pl.BlockSpec((1, d, tn), lambda l, j: (l, 0, j)),
                  pl.BlockSpec((1, 1, tn), lambda l, j: (l, 0, j))],
        out_specs=pl.BlockSpec((1, rows, tn), lambda l, j: (l, 0, j)),
        out_shape=jax.ShapeDtypeStruct((depth, rows, n6), F32),
        compiler_params=_cparams(("parallel", "parallel")),
        name="adaln",
    )(c_all, ada_w, ada_b.reshape(depth, 1, n6))


def _swap_half(x, half):
    if 2 * half == LANES:
        return pltpu.roll(x, half, 1)
    lane = _iota((1, LANES), 1)
    lo = (lane % (2 * half)) < half
    return jnp.where(lo, pltpu.roll(x, LANES - half, 1), pltpu.roll(x, half, 1))


def _even_inproj_kernel(x_ref, sh_ref, sc_ref, nw_ref, w_ref, wg_ref, c64_ref, s64_ref, c128_ref, s128_ref,
                        qa_ref, ka_ref, va_ref, qb_ref, cmp_ref, sel_ref, win_ref, gt_ref):
    h = _norm_mod(x_ref[...], nw_ref[...], sh_ref[0], sc_ref[0])
    h = h.reshape(-1, D_MODEL).astype(BF16)
    c64, s64, c128, s128 = c64_ref[...], s64_ref[...], c128_ref[...], s128_ref[...]

    def rope64(p):
        return p * c64 + _swap_half(p, DK_A // 2) * s64

    def rope128(p):
        return p * c128 + _swap_half(p, DH_B // 2) * s128

    def project(ref, off, ropes):
        p = _dot(h, w_ref[:, off:off + 4 * LANES])
        for j, rope in enumerate(ropes):
            sl = slice(j * LANES, (j + 1) * LANES)
            ref[:, sl] = p[:, sl] if rope is None else rope(p[:, sl])

    project(qa_ref, 0, [rope64] * 4)
    project(ka_ref, 512, [rope64] * 4)
    project(va_ref, 1024, [None] * 4)
    project(qb_ref, 1536, [rope128] * 4)
    for t, ref in enumerate((cmp_ref, sel_ref, win_ref)):
        project(ref, 2048 + t * 512, [rope128, rope128, None, None])
    gt_ref[...] = jax.nn.sigmoid(_dot(h, wg_ref[...]))


def _even_inproj(til, x, mods, nw, w_main, w_gate, tabs):
    rows_total = til.nb * til.r
    widths = (512, 512, 512, 512, 512, 512, 512, LANES)
    return pl.pallas_call(
        _even_inproj_kernel,
        grid=til.grid,
        in_specs=[til.x_spec(D_MODEL), til.mod_spec(0), til.mod_spec(1), _full_spec((1, D_MODEL)),
                  _full_spec(w_main.shape), _full_spec(w_gate.shape)] + [til.tab_spec()] * 4,
        out_specs=[til.flat_spec(c) for c in widths],
        out_shape=[jax.ShapeDtypeStruct((rows_total, c), F32) for c in widths],
        compiler_params=_cparams(("parallel", "parallel")),
        name="even_inproj",
    )(x, mods, mods, nw, w_main, w_gate, *tabs)


def _for_tiles(lo, hi, fn):
    n = hi - lo

    def pair(j, carry):
        fn(lo + 2 * j)
        fn(lo + 2 * j + 1)
        return carry

    lax.fori_loop(0, n // 2, pair, 0)

    @pl.when(n % 2 == 1)
    def _():
        fn(hi - 1)


def _two_pass_attention(mx_ref, acc_ref, lo, hi, last, scores, values, mask_body=False):
    def lane_max(s):
        m = s[:, 0:LANES]
        for c in range(1, s.shape[1] // LANES):
            m = jnp.maximum(m, s[:, c * LANES:(c + 1) * LANES])
        return m

    mx_ref[...] = jnp.full(mx_ref.shape, MASK_NEG, F32)

    def pass1(kb, masked):
        mx_ref[...] = jnp.maximum(mx_ref[...], lane_max(scores(kb, masked)))

    _for_tiles(lo, hi, lambda kb: pass1(kb, mask_body))
    pass1(last, True)
    mx_ref[...] = jnp.broadcast_to(jnp.max(mx_ref[...], axis=-1, keepdims=True), mx_ref.shape)
    acc_ref[...] = jnp.zeros(acc_ref.shape, F32)

    def pass2(kb, masked):
        s = scores(kb, masked)
        m = mx_ref[...]
        p = jnp.concatenate([jnp.exp2(s[:, c * LANES:(c + 1) * LANES] - m) for c in range(s.shape[1] // LANES)],
                            axis=-1).astype(BF16)
        v = values(kb)
        v1 = jnp.concatenate([v, jnp.ones(v.shape, BF16)], axis=-1)
        acc_ref[...] += _dot(p, v1)

    _for_tiles(lo, hi, lambda kb: pass2(kb, mask_body))
    pass2(last, True)
    return acc_ref[:, 0:LANES] / acc_ref[:, LANES:2 * LANES]


def _diff_lambda(lw, lam_init):
    a = jnp.sum(lw[0:1] * lw[1:2], axis=-1, keepdims=True)
    b = jnp.sum(lw[2:3] * lw[3:4], axis=-1, keepdims=True)
    return jnp.exp(a) - jnp.exp(b) + lam_init


def _diff_finalize(o2, tq, lam, sub_w, lam_init):
    od = o2[:tq] - lam * o2[tq:]
    return _rms(od, sub_w) * (1.0 - lam_init)


def _diff_flash_kernel(q_ref, k_ref, v_ref, lamw_ref, sub_ref, o_ref, q2_ref, mx_ref, acc_ref, *,
                       tq, lam_init):
    i = pl.program_id(2)
    q = q_ref[0] * (DK_A ** -0.5 * LOG2E)
    lane = _iota((1, LANES), 1)
    q2_ref[0:tq] = jnp.where(lane < DK_A, q, 0.0).astype(BF16)
    q2_ref[tq:2 * tq] = jnp.where(lane >= DK_A, q, 0.0).astype(BF16)

    def scores(kb, masked):
        k = k_ref[0, pl.ds(pl.multiple_of(kb * tq, tq), tq), :].astype(BF16)
        s = _dot_nt(q2_ref[...], k)
        if masked:
            r = _iota((2 * tq, tq), 0)
            r = jnp.where(r >= tq, r - tq, r)
            s = jnp.where(_iota((2 * tq, tq), 1) <= r, s, MASK_NEG)
        return s

    def values(kb):
        return v_ref[0, pl.ds(pl.multiple_of(kb * tq, tq), tq), :].astype(BF16)

    o2 = _two_pass_attention(mx_ref, acc_ref, 0, i, i, scores, values)
    o_ref[0] = _diff_finalize(o2, tq, _diff_lambda(lamw_ref[...], lam_init), sub_ref[...], lam_init)


def _diff_attn_prompt(qa, ka, va, lam_w, sub_w, lam_init, tq):
    n, t, _ = qa.shape
    kern = functools.partial(_diff_flash_kernel, tq=tq, lam_init=lam_init)
    return pl.pallas_call(
        kern,
        grid=(n, H_A, t // tq),
        in_specs=[pl.BlockSpec((1, tq, LANES), lambda b, h, i: (b, i, h)),
                  pl.BlockSpec((1, t, LANES), lambda b, h, i: (b, 0, h)),
                  pl.BlockSpec((1, t, LANES), lambda b, h, i: (b, 0, h)),
                  _full_spec(lam_w.shape), _full_spec(sub_w.shape)],
        out_specs=pl.BlockSpec((1, tq, LANES), lambda b, h, i: (b, i, h)),
        out_shape=jax.ShapeDtypeStruct((n, t, H_A * DV_A), F32),
        scratch_shapes=[pltpu.VMEM((2 * tq, LANES), BF16), pltpu.VMEM((2 * tq, LANES), F32),
                        pltpu.VMEM((2 * tq, 2 * LANES), F32)],
        compiler_params=_cparams(("parallel", "parallel", "parallel")),
        name="diff_attn_prompt",
    )(qa, ka, va, lam_w, sub_w)


def _sel_onehot(rows, first_block):
    blk = _iota((rows, LANES), 0) // SEL_BLOCK + first_block
    return jnp.where(blk == _iota((rows, LANES), 1), 1.0, 0.0).astype(BF16)


def _nsa_flash_kernel(*refs, tq, use_sel):
    if use_sel:
        q_ref, selm_ref, k_ref, v_ref, o_ref, q2_ref, mx_ref, acc_ref = refs
    else:
        q_ref, k_ref, v_ref, o_ref, q2_ref, mx_ref, acc_ref = refs
    i = pl.program_id(2)
    q = q_ref[0] * (DH_B ** -0.5 * LOG2E)
    for r in range(R_B):
        q2_ref[r * tq:(r + 1) * tq, 0:LANES] = q[:, r * LANES:(r + 1) * LANES].astype(BF16)
        if use_sel:
            q2_ref[r * tq:(r + 1) * tq, LANES:2 * LANES] = selm_ref[0, 0]

    def scores(kb, masked):
        k = k_ref[0, pl.ds(pl.multiple_of(kb * tq, tq), tq), :].astype(BF16)
        if use_sel:
            k = jnp.concatenate([k, _sel_onehot(tq, kb * (tq // SEL_BLOCK))], axis=-1)
        s = _dot_nt(q2_ref[...], k)
        if masked:
            r = _iota((R_B * tq, tq), 0)
            qp = i * tq + jnp.where(r >= tq, r - tq, r)
            kp = kb * tq + _iota((R_B * tq, tq), 1)
            ok = kp <= qp
            if not use_sel:
                ok = ok & (kp > qp - WINDOW)
            s = jnp.where(ok, s, MASK_NEG)
        return s

    def values(kb):
        return v_ref[0, pl.ds(pl.multiple_of(kb * tq, tq), tq), :].astype(BF16)

    lo = 0 if use_sel else jnp.maximum(i - WINDOW // tq, 0)
    o = _two_pass_attention(mx_ref, acc_ref, lo, i, i, scores, values, mask_body=not use_sel)
    for r in range(R_B):
        o_ref[0, :, r * LANES:(r + 1) * LANES] = o[r * tq:(r + 1) * tq]


def _nsa_attn_prompt(qb, kv, selm, tq):
    n, t, _ = qb.shape
    use_sel = selm is not None
    kd = 2 * LANES if use_sel else LANES
    kern = functools.partial(_nsa_flash_kernel, tq=tq, use_sel=use_sel)
    in_specs = [pl.BlockSpec((1, tq, R_B * LANES), lambda b, g, i: (b, i, g))]
    args = [qb]
    if use_sel:
        in_specs.append(pl.BlockSpec((1, 1, tq, LANES), lambda b, g, i: (b, g, i, 0)))
        args.append(selm)
    in_specs += [pl.BlockSpec((1, t, LANES), lambda b, g, i: (b, 0, g)),
                 pl.BlockSpec((1, t, LANES), lambda b, g, i: (b, 0, G_B + g))]
    args += [kv, kv]
    return pl.pallas_call(
        kern,
        grid=(n, G_B, t // tq),
        in_specs=in_specs,
        out_specs=pl.BlockSpec((1, tq, R_B * LANES), lambda b, g, i: (b, i, g)),
        out_shape=jax.ShapeDtypeStruct((n, t, H_B * DH_B), F32),
        scratch_shapes=[pltpu.VMEM((R_B * tq, kd), BF16), pltpu.VMEM((R_B * tq, LANES), F32),
                        pltpu.VMEM((R_B * tq, 2 * LANES), F32)],
        compiler_params=_cparams(("parallel", "parallel", "parallel")),
        name="nsa_sel_prompt" if use_sel else "nsa_win_prompt",
    )(*args)


def _compress_core(load, wc_ref, pe_ref, w2_ref, wi, nchunk):
    rows = G_B * nchunk
    acc = jnp.zeros((rows, 2 * LANES), F32)
    pew = jnp.zeros((16, 2 * LANES), F32)
    for u in range(CMP_STRIDE // 2):
        lhs = jnp.concatenate([load(2 * u), load(2 * u + 1)], axis=-1).astype(BF16)
        w = wc_ref[wi, u]
        acc = acc + _dot(lhs, w)
        pew = pew + _dot(pe_ref[wi, u], w)
    first = acc[:, :LANES]
    second = pltpu.roll(acc[:, LANES:], rows - 1, 0)
    hid = first + second + pew[0:1, :LANES] + pew[8:9, LANES:]
    hid = hid * jax.nn.sigmoid(hid)
    return _dot(hid.astype(BF16), w2_ref[wi])


def _compress_prompt_kernel(*refs, nchunk):
    x_refs = refs[:G_B]
    wc_ref, pe_ref, w2_ref, o_ref = refs[G_B:]

    def load(tok):
        return jnp.concatenate([xr[0, pl.ds(tok, nchunk, stride=CMP_STRIDE), :] for xr in x_refs], axis=0)

    out = _compress_core(load, wc_ref, pe_ref, w2_ref, 0, nchunk)
    for g in range(G_B):
        o_ref[0, 0, g] = out[g * nchunk:(g + 1) * nchunk]


def _compress_sample_kernel(pt_ref, *refs, n_pages, spb):
    pg_refs = refs[:spb * n_pages]
    wc_ref, pe_ref, w2_ref, o_ref = refs[spb * n_pages:]
    cps = PAGE_SIZE // CMP_STRIDE
    nchunk = n_pages * cps
    for kv in range(2):
        def load(tok):
            return jnp.concatenate(
                [pg[0, pl.ds(tok * SLOTS + kv * G_B + g, cps, stride=CMP_STRIDE * SLOTS), :]
                 for smp in range(spb) for g in range(G_B) for pg in pg_refs[smp * n_pages:(smp + 1) * n_pages]],
                axis=0)

        out = _compress_core(load, wc_ref, pe_ref, w2_ref, kv, spb * nchunk)
        for smp in range(spb):
            for g in range(G_B):
                seg = smp * G_B + g
                o_ref[smp, kv, g] = out[seg * nchunk:(seg + 1) * nchunk]


def _compress_weights(cmp_pe, cmp_w1, cmp_w2):
    w1 = cmp_w1.reshape(2, CMP_LEN, DH_B, DH_B)
    wab = jnp.concatenate([w1[:, :CMP_STRIDE], w1[:, CMP_STRIDE:]], axis=-1)
    wc = wab.reshape(2, CMP_STRIDE // 2, 2 * DH_B, 2 * DH_B).astype(BF16)
    pa = cmp_pe[:, :CMP_STRIDE].reshape(2, CMP_STRIDE // 2, 1, 2 * DH_B)
    pb = cmp_pe[:, CMP_STRIDE:].reshape(2, CMP_STRIDE // 2, 1, 2 * DH_B)
    z = jnp.zeros((2, CMP_STRIDE // 2, 7, 2 * DH_B), F32)
    pe = jnp.concatenate([pa, z, pb, z], axis=2).astype(BF16)
    return wc, pe, cmp_w2.astype(BF16)


def _compress_prompt(cmp_kv, wc, pe, w2):
    n, t, _ = cmp_kv.shape
    nchunk = t // CMP_STRIDE
    kern = functools.partial(_compress_prompt_kernel, nchunk=nchunk)
    return pl.pallas_call(
        kern,
        grid=(n, 2),
        in_specs=[pl.BlockSpec((1, t, DH_B), functools.partial(lambda b, kv, g: (b, 0, kv * G_B + g), g=g))
                  for g in range(G_B)] + [
                  pl.BlockSpec((1,) + wc.shape[1:], lambda b, kv: (kv, 0, 0, 0)),
                  pl.BlockSpec((1,) + pe.shape[1:], lambda b, kv: (kv, 0, 0, 0)),
                  pl.BlockSpec((1, DH_B, DH_B), lambda b, kv: (kv, 0, 0))],
        out_specs=pl.BlockSpec((1, 1, G_B, nchunk, DH_B), lambda b, kv: (b, kv, 0, 0, 0)),
        out_shape=jax.ShapeDtypeStruct((n, 2, G_B, nchunk, DH_B), F32),
        compiler_params=_cparams(("parallel", "parallel")),
        name="compress_prompt",
    )(*([cmp_kv] * G_B), wc, pe, w2)


def _compress_sample(cache, page_table, wc, pe, w2):
    n, n_pages = page_table.shape
    nchunk = n_pages * (PAGE_SIZE // CMP_STRIDE)
    spb = math.gcd(n, 2)
    kern = functools.partial(_compress_sample_kernel, n_pages=n_pages, spb=spb)
    page_specs = [pl.BlockSpec((1, PAGE_SIZE * SLOTS, LANES),
                               functools.partial(lambda b, pt, smp, j: (pt[b * spb + smp, j], 0, 0), smp=smp, j=j))
                  for smp in range(spb) for j in range(n_pages)]
    grid_spec = pltpu.PrefetchScalarGridSpec(
        num_scalar_prefetch=1,
        grid=(n // spb,),
        in_specs=page_specs + [
            pl.BlockSpec(wc.shape, lambda b, pt: (0, 0, 0, 0)),
            pl.BlockSpec(pe.shape, lambda b, pt: (0, 0, 0, 0)),
            pl.BlockSpec(w2.shape, lambda b, pt: (0, 0, 0))],
        out_specs=pl.BlockSpec((spb, 2, G_B, nchunk, DH_B), lambda b, pt: (b, 0, 0, 0, 0)),
    )
    return pl.pallas_call(
        kern,
        grid_spec=grid_spec,
        out_shape=jax.ShapeDtypeStruct((n, 2, G_B, nchunk, DH_B), F32),
        compiler_params=_cparams(("parallel",)),
        name="compress_sample",
    )(page_table, *([cache] * (spb * n_pages)), wc, pe, w2)


def _cmp_topk_kernel(q_ref, kc_ref, vc_ref, ocmp_ref, selm_ref, p_scr, v_scr, *,
                     nb, rq, nck, n_cmp, n_sel, nsp, pos_base):
    tq = nb * rq
    j = pl.program_id(2)
    scale = DH_B ** -0.5
    kidx = _iota((1, nck), 1)
    qpos_c = pos_base + j * rq + _iota((rq, 1), 0)
    valid = (kidx < n_cmp) & (CMP_STRIDE * kidx + (CMP_LEN - 1) <= qpos_c)
    for b in range(nb):
        kcc = kc_ref[b, 0, 0].astype(BF16)
        vcc = vc_ref[b, 0, 0].astype(BF16)
        q = q_ref[b] * scale
        psum = jnp.zeros((rq, nck), F32)
        for r in range(R_B):
            s = _dot_nt(q[:, r * LANES:(r + 1) * LANES].astype(BF16), kcc)
            s = jnp.where(valid, s, MASK_NEG)
            m = jnp.max(s, axis=-1, keepdims=True)
            p = jnp.where(valid, jnp.exp(s - m), 0.0)
            p = p / jnp.maximum(jnp.sum(p, axis=-1, keepdims=True), 1e-30)
            ocmp_ref[b, :, r * LANES:(r + 1) * LANES] = _dot(p.astype(BF16), vcc)
            psum = psum + p
        p_scr[b * rq:(b + 1) * rq, :] = psum
    psum = p_scr[...]
    p_hi = psum.astype(BF16)
    p_lo = (psum - p_hi.astype(F32)).astype(BF16)
    srow = _iota((LANES, nck), 0)
    kcol = _iota((LANES, nck), 1)
    cov = (CMP_STRIDE * kcol < SEL_BLOCK * srow + SEL_BLOCK) & (CMP_STRIDE * kcol + CMP_LEN > SEL_BLOCK * srow)
    cov = cov & (srow < n_sel) & (kcol < n_cmp)
    cov_t = jnp.where(cov, 1.0, 0.0).astype(BF16)
    imp_t = _dot_nt(cov_t, p_hi) + _dot_nt(cov_t, p_lo)
    qpos_l = pos_base + j * rq + _iota((1, tq), 1) % rq
    sb = _iota((nsp, 1), 0)
    valid_b = (sb < n_sel) & (sb * SEL_BLOCK <= qpos_l)
    forced = (sb == 0) | (sb == qpos_l // SEL_BLOCK)
    vals = jnp.where(forced, jnp.inf, jnp.where(valid_b, imp_t[:nsp], -jnp.inf))
    v_scr[...] = vals

    def body(jj, cnt):
        vj = v_scr[pl.ds(jj, 1), :]
        tie = jnp.where(sb > jj, 1.0, 0.0)
        return cnt + jnp.where(vj > vals, 1.0, jnp.where(vj == vals, tie, 0.0))

    cnt = lax.fori_loop(0, nsp, body, jnp.zeros((nsp, tq), F32))
    keep = valid_b & (cnt < float(min(TOP_N, n_sel)))
    selm_t = jnp.where(keep, 0.0, -SEL_NEG)
    if nsp < LANES:
        selm_t = jnp.concatenate([selm_t, jnp.zeros((LANES - nsp, tq), F32)], axis=0)
    selm = selm_t.T.astype(BF16)
    for b in range(nb):
        selm_ref[b, 0] = selm[b * rq:(b + 1) * rq]


def _cmp_topk(qb, ccmp, *, nb, rq, n_cmp, n_sel, pos_base):
    n, r, _ = qb.shape
    nck = ccmp.shape[3]
    nsp = -(-n_sel // 8) * 8
    tq = nb * rq
    kern = functools.partial(_cmp_topk_kernel, nb=nb, rq=rq, nck=nck, n_cmp=n_cmp, n_sel=n_sel, nsp=nsp,
                             pos_base=pos_base)
    return pl.pallas_call(
        kern,
        grid=(n // nb, G_B, r // rq),
        in_specs=[pl.BlockSpec((nb, rq, R_B * LANES), lambda a, g, j: (a, j, g)),
                  pl.BlockSpec((nb, 1, 1, nck, DH_B), lambda a, g, j: (a, 0, g, 0, 0)),
                  pl.BlockSpec((nb, 1, 1, nck, DH_B), lambda a, g, j: (a, 1, g, 0, 0))],
        out_specs=[pl.BlockSpec((nb, rq, R_B * LANES), lambda a, g, j: (a, j, g)),
                   pl.BlockSpec((nb, 1, rq, LANES), lambda a, g, j: (a, g, j, 0))],
        out_shape=[jax.ShapeDtypeStruct((n, r, H_B * DH_B), F32),
                   jax.ShapeDtypeStruct((n, G_B, r, LANES), BF16)],
        scratch_shapes=[pltpu.VMEM((tq, nck), F32), pltpu.VMEM((nsp, tq), F32)],
        compiler_params=_cparams(("parallel", "parallel", "parallel")),
        name="cmp_topk",
    )(qb, ccmp, ccmp)


def _attend_scores(pieces):
    m = None
    for s, _ in pieces:
        ms = jnp.max(s, axis=-1, keepdims=True)
        m = ms if m is None else jnp.maximum(m, ms)
    l = jnp.zeros_like(m)
    acc = jnp.zeros((m.shape[0], LANES), F32)
    for s, v in pieces:
        p = jnp.exp(s - m)
        l = l + jnp.sum(p, axis=-1, keepdims=True)
        acc = acc + _dot(p.astype(BF16), v)
    return acc / l


def _attend_pieces(q2, pieces):
    scored = []
    for k, v, mask in pieces:
        s = _dot_nt(q2, k)
        scored.append((s if mask is None else jnp.where(mask, s, MASK_NEG), v))
    return _attend_scores(scored)


def _pad_rows(x, rows):
    return jnp.concatenate([x, jnp.zeros((rows - x.shape[0], x.shape[1]), x.dtype)], axis=0)


def _new_token_mask(m_rows, s):
    r = _iota((m_rows, LANES), 0) % s
    return _iota((m_rows, LANES), 1) <= r


def _diff_sample_kernel(pt_ref, *refs, n_pages, s, lam_init):
    kp_refs = refs[:n_pages]
    vp_refs = refs[n_pages:2 * n_pages]
    q_ref, kn_ref, vn_ref, lamw_ref, sub_ref, o_ref, kscr = refs[2 * n_pages:]
    krows = PAGE_SIZE * 2 * H_A

    @pl.when(pl.program_id(0) == 0)
    def _():
        kscr[...] = jnp.zeros(kscr.shape, F32)

    for j, r in enumerate(kp_refs):
        kscr[j * krows:(j + 1) * krows, 0:DK_A] = r[0]
    lane = _iota((1, LANES), 1)
    lam = _diff_lambda(lamw_ref[...], lam_init)
    new_mask = _new_token_mask(2 * s, s)
    for h in range(H_A):
        cols = slice(h * LANES, (h + 1) * LANES)
        q = q_ref[0, :, cols] * (DK_A ** -0.5)
        s_past = []
        for mp in range(2):
            kmap = jnp.concatenate(
                [kscr[pl.ds(j * krows + 2 * h + mp, PAGE_SIZE, stride=2 * H_A), :][:, 0:DK_A].astype(BF16)
                 for j in range(n_pages)], axis=0)
            s_past.append(_dot_nt(q[:, mp * DK_A:(mp + 1) * DK_A].astype(BF16), kmap))
        s_past = jnp.concatenate(s_past, axis=0)
        vpast = jnp.concatenate([_slot_rows(r, h, PAGE_SIZE).astype(BF16) for r in vp_refs], axis=0)
        q2 = jnp.concatenate([jnp.where(lane < DK_A, q, 0.0), jnp.where(lane >= DK_A, q, 0.0)], axis=0)
        knew = _pad_rows(kn_ref[0, :, cols], LANES).astype(BF16)
        vnew = _pad_rows(vn_ref[0, :, cols], LANES).astype(BF16)
        s_new = jnp.where(new_mask, _dot_nt(q2.astype(BF16), knew), MASK_NEG)
        o2 = _attend_scores([(s_past, vpast), (s_new, vnew)])
        o_ref[0, :, cols] = _diff_finalize(o2, s, lam, sub_ref[...], lam_init)


def _page_specs(n_pages, rows, width):
    return [pl.BlockSpec((1, rows, width), functools.partial(lambda b, pt, j: (pt[b, j], 0, 0), j=j))
            for j in range(n_pages)]


def _slot_rows(ref, slot, n_tok):
    return ref[0, pl.ds(slot, n_tok, stride=SLOTS), :]


def _diff_attn_sample(qa, ka, va, cache_k, cache_v, page_table, lam_w, sub_w, lam_init):
    n, s, _ = qa.shape
    n_pages = page_table.shape[1]
    krows = PAGE_SIZE * 2 * H_A
    kern = functools.partial(_diff_sample_kernel, n_pages=n_pages, s=s, lam_init=lam_init)
    row_spec = pl.BlockSpec((1, s, 512), lambda b, pt: (b, 0, 0))
    grid_spec = pltpu.PrefetchScalarGridSpec(
        num_scalar_prefetch=1,
        grid=(n,),
        in_specs=_page_specs(n_pages, krows, DK_A) + _page_specs(n_pages, PAGE_SIZE * SLOTS, LANES)
        + [row_spec, row_spec, row_spec,
           pl.BlockSpec(lam_w.shape, lambda b, pt: (0, 0)), pl.BlockSpec(sub_w.shape, lambda b, pt: (0, 0))],
        out_specs=row_spec,
        scratch_shapes=[pltpu.VMEM((n_pages * krows, LANES), F32)],
    )
    return pl.pallas_call(
        kern,
        grid_spec=grid_spec,
        out_shape=jax.ShapeDtypeStruct((n, s, 512), F32),
        compiler_params=_cparams(("arbitrary",)),
        name="diff_attn_sample",
    )(page_table, *([cache_k] * n_pages), *([cache_v] * n_pages), qa, ka, va, lam_w, sub_w)


def _sel_sample_kernel(pt_ref, *refs, n_pages, s):
    pg_refs = refs[:n_pages]
    q_ref, selm_ref, new_ref, o_ref = refs[n_pages:]
    past = n_pages * PAGE_SIZE
    new_mask = _new_token_mask(R_B * s, s)
    oh_past = _sel_onehot(past, 0)
    new_blk = past // SEL_BLOCK
    oh_new = jnp.where((_iota((LANES, LANES), 1) == new_blk) & (_iota((LANES, LANES), 0) < s), 1.0, 0.0)
    oh_new = oh_new.astype(BF16)
    scale = DH_B ** -0.5
    for g in range(G_B):
        kcols = slice(g * LANES, (g + 1) * LANES)
        vcols = slice((G_B + g) * LANES, (G_B + g + 1) * LANES)
        selm = selm_ref[0, g]
        q2 = jnp.concatenate(
            [jnp.concatenate([(q_ref[0, :, (g * R_B + r) * LANES:(g * R_B + r + 1) * LANES] * scale).astype(BF16),
                              selm], axis=-1) for r in range(R_B)], axis=0)
        kpast = jnp.concatenate([_slot_rows(r, g, PAGE_SIZE).astype(BF16) for r in pg_refs], axis=0)
        vpast = jnp.concatenate([_slot_rows(r, G_B + g, PAGE_SIZE).astype(BF16) for r in pg_refs], axis=0)
        kpast = jnp.concatenate([kpast, oh_past], axis=-1)
        knew = jnp.concatenate([_pad_rows(new_ref[0, :, kcols], LANES).astype(BF16), oh_new], axis=-1)
        vnew = _pad_rows(new_ref[0, :, vcols], LANES).astype(BF16)
        o2 = _attend_pieces(q2, [(kpast, vpast, None), (knew, vnew, new_mask)])
        for r in range(R_B):
            o_ref[0, :, (g * R_B + r) * LANES:(g * R_B + r + 1) * LANES] = o2[r * s:(r + 1) * s]


def _sel_attn_sample(qb, selm, sel_new, cache, page_table):
    n, s, _ = qb.shape
    n_pages = page_table.shape[1]
    assert s <= SEL_BLOCK and (n_pages * PAGE_SIZE) % SEL_BLOCK == 0
    kern = functools.partial(_sel_sample_kernel, n_pages=n_pages, s=s)
    row_spec = pl.BlockSpec((1, s, 512), lambda b, pt: (b, 0, 0))
    grid_spec = pltpu.PrefetchScalarGridSpec(
        num_scalar_prefetch=1,
        grid=(n,),
        in_specs=_page_specs(n_pages, PAGE_SIZE * SLOTS, LANES)
        + [row_spec, pl.BlockSpec((1, G_B, s, LANES), lambda b, pt: (b, 0, 0, 0)), row_spec],
        out_specs=row_spec,
    )
    return pl.pallas_call(
        kern,
        grid_spec=grid_spec,
        out_shape=jax.ShapeDtypeStruct((n, s, 512), F32),
        compiler_params=_cparams(("parallel",)),
        name="nsa_sel_sample",
    )(page_table, *([cache] * n_pages), qb, selm, sel_new)


def _win_sample_kernel(q_ref, buf_ref, new_ref, o_ref, wout_ref, *, s, wb):
    new_mask = _new_token_mask(R_B * s, s)
    r = _iota((R_B * s, wb), 0) % s
    buf_mask = _iota((R_B * s, wb), 1) > r + (wb - WINDOW)
    scale = DH_B ** -0.5
    for g in range(G_B):
        kcols = slice(g * LANES, (g + 1) * LANES)
        vcols = slice((G_B + g) * LANES, (G_B + g + 1) * LANES)
        q2 = jnp.concatenate(
            [(q_ref[0, :, (g * R_B + r_) * LANES:(g * R_B + r_ + 1) * LANES] * scale).astype(BF16)
             for r_ in range(R_B)], axis=0)
        kbuf = _slot_rows(buf_ref, g, wb).astype(BF16)
        vbuf = _slot_rows(buf_ref, G_B + g, wb).astype(BF16)
        knew = _pad_rows(new_ref[0, :, kcols], LANES).astype(BF16)
        vnew = _pad_rows(new_ref[0, :, vcols], LANES).astype(BF16)
        o2 = _attend_pieces(q2, [(kbuf, vbuf, buf_mask), (knew, vnew, new_mask)])
        for r_ in range(R_B):
            o_ref[0, :, (g * R_B + r_) * LANES:(g * R_B + r_ + 1) * LANES] = o2[r_ * s:(r_ + 1) * s]
    keep = (wb - s) * SLOTS
    wout_ref[0, 0:keep] = buf_ref[0, s * SLOTS:wb * SLOTS]
    for slot in range(SLOTS):
        wout_ref[0, pl.ds(keep + slot, s, stride=SLOTS), :] = new_ref[0, :, slot * LANES:(slot + 1) * LANES]


def _win_attn_sample(qb, win_buf, win_new, past):
    n, s, _ = qb.shape
    wb = win_buf.shape[1] // SLOTS
    assert past >= wb and wb % 8 == 0 and s % 8 == 0
    kern = functools.partial(_win_sample_kernel, s=s, wb=wb)
    row_spec = pl.BlockSpec((1, s, 512), lambda b: (b, 0, 0))
    buf_spec = pl.BlockSpec((1, wb * SLOTS, LANES), lambda b: (b, 0, 0))
    return pl.pallas_call(
        kern,
        grid=(n,),
        in_specs=[row_spec, buf_spec, row_spec],
        out_specs=[row_spec, buf_spec],
        out_shape=[jax.ShapeDtypeStruct((n, s, 512), F32), jax.ShapeDtypeStruct((n, wb * SLOTS, LANES), F32)],
        compiler_params=_cparams(("parallel",)),
        name="nsa_win_sample",
    )(qb, win_buf, win_new)


def _even_out_kernel(y_ref, g1_ref, oa_ref, oc_ref, os_ref, ow_ref, gt_ref, w_ref, o_ref):
    gt = gt_ref[...]
    parts = [oa_ref[...].astype(BF16)]
    for hb in range(H_B):
        sl = slice(hb * LANES, (hb + 1) * LANES)
        ob = (gt[:, 3 * hb:3 * hb + 1] * oc_ref[:, sl] + gt[:, 3 * hb + 1:3 * hb + 2] * os_ref[:, sl]
              + gt[:, 3 * hb + 2:3 * hb + 3] * ow_ref[:, sl])
        parts.append(ob.astype(BF16))
    out = _dot(jnp.concatenate(parts, axis=-1), w_ref[...])
    y = y_ref[...]
    o_ref[...] = y + g1_ref[0] * out.reshape(y.shape)


def _even_out(til, y, mods, oa, oc, os_, ow, gt, w_out):
    return pl.pallas_call(
        _even_out_kernel,
        grid=til.grid,
        in_specs=[til.x_spec(D_MODEL), til.mod_spec(2), til.flat_spec(512), til.flat_spec(512),
                  til.flat_spec(512), til.flat_spec(512), til.flat_spec(LANES), _full_spec(w_out.shape)],
        out_specs=til.x_spec(D_MODEL),
        out_shape=jax.ShapeDtypeStruct(y.shape, F32),
        compiler_params=_cparams(("parallel", "parallel")),
        name="even_out",
    )(y, mods, oa, oc, os_, ow, gt, w_out)


def _mlp_kernel(y_ref, sh_ref, sc_ref, g_ref, nw_ref, w1_ref, w2_ref, fw_ref, o_ref, *, final):
    y = y_ref[...]
    h = _norm_mod(y, nw_ref[...], sh_ref[0], sc_ref[0]).reshape(-1, D_MODEL).astype(BF16)
    a = jnp.maximum(_dot(h, w1_ref[...]), 0.0)
    out = _dot((a * a).astype(BF16), w2_ref[...])
    y2 = y + g_ref[0] * out.reshape(y.shape)
    if final:
        y2 = _rms(y2, fw_ref[...])
    o_ref[...] = y2


def _mlp(til, y, mods, nw, w1, w2, fw, final):
    kern = functools.partial(_mlp_kernel, final=final)
    return pl.pallas_call(
        kern,
        grid=til.grid,
        in_specs=[til.x_spec(D_MODEL), til.mod_spec(3), til.mod_spec(4), til.mod_spec(5),
                  _full_spec((1, D_MODEL)), _full_spec(w1.shape), _full_spec(w2.shape), _full_spec((1, D_MODEL))],
        out_specs=til.x_spec(D_MODEL),
        out_shape=jax.ShapeDtypeStruct(y.shape, F32),
        compiler_params=_cparams(("parallel", "parallel")),
        name="mlp",
    )(y, mods, mods, mods, nw, w1, w2, fw)


def _gla_inproj_kernel(x_ref, sh_ref, sc_ref, nw_ref, w_ref, wgl_ref, wgate_ref, bgate_ref,
                       q_ref, k_ref, v_ref, r_ref, la_ref):
    h = _norm_mod(x_ref[...], nw_ref[...], sh_ref[0], sc_ref[0]).reshape(-1, D_MODEL).astype(BF16)
    nk = H_C * DK_C
    nv = H_C * DV_C
    q_ref[...] = _dot(h, w_ref[:, 0:nk]) * (DK_C ** -0.5)
    k_ref[...] = _dot(h, w_ref[:, nk:2 * nk])
    v_ref[...] = _dot(h, w_ref[:, 2 * nk:2 * nk + nv])
    r_ref[...] = _dot(h, w_ref[:, 2 * nk + nv:2 * nk + 2 * nv])
    gl = _dot(h, wgl_ref[...])
    x = _dot(gl.astype(BF16), wgate_ref[...]) + bgate_ref[...]
    log_sig = jnp.minimum(x, 0.0) - jnp.log1p(jnp.exp(-jnp.abs(x)))
    la_ref[...] = log_sig / GATE_TAU


def _gla_inproj(til, x, mods, nw, w_main, w_gl, w_gate, b_gate):
    rows_total = til.nb * til.r
    widths = (H_C * DK_C, H_C * DK_C, H_C * DV_C, H_C * DV_C, H_C * DK_C)
    return pl.pallas_call(
        _gla_inproj_kernel,
        grid=til.grid,
        in_specs=[til.x_spec(D_MODEL), til.mod_spec(0), til.mod_spec(1), _full_spec((1, D_MODEL)),
                  _full_spec(w_main.shape), _full_spec(w_gl.shape), _full_spec(w_gate.shape),
                  _full_spec(b_gate.shape)],
        out_specs=[til.flat_spec(c) for c in widths],
        out_shape=[jax.ShapeDtypeStruct((rows_total, c), F32) for c in widths],
        compiler_params=_cparams(("parallel", "parallel")),
        name="gla_inproj",
    )(x, mods, mods, nw, w_main, w_gl, w_gate, b_gate)


def _gla_prep(q, k, g, c, sub, n_real):
    tri = jnp.where(_iota((c, c), 0) >= _iota((c, c), 1), 1.0, 0.0)
    b = jnp.dot(tri, g, preferred_element_type=F32, precision=lax.Precision.HIGHEST)
    qe = (q * jnp.exp(b)).astype(BF16)
    lane_c = _iota((sub, c), 1)
    row_s = _iota((sub, c), 0)
    att_rows = []
    for blk in range(c // sub):
        lo = blk * sub
        qi, ki, bi = q[lo:lo + sub], k[lo:lo + sub], b[lo:lo + sub]
        diag = jnp.zeros((sub, c), F32)
        for jj in range(min(sub, max(n_real - lo, 0))):
            e = jnp.exp(jnp.minimum(bi - bi[jj:jj + 1], 0.0))
            col = jnp.sum(qi * ki[jj:jj + 1] * e, axis=-1, keepdims=True)
            diag = jnp.where(lane_c == lo + jj, col, diag)
        att = jnp.where(lane_c - lo <= row_s, diag, 0.0)
        if blk > 0:
            bs = b[lo - 1:lo]
            q_in = qi * jnp.exp(bi - bs)
            k_out = k * jnp.exp(jnp.minimum(bs - b, 0.0))
            att = jnp.where(lane_c < lo, _dot_nt(q_in.astype(BF16), k_out.astype(BF16)), att)
        att_rows.append(att)
    att = att_rows[0] if len(att_rows) == 1 else jnp.concatenate(att_rows, axis=0)
    bl = b[c - 1:c]
    kd = (k * jnp.exp(bl - b)).astype(BF16)
    eye = _iota((DK_C, DK_C), 0) == _iota((DK_C, DK_C), 1)
    decay = jnp.sum(jnp.where(eye, jnp.exp(bl), 0.0), axis=-1, keepdims=True)
    return qe, att, kd, decay


def _gla_apply(state, prep, v):
    qe, att, kd, decay = prep
    o = _dot(qe, state.astype(BF16)) + _dot(att, v)
    return o, decay * state + _dot_tn(kd, v.astype(BF16))


def _gla_rec_kernel(*refs, tt, c, sub, hp, has_s0):
    if has_s0:
        q_ref, k_ref, v_ref, g_ref, s0_ref, o_ref, sfin_ref, s_ref = refs
    else:
        q_ref, k_ref, v_ref, g_ref, o_ref, sfin_ref, s_ref = refs
    t = pl.program_id(2)

    @pl.when(t == 0)
    def _():
        s_ref[...] = s0_ref[0] if has_s0 else jnp.zeros(s_ref.shape, F32)

    def kcols(hh):
        return slice(hh * DK_C, (hh + 1) * DK_C)

    def vcols(hh):
        return slice(hh * DV_C, (hh + 1) * DV_C)

    if tt < c:
        pad = lambda x: _pad_rows(x, c)
        for hh in range(hp):
            prep = _gla_prep(pad(q_ref[0, :, kcols(hh)]), pad(k_ref[0, :, kcols(hh)]),
                             pad(g_ref[0, :, kcols(hh)]), c, sub, tt)
            o, s_ref[hh] = _gla_apply(s_ref[hh], prep, pad(v_ref[0, :, vcols(hh)]))
            o_ref[0, :, vcols(hh)] = o[:tt]
    else:
        per_trip = 2 if (tt // c) % 2 == 0 else 1

        def body(ci, carry):
            for hh in range(hp):
                rows = [pl.ds(pl.multiple_of((ci * per_trip + u) * c, c), c) for u in range(per_trip)]
                preps = [_gla_prep(q_ref[0, r, kcols(hh)], k_ref[0, r, kcols(hh)], g_ref[0, r, kcols(hh)],
                                   c, sub, c) for r in rows]
                state = s_ref[hh]
                for r, prep in zip(rows, preps):
                    o_ref[0, r, vcols(hh)], state = _gla_apply(state, prep, v_ref[0, r, vcols(hh)])
                s_ref[hh] = state
            return carry
        lax.fori_loop(0, tt // c // per_trip, body, 0)

    @pl.when(t == pl.num_programs(2) - 1)
    def _():
        sfin_ref[0] = s_ref[...]


def _gla_recurrence(q, k, v, g, s0, tt, c, sub, hp):
    n, t, _ = q.shape
    kern = functools.partial(_gla_rec_kernel, tt=tt, c=c, sub=sub, hp=hp, has_s0=s0 is not None)
    kspec = pl.BlockSpec((1, tt, hp * DK_C), lambda b, h, i: (b, i, h))
    vspec = pl.BlockSpec((1, tt, hp * DV_C), lambda b, h, i: (b, i, h))
    sspec = pl.BlockSpec((1, hp, DK_C, DV_C), lambda b, h, i: (b, h, 0, 0))
    in_specs = [kspec, kspec, vspec, kspec]
    args = [q, k, v, g]
    if s0 is not None:
        in_specs.append(sspec)
        args.append(s0)
    return pl.pallas_call(
        kern,
        grid=(n, H_C // hp, t // tt),
        in_specs=in_specs,
        out_specs=[vspec, sspec],
        out_shape=[jax.ShapeDtypeStruct((n, t, H_C * DV_C), F32),
                   jax.ShapeDtypeStruct((n, H_C, DK_C, DV_C), F32)],
        scratch_shapes=[pltpu.VMEM((hp, DK_C, DV_C), F32)],
        compiler_params=_cparams(("parallel", "parallel", "arbitrary")),
        name="gla_recurrence",
    )(*args)


def _gla_out_kernel(y_ref, g1_ref, o_ref_in, r_ref, nw_ref, w_ref, out_ref):
    parts = []
    for h in range(H_C):
        sl = slice(h * DV_C, (h + 1) * DV_C)
        r = r_ref[:, sl]
        parts.append((_rms(o_ref_in[:, sl], nw_ref[...]) * (r * jax.nn.sigmoid(r))).astype(BF16))
    out = _dot(jnp.concatenate(parts, axis=-1), w_ref[...])
    y = y_ref[...]
    out_ref[...] = y + g1_ref[0] * out.reshape(y.shape)


def _gla_out(til, y, mods, o, r, nw, w_out):
    return pl.pallas_call(
        _gla_out_kernel,
        grid=til.grid,
        in_specs=[til.x_spec(D_MODEL), til.mod_spec(2), til.flat_spec(H_C * DV_C), til.flat_spec(H_C * DV_C),
                  _full_spec(nw.shape), _full_spec(w_out.shape)],
        out_specs=til.x_spec(D_MODEL),
        out_shape=jax.ShapeDtypeStruct(y.shape, F32),
        compiler_params=_cparams(("parallel", "parallel")),
        name="gla_out",
    )(y, mods, o, r, nw, w_out)


def _rope_tables(pos, d):
    inv = ROPE_THETA ** (-jnp.arange(0, d, 2, dtype=F32) / d)
    ang = pos.astype(F32)[:, None] * inv[None, :]
    cos, sin = jnp.cos(ang), jnp.sin(ang)
    rep = LANES // d
    c = jnp.tile(jnp.concatenate([cos, cos], axis=-1), (1, rep))
    s = jnp.tile(jnp.concatenate([-sin, sin], axis=-1), (1, rep))
    return c, s


def _mods_for(mod_l, lo, hi):
    nb = hi - lo
    return mod_l[lo:hi].reshape(nb, 6, D_MODEL).transpose(1, 0, 2).reshape(6, nb, 1, D_MODEL)


def _prompt_tile_rows(t):
    return math.gcd(t, 256)


def _even_layer(yp, ys, mods_p, mods_s, caches, page_table, wts, lam_init, til_p, til_s):
    (c_dk, c_dv, c_cmp, c_sel, win_buf) = caches
    n, t, _ = yp.shape
    ns, s, _ = ys.shape
    n_pages = page_table.shape[1]
    past = n_pages * PAGE_SIZE
    w_in = wts["w_in"]
    n_main = w_in.shape[1] - 3 * H_B
    w_main = w_in[:, :n_main].astype(BF16)
    w_gate = jnp.pad(w_in[:, n_main:], ((0, 0), (0, LANES - 3 * H_B))).astype(BF16)
    nw1 = wts["norm1"].reshape(1, D_MODEL)
    lam_w = wts["lam_w"]
    sub_w = wts["sub_w"].reshape(1, DV_A)
    wc, pe, w2c = _compress_weights(wts["cmp_pe"], wts["cmp_w1"], wts["cmp_w2"])
    w_out = wts["w_out"].astype(BF16)

    tabs_p = _rope_tables(jnp.arange(t), DK_A) + _rope_tables(jnp.arange(t), DH_B)
    qa, ka, va, qb, cmp_kv, sel_kv, win_kv, gt = _even_inproj(til_p, yp, mods_p, nw1, w_main, w_gate, tabs_p)
    r3 = lambda a: a.reshape(n, t, a.shape[-1])
    tq = _prompt_tile_rows(t)
    oa = _diff_attn_prompt(r3(qa), r3(ka), r3(va), lam_w, sub_w, lam_init, tq)
    ccmp = _compress_prompt(r3(cmp_kv), wc, pe, w2c)
    n_cmp = (t - CMP_LEN) // CMP_STRIDE + 1
    n_sel = -(-t // SEL_BLOCK)
    o_cmp, selm = _cmp_topk(r3(qb), ccmp, nb=1, rq=tq, n_cmp=n_cmp, n_sel=n_sel, pos_base=0)
    o_sel = _nsa_attn_prompt(r3(qb), r3(sel_kv), selm, tq)
    o_win = _nsa_attn_prompt(r3(qb), r3(win_kv), None, tq)
    f2 = lambda a: a.reshape(n * t, a.shape[-1])
    yp = _even_out(til_p, yp, mods_p, f2(oa), f2(o_cmp), f2(o_sel), f2(o_win), gt, w_out)
    wl = min(WINDOW, t)
    st_p = (r3(ka).reshape(n, t, H_A, 2, DK_A), r3(va).reshape(n, t, H_A, DV_A),
            r3(cmp_kv).reshape(n, t, 2, G_B, DH_B), r3(sel_kv).reshape(n, t, 2, G_B, DH_B),
            r3(win_kv)[:, t - wl:].reshape(n, wl, 2, G_B, DH_B))

    pos_s = past + jnp.arange(s)
    tabs_s = tuple(jnp.tile(x, (til_s.b, 1)) for x in _rope_tables(pos_s, DK_A) + _rope_tables(pos_s, DH_B))
    qa, ka, va, qb, cmp_kv, sel_kv, win_kv, gt = _even_inproj(til_s, ys, mods_s, nw1, w_main, w_gate, tabs_s)
    r3 = lambda a: a.reshape(ns, s, a.shape[-1])
    slot_cache = lambda c: c.reshape(c.shape[0], PAGE_SIZE * SLOTS, LANES)
    oa = _diff_attn_sample(r3(qa), r3(ka), r3(va), c_dk.reshape(c_dk.shape[0], PAGE_SIZE * 2 * H_A, DK_A),
                           slot_cache(c_dv), page_table, lam_w, sub_w, lam_init)
    ccmp = _compress_sample(slot_cache(c_cmp), page_table, wc, pe, w2c)
    total = past + s
    n_cmp = (total - CMP_LEN) // CMP_STRIDE + 1
    n_sel = -(-total // SEL_BLOCK)
    assert n_cmp <= ccmp.shape[3] and n_sel <= LANES
    nb = math.gcd(ns, LANES // s)
    o_cmp, selm = _cmp_topk(r3(qb), ccmp, nb=nb, rq=s, n_cmp=n_cmp, n_sel=n_sel, pos_base=past)
    o_sel = _sel_attn_sample(r3(qb), selm, r3(sel_kv), slot_cache(c_sel), page_table)
    wb = win_buf.shape[1]
    o_win, win_out = _win_attn_sample(r3(qb), win_buf.reshape(ns, wb * SLOTS, LANES), r3(win_kv), past)
    f2 = lambda a: a.reshape(ns * s, a.shape[-1])
    ys = _even_out(til_s, ys, mods_s, f2(oa), f2(o_cmp), f2(o_sel), f2(o_win), gt, w_out)
    st_s = (r3(ka).reshape(ns, s, H_A, 2, DK_A), r3(va).reshape(ns, s, H_A, DV_A),
            r3(cmp_kv).reshape(ns, s, 2, G_B, DH_B), r3(sel_kv).reshape(ns, s, 2, G_B, DH_B),
            win_out.reshape(ns, wb, 2, G_B, DH_B))
    return yp, ys, st_p, st_s


def _odd_layer(yp, ys, mods_p, mods_s, s0, wts, til_p, til_s):
    w_in = wts["w_in"]
    n_main = w_in.shape[1] - GATE_RANK
    w_main = w_in[:, :n_main].astype(BF16)
    w_gl = jnp.pad(w_in[:, n_main:], ((0, 0), (0, LANES - GATE_RANK))).astype(BF16)
    w_gate = jnp.pad(wts["w_gate"], ((0, LANES - GATE_RANK), (0, 0))).astype(BF16)
    b_gate = wts["b_gate"].reshape(1, -1)
    nw1 = wts["norm1"].reshape(1, D_MODEL)
    gnw = wts["gnorm"].reshape(1, DV_C)
    w_out = wts["w_out"].astype(BF16)
    outs = []
    for y, mods, til, state in ((yp, mods_p, til_p, None), (ys, mods_s, til_s, s0)):
        n, t, _ = y.shape
        q, k, v, r, la = _gla_inproj(til, y, mods, nw1, w_main, w_gl, w_gate, b_gate)
        r3 = lambda a: a.reshape(n, t, a.shape[-1])
        c = math.gcd(t, GLA_CHUNK)
        if c >= GLA_SUB:
            tt, cc, hp = math.gcd(t, 8 * c), c, 1
        else:
            tt, cc, hp = t, GLA_SUB, H_C
        o, s_fin = _gla_recurrence(r3(q), r3(k), r3(v), r3(la), state, tt, cc, GLA_SUB, hp)
        y = _gla_out(til, y, mods, o.reshape(n * t, -1), r, gnw, w_out)
        outs.append((y, s_fin))
    return outs[0][0], outs[1][0], outs[0][1], outs[1][1]


def kernel(x_prompt, x_sample, c_prompt, c_sample, cache_diff_k, cache_diff_v, cache_cmp_kv, cache_sel_kv,
           state_win_kv, state_gla, page_table, norm1_w, norm2_w, ada_w, ada_b, even_w_in, even_w_out,
           diff_lambda_w, diff_subln_w, cmp_pe, cmp_w1, cmp_w2, gla_w_in, gla_w_gate, gla_b_gate, gla_norm_w,
           gla_w_out, mlp_w1, mlp_w2, final_norm_w):
    depth = ada_w.shape[0]
    n, t, _ = x_prompt.shape
    ns, s, _ = x_sample.shape
    til_p = _Tiling(n, t, 1, _prompt_tile_rows(t))
    til_s = _Tiling(ns, s, math.gcd(ns, 256 // s), s)

    pad = (-(n + ns)) % 8
    c_all = jnp.concatenate([c_prompt, c_sample, jnp.zeros((pad, D_MODEL), F32)], axis=0)
    mod = _adaln(c_all, ada_w, ada_b)

    yp, ys = x_prompt, x_sample
    st_p = [[] for _ in range(6)]
    st_s = [[] for _ in range(6)]
    fw = final_norm_w.reshape(1, D_MODEL)
    for l in range(depth):
        mods_p = _mods_for(mod[l], 0, n)
        mods_s = _mods_for(mod[l], n, n + ns)
        if l % 2 == 0:
            e = l // 2
            lam_init = 0.8 - 0.6 * math.exp(-0.3 * l)
            wts = dict(w_in=even_w_in[e], w_out=even_w_out[e], lam_w=diff_lambda_w[e], sub_w=diff_subln_w[e],
                       cmp_pe=cmp_pe[e], cmp_w1=cmp_w1[e], cmp_w2=cmp_w2[e], norm1=norm1_w[l])
            caches = (cache_diff_k[e], cache_diff_v[e], cache_cmp_kv[e], cache_sel_kv[e], state_win_kv[e])
            yp, ys, sp, ss = _even_layer(yp, ys, mods_p, mods_s, caches, page_table, wts, lam_init, til_p, til_s)
            for i in range(5):
                st_p[i].append(sp[i])
                st_s[i].append(ss[i])
        else:
            o = l // 2
            wts = dict(w_in=gla_w_in[o], w_gate=gla_w_gate[o], b_gate=gla_b_gate[o], gnorm=gla_norm_w[o],
                       w_out=gla_w_out[o], norm1=norm1_w[l])
            yp, ys, gp, gs = _odd_layer(yp, ys, mods_p, mods_s, state_gla[o], wts, til_p, til_s)
            st_p[5].append(gp)
            st_s[5].append(gs)
        final = l == depth - 1
        w1 = mlp_w1[l].astype(BF16)
        w2 = mlp_w2[l].astype(BF16)
        nw2 = norm2_w[l].reshape(1, D_MODEL)
        yp = _mlp(til_p, yp, mods_p, nw2, w1, w2, fw, final)
        ys = _mlp(til_s, ys, mods_s, nw2, w1, w2, fw, final)
    outs_p = [jnp.stack(x, axis=0) for x in st_p]
    outs_s = [jnp.stack(x, axis=0) for x in st_s]
    return (yp, ys, *outs_p, *outs_s)
```

```python
import functools
import math

import jax
import jax.numpy as jnp
from jax import lax
from jax.experimental import pallas as pl
from jax.experimental.pallas import tpu as pltpu

F32 = jnp.float32
BF16 = jnp.bfloat16

D_MODEL = 1024
PAGE_SIZE = 128
H_A = 4
DK_A = 64
DV_A = 128
H_B = 4
G_B = 2
R_B = 2
DH_B = 128
CMP_LEN = 32
CMP_STRIDE = 16
SEL_BLOCK = 64
TOP_N = 16
WINDOW = 512
H_C = 4
DK_C = 128
DV_C = 256
GATE_RANK = 16
GATE_TAU = 16.0
GLA_CHUNK = 64
GLA_SUB = 16
D_FF = 4 * D_MODEL
ROPE_THETA = 10000.0
EPS = 1e-6

LANES = 128
SLOTS = 4
MASK_NEG = -1e30
LOG2E = 1.4426950408889634
SAFE_SHIFT = 56.0
SEL_NEG = 32768.0
VMEM_LIMIT_MB = 56


def _cparams(sem, vmem_mb=VMEM_LIMIT_MB):
    return pltpu.CompilerParams(dimension_semantics=sem, vmem_limit_bytes=vmem_mb * 1024 * 1024)


def _dot(a, b):
    return jnp.dot(a, b, preferred_element_type=F32)


def _dot_nt(a, b):
    return lax.dot_general(a, b, (((1,), (1,)), ((), ())), preferred_element_type=F32)


def _dot_tn(a, b):
    return lax.dot_general(a, b, (((0,), (0,)), ((), ())), preferred_element_type=F32)


def _iota(shape, dim):
    return lax.broadcasted_iota(jnp.int32, shape, dim)


def _rms(x, w):
    ms = jnp.mean(x * x, axis=-1, keepdims=True)
    return x * lax.rsqrt(ms + EPS) * w


def _norm_mod(x, nw, shift, scale):
    return _rms(x, nw) * (1.0 + scale) + shift


class _Tiling:
    def __init__(self, nb, r, b, rt):
        assert nb % b == 0 and r % rt == 0 and (b == 1 or rt == r)
        self.nb, self.r, self.b, self.rt = nb, r, b, rt
        self.grid = (nb // b, r // rt)
        self.rows = b * rt
        self.nrb = r // rt

    def x_spec(self, d):
        return pl.BlockSpec((self.b, self.rt, d), lambda i, j: (i, j, 0))

    def mod_spec(self, k):
        return pl.BlockSpec((1, self.b, 1, D_MODEL), lambda i, j: (k, i, 0, 0))

    def flat_spec(self, c):
        nrb = self.nrb
        return pl.BlockSpec((self.rows, c), lambda i, j: (i * nrb + j, 0))

    def tab_spec(self):
        return pl.BlockSpec((self.rows, LANES), lambda i, j: (j, 0))


def _full_spec(shape):
    nd = len(shape)
    return pl.BlockSpec(shape, lambda *_: (0,) * nd)


def _adaln_kernel(c_ref, w_ref, b_ref, o_ref):
    c = c_ref[...]
    a = (c * jax.nn.sigmoid(c)).astype(BF16)
    o_ref[0] = _dot(a, w_ref[0].astype(BF16)) + b_ref[0]


def _adaln(c_all, ada_w, ada_b):
    depth, d, n6 = ada_w.shape
    rows = c_all.shape[0]
    tn = 1536
    return pl.pallas_call(
        _adaln_kernel,
        grid=(depth, n6 // tn),
        in_specs=[pl.BlockSpec((rows, d), lambda l, j: (0, 0)),
                  pl.BlockSpec((1, d, tn), lambda l, j: (l, 0, j)),
                  pl.BlockSpec((1, 1, tn), lambda l, j: (l, 0, j))],
        out_specs=pl.BlockSpec((1, rows, tn), lambda l, j: (l, 0, j)),
        out_shape=jax.ShapeDtypeStruct((depth, rows, n6), F32),
        compiler_params=_cparams(("parallel", "parallel")),
        name="adaln",
    )(c_all, ada_w, ada_b.reshape(depth, 1, n6))


def _swap_half(x, half):
    if 2 * half == LANES:
        return pltpu.roll(x, half, 1)
    lane = _iota((1, LANES), 1)
    lo = (lane % (2 * half)) < half
    return jnp.where(lo, pltpu.roll(x, LANES - half, 1), pltpu.roll(x, half, 1))


def _even_inproj_kernel(x_ref, sh_ref, sc_ref, nw_ref, w_ref, wg_ref, c64_ref, s64_ref, c128_ref, s128_ref,
                        qa_ref, ka_ref, va_ref, qb_ref, cmp_ref, sel_ref, win_ref, gt_ref):
    h = _norm_mod(x_ref[...], nw_ref[...], sh_ref[0], sc_ref[0])
    h = h.reshape(-1, D_MODEL).astype(BF16)
    c64, s64, c128, s128 = c64_ref[...], s64_ref[...], c128_ref[...], s128_ref[...]

    def rope64(p):
        return p * c64 + _swap_half(p, DK_A // 2) * s64

    def rope128(p):
        return p * c128 + _swap_half(p, DH_B // 2) * s128

    def project(ref, off, ropes):
        p = _dot(h, w_ref[:, off:off + 4 * LANES])
        for j, rope in enumerate(ropes):
            sl = slice(j * LANES, (j + 1) * LANES)
            ref[:, sl] = p[:, sl] if rope is None else rope(p[:, sl])

    project(qa_ref, 0, [rope64] * 4)
    project(ka_ref, 512, [rope64] * 4)
    project(va_ref, 1024, [None] * 4)
    project(qb_ref, 1536, [rope128] * 4)
    for t, ref in enumerate((cmp_ref, sel_ref, win_ref)):
        project(ref, 2048 + t * 512, [rope128, rope128, None, None])
    gt_ref[...] = jax.nn.sigmoid(_dot(h, wg_ref[...]))


def _even_inproj(til, x, mods, nw, w_main, w_gate, tabs):
    rows_total = til.nb * til.r
    widths = (512, 512, 512, 512, 512, 512, 512, LANES)
    return pl.pallas_call(
        _even_inproj_kernel,
        grid=til.grid,
        in_specs=[til.x_spec(D_MODEL), til.mod_spec(0), til.mod_spec(1), _full_spec((1, D_MODEL)),
                  _full_spec(w_main.shape), _full_spec(w_gate.shape)] + [til.tab_spec()] * 4,
        out_specs=[til.flat_spec(c) for c in widths],
        out_shape=[jax.ShapeDtypeStruct((rows_total, c), F32) for c in widths],
        compiler_params=_cparams(("parallel", "parallel")),
        name="even_inproj",
    )(x, mods, mods, nw, w_main, w_gate, *tabs)


def _for_tiles(lo, hi, fn):
    n = hi - lo

    def pair(j, carry):
        fn(lo + 2 * j)
        fn(lo + 2 * j + 1)
        return carry

    lax.fori_loop(0, n // 2, pair, 0)

    @pl.when(n % 2 == 1)
    def _():
        fn(hi - 1)


def _row_sumsq(x):
    xf = x.astype(F32)
    return _dot((xf * xf).astype(BF16), jnp.ones((LANES, LANES), BF16))


def _key_norm_bound(k_ref, n_keys, tile):
    def body(kb, m):
        k = k_ref[0, pl.ds(pl.multiple_of(kb * tile, tile), tile), :].astype(BF16)
        return jnp.maximum(m, _row_sumsq(k))

    m = lax.fori_loop(0, n_keys // tile, body, jnp.zeros((tile, LANES), F32))
    return jnp.max(m, axis=0, keepdims=True)


def _score_bound(q2, kmax2):
    return jnp.sqrt(_row_sumsq(q2) * kmax2) * 1.05


def _two_pass_attention(mx_ref, acc_ref, lo, hi, last, scores, values, mask_body=False, bound=None):
    def lane_max(s):
        m = s[:, 0:LANES]
        for c in range(1, s.shape[1] // LANES):
            m = jnp.maximum(m, s[:, c * LANES:(c + 1) * LANES])
        return m

    def pass1(kb, masked):
        mx_ref[...] = jnp.maximum(mx_ref[...], lane_max(scores(kb, masked)))

    def exact_max():
        mx_ref[...] = jnp.full(mx_ref.shape, MASK_NEG, F32)
        _for_tiles(lo, hi, lambda kb: pass1(kb, mask_body))
        pass1(last, True)
        mx_ref[...] = jnp.broadcast_to(jnp.max(mx_ref[...], axis=-1, keepdims=True), mx_ref.shape)

    if bound is None:
        exact_max()
    else:
        safe = jnp.max(bound) <= SAFE_SHIFT

        @pl.when(safe)
        def _():
            mx_ref[...] = bound

        @pl.when(jnp.logical_not(safe))
        def _():
            exact_max()

    acc_ref[...] = jnp.zeros(acc_ref.shape, F32)

    def pass2(kb, masked):
        s = scores(kb, masked)
        m = mx_ref[...]
        p = jnp.concatenate([jnp.exp2(s[:, c * LANES:(c + 1) * LANES] - m) for c in range(s.shape[1] // LANES)],
                            axis=-1).astype(BF16)
        v = values(kb)
        v1 = jnp.concatenate([v, jnp.ones(v.shape, BF16)], axis=-1)
        acc_ref[...] += _dot(p, v1)

    _for_tiles(lo, hi, lambda kb: pass2(kb, mask_body))
    pass2(last, True)
    return acc_ref[:, 0:LANES] / acc_ref[:, LANES:2 * LANES]


def _diff_lambda(lw, lam_init):
    a = jnp.sum(lw[0:1] * lw[1:2], axis=-1, keepdims=True)
    b = jnp.sum(lw[2:3] * lw[3:4], axis=-1, keepdims=True)
    return jnp.exp(a) - jnp.exp(b) + lam_init


def _diff_finalize(o2, tq, lam, sub_w, lam_init):
    od = o2[:tq] - lam * o2[tq:]
    return _rms(od, sub_w) * (1.0 - lam_init)


def _diff_flash_kernel(q_ref, k_ref, v_ref, lamw_ref, sub_ref, o_ref, q2_ref, mx_ref, acc_ref, kmax_ref, *,
                       tq, lam_init):
    i = pl.program_id(2)

    @pl.when(i == 0)
    def _():
        kmax_ref[...] = jnp.broadcast_to(_key_norm_bound(k_ref, k_ref.shape[1], tq), kmax_ref.shape)

    q = q_ref[0] * (DK_A ** -0.5 * LOG2E)
    lane = _iota((1, LANES), 1)
    q2_ref[0:tq] = jnp.where(lane < DK_A, q, 0.0).astype(BF16)
    q2_ref[tq:2 * tq] = jnp.where(lane >= DK_A, q, 0.0).astype(BF16)
    bound = _score_bound(q2_ref[...], kmax_ref[0:1])

    def scores(kb, masked):
        k = k_ref[0, pl.ds(pl.multiple_of(kb * tq, tq), tq), :].astype(BF16)
        s = _dot_nt(q2_ref[...], k)
        if masked:
            r = _iota((2 * tq, tq), 0)
            r = jnp.where(r >= tq, r - tq, r)
            s = jnp.where(_iota((2 * tq, tq), 1) <= r, s, MASK_NEG)
        return s

    def values(kb):
        return v_ref[0, pl.ds(pl.multiple_of(kb * tq, tq), tq), :].astype(BF16)

    o2 = _two_pass_attention(mx_ref, acc_ref, 0, i, i, scores, values, bound=bound)
    o_ref[0] = _diff_finalize(o2, tq, _diff_lambda(lamw_ref[...], lam_init), sub_ref[...], lam_init)


def _diff_attn_prompt(qa, ka, va, lam_w, sub_w, lam_init, tq):
    n, t, _ = qa.shape
    kern = functools.partial(_diff_flash_kernel, tq=tq, lam_init=lam_init)
    return pl.pallas_call(
        kern,
        grid=(n, H_A, t // tq),
        in_specs=[pl.BlockSpec((1, tq, LANES), lambda b, h, i: (b, i, h)),
                  pl.BlockSpec((1, t, LANES), lambda b, h, i: (b, 0, h)),
                  pl.BlockSpec((1, t, LANES), lambda b, h, i: (b, 0, h)),
                  _full_spec(lam_w.shape), _full_spec(sub_w.shape)],
        out_specs=pl.BlockSpec((1, tq, LANES), lambda b, h, i: (b, i, h)),
        out_shape=jax.ShapeDtypeStruct((n, t, H_A * DV_A), F32),
        scratch_shapes=[pltpu.VMEM((2 * tq, LANES), BF16), pltpu.VMEM((2 * tq, LANES), F32),
                        pltpu.VMEM((2 * tq, 2 * LANES), F32), pltpu.VMEM((8, LANES), F32)],
        compiler_params=_cparams(("parallel", "parallel", "arbitrary")),
        name="diff_attn_prompt",
    )(qa, ka, va, lam_w, sub_w)


def _sel_onehot(rows, first_block):
    blk = _iota((rows, LANES), 0) // SEL_BLOCK + first_block
    return jnp.where(blk == _iota((rows, LANES), 1), 1.0, 0.0).astype(BF16)


def _nsa_flash_kernel(*refs, tq, use_sel):
    if use_sel:
        q_ref, selm_ref, k_ref, v_ref, o_ref, q2_ref, mx_ref, acc_ref, kmax_ref = refs
    else:
        q_ref, k_ref, v_ref, o_ref, q2_ref, mx_ref, acc_ref, kmax_ref = refs
    i = pl.program_id(2)

    @pl.when(i == 0)
    def _():
        kmax_ref[...] = jnp.broadcast_to(_key_norm_bound(k_ref, k_ref.shape[1], tq), kmax_ref.shape)

    q = q_ref[0] * (DH_B ** -0.5 * LOG2E)
    for r in range(R_B):
        q2_ref[r * tq:(r + 1) * tq, 0:LANES] = q[:, r * LANES:(r + 1) * LANES].astype(BF16)
        if use_sel:
            q2_ref[r * tq:(r + 1) * tq, LANES:2 * LANES] = selm_ref[0, 0]
    bound = _score_bound(q2_ref[:, 0:LANES], kmax_ref[0:1])

    def scores(kb, masked):
        k = k_ref[0, pl.ds(pl.multiple_of(kb * tq, tq), tq), :].astype(BF16)
        if use_sel:
            k = jnp.concatenate([k, _sel_onehot(tq, kb * (tq // SEL_BLOCK))], axis=-1)
        s = _dot_nt(q2_ref[...], k)
        if masked:
            r = _iota((R_B * tq, tq), 0)
            qp = i * tq + jnp.where(r >= tq, r - tq, r)
            kp = kb * tq + _iota((R_B * tq, tq), 1)
            ok = kp <= qp
            if not use_sel:
                ok = ok & (kp > qp - WINDOW)
            s = jnp.where(ok, s, MASK_NEG)
        return s

    def values(kb):
        return v_ref[0, pl.ds(pl.multiple_of(kb * tq, tq), tq), :].astype(BF16)

    lo = 0 if use_sel else jnp.maximum(i - WINDOW // tq, 0)
    o = _two_pass_attention(mx_ref, acc_ref, lo, i, i, scores, values, mask_body=not use_sel, bound=bound)
    for r in range(R_B):
        o_ref[0, :, r * LANES:(r + 1) * LANES] = o[r * tq:(r + 1) * tq]


def _nsa_attn_prompt(qb, kv, selm, tq):
    n, t, _ = qb.shape
    use_sel = selm is not None
    kd = 2 * LANES if use_sel else LANES
    kern = functools.partial(_nsa_flash_kernel, tq=tq, use_sel=use_sel)
    in_specs = [pl.BlockSpec((1, tq, R_B * LANES), lambda b, g, i: (b, i, g))]
    args = [qb]
    if use_sel:
        in_specs.append(pl.BlockSpec((1, 1, tq, LANES), lambda b, g, i: (b, g, i, 0)))
        args.append(selm)
    in_specs += [pl.BlockSpec((1, t, LANES), lambda b, g, i: (b, 0, g)),
                 pl.BlockSpec((1, t, LANES), lambda b, g, i: (b, 0, G_B + g))]
    args += [kv, kv]
    return pl.pallas_call(
        kern,
        grid=(n, G_B, t // tq),
        in_specs=in_specs,
        out_specs=pl.BlockSpec((1, tq, R_B * LANES), lambda b, g, i: (b, i, g)),
        out_shape=jax.ShapeDtypeStruct((n, t, H_B * DH_B), F32),
        scratch_shapes=[pltpu.VMEM((R_B * tq, kd), BF16), pltpu.VMEM((R_B * tq, LANES), F32),
                        pltpu.VMEM((R_B * tq, 2 * LANES), F32), pltpu.VMEM((8, LANES), F32)],
        compiler_params=_cparams(("parallel", "parallel", "arbitrary")),
        name="nsa_sel_prompt" if use_sel else "nsa_win_prompt",
    )(*args)


def _compress_core(load, wc_ref, pe_ref, w2_ref, wi, nchunk):
    rows = G_B * nchunk
    acc = jnp.zeros((rows, 2 * LANES), F32)
    pew = jnp.zeros((16, 2 * LANES), F32)
    for u in range(CMP_STRIDE // 2):
        lhs = jnp.concatenate([load(2 * u), load(2 * u + 1)], axis=-1).astype(BF16)
        w = wc_ref[wi, u]
        acc = acc + _dot(lhs, w)
        pew = pew + _dot(pe_ref[wi, u], w)
    first = acc[:, :LANES]
    second = pltpu.roll(acc[:, LANES:], rows - 1, 0)
    hid = first + second + pew[0:1, :LANES] + pew[8:9, LANES:]
    hid = hid * jax.nn.sigmoid(hid)
    return _dot(hid.astype(BF16), w2_ref[wi])


def _compress_prompt_kernel(*refs, nchunk):
    x_refs = refs[:G_B]
    wc_ref, pe_ref, w2_ref, o_ref = refs[G_B:]

    def load(tok):
        return jnp.concatenate([xr[0, pl.ds(tok, nchunk, stride=CMP_STRIDE), :] for xr in x_refs], axis=0)

    out = _compress_core(load, wc_ref, pe_ref, w2_ref, 0, nchunk)
    for g in range(G_B):
        o_ref[0, 0, g] = out[g * nchunk:(g + 1) * nchunk]


def _compress_sample_kernel(pt_ref, *refs, n_pages, spb):
    pg_refs = refs[:spb * n_pages]
    wc_ref, pe_ref, w2_ref, o_ref = refs[spb * n_pages:]
    cps = PAGE_SIZE // CMP_STRIDE
    nchunk = n_pages * cps
    for kv in range(2):
        def load(tok):
            return jnp.concatenate(
                [pg[0, pl.ds(tok * SLOTS + kv * G_B + g, cps, stride=CMP_STRIDE * SLOTS), :]
                 for smp in range(spb) for g in range(G_B) for pg in pg_refs[smp * n_pages:(smp + 1) * n_pages]],
                axis=0)

        out = _compress_core(load, wc_ref, pe_ref, w2_ref, kv, spb * nchunk)
        for smp in range(spb):
            for g in range(G_B):
                seg = smp * G_B + g
                o_ref[smp, kv, g] = out[seg * nchunk:(seg + 1) * nchunk]


def _compress_weights(cmp_pe, cmp_w1, cmp_w2):
    w1 = cmp_w1.reshape(2, CMP_LEN, DH_B, DH_B)
    wab = jnp.concatenate([w1[:, :CMP_STRIDE], w1[:, CMP_STRIDE:]], axis=-1)
    wc = wab.reshape(2, CMP_STRIDE // 2, 2 * DH_B, 2 * DH_B).astype(BF16)
    pa = cmp_pe[:, :CMP_STRIDE].reshape(2, CMP_STRIDE // 2, 1, 2 * DH_B)
    pb = cmp_pe[:, CMP_STRIDE:].reshape(2, CMP_STRIDE // 2, 1, 2 * DH_B)
    z = jnp.zeros((2, CMP_STRIDE // 2, 7, 2 * DH_B), F32)
    pe = jnp.concatenate([pa, z, pb, z], axis=2).astype(BF16)
    return wc, pe, cmp_w2.astype(BF16)


def _compress_prompt(cmp_kv, wc, pe, w2):
    n, t, _ = cmp_kv.shape
    nchunk = t // CMP_STRIDE
    kern = functools.partial(_compress_prompt_kernel, nchunk=nchunk)
    return pl.pallas_call(
        kern,
        grid=(n, 2),
        in_specs=[pl.BlockSpec((1, t, DH_B), functools.partial(lambda b, kv, g: (b, 0, kv * G_B + g), g=g))
                  for g in range(G_B)] + [
                  pl.BlockSpec((1,) + wc.shape[1:], lambda b, kv: (kv, 0, 0, 0)),
                  pl.BlockSpec((1,) + pe.shape[1:], lambda b, kv: (kv, 0, 0, 0)),
                  pl.BlockSpec((1, DH_B, DH_B), lambda b, kv: (kv, 0, 0))],
        out_specs=pl.BlockSpec((1, 1, G_B, nchunk, DH_B), lambda b, kv: (b, kv, 0, 0, 0)),
        out_shape=jax.ShapeDtypeStruct((n, 2, G_B, nchunk, DH_B), F32),
        compiler_params=_cparams(("parallel", "parallel")),
        name="compress_prompt",
    )(*([cmp_kv] * G_B), wc, pe, w2)


def _compress_sample(cache, page_table, wc, pe, w2):
    n, n_pages = page_table.shape
    nchunk = n_pages * (PAGE_SIZE // CMP_STRIDE)
    spb = math.gcd(n, 2)
    kern = functools.partial(_compress_sample_kernel, n_pages=n_pages, spb=spb)
    page_specs = [pl.BlockSpec((1, PAGE_SIZE * SLOTS, LANES),
                               functools.partial(lambda b, pt, smp, j: (pt[b * spb + smp, j], 0, 0), smp=smp, j=j))
                  for smp in range(spb) for j in range(n_pages)]
    grid_spec = pltpu.PrefetchScalarGridSpec(
        num_scalar_prefetch=1,
        grid=(n // spb,),
        in_specs=page_specs + [
            pl.BlockSpec(wc.shape, lambda b, pt: (0, 0, 0, 0)),
            pl.BlockSpec(pe.shape, lambda b, pt: (0, 0, 0, 0)),
            pl.BlockSpec(w2.shape, lambda b, pt: (0, 0, 0))],
        out_specs=pl.BlockSpec((spb, 2, G_B, nchunk, DH_B), lambda b, pt: (b, 0, 0, 0, 0)),
    )
    return pl.pallas_call(
        kern,
        grid_spec=grid_spec,
        out_shape=jax.ShapeDtypeStruct((n, 2, G_B, nchunk, DH_B), F32),
        compiler_params=_cparams(("parallel",)),
        name="compress_sample",
    )(page_table, *([cache] * (spb * n_pages)), wc, pe, w2)


def _cmp_topk_kernel(q_ref, kc_ref, vc_ref, ocmp_ref, selm_ref, p_scr, v_scr, *,
                     nb, rq, nck, n_cmp, n_sel, nsp, pos_base):
    tq = nb * rq
    j = pl.program_id(2)
    scale = DH_B ** -0.5
    kidx = _iota((1, nck), 1)
    qpos_c = pos_base + j * rq + _iota((rq, 1), 0)
    valid = (kidx < n_cmp) & (CMP_STRIDE * kidx + (CMP_LEN - 1) <= qpos_c)
    for b in range(nb):
        kcc = kc_ref[b, 0, 0].astype(BF16)
        vcc = vc_ref[b, 0, 0].astype(BF16)
        q = q_ref[b] * scale
        psum = jnp.zeros((rq, nck), F32)
        for r in range(R_B):
            s = _dot_nt(q[:, r * LANES:(r + 1) * LANES].astype(BF16), kcc)
            s = jnp.where(valid, s, MASK_NEG)
            m = jnp.max(s, axis=-1, keepdims=True)
            p = jnp.where(valid, jnp.exp(s - m), 0.0)
            p = p / jnp.maximum(jnp.sum(p, axis=-1, keepdims=True), 1e-30)
            ocmp_ref[b, :, r * LANES:(r + 1) * LANES] = _dot(p.astype(BF16), vcc)
            psum = psum + p
        p_scr[b * rq:(b + 1) * rq, :] = psum
    psum = p_scr[...]
    p_hi = psum.astype(BF16)
    p_lo = (psum - p_hi.astype(F32)).astype(BF16)
    srow = _iota((LANES, nck), 0)
    kcol = _iota((LANES, nck), 1)
    cov = (CMP_STRIDE * kcol < SEL_BLOCK * srow + SEL_BLOCK) & (CMP_STRIDE * kcol + CMP_LEN > SEL_BLOCK * srow)
    cov = cov & (srow < n_sel) & (kcol < n_cmp)
    cov_t = jnp.where(cov, 1.0, 0.0).astype(BF16)
    imp_t = _dot_nt(cov_t, p_hi) + _dot_nt(cov_t, p_lo)
    qpos_l = pos_base + j * rq + _iota((1, tq), 1) % rq
    sb = _iota((nsp, 1), 0)
    valid_b = (sb < n_sel) & (sb * SEL_BLOCK <= qpos_l)
    forced = (sb == 0) | (sb == qpos_l // SEL_BLOCK)
    vals = jnp.where(forced, jnp.inf, jnp.where(valid_b, imp_t[:nsp], -jnp.inf))
    v_scr[...] = vals

    def body(jj, cnt):
        vj = v_scr[pl.ds(jj, 1), :]
        tie = jnp.where(sb > jj, 1.0, 0.0)
        return cnt + jnp.where(vj > vals, 1.0, jnp.where(vj == vals, tie, 0.0))

    cnt = lax.fori_loop(0, nsp, body, jnp.zeros((nsp, tq), F32))
    keep = valid_b & (cnt < float(min(TOP_N, n_sel)))
    selm_t = jnp.where(keep, 0.0, -SEL_NEG)
    if nsp < LANES:
        selm_t = jnp.concatenate([selm_t, jnp.zeros((LANES - nsp, tq), F32)], axis=0)
    selm = selm_t.T.astype(BF16)
    for b in range(nb):
        selm_ref[b, 0] = selm[b * rq:(b + 1) * rq]


def _cmp_topk(qb, ccmp, *, nb, rq, n_cmp, n_sel, pos_base):
    n, r, _ = qb.shape
    nck = ccmp.shape[3]
    nsp = -(-n_sel // 8) * 8
    tq = nb * rq
    kern = functools.partial(_cmp_topk_kernel, nb=nb, rq=rq, nck=nck, n_cmp=n_cmp, n_sel=n_sel, nsp=nsp,
                             pos_base=pos_base)
    return pl.pallas_call(
        kern,
        grid=(n // nb, G_B, r // rq),
        in_specs=[pl.BlockSpec((nb, rq, R_B * LANES), lambda a, g, j: (a, j, g)),
                  pl.BlockSpec((nb, 1, 1, nck, DH_B), lambda a, g, j: (a, 0, g, 0, 0)),
                  pl.BlockSpec((nb, 1, 1, nck, DH_B), lambda a, g, j: (a, 1, g, 0, 0))],
        out_specs=[pl.BlockSpec((nb, rq, R_B * LANES), lambda a, g, j: (a, j, g)),
                   pl.BlockSpec((nb, 1, rq, LANES), lambda a, g, j: (a, g, j, 0))],
        out_shape=[jax.ShapeDtypeStruct((n, r, H_B * DH_B), F32),
                   jax.ShapeDtypeStruct((n, G_B, r, LANES), BF16)],
        scratch_shapes=[pltpu.VMEM((tq, nck), F32), pltpu.VMEM((nsp, tq), F32)],
        compiler_params=_cparams(("parallel", "parallel", "parallel")),
        name="cmp_topk",
    )(qb, ccmp, ccmp)


def _attend_scores(pieces):
    m = None
    for s, _ in pieces:
        ms = jnp.max(s, axis=-1, keepdims=True)
        m = ms if m is None else jnp.maximum(m, ms)
    l = jnp.zeros_like(m)
    acc = jnp.zeros((m.shape[0], LANES), F32)
    for s, v in pieces:
        p = jnp.exp(s - m)
        l = l + jnp.sum(p, axis=-1, keepdims=True)
        acc = acc + _dot(p.astype(BF16), v)
    return acc / l


def _attend_pieces(q2, pieces):
    scored = []
    for k, v, mask in pieces:
        s = _dot_nt(q2, k)
        scored.append((s if mask is None else jnp.where(mask, s, MASK_NEG), v))
    return _attend_scores(scored)


def _pad_rows(x, rows):
    return jnp.concatenate([x, jnp.zeros((rows - x.shape[0], x.shape[1]), x.dtype)], axis=0)


def _new_token_mask(m_rows, s):
    r = _iota((m_rows, LANES), 0) % s
    return _iota((m_rows, LANES), 1) <= r


def _diff_sample_kernel(pt_ref, *refs, n_pages, s, lam_init):
    kp_refs = refs[:n_pages]
    vp_refs = refs[n_pages:2 * n_pages]
    q_ref, kn_ref, vn_ref, lamw_ref, sub_ref, o_ref = refs[2 * n_pages:]
    lane = _iota((1, LANES), 1)
    lam = _diff_lambda(lamw_ref[...], lam_init)
    new_mask = _new_token_mask(2 * s, s)
    for h in range(H_A):
        cols = slice(h * LANES, (h + 1) * LANES)
        q = q_ref[0, :, cols] * (DK_A ** -0.5)
        q2 = jnp.concatenate([jnp.where(lane < DK_A, q, 0.0), jnp.where(lane >= DK_A, q, 0.0)], axis=0).astype(BF16)
        kt_past = jnp.concatenate([r[0, cols, :].astype(BF16) for r in kp_refs], axis=1)
        s_past = _dot(q2, kt_past)
        vpast = jnp.concatenate([_slot_rows(r, h, PAGE_SIZE).astype(BF16) for r in vp_refs], axis=0)
        knew = _pad_rows(kn_ref[0, :, cols], LANES).astype(BF16)
        vnew = _pad_rows(vn_ref[0, :, cols], LANES).astype(BF16)
        s_new = jnp.where(new_mask, _dot_nt(q2, knew), MASK_NEG)
        o2 = _attend_scores([(s_past, vpast), (s_new, vnew)])
        o_ref[0, :, cols] = _diff_finalize(o2, s, lam, sub_ref[...], lam_init)


def _page_specs(n_pages, rows, width):
    return [pl.BlockSpec((1, rows, width), functools.partial(lambda b, pt, j: (pt[b, j], 0, 0), j=j))
            for j in range(n_pages)]


def _slot_rows(ref, slot, n_tok):
    return ref[0, pl.ds(slot, n_tok, stride=SLOTS), :]


def _diff_attn_sample(qa, ka, va, cache_k, cache_v, page_table, lam_w, sub_w, lam_init):
    n, s, _ = qa.shape
    n_pages = page_table.shape[1]
    kern = functools.partial(_diff_sample_kernel, n_pages=n_pages, s=s, lam_init=lam_init)
    row_spec = pl.BlockSpec((1, s, 512), lambda b, pt: (b, 0, 0))
    grid_spec = pltpu.PrefetchScalarGridSpec(
        num_scalar_prefetch=1,
        grid=(n,),
        in_specs=_page_specs(n_pages, 512, PAGE_SIZE) + _page_specs(n_pages, PAGE_SIZE * SLOTS, LANES)
        + [row_spec, row_spec, row_spec,
           pl.BlockSpec(lam_w.shape, lambda b, pt: (0, 0)), pl.BlockSpec(sub_w.shape, lambda b, pt: (0, 0))],
        out_specs=row_spec,
    )
    return pl.pallas_call(
        kern,
        grid_spec=grid_spec,
        out_shape=jax.ShapeDtypeStruct((n, s, 512), F32),
        compiler_params=_cparams(("parallel",)),
        name="diff_attn_sample",
    )(page_table, *([cache_k] * n_pages), *([cache_v] * n_pages), qa, ka, va, lam_w, sub_w)


def _sel_sample_kernel(pt_ref, *refs, n_pages, s):
    pg_refs = refs[:n_pages]
    q_ref, selm_ref, new_ref, o_ref = refs[n_pages:]
    past = n_pages * PAGE_SIZE
    new_mask = _new_token_mask(R_B * s, s)
    oh_past = _sel_onehot(past, 0)
    new_blk = past // SEL_BLOCK
    oh_new = jnp.where((_iota((LANES, LANES), 1) == new_blk) & (_iota((LANES, LANES), 0) < s), 1.0, 0.0)
    oh_new = oh_new.astype(BF16)
    scale = DH_B ** -0.5
    for g in range(G_B):
        kcols = slice(g * LANES, (g + 1) * LANES)
        vcols = slice((G_B + g) * LANES, (G_B + g + 1) * LANES)
        selm = selm_ref[0, g]
        q2 = jnp.concatenate(
            [jnp.concatenate([(q_ref[0, :, (g * R_B + r) * LANES:(g * R_B + r + 1) * LANES] * scale).astype(BF16),
                              selm], axis=-1) for r in range(R_B)], axis=0)
        kpast = jnp.concatenate([_slot_rows(r, g, PAGE_SIZE).astype(BF16) for r in pg_refs], axis=0)
        vpast = jnp.concatenate([_slot_rows(r, G_B + g, PAGE_SIZE).astype(BF16) for r in pg_refs], axis=0)
        kpast = jnp.concatenate([kpast, oh_past], axis=-1)
        knew = jnp.concatenate([_pad_rows(new_ref[0, :, kcols], LANES).astype(BF16), oh_new], axis=-1)
        vnew = _pad_rows(new_ref[0, :, vcols], LANES).astype(BF16)
        o2 = _attend_pieces(q2, [(kpast, vpast, None), (knew, vnew, new_mask)])
        for r in range(R_B):
            o_ref[0, :, (g * R_B + r) * LANES:(g * R_B + r + 1) * LANES] = o2[r * s:(r + 1) * s]


def _sel_attn_sample(qb, selm, sel_new, cache, page_table):
    n, s, _ = qb.shape
    n_pages = page_table.shape[1]
    assert s <= SEL_BLOCK and (n_pages * PAGE_SIZE) % SEL_BLOCK == 0
    kern = functools.partial(_sel_sample_kernel, n_pages=n_pages, s=s)
    row_spec = pl.BlockSpec((1, s, 512), lambda b, pt: (b, 0, 0))
    grid_spec = pltpu.PrefetchScalarGridSpec(
        num_scalar_prefetch=1,
        grid=(n,),
        in_specs=_page_specs(n_pages, PAGE_SIZE * SLOTS, LANES)
        + [row_spec, pl.BlockSpec((1, G_B, s, LANES), lambda b, pt: (b, 0, 0, 0)), row_spec],
        out_specs=row_spec,
    )
    return pl.pallas_call(
        kern,
        grid_spec=grid_spec,
        out_shape=jax.ShapeDtypeStruct((n, s, 512), F32),
        compiler_params=_cparams(("parallel",)),
        name="nsa_sel_sample",
    )(page_table, *([cache] * n_pages), qb, selm, sel_new)


def _win_sample_kernel(q_ref, buf_ref, new_ref, o_ref, wout_ref, *, s, wb):
    new_mask = _new_token_mask(R_B * s, s)
    r = _iota((R_B * s, wb), 0) % s
    buf_mask = _iota((R_B * s, wb), 1) > r + (wb - WINDOW)
    scale = DH_B ** -0.5
    for g in range(G_B):
        kcols = slice(g * LANES, (g + 1) * LANES)
        vcols = slice((G_B + g) * LANES, (G_B + g + 1) * LANES)
        q2 = jnp.concatenate(
            [(q_ref[0, :, (g * R_B + r_) * LANES:(g * R_B + r_ + 1) * LANES] * scale).astype(BF16)
             for r_ in range(R_B)], axis=0)
        kbuf = _slot_rows(buf_ref, g, wb).astype(BF16)
        vbuf = _slot_rows(buf_ref, G_B + g, wb).astype(BF16)
        knew = _pad_rows(new_ref[0, :, kcols], LANES).astype(BF16)
        vnew = _pad_rows(new_ref[0, :, vcols], LANES).astype(BF16)
        o2 = _attend_pieces(q2, [(kbuf, vbuf, buf_mask), (knew, vnew, new_mask)])
        for r_ in range(R_B):
            o_ref[0, :, (g * R_B + r_) * LANES:(g * R_B + r_ + 1) * LANES] = o2[r_ * s:(r_ + 1) * s]
    keep = (wb - s) * SLOTS
    wout_ref[0, 0:keep] = buf_ref[0, s * SLOTS:wb * SLOTS]
    for slot in range(SLOTS):
        wout_ref[0, pl.ds(keep + slot, s, stride=SLOTS), :] = new_ref[0, :, slot * LANES:(slot + 1) * LANES]


def _win_attn_sample(qb, win_buf, win_new, past):
    n, s, _ = qb.shape
    wb = win_buf.shape[1] // SLOTS
    assert past >= wb and wb % 8 == 0 and s % 8 == 0
    kern = functools.partial(_win_sample_kernel, s=s, wb=wb)
    row_spec = pl.BlockSpec((1, s, 512), lambda b: (b, 0, 0))
    buf_spec = pl.BlockSpec((1, wb * SLOTS, LANES), lambda b: (b, 0, 0))
    return pl.pallas_call(
        kern,
        grid=(n,),
        in_specs=[row_spec, buf_spec, row_spec],
        out_specs=[row_spec, buf_spec],
        out_shape=[jax.ShapeDtypeStruct((n, s, 512), F32), jax.ShapeDtypeStruct((n, wb * SLOTS, LANES), F32)],
        compiler_params=_cparams(("parallel",)),
        name="nsa_win_sample",
    )(qb, win_buf, win_new)


def _even_out_kernel(y_ref, g1_ref, oa_ref, oc_ref, os_ref, ow_ref, gt_ref, w_ref, o_ref):
    gt = gt_ref[...]
    parts = [oa_ref[...].astype(BF16)]
    for hb in range(H_B):
        sl = slice(hb * LANES, (hb + 1) * LANES)
        ob = (gt[:, 3 * hb:3 * hb + 1] * oc_ref[:, sl] + gt[:, 3 * hb + 1:3 * hb + 2] * os_ref[:, sl]
              + gt[:, 3 * hb + 2:3 * hb + 3] * ow_ref[:, sl])
        parts.append(ob.astype(BF16))
    out = _dot(jnp.concatenate(parts, axis=-1), w_ref[...])
    y = y_ref[...]
    o_ref[...] = y + g1_ref[0] * out.reshape(y.shape)


def _even_out(til, y, mods, oa, oc, os_, ow, gt, w_out):
    return pl.pallas_call(
        _even_out_kernel,
        grid=til.grid,
        in_specs=[til.x_spec(D_MODEL), til.mod_spec(2), til.flat_spec(512), til.flat_spec(512),
                  til.flat_spec(512), til.flat_spec(512), til.flat_spec(LANES), _full_spec(w_out.shape)],
        out_specs=til.x_spec(D_MODEL),
        out_shape=jax.ShapeDtypeStruct(y.shape, F32),
        compiler_params=_cparams(("parallel", "parallel")),
        name="even_out",
    )(y, mods, oa, oc, os_, ow, gt, w_out)


def _mlp_kernel(y_ref, sh_ref, sc_ref, g_ref, nw_ref, w1_ref, w2_ref, fw_ref, o_ref, *, final):
    y = y_ref[...]
    h = _norm_mod(y, nw_ref[...], sh_ref[0], sc_ref[0]).reshape(-1, D_MODEL).astype(BF16)
    a = jnp.maximum(_dot(h, w1_ref[...]), 0.0)
    out = _dot((a * a).astype(BF16), w2_ref[...])
    y2 = y + g_ref[0] * out.reshape(y.shape)
    if final:
        y2 = _rms(y2, fw_ref[...])
    o_ref[...] = y2


def _mlp(til, y, mods, nw, w1, w2, fw, final):
    kern = functools.partial(_mlp_kernel, final=final)
    return pl.pallas_call(
        kern,
        grid=til.grid,
        in_specs=[til.x_spec(D_MODEL), til.mod_spec(3), til.mod_spec(4), til.mod_spec(5),
                  _full_spec((1, D_MODEL)), _full_spec(w1.shape), _full_spec(w2.shape), _full_spec((1, D_MODEL))],
        out_specs=til.x_spec(D_MODEL),
        out_shape=jax.ShapeDtypeStruct(y.shape, F32),
        compiler_params=_cparams(("parallel", "parallel")),
        name="mlp",
    )(y, mods, mods, mods, nw, w1, w2, fw)


def _gla_inproj_kernel(x_ref, sh_ref, sc_ref, nw_ref, w_ref, wgl_ref, wgate_ref, bgate_ref,
                       q_ref, k_ref, v_ref, r_ref, la_ref):
    h = _norm_mod(x_ref[...], nw_ref[...], sh_ref[0], sc_ref[0]).reshape(-1, D_MODEL).astype(BF16)
    nk = H_C * DK_C
    nv = H_C * DV_C
    q_ref[...] = _dot(h, w_ref[:, 0:nk]) * (DK_C ** -0.5)
    k_ref[...] = _dot(h, w_ref[:, nk:2 * nk])
    v_ref[...] = _dot(h, w_ref[:, 2 * nk:2 * nk + nv])
    r_ref[...] = _dot(h, w_ref[:, 2 * nk + nv:2 * nk + 2 * nv])
    gl = _dot(h, wgl_ref[...])
    x = _dot(gl.astype(BF16), wgate_ref[...]) + bgate_ref[...]
    log_sig = jnp.minimum(x, 0.0) - jnp.log1p(jnp.exp(-jnp.abs(x)))
    la_ref[...] = log_sig / GATE_TAU


def _gla_inproj(til, x, mods, nw, w_main, w_gl, w_gate, b_gate):
    rows_total = til.nb * til.r
    widths = (H_C * DK_C, H_C * DK_C, H_C * DV_C, H_C * DV_C, H_C * DK_C)
    return pl.pallas_call(
        _gla_inproj_kernel,
        grid=til.grid,
        in_specs=[til.x_spec(D_MODEL), til.mod_spec(0), til.mod_spec(1), _full_spec((1, D_MODEL)),
                  _full_spec(w_main.shape), _full_spec(w_gl.shape), _full_spec(w_gate.shape),
                  _full_spec(b_gate.shape)],
        out_specs=[til.flat_spec(c) for c in widths],
        out_shape=[jax.ShapeDtypeStruct((rows_total, c), F32) for c in widths],
        compiler_params=_cparams(("parallel", "parallel")),
        name="gla_inproj",
    )(x, mods, mods, nw, w_main, w_gl, w_gate, b_gate)


def _gla_prep(q, k, g, c, sub, n_real):
    tri = jnp.where(_iota((c, c), 0) >= _iota((c, c), 1), 1.0, 0.0)
    b = jnp.dot(tri, g, preferred_element_type=F32, precision=lax.Precision.HIGHEST)
    qe = (q * jnp.exp(b)).astype(BF16)
    lane_c = _iota((sub, c), 1)
    row_s = _iota((sub, c), 0)
    att_rows = []
    for blk in range(c // sub):
        lo = blk * sub
        qi, ki, bi = q[lo:lo + sub], k[lo:lo + sub], b[lo:lo + sub]
        diag = jnp.zeros((sub, c), F32)
        for jj in range(min(sub, max(n_real - lo, 0))):
            e = jnp.exp(jnp.minimum(bi - bi[jj:jj + 1], 0.0))
            col = jnp.sum(qi * ki[jj:jj + 1] * e, axis=-1, keepdims=True)
            diag = jnp.where(lane_c == lo + jj, col, diag)
        att = jnp.where(lane_c - lo <= row_s, diag, 0.0)
        if blk > 0:
            bs = b[lo - 1:lo]
            q_in = qi * jnp.exp(bi - bs)
            k_out = k * jnp.exp(jnp.minimum(bs - b, 0.0))
            att = jnp.where(lane_c < lo, _dot_nt(q_in.astype(BF16), k_out.astype(BF16)), att)
        att_rows.append(att)
    att = att_rows[0] if len(att_rows) == 1 else jnp.concatenate(att_rows, axis=0)
    bl = b[c - 1:c]
    kd = (k * jnp.exp(bl - b)).astype(BF16)
    eye = _iota((DK_C, DK_C), 0) == _iota((DK_C, DK_C), 1)
    decay = jnp.sum(jnp.where(eye, jnp.exp(bl), 0.0), axis=-1, keepdims=True)
    return qe, att, kd, decay


def _gla_apply(state, prep, v):
    qe, att, kd, decay = prep
    o = _dot(qe, state.astype(BF16)) + _dot(att, v)
    return o, decay * state + _dot_tn(kd, v.astype(BF16))


def _gla_rec_kernel(*refs, tt, c, sub, hp, has_s0):
    if has_s0:
        q_ref, k_ref, v_ref, g_ref, s0_ref, o_ref, sfin_ref, s_ref = refs
    else:
        q_ref, k_ref, v_ref, g_ref, o_ref, sfin_ref, s_ref = refs
    t = pl.program_id(2)

    @pl.when(t == 0)
    def _():
        s_ref[...] = s0_ref[0] if has_s0 else jnp.zeros(s_ref.shape, F32)

    def kcols(hh):
        return slice(hh * DK_C, (hh + 1) * DK_C)

    def vcols(hh):
        return slice(hh * DV_C, (hh + 1) * DV_C)

    if tt < c:
        pad = lambda x: _pad_rows(x, c)
        for hh in range(hp):
            prep = _gla_prep(pad(q_ref[0, :, kcols(hh)]), pad(k_ref[0, :, kcols(hh)]),
                             pad(g_ref[0, :, kcols(hh)]), c, sub, tt)
            o, s_ref[hh] = _gla_apply(s_ref[hh], prep, pad(v_ref[0, :, vcols(hh)]))
            o_ref[0, :, vcols(hh)] = o[:tt]
    else:
        per_trip = 2 if (tt // c) % 2 == 0 else 1

        def body(ci, carry):
            for hh in range(hp):
                rows = [pl.ds(pl.multiple_of((ci * per_trip + u) * c, c), c) for u in range(per_trip)]
                preps = [_gla_prep(q_ref[0, r, kcols(hh)], k_ref[0, r, kcols(hh)], g_ref[0, r, kcols(hh)],
                                   c, sub, c) for r in rows]
                state = s_ref[hh]
                for r, prep in zip(rows, preps):
                    o_ref[0, r, vcols(hh)], state = _gla_apply(state, prep, v_ref[0, r, vcols(hh)])
                s_ref[hh] = state
            return carry
        lax.fori_loop(0, tt // c // per_trip, body, 0)

    @pl.when(t == pl.num_programs(2) - 1)
    def _():
        sfin_ref[0] = s_ref[...]


def _gla_recurrence(q, k, v, g, s0, tt, c, sub, hp):
    n, t, _ = q.shape
    kern = functools.partial(_gla_rec_kernel, tt=tt, c=c, sub=sub, hp=hp, has_s0=s0 is not None)
    kspec = pl.BlockSpec((1, tt, hp * DK_C), lambda b, h, i: (b, i, h))
    vspec = pl.BlockSpec((1, tt, hp * DV_C), lambda b, h, i: (b, i, h))
    sspec = pl.BlockSpec((1, hp, DK_C, DV_C), lambda b, h, i: (b, h, 0, 0))
    in_specs = [kspec, kspec, vspec, kspec]
    args = [q, k, v, g]
    if s0 is not None:
        in_specs.append(sspec)
        args.append(s0)
    return pl.pallas_call(
        kern,
        grid=(n, H_C // hp, t // tt),
        in_specs=in_specs,
        out_specs=[vspec, sspec],
        out_shape=[jax.ShapeDtypeStruct((n, t, H_C * DV_C), F32),
                   jax.ShapeDtypeStruct((n, H_C, DK_C, DV_C), F32)],
        scratch_shapes=[pltpu.VMEM((hp, DK_C, DV_C), F32)],
        compiler_params=_cparams(("parallel", "parallel", "arbitrary")),
        name="gla_recurrence",
    )(*args)


def _gla_out_kernel(y_ref, g1_ref, o_ref_in, r_ref, nw_ref, w_ref, out_ref):
    parts = []
    for h in range(H_C):
        sl = slice(h * DV_C, (h + 1) * DV_C)
        r = r_ref[:, sl]
        parts.append((_rms(o_ref_in[:, sl], nw_ref[...]) * (r * jax.nn.sigmoid(r))).astype(BF16))
    out = _dot(jnp.concatenate(parts, axis=-1), w_ref[...])
    y = y_ref[...]
    out_ref[...] = y + g1_ref[0] * out.reshape(y.shape)


def _gla_out(til, y, mods, o, r, nw, w_out):
    return pl.pallas_call(
        _gla_out_kernel,
        grid=til.grid,
        in_specs=[til.x_spec(D_MODEL), til.mod_spec(2), til.flat_spec(H_C * DV_C), til.flat_spec(H_C * DV_C),
                  _full_spec(nw.shape), _full_spec(w_out.shape)],
        out_specs=til.x_spec(D_MODEL),
        out_shape=jax.ShapeDtypeStruct(y.shape, F32),
        compiler_params=_cparams(("parallel", "parallel")),
        name="gla_out",
    )(y, mods, o, r, nw, w_out)


def _rope_tables(pos, d):
    inv = ROPE_THETA ** (-jnp.arange(0, d, 2, dtype=F32) / d)
    ang = pos.astype(F32)[:, None] * inv[None, :]
    cos, sin = jnp.cos(ang), jnp.sin(ang)
    rep = LANES // d
    c = jnp.tile(jnp.concatenate([cos, cos], axis=-1), (1, rep))
    s = jnp.tile(jnp.concatenate([-sin, sin], axis=-1), (1, rep))
    return c, s


def _mods_for(mod_l, lo, hi):
    nb = hi - lo
    return mod_l[lo:hi].reshape(nb, 6, D_MODEL).transpose(1, 0, 2).reshape(6, nb, 1, D_MODEL)


def _prompt_tile_rows(t):
    return math.gcd(t, 256)


def _even_layer(yp, ys, mods_p, mods_s, caches, page_table, wts, lam_init, til_p, til_s):
    (c_dk, c_dv, c_cmp, c_sel, win_buf) = caches
    n, t, _ = yp.shape
    ns, s, _ = ys.shape
    n_pages = page_table.shape[1]
    past = n_pages * PAGE_SIZE
    w_in = wts["w_in"]
    n_main = w_in.shape[1] - 3 * H_B
    w_main = w_in[:, :n_main].astype(BF16)
    w_gate = jnp.pad(w_in[:, n_main:], ((0, 0), (0, LANES - 3 * H_B))).astype(BF16)
    nw1 = wts["norm1"].reshape(1, D_MODEL)
    lam_w = wts["lam_w"]
    sub_w = wts["sub_w"].reshape(1, DV_A)
    wc, pe, w2c = _compress_weights(wts["cmp_pe"], wts["cmp_w1"], wts["cmp_w2"])
    w_out = wts["w_out"].astype(BF16)

    tabs_p = _rope_tables(jnp.arange(t), DK_A) + _rope_tables(jnp.arange(t), DH_B)
    qa, ka, va, qb, cmp_kv, sel_kv, win_kv, gt = _even_inproj(til_p, yp, mods_p, nw1, w_main, w_gate, tabs_p)
    r3 = lambda a: a.reshape(n, t, a.shape[-1])
    tq = _prompt_tile_rows(t)
    oa = _diff_attn_prompt(r3(qa), r3(ka), r3(va), lam_w, sub_w, lam_init, tq)
    ccmp = _compress_prompt(r3(cmp_kv), wc, pe, w2c)
    n_cmp = (t - CMP_LEN) // CMP_STRIDE + 1
    n_sel = -(-t // SEL_BLOCK)
    o_cmp, selm = _cmp_topk(r3(qb), ccmp, nb=1, rq=tq, n_cmp=n_cmp, n_sel=n_sel, pos_base=0)
    o_sel = _nsa_attn_prompt(r3(qb), r3(sel_kv), selm, tq)
    o_win = _nsa_attn_prompt(r3(qb), r3(win_kv), None, tq)
    f2 = lambda a: a.reshape(n * t, a.shape[-1])
    yp = _even_out(til_p, yp, mods_p, f2(oa), f2(o_cmp), f2(o_sel), f2(o_win), gt, w_out)
    wl = min(WINDOW, t)
    st_p = (r3(ka).reshape(n, t, H_A, 2, DK_A), r3(va).reshape(n, t, H_A, DV_A),
            r3(cmp_kv).reshape(n, t, 2, G_B, DH_B), r3(sel_kv).reshape(n, t, 2, G_B, DH_B),
            r3(win_kv)[:, t - wl:].reshape(n, wl, 2, G_B, DH_B))

    pos_s = past + jnp.arange(s)
    tabs_s = tuple(jnp.tile(x, (til_s.b, 1)) for x in _rope_tables(pos_s, DK_A) + _rope_tables(pos_s, DH_B))
    qa, ka, va, qb, cmp_kv, sel_kv, win_kv, gt = _even_inproj(til_s, ys, mods_s, nw1, w_main, w_gate, tabs_s)
    r3 = lambda a: a.reshape(ns, s, a.shape[-1])
    slot_cache = lambda c: c.reshape(c.shape[0], PAGE_SIZE * SLOTS, LANES)
    c_dkt = jnp.transpose(c_dk, (0, 2, 3, 4, 1)).reshape(c_dk.shape[0], H_A * 2 * DK_A, PAGE_SIZE)
    oa = _diff_attn_sample(r3(qa), r3(ka), r3(va), c_dkt, slot_cache(c_dv), page_table, lam_w, sub_w, lam_init)
    ccmp = _compress_sample(slot_cache(c_cmp), page_table, wc, pe, w2c)
    total = past + s
    n_cmp = (total - CMP_LEN) // CMP_STRIDE + 1
    n_sel = -(-total // SEL_BLOCK)
    assert n_cmp <= ccmp.shape[3] and n_sel <= LANES
    nb = math.gcd(ns, LANES // s)
    o_cmp, selm = _cmp_topk(r3(qb), ccmp, nb=nb, rq=s, n_cmp=n_cmp, n_sel=n_sel, pos_base=past)
    o_sel = _sel_attn_sample(r3(qb), selm, r3(sel_kv), slot_cache(c_sel), page_table)
    wb = win_buf.shape[1]
    o_win, win_out = _win_attn_sample(r3(qb), win_buf.reshape(ns, wb * SLOTS, LANES), r3(win_kv), past)
    f2 = lambda a: a.reshape(ns * s, a.shape[-1])
    ys = _even_out(til_s, ys, mods_s, f2(oa), f2(o_cmp), f2(o_sel), f2(o_win), gt, w_out)
    st_s = (r3(ka).reshape(ns, s, H_A, 2, DK_A), r3(va).reshape(ns, s, H_A, DV_A),
            r3(cmp_kv).reshape(ns, s, 2, G_B, DH_B), r3(sel_kv).reshape(ns, s, 2, G_B, DH_B),
            win_out.reshape(ns, wb, 2, G_B, DH_B))
    return yp, ys, st_p, st_s


def _odd_layer(yp, ys, mods_p, mods_s, s0, wts, til_p, til_s):
    w_in = wts["w_in"]
    n_main = w_in.shape[1] - GATE_RANK
    w_main = w_in[:, :n_main].astype(BF16)
    w_gl = jnp.pad(w_in[:, n_main:], ((0, 0), (0, LANES - GATE_RANK))).astype(BF16)
    w_gate = jnp.pad(wts["w_gate"], ((0, LANES - GATE_RANK), (0, 0))).astype(BF16)
    b_gate = wts["b_gate"].reshape(1, -1)
    nw1 = wts["norm1"].reshape(1, D_MODEL)
    gnw = wts["gnorm"].reshape(1, DV_C)
    w_out = wts["w_out"].astype(BF16)
    outs = []
    for y, mods, til, state in ((yp, mods_p, til_p, None), (ys, mods_s, til_s, s0)):
        n, t, _ = y.shape
        q, k, v, r, la = _gla_inproj(til, y, mods, nw1, w_main, w_gl, w_gate, b_gate)
        r3 = lambda a: a.reshape(n, t, a.shape[-1])
        c = math.gcd(t, GLA_CHUNK)
        if c >= GLA_SUB:
            tt, cc, hp = math.gcd(t, 8 * c), c, 1
        else:
            tt, cc, hp = t, GLA_SUB, H_C
        o, s_fin = _gla_recurrence(r3(q), r3(k), r3(v), r3(la), state, tt, cc, GLA_SUB, hp)
        y = _gla_out(til, y, mods, o.reshape(n * t, -1), r, gnw, w_out)
        outs.append((y, s_fin))
    return outs[0][0], outs[1][0], outs[0][1], outs[1][1]


def kernel(x_prompt, x_sample, c_prompt, c_sample, cache_diff_k, cache_diff_v, cache_cmp_kv, cache_sel_kv,
           state_win_kv, state_gla, page_table, norm1_w, norm2_w, ada_w, ada_b, even_w_in, even_w_out,
           diff_lambda_w, diff_subln_w, cmp_pe, cmp_w1, cmp_w2, gla_w_in, gla_w_gate, gla_b_gate, gla_norm_w,
           gla_w_out, mlp_w1, mlp_w2, final_norm_w):
    depth = ada_w.shape[0]
    n, t, _ = x_prompt.shape
    ns, s, _ = x_sample.shape
    til_p = _Tiling(n, t, 1, _prompt_tile_rows(t))
    til_s = _Tiling(ns, s, math.gcd(ns, 256 // s), s)

    pad = (-(n + ns)) % 8
    c_all = jnp.concatenate([c_prompt, c_sample, jnp.zeros((pad, D_MODEL), F32)], axis=0)
    mod = _adaln(c_all, ada_w, ada_b)

    yp, ys = x_prompt, x_sample
    st_p = [[] for _ in range(6)]
    st_s = [[] for _ in range(6)]
    fw = final_norm_w.reshape(1, D_MODEL)
    for l in range(depth):
        mods_p = _mods_for(mod[l], 0, n)
        mods_s = _mods_for(mod[l], n, n + ns)
        if l % 2 == 0:
            e = l // 2
            lam_init = 0.8 - 0.6 * math.exp(-0.3 * l)
            wts = dict(w_in=even_w_in[e], w_out=even_w_out[e], lam_w=diff_lambda_w[e], sub_w=diff_subln_w[e],
                       cmp_pe=cmp_pe[e], cmp_w1=cmp_w1[e], cmp_w2=cmp_w2[e], norm1=norm1_w[l])
            caches = (cache_diff_k[e], cache_diff_v[e], cache_cmp_kv[e], cache_sel_kv[e], state_win_kv[e])
            yp, ys, sp, ss = _even_layer(yp, ys, mods_p, mods_s, caches, page_table, wts, lam_init, til_p, til_s)
            for i in range(5):
                st_p[i].append(sp[i])
                st_s[i].append(ss[i])
        else:
            o = l // 2
            wts = dict(w_in=gla_w_in[o], w_gate=gla_w_gate[o], b_gate=gla_b_gate[o], gnorm=gla_norm_w[o],
                       w_out=gla_w_out[o], norm1=norm1_w[l])
            yp, ys, gp, gs = _odd_layer(yp, ys, mods_p, mods_s, state_gla[o], wts, til_p, til_s)
            st_p[5].append(gp)
            st_s[5].append(gs)
        final = l == depth - 1
        w1 = mlp_w1[l].astype(BF16)
        w2 = mlp_w2[l].astype(BF16)
        nw2 = norm2_w[l].reshape(1, D_MODEL)
        yp = _mlp(til_p, yp, mods_p, nw2, w1, w2, fw, final)
        ys = _mlp(til_s, ys, mods_s, nw2, w1, w2, fw, final)
    outs_p = [jnp.stack(x, axis=0) for x in st_p]
    outs_s = [jnp.stack(x, axis=0) for x in st_s]
    return (yp, ys, *outs_p, *outs_s)
```

```python
import functools
import math

import jax
import jax.numpy as jnp
from jax import lax
from jax.experimental import pallas as pl
from jax.experimental.pallas import tpu as pltpu

F32 = jnp.float32
BF16 = jnp.bfloat16

D_MODEL = 1024
PAGE_SIZE = 128
H_A = 4
DK_A = 64
DV_A = 128
H_B = 4
G_B = 2
R_B = 2
DH_B = 128
CMP_LEN = 32
CMP_STRIDE = 16
SEL_BLOCK = 64
TOP_N = 16
WINDOW = 512
H_C = 4
DK_C = 128
DV_C = 256
GATE_RANK = 16
GATE_TAU = 16.0
GLA_CHUNK = 64
GLA_SUB = 16
GLA_SUB_LONG = 16
D_FF = 4 * D_MODEL
ROPE_THETA = 10000.0
EPS = 1e-6

LANES = 128
SLOTS = 4
MASK_NEG = -1e30
LOG2E = 1.4426950408889634
SAFE_SHIFT = 56.0
SEL_NEG = 32768.0
VMEM_LIMIT_MB = 56


def _cparams(sem, vmem_mb=VMEM_LIMIT_MB):
    return pltpu.CompilerParams(dimension_semantics=sem, vmem_limit_bytes=vmem_mb * 1024 * 1024)


def _dot(a, b):
    return jnp.dot(a, b, preferred_element_type=F32)


def _dot_nt(a, b):
    return lax.dot_general(a, b, (((1,), (1,)), ((), ())), preferred_element_type=F32)


def _dot_tn(a, b):
    return lax.dot_general(a, b, (((0,), (0,)), ((), ())), preferred_element_type=F32)


def _iota(shape, dim):
    return lax.broadcasted_iota(jnp.int32, shape, dim)


def _rms(x, w):
    ms = jnp.mean(x * x, axis=-1, keepdims=True)
    return x * lax.rsqrt(ms + EPS) * w


def _norm_mod(x, nw, shift, scale):
    return _rms(x, nw) * (1.0 + scale) + shift


class _Tiling:
    def __init__(self, nb, r, b, rt):
        assert nb % b == 0 and r % rt == 0 and (b == 1 or rt == r)
        self.nb, self.r, self.b, self.rt = nb, r, b, rt
        self.grid = (nb // b, r // rt)
        self.rows = b * rt
        self.nrb = r // rt

    def x_spec(self, d):
        return pl.BlockSpec((self.b, self.rt, d), lambda i, j: (i, j, 0))

    def mod_spec(self, k):
        return pl.BlockSpec((1, self.b, 1, D_MODEL), lambda i, j: (k, i, 0, 0))

    def flat_spec(self, c):
        nrb = self.nrb
        return pl.BlockSpec((self.rows, c), lambda i, j: (i * nrb + j, 0))

    def tab_spec(self):
        return pl.BlockSpec((self.rows, LANES), lambda i, j: (j, 0))


def _full_spec(shape):
    nd = len(shape)
    return pl.BlockSpec(shape, lambda *_: (0,) * nd)


def _adaln_kernel(c_ref, w_ref, b_ref, o_ref):
    c = c_ref[...]
    a = (c * jax.nn.sigmoid(c)).astype(BF16)
    o_ref[0] = _dot(a, w_ref[0].astype(BF16)) + b_ref[0]


def _adaln(c_all, ada_w, ada_b):
    depth, d, n6 = ada_w.shape
    rows = c_all.shape[0]
    tn = 1536
    return pl.pallas_call(
        _adaln_kernel,
        grid=(depth, n6 // tn),
        in_specs=[pl.BlockSpec((rows, d), lambda l, j: (0, 0)),
                  pl.BlockSpec((1, d, tn), lambda l, j: (l, 0, j)),
                  pl.BlockSpec((1, 1, tn), lambda l, j: (l, 0, j))],
        out_specs=pl.BlockSpec((1, rows, tn), lambda l, j: (l, 0, j)),
        out_shape=jax.ShapeDtypeStruct((depth, rows, n6), F32),
        compiler_params=_cparams(("parallel", "parallel")),
        name="adaln",
    )(c_all, ada_w, ada_b.reshape(depth, 1, n6))


def _swap_half(x, half):
    if 2 * half == LANES:
        return pltpu.roll(x, half, 1)
    lane = _iota((1, LANES), 1)
    lo = (lane % (2 * half)) < half
    return jnp.where(lo, pltpu.roll(x, LANES - half, 1), pltpu.roll(x, half, 1))


def _even_inproj_kernel(*refs, k_feature_major):
    if k_feature_major:
        (x_ref, sh_ref, sc_ref, nw_ref, w_ref, wg_ref, c64_ref, s64_ref, c128_ref, s128_ref, wkt_ref, ct_ref, st_ref,
         qa_ref, ka_ref, va_ref, qb_ref, cmp_ref, sel_ref, win_ref, gt_ref) = refs
    else:
        (x_ref, sh_ref, sc_ref, nw_ref, w_ref, wg_ref, c64_ref, s64_ref, c128_ref, s128_ref,
         qa_ref, ka_ref, va_ref, qb_ref, cmp_ref, sel_ref, win_ref, gt_ref) = refs
    h = _norm_mod(x_ref[...], nw_ref[...], sh_ref[0], sc_ref[0])
    h = h.reshape(-1, D_MODEL).astype(BF16)
    c64, s64, c128, s128 = c64_ref[...], s64_ref[...], c128_ref[...], s128_ref[...]

    def rope64(p):
        return p * c64 + _swap_half(p, DK_A // 2) * s64

    def rope128(p):
        return p * c128 + _swap_half(p, DH_B // 2) * s128

    def project(ref, off, ropes):
        p = _dot(h, w_ref[:, off:off + 4 * LANES])
        for j, rope in enumerate(ropes):
            sl = slice(j * LANES, (j + 1) * LANES)
            ref[:, sl] = p[:, sl] if rope is None else rope(p[:, sl])

    project(qa_ref, 0, [rope64] * 4)
    if k_feature_major:
        kt = _dot_nt(wkt_ref[...], h)
        ct, st = ct_ref[...], st_ref[...]
        half = DK_A // 2
        for grp in range(H_A * 2):
            x1 = kt[grp * DK_A:grp * DK_A + half]
            x2 = kt[grp * DK_A + half:(grp + 1) * DK_A]
            ka_ref[0, grp * DK_A:grp * DK_A + half, :] = x1 * ct - x2 * st
            ka_ref[0, grp * DK_A + half:(grp + 1) * DK_A, :] = x2 * ct + x1 * st
    else:
        project(ka_ref, 512, [rope64] * 4)
    project(va_ref, 1024, [None] * 4)
    project(qb_ref, 1536, [rope128] * 4)
    for t, ref in enumerate((cmp_ref, sel_ref, win_ref)):
        project(ref, 2048 + t * 512, [rope128, rope128, None, None])
    gt_ref[...] = jax.nn.sigmoid(_dot(h, wg_ref[...]))


def _even_inproj(til, x, mods, nw, w_main, w_gate, tabs, kt_tabs=None):
    rows_total = til.nb * til.r
    widths = (512, 512, 512, 512, 512, 512, 512, LANES)
    in_specs = [til.x_spec(D_MODEL), til.mod_spec(0), til.mod_spec(1), _full_spec((1, D_MODEL)),
                _full_spec(w_main.shape), _full_spec(w_gate.shape)] + [til.tab_spec()] * 4
    args = [x, mods, mods, nw, w_main, w_gate, *tabs]
    out_specs = [til.flat_spec(c) for c in widths]
    out_shape = [jax.ShapeDtypeStruct((rows_total, c), F32) for c in widths]
    if kt_tabs is not None:
        assert til.b == 1
        w_kt = jnp.transpose(w_main[:, 512:1024])
        in_specs += [_full_spec(w_kt.shape)] + [pl.BlockSpec((DK_A // 2, til.rt), lambda i, j: (0, j))] * 2
        args += [w_kt, *kt_tabs]
        out_specs[1] = pl.BlockSpec((1, 512, til.rt), lambda i, j: (i, 0, j))
        out_shape[1] = jax.ShapeDtypeStruct((til.nb, 512, til.r), F32)
    return pl.pallas_call(
        functools.partial(_even_inproj_kernel, k_feature_major=kt_tabs is not None),
        grid=til.grid,
        in_specs=in_specs,
        out_specs=out_specs,
        out_shape=out_shape,
        compiler_params=_cparams(("parallel", "parallel")),
        name="even_inproj",
    )(*args)


def _for_tiles(lo, hi, fn):
    n = hi - lo

    def pair(j, carry):
        fn(lo + 2 * j)
        fn(lo + 2 * j + 1)
        return carry

    lax.fori_loop(0, n // 2, pair, 0)

    @pl.when(n % 2 == 1)
    def _():
        fn(hi - 1)


def _row_sumsq(x):
    xf = x.astype(F32)
    return _dot((xf * xf).astype(BF16), jnp.ones((LANES, LANES), BF16))


def _key_norm_bound(k_ref, n_keys, tile):
    def body(kb, m):
        k = k_ref[0, pl.ds(pl.multiple_of(kb * tile, tile), tile), :].astype(BF16)
        return jnp.maximum(m, _row_sumsq(k))

    m = lax.fori_loop(0, n_keys // tile, body, jnp.zeros((tile, LANES), F32))
    return jnp.max(m, axis=0, keepdims=True)


def _score_bound(q2, kmax2):
    qmax2 = jnp.max(_row_sumsq(q2), axis=0, keepdims=True)
    return jnp.sqrt(qmax2 * kmax2) * 1.05


def _two_pass_attention(mx_ref, acc_ref, lo, hi, last, scores, values, mask_body=False, bound=None):
    def lane_max(s):
        m = s[:, 0:LANES]
        for c in range(1, s.shape[1] // LANES):
            m = jnp.maximum(m, s[:, c * LANES:(c + 1) * LANES])
        return m

    def pass1(kb, masked):
        mx_ref[...] = jnp.maximum(mx_ref[...], lane_max(scores(kb, masked)))

    def exact_max():
        mx_ref[...] = jnp.full(mx_ref.shape, MASK_NEG, F32)
        _for_tiles(lo, hi, lambda kb: pass1(kb, mask_body))
        pass1(last, True)
        mx_ref[...] = jnp.broadcast_to(jnp.max(mx_ref[...], axis=-1, keepdims=True), mx_ref.shape)

    if bound is None:
        exact_max()
    else:
        safe = jnp.max(bound) <= SAFE_SHIFT

        @pl.when(safe)
        def _():
            mx_ref[...] = jnp.broadcast_to(bound, mx_ref.shape)

        @pl.when(jnp.logical_not(safe))
        def _():
            exact_max()

    acc_ref[...] = jnp.zeros(acc_ref.shape, F32)

    def pass2(kb, masked):
        s = scores(kb, masked)
        m = mx_ref[...]
        p = jnp.concatenate([jnp.exp2(s[:, c * LANES:(c + 1) * LANES] - m) for c in range(s.shape[1] // LANES)],
                            axis=-1).astype(BF16)
        v = values(kb)
        v1 = jnp.concatenate([v, jnp.ones(v.shape, BF16)], axis=-1)
        acc_ref[...] += _dot(p, v1)

    _for_tiles(lo, hi, lambda kb: pass2(kb, mask_body))
    pass2(last, True)
    return acc_ref[:, 0:LANES] / acc_ref[:, LANES:2 * LANES]


def _diff_lambda(lw, lam_init):
    a = jnp.sum(lw[0:1] * lw[1:2], axis=-1, keepdims=True)
    b = jnp.sum(lw[2:3] * lw[3:4], axis=-1, keepdims=True)
    return jnp.exp(a) - jnp.exp(b) + lam_init


def _diff_finalize(o2, tq, lam, sub_w, lam_init):
    od = o2[:tq] - lam * o2[tq:]
    return _rms(od, sub_w) * (1.0 - lam_init)


def _diff_flash_kernel(q_ref, k_ref, v_ref, lamw_ref, sub_ref, o_ref, q2_ref, mx_ref, acc_ref, kmax_ref, *,
                       tq, lam_init):
    i = pl.program_id(2)

    def key_tile(kb):
        return k_ref[0, :, pl.ds(pl.multiple_of(kb * tq, tq), tq)].astype(BF16)

    @pl.when(i == 0)
    def _():
        def body(kb, m):
            kf = key_tile(kb).astype(F32)
            return jnp.maximum(m, jnp.sum(kf * kf, axis=0, keepdims=True))

        m = lax.fori_loop(0, k_ref.shape[2] // tq, body, jnp.zeros((1, tq), F32))
        kmax_ref[...] = jnp.broadcast_to(jnp.max(m, axis=-1, keepdims=True), kmax_ref.shape)

    q = q_ref[0] * (DK_A ** -0.5 * LOG2E)
    lane = _iota((1, LANES), 1)
    q2_ref[0:tq] = jnp.where(lane < DK_A, q, 0.0).astype(BF16)
    q2_ref[tq:2 * tq] = jnp.where(lane >= DK_A, q, 0.0).astype(BF16)
    bound = _score_bound(q2_ref[...], kmax_ref[0:1])

    def scores(kb, masked):
        s = _dot(q2_ref[...], key_tile(kb))
        if masked:
            r = _iota((2 * tq, tq), 0)
            r = jnp.where(r >= tq, r - tq, r)
            s = jnp.where(_iota((2 * tq, tq), 1) <= r, s, MASK_NEG)
        return s

    def values(kb):
        return v_ref[0, pl.ds(pl.multiple_of(kb * tq, tq), tq), :].astype(BF16)

    o2 = _two_pass_attention(mx_ref, acc_ref, 0, i, i, scores, values, bound=bound)
    o_ref[0] = _diff_finalize(o2, tq, _diff_lambda(lamw_ref[...], lam_init), sub_ref[...], lam_init)


def _diff_attn_prompt(qa, kat, va, lam_w, sub_w, lam_init, tq):
    n, t, _ = qa.shape
    kern = functools.partial(_diff_flash_kernel, tq=tq, lam_init=lam_init)
    return pl.pallas_call(
        kern,
        grid=(n, H_A, t // tq),
        in_specs=[pl.BlockSpec((1, tq, LANES), lambda b, h, i: (b, i, h)),
                  pl.BlockSpec((1, LANES, t), lambda b, h, i: (b, h, 0)),
                  pl.BlockSpec((1, t, LANES), lambda b, h, i: (b, 0, h)),
                  _full_spec(lam_w.shape), _full_spec(sub_w.shape)],
        out_specs=pl.BlockSpec((1, tq, LANES), lambda b, h, i: (b, i, h)),
        out_shape=jax.ShapeDtypeStruct((n, t, H_A * DV_A), F32),
        scratch_shapes=[pltpu.VMEM((2 * tq, LANES), BF16), pltpu.VMEM((2 * tq, LANES), F32),
                        pltpu.VMEM((2 * tq, 2 * LANES), F32), pltpu.VMEM((8, LANES), F32)],
        compiler_params=_cparams(("parallel", "parallel", "arbitrary")),
        name="diff_attn_prompt",
    )(qa, kat, va, lam_w, sub_w)


def _sel_onehot(rows, first_block):
    blk = _iota((rows, LANES), 0) // SEL_BLOCK + first_block
    return jnp.where(blk == _iota((rows, LANES), 1), 1.0, 0.0).astype(BF16)


def _nsa_flash_kernel(*refs, tq, use_sel):
    if use_sel:
        q_ref, selm_ref, k_ref, v_ref, o_ref, q2_ref, mx_ref, acc_ref, kmax_ref = refs
    else:
        q_ref, k_ref, v_ref, o_ref, q2_ref, mx_ref, acc_ref, kmax_ref = refs
    i = pl.program_id(2)

    @pl.when(i == 0)
    def _():
        kmax_ref[...] = jnp.broadcast_to(_key_norm_bound(k_ref, k_ref.shape[1], tq), kmax_ref.shape)

    q = q_ref[0] * (DH_B ** -0.5 * LOG2E)
    for r in range(R_B):
        q2_ref[r * tq:(r + 1) * tq, 0:LANES] = q[:, r * LANES:(r + 1) * LANES].astype(BF16)
        if use_sel:
            q2_ref[r * tq:(r + 1) * tq, LANES:2 * LANES] = selm_ref[0, 0]
    bound = _score_bound(q2_ref[:, 0:LANES], kmax_ref[0:1])

    def scores(kb, masked):
        k = k_ref[0, pl.ds(pl.multiple_of(kb * tq, tq), tq), :].astype(BF16)
        if use_sel:
            k = jnp.concatenate([k, _sel_onehot(tq, kb * (tq // SEL_BLOCK))], axis=-1)
        s = _dot_nt(q2_ref[...], k)
        if masked:
            r = _iota((R_B * tq, tq), 0)
            qp = i * tq + jnp.where(r >= tq, r - tq, r)
            kp = kb * tq + _iota((R_B * tq, tq), 1)
            ok = kp <= qp
            if not use_sel:
                ok = ok & (kp > qp - WINDOW)
            s = jnp.where(ok, s, MASK_NEG)
        return s

    def values(kb):
        return v_ref[0, pl.ds(pl.multiple_of(kb * tq, tq), tq), :].astype(BF16)

    lo = 0 if use_sel else jnp.maximum(i - WINDOW // tq, 0)
    o = _two_pass_attention(mx_ref, acc_ref, lo, i, i, scores, values, mask_body=not use_sel, bound=bound)
    for r in range(R_B):
        o_ref[0, :, r * LANES:(r + 1) * LANES] = o[r * tq:(r + 1) * tq]


def _nsa_attn_prompt(qb, kv, selm, tq):
    n, t, _ = qb.shape
    use_sel = selm is not None
    kd = 2 * LANES if use_sel else LANES
    kern = functools.partial(_nsa_flash_kernel, tq=tq, use_sel=use_sel)
    in_specs = [pl.BlockSpec((1, tq, R_B * LANES), lambda b, g, i: (b, i, g))]
    args = [qb]
    if use_sel:
        in_specs.append(pl.BlockSpec((1, 1, tq, LANES), lambda b, g, i: (b, g, i, 0)))
        args.append(selm)
    in_specs += [pl.BlockSpec((1, t, LANES), lambda b, g, i: (b, 0, g)),
                 pl.BlockSpec((1, t, LANES), lambda b, g, i: (b, 0, G_B + g))]
    args += [kv, kv]
    return pl.pallas_call(
        kern,
        grid=(n, G_B, t // tq),
        in_specs=in_specs,
        out_specs=pl.BlockSpec((1, tq, R_B * LANES), lambda b, g, i: (b, i, g)),
        out_shape=jax.ShapeDtypeStruct((n, t, H_B * DH_B), F32),
        scratch_shapes=[pltpu.VMEM((R_B * tq, kd), BF16), pltpu.VMEM((R_B * tq, LANES), F32),
                        pltpu.VMEM((R_B * tq, 2 * LANES), F32), pltpu.VMEM((8, LANES), F32)],
        compiler_params=_cparams(("parallel", "parallel", "arbitrary")),
        name="nsa_sel_prompt" if use_sel else "nsa_win_prompt",
    )(*args)


def _compress_core(load, wc_ref, pe_ref, w2_ref, wi, nchunk):
    rows = G_B * nchunk
    acc = jnp.zeros((rows, 2 * LANES), F32)
    pew = jnp.zeros((16, 2 * LANES), F32)
    for u in range(CMP_STRIDE // 2):
        lhs = jnp.concatenate([load(2 * u), load(2 * u + 1)], axis=-1).astype(BF16)
        w = wc_ref[wi, u]
        acc = acc + _dot(lhs, w)
        pew = pew + _dot(pe_ref[wi, u], w)
    first = acc[:, :LANES]
    second = pltpu.roll(acc[:, LANES:], rows - 1, 0)
    hid = first + second + pew[0:1, :LANES] + pew[8:9, LANES:]
    hid = hid * jax.nn.sigmoid(hid)
    return _dot(hid.astype(BF16), w2_ref[wi])


def _compress_prompt_kernel(*refs, nchunk):
    x_refs = refs[:G_B]
    wc_ref, pe_ref, w2_ref, o_ref = refs[G_B:]

    def load(tok):
        return jnp.concatenate([xr[0, pl.ds(tok, nchunk, stride=CMP_STRIDE), :] for xr in x_refs], axis=0)

    out = _compress_core(load, wc_ref, pe_ref, w2_ref, 0, nchunk)
    for g in range(G_B):
        o_ref[0, 0, g] = out[g * nchunk:(g + 1) * nchunk]


def _compress_sample_kernel(pt_ref, *refs, n_pages, spb):
    pg_refs = refs[:spb * n_pages]
    wc_ref, pe_ref, w2_ref, o_ref = refs[spb * n_pages:]
    cps = PAGE_SIZE // CMP_STRIDE
    nchunk = n_pages * cps
    for kv in range(2):
        def load(tok):
            return jnp.concatenate(
                [pg[0, pl.ds(tok * SLOTS + kv * G_B + g, cps, stride=CMP_STRIDE * SLOTS), :]
                 for smp in range(spb) for g in range(G_B) for pg in pg_refs[smp * n_pages:(smp + 1) * n_pages]],
                axis=0)

        out = _compress_core(load, wc_ref, pe_ref, w2_ref, kv, spb * nchunk)
        for smp in range(spb):
            for g in range(G_B):
                seg = smp * G_B + g
                o_ref[smp, kv, g] = out[seg * nchunk:(seg + 1) * nchunk]


def _compress_weights(cmp_pe, cmp_w1, cmp_w2):
    w1 = cmp_w1.reshape(2, CMP_LEN, DH_B, DH_B)
    wab = jnp.concatenate([w1[:, :CMP_STRIDE], w1[:, CMP_STRIDE:]], axis=-1)
    wc = wab.reshape(2, CMP_STRIDE // 2, 2 * DH_B, 2 * DH_B).astype(BF16)
    pa = cmp_pe[:, :CMP_STRIDE].reshape(2, CMP_STRIDE // 2, 1, 2 * DH_B)
    pb = cmp_pe[:, CMP_STRIDE:].reshape(2, CMP_STRIDE // 2, 1, 2 * DH_B)
    z = jnp.zeros((2, CMP_STRIDE // 2, 7, 2 * DH_B), F32)
    pe = jnp.concatenate([pa, z, pb, z], axis=2).astype(BF16)
    return wc, pe, cmp_w2.astype(BF16)


def _compress_prompt(cmp_kv, wc, pe, w2):
    n, t, _ = cmp_kv.shape
    nchunk = t // CMP_STRIDE
    kern = functools.partial(_compress_prompt_kernel, nchunk=nchunk)
    return pl.pallas_call(
        kern,
        grid=(n, 2),
        in_specs=[pl.BlockSpec((1, t, DH_B), functools.partial(lambda b, kv, g: (b, 0, kv * G_B + g), g=g))
                  for g in range(G_B)] + [
                  pl.BlockSpec((1,) + wc.shape[1:], lambda b, kv: (kv, 0, 0, 0)),
                  pl.BlockSpec((1,) + pe.shape[1:], lambda b, kv: (kv, 0, 0, 0)),
                  pl.BlockSpec((1, DH_B, DH_B), lambda b, kv: (kv, 0, 0))],
        out_specs=pl.BlockSpec((1, 1, G_B, nchunk, DH_B), lambda b, kv: (b, kv, 0, 0, 0)),
        out_shape=jax.ShapeDtypeStruct((n, 2, G_B, nchunk, DH_B), F32),
        compiler_params=_cparams(("parallel", "parallel")),
        name="compress_prompt",
    )(*([cmp_kv] * G_B), wc, pe, w2)


def _compress_sample(cache, page_table, wc, pe, w2):
    n, n_pages = page_table.shape
    nchunk = n_pages * (PAGE_SIZE // CMP_STRIDE)
    spb = math.gcd(n, 2)
    kern = functools.partial(_compress_sample_kernel, n_pages=n_pages, spb=spb)
    page_specs = [pl.BlockSpec((1, PAGE_SIZE * SLOTS, LANES),
                               functools.partial(lambda b, pt, smp, j: (pt[b * spb + smp, j], 0, 0), smp=smp, j=j))
                  for smp in range(spb) for j in range(n_pages)]
    grid_spec = pltpu.PrefetchScalarGridSpec(
        num_scalar_prefetch=1,
        grid=(n // spb,),
        in_specs=page_specs + [
            pl.BlockSpec(wc.shape, lambda b, pt: (0, 0, 0, 0)),
            pl.BlockSpec(pe.shape, lambda b, pt: (0, 0, 0, 0)),
            pl.BlockSpec(w2.shape, lambda b, pt: (0, 0, 0))],
        out_specs=pl.BlockSpec((spb, 2, G_B, nchunk, DH_B), lambda b, pt: (b, 0, 0, 0, 0)),
    )
    return pl.pallas_call(
        kern,
        grid_spec=grid_spec,
        out_shape=jax.ShapeDtypeStruct((n, 2, G_B, nchunk, DH_B), F32),
        compiler_params=_cparams(("parallel",)),
        name="compress_sample",
    )(page_table, *([cache] * (spb * n_pages)), wc, pe, w2)


def _cmp_topk_kernel(q_ref, kc_ref, vc_ref, ocmp_ref, selm_ref, p_scr, v_scr, *,
                     nb, rq, nck, n_cmp, n_sel, nsp, pos_base):
    tq = nb * rq
    j = pl.program_id(2)
    scale = DH_B ** -0.5
    kidx = _iota((1, nck), 1)
    qpos_c = pos_base + j * rq + _iota((rq, 1), 0)
    valid = (kidx < n_cmp) & (CMP_STRIDE * kidx + (CMP_LEN - 1) <= qpos_c)
    for b in range(nb):
        kcc = kc_ref[b, 0, 0].astype(BF16)
        vcc = vc_ref[b, 0, 0].astype(BF16)
        q = q_ref[b] * scale
        q2 = jnp.concatenate([q[:, r * LANES:(r + 1) * LANES] for r in range(R_B)], axis=0).astype(BF16)
        valid2 = jnp.concatenate([valid] * R_B, axis=0)
        s = jnp.where(valid2, _dot_nt(q2, kcc), MASK_NEG)
        m = jnp.max(s, axis=-1, keepdims=True)
        p = jnp.where(valid2, jnp.exp(s - m), 0.0)
        p = p / jnp.maximum(jnp.sum(p, axis=-1, keepdims=True), 1e-30)
        o = _dot(p.astype(BF16), vcc)
        psum = jnp.zeros((rq, nck), F32)
        for r in range(R_B):
            ocmp_ref[b, :, r * LANES:(r + 1) * LANES] = o[r * rq:(r + 1) * rq]
            psum = psum + p[r * rq:(r + 1) * rq]
        p_scr[b * rq:(b + 1) * rq, :] = psum
    psum = p_scr[...]
    p_hi = psum.astype(BF16)
    p_lo = (psum - p_hi.astype(F32)).astype(BF16)
    srow = _iota((LANES, nck), 0)
    kcol = _iota((LANES, nck), 1)
    cov = (CMP_STRIDE * kcol < SEL_BLOCK * srow + SEL_BLOCK) & (CMP_STRIDE * kcol + CMP_LEN > SEL_BLOCK * srow)
    cov = cov & (srow < n_sel) & (kcol < n_cmp)
    cov_t = jnp.where(cov, 1.0, 0.0).astype(BF16)
    imp_t = _dot_nt(cov_t, p_hi) + _dot_nt(cov_t, p_lo)
    qpos_l = pos_base + j * rq + _iota((1, tq), 1) % rq
    sb = _iota((nsp, 1), 0)
    valid_b = (sb < n_sel) & (sb * SEL_BLOCK <= qpos_l)
    forced = (sb == 0) | (sb == qpos_l // SEL_BLOCK)
    vals = jnp.where(forced, jnp.inf, jnp.where(valid_b, imp_t[:nsp], -jnp.inf))
    v_scr[...] = vals

    def body(jj, cnt):
        vj = v_scr[pl.ds(jj, 1), :]
        tie = jnp.where(sb > jj, 1.0, 0.0)
        return cnt + jnp.where(vj > vals, 1.0, jnp.where(vj == vals, tie, 0.0))

    cnt = lax.fori_loop(0, nsp, body, jnp.zeros((nsp, tq), F32))
    keep = valid_b & (cnt < float(min(TOP_N, n_sel)))
    selm_t = jnp.where(keep, 0.0, -SEL_NEG)
    if nsp < LANES:
        selm_t = jnp.concatenate([selm_t, jnp.zeros((LANES - nsp, tq), F32)], axis=0)
    selm = selm_t.T.astype(BF16)
    for b in range(nb):
        selm_ref[b, 0] = selm[b * rq:(b + 1) * rq]


def _cmp_topk(qb, ccmp, *, nb, rq, n_cmp, n_sel, pos_base):
    n, r, _ = qb.shape
    nck = ccmp.shape[3]
    nsp = -(-n_sel // 8) * 8
    tq = nb * rq
    kern = functools.partial(_cmp_topk_kernel, nb=nb, rq=rq, nck=nck, n_cmp=n_cmp, n_sel=n_sel, nsp=nsp,
                             pos_base=pos_base)
    return pl.pallas_call(
        kern,
        grid=(n // nb, G_B, r // rq),
        in_specs=[pl.BlockSpec((nb, rq, R_B * LANES), lambda a, g, j: (a, j, g)),
                  pl.BlockSpec((nb, 1, 1, nck, DH_B), lambda a, g, j: (a, 0, g, 0, 0)),
                  pl.BlockSpec((nb, 1, 1, nck, DH_B), lambda a, g, j: (a, 1, g, 0, 0))],
        out_specs=[pl.BlockSpec((nb, rq, R_B * LANES), lambda a, g, j: (a, j, g)),
                   pl.BlockSpec((nb, 1, rq, LANES), lambda a, g, j: (a, g, j, 0))],
        out_shape=[jax.ShapeDtypeStruct((n, r, H_B * DH_B), F32),
                   jax.ShapeDtypeStruct((n, G_B, r, LANES), BF16)],
        scratch_shapes=[pltpu.VMEM((tq, nck), F32), pltpu.VMEM((nsp, tq), F32)],
        compiler_params=_cparams(("parallel", "parallel", "parallel")),
        name="cmp_topk",
    )(qb, ccmp, ccmp)


def _attend_scores(pieces):
    m = None
    for s, _ in pieces:
        ms = jnp.max(s, axis=-1, keepdims=True)
        m = ms if m is None else jnp.maximum(m, ms)
    l = jnp.zeros_like(m)
    acc = jnp.zeros((m.shape[0], LANES), F32)
    for s, v in pieces:
        p = jnp.exp(s - m)
        l = l + jnp.sum(p, axis=-1, keepdims=True)
        acc = acc + _dot(p.astype(BF16), v)
    return acc / l


def _attend_pieces(q2, pieces):
    scored = []
    for k, v, mask in pieces:
        s = _dot_nt(q2, k)
        scored.append((s if mask is None else jnp.where(mask, s, MASK_NEG), v))
    return _attend_scores(scored)


def _pad_rows(x, rows):
    return jnp.concatenate([x, jnp.zeros((rows - x.shape[0], x.shape[1]), x.dtype)], axis=0)


def _new_token_mask(m_rows, s):
    r = _iota((m_rows, LANES), 0) % s
    return _iota((m_rows, LANES), 1) <= r


def _diff_sample_kernel(pt_ref, *refs, n_pages, s, lam_init):
    kp_refs = refs[:n_pages]
    vp_refs = refs[n_pages:2 * n_pages]
    q_ref, kn_ref, vn_ref, lamw_ref, sub_ref, o_ref = refs[2 * n_pages:]
    lane = _iota((1, LANES), 1)
    lam = _diff_lambda(lamw_ref[...], lam_init)
    new_mask = _new_token_mask(2 * s, s)
    for h in range(H_A):
        cols = slice(h * LANES, (h + 1) * LANES)
        q = q_ref[0, :, cols] * (DK_A ** -0.5)
        q2 = jnp.concatenate([jnp.where(lane < DK_A, q, 0.0), jnp.where(lane >= DK_A, q, 0.0)], axis=0).astype(BF16)
        kt_past = jnp.concatenate([r[0, cols, :].astype(BF16) for r in kp_refs], axis=1)
        s_past = _dot(q2, kt_past)
        vpast = jnp.concatenate([_slot_rows(r, h, PAGE_SIZE).astype(BF16) for r in vp_refs], axis=0)
        knew = _pad_rows(kn_ref[0, :, cols], LANES).astype(BF16)
        vnew = _pad_rows(vn_ref[0, :, cols], LANES).astype(BF16)
        s_new = jnp.where(new_mask, _dot_nt(q2, knew), MASK_NEG)
        o2 = _attend_scores([(s_past, vpast), (s_new, vnew)])
        o_ref[0, :, cols] = _diff_finalize(o2, s, lam, sub_ref[...], lam_init)


def _page_specs(n_pages, rows, width):
    return [pl.BlockSpec((1, rows, width), functools.partial(lambda b, pt, j: (pt[b, j], 0, 0), j=j))
            for j in range(n_pages)]


def _slot_rows(ref, slot, n_tok):
    return ref[0, pl.ds(slot, n_tok, stride=SLOTS), :]


def _diff_attn_sample(qa, ka, va, cache_k, cache_v, page_table, lam_w, sub_w, lam_init):
    n, s, _ = qa.shape
    n_pages = page_table.shape[1]
    kern = functools.partial(_diff_sample_kernel, n_pages=n_pages, s=s, lam_init=lam_init)
    row_spec = pl.BlockSpec((1, s, 512), lambda b, pt: (b, 0, 0))
    grid_spec = pltpu.PrefetchScalarGridSpec(
        num_scalar_prefetch=1,
        grid=(n,),
        in_specs=_page_specs(n_pages, 512, PAGE_SIZE) + _page_specs(n_pages, PAGE_SIZE * SLOTS, LANES)
        + [row_spec, row_spec, row_spec,
           pl.BlockSpec(lam_w.shape, lambda b, pt: (0, 0)), pl.BlockSpec(sub_w.shape, lambda b, pt: (0, 0))],
        out_specs=row_spec,
    )
    return pl.pallas_call(
        kern,
        grid_spec=grid_spec,
        out_shape=jax.ShapeDtypeStruct((n, s, 512), F32),
        compiler_params=_cparams(("parallel",)),
        name="diff_attn_sample",
    )(page_table, *([cache_k] * n_pages), *([cache_v] * n_pages), qa, ka, va, lam_w, sub_w)


def _sel_sample_kernel(pt_ref, *refs, n_pages, s):
    pg_refs = refs[:n_pages]
    q_ref, selm_ref, new_ref, o_ref = refs[n_pages:]
    past = n_pages * PAGE_SIZE
    new_mask = _new_token_mask(R_B * s, s)
    oh_past = _sel_onehot(past, 0)
    new_blk = past // SEL_BLOCK
    oh_new = jnp.where((_iota((LANES, LANES), 1) == new_blk) & (_iota((LANES, LANES), 0) < s), 1.0, 0.0)
    oh_new = oh_new.astype(BF16)
    scale = DH_B ** -0.5
    for g in range(G_B):
        kcols = slice(g * LANES, (g + 1) * LANES)
        vcols = slice((G_B + g) * LANES, (G_B + g + 1) * LANES)
        selm = selm_ref[0, g]
        q2 = jnp.concatenate(
            [jnp.concatenate([(q_ref[0, :, (g * R_B + r) * LANES:(g * R_B + r + 1) * LANES] * scale).astype(BF16),
                              selm], axis=-1) for r in range(R_B)], axis=0)
        kpast = jnp.concatenate([_slot_rows(r, g, PAGE_SIZE).astype(BF16) for r in pg_refs], axis=0)
        vpast = jnp.concatenate([_slot_rows(r, G_B + g, PAGE_SIZE).astype(BF16) for r in pg_refs], axis=0)
        kpast = jnp.concatenate([kpast, oh_past], axis=-1)
        knew = jnp.concatenate([_pad_rows(new_ref[0, :, kcols], LANES).astype(BF16), oh_new], axis=-1)
        vnew = _pad_rows(new_ref[0, :, vcols], LANES).astype(BF16)
        o2 = _attend_pieces(q2, [(kpast, vpast, None), (knew, vnew, new_mask)])
        for r in range(R_B):
            o_ref[0, :, (g * R_B + r) * LANES:(g * R_B + r + 1) * LANES] = o2[r * s:(r + 1) * s]


def _sel_attn_sample(qb, selm, sel_new, cache, page_table):
    n, s, _ = qb.shape
    n_pages = page_table.shape[1]
    assert s <= SEL_BLOCK and (n_pages * PAGE_SIZE) % SEL_BLOCK == 0
    kern = functools.partial(_sel_sample_kernel, n_pages=n_pages, s=s)
    row_spec = pl.BlockSpec((1, s, 512), lambda b, pt: (b, 0, 0))
    grid_spec = pltpu.PrefetchScalarGridSpec(
        num_scalar_prefetch=1,
        grid=(n,),
        in_specs=_page_specs(n_pages, PAGE_SIZE * SLOTS, LANES)
        + [row_spec, pl.BlockSpec((1, G_B, s, LANES), lambda b, pt: (b, 0, 0, 0)), row_spec],
        out_specs=row_spec,
    )
    return pl.pallas_call(
        kern,
        grid_spec=grid_spec,
        out_shape=jax.ShapeDtypeStruct((n, s, 512), F32),
        compiler_params=_cparams(("parallel",)),
        name="nsa_sel_sample",
    )(page_table, *([cache] * n_pages), qb, selm, sel_new)


def _win_sample_kernel(q_ref, buf_ref, new_ref, o_ref, wout_ref, *, s, wb, spb):
    new_mask = _new_token_mask(R_B * s, s)
    r = _iota((R_B * s, wb), 0) % s
    buf_mask = _iota((R_B * s, wb), 1) > r + (wb - WINDOW)
    scale = DH_B ** -0.5
    keep = (wb - s) * SLOTS
    for b in range(spb):
        for g in range(G_B):
            kcols = slice(g * LANES, (g + 1) * LANES)
            vcols = slice((G_B + g) * LANES, (G_B + g + 1) * LANES)
            q2 = jnp.concatenate(
                [(q_ref[b, :, (g * R_B + r_) * LANES:(g * R_B + r_ + 1) * LANES] * scale).astype(BF16)
                 for r_ in range(R_B)], axis=0)
            kbuf = buf_ref[b, pl.ds(g, wb, stride=SLOTS), :].astype(BF16)
            vbuf = buf_ref[b, pl.ds(G_B + g, wb, stride=SLOTS), :].astype(BF16)
            knew = _pad_rows(new_ref[b, :, kcols], LANES).astype(BF16)
            vnew = _pad_rows(new_ref[b, :, vcols], LANES).astype(BF16)
            o2 = _attend_pieces(q2, [(kbuf, vbuf, buf_mask), (knew, vnew, new_mask)])
            for r_ in range(R_B):
                o_ref[b, :, (g * R_B + r_) * LANES:(g * R_B + r_ + 1) * LANES] = o2[r_ * s:(r_ + 1) * s]
        wout_ref[b, 0:keep] = buf_ref[b, s * SLOTS:wb * SLOTS]
        for slot in range(SLOTS):
            wout_ref[b, pl.ds(keep + slot, s, stride=SLOTS), :] = new_ref[b, :, slot * LANES:(slot + 1) * LANES]


def _win_attn_sample(qb, win_buf, win_new, past):
    n, s, _ = qb.shape
    wb = win_buf.shape[1] // SLOTS
    assert past >= wb and wb % 8 == 0 and s % 8 == 0
    spb = math.gcd(n, 4)
    kern = functools.partial(_win_sample_kernel, s=s, wb=wb, spb=spb)
    row_spec = pl.BlockSpec((spb, s, 512), lambda b: (b, 0, 0))
    buf_spec = pl.BlockSpec((spb, wb * SLOTS, LANES), lambda b: (b, 0, 0))
    return pl.pallas_call(
        kern,
        grid=(n // spb,),
        in_specs=[row_spec, buf_spec, row_spec],
        out_specs=[row_spec, buf_spec],
        out_shape=[jax.ShapeDtypeStruct((n, s, 512), F32), jax.ShapeDtypeStruct((n, wb * SLOTS, LANES), F32)],
        compiler_params=_cparams(("parallel",)),
        name="nsa_win_sample",
    )(qb, win_buf, win_new)


def _even_out_kernel(y_ref, g1_ref, oa_ref, oc_ref, os_ref, ow_ref, gt_ref, w_ref, o_ref):
    gt = gt_ref[...]
    parts = [oa_ref[...].astype(BF16)]
    for hb in range(H_B):
        sl = slice(hb * LANES, (hb + 1) * LANES)
        ob = (gt[:, 3 * hb:3 * hb + 1] * oc_ref[:, sl] + gt[:, 3 * hb + 1:3 * hb + 2] * os_ref[:, sl]
              + gt[:, 3 * hb + 2:3 * hb + 3] * ow_ref[:, sl])
        parts.append(ob.astype(BF16))
    out = _dot(jnp.concatenate(parts, axis=-1), w_ref[...])
    y = y_ref[...]
    o_ref[...] = y + g1_ref[0] * out.reshape(y.shape)


def _even_out(til, y, mods, oa, oc, os_, ow, gt, w_out):
    return pl.pallas_call(
        _even_out_kernel,
        grid=til.grid,
        in_specs=[til.x_spec(D_MODEL), til.mod_spec(2), til.flat_spec(512), til.flat_spec(512),
                  til.flat_spec(512), til.flat_spec(512), til.flat_spec(LANES), _full_spec(w_out.shape)],
        out_specs=til.x_spec(D_MODEL),
        out_shape=jax.ShapeDtypeStruct(y.shape, F32),
        compiler_params=_cparams(("parallel", "parallel")),
        name="even_out",
    )(y, mods, oa, oc, os_, ow, gt, w_out)


def _mlp_kernel(y_ref, sh_ref, sc_ref, g_ref, nw_ref, w1_ref, w2_ref, fw_ref, o_ref, *, final):
    y = y_ref[...]
    h = _norm_mod(y, nw_ref[...], sh_ref[0], sc_ref[0]).reshape(-1, D_MODEL).astype(BF16)
    a = jnp.maximum(_dot(h, w1_ref[...]), 0.0)
    out = _dot((a * a).astype(BF16), w2_ref[...])
    y2 = y + g_ref[0] * out.reshape(y.shape)
    if final:
        y2 = _rms(y2, fw_ref[...])
    o_ref[...] = y2


def _mlp(til, y, mods, nw, w1, w2, fw, final):
    kern = functools.partial(_mlp_kernel, final=final)
    return pl.pallas_call(
        kern,
        grid=til.grid,
        in_specs=[til.x_spec(D_MODEL), til.mod_spec(3), til.mod_spec(4), til.mod_spec(5),
                  _full_spec((1, D_MODEL)), _full_spec(w1.shape), _full_spec(w2.shape), _full_spec((1, D_MODEL))],
        out_specs=til.x_spec(D_MODEL),
        out_shape=jax.ShapeDtypeStruct(y.shape, F32),
        compiler_params=_cparams(("parallel", "parallel")),
        name="mlp",
    )(y, mods, mods, mods, nw, w1, w2, fw)


def _gla_inproj_kernel(x_ref, sh_ref, sc_ref, nw_ref, w_ref, wgl_ref, wgate_ref, bgate_ref,
                       q_ref, k_ref, v_ref, r_ref, la_ref):
    h = _norm_mod(x_ref[...], nw_ref[...], sh_ref[0], sc_ref[0]).reshape(-1, D_MODEL).astype(BF16)
    nk = H_C * DK_C
    nv = H_C * DV_C
    q_ref[...] = _dot(h, w_ref[:, 0:nk]) * (DK_C ** -0.5)
    k_ref[...] = _dot(h, w_ref[:, nk:2 * nk])
    v_ref[...] = _dot(h, w_ref[:, 2 * nk:2 * nk + nv])
    r_ref[...] = _dot(h, w_ref[:, 2 * nk + nv:2 * nk + 2 * nv])
    gl = _dot(h, wgl_ref[...])
    x = _dot(gl.astype(BF16), wgate_ref[...]) + bgate_ref[...]
    log_sig = jnp.minimum(x, 0.0) - jnp.log1p(jnp.exp(-jnp.abs(x)))
    la_ref[...] = log_sig / GATE_TAU


def _gla_inproj(til, x, mods, nw, w_main, w_gl, w_gate, b_gate):
    rows_total = til.nb * til.r
    widths = (H_C * DK_C, H_C * DK_C, H_C * DV_C, H_C * DV_C, H_C * DK_C)
    return pl.pallas_call(
        _gla_inproj_kernel,
        grid=til.grid,
        in_specs=[til.x_spec(D_MODEL), til.mod_spec(0), til.mod_spec(1), _full_spec((1, D_MODEL)),
                  _full_spec(w_main.shape), _full_spec(w_gl.shape), _full_spec(w_gate.shape),
                  _full_spec(b_gate.shape)],
        out_specs=[til.flat_spec(c) for c in widths],
        out_shape=[jax.ShapeDtypeStruct((rows_total, c), F32) for c in widths],
        compiler_params=_cparams(("parallel", "parallel")),
        name="gla_inproj",
    )(x, mods, mods, nw, w_main, w_gl, w_gate, b_gate)


def _gla_prep(q, k, g, c, sub, n_real):
    tri = jnp.where(_iota((c, c), 0) >= _iota((c, c), 1), 1.0, 0.0)
    b = jnp.dot(tri, g, preferred_element_type=F32, precision=lax.Precision.HIGHEST)
    qe = (q * jnp.exp(b)).astype(BF16)
    lane_c = _iota((sub, c), 1)
    row_s = _iota((sub, c), 0)
    att_rows = []
    for blk in range(c // sub):
        lo = blk * sub
        qi, ki, bi = q[lo:lo + sub], k[lo:lo + sub], b[lo:lo + sub]
        diag = jnp.zeros((sub, c), F32)
        for jj in range(min(sub, max(n_real - lo, 0))):
            e = jnp.exp(jnp.minimum(bi - bi[jj:jj + 1], 0.0))
            col = jnp.sum(qi * ki[jj:jj + 1] * e, axis=-1, keepdims=True)
            diag = jnp.where(lane_c == lo + jj, col, diag)
        att = jnp.where(lane_c - lo <= row_s, diag, 0.0)
        if blk > 0:
            bs = b[lo - 1:lo]
            q_in = qi * jnp.exp(bi - bs)
            k_out = k * jnp.exp(jnp.minimum(bs - b, 0.0))
            att = jnp.where(lane_c < lo, _dot_nt(q_in.astype(BF16), k_out.astype(BF16)), att)
        att_rows.append(att)
    att = att_rows[0] if len(att_rows) == 1 else jnp.concatenate(att_rows, axis=0)
    bl = b[c - 1:c]
    kd = (k * jnp.exp(bl - b)).astype(BF16)
    eye = _iota((DK_C, DK_C), 0) == _iota((DK_C, DK_C), 1)
    decay = jnp.sum(jnp.where(eye, jnp.exp(bl), 0.0), axis=-1, keepdims=True)
    return qe, att, kd, decay


def _gla_apply(state, prep, v):
    qe, att, kd, decay = prep
    o = _dot(qe, state.astype(BF16)) + _dot(att, v)
    return o, decay * state + _dot_tn(kd, v.astype(BF16))


def _gla_rec_kernel(*refs, tt, c, sub, hp, has_s0):
    if has_s0:
        q_ref, k_ref, v_ref, g_ref, s0_ref, o_ref, sfin_ref, s_ref = refs
    else:
        q_ref, k_ref, v_ref, g_ref, o_ref, sfin_ref, s_ref = refs
    t = pl.program_id(2)

    @pl.when(t == 0)
    def _():
        s_ref[...] = s0_ref[0] if has_s0 else jnp.zeros(s_ref.shape, F32)

    def kcols(hh):
        return slice(hh * DK_C, (hh + 1) * DK_C)

    def vcols(hh):
        return slice(hh * DV_C, (hh + 1) * DV_C)

    if tt < c:
        pad = lambda x: _pad_rows(x, c)
        for hh in range(hp):
            prep = _gla_prep(pad(q_ref[0, :, kcols(hh)]), pad(k_ref[0, :, kcols(hh)]),
                             pad(g_ref[0, :, kcols(hh)]), c, sub, tt)
            o, s_ref[hh] = _gla_apply(s_ref[hh], prep, pad(v_ref[0, :, vcols(hh)]))
            o_ref[0, :, vcols(hh)] = o[:tt]
    else:
        per_trip = 2 if (tt // c) % 2 == 0 else 1

        def body(ci, carry):
            for hh in range(hp):
                rows = [pl.ds(pl.multiple_of((ci * per_trip + u) * c, c), c) for u in range(per_trip)]
                preps = [_gla_prep(q_ref[0, r, kcols(hh)], k_ref[0, r, kcols(hh)], g_ref[0, r, kcols(hh)],
                                   c, sub, c) for r in rows]
                state = s_ref[hh]
                for r, prep in zip(rows, preps):
                    o_ref[0, r, vcols(hh)], state = _gla_apply(state, prep, v_ref[0, r, vcols(hh)])
                s_ref[hh] = state
            return carry
        lax.fori_loop(0, tt // c // per_trip, body, 0)

    @pl.when(t == pl.num_programs(2) - 1)
    def _():
        sfin_ref[0] = s_ref[...]


def _gla_recurrence(q, k, v, g, s0, tt, c, sub, hp):
    n, t, _ = q.shape
    kern = functools.partial(_gla_rec_kernel, tt=tt, c=c, sub=sub, hp=hp, has_s0=s0 is not None)
    kspec = pl.BlockSpec((1, tt, hp * DK_C), lambda b, h, i: (b, i, h))
    vspec = pl.BlockSpec((1, tt, hp * DV_C), lambda b, h, i: (b, i, h))
    sspec = pl.BlockSpec((1, hp, DK_C, DV_C), lambda b, h, i: (b, h, 0, 0))
    in_specs = [kspec, kspec, vspec, kspec]
    args = [q, k, v, g]
    if s0 is not None:
        in_specs.append(sspec)
        args.append(s0)
    return pl.pallas_call(
        kern,
        grid=(n, H_C // hp, t // tt),
        in_specs=in_specs,
        out_specs=[vspec, sspec],
        out_shape=[jax.ShapeDtypeStruct((n, t, H_C * DV_C), F32),
                   jax.ShapeDtypeStruct((n, H_C, DK_C, DV_C), F32)],
        scratch_shapes=[pltpu.VMEM((hp, DK_C, DV_C), F32)],
        compiler_params=_cparams(("parallel", "parallel", "arbitrary")),
        name="gla_recurrence",
    )(*args)


def _gla_out_kernel(y_ref, g1_ref, o_ref_in, r_ref, nw_ref, w_ref, out_ref):
    parts = []
    for h in range(H_C):
        sl = slice(h * DV_C, (h + 1) * DV_C)
        r = r_ref[:, sl]
        parts.append((_rms(o_ref_in[:, sl], nw_ref[...]) * (r * jax.nn.sigmoid(r))).astype(BF16))
    out = _dot(jnp.concatenate(parts, axis=-1), w_ref[...])
    y = y_ref[...]
    out_ref[...] = y + g1_ref[0] * out.reshape(y.shape)


def _gla_out(til, y, mods, o, r, nw, w_out):
    return pl.pallas_call(
        _gla_out_kernel,
        grid=til.grid,
        in_specs=[til.x_spec(D_MODEL), til.mod_spec(2), til.flat_spec(H_C * DV_C), til.flat_spec(H_C * DV_C),
                  _full_spec(nw.shape), _full_spec(w_out.shape)],
        out_specs=til.x_spec(D_MODEL),
        out_shape=jax.ShapeDtypeStruct(y.shape, F32),
        compiler_params=_cparams(("parallel", "parallel")),
        name="gla_out",
    )(y, mods, o, r, nw, w_out)


def _rope_tables(pos, d):
    inv = ROPE_THETA ** (-jnp.arange(0, d, 2, dtype=F32) / d)
    ang = pos.astype(F32)[:, None] * inv[None, :]
    cos, sin = jnp.cos(ang), jnp.sin(ang)
    rep = LANES // d
    c = jnp.tile(jnp.concatenate([cos, cos], axis=-1), (1, rep))
    s = jnp.tile(jnp.concatenate([-sin, sin], axis=-1), (1, rep))
    return c, s


def _rope_tables_t(pos, d):
    inv = ROPE_THETA ** (-jnp.arange(0, d, 2, dtype=F32) / d)
    ang = inv[:, None] * pos.astype(F32)[None, :]
    return jnp.cos(ang), jnp.sin(ang)


def _mods_for(mod_l, lo, hi):
    nb = hi - lo
    return mod_l[lo:hi].reshape(nb, 6, D_MODEL).transpose(1, 0, 2).reshape(6, nb, 1, D_MODEL)


def _prompt_tile_rows(t):
    return math.gcd(t, 256)


def _even_layer(yp, ys, mods_p, mods_s, caches, page_table, wts, lam_init, til_p, til_s):
    (c_dk, c_dv, c_cmp, c_sel, win_buf) = caches
    n, t, _ = yp.shape
    ns, s, _ = ys.shape
    n_pages = page_table.shape[1]
    past = n_pages * PAGE_SIZE
    w_in = wts["w_in"]
    n_main = w_in.shape[1] - 3 * H_B
    w_main = w_in[:, :n_main].astype(BF16)
    w_gate = jnp.pad(w_in[:, n_main:], ((0, 0), (0, LANES - 3 * H_B))).astype(BF16)
    nw1 = wts["norm1"].reshape(1, D_MODEL)
    lam_w = wts["lam_w"]
    sub_w = wts["sub_w"].reshape(1, DV_A)
    wc, pe, w2c = _compress_weights(wts["cmp_pe"], wts["cmp_w1"], wts["cmp_w2"])
    w_out = wts["w_out"].astype(BF16)

    tabs_p = _rope_tables(jnp.arange(t), DK_A) + _rope_tables(jnp.arange(t), DH_B)
    kt_tabs = _rope_tables_t(jnp.arange(t), DK_A)
    qa, kat, va, qb, cmp_kv, sel_kv, win_kv, gt = _even_inproj(til_p, yp, mods_p, nw1, w_main, w_gate, tabs_p, kt_tabs)
    r3 = lambda a: a.reshape(n, t, a.shape[-1])
    tq = _prompt_tile_rows(t)
    oa = _diff_attn_prompt(r3(qa), kat, r3(va), lam_w, sub_w, lam_init, tq)
    ccmp = _compress_prompt(r3(cmp_kv), wc, pe, w2c)
    n_cmp = (t - CMP_LEN) // CMP_STRIDE + 1
    n_sel = -(-t // SEL_BLOCK)
    o_cmp, selm = _cmp_topk(r3(qb), ccmp, nb=1, rq=tq, n_cmp=n_cmp, n_sel=n_sel, pos_base=0)
    o_sel = _nsa_attn_prompt(r3(qb), r3(sel_kv), selm, tq)
    o_win = _nsa_attn_prompt(r3(qb), r3(win_kv), None, tq)
    f2 = lambda a: a.reshape(n * t, a.shape[-1])
    yp = _even_out(til_p, yp, mods_p, f2(oa), f2(o_cmp), f2(o_sel), f2(o_win), gt, w_out)
    wl = min(WINDOW, t)
    ka_state = jnp.transpose(kat.reshape(n, H_A, 2, DK_A, t), (0, 4, 1, 2, 3))
    st_p = (ka_state, r3(va).reshape(n, t, H_A, DV_A),
            r3(cmp_kv).reshape(n, t, 2, G_B, DH_B), r3(sel_kv).reshape(n, t, 2, G_B, DH_B),
            r3(win_kv)[:, t - wl:].reshape(n, wl, 2, G_B, DH_B))

    pos_s = past + jnp.arange(s)
    tabs_s = tuple(jnp.tile(x, (til_s.b, 1)) for x in _rope_tables(pos_s, DK_A) + _rope_tables(pos_s, DH_B))
    qa, ka, va, qb, cmp_kv, sel_kv, win_kv, gt = _even_inproj(til_s, ys, mods_s, nw1, w_main, w_gate, tabs_s)
    r3 = lambda a: a.reshape(ns, s, a.shape[-1])
    slot_cache = lambda c: c.reshape(c.shape[0], PAGE_SIZE * SLOTS, LANES)
    c_dkt = jnp.transpose(c_dk, (0, 2, 3, 4, 1)).reshape(c_dk.shape[0], H_A * 2 * DK_A, PAGE_SIZE)
    oa = _diff_attn_sample(r3(qa), r3(ka), r3(va), c_dkt, slot_cache(c_dv), page_table, lam_w, sub_w, lam_init)
    ccmp = _compress_sample(slot_cache(c_cmp), page_table, wc, pe, w2c)
    total = past + s
    n_cmp = (total - CMP_LEN) // CMP_STRIDE + 1
    n_sel = -(-total // SEL_BLOCK)
    assert n_cmp <= ccmp.shape[3] and n_sel <= LANES
    nb = math.gcd(ns, LANES // s)
    o_cmp, selm = _cmp_topk(r3(qb), ccmp, nb=nb, rq=s, n_cmp=n_cmp, n_sel=n_sel, pos_base=past)
    o_sel = _sel_attn_sample(r3(qb), selm, r3(sel_kv), slot_cache(c_sel), page_table)
    wb = win_buf.shape[1]
    o_win, win_out = _win_attn_sample(r3(qb), win_buf.reshape(ns, wb * SLOTS, LANES), r3(win_kv), past)
    f2 = lambda a: a.reshape(ns * s, a.shape[-1])
    ys = _even_out(til_s, ys, mods_s, f2(oa), f2(o_cmp), f2(o_sel), f2(o_win), gt, w_out)
    st_s = (r3(ka).reshape(ns, s, H_A, 2, DK_A), r3(va).reshape(ns, s, H_A, DV_A),
            r3(cmp_kv).reshape(ns, s, 2, G_B, DH_B), r3(sel_kv).reshape(ns, s, 2, G_B, DH_B),
            win_out.reshape(ns, wb, 2, G_B, DH_B))
    return yp, ys, st_p, st_s


def _odd_layer(yp, ys, mods_p, mods_s, s0, wts, til_p, til_s):
    w_in = wts["w_in"]
    n_main = w_in.shape[1] - GATE_RANK
    w_main = w_in[:, :n_main].astype(BF16)
    w_gl = jnp.pad(w_in[:, n_main:], ((0, 0), (0, LANES - GATE_RANK))).astype(BF16)
    w_gate = jnp.pad(wts["w_gate"], ((0, LANES - GATE_RANK), (0, 0))).astype(BF16)
    b_gate = wts["b_gate"].reshape(1, -1)
    nw1 = wts["norm1"].reshape(1, D_MODEL)
    gnw = wts["gnorm"].reshape(1, DV_C)
    w_out = wts["w_out"].astype(BF16)
    outs = []
    for y, mods, til, state in ((yp, mods_p, til_p, None), (ys, mods_s, til_s, s0)):
        n, t, _ = y.shape
        q, k, v, r, la = _gla_inproj(til, y, mods, nw1, w_main, w_gl, w_gate, b_gate)
        r3 = lambda a: a.reshape(n, t, a.shape[-1])
        c = math.gcd(t, GLA_CHUNK)
        if c >= GLA_SUB:
            tt, cc, sub = math.gcd(t, 8 * c), c, GLA_SUB_LONG
        else:
            tt, cc, sub = t, GLA_SUB, GLA_SUB
        o, s_fin = _gla_recurrence(r3(q), r3(k), r3(v), r3(la), state, tt, cc, sub, H_C)
        y = _gla_out(til, y, mods, o.reshape(n * t, -1), r, gnw, w_out)
        outs.append((y, s_fin))
    return outs[0][0], outs[1][0], outs[0][1], outs[1][1]


def kernel(x_prompt, x_sample, c_prompt, c_sample, cache_diff_k, cache_diff_v, cache_cmp_kv, cache_sel_kv,
           state_win_kv, state_gla, page_table, norm1_w, norm2_w, ada_w, ada_b, even_w_in, even_w_out,
           diff_lambda_w, diff_subln_w, cmp_pe, cmp_w1, cmp_w2, gla_w_in, gla_w_gate, gla_b_gate, gla_norm_w,
           gla_w_out, mlp_w1, mlp_w2, final_norm_w):
    depth = ada_w.shape[0]
    n, t, _ = x_prompt.shape
    ns, s, _ = x_sample.shape
    til_p = _Tiling(n, t, 1, _prompt_tile_rows(t))
    til_s = _Tiling(ns, s, math.gcd(ns, 256 // s), s)

    pad = (-(n + ns)) % 8
    c_all = jnp.concatenate([c_prompt, c_sample, jnp.zeros((pad, D_MODEL), F32)], axis=0)
    mod = _adaln(c_all, ada_w, ada_b)

    yp, ys = x_prompt, x_sample
    st_p = [[] for _ in range(6)]
    st_s = [[] for _ in range(6)]
    fw = final_norm_w.reshape(1, D_MODEL)
    for l in range(depth):
        mods_p = _mods_for(mod[l], 0, n)
        mods_s = _mods_for(mod[l], n, n + ns)
        if l % 2 == 0:
            e = l // 2
            lam_init = 0.8 - 0.6 * math.exp(-0.3 * l)
            wts = dict(w_in=even_w_in[e], w_out=even_w_out[e], lam_w=diff_lambda_w[e], sub_w=diff_subln_w[e],
                       cmp_pe=cmp_pe[e], cmp_w1=cmp_w1[e], cmp_w2=cmp_w2[e], norm1=norm1_w[l])
            caches = (cache_diff_k[e], cache_diff_v[e], cache_cmp_kv[e], cache_sel_kv[e], state_win_kv[e])
            yp, ys, sp, ss = _even_layer(yp, ys, mods_p, mods_s, caches, page_table, wts, lam_init, til_p, til_s)
            for i in range(5):
                st_p[i].append(sp[i])
                st_s[i].append(ss[i])
        else:
            o = l // 2
            wts = dict(w_in=gla_w_in[o], w_gate=gla_w_gate[o], b_gate=gla_b_gate[o], gnorm=gla_norm_w[o],
                       w_out=gla_w_out[o], norm1=norm1_w[l])
            yp, ys, gp, gs = _odd_layer(yp, ys, mods_p, mods_s, state_gla[o], wts, til_p, til_s)
            st_p[5].append(gp)
            st_s[5].append(gs)
        final = l == depth - 1
        w1 = mlp_w1[l].astype(BF16)
        w2 = mlp_w2[l].astype(BF16)
        nw2 = norm2_w[l].reshape(1, D_MODEL)
        yp = _mlp(til_p, yp, mods_p, nw2, w1, w2, fw, final)
        ys = _mlp(til_s, ys, mods_s, nw2, w1, w2, fw, final)
    outs_p = [jnp.stack(x, axis=0) for x in st_p]
    outs_s = [jnp.stack(x, axis=0) for x in st_s]
    return (yp, ys, *outs_p, *outs_s)
```

```python
import functools
import math

import jax
import jax.numpy as jnp
from jax import lax
from jax.experimental import pallas as pl
from jax.experimental.pallas import tpu as pltpu

F32 = jnp.float32
BF16 = jnp.bfloat16

D_MODEL = 1024
PAGE_SIZE = 128
H_A = 4
DK_A = 64
DV_A = 128
H_B = 4
G_B = 2
R_B = 2
DH_B = 128
CMP_LEN = 32
CMP_STRIDE = 16
SEL_BLOCK = 64
TOP_N = 16
WINDOW = 512
H_C = 4
DK_C = 128
DV_C = 256
GATE_RANK = 16
GATE_TAU = 16.0
GLA_CHUNK = 64
GLA_SUB = 16
GLA_SUB_LONG = 16
D_FF = 4 * D_MODEL
ROPE_THETA = 10000.0
EPS = 1e-6

LANES = 128
SLOTS = 4
MASK_NEG = -1e30
LOG2E = 1.4426950408889634
SAFE_SHIFT = 56.0
SEL_NEG = 32768.0
VMEM_LIMIT_MB = 56


def _cparams(sem, vmem_mb=VMEM_LIMIT_MB):
    return pltpu.CompilerParams(dimension_semantics=sem, vmem_limit_bytes=vmem_mb * 1024 * 1024)


def _dot(a, b):
    return jnp.dot(a, b, preferred_element_type=F32)


def _dot_nt(a, b):
    return lax.dot_general(a, b, (((1,), (1,)), ((), ())), preferred_element_type=F32)


def _dot_tn(a, b):
    return lax.dot_general(a, b, (((0,), (0,)), ((), ())), preferred_element_type=F32)


def _iota(shape, dim):
    return lax.broadcasted_iota(jnp.int32, shape, dim)


def _rms(x, w):
    ms = jnp.mean(x * x, axis=-1, keepdims=True)
    return x * lax.rsqrt(ms + EPS) * w


def _norm_mod(x, nw, shift, scale):
    return _rms(x, nw) * (1.0 + scale) + shift


class _Tiling:
    def __init__(self, nb, r, b, rt):
        assert nb % b == 0 and r % rt == 0 and (b == 1 or rt == r)
        self.nb, self.r, self.b, self.rt = nb, r, b, rt
        self.grid = (nb // b, r // rt)
        self.rows = b * rt
        self.nrb = r // rt

    def x_spec(self, d):
        return pl.BlockSpec((self.b, self.rt, d), lambda i, j: (i, j, 0))

    def mod_spec(self, k):
        return pl.BlockSpec((1, self.b, 1, D_MODEL), lambda i, j: (k, i, 0, 0))

    def flat_spec(self, c):
        nrb = self.nrb
        return pl.BlockSpec((self.rows, c), lambda i, j: (i * nrb + j, 0))

    def tab_spec(self):
        return pl.BlockSpec((self.rows, LANES), lambda i, j: (j, 0))


def _full_spec(shape):
    nd = len(shape)
    return pl.BlockSpec(shape, lambda *_: (0,) * nd)


def _adaln_kernel(c_ref, w_ref, b_ref, o_ref):
    c = c_ref[...]
    a = (c * jax.nn.sigmoid(c)).astype(BF16)
    o_ref[0] = _dot(a, w_ref[0].astype(BF16)) + b_ref[0]


def _adaln(c_all, ada_w, ada_b):
    depth, d, n6 = ada_w.shape
    rows = c_all.shape[0]
    tn = 1536
    return pl.pallas_call(
        _adaln_kernel,
        grid=(depth, n6 // tn),
        in_specs=[pl.BlockSpec((rows, d), lambda l, j: (0, 0)),
                  pl.BlockSpec((1, d, tn), lambda l, j: (l, 0, j)),
                  pl.BlockSpec((1, 1, tn), lambda l, j: (l, 0, j))],
        out_specs=pl.BlockSpec((1, rows, tn), lambda l, j: (l, 0, j)),
        out_shape=jax.ShapeDtypeStruct((depth, rows, n6), F32),
        compiler_params=_cparams(("parallel", "parallel")),
        name="adaln",
    )(c_all, ada_w, ada_b.reshape(depth, 1, n6))


def _swap_half(x, half):
    if 2 * half == LANES:
        return pltpu.roll(x, half, 1)
    lane = _iota((1, LANES), 1)
    lo = (lane % (2 * half)) < half
    return jnp.where(lo, pltpu.roll(x, LANES - half, 1), pltpu.roll(x, half, 1))


def _even_inproj_kernel(*refs, k_feature_major):
    if k_feature_major:
        (x_ref, sh_ref, sc_ref, nw_ref, w_ref, wg_ref, c64_ref, s64_ref, c128_ref, s128_ref, wkt_ref, ct_ref, st_ref,
         qa_ref, ka_ref, va_ref, qb_ref, cmp_ref, sel_ref, win_ref, gt_ref) = refs
    else:
        (x_ref, sh_ref, sc_ref, nw_ref, w_ref, wg_ref, c64_ref, s64_ref, c128_ref, s128_ref,
         qa_ref, ka_ref, va_ref, qb_ref, cmp_ref, sel_ref, win_ref, gt_ref) = refs
    h = _norm_mod(x_ref[...], nw_ref[...], sh_ref[0], sc_ref[0])
    h = h.reshape(-1, D_MODEL).astype(BF16)
    c64, s64, c128, s128 = c64_ref[...], s64_ref[...], c128_ref[...], s128_ref[...]

    def rope64(p):
        return p * c64 + _swap_half(p, DK_A // 2) * s64

    def rope128(p):
        return p * c128 + _swap_half(p, DH_B // 2) * s128

    def project(ref, off, ropes):
        p = _dot(h, w_ref[:, off:off + 4 * LANES])
        for j, rope in enumerate(ropes):
            sl = slice(j * LANES, (j + 1) * LANES)
            ref[:, sl] = p[:, sl] if rope is None else rope(p[:, sl])

    project(qa_ref, 0, [rope64] * 4)
    if k_feature_major:
        kt = _dot_nt(wkt_ref[...], h)
        ct, st = ct_ref[...], st_ref[...]
        half = DK_A // 2
        for grp in range(H_A * 2):
            x1 = kt[grp * DK_A:grp * DK_A + half]
            x2 = kt[grp * DK_A + half:(grp + 1) * DK_A]
            ka_ref[0, grp * DK_A:grp * DK_A + half, :] = x1 * ct - x2 * st
            ka_ref[0, grp * DK_A + half:(grp + 1) * DK_A, :] = x2 * ct + x1 * st
    else:
        project(ka_ref, 512, [rope64] * 4)
    project(va_ref, 1024, [None] * 4)
    project(qb_ref, 1536, [rope128] * 4)
    for t, ref in enumerate((cmp_ref, sel_ref, win_ref)):
        project(ref, 2048 + t * 512, [rope128, rope128, None, None])
    gt_ref[...] = jax.nn.sigmoid(_dot(h, wg_ref[...]))


def _even_inproj(til, x, mods, nw, w_main, w_gate, tabs, kt_tabs=None):
    rows_total = til.nb * til.r
    widths = (512, 512, 512, 512, 512, 512, 512, LANES)
    in_specs = [til.x_spec(D_MODEL), til.mod_spec(0), til.mod_spec(1), _full_spec((1, D_MODEL)),
                _full_spec(w_main.shape), _full_spec(w_gate.shape)] + [til.tab_spec()] * 4
    args = [x, mods, mods, nw, w_main, w_gate, *tabs]
    out_specs = [til.flat_spec(c) for c in widths]
    out_shape = [jax.ShapeDtypeStruct((rows_total, c), F32) for c in widths]
    if kt_tabs is not None:
        assert til.b == 1
        w_kt = jnp.transpose(w_main[:, 512:1024])
        in_specs += [_full_spec(w_kt.shape)] + [pl.BlockSpec((DK_A // 2, til.rt), lambda i, j: (0, j))] * 2
        args += [w_kt, *kt_tabs]
        out_specs[1] = pl.BlockSpec((1, 512, til.rt), lambda i, j: (i, 0, j))
        out_shape[1] = jax.ShapeDtypeStruct((til.nb, 512, til.r), F32)
    return pl.pallas_call(
        functools.partial(_even_inproj_kernel, k_feature_major=kt_tabs is not None),
        grid=til.grid,
        in_specs=in_specs,
        out_specs=out_specs,
        out_shape=out_shape,
        compiler_params=_cparams(("parallel", "parallel")),
        name="even_inproj",
    )(*args)


def _for_tiles(lo, hi, fn):
    n = hi - lo

    def pair(j, carry):
        fn(lo + 2 * j)
        fn(lo + 2 * j + 1)
        return carry

    lax.fori_loop(0, n // 2, pair, 0)

    @pl.when(n % 2 == 1)
    def _():
        fn(hi - 1)


def _row_sumsq(x):
    xf = x.astype(F32)
    return _dot((xf * xf).astype(BF16), jnp.ones((LANES, LANES), BF16))


def _key_norm_bound(k_ref, n_keys, tile):
    def body(kb, m):
        k = k_ref[0, pl.ds(pl.multiple_of(kb * tile, tile), tile), :].astype(BF16)
        return jnp.maximum(m, _row_sumsq(k))

    m = lax.fori_loop(0, n_keys // tile, body, jnp.zeros((tile, LANES), F32))
    return jnp.max(m, axis=0, keepdims=True)


def _score_bound(q2, kmax2):
    qmax2 = jnp.max(_row_sumsq(q2), axis=0, keepdims=True)
    return jnp.sqrt(qmax2 * kmax2) * 1.05


def _two_pass_attention(mx_ref, acc_ref, lo, hi, last, scores, values, mask_body=False, bound=None):
    def lane_max(s):
        m = s[:, 0:LANES]
        for c in range(1, s.shape[1] // LANES):
            m = jnp.maximum(m, s[:, c * LANES:(c + 1) * LANES])
        return m

    def pass1(kb, masked):
        mx_ref[...] = jnp.maximum(mx_ref[...], lane_max(scores(kb, masked)))

    def exact_max():
        mx_ref[...] = jnp.full(mx_ref.shape, MASK_NEG, F32)
        _for_tiles(lo, hi, lambda kb: pass1(kb, mask_body))
        for kb in last:
            pass1(kb, True)
        mx_ref[...] = jnp.broadcast_to(jnp.max(mx_ref[...], axis=-1, keepdims=True), mx_ref.shape)

    if bound is None:
        exact_max()
    else:
        safe = jnp.max(bound) <= SAFE_SHIFT

        @pl.when(safe)
        def _():
            mx_ref[...] = jnp.broadcast_to(bound, mx_ref.shape)

        @pl.when(jnp.logical_not(safe))
        def _():
            exact_max()

    acc_ref[...] = jnp.zeros(acc_ref.shape, F32)

    def pass2(kb, masked):
        s = scores(kb, masked)
        m = mx_ref[...]
        p = jnp.concatenate([jnp.exp2(s[:, c * LANES:(c + 1) * LANES] - m) for c in range(s.shape[1] // LANES)],
                            axis=-1).astype(BF16)
        v = values(kb)
        v1 = jnp.concatenate([v, jnp.ones(v.shape, BF16)], axis=-1)
        acc_ref[...] += _dot(p, v1)

    _for_tiles(lo, hi, lambda kb: pass2(kb, mask_body))
    for kb in last:
        pass2(kb, True)
    return acc_ref[:, 0:LANES] / acc_ref[:, LANES:2 * LANES]


def _diff_lambda(lw, lam_init):
    a = jnp.sum(lw[0:1] * lw[1:2], axis=-1, keepdims=True)
    b = jnp.sum(lw[2:3] * lw[3:4], axis=-1, keepdims=True)
    return jnp.exp(a) - jnp.exp(b) + lam_init


def _diff_finalize(o2, tq, lam, sub_w, lam_init):
    od = o2[:tq] - lam * o2[tq:]
    return _rms(od, sub_w) * (1.0 - lam_init)


def _diff_flash_kernel(q_ref, k_ref, v_ref, lamw_ref, sub_ref, o_ref, q2_ref, mx_ref, acc_ref, kmax_ref, *,
                       tq, tk, lam_init):
    i = pl.program_id(2)

    def key_tile(kb):
        return k_ref[0, :, pl.ds(pl.multiple_of(kb * tk, tk), tk)].astype(BF16)

    @pl.when(i == 0)
    def _():
        def body(kb, m):
            kf = key_tile(kb).astype(F32)
            return jnp.maximum(m, jnp.sum(kf * kf, axis=0, keepdims=True))

        m = lax.fori_loop(0, k_ref.shape[2] // tk, body, jnp.zeros((1, tk), F32))
        kmax_ref[...] = jnp.broadcast_to(jnp.max(m, axis=-1, keepdims=True), kmax_ref.shape)

    q = q_ref[0] * (DK_A ** -0.5 * LOG2E)
    lane = _iota((1, LANES), 1)
    q2_ref[0:tq] = jnp.where(lane < DK_A, q, 0.0).astype(BF16)
    q2_ref[tq:2 * tq] = jnp.where(lane >= DK_A, q, 0.0).astype(BF16)
    bound = _score_bound(q2_ref[...], kmax_ref[0:1])

    def scores(kb, masked):
        s = _dot(q2_ref[...], key_tile(kb))
        if masked:
            r = _iota((2 * tq, tk), 0)
            qp = i * tq + jnp.where(r >= tq, r - tq, r)
            s = jnp.where(kb * tk + _iota((2 * tq, tk), 1) <= qp, s, MASK_NEG)
        return s

    def values(kb):
        return v_ref[0, pl.ds(pl.multiple_of(kb * tk, tk), tk), :].astype(BF16)

    per = tq // tk
    o2 = _two_pass_attention(mx_ref, acc_ref, 0, i * per, [i * per + u for u in range(per)], scores, values,
                             bound=bound)
    o_ref[0] = _diff_finalize(o2, tq, _diff_lambda(lamw_ref[...], lam_init), sub_ref[...], lam_init)


def _diff_attn_prompt(qa, kat, va, lam_w, sub_w, lam_init, tq, tk):
    n, t, _ = qa.shape
    kern = functools.partial(_diff_flash_kernel, tq=tq, tk=tk, lam_init=lam_init)
    return pl.pallas_call(
        kern,
        grid=(n, H_A, t // tq),
        in_specs=[pl.BlockSpec((1, tq, LANES), lambda b, h, i: (b, i, h)),
                  pl.BlockSpec((1, LANES, t), lambda b, h, i: (b, h, 0)),
                  pl.BlockSpec((1, t, LANES), lambda b, h, i: (b, 0, h)),
                  _full_spec(lam_w.shape), _full_spec(sub_w.shape)],
        out_specs=pl.BlockSpec((1, tq, LANES), lambda b, h, i: (b, i, h)),
        out_shape=jax.ShapeDtypeStruct((n, t, H_A * DV_A), F32),
        scratch_shapes=[pltpu.VMEM((2 * tq, LANES), BF16), pltpu.VMEM((2 * tq, LANES), F32),
                        pltpu.VMEM((2 * tq, 2 * LANES), F32), pltpu.VMEM((8, LANES), F32)],
        compiler_params=_cparams(("parallel", "parallel", "arbitrary")),
        name="diff_attn_prompt",
    )(qa, kat, va, lam_w, sub_w)


def _sel_onehot(rows, first_block):
    blk = _iota((rows, LANES), 0) // SEL_BLOCK + first_block
    return jnp.where(blk == _iota((rows, LANES), 1), 1.0, 0.0).astype(BF16)


def _nsa_flash_kernel(*refs, tq, tk, use_sel):
    if use_sel:
        q_ref, selm_ref, k_ref, v_ref, o_ref, q2_ref, mx_ref, acc_ref, kmax_ref = refs
    else:
        q_ref, k_ref, v_ref, o_ref, q2_ref, mx_ref, acc_ref, kmax_ref = refs
    i = pl.program_id(2)

    @pl.when(i == 0)
    def _():
        kmax_ref[...] = jnp.broadcast_to(_key_norm_bound(k_ref, k_ref.shape[1], tk), kmax_ref.shape)

    q = q_ref[0] * (DH_B ** -0.5 * LOG2E)
    for r in range(R_B):
        q2_ref[r * tq:(r + 1) * tq, 0:LANES] = q[:, r * LANES:(r + 1) * LANES].astype(BF16)
        if use_sel:
            q2_ref[r * tq:(r + 1) * tq, LANES:2 * LANES] = selm_ref[0, 0]
    bound = _score_bound(q2_ref[:, 0:LANES], kmax_ref[0:1])

    def scores(kb, masked):
        k = k_ref[0, pl.ds(pl.multiple_of(kb * tk, tk), tk), :].astype(BF16)
        if use_sel:
            k = jnp.concatenate([k, _sel_onehot(tk, kb * (tk // SEL_BLOCK))], axis=-1)
        s = _dot_nt(q2_ref[...], k)
        if masked:
            r = _iota((R_B * tq, tk), 0)
            qp = i * tq + jnp.where(r >= tq, r - tq, r)
            kp = kb * tk + _iota((R_B * tq, tk), 1)
            ok = kp <= qp
            if not use_sel:
                ok = ok & (kp > qp - WINDOW)
            s = jnp.where(ok, s, MASK_NEG)
        return s

    def values(kb):
        return v_ref[0, pl.ds(pl.multiple_of(kb * tk, tk), tk), :].astype(BF16)

    per = tq // tk
    lo = 0 if use_sel else jnp.maximum(i * per - WINDOW // tk, 0)
    o = _two_pass_attention(mx_ref, acc_ref, lo, i * per, [i * per + u for u in range(per)], scores, values,
                            mask_body=not use_sel, bound=bound)
    for r in range(R_B):
        o_ref[0, :, r * LANES:(r + 1) * LANES] = o[r * tq:(r + 1) * tq]


def _nsa_attn_prompt(qb, kv, selm, tq, tk):
    n, t, _ = qb.shape
    use_sel = selm is not None
    kd = 2 * LANES if use_sel else LANES
    kern = functools.partial(_nsa_flash_kernel, tq=tq, tk=tk, use_sel=use_sel)
    in_specs = [pl.BlockSpec((1, tq, R_B * LANES), lambda b, g, i: (b, i, g))]
    args = [qb]
    if use_sel:
        in_specs.append(pl.BlockSpec((1, 1, tq, LANES), lambda b, g, i: (b, g, i, 0)))
        args.append(selm)
    in_specs += [pl.BlockSpec((1, t, LANES), lambda b, g, i: (b, 0, g)),
                 pl.BlockSpec((1, t, LANES), lambda b, g, i: (b, 0, G_B + g))]
    args += [kv, kv]
    return pl.pallas_call(
        kern,
        grid=(n, G_B, t // tq),
        in_specs=in_specs,
        out_specs=pl.BlockSpec((1, tq, R_B * LANES), lambda b, g, i: (b, i, g)),
        out_shape=jax.ShapeDtypeStruct((n, t, H_B * DH_B), F32),
        scratch_shapes=[pltpu.VMEM((R_B * tq, kd), BF16), pltpu.VMEM((R_B * tq, LANES), F32),
                        pltpu.VMEM((R_B * tq, 2 * LANES), F32), pltpu.VMEM((8, LANES), F32)],
        compiler_params=_cparams(("parallel", "parallel", "arbitrary")),
        name="nsa_sel_prompt" if use_sel else "nsa_win_prompt",
    )(*args)


def _compress_core(load, wc_ref, pe_ref, w2_ref, wi, nchunk):
    rows = G_B * nchunk
    acc = jnp.zeros((rows, 2 * LANES), F32)
    pew = jnp.zeros((16, 2 * LANES), F32)
    for u in range(CMP_STRIDE // 2):
        lhs = jnp.concatenate([load(2 * u), load(2 * u + 1)], axis=-1).astype(BF16)
        w = wc_ref[wi, u]
        acc = acc + _dot(lhs, w)
        pew = pew + _dot(pe_ref[wi, u], w)
    first = acc[:, :LANES]
    second = pltpu.roll(acc[:, LANES:], rows - 1, 0)
    hid = first + second + pew[0:1, :LANES] + pew[8:9, LANES:]
    hid = hid * jax.nn.sigmoid(hid)
    return _dot(hid.astype(BF16), w2_ref[wi])


def _compress_prompt_kernel(*refs, nchunk):
    x_refs = refs[:G_B]
    wc_ref, pe_ref, w2_ref, o_ref = refs[G_B:]

    def load(tok):
        return jnp.concatenate([xr[0, pl.ds(tok, nchunk, stride=CMP_STRIDE), :] for xr in x_refs], axis=0)

    out = _compress_core(load, wc_ref, pe_ref, w2_ref, 0, nchunk)
    for g in range(G_B):
        o_ref[0, 0, g] = out[g * nchunk:(g + 1) * nchunk]


def _compress_sample_kernel(pt_ref, *refs, n_pages, spb):
    pg_refs = refs[:spb * n_pages]
    wc_ref, pe_ref, w2_ref, o_ref = refs[spb * n_pages:]
    cps = PAGE_SIZE // CMP_STRIDE
    nchunk = n_pages * cps
    for kv in range(2):
        def load(tok):
            return jnp.concatenate(
                [pg[0, pl.ds(tok * SLOTS + kv * G_B + g, cps, stride=CMP_STRIDE * SLOTS), :]
                 for smp in range(spb) for g in range(G_B) for pg in pg_refs[smp * n_pages:(smp + 1) * n_pages]],
                axis=0)

        out = _compress_core(load, wc_ref, pe_ref, w2_ref, kv, spb * nchunk)
        for smp in range(spb):
            for g in range(G_B):
                seg = smp * G_B + g
                o_ref[smp, kv, g] = out[seg * nchunk:(seg + 1) * nchunk]


def _compress_weights(cmp_pe, cmp_w1, cmp_w2):
    w1 = cmp_w1.reshape(2, CMP_LEN, DH_B, DH_B)
    wab = jnp.concatenate([w1[:, :CMP_STRIDE], w1[:, CMP_STRIDE:]], axis=-1)
    wc = wab.reshape(2, CMP_STRIDE // 2, 2 * DH_B, 2 * DH_B).astype(BF16)
    pa = cmp_pe[:, :CMP_STRIDE].reshape(2, CMP_STRIDE // 2, 1, 2 * DH_B)
    pb = cmp_pe[:, CMP_STRIDE:].reshape(2, CMP_STRIDE // 2, 1, 2 * DH_B)
    z = jnp.zeros((2, CMP_STRIDE // 2, 7, 2 * DH_B), F32)
    pe = jnp.concatenate([pa, z, pb, z], axis=2).astype(BF16)
    return wc, pe, cmp_w2.astype(BF16)


def _compress_prompt(cmp_kv, wc, pe, w2):
    n, t, _ = cmp_kv.shape
    nchunk = t // CMP_STRIDE
    kern = functools.partial(_compress_prompt_kernel, nchunk=nchunk)
    return pl.pallas_call(
        kern,
        grid=(n, 2),
        in_specs=[pl.BlockSpec((1, t, DH_B), functools.partial(lambda b, kv, g: (b, 0, kv * G_B + g), g=g))
                  for g in range(G_B)] + [
                  pl.BlockSpec((1,) + wc.shape[1:], lambda b, kv: (kv, 0, 0, 0)),
                  pl.BlockSpec((1,) + pe.shape[1:], lambda b, kv: (kv, 0, 0, 0)),
                  pl.BlockSpec((1, DH_B, DH_B), lambda b, kv: (kv, 0, 0))],
        out_specs=pl.BlockSpec((1, 1, G_B, nchunk, DH_B), lambda b, kv: (b, kv, 0, 0, 0)),
        out_shape=jax.ShapeDtypeStruct((n, 2, G_B, nchunk, DH_B), F32),
        compiler_params=_cparams(("parallel", "parallel")),
        name="compress_prompt",
    )(*([cmp_kv] * G_B), wc, pe, w2)


def _compress_sample(cache, page_table, wc, pe, w2):
    n, n_pages = page_table.shape
    nchunk = n_pages * (PAGE_SIZE // CMP_STRIDE)
    spb = math.gcd(n, 2)
    kern = functools.partial(_compress_sample_kernel, n_pages=n_pages, spb=spb)
    page_specs = [pl.BlockSpec((1, PAGE_SIZE * SLOTS, LANES),
                               functools.partial(lambda b, pt, smp, j: (pt[b * spb + smp, j], 0, 0), smp=smp, j=j))
                  for smp in range(spb) for j in range(n_pages)]
    grid_spec = pltpu.PrefetchScalarGridSpec(
        num_scalar_prefetch=1,
        grid=(n // spb,),
        in_specs=page_specs + [
            pl.BlockSpec(wc.shape, lambda b, pt: (0, 0, 0, 0)),
            pl.BlockSpec(pe.shape, lambda b, pt: (0, 0, 0, 0)),
            pl.BlockSpec(w2.shape, lambda b, pt: (0, 0, 0))],
        out_specs=pl.BlockSpec((spb, 2, G_B, nchunk, DH_B), lambda b, pt: (b, 0, 0, 0, 0)),
    )
    return pl.pallas_call(
        kern,
        grid_spec=grid_spec,
        out_shape=jax.ShapeDtypeStruct((n, 2, G_B, nchunk, DH_B), F32),
        compiler_params=_cparams(("parallel",)),
        name="compress_sample",
    )(page_table, *([cache] * (spb * n_pages)), wc, pe, w2)


def _cmp_topk_kernel(q_ref, kc_ref, vc_ref, ocmp_ref, selm_ref, p_scr, v_scr, *,
                     nb, rq, nck, n_cmp, n_sel, nsp, pos_base):
    tq = nb * rq
    j = pl.program_id(2)
    scale = DH_B ** -0.5
    kidx = _iota((1, nck), 1)
    qpos_c = pos_base + j * rq + _iota((rq, 1), 0)
    valid = (kidx < n_cmp) & (CMP_STRIDE * kidx + (CMP_LEN - 1) <= qpos_c)
    for b in range(nb):
        kcc = kc_ref[b, 0, 0].astype(BF16)
        vcc = vc_ref[b, 0, 0].astype(BF16)
        q = q_ref[b] * scale
        q2 = jnp.concatenate([q[:, r * LANES:(r + 1) * LANES] for r in range(R_B)], axis=0).astype(BF16)
        valid2 = jnp.concatenate([valid] * R_B, axis=0)
        s = jnp.where(valid2, _dot_nt(q2, kcc), MASK_NEG)
        m = jnp.max(s, axis=-1, keepdims=True)
        p = jnp.where(valid2, jnp.exp(s - m), 0.0)
        p = p / jnp.maximum(jnp.sum(p, axis=-1, keepdims=True), 1e-30)
        o = _dot(p.astype(BF16), vcc)
        psum = jnp.zeros((rq, nck), F32)
        for r in range(R_B):
            ocmp_ref[b, :, r * LANES:(r + 1) * LANES] = o[r * rq:(r + 1) * rq]
            psum = psum + p[r * rq:(r + 1) * rq]
        p_scr[b * rq:(b + 1) * rq, :] = psum
    psum = p_scr[...]
    p_hi = psum.astype(BF16)
    p_lo = (psum - p_hi.astype(F32)).astype(BF16)
    srow = _iota((LANES, nck), 0)
    kcol = _iota((LANES, nck), 1)
    cov = (CMP_STRIDE * kcol < SEL_BLOCK * srow + SEL_BLOCK) & (CMP_STRIDE * kcol + CMP_LEN > SEL_BLOCK * srow)
    cov = cov & (srow < n_sel) & (kcol < n_cmp)
    cov_t = jnp.where(cov, 1.0, 0.0).astype(BF16)
    imp_t = _dot_nt(cov_t, p_hi) + _dot_nt(cov_t, p_lo)
    qpos_l = pos_base + j * rq + _iota((1, tq), 1) % rq
    sb = _iota((nsp, 1), 0)
    valid_b = (sb < n_sel) & (sb * SEL_BLOCK <= qpos_l)
    forced = (sb == 0) | (sb == qpos_l // SEL_BLOCK)
    vals = jnp.where(forced, jnp.inf, jnp.where(valid_b, imp_t[:nsp], -jnp.inf))
    v_scr[...] = vals

    def body(jj, cnt):
        vj = v_scr[pl.ds(jj, 1), :]
        tie = jnp.where(sb > jj, 1.0, 0.0)
        return cnt + jnp.where(vj > vals, 1.0, jnp.where(vj == vals, tie, 0.0))

    cnt = lax.fori_loop(0, nsp, body, jnp.zeros((nsp, tq), F32))
    keep = valid_b & (cnt < float(min(TOP_N, n_sel)))
    selm_t = jnp.where(keep, 0.0, -SEL_NEG)
    if nsp < LANES:
        selm_t = jnp.concatenate([selm_t, jnp.zeros((LANES - nsp, tq), F32)], axis=0)
    selm = selm_t.T.astype(BF16)
    for b in range(nb):
        selm_ref[b, 0] = selm[b * rq:(b + 1) * rq]


def _cmp_topk(qb, ccmp, *, nb, rq, n_cmp, n_sel, pos_base):
    n, r, _ = qb.shape
    nck = ccmp.shape[3]
    nsp = -(-n_sel // 8) * 8
    tq = nb * rq
    kern = functools.partial(_cmp_topk_kernel, nb=nb, rq=rq, nck=nck, n_cmp=n_cmp, n_sel=n_sel, nsp=nsp,
                             pos_base=pos_base)
    return pl.pallas_call(
        kern,
        grid=(n // nb, G_B, r // rq),
        in_specs=[pl.BlockSpec((nb, rq, R_B * LANES), lambda a, g, j: (a, j, g)),
                  pl.BlockSpec((nb, 1, 1, nck, DH_B), lambda a, g, j: (a, 0, g, 0, 0)),
                  pl.BlockSpec((nb, 1, 1, nck, DH_B), lambda a, g, j: (a, 1, g, 0, 0))],
        out_specs=[pl.BlockSpec((nb, rq, R_B * LANES), lambda a, g, j: (a, j, g)),
                   pl.BlockSpec((nb, 1, rq, LANES), lambda a, g, j: (a, g, j, 0))],
        out_shape=[jax.ShapeDtypeStruct((n, r, H_B * DH_B), F32),
                   jax.ShapeDtypeStruct((n, G_B, r, LANES), BF16)],
        scratch_shapes=[pltpu.VMEM((tq, nck), F32), pltpu.VMEM((nsp, tq), F32)],
        compiler_params=_cparams(("parallel", "parallel", "parallel")),
        name="cmp_topk",
    )(qb, ccmp, ccmp)


def _attend_scores(pieces):
    m = None
    for s, _ in pieces:
        ms = jnp.max(s, axis=-1, keepdims=True)
        m = ms if m is None else jnp.maximum(m, ms)
    l = jnp.zeros_like(m)
    acc = jnp.zeros((m.shape[0], LANES), F32)
    for s, v in pieces:
        p = jnp.exp(s - m)
        l = l + jnp.sum(p, axis=-1, keepdims=True)
        acc = acc + _dot(p.astype(BF16), v)
    return acc / l


def _attend_pieces(q2, pieces):
    scored = []
    for k, v, mask in pieces:
        s = _dot_nt(q2, k)
        scored.append((s if mask is None else jnp.where(mask, s, MASK_NEG), v))
    return _attend_scores(scored)


def _pad_rows(x, rows):
    return jnp.concatenate([x, jnp.zeros((rows - x.shape[0], x.shape[1]), x.dtype)], axis=0)


def _new_token_mask(m_rows, s):
    r = _iota((m_rows, LANES), 0) % s
    return _iota((m_rows, LANES), 1) <= r


def _diff_sample_kernel(pt_ref, *refs, n_pages, s, lam_init):
    kp_refs = refs[:n_pages]
    vp_refs = refs[n_pages:2 * n_pages]
    q_ref, kn_ref, vn_ref, lamw_ref, sub_ref, o_ref = refs[2 * n_pages:]
    lane = _iota((1, LANES), 1)
    lam = _diff_lambda(lamw_ref[...], lam_init)
    new_mask = _new_token_mask(2 * s, s)
    for h in range(H_A):
        cols = slice(h * LANES, (h + 1) * LANES)
        q = q_ref[0, :, cols] * (DK_A ** -0.5)
        q2 = jnp.concatenate([jnp.where(lane < DK_A, q, 0.0), jnp.where(lane >= DK_A, q, 0.0)], axis=0).astype(BF16)
        kt_past = jnp.concatenate([r[0, cols, :].astype(BF16) for r in kp_refs], axis=1)
        s_past = _dot(q2, kt_past)
        vpast = jnp.concatenate([_slot_rows(r, h, PAGE_SIZE).astype(BF16) for r in vp_refs], axis=0)
        knew = _pad_rows(kn_ref[0, :, cols], LANES).astype(BF16)
        vnew = _pad_rows(vn_ref[0, :, cols], LANES).astype(BF16)
        s_new = jnp.where(new_mask, _dot_nt(q2, knew), MASK_NEG)
        o2 = _attend_scores([(s_past, vpast), (s_new, vnew)])
        o_ref[0, :, cols] = _diff_finalize(o2, s, lam, sub_ref[...], lam_init)


def _page_specs(n_pages, rows, width):
    return [pl.BlockSpec((1, rows, width), functools.partial(lambda b, pt, j: (pt[b, j], 0, 0), j=j))
            for j in range(n_pages)]


def _slot_rows(ref, slot, n_tok):
    return ref[0, pl.ds(slot, n_tok, stride=SLOTS), :]


def _diff_attn_sample(qa, ka, va, cache_k, cache_v, page_table, lam_w, sub_w, lam_init):
    n, s, _ = qa.shape
    n_pages = page_table.shape[1]
    kern = functools.partial(_diff_sample_kernel, n_pages=n_pages, s=s, lam_init=lam_init)
    row_spec = pl.BlockSpec((1, s, 512), lambda b, pt: (b, 0, 0))
    grid_spec = pltpu.PrefetchScalarGridSpec(
        num_scalar_prefetch=1,
        grid=(n,),
        in_specs=_page_specs(n_pages, 512, PAGE_SIZE) + _page_specs(n_pages, PAGE_SIZE * SLOTS, LANES)
        + [row_spec, row_spec, row_spec,
           pl.BlockSpec(lam_w.shape, lambda b, pt: (0, 0)), pl.BlockSpec(sub_w.shape, lambda b, pt: (0, 0))],
        out_specs=row_spec,
    )
    return pl.pallas_call(
        kern,
        grid_spec=grid_spec,
        out_shape=jax.ShapeDtypeStruct((n, s, 512), F32),
        compiler_params=_cparams(("parallel",)),
        name="diff_attn_sample",
    )(page_table, *([cache_k] * n_pages), *([cache_v] * n_pages), qa, ka, va, lam_w, sub_w)


def _sel_sample_kernel(pt_ref, *refs, n_pages, s):
    pg_refs = refs[:n_pages]
    q_ref, selm_ref, new_ref, o_ref = refs[n_pages:]
    past = n_pages * PAGE_SIZE
    new_mask = _new_token_mask(R_B * s, s)
    oh_past = _sel_onehot(past, 0)
    new_blk = past // SEL_BLOCK
    oh_new = jnp.where((_iota((LANES, LANES), 1) == new_blk) & (_iota((LANES, LANES), 0) < s), 1.0, 0.0)
    oh_new = oh_new.astype(BF16)
    scale = DH_B ** -0.5
    for g in range(G_B):
        kcols = slice(g * LANES, (g + 1) * LANES)
        vcols = slice((G_B + g) * LANES, (G_B + g + 1) * LANES)
        selm = selm_ref[0, g]
        q2 = jnp.concatenate(
            [jnp.concatenate([(q_ref[0, :, (g * R_B + r) * LANES:(g * R_B + r + 1) * LANES] * scale).astype(BF16),
                              selm], axis=-1) for r in range(R_B)], axis=0)
        kpast = jnp.concatenate([_slot_rows(r, g, PAGE_SIZE).astype(BF16) for r in pg_refs], axis=0)
        vpast = jnp.concatenate([_slot_rows(r, G_B + g, PAGE_SIZE).astype(BF16) for r in pg_refs], axis=0)
        kpast = jnp.concatenate([kpast, oh_past], axis=-1)
        knew = jnp.concatenate([_pad_rows(new_ref[0, :, kcols], LANES).astype(BF16), oh_new], axis=-1)
        vnew = _pad_rows(new_ref[0, :, vcols], LANES).astype(BF16)
        o2 = _attend_pieces(q2, [(kpast, vpast, None), (knew, vnew, new_mask)])
        for r in range(R_B):
            o_ref[0, :, (g * R_B + r) * LANES:(g * R_B + r + 1) * LANES] = o2[r * s:(r + 1) * s]


def _sel_attn_sample(qb, selm, sel_new, cache, page_table):
    n, s, _ = qb.shape
    n_pages = page_table.shape[1]
    assert s <= SEL_BLOCK and (n_pages * PAGE_SIZE) % SEL_BLOCK == 0
    kern = functools.partial(_sel_sample_kernel, n_pages=n_pages, s=s)
    row_spec = pl.BlockSpec((1, s, 512), lambda b, pt: (b, 0, 0))
    grid_spec = pltpu.PrefetchScalarGridSpec(
        num_scalar_prefetch=1,
        grid=(n,),
        in_specs=_page_specs(n_pages, PAGE_SIZE * SLOTS, LANES)
        + [row_spec, pl.BlockSpec((1, G_B, s, LANES), lambda b, pt: (b, 0, 0, 0)), row_spec],
        out_specs=row_spec,
    )
    return pl.pallas_call(
        kern,
        grid_spec=grid_spec,
        out_shape=jax.ShapeDtypeStruct((n, s, 512), F32),
        compiler_params=_cparams(("parallel",)),
        name="nsa_sel_sample",
    )(page_table, *([cache] * n_pages), qb, selm, sel_new)


def _win_sample_kernel(q_ref, buf_ref, new_ref, o_ref, wout_ref, *, s, wb, spb):
    new_mask = _new_token_mask(R_B * s, s)
    r = _iota((R_B * s, wb), 0) % s
    buf_mask = _iota((R_B * s, wb), 1) > r + (wb - WINDOW)
    scale = DH_B ** -0.5
    keep = (wb - s) * SLOTS
    for b in range(spb):
        for g in range(G_B):
            kcols = slice(g * LANES, (g + 1) * LANES)
            vcols = slice((G_B + g) * LANES, (G_B + g + 1) * LANES)
            q2 = jnp.concatenate(
                [(q_ref[b, :, (g * R_B + r_) * LANES:(g * R_B + r_ + 1) * LANES] * scale).astype(BF16)
                 for r_ in range(R_B)], axis=0)
            kbuf = buf_ref[b, pl.ds(g, wb, stride=SLOTS), :].astype(BF16)
            vbuf = buf_ref[b, pl.ds(G_B + g, wb, stride=SLOTS), :].astype(BF16)
            knew = _pad_rows(new_ref[b, :, kcols], LANES).astype(BF16)
            vnew = _pad_rows(new_ref[b, :, vcols], LANES).astype(BF16)
            o2 = _attend_pieces(q2, [(kbuf, vbuf, buf_mask), (knew, vnew, new_mask)])
            for r_ in range(R_B):
                o_ref[b, :, (g * R_B + r_) * LANES:(g * R_B + r_ + 1) * LANES] = o2[r_ * s:(r_ + 1) * s]
        wout_ref[b, 0:keep] = buf_ref[b, s * SLOTS:wb * SLOTS]
        for slot in range(SLOTS):
            wout_ref[b, pl.ds(keep + slot, s, stride=SLOTS), :] = new_ref[b, :, slot * LANES:(slot + 1) * LANES]


def _win_attn_sample(qb, win_buf, win_new, past):
    n, s, _ = qb.shape
    wb = win_buf.shape[1] // SLOTS
    assert past >= wb and wb % 8 == 0 and s % 8 == 0
    spb = math.gcd(n, 4)
    kern = functools.partial(_win_sample_kernel, s=s, wb=wb, spb=spb)
    row_spec = pl.BlockSpec((spb, s, 512), lambda b: (b, 0, 0))
    buf_spec = pl.BlockSpec((spb, wb * SLOTS, LANES), lambda b: (b, 0, 0))
    return pl.pallas_call(
        kern,
        grid=(n // spb,),
        in_specs=[row_spec, buf_spec, row_spec],
        out_specs=[row_spec, buf_spec],
        out_shape=[jax.ShapeDtypeStruct((n, s, 512), F32), jax.ShapeDtypeStruct((n, wb * SLOTS, LANES), F32)],
        compiler_params=_cparams(("parallel",)),
        name="nsa_win_sample",
    )(qb, win_buf, win_new)


def _even_out_kernel(y_ref, g1_ref, oa_ref, oc_ref, os_ref, ow_ref, gt_ref, w_ref, o_ref):
    gt = gt_ref[...]
    parts = [oa_ref[...].astype(BF16)]
    for hb in range(H_B):
        sl = slice(hb * LANES, (hb + 1) * LANES)
        ob = (gt[:, 3 * hb:3 * hb + 1] * oc_ref[:, sl] + gt[:, 3 * hb + 1:3 * hb + 2] * os_ref[:, sl]
              + gt[:, 3 * hb + 2:3 * hb + 3] * ow_ref[:, sl])
        parts.append(ob.astype(BF16))
    out = _dot(jnp.concatenate(parts, axis=-1), w_ref[...])
    y = y_ref[...]
    o_ref[...] = y + g1_ref[0] * out.reshape(y.shape)


def _even_out(til, y, mods, oa, oc, os_, ow, gt, w_out):
    return pl.pallas_call(
        _even_out_kernel,
        grid=til.grid,
        in_specs=[til.x_spec(D_MODEL), til.mod_spec(2), til.flat_spec(512), til.flat_spec(512),
                  til.flat_spec(512), til.flat_spec(512), til.flat_spec(LANES), _full_spec(w_out.shape)],
        out_specs=til.x_spec(D_MODEL),
        out_shape=jax.ShapeDtypeStruct(y.shape, F32),
        compiler_params=_cparams(("parallel", "parallel")),
        name="even_out",
    )(y, mods, oa, oc, os_, ow, gt, w_out)


def _mlp_kernel(y_ref, sh_ref, sc_ref, g_ref, nw_ref, w1_ref, w2_ref, fw_ref, o_ref, *, final):
    y = y_ref[...]
    h = _norm_mod(y, nw_ref[...], sh_ref[0], sc_ref[0]).reshape(-1, D_MODEL).astype(BF16)
    a = jnp.maximum(_dot(h, w1_ref[...]), 0.0)
    out = _dot((a * a).astype(BF16), w2_ref[...])
    y2 = y + g_ref[0] * out.reshape(y.shape)
    if final:
        y2 = _rms(y2, fw_ref[...])
    o_ref[...] = y2


def _mlp(til, y, mods, nw, w1, w2, fw, final):
    kern = functools.partial(_mlp_kernel, final=final)
    return pl.pallas_call(
        kern,
        grid=til.grid,
        in_specs=[til.x_spec(D_MODEL), til.mod_spec(3), til.mod_spec(4), til.mod_spec(5),
                  _full_spec((1, D_MODEL)), _full_spec(w1.shape), _full_spec(w2.shape), _full_spec((1, D_MODEL))],
        out_specs=til.x_spec(D_MODEL),
        out_shape=jax.ShapeDtypeStruct(y.shape, F32),
        compiler_params=_cparams(("parallel", "parallel")),
        name="mlp",
    )(y, mods, mods, mods, nw, w1, w2, fw)


def _gla_inproj_kernel(x_ref, sh_ref, sc_ref, nw_ref, w_ref, wgl_ref, wgate_ref, bgate_ref,
                       q_ref, k_ref, v_ref, r_ref, la_ref):
    h = _norm_mod(x_ref[...], nw_ref[...], sh_ref[0], sc_ref[0]).reshape(-1, D_MODEL).astype(BF16)
    nk = H_C * DK_C
    nv = H_C * DV_C
    q_ref[...] = _dot(h, w_ref[:, 0:nk]) * (DK_C ** -0.5)
    k_ref[...] = _dot(h, w_ref[:, nk:2 * nk])
    v_ref[...] = _dot(h, w_ref[:, 2 * nk:2 * nk + nv])
    r_ref[...] = _dot(h, w_ref[:, 2 * nk + nv:2 * nk + 2 * nv])
    gl = _dot(h, wgl_ref[...])
    x = _dot(gl.astype(BF16), wgate_ref[...]) + bgate_ref[...]
    log_sig = jnp.minimum(x, 0.0) - jnp.log1p(jnp.exp(-jnp.abs(x)))
    la_ref[...] = log_sig / GATE_TAU


def _gla_inproj(til, x, mods, nw, w_main, w_gl, w_gate, b_gate):
    rows_total = til.nb * til.r
    widths = (H_C * DK_C, H_C * DK_C, H_C * DV_C, H_C * DV_C, H_C * DK_C)
    return pl.pallas_call(
        _gla_inproj_kernel,
        grid=til.grid,
        in_specs=[til.x_spec(D_MODEL), til.mod_spec(0), til.mod_spec(1), _full_spec((1, D_MODEL)),
                  _full_spec(w_main.shape), _full_spec(w_gl.shape), _full_spec(w_gate.shape),
                  _full_spec(b_gate.shape)],
        out_specs=[til.flat_spec(c) for c in widths],
        out_shape=[jax.ShapeDtypeStruct((rows_total, c), F32) for c in widths],
        compiler_params=_cparams(("parallel", "parallel")),
        name="gla_inproj",
    )(x, mods, mods, nw, w_main, w_gl, w_gate, b_gate)


def _gla_prep(q, k, g, c, sub, n_real):
    tri = jnp.where(_iota((c, c), 0) >= _iota((c, c), 1), 1.0, 0.0)
    b = jnp.dot(tri, g, preferred_element_type=F32, precision=lax.Precision.HIGHEST)
    qe = (q * jnp.exp(b)).astype(BF16)
    lane_c = _iota((sub, c), 1)
    row_s = _iota((sub, c), 0)
    att_rows = []
    for blk in range(c // sub):
        lo = blk * sub
        qi, ki, bi = q[lo:lo + sub], k[lo:lo + sub], b[lo:lo + sub]
        diag = jnp.zeros((sub, c), F32)
        for jj in range(min(sub, max(n_real - lo, 0))):
            e = jnp.exp(jnp.minimum(bi - bi[jj:jj + 1], 0.0))
            col = jnp.sum(qi * ki[jj:jj + 1] * e, axis=-1, keepdims=True)
            diag = jnp.where(lane_c == lo + jj, col, diag)
        att = jnp.where(lane_c - lo <= row_s, diag, 0.0)
        if blk > 0:
            bs = b[lo - 1:lo]
            q_in = qi * jnp.exp(bi - bs)
            k_out = k * jnp.exp(jnp.minimum(bs - b, 0.0))
            att = jnp.where(lane_c < lo, _dot_nt(q_in.astype(BF16), k_out.astype(BF16)), att)
        att_rows.append(att)
    att = att_rows[0] if len(att_rows) == 1 else jnp.concatenate(att_rows, axis=0)
    bl = b[c - 1:c]
    kd = (k * jnp.exp(bl - b)).astype(BF16)
    eye = _iota((DK_C, DK_C), 0) == _iota((DK_C, DK_C), 1)
    decay = jnp.sum(jnp.where(eye, jnp.exp(bl), 0.0), axis=-1, keepdims=True)
    return qe, att, kd, decay


def _gla_apply(state, prep, v):
    qe, att, kd, decay = prep
    o = _dot(qe, state.astype(BF16)) + _dot(att, v)
    return o, decay * state + _dot_tn(kd, v.astype(BF16))


def _gla_rec_kernel(*refs, tt, c, sub, hp, has_s0):
    if has_s0:
        q_ref, k_ref, v_ref, g_ref, s0_ref, o_ref, sfin_ref, s_ref = refs
    else:
        q_ref, k_ref, v_ref, g_ref, o_ref, sfin_ref, s_ref = refs
    t = pl.program_id(2)

    @pl.when(t == 0)
    def _():
        s_ref[...] = s0_ref[0] if has_s0 else jnp.zeros(s_ref.shape, F32)

    def kcols(hh):
        return slice(hh * DK_C, (hh + 1) * DK_C)

    def vcols(hh):
        return slice(hh * DV_C, (hh + 1) * DV_C)

    if tt < c:
        pad = lambda x: _pad_rows(x, c)
        for hh in range(hp):
            prep = _gla_prep(pad(q_ref[0, :, kcols(hh)]), pad(k_ref[0, :, kcols(hh)]),
                             pad(g_ref[0, :, kcols(hh)]), c, sub, tt)
            o, s_ref[hh] = _gla_apply(s_ref[hh], prep, pad(v_ref[0, :, vcols(hh)]))
            o_ref[0, :, vcols(hh)] = o[:tt]
    else:
        per_trip = 2 if (tt // c) % 2 == 0 else 1

        def body(ci, carry):
            for hh in range(hp):
                rows = [pl.ds(pl.multiple_of((ci * per_trip + u) * c, c), c) for u in range(per_trip)]
                preps = [_gla_prep(q_ref[0, r, kcols(hh)], k_ref[0, r, kcols(hh)], g_ref[0, r, kcols(hh)],
                                   c, sub, c) for r in rows]
                state = s_ref[hh]
                for r, prep in zip(rows, preps):
                    o_ref[0, r, vcols(hh)], state = _gla_apply(state, prep, v_ref[0, r, vcols(hh)])
                s_ref[hh] = state
            return carry
        lax.fori_loop(0, tt // c // per_trip, body, 0)

    @pl.when(t == pl.num_programs(2) - 1)
    def _():
        sfin_ref[0] = s_ref[...]


def _gla_recurrence(q, k, v, g, s0, tt, c, sub, hp):
    n, t, _ = q.shape
    kern = functools.partial(_gla_rec_kernel, tt=tt, c=c, sub=sub, hp=hp, has_s0=s0 is not None)
    kspec = pl.BlockSpec((1, tt, hp * DK_C), lambda b, h, i: (b, i, h))
    vspec = pl.BlockSpec((1, tt, hp * DV_C), lambda b, h, i: (b, i, h))
    sspec = pl.BlockSpec((1, hp, DK_C, DV_C), lambda b, h, i: (b, h, 0, 0))
    in_specs = [kspec, kspec, vspec, kspec]
    args = [q, k, v, g]
    if s0 is not None:
        in_specs.append(sspec)
        args.append(s0)
    return pl.pallas_call(
        kern,
        grid=(n, H_C // hp, t // tt),
        in_specs=in_specs,
        out_specs=[vspec, sspec],
        out_shape=[jax.ShapeDtypeStruct((n, t, H_C * DV_C), F32),
                   jax.ShapeDtypeStruct((n, H_C, DK_C, DV_C), F32)],
        scratch_shapes=[pltpu.VMEM((hp, DK_C, DV_C), F32)],
        compiler_params=_cparams(("parallel", "parallel", "arbitrary")),
        name="gla_recurrence",
    )(*args)


def _gla_out_kernel(y_ref, g1_ref, o_ref_in, r_ref, nw_ref, w_ref, out_ref):
    parts = []
    for h in range(H_C):
        sl = slice(h * DV_C, (h + 1) * DV_C)
        r = r_ref[:, sl]
        parts.append((_rms(o_ref_in[:, sl], nw_ref[...]) * (r * jax.nn.sigmoid(r))).astype(BF16))
    out = _dot(jnp.concatenate(parts, axis=-1), w_ref[...])
    y = y_ref[...]
    out_ref[...] = y + g1_ref[0] * out.reshape(y.shape)


def _gla_out(til, y, mods, o, r, nw, w_out):
    return pl.pallas_call(
        _gla_out_kernel,
        grid=til.grid,
        in_specs=[til.x_spec(D_MODEL), til.mod_spec(2), til.flat_spec(H_C * DV_C), til.flat_spec(H_C * DV_C),
                  _full_spec(nw.shape), _full_spec(w_out.shape)],
        out_specs=til.x_spec(D_MODEL),
        out_shape=jax.ShapeDtypeStruct(y.shape, F32),
        compiler_params=_cparams(("parallel", "parallel")),
        name="gla_out",
    )(y, mods, o, r, nw, w_out)


def _rope_tables(pos, d):
    inv = ROPE_THETA ** (-jnp.arange(0, d, 2, dtype=F32) / d)
    ang = pos.astype(F32)[:, None] * inv[None, :]
    cos, sin = jnp.cos(ang), jnp.sin(ang)
    rep = LANES // d
    c = jnp.tile(jnp.concatenate([cos, cos], axis=-1), (1, rep))
    s = jnp.tile(jnp.concatenate([-sin, sin], axis=-1), (1, rep))
    return c, s


def _rope_tables_t(pos, d):
    inv = ROPE_THETA ** (-jnp.arange(0, d, 2, dtype=F32) / d)
    ang = inv[:, None] * pos.astype(F32)[None, :]
    return jnp.cos(ang), jnp.sin(ang)


def _mods_for(mod_l, lo, hi):
    nb = hi - lo
    return mod_l[lo:hi].reshape(nb, 6, D_MODEL).transpose(1, 0, 2).reshape(6, nb, 1, D_MODEL)


def _prompt_tile_rows(t):
    return math.gcd(t, 256)


def _even_layer(yp, ys, mods_p, mods_s, caches, page_table, wts, lam_init, til_p, til_s):
    (c_dk, c_dv, c_cmp, c_sel, win_buf) = caches
    n, t, _ = yp.shape
    ns, s, _ = ys.shape
    n_pages = page_table.shape[1]
    past = n_pages * PAGE_SIZE
    w_in = wts["w_in"]
    n_main = w_in.shape[1] - 3 * H_B
    w_main = w_in[:, :n_main].astype(BF16)
    w_gate = jnp.pad(w_in[:, n_main:], ((0, 0), (0, LANES - 3 * H_B))).astype(BF16)
    nw1 = wts["norm1"].reshape(1, D_MODEL)
    lam_w = wts["lam_w"]
    sub_w = wts["sub_w"].reshape(1, DV_A)
    wc, pe, w2c = _compress_weights(wts["cmp_pe"], wts["cmp_w1"], wts["cmp_w2"])
    w_out = wts["w_out"].astype(BF16)

    tabs_p = _rope_tables(jnp.arange(t), DK_A) + _rope_tables(jnp.arange(t), DH_B)
    kt_tabs = _rope_tables_t(jnp.arange(t), DK_A)
    qa, kat, va, qb, cmp_kv, sel_kv, win_kv, gt = _even_inproj(til_p, yp, mods_p, nw1, w_main, w_gate, tabs_p, kt_tabs)
    r3 = lambda a: a.reshape(n, t, a.shape[-1])
    tq = _prompt_tile_rows(t)
    tq_attn = math.gcd(t, 2 * tq)
    oa = _diff_attn_prompt(r3(qa), kat, r3(va), lam_w, sub_w, lam_init, tq_attn, tq)
    ccmp = _compress_prompt(r3(cmp_kv), wc, pe, w2c)
    n_cmp = (t - CMP_LEN) // CMP_STRIDE + 1
    n_sel = -(-t // SEL_BLOCK)
    o_cmp, selm = _cmp_topk(r3(qb), ccmp, nb=1, rq=tq, n_cmp=n_cmp, n_sel=n_sel, pos_base=0)
    o_sel = _nsa_attn_prompt(r3(qb), r3(sel_kv), selm, tq_attn, tq)
    o_win = _nsa_attn_prompt(r3(qb), r3(win_kv), None, tq_attn, tq)
    f2 = lambda a: a.reshape(n * t, a.shape[-1])
    yp = _even_out(til_p, yp, mods_p, f2(oa), f2(o_cmp), f2(o_sel), f2(o_win), gt, w_out)
    wl = min(WINDOW, t)
    ka_state = jnp.transpose(kat.reshape(n, H_A, 2, DK_A, t), (0, 4, 1, 2, 3))
    st_p = (ka_state, r3(va).reshape(n, t, H_A, DV_A),
            r3(cmp_kv).reshape(n, t, 2, G_B, DH_B), r3(sel_kv).reshape(n, t, 2, G_B, DH_B),
            r3(win_kv)[:, t - wl:].reshape(n, wl, 2, G_B, DH_B))

    pos_s = past + jnp.arange(s)
    tabs_s = tuple(jnp.tile(x, (til_s.b, 1)) for x in _rope_tables(pos_s, DK_A) + _rope_tables(pos_s, DH_B))
    qa, ka, va, qb, cmp_kv, sel_kv, win_kv, gt = _even_inproj(til_s, ys, mods_s, nw1, w_main, w_gate, tabs_s)
    r3 = lambda a: a.reshape(ns, s, a.shape[-1])
    slot_cache = lambda c: c.reshape(c.shape[0], PAGE_SIZE * SLOTS, LANES)
    c_dkt = jnp.transpose(c_dk, (0, 2, 3, 4, 1)).reshape(c_dk.shape[0], H_A * 2 * DK_A, PAGE_SIZE)
    oa = _diff_attn_sample(r3(qa), r3(ka), r3(va), c_dkt, slot_cache(c_dv), page_table, lam_w, sub_w, lam_init)
    ccmp = _compress_sample(slot_cache(c_cmp), page_table, wc, pe, w2c)
    total = past + s
    n_cmp = (total - CMP_LEN) // CMP_STRIDE + 1
    n_sel = -(-total // SEL_BLOCK)
    assert n_cmp <= ccmp.shape[3] and n_sel <= LANES
    nb = math.gcd(ns, LANES // s)
    o_cmp, selm = _cmp_topk(r3(qb), ccmp, nb=nb, rq=s, n_cmp=n_cmp, n_sel=n_sel, pos_base=past)
    o_sel = _sel_attn_sample(r3(qb), selm, r3(sel_kv), slot_cache(c_sel), page_table)
    wb = win_buf.shape[1]
    o_win, win_out = _win_attn_sample(r3(qb), win_buf.reshape(ns, wb * SLOTS, LANES), r3(win_kv), past)
    f2 = lambda a: a.reshape(ns * s, a.shape[-1])
    ys = _even_out(til_s, ys, mods_s, f2(oa), f2(o_cmp), f2(o_sel), f2(o_win), gt, w_out)
    st_s = (r3(ka).reshape(ns, s, H_A, 2, DK_A), r3(va).reshape(ns, s, H_A, DV_A),
            r3(cmp_kv).reshape(ns, s, 2, G_B, DH_B), r3(sel_kv).reshape(ns, s, 2, G_B, DH_B),
            win_out.reshape(ns, wb, 2, G_B, DH_B))
    return yp, ys, st_p, st_s


def _odd_layer(yp, ys, mods_p, mods_s, s0, wts, til_p, til_s):
    w_in = wts["w_in"]
    n_main = w_in.shape[1] - GATE_RANK
    w_main = w_in[:, :n_main].astype(BF16)
    w_gl = jnp.pad(w_in[:, n_main:], ((0, 0), (0, LANES - GATE_RANK))).astype(BF16)
    w_gate = jnp.pad(wts["w_gate"], ((0, LANES - GATE_RANK), (0, 0))).astype(BF16)
    b_gate = wts["b_gate"].reshape(1, -1)
    nw1 = wts["norm1"].reshape(1, D_MODEL)
    gnw = wts["gnorm"].reshape(1, DV_C)
    w_out = wts["w_out"].astype(BF16)
    outs = []
    for y, mods, til, state in ((yp, mods_p, til_p, None), (ys, mods_s, til_s, s0)):
        n, t, _ = y.shape
        q, k, v, r, la = _gla_inproj(til, y, mods, nw1, w_main, w_gl, w_gate, b_gate)
        r3 = lambda a: a.reshape(n, t, a.shape[-1])
        c = math.gcd(t, GLA_CHUNK)
        if c >= GLA_SUB:
            tt, cc, sub = math.gcd(t, 8 * c), c, GLA_SUB_LONG
        else:
            tt, cc, sub = t, GLA_SUB, GLA_SUB
        o, s_fin = _gla_recurrence(r3(q), r3(k), r3(v), r3(la), state, tt, cc, sub, H_C)
        y = _gla_out(til, y, mods, o.reshape(n * t, -1), r, gnw, w_out)
        outs.append((y, s_fin))
    return outs[0][0], outs[1][0], outs[0][1], outs[1][1]


def kernel(x_prompt, x_sample, c_prompt, c_sample, cache_diff_k, cache_diff_v, cache_cmp_kv, cache_sel_kv,
           state_win_kv, state_gla, page_table, norm1_w, norm2_w, ada_w, ada_b, even_w_in, even_w_out,
           diff_lambda_w, diff_subln_w, cmp_pe, cmp_w1, cmp_w2, gla_w_in, gla_w_gate, gla_b_gate, gla_norm_w,
           gla_w_out, mlp_w1, mlp_w2, final_norm_w):
    depth = ada_w.shape[0]
    n, t, _ = x_prompt.shape
    ns, s, _ = x_sample.shape
    til_p = _Tiling(n, t, 1, _prompt_tile_rows(t))
    til_s = _Tiling(ns, s, math.gcd(ns, 256 // s), s)

    pad = (-(n + ns)) % 8
    c_all = jnp.concatenate([c_prompt, c_sample, jnp.zeros((pad, D_MODEL), F32)], axis=0)
    mod = _adaln(c_all, ada_w, ada_b)

    yp, ys = x_prompt, x_sample
    st_p = [[] for _ in range(6)]
    st_s = [[] for _ in range(6)]
    fw = final_norm_w.reshape(1, D_MODEL)
    for l in range(depth):
        mods_p = _mods_for(mod[l], 0, n)
        mods_s = _mods_for(mod[l], n, n + ns)
        if l % 2 == 0:
            e = l // 2
            lam_init = 0.8 - 0.6 * math.exp(-0.3 * l)
            wts = dict(w_in=even_w_in[e], w_out=even_w_out[e], lam_w=diff_lambda_w[e], sub_w=diff_subln_w[e],
                       cmp_pe=cmp_pe[e], cmp_w1=cmp_w1[e], cmp_w2=cmp_w2[e], norm1=norm1_w[l])
            caches = (cache_diff_k[e], cache_diff_v[e], cache_cmp_kv[e], cache_sel_kv[e], state_win_kv[e])
            yp, ys, sp, ss = _even_layer(yp, ys, mods_p, mods_s, caches, page_table, wts, lam_init, til_p, til_s)
            for i in range(5):
                st_p[i].append(sp[i])
                st_s[i].append(ss[i])
        else:
            o = l // 2
            wts = dict(w_in=gla_w_in[o], w_gate=gla_w_gate[o], b_gate=gla_b_gate[o], gnorm=gla_norm_w[o],
                       w_out=gla_w_out[o], norm1=norm1_w[l])
            yp, ys, gp, gs = _odd_layer(yp, ys, mods_p, mods_s, state_gla[o], wts, til_p, til_s)
            st_p[5].append(gp)
            st_s[5].append(gs)
        final = l == depth - 1
        w1 = mlp_w1[l].astype(BF16)
        w2 = mlp_w2[l].astype(BF16)
        nw2 = norm2_w[l].reshape(1, D_MODEL)
        yp = _mlp(til_p, yp, mods_p, nw2, w1, w2, fw, final)
        ys = _mlp(til_s, ys, mods_s, nw2, w1, w2, fw, final)
    outs_p = [jnp.stack(x, axis=0) for x in st_p]
    outs_s = [jnp.stack(x, axis=0) for x in st_s]
    return (yp, ys, *outs_p, *outs_s)
```

```python
import functools
import math

import jax
import jax.numpy as jnp
from jax import lax
from jax.experimental import pallas as pl
from jax.experimental.pallas import tpu as pltpu

F32 = jnp.float32
BF16 = jnp.bfloat16

D_MODEL = 1024
PAGE_SIZE = 128
H_A = 4
DK_A = 64
DV_A = 128
H_B = 4
G_B = 2
R_B = 2
DH_B = 128
CMP_LEN = 32
CMP_STRIDE = 16
SEL_BLOCK = 64
TOP_N = 16
WINDOW = 512
H_C = 4
DK_C = 128
DV_C = 256
GATE_RANK = 16
GATE_TAU = 16.0
GLA_CHUNK = 64
GLA_SUB = 16
GLA_SUB_LONG = 16
GLA_SAFE_DECAY = 80.0
D_FF = 4 * D_MODEL
ROPE_THETA = 10000.0
EPS = 1e-6

LANES = 128
SLOTS = 4
MASK_NEG = -1e30
LOG2E = 1.4426950408889634
SAFE_SHIFT = 56.0
SEL_NEG = 32768.0
VMEM_LIMIT_MB = 56


def _cparams(sem, vmem_mb=VMEM_LIMIT_MB):
    return pltpu.CompilerParams(dimension_semantics=sem, vmem_limit_bytes=vmem_mb * 1024 * 1024)


def _dot(a, b):
    return jnp.dot(a, b, preferred_element_type=F32)


def _dot_nt(a, b):
    return lax.dot_general(a, b, (((1,), (1,)), ((), ())), preferred_element_type=F32)


def _dot_tn(a, b):
    return lax.dot_general(a, b, (((0,), (0,)), ((), ())), preferred_element_type=F32)


def _iota(shape, dim):
    return lax.broadcasted_iota(jnp.int32, shape, dim)


def _rms(x, w):
    ms = jnp.mean(x * x, axis=-1, keepdims=True)
    return x * lax.rsqrt(ms + EPS) * w


def _norm_mod(x, nw, shift, scale):
    return _rms(x, nw) * (1.0 + scale) + shift


class _Tiling:
    def __init__(self, nb, r, b, rt):
        assert nb % b == 0 and r % rt == 0 and (b == 1 or rt == r)
        self.nb, self.r, self.b, self.rt = nb, r, b, rt
        self.grid = (nb // b, r // rt)
        self.rows = b * rt
        self.nrb = r // rt

    def x_spec(self, d):
        return pl.BlockSpec((self.b, self.rt, d), lambda i, j: (i, j, 0))

    def mod_spec(self, k):
        return pl.BlockSpec((1, self.b, 1, D_MODEL), lambda i, j: (k, i, 0, 0))

    def flat_spec(self, c):
        nrb = self.nrb
        return pl.BlockSpec((self.rows, c), lambda i, j: (i * nrb + j, 0))

    def tab_spec(self):
        return pl.BlockSpec((self.rows, LANES), lambda i, j: (j, 0))


def _full_spec(shape):
    nd = len(shape)
    return pl.BlockSpec(shape, lambda *_: (0,) * nd)


def _adaln_kernel(c_ref, w_ref, b_ref, o_ref):
    c = c_ref[...]
    a = (c * jax.nn.sigmoid(c)).astype(BF16)
    o_ref[0] = _dot(a, w_ref[0].astype(BF16)) + b_ref[0]


def _adaln(c_all, ada_w, ada_b):
    depth, d, n6 = ada_w.shape
    rows = c_all.shape[0]
    tn = 1536
    return pl.pallas_call(
        _adaln_kernel,
        grid=(depth, n6 // tn),
        in_specs=[pl.BlockSpec((rows, d), lambda l, j: (0, 0)),
                  pl.BlockSpec((1, d, tn), lambda l, j: (l, 0, j)),
                  pl.BlockSpec((1, 1, tn), lambda l, j: (l, 0, j))],
        out_specs=pl.BlockSpec((1, rows, tn), lambda l, j: (l, 0, j)),
        out_shape=jax.ShapeDtypeStruct((depth, rows, n6), F32),
        compiler_params=_cparams(("parallel", "parallel")),
        name="adaln",
    )(c_all, ada_w, ada_b.reshape(depth, 1, n6))


def _swap_half(x, half):
    if 2 * half == LANES:
        return pltpu.roll(x, half, 1)
    lane = _iota((1, LANES), 1)
    lo = (lane % (2 * half)) < half
    return jnp.where(lo, pltpu.roll(x, LANES - half, 1), pltpu.roll(x, half, 1))


def _even_inproj_kernel(*refs, k_feature_major):
    if k_feature_major:
        (x_ref, sh_ref, sc_ref, nw_ref, w_ref, wg_ref, c64_ref, s64_ref, c128_ref, s128_ref, wkt_ref, ct_ref, st_ref,
         qa_ref, ka_ref, va_ref, qb_ref, cmp_ref, sel_ref, win_ref, gt_ref) = refs
    else:
        (x_ref, sh_ref, sc_ref, nw_ref, w_ref, wg_ref, c64_ref, s64_ref, c128_ref, s128_ref,
         qa_ref, ka_ref, va_ref, qb_ref, cmp_ref, sel_ref, win_ref, gt_ref) = refs
    h = _norm_mod(x_ref[...], nw_ref[...], sh_ref[0], sc_ref[0])
    h = h.reshape(-1, D_MODEL).astype(BF16)
    c64, s64, c128, s128 = c64_ref[...], s64_ref[...], c128_ref[...], s128_ref[...]

    def rope64(p):
        return p * c64 + _swap_half(p, DK_A // 2) * s64

    def rope128(p):
        return p * c128 + _swap_half(p, DH_B // 2) * s128

    def project(ref, off, ropes):
        p = _dot(h, w_ref[:, off:off + 4 * LANES])
        for j, rope in enumerate(ropes):
            sl = slice(j * LANES, (j + 1) * LANES)
            ref[:, sl] = p[:, sl] if rope is None else rope(p[:, sl])

    project(qa_ref, 0, [rope64] * 4)
    if k_feature_major:
        kt = _dot_nt(wkt_ref[...], h)
        ct, st = ct_ref[...], st_ref[...]
        half = DK_A // 2
        for grp in range(H_A * 2):
            x1 = kt[grp * DK_A:grp * DK_A + half]
            x2 = kt[grp * DK_A + half:(grp + 1) * DK_A]
            ka_ref[0, grp * DK_A:grp * DK_A + half, :] = x1 * ct - x2 * st
            ka_ref[0, grp * DK_A + half:(grp + 1) * DK_A, :] = x2 * ct + x1 * st
    else:
        project(ka_ref, 512, [rope64] * 4)
    project(va_ref, 1024, [None] * 4)
    project(qb_ref, 1536, [rope128] * 4)
    for t, ref in enumerate((cmp_ref, sel_ref, win_ref)):
        project(ref, 2048 + t * 512, [rope128, rope128, None, None])
    gt_ref[...] = jax.nn.sigmoid(_dot(h, wg_ref[...]))


def _even_inproj(til, x, mods, nw, w_main, w_gate, tabs, kt_tabs=None):
    rows_total = til.nb * til.r
    widths = (512, 512, 512, 512, 512, 512, 512, LANES)
    in_specs = [til.x_spec(D_MODEL), til.mod_spec(0), til.mod_spec(1), _full_spec((1, D_MODEL)),
                _full_spec(w_main.shape), _full_spec(w_gate.shape)] + [til.tab_spec()] * 4
    args = [x, mods, mods, nw, w_main, w_gate, *tabs]
    out_specs = [til.flat_spec(c) for c in widths]
    out_shape = [jax.ShapeDtypeStruct((rows_total, c), F32) for c in widths]
    if kt_tabs is not None:
        assert til.b == 1
        w_kt = jnp.transpose(w_main[:, 512:1024])
        in_specs += [_full_spec(w_kt.shape)] + [pl.BlockSpec((DK_A // 2, til.rt), lambda i, j: (0, j))] * 2
        args += [w_kt, *kt_tabs]
        out_specs[1] = pl.BlockSpec((1, 512, til.rt), lambda i, j: (i, 0, j))
        out_shape[1] = jax.ShapeDtypeStruct((til.nb, 512, til.r), F32)
    return pl.pallas_call(
        functools.partial(_even_inproj_kernel, k_feature_major=kt_tabs is not None),
        grid=til.grid,
        in_specs=in_specs,
        out_specs=out_specs,
        out_shape=out_shape,
        compiler_params=_cparams(("parallel", "parallel")),
        name="even_inproj",
    )(*args)


def _for_tiles(lo, hi, fn):
    n = hi - lo

    def pair(j, carry):
        fn(lo + 2 * j)
        fn(lo + 2 * j + 1)
        return carry

    lax.fori_loop(0, n // 2, pair, 0)

    @pl.when(n % 2 == 1)
    def _():
        fn(hi - 1)


def _row_sumsq(x):
    xf = x.astype(F32)
    return _dot((xf * xf).astype(BF16), jnp.ones((LANES, LANES), BF16))


def _key_norm_bound(k_ref, n_keys, tile):
    def body(kb, m):
        k = k_ref[0, pl.ds(pl.multiple_of(kb * tile, tile), tile), :].astype(BF16)
        return jnp.maximum(m, _row_sumsq(k))

    m = lax.fori_loop(0, n_keys // tile, body, jnp.zeros((tile, LANES), F32))
    return jnp.max(m, axis=0, keepdims=True)


def _score_bound(q2, kmax2):
    qmax2 = jnp.max(_row_sumsq(q2), axis=0, keepdims=True)
    return jnp.sqrt(qmax2 * kmax2) * 1.05


def _two_pass_attention(mx_ref, acc_ref, lo, hi, last, scores, values, mask_body=False, bound=None):
    def lane_max(s):
        m = s[:, 0:LANES]
        for c in range(1, s.shape[1] // LANES):
            m = jnp.maximum(m, s[:, c * LANES:(c + 1) * LANES])
        return m

    def pass1(kb, masked):
        mx_ref[...] = jnp.maximum(mx_ref[...], lane_max(scores(kb, masked)))

    def exact_max():
        mx_ref[...] = jnp.full(mx_ref.shape, MASK_NEG, F32)
        _for_tiles(lo, hi, lambda kb: pass1(kb, mask_body))
        for kb in last:
            pass1(kb, True)
        mx_ref[...] = jnp.broadcast_to(jnp.max(mx_ref[...], axis=-1, keepdims=True), mx_ref.shape)

    if bound is None:
        exact_max()
    else:
        safe = jnp.max(bound) <= SAFE_SHIFT

        @pl.when(safe)
        def _():
            mx_ref[...] = jnp.broadcast_to(bound, mx_ref.shape)

        @pl.when(jnp.logical_not(safe))
        def _():
            exact_max()

    acc_ref[...] = jnp.zeros(acc_ref.shape, F32)

    def pass2(kb, masked):
        s = scores(kb, masked)
        m = mx_ref[...]
        p = jnp.concatenate([jnp.exp2(s[:, c * LANES:(c + 1) * LANES] - m) for c in range(s.shape[1] // LANES)],
                            axis=-1).astype(BF16)
        v = values(kb)
        v1 = jnp.concatenate([v, jnp.ones(v.shape, BF16)], axis=-1)
        acc_ref[...] += _dot(p, v1)

    _for_tiles(lo, hi, lambda kb: pass2(kb, mask_body))
    for kb in last:
        pass2(kb, True)
    return acc_ref[:, 0:LANES] / acc_ref[:, LANES:2 * LANES]


def _diff_lambda(lw, lam_init):
    a = jnp.sum(lw[0:1] * lw[1:2], axis=-1, keepdims=True)
    b = jnp.sum(lw[2:3] * lw[3:4], axis=-1, keepdims=True)
    return jnp.exp(a) - jnp.exp(b) + lam_init


def _diff_finalize(o2, tq, lam, sub_w, lam_init):
    od = o2[:tq] - lam * o2[tq:]
    return _rms(od, sub_w) * (1.0 - lam_init)


def _diff_flash_kernel(q_ref, k_ref, v_ref, lamw_ref, sub_ref, o_ref, q2_ref, mx_ref, acc_ref, kmax_ref, *,
                       tq, tk, lam_init):
    i = pl.program_id(2)

    def key_tile(kb):
        return k_ref[0, :, pl.ds(pl.multiple_of(kb * tk, tk), tk)].astype(BF16)

    @pl.when(i == 0)
    def _():
        def body(kb, m):
            kf = key_tile(kb).astype(F32)
            return jnp.maximum(m, jnp.sum(kf * kf, axis=0, keepdims=True))

        m = lax.fori_loop(0, k_ref.shape[2] // tk, body, jnp.zeros((1, tk), F32))
        kmax_ref[...] = jnp.broadcast_to(jnp.max(m, axis=-1, keepdims=True), kmax_ref.shape)

    q = q_ref[0] * (DK_A ** -0.5 * LOG2E)
    lane = _iota((1, LANES), 1)
    q2_ref[0:tq] = jnp.where(lane < DK_A, q, 0.0).astype(BF16)
    q2_ref[tq:2 * tq] = jnp.where(lane >= DK_A, q, 0.0).astype(BF16)
    bound = _score_bound(q2_ref[...], kmax_ref[0:1])

    def scores(kb, masked):
        s = _dot(q2_ref[...], key_tile(kb))
        if masked:
            r = _iota((2 * tq, tk), 0)
            qp = i * tq + jnp.where(r >= tq, r - tq, r)
            s = jnp.where(kb * tk + _iota((2 * tq, tk), 1) <= qp, s, MASK_NEG)
        return s

    def values(kb):
        return v_ref[0, pl.ds(pl.multiple_of(kb * tk, tk), tk), :].astype(BF16)

    per = tq // tk
    o2 = _two_pass_attention(mx_ref, acc_ref, 0, i * per, [i * per + u for u in range(per)], scores, values,
                             bound=bound)
    o_ref[0] = _diff_finalize(o2, tq, _diff_lambda(lamw_ref[...], lam_init), sub_ref[...], lam_init)


def _diff_attn_prompt(qa, kat, va, lam_w, sub_w, lam_init, tq, tk):
    n, t, _ = qa.shape
    kern = functools.partial(_diff_flash_kernel, tq=tq, tk=tk, lam_init=lam_init)
    return pl.pallas_call(
        kern,
        grid=(n, H_A, t // tq),
        in_specs=[pl.BlockSpec((1, tq, LANES), lambda b, h, i: (b, i, h)),
                  pl.BlockSpec((1, LANES, t), lambda b, h, i: (b, h, 0)),
                  pl.BlockSpec((1, t, LANES), lambda b, h, i: (b, 0, h)),
                  _full_spec(lam_w.shape), _full_spec(sub_w.shape)],
        out_specs=pl.BlockSpec((1, tq, LANES), lambda b, h, i: (b, i, h)),
        out_shape=jax.ShapeDtypeStruct((n, t, H_A * DV_A), F32),
        scratch_shapes=[pltpu.VMEM((2 * tq, LANES), BF16), pltpu.VMEM((2 * tq, LANES), F32),
                        pltpu.VMEM((2 * tq, 2 * LANES), F32), pltpu.VMEM((8, LANES), F32)],
        compiler_params=_cparams(("parallel", "parallel", "arbitrary")),
        name="diff_attn_prompt",
    )(qa, kat, va, lam_w, sub_w)


def _sel_onehot(rows, first_block):
    blk = _iota((rows, LANES), 0) // SEL_BLOCK + first_block
    return jnp.where(blk == _iota((rows, LANES), 1), 1.0, 0.0).astype(BF16)


def _nsa_flash_kernel(*refs, tq, tk, use_sel):
    if use_sel:
        q_ref, selm_ref, k_ref, v_ref, o_ref, q2_ref, mx_ref, acc_ref, kmax_ref = refs
    else:
        q_ref, k_ref, v_ref, o_ref, q2_ref, mx_ref, acc_ref, kmax_ref = refs
    i = pl.program_id(2)

    @pl.when(i == 0)
    def _():
        kmax_ref[...] = jnp.broadcast_to(_key_norm_bound(k_ref, k_ref.shape[1], tk), kmax_ref.shape)

    q = q_ref[0] * (DH_B ** -0.5 * LOG2E)
    for r in range(R_B):
        q2_ref[r * tq:(r + 1) * tq, 0:LANES] = q[:, r * LANES:(r + 1) * LANES].astype(BF16)
        if use_sel:
            q2_ref[r * tq:(r + 1) * tq, LANES:2 * LANES] = selm_ref[0, 0]
    bound = _score_bound(q2_ref[:, 0:LANES], kmax_ref[0:1])

    def scores(kb, masked):
        k = k_ref[0, pl.ds(pl.multiple_of(kb * tk, tk), tk), :].astype(BF16)
        if use_sel:
            k = jnp.concatenate([k, _sel_onehot(tk, kb * (tk // SEL_BLOCK))], axis=-1)
        s = _dot_nt(q2_ref[...], k)
        if masked:
            r = _iota((R_B * tq, tk), 0)
            qp = i * tq + jnp.where(r >= tq, r - tq, r)
            kp = kb * tk + _iota((R_B * tq, tk), 1)
            ok = kp <= qp
            if not use_sel:
                ok = ok & (kp > qp - WINDOW)
            s = jnp.where(ok, s, MASK_NEG)
        return s

    def values(kb):
        return v_ref[0, pl.ds(pl.multiple_of(kb * tk, tk), tk), :].astype(BF16)

    per = tq // tk
    lo = 0 if use_sel else jnp.maximum(i * per - WINDOW // tk, 0)
    o = _two_pass_attention(mx_ref, acc_ref, lo, i * per, [i * per + u for u in range(per)], scores, values,
                            mask_body=not use_sel, bound=bound)
    for r in range(R_B):
        o_ref[0, :, r * LANES:(r + 1) * LANES] = o[r * tq:(r + 1) * tq]


def _nsa_attn_prompt(qb, kv, selm, tq, tk):
    n, t, _ = qb.shape
    use_sel = selm is not None
    kd = 2 * LANES if use_sel else LANES
    kern = functools.partial(_nsa_flash_kernel, tq=tq, tk=tk, use_sel=use_sel)
    in_specs = [pl.BlockSpec((1, tq, R_B * LANES), lambda b, g, i: (b, i, g))]
    args = [qb]
    if use_sel:
        in_specs.append(pl.BlockSpec((1, 1, tq, LANES), lambda b, g, i: (b, g, i, 0)))
        args.append(selm)
    in_specs += [pl.BlockSpec((1, t, LANES), lambda b, g, i: (b, 0, g)),
                 pl.BlockSpec((1, t, LANES), lambda b, g, i: (b, 0, G_B + g))]
    args += [kv, kv]
    return pl.pallas_call(
        kern,
        grid=(n, G_B, t // tq),
        in_specs=in_specs,
        out_specs=pl.BlockSpec((1, tq, R_B * LANES), lambda b, g, i: (b, i, g)),
        out_shape=jax.ShapeDtypeStruct((n, t, H_B * DH_B), F32),
        scratch_shapes=[pltpu.VMEM((R_B * tq, kd), BF16), pltpu.VMEM((R_B * tq, LANES), F32),
                        pltpu.VMEM((R_B * tq, 2 * LANES), F32), pltpu.VMEM((8, LANES), F32)],
        compiler_params=_cparams(("parallel", "parallel", "arbitrary")),
        name="nsa_sel_prompt" if use_sel else "nsa_win_prompt",
    )(*args)


def _compress_core(load, wc_ref, pe_ref, w2_ref, wi, nchunk):
    rows = G_B * nchunk
    acc = jnp.zeros((rows, 2 * LANES), F32)
    pew = jnp.zeros((16, 2 * LANES), F32)
    for u in range(CMP_STRIDE // 2):
        lhs = jnp.concatenate([load(2 * u), load(2 * u + 1)], axis=-1).astype(BF16)
        w = wc_ref[wi, u]
        acc = acc + _dot(lhs, w)
        pew = pew + _dot(pe_ref[wi, u], w)
    first = acc[:, :LANES]
    second = pltpu.roll(acc[:, LANES:], rows - 1, 0)
    hid = first + second + pew[0:1, :LANES] + pew[8:9, LANES:]
    hid = hid * jax.nn.sigmoid(hid)
    return _dot(hid.astype(BF16), w2_ref[wi])


def _compress_prompt_kernel(*refs, nchunk):
    x_refs = refs[:G_B]
    wc_ref, pe_ref, w2_ref, o_ref = refs[G_B:]

    def load(tok):
        return jnp.concatenate([xr[0, pl.ds(tok, nchunk, stride=CMP_STRIDE), :] for xr in x_refs], axis=0)

    out = _compress_core(load, wc_ref, pe_ref, w2_ref, 0, nchunk)
    for g in range(G_B):
        o_ref[0, 0, g] = out[g * nchunk:(g + 1) * nchunk]


def _compress_sample_kernel(pt_ref, *refs, n_pages, spb):
    pg_refs = refs[:spb * n_pages]
    wc_ref, pe_ref, w2_ref, o_ref = refs[spb * n_pages:]
    cps = PAGE_SIZE // CMP_STRIDE
    nchunk = n_pages * cps
    for kv in range(2):
        def load(tok):
            return jnp.concatenate(
                [pg[0, pl.ds(tok * SLOTS + kv * G_B + g, cps, stride=CMP_STRIDE * SLOTS), :]
                 for smp in range(spb) for g in range(G_B) for pg in pg_refs[smp * n_pages:(smp + 1) * n_pages]],
                axis=0)

        out = _compress_core(load, wc_ref, pe_ref, w2_ref, kv, spb * nchunk)
        for smp in range(spb):
            for g in range(G_B):
                seg = smp * G_B + g
                o_ref[smp, kv, g] = out[seg * nchunk:(seg + 1) * nchunk]


def _compress_weights(cmp_pe, cmp_w1, cmp_w2):
    w1 = cmp_w1.reshape(2, CMP_LEN, DH_B, DH_B)
    wab = jnp.concatenate([w1[:, :CMP_STRIDE], w1[:, CMP_STRIDE:]], axis=-1)
    wc = wab.reshape(2, CMP_STRIDE // 2, 2 * DH_B, 2 * DH_B).astype(BF16)
    pa = cmp_pe[:, :CMP_STRIDE].reshape(2, CMP_STRIDE // 2, 1, 2 * DH_B)
    pb = cmp_pe[:, CMP_STRIDE:].reshape(2, CMP_STRIDE // 2, 1, 2 * DH_B)
    z = jnp.zeros((2, CMP_STRIDE // 2, 7, 2 * DH_B), F32)
    pe = jnp.concatenate([pa, z, pb, z], axis=2).astype(BF16)
    return wc, pe, cmp_w2.astype(BF16)


def _compress_prompt(cmp_kv, wc, pe, w2):
    n, t, _ = cmp_kv.shape
    nchunk = t // CMP_STRIDE
    kern = functools.partial(_compress_prompt_kernel, nchunk=nchunk)
    return pl.pallas_call(
        kern,
        grid=(n, 2),
        in_specs=[pl.BlockSpec((1, t, DH_B), functools.partial(lambda b, kv, g: (b, 0, kv * G_B + g), g=g))
                  for g in range(G_B)] + [
                  pl.BlockSpec((1,) + wc.shape[1:], lambda b, kv: (kv, 0, 0, 0)),
                  pl.BlockSpec((1,) + pe.shape[1:], lambda b, kv: (kv, 0, 0, 0)),
                  pl.BlockSpec((1, DH_B, DH_B), lambda b, kv: (kv, 0, 0))],
        out_specs=pl.BlockSpec((1, 1, G_B, nchunk, DH_B), lambda b, kv: (b, kv, 0, 0, 0)),
        out_shape=jax.ShapeDtypeStruct((n, 2, G_B, nchunk, DH_B), F32),
        compiler_params=_cparams(("parallel", "parallel")),
        name="compress_prompt",
    )(*([cmp_kv] * G_B), wc, pe, w2)


def _compress_sample(cache, page_table, wc, pe, w2):
    n, n_pages = page_table.shape
    nchunk = n_pages * (PAGE_SIZE // CMP_STRIDE)
    spb = math.gcd(n, 2)
    kern = functools.partial(_compress_sample_kernel, n_pages=n_pages, spb=spb)
    page_specs = [pl.BlockSpec((1, PAGE_SIZE * SLOTS, LANES),
                               functools.partial(lambda b, pt, smp, j: (pt[b * spb + smp, j], 0, 0), smp=smp, j=j))
                  for smp in range(spb) for j in range(n_pages)]
    grid_spec = pltpu.PrefetchScalarGridSpec(
        num_scalar_prefetch=1,
        grid=(n // spb,),
        in_specs=page_specs + [
            pl.BlockSpec(wc.shape, lambda b, pt: (0, 0, 0, 0)),
            pl.BlockSpec(pe.shape, lambda b, pt: (0, 0, 0, 0)),
            pl.BlockSpec(w2.shape, lambda b, pt: (0, 0, 0))],
        out_specs=pl.BlockSpec((spb, 2, G_B, nchunk, DH_B), lambda b, pt: (b, 0, 0, 0, 0)),
    )
    return pl.pallas_call(
        kern,
        grid_spec=grid_spec,
        out_shape=jax.ShapeDtypeStruct((n, 2, G_B, nchunk, DH_B), F32),
        compiler_params=_cparams(("parallel",)),
        name="compress_sample",
    )(page_table, *([cache] * (spb * n_pages)), wc, pe, w2)


def _cmp_topk_kernel(q_ref, kc_ref, vc_ref, ocmp_ref, selm_ref, p_scr, v_scr, *,
                     nb, rq, nck, n_cmp, n_sel, nsp, pos_base):
    tq = nb * rq
    j = pl.program_id(2)
    scale = DH_B ** -0.5
    kidx = _iota((1, nck), 1)
    qpos_c = pos_base + j * rq + _iota((rq, 1), 0)
    valid = (kidx < n_cmp) & (CMP_STRIDE * kidx + (CMP_LEN - 1) <= qpos_c)
    for b in range(nb):
        kcc = kc_ref[b, 0, 0].astype(BF16)
        vcc = vc_ref[b, 0, 0].astype(BF16)
        q = q_ref[b] * scale
        q2 = jnp.concatenate([q[:, r * LANES:(r + 1) * LANES] for r in range(R_B)], axis=0).astype(BF16)
        valid2 = jnp.concatenate([valid] * R_B, axis=0)
        s = jnp.where(valid2, _dot_nt(q2, kcc), MASK_NEG)
        m = jnp.max(s, axis=-1, keepdims=True)
        p = jnp.where(valid2, jnp.exp(s - m), 0.0)
        p = p / jnp.maximum(jnp.sum(p, axis=-1, keepdims=True), 1e-30)
        o = _dot(p.astype(BF16), vcc)
        psum = jnp.zeros((rq, nck), F32)
        for r in range(R_B):
            ocmp_ref[b, :, r * LANES:(r + 1) * LANES] = o[r * rq:(r + 1) * rq]
            psum = psum + p[r * rq:(r + 1) * rq]
        p_scr[b * rq:(b + 1) * rq, :] = psum
    psum = p_scr[...]
    p_hi = psum.astype(BF16)
    p_lo = (psum - p_hi.astype(F32)).astype(BF16)
    srow = _iota((LANES, nck), 0)
    kcol = _iota((LANES, nck), 1)
    cov = (CMP_STRIDE * kcol < SEL_BLOCK * srow + SEL_BLOCK) & (CMP_STRIDE * kcol + CMP_LEN > SEL_BLOCK * srow)
    cov = cov & (srow < n_sel) & (kcol < n_cmp)
    cov_t = jnp.where(cov, 1.0, 0.0).astype(BF16)
    imp_t = _dot_nt(cov_t, p_hi) + _dot_nt(cov_t, p_lo)
    qpos_l = pos_base + j * rq + _iota((1, tq), 1) % rq
    sb = _iota((nsp, 1), 0)
    valid_b = (sb < n_sel) & (sb * SEL_BLOCK <= qpos_l)
    forced = (sb == 0) | (sb == qpos_l // SEL_BLOCK)
    vals = jnp.where(forced, jnp.inf, jnp.where(valid_b, imp_t[:nsp], -jnp.inf))
    v_scr[...] = vals

    def body(jj, cnt):
        vj = v_scr[pl.ds(jj, 1), :]
        tie = jnp.where(sb > jj, 1.0, 0.0)
        return cnt + jnp.where(vj > vals, 1.0, jnp.where(vj == vals, tie, 0.0))

    cnt = lax.fori_loop(0, nsp, body, jnp.zeros((nsp, tq), F32))
    keep = valid_b & (cnt < float(min(TOP_N, n_sel)))
    selm_t = jnp.where(keep, 0.0, -SEL_NEG)
    if nsp < LANES:
        selm_t = jnp.concatenate([selm_t, jnp.zeros((LANES - nsp, tq), F32)], axis=0)
    selm = selm_t.T.astype(BF16)
    for b in range(nb):
        selm_ref[b, 0] = selm[b * rq:(b + 1) * rq]


def _cmp_topk(qb, ccmp, *, nb, rq, n_cmp, n_sel, pos_base):
    n, r, _ = qb.shape
    nck = ccmp.shape[3]
    nsp = -(-n_sel // 8) * 8
    tq = nb * rq
    kern = functools.partial(_cmp_topk_kernel, nb=nb, rq=rq, nck=nck, n_cmp=n_cmp, n_sel=n_sel, nsp=nsp,
                             pos_base=pos_base)
    return pl.pallas_call(
        kern,
        grid=(n // nb, G_B, r // rq),
        in_specs=[pl.BlockSpec((nb, rq, R_B * LANES), lambda a, g, j: (a, j, g)),
                  pl.BlockSpec((nb, 1, 1, nck, DH_B), lambda a, g, j: (a, 0, g, 0, 0)),
                  pl.BlockSpec((nb, 1, 1, nck, DH_B), lambda a, g, j: (a, 1, g, 0, 0))],
        out_specs=[pl.BlockSpec((nb, rq, R_B * LANES), lambda a, g, j: (a, j, g)),
                   pl.BlockSpec((nb, 1, rq, LANES), lambda a, g, j: (a, g, j, 0))],
        out_shape=[jax.ShapeDtypeStruct((n, r, H_B * DH_B), F32),
                   jax.ShapeDtypeStruct((n, G_B, r, LANES), BF16)],
        scratch_shapes=[pltpu.VMEM((tq, nck), F32), pltpu.VMEM((nsp, tq), F32)],
        compiler_params=_cparams(("parallel", "parallel", "parallel")),
        name="cmp_topk",
    )(qb, ccmp, ccmp)


def _attend_scores(pieces):
    m = None
    for s, _ in pieces:
        ms = jnp.max(s, axis=-1, keepdims=True)
        m = ms if m is None else jnp.maximum(m, ms)
    l = jnp.zeros_like(m)
    acc = jnp.zeros((m.shape[0], LANES), F32)
    for s, v in pieces:
        p = jnp.exp(s - m)
        l = l + jnp.sum(p, axis=-1, keepdims=True)
        acc = acc + _dot(p.astype(BF16), v)
    return acc / l


def _attend_pieces(q2, pieces):
    scored = []
    for k, v, mask in pieces:
        s = _dot_nt(q2, k)
        scored.append((s if mask is None else jnp.where(mask, s, MASK_NEG), v))
    return _attend_scores(scored)


def _pad_rows(x, rows):
    return jnp.concatenate([x, jnp.zeros((rows - x.shape[0], x.shape[1]), x.dtype)], axis=0)


def _new_token_mask(m_rows, s):
    r = _iota((m_rows, LANES), 0) % s
    return _iota((m_rows, LANES), 1) <= r


def _diff_sample_kernel(pt_ref, *refs, n_pages, s, lam_init):
    kp_refs = refs[:n_pages]
    vp_refs = refs[n_pages:2 * n_pages]
    q_ref, kn_ref, vn_ref, lamw_ref, sub_ref, o_ref = refs[2 * n_pages:]
    lane = _iota((1, LANES), 1)
    lam = _diff_lambda(lamw_ref[...], lam_init)
    new_mask = _new_token_mask(2 * s, s)
    for h in range(H_A):
        cols = slice(h * LANES, (h + 1) * LANES)
        q = q_ref[0, :, cols] * (DK_A ** -0.5)
        q2 = jnp.concatenate([jnp.where(lane < DK_A, q, 0.0), jnp.where(lane >= DK_A, q, 0.0)], axis=0).astype(BF16)
        kt_past = jnp.concatenate([r[0, cols, :].astype(BF16) for r in kp_refs], axis=1)
        s_past = _dot(q2, kt_past)
        vpast = jnp.concatenate([_slot_rows(r, h, PAGE_SIZE).astype(BF16) for r in vp_refs], axis=0)
        knew = _pad_rows(kn_ref[0, :, cols], LANES).astype(BF16)
        vnew = _pad_rows(vn_ref[0, :, cols], LANES).astype(BF16)
        s_new = jnp.where(new_mask, _dot_nt(q2, knew), MASK_NEG)
        o2 = _attend_scores([(s_past, vpast), (s_new, vnew)])
        o_ref[0, :, cols] = _diff_finalize(o2, s, lam, sub_ref[...], lam_init)


def _page_specs(n_pages, rows, width):
    return [pl.BlockSpec((1, rows, width), functools.partial(lambda b, pt, j: (pt[b, j], 0, 0), j=j))
            for j in range(n_pages)]


def _slot_rows(ref, slot, n_tok):
    return ref[0, pl.ds(slot, n_tok, stride=SLOTS), :]


def _diff_attn_sample(qa, ka, va, cache_k, cache_v, page_table, lam_w, sub_w, lam_init):
    n, s, _ = qa.shape
    n_pages = page_table.shape[1]
    kern = functools.partial(_diff_sample_kernel, n_pages=n_pages, s=s, lam_init=lam_init)
    row_spec = pl.BlockSpec((1, s, 512), lambda b, pt: (b, 0, 0))
    grid_spec = pltpu.PrefetchScalarGridSpec(
        num_scalar_prefetch=1,
        grid=(n,),
        in_specs=_page_specs(n_pages, 512, PAGE_SIZE) + _page_specs(n_pages, PAGE_SIZE * SLOTS, LANES)
        + [row_spec, row_spec, row_spec,
           pl.BlockSpec(lam_w.shape, lambda b, pt: (0, 0)), pl.BlockSpec(sub_w.shape, lambda b, pt: (0, 0))],
        out_specs=row_spec,
    )
    return pl.pallas_call(
        kern,
        grid_spec=grid_spec,
        out_shape=jax.ShapeDtypeStruct((n, s, 512), F32),
        compiler_params=_cparams(("parallel",)),
        name="diff_attn_sample",
    )(page_table, *([cache_k] * n_pages), *([cache_v] * n_pages), qa, ka, va, lam_w, sub_w)


def _sel_sample_kernel(pt_ref, *refs, n_pages, s):
    pg_refs = refs[:n_pages]
    q_ref, selm_ref, new_ref, o_ref = refs[n_pages:]
    past = n_pages * PAGE_SIZE
    new_mask = _new_token_mask(R_B * s, s)
    oh_past = _sel_onehot(past, 0)
    new_blk = past // SEL_BLOCK
    oh_new = jnp.where((_iota((LANES, LANES), 1) == new_blk) & (_iota((LANES, LANES), 0) < s), 1.0, 0.0)
    oh_new = oh_new.astype(BF16)
    scale = DH_B ** -0.5
    for g in range(G_B):
        kcols = slice(g * LANES, (g + 1) * LANES)
        vcols = slice((G_B + g) * LANES, (G_B + g + 1) * LANES)
        selm = selm_ref[0, g]
        q2 = jnp.concatenate(
            [jnp.concatenate([(q_ref[0, :, (g * R_B + r) * LANES:(g * R_B + r + 1) * LANES] * scale).astype(BF16),
                              selm], axis=-1) for r in range(R_B)], axis=0)
        kpast = jnp.concatenate([_slot_rows(r, g, PAGE_SIZE).astype(BF16) for r in pg_refs], axis=0)
        vpast = jnp.concatenate([_slot_rows(r, G_B + g, PAGE_SIZE).astype(BF16) for r in pg_refs], axis=0)
        kpast = jnp.concatenate([kpast, oh_past], axis=-1)
        knew = jnp.concatenate([_pad_rows(new_ref[0, :, kcols], LANES).astype(BF16), oh_new], axis=-1)
        vnew = _pad_rows(new_ref[0, :, vcols], LANES).astype(BF16)
        o2 = _attend_pieces(q2, [(kpast, vpast, None), (knew, vnew, new_mask)])
        for r in range(R_B):
            o_ref[0, :, (g * R_B + r) * LANES:(g * R_B + r + 1) * LANES] = o2[r * s:(r + 1) * s]


def _sel_attn_sample(qb, selm, sel_new, cache, page_table):
    n, s, _ = qb.shape
    n_pages = page_table.shape[1]
    assert s <= SEL_BLOCK and (n_pages * PAGE_SIZE) % SEL_BLOCK == 0
    kern = functools.partial(_sel_sample_kernel, n_pages=n_pages, s=s)
    row_spec = pl.BlockSpec((1, s, 512), lambda b, pt: (b, 0, 0))
    grid_spec = pltpu.PrefetchScalarGridSpec(
        num_scalar_prefetch=1,
        grid=(n,),
        in_specs=_page_specs(n_pages, PAGE_SIZE * SLOTS, LANES)
        + [row_spec, pl.BlockSpec((1, G_B, s, LANES), lambda b, pt: (b, 0, 0, 0)), row_spec],
        out_specs=row_spec,
    )
    return pl.pallas_call(
        kern,
        grid_spec=grid_spec,
        out_shape=jax.ShapeDtypeStruct((n, s, 512), F32),
        compiler_params=_cparams(("parallel",)),
        name="nsa_sel_sample",
    )(page_table, *([cache] * n_pages), qb, selm, sel_new)


def _win_sample_kernel(q_ref, buf_ref, new_ref, o_ref, wout_ref, *, s, wb, spb):
    new_mask = _new_token_mask(R_B * s, s)
    r = _iota((R_B * s, wb), 0) % s
    buf_mask = _iota((R_B * s, wb), 1) > r + (wb - WINDOW)
    scale = DH_B ** -0.5
    keep = (wb - s) * SLOTS
    for b in range(spb):
        for g in range(G_B):
            kcols = slice(g * LANES, (g + 1) * LANES)
            vcols = slice((G_B + g) * LANES, (G_B + g + 1) * LANES)
            q2 = jnp.concatenate(
                [(q_ref[b, :, (g * R_B + r_) * LANES:(g * R_B + r_ + 1) * LANES] * scale).astype(BF16)
                 for r_ in range(R_B)], axis=0)
            kbuf = buf_ref[b, pl.ds(g, wb, stride=SLOTS), :].astype(BF16)
            vbuf = buf_ref[b, pl.ds(G_B + g, wb, stride=SLOTS), :].astype(BF16)
            knew = _pad_rows(new_ref[b, :, kcols], LANES).astype(BF16)
            vnew = _pad_rows(new_ref[b, :, vcols], LANES).astype(BF16)
            o2 = _attend_pieces(q2, [(kbuf, vbuf, buf_mask), (knew, vnew, new_mask)])
            for r_ in range(R_B):
                o_ref[b, :, (g * R_B + r_) * LANES:(g * R_B + r_ + 1) * LANES] = o2[r_ * s:(r_ + 1) * s]
        wout_ref[b, 0:keep] = buf_ref[b, s * SLOTS:wb * SLOTS]
        for slot in range(SLOTS):
            wout_ref[b, pl.ds(keep + slot, s, stride=SLOTS), :] = new_ref[b, :, slot * LANES:(slot + 1) * LANES]


def _win_attn_sample(qb, win_buf, win_new, past):
    n, s, _ = qb.shape
    wb = win_buf.shape[1] // SLOTS
    assert past >= wb and wb % 8 == 0 and s % 8 == 0
    spb = math.gcd(n, 4)
    kern = functools.partial(_win_sample_kernel, s=s, wb=wb, spb=spb)
    row_spec = pl.BlockSpec((spb, s, 512), lambda b: (b, 0, 0))
    buf_spec = pl.BlockSpec((spb, wb * SLOTS, LANES), lambda b: (b, 0, 0))
    return pl.pallas_call(
        kern,
        grid=(n // spb,),
        in_specs=[row_spec, buf_spec, row_spec],
        out_specs=[row_spec, buf_spec],
        out_shape=[jax.ShapeDtypeStruct((n, s, 512), F32), jax.ShapeDtypeStruct((n, wb * SLOTS, LANES), F32)],
        compiler_params=_cparams(("parallel",)),
        name="nsa_win_sample",
    )(qb, win_buf, win_new)


def _even_out_kernel(y_ref, g1_ref, oa_ref, oc_ref, os_ref, ow_ref, gt_ref, w_ref, o_ref):
    gt = gt_ref[...]
    parts = [oa_ref[...].astype(BF16)]
    for hb in range(H_B):
        sl = slice(hb * LANES, (hb + 1) * LANES)
        ob = (gt[:, 3 * hb:3 * hb + 1] * oc_ref[:, sl] + gt[:, 3 * hb + 1:3 * hb + 2] * os_ref[:, sl]
              + gt[:, 3 * hb + 2:3 * hb + 3] * ow_ref[:, sl])
        parts.append(ob.astype(BF16))
    out = _dot(jnp.concatenate(parts, axis=-1), w_ref[...])
    y = y_ref[...]
    o_ref[...] = y + g1_ref[0] * out.reshape(y.shape)


def _even_out(til, y, mods, oa, oc, os_, ow, gt, w_out):
    return pl.pallas_call(
        _even_out_kernel,
        grid=til.grid,
        in_specs=[til.x_spec(D_MODEL), til.mod_spec(2), til.flat_spec(512), til.flat_spec(512),
                  til.flat_spec(512), til.flat_spec(512), til.flat_spec(LANES), _full_spec(w_out.shape)],
        out_specs=til.x_spec(D_MODEL),
        out_shape=jax.ShapeDtypeStruct(y.shape, F32),
        compiler_params=_cparams(("parallel", "parallel")),
        name="even_out",
    )(y, mods, oa, oc, os_, ow, gt, w_out)


def _mlp_kernel(y_ref, sh_ref, sc_ref, g_ref, nw_ref, w1_ref, w2_ref, fw_ref, o_ref, *, final):
    y = y_ref[...]
    h = _norm_mod(y, nw_ref[...], sh_ref[0], sc_ref[0]).reshape(-1, D_MODEL).astype(BF16)
    a = jnp.maximum(_dot(h, w1_ref[...]), 0.0)
    out = _dot((a * a).astype(BF16), w2_ref[...])
    y2 = y + g_ref[0] * out.reshape(y.shape)
    if final:
        y2 = _rms(y2, fw_ref[...])
    o_ref[...] = y2


def _mlp(til, y, mods, nw, w1, w2, fw, final):
    kern = functools.partial(_mlp_kernel, final=final)
    return pl.pallas_call(
        kern,
        grid=til.grid,
        in_specs=[til.x_spec(D_MODEL), til.mod_spec(3), til.mod_spec(4), til.mod_spec(5),
                  _full_spec((1, D_MODEL)), _full_spec(w1.shape), _full_spec(w2.shape), _full_spec((1, D_MODEL))],
        out_specs=til.x_spec(D_MODEL),
        out_shape=jax.ShapeDtypeStruct(y.shape, F32),
        compiler_params=_cparams(("parallel", "parallel")),
        name="mlp",
    )(y, mods, mods, mods, nw, w1, w2, fw)


def _gla_inproj_kernel(x_ref, sh_ref, sc_ref, nw_ref, w_ref, wgl_ref, wgate_ref, bgate_ref,
                       q_ref, k_ref, v_ref, r_ref, la_ref):
    h = _norm_mod(x_ref[...], nw_ref[...], sh_ref[0], sc_ref[0]).reshape(-1, D_MODEL).astype(BF16)
    nk = H_C * DK_C
    nv = H_C * DV_C
    q_ref[...] = _dot(h, w_ref[:, 0:nk]) * (DK_C ** -0.5)
    k_ref[...] = _dot(h, w_ref[:, nk:2 * nk])
    v_ref[...] = _dot(h, w_ref[:, 2 * nk:2 * nk + nv])
    r_ref[...] = _dot(h, w_ref[:, 2 * nk + nv:2 * nk + 2 * nv])
    gl = _dot(h, wgl_ref[...])
    x = _dot(gl.astype(BF16), wgate_ref[...]) + bgate_ref[...]
    log_sig = jnp.minimum(x, 0.0) - jnp.log1p(jnp.exp(-jnp.abs(x)))
    la_ref[...] = log_sig / GATE_TAU


def _gla_inproj(til, x, mods, nw, w_main, w_gl, w_gate, b_gate):
    rows_total = til.nb * til.r
    widths = (H_C * DK_C, H_C * DK_C, H_C * DV_C, H_C * DV_C, H_C * DK_C)
    return pl.pallas_call(
        _gla_inproj_kernel,
        grid=til.grid,
        in_specs=[til.x_spec(D_MODEL), til.mod_spec(0), til.mod_spec(1), _full_spec((1, D_MODEL)),
                  _full_spec(w_main.shape), _full_spec(w_gl.shape), _full_spec(w_gate.shape),
                  _full_spec(b_gate.shape)],
        out_specs=[til.flat_spec(c) for c in widths],
        out_shape=[jax.ShapeDtypeStruct((rows_total, c), F32) for c in widths],
        compiler_params=_cparams(("parallel", "parallel")),
        name="gla_inproj",
    )(x, mods, mods, nw, w_main, w_gl, w_gate, b_gate)


def _cumsum_rows(g):
    c = g.shape[0]
    row = _iota((c, 1), 0)
    b = g
    shift = 1
    while shift < c:
        b = b + jnp.where(row >= shift, pltpu.roll(b, shift, 0), 0.0)
        shift *= 2
    return b


def _gla_prep(q, k, g, c, sub, n_real):
    b = _cumsum_rows(g)
    qe = (q * jnp.exp(b)).astype(BF16)
    lane_c = _iota((sub, c), 1)
    row_s = _iota((sub, c), 0)
    att_rows = []
    for blk in range(c // sub):
        lo = blk * sub
        qi, ki, bi = q[lo:lo + sub], k[lo:lo + sub], b[lo:lo + sub]
        diag = jnp.zeros((sub, c), F32)
        for jj in range(min(sub, max(n_real - lo, 0))):
            e = jnp.exp(jnp.minimum(bi - bi[jj:jj + 1], 0.0))
            col = jnp.sum(qi * ki[jj:jj + 1] * e, axis=-1, keepdims=True)
            diag = jnp.where(lane_c == lo + jj, col, diag)
        att = jnp.where(lane_c - lo <= row_s, diag, 0.0)
        if blk > 0:
            bs = b[lo - 1:lo]
            q_in = qi * jnp.exp(bi - bs)
            k_out = k * jnp.exp(jnp.minimum(bs - b, 0.0))
            att = jnp.where(lane_c < lo, _dot_nt(q_in.astype(BF16), k_out.astype(BF16)), att)
        att_rows.append(att)
    att = att_rows[0] if len(att_rows) == 1 else jnp.concatenate(att_rows, axis=0)
    bl = b[c - 1:c]
    kd = (k * jnp.exp(bl - b)).astype(BF16)
    eye = _iota((DK_C, DK_C), 0) == _iota((DK_C, DK_C), 1)
    decay = jnp.sum(jnp.where(eye, jnp.exp(bl), 0.0), axis=-1, keepdims=True)
    return qe, att, kd, decay


def _gla_prep_bounded(q, k, g, c, sub, n_real):
    del sub, n_real
    b = _cumsum_rows(g)
    r = b[0:1]
    bl = b[c - 1:c]
    q_in = q * jnp.exp(b - r)
    k_out = k * jnp.exp(r - b)
    att = jnp.where(_iota((c, c), 0) >= _iota((c, c), 1), _dot_nt(q_in.astype(BF16), k_out.astype(BF16)), 0.0)
    qe = (q_in * jnp.exp(r)).astype(BF16)
    kd = (k_out * jnp.exp(bl - r)).astype(BF16)
    eye = _iota((DK_C, DK_C), 0) == _iota((DK_C, DK_C), 1)
    decay = jnp.sum(jnp.where(eye, jnp.exp(bl), 0.0), axis=-1, keepdims=True)
    return qe, att, kd, decay


def _gla_apply(state, prep, v):
    qe, att, kd, decay = prep
    o = _dot(qe, state.astype(BF16)) + _dot(att, v)
    return o, decay * state + _dot_tn(kd, v.astype(BF16))


def _gla_rec_kernel(*refs, tt, c, sub, hp, has_s0):
    if has_s0:
        q_ref, k_ref, v_ref, g_ref, s0_ref, o_ref, sfin_ref, s_ref = refs
    else:
        q_ref, k_ref, v_ref, g_ref, o_ref, sfin_ref, s_ref = refs
    t = pl.program_id(2)

    @pl.when(t == 0)
    def _():
        s_ref[...] = s0_ref[0] if has_s0 else jnp.zeros(s_ref.shape, F32)

    def kcols(hh):
        return slice(hh * DK_C, (hh + 1) * DK_C)

    def vcols(hh):
        return slice(hh * DV_C, (hh + 1) * DV_C)

    def run(prep_fn):
        if tt < c:
            pad = lambda x: _pad_rows(x, c)
            for hh in range(hp):
                prep = prep_fn(pad(q_ref[0, :, kcols(hh)]), pad(k_ref[0, :, kcols(hh)]),
                               pad(g_ref[0, :, kcols(hh)]), c, sub, tt)
                o, s_ref[hh] = _gla_apply(s_ref[hh], prep, pad(v_ref[0, :, vcols(hh)]))
                o_ref[0, :, vcols(hh)] = o[:tt]
            return
        per_trip = 2 if (tt // c) % 2 == 0 else 1

        def body(ci, carry):
            for hh in range(hp):
                rows = [pl.ds(pl.multiple_of((ci * per_trip + u) * c, c), c) for u in range(per_trip)]
                preps = [prep_fn(q_ref[0, r, kcols(hh)], k_ref[0, r, kcols(hh)], g_ref[0, r, kcols(hh)],
                                 c, sub, c) for r in rows]
                state = s_ref[hh]
                for r, prep in zip(rows, preps):
                    o_ref[0, r, vcols(hh)], state = _gla_apply(state, prep, v_ref[0, r, vcols(hh)])
                s_ref[hh] = state
            return carry
        lax.fori_loop(0, tt // c // per_trip, body, 0)

    g_all = g_ref[0]
    if tt > c:
        chunk_decay = -jnp.sum(g_all.reshape(tt // c, c, g_all.shape[-1]), axis=1)
    else:
        chunk_decay = -jnp.sum(g_all, axis=0, keepdims=True)
    bounded = jnp.max(chunk_decay) <= GLA_SAFE_DECAY

    @pl.when(bounded)
    def _():
        run(_gla_prep_bounded)

    @pl.when(jnp.logical_not(bounded))
    def _():
        run(_gla_prep)

    @pl.when(t == pl.num_programs(2) - 1)
    def _():
        sfin_ref[0] = s_ref[...]


def _gla_recurrence(q, k, v, g, s0, tt, c, sub, hp):
    n, t, _ = q.shape
    kern = functools.partial(_gla_rec_kernel, tt=tt, c=c, sub=sub, hp=hp, has_s0=s0 is not None)
    kspec = pl.BlockSpec((1, tt, hp * DK_C), lambda b, h, i: (b, i, h))
    vspec = pl.BlockSpec((1, tt, hp * DV_C), lambda b, h, i: (b, i, h))
    sspec = pl.BlockSpec((1, hp, DK_C, DV_C), lambda b, h, i: (b, h, 0, 0))
    in_specs = [kspec, kspec, vspec, kspec]
    args = [q, k, v, g]
    if s0 is not None:
        in_specs.append(sspec)
        args.append(s0)
    return pl.pallas_call(
        kern,
        grid=(n, H_C // hp, t // tt),
        in_specs=in_specs,
        out_specs=[vspec, sspec],
        out_shape=[jax.ShapeDtypeStruct((n, t, H_C * DV_C), F32),
                   jax.ShapeDtypeStruct((n, H_C, DK_C, DV_C), F32)],
        scratch_shapes=[pltpu.VMEM((hp, DK_C, DV_C), F32)],
        compiler_params=_cparams(("parallel", "parallel", "arbitrary")),
        name="gla_recurrence",
    )(*args)


def _gla_out_kernel(y_ref, g1_ref, o_ref_in, r_ref, nw_ref, w_ref, out_ref):
    parts = []
    for h in range(H_C):
        sl = slice(h * DV_C, (h + 1) * DV_C)
        r = r_ref[:, sl]
        parts.append((_rms(o_ref_in[:, sl], nw_ref[...]) * (r * jax.nn.sigmoid(r))).astype(BF16))
    out = _dot(jnp.concatenate(parts, axis=-1), w_ref[...])
    y = y_ref[...]
    out_ref[...] = y + g1_ref[0] * out.reshape(y.shape)


def _gla_out(til, y, mods, o, r, nw, w_out):
    return pl.pallas_call(
        _gla_out_kernel,
        grid=til.grid,
        in_specs=[til.x_spec(D_MODEL), til.mod_spec(2), til.flat_spec(H_C * DV_C), til.flat_spec(H_C * DV_C),
                  _full_spec(nw.shape), _full_spec(w_out.shape)],
        out_specs=til.x_spec(D_MODEL),
        out_shape=jax.ShapeDtypeStruct(y.shape, F32),
        compiler_params=_cparams(("parallel", "parallel")),
        name="gla_out",
    )(y, mods, o, r, nw, w_out)


def _rope_tables(pos, d):
    inv = ROPE_THETA ** (-jnp.arange(0, d, 2, dtype=F32) / d)
    ang = pos.astype(F32)[:, None] * inv[None, :]
    cos, sin = jnp.cos(ang), jnp.sin(ang)
    rep = LANES // d
    c = jnp.tile(jnp.concatenate([cos, cos], axis=-1), (1, rep))
    s = jnp.tile(jnp.concatenate([-sin, sin], axis=-1), (1, rep))
    return c, s


def _rope_tables_t(pos, d):
    inv = ROPE_THETA ** (-jnp.arange(0, d, 2, dtype=F32) / d)
    ang = inv[:, None] * pos.astype(F32)[None, :]
    return jnp.cos(ang), jnp.sin(ang)


def _mods_for(mod_l, lo, hi):
    nb = hi - lo
    return mod_l[lo:hi].reshape(nb, 6, D_MODEL).transpose(1, 0, 2).reshape(6, nb, 1, D_MODEL)


def _prompt_tile_rows(t):
    return math.gcd(t, 256)


def _even_layer(yp, ys, mods_p, mods_s, caches, page_table, wts, lam_init, til_p, til_s):
    (c_dk, c_dv, c_cmp, c_sel, win_buf) = caches
    n, t, _ = yp.shape
    ns, s, _ = ys.shape
    n_pages = page_table.shape[1]
    past = n_pages * PAGE_SIZE
    w_in = wts["w_in"]
    n_main = w_in.shape[1] - 3 * H_B
    w_main = w_in[:, :n_main].astype(BF16)
    w_gate = jnp.pad(w_in[:, n_main:], ((0, 0), (0, LANES - 3 * H_B))).astype(BF16)
    nw1 = wts["norm1"].reshape(1, D_MODEL)
    lam_w = wts["lam_w"]
    sub_w = wts["sub_w"].reshape(1, DV_A)
    wc, pe, w2c = _compress_weights(wts["cmp_pe"], wts["cmp_w1"], wts["cmp_w2"])
    w_out = wts["w_out"].astype(BF16)

    tabs_p = _rope_tables(jnp.arange(t), DK_A) + _rope_tables(jnp.arange(t), DH_B)
    kt_tabs = _rope_tables_t(jnp.arange(t), DK_A)
    qa, kat, va, qb, cmp_kv, sel_kv, win_kv, gt = _even_inproj(til_p, yp, mods_p, nw1, w_main, w_gate, tabs_p, kt_tabs)
    r3 = lambda a: a.reshape(n, t, a.shape[-1])
    tq = _prompt_tile_rows(t)
    tq_attn = math.gcd(t, 2 * tq)
    oa = _diff_attn_prompt(r3(qa), kat, r3(va), lam_w, sub_w, lam_init, tq_attn, tq)
    ccmp = _compress_prompt(r3(cmp_kv), wc, pe, w2c)
    n_cmp = (t - CMP_LEN) // CMP_STRIDE + 1
    n_sel = -(-t // SEL_BLOCK)
    o_cmp, selm = _cmp_topk(r3(qb), ccmp, nb=1, rq=tq, n_cmp=n_cmp, n_sel=n_sel, pos_base=0)
    o_sel = _nsa_attn_prompt(r3(qb), r3(sel_kv), selm, tq_attn, tq)
    o_win = _nsa_attn_prompt(r3(qb), r3(win_kv), None, tq_attn, tq)
    f2 = lambda a: a.reshape(n * t, a.shape[-1])
    yp = _even_out(til_p, yp, mods_p, f2(oa), f2(o_cmp), f2(o_sel), f2(o_win), gt, w_out)
    wl = min(WINDOW, t)
    ka_state = jnp.transpose(kat.reshape(n, H_A, 2, DK_A, t), (0, 4, 1, 2, 3))
    st_p = (ka_state, r3(va).reshape(n, t, H_A, DV_A),
            r3(cmp_kv).reshape(n, t, 2, G_B, DH_B), r3(sel_kv).reshape(n, t, 2, G_B, DH_B),
            r3(win_kv)[:, t - wl:].reshape(n, wl, 2, G_B, DH_B))

    pos_s = past + jnp.arange(s)
    tabs_s = tuple(jnp.tile(x, (til_s.b, 1)) for x in _rope_tables(pos_s, DK_A) + _rope_tables(pos_s, DH_B))
    qa, ka, va, qb, cmp_kv, sel_kv, win_kv, gt = _even_inproj(til_s, ys, mods_s, nw1, w_main, w_gate, tabs_s)
    r3 = lambda a: a.reshape(ns, s, a.shape[-1])
    slot_cache = lambda c: c.reshape(c.shape[0], PAGE_SIZE * SLOTS, LANES)
    c_dkt = jnp.transpose(c_dk, (0, 2, 3, 4, 1)).reshape(c_dk.shape[0], H_A * 2 * DK_A, PAGE_SIZE)
    oa = _diff_attn_sample(r3(qa), r3(ka), r3(va), c_dkt, slot_cache(c_dv), page_table, lam_w, sub_w, lam_init)
    ccmp = _compress_sample(slot_cache(c_cmp), page_table, wc, pe, w2c)
    total = past + s
    n_cmp = (total - CMP_LEN) // CMP_STRIDE + 1
    n_sel = -(-total // SEL_BLOCK)
    assert n_cmp <= ccmp.shape[3] and n_sel <= LANES
    nb = math.gcd(ns, LANES // s)
    o_cmp, selm = _cmp_topk(r3(qb), ccmp, nb=nb, rq=s, n_cmp=n_cmp, n_sel=n_sel, pos_base=past)
    o_sel = _sel_attn_sample(r3(qb), selm, r3(sel_kv), slot_cache(c_sel), page_table)
    wb = win_buf.shape[1]
    o_win, win_out = _win_attn_sample(r3(qb), win_buf.reshape(ns, wb * SLOTS, LANES), r3(win_kv), past)
    f2 = lambda a: a.reshape(ns * s, a.shape[-1])
    ys = _even_out(til_s, ys, mods_s, f2(oa), f2(o_cmp), f2(o_sel), f2(o_win), gt, w_out)
    st_s = (r3(ka).reshape(ns, s, H_A, 2, DK_A), r3(va).reshape(ns, s, H_A, DV_A),
            r3(cmp_kv).reshape(ns, s, 2, G_B, DH_B), r3(sel_kv).reshape(ns, s, 2, G_B, DH_B),
            win_out.reshape(ns, wb, 2, G_B, DH_B))
    return yp, ys, st_p, st_s


def _odd_layer(yp, ys, mods_p, mods_s, s0, wts, til_p, til_s):
    w_in = wts["w_in"]
    n_main = w_in.shape[1] - GATE_RANK
    w_main = w_in[:, :n_main].astype(BF16)
    w_gl = jnp.pad(w_in[:, n_main:], ((0, 0), (0, LANES - GATE_RANK))).astype(BF16)
    w_gate = jnp.pad(wts["w_gate"], ((0, LANES - GATE_RANK), (0, 0))).astype(BF16)
    b_gate = wts["b_gate"].reshape(1, -1)
    nw1 = wts["norm1"].reshape(1, D_MODEL)
    gnw = wts["gnorm"].reshape(1, DV_C)
    w_out = wts["w_out"].astype(BF16)
    outs = []
    for y, mods, til, state in ((yp, mods_p, til_p, None), (ys, mods_s, til_s, s0)):
        n, t, _ = y.shape
        q, k, v, r, la = _gla_inproj(til, y, mods, nw1, w_main, w_gl, w_gate, b_gate)
        r3 = lambda a: a.reshape(n, t, a.shape[-1])
        c = math.gcd(t, GLA_CHUNK)
        if c >= GLA_SUB:
            tt, cc, sub = math.gcd(t, 8 * c), c, GLA_SUB_LONG
        else:
            tt, cc, sub = t, GLA_SUB, GLA_SUB
        o, s_fin = _gla_recurrence(r3(q), r3(k), r3(v), r3(la), state, tt, cc, sub, H_C)
        y = _gla_out(til, y, mods, o.reshape(n * t, -1), r, gnw, w_out)
        outs.append((y, s_fin))
    return outs[0][0], outs[1][0], outs[0][1], outs[1][1]


def kernel(x_prompt, x_sample, c_prompt, c_sample, cache_diff_k, cache_diff_v, cache_cmp_kv, cache_sel_kv,
           state_win_kv, state_gla, page_table, norm1_w, norm2_w, ada_w, ada_b, even_w_in, even_w_out,
           diff_lambda_w, diff_subln_w, cmp_pe, cmp_w1, cmp_w2, gla_w_in, gla_w_gate, gla_b_gate, gla_norm_w,
           gla_w_out, mlp_w1, mlp_w2, final_norm_w):
    depth = ada_w.shape[0]
    n, t, _ = x_prompt.shape
    ns, s, _ = x_sample.shape
    til_p = _Tiling(n, t, 1, _prompt_tile_rows(t))
    til_s = _Tiling(ns, s, math.gcd(ns, 256 // s), s)

    pad = (-(n + ns)) % 8
    c_all = jnp.concatenate([c_prompt, c_sample, jnp.zeros((pad, D_MODEL), F32)], axis=0)
    mod = _adaln(c_all, ada_w, ada_b)

    yp, ys = x_prompt, x_sample
    st_p = [[] for _ in range(6)]
    st_s = [[] for _ in range(6)]
    fw = final_norm_w.reshape(1, D_MODEL)
    for l in range(depth):
        mods_p = _mods_for(mod[l], 0, n)
        mods_s = _mods_for(mod[l], n, n + ns)
        if l % 2 == 0:
            e = l // 2
            lam_init = 0.8 - 0.6 * math.exp(-0.3 * l)
            wts = dict(w_in=even_w_in[e], w_out=even_w_out[e], lam_w=diff_lambda_w[e], sub_w=diff_subln_w[e],
                       cmp_pe=cmp_pe[e], cmp_w1=cmp_w1[e], cmp_w2=cmp_w2[e], norm1=norm1_w[l])
            caches = (cache_diff_k[e], cache_diff_v[e], cache_cmp_kv[e], cache_sel_kv[e], state_win_kv[e])
            yp, ys, sp, ss = _even_layer(yp, ys, mods_p, mods_s, caches, page_table, wts, lam_init, til_p, til_s)
            for i in range(5):
                st_p[i].append(sp[i])
                st_s[i].append(ss[i])
        else:
            o = l // 2
            wts = dict(w_in=gla_w_in[o], w_gate=gla_w_gate[o], b_gate=gla_b_gate[o], gnorm=gla_norm_w[o],
                       w_out=gla_w_out[o], norm1=norm1_w[l])
            yp, ys, gp, gs = _odd_layer(yp, ys, mods_p, mods_s, state_gla[o], wts, til_p, til_s)
            st_p[5].append(gp)
            st_s[5].append(gs)
        final = l == depth - 1
        w1 = mlp_w1[l].astype(BF16)
        w2 = mlp_w2[l].astype(BF16)
        nw2 = norm2_w[l].reshape(1, D_MODEL)
        yp = _mlp(til_p, yp, mods_p, nw2, w1, w2, fw, final)
        ys = _mlp(til_s, ys, mods_s, nw2, w1, w2, fw, final)
    outs_p = [jnp.stack(x, axis=0) for x in st_p]
    outs_s = [jnp.stack(x, axis=0) for x in st_s]
    return (yp, ys, *outs_p, *outs_s)
```

```python
import functools
import math

import jax
import jax.numpy as jnp
from jax import lax
from jax.experimental import pallas as pl
from jax.experimental.pallas import tpu as pltpu

F32 = jnp.float32
BF16 = jnp.bfloat16

D_MODEL = 1024
PAGE_SIZE = 128
H_A = 4
DK_A = 64
DV_A = 128
H_B = 4
G_B = 2
R_B = 2
DH_B = 128
CMP_LEN = 32
CMP_STRIDE = 16
SEL_BLOCK = 64
TOP_N = 16
WINDOW = 512
H_C = 4
DK_C = 128
DV_C = 256
GATE_RANK = 16
GATE_TAU = 16.0
GLA_CHUNK = 64
GLA_SUB = 16
GLA_SUB_LONG = 16
GLA_SAFE_DECAY = 80.0
D_FF = 4 * D_MODEL
ROPE_THETA = 10000.0
EPS = 1e-6

LANES = 128
SLOTS = 4
MASK_NEG = -1e30
LOG2E = 1.4426950408889634
SAFE_SHIFT = 56.0
SEL_NEG = 32768.0
VMEM_LIMIT_MB = 56


def _cparams(sem, vmem_mb=VMEM_LIMIT_MB):
    return pltpu.CompilerParams(dimension_semantics=sem, vmem_limit_bytes=vmem_mb * 1024 * 1024)


def _dot(a, b):
    return jnp.dot(a, b, preferred_element_type=F32)


def _dot_nt(a, b):
    return lax.dot_general(a, b, (((1,), (1,)), ((), ())), preferred_element_type=F32)


def _dot_tn(a, b):
    return lax.dot_general(a, b, (((0,), (0,)), ((), ())), preferred_element_type=F32)


def _iota(shape, dim):
    return lax.broadcasted_iota(jnp.int32, shape, dim)


def _rms(x, w):
    ms = jnp.mean(x * x, axis=-1, keepdims=True)
    return x * lax.rsqrt(ms + EPS) * w


def _norm_mod(x, nw, shift, scale):
    return _rms(x, nw) * (1.0 + scale) + shift


class _Tiling:
    def __init__(self, nb, r, b, rt):
        assert nb % b == 0 and r % rt == 0 and (b == 1 or rt == r)
        self.nb, self.r, self.b, self.rt = nb, r, b, rt
        self.grid = (nb // b, r // rt)
        self.rows = b * rt
        self.nrb = r // rt

    def x_spec(self, d):
        return pl.BlockSpec((self.b, self.rt, d), lambda i, j: (i, j, 0))

    def mod_spec(self, k):
        return pl.BlockSpec((1, self.b, 1, D_MODEL), lambda i, j: (k, i, 0, 0))

    def flat_spec(self, c):
        nrb = self.nrb
        return pl.BlockSpec((self.rows, c), lambda i, j: (i * nrb + j, 0))

    def tab_spec(self):
        return pl.BlockSpec((self.rows, LANES), lambda i, j: (j, 0))


def _full_spec(shape):
    nd = len(shape)
    return pl.BlockSpec(shape, lambda *_: (0,) * nd)


def _const_spec(shape):
    nd = len(shape)
    return pl.BlockSpec(shape, lambda *_: (0,) * nd, pipeline_mode=pl.Buffered(1))


def _adaln_kernel(c_ref, w_ref, b_ref, o_ref):
    c = c_ref[...]
    a = (c * jax.nn.sigmoid(c)).astype(BF16)
    o_ref[0] = _dot(a, w_ref[0].astype(BF16)) + b_ref[0]


def _adaln(c_all, ada_w, ada_b):
    depth, d, n6 = ada_w.shape
    rows = c_all.shape[0]
    tn = 1536
    return pl.pallas_call(
        _adaln_kernel,
        grid=(depth, n6 // tn),
        in_specs=[pl.BlockSpec((rows, d), lambda l, j: (0, 0)),
                  pl.BlockSpec((1, d, tn), lambda l, j: (l, 0, j)),
                  pl.BlockSpec((1, 1, tn), lambda l, j: (l, 0, j))],
        out_specs=pl.BlockSpec((1, rows, tn), lambda l, j: (l, 0, j)),
        out_shape=jax.ShapeDtypeStruct((depth, rows, n6), F32),
        compiler_params=_cparams(("parallel", "parallel")),
        name="adaln",
    )(c_all, ada_w, ada_b.reshape(depth, 1, n6))


def _swap_half(x, half):
    if 2 * half == LANES:
        return pltpu.roll(x, half, 1)
    lane = _iota((1, LANES), 1)
    lo = (lane % (2 * half)) < half
    return jnp.where(lo, pltpu.roll(x, LANES - half, 1), pltpu.roll(x, half, 1))


def _even_inproj_kernel(*refs, k_feature_major):
    if k_feature_major:
        (x_ref, sh_ref, sc_ref, nw_ref, w_ref, wg_ref, c64_ref, s64_ref, c128_ref, s128_ref, wkt_ref, ct_ref, st_ref,
         qa_ref, ka_ref, va_ref, qb_ref, cmp_ref, sel_ref, win_ref, gt_ref) = refs
    else:
        (x_ref, sh_ref, sc_ref, nw_ref, w_ref, wg_ref, c64_ref, s64_ref, c128_ref, s128_ref,
         qa_ref, ka_ref, va_ref, qb_ref, cmp_ref, sel_ref, win_ref, gt_ref) = refs
    h = _norm_mod(x_ref[...], nw_ref[...], sh_ref[0], sc_ref[0])
    h = h.reshape(-1, D_MODEL).astype(BF16)
    c64, s64, c128, s128 = c64_ref[...], s64_ref[...], c128_ref[...], s128_ref[...]

    def rope64(p):
        return p * c64 + _swap_half(p, DK_A // 2) * s64

    def rope128(p):
        return p * c128 + _swap_half(p, DH_B // 2) * s128

    def project(ref, off, ropes):
        p = _dot(h, w_ref[:, off:off + 4 * LANES])
        for j, rope in enumerate(ropes):
            sl = slice(j * LANES, (j + 1) * LANES)
            ref[:, sl] = p[:, sl] if rope is None else rope(p[:, sl])

    project(qa_ref, 0, [rope64] * 4)
    if k_feature_major:
        kt = _dot_nt(wkt_ref[...], h)
        ct, st = ct_ref[...], st_ref[...]
        half = DK_A // 2
        for grp in range(H_A * 2):
            x1 = kt[grp * DK_A:grp * DK_A + half]
            x2 = kt[grp * DK_A + half:(grp + 1) * DK_A]
            ka_ref[0, grp * DK_A:grp * DK_A + half, :] = x1 * ct - x2 * st
            ka_ref[0, grp * DK_A + half:(grp + 1) * DK_A, :] = x2 * ct + x1 * st
    else:
        project(ka_ref, 512, [rope64] * 4)
    project(va_ref, 1024, [None] * 4)
    project(qb_ref, 1536, [rope128] * 4)
    for t, ref in enumerate((cmp_ref, sel_ref, win_ref)):
        project(ref, 2048 + t * 512, [rope128, rope128, None, None])
    gt_ref[...] = jax.nn.sigmoid(_dot(h, wg_ref[...]))


def _even_inproj(til, x, mods, nw, w_main, w_gate, tabs, kt_tabs=None):
    rows_total = til.nb * til.r
    widths = (512, 512, 512, 512, 512, 512, 512, LANES)
    in_specs = [til.x_spec(D_MODEL), til.mod_spec(0), til.mod_spec(1), _full_spec((1, D_MODEL)),
                _full_spec(w_main.shape), _full_spec(w_gate.shape)] + [til.tab_spec()] * 4
    args = [x, mods, mods, nw, w_main, w_gate, *tabs]
    out_specs = [til.flat_spec(c) for c in widths]
    out_shape = [jax.ShapeDtypeStruct((rows_total, c), F32) for c in widths]
    if kt_tabs is not None:
        assert til.b == 1
        w_kt = jnp.transpose(w_main[:, 512:1024])
        in_specs += [_full_spec(w_kt.shape)] + [pl.BlockSpec((DK_A // 2, til.rt), lambda i, j: (0, j))] * 2
        args += [w_kt, *kt_tabs]
        out_specs[1] = pl.BlockSpec((1, 512, til.rt), lambda i, j: (i, 0, j))
        out_shape[1] = jax.ShapeDtypeStruct((til.nb, 512, til.r), F32)
    return pl.pallas_call(
        functools.partial(_even_inproj_kernel, k_feature_major=kt_tabs is not None),
        grid=til.grid,
        in_specs=in_specs,
        out_specs=out_specs,
        out_shape=out_shape,
        compiler_params=_cparams(("parallel", "parallel")),
        name="even_inproj",
    )(*args)


def _for_tiles(lo, hi, fn):
    n = hi - lo

    def pair(j, carry):
        fn(lo + 2 * j)
        fn(lo + 2 * j + 1)
        return carry

    lax.fori_loop(0, n // 2, pair, 0)

    @pl.when(n % 2 == 1)
    def _():
        fn(hi - 1)


def _row_sumsq(x):
    xf = x.astype(F32)
    return _dot((xf * xf).astype(BF16), jnp.ones((LANES, LANES), BF16))


def _key_norm_bound(k_ref, n_keys, tile):
    def body(kb, m):
        k = k_ref[0, pl.ds(pl.multiple_of(kb * tile, tile), tile), :].astype(BF16)
        return jnp.maximum(m, _row_sumsq(k))

    m = lax.fori_loop(0, n_keys // tile, body, jnp.zeros((tile, LANES), F32))
    return jnp.max(m, axis=0, keepdims=True)


def _score_bound(q2, kmax2):
    qmax2 = jnp.max(_row_sumsq(q2), axis=0, keepdims=True)
    return jnp.sqrt(qmax2 * kmax2) * 1.05


def _two_pass_attention(mx_ref, acc_ref, lo, hi, last, scores, values, mask_body=False, bound=None):
    def lane_max(s):
        m = s[:, 0:LANES]
        for c in range(1, s.shape[1] // LANES):
            m = jnp.maximum(m, s[:, c * LANES:(c + 1) * LANES])
        return m

    def pass1(kb, masked):
        mx_ref[...] = jnp.maximum(mx_ref[...], lane_max(scores(kb, masked)))

    def exact_max():
        mx_ref[...] = jnp.full(mx_ref.shape, MASK_NEG, F32)
        _for_tiles(lo, hi, lambda kb: pass1(kb, mask_body))
        for kb in last:
            pass1(kb, True)
        mx_ref[...] = jnp.broadcast_to(jnp.max(mx_ref[...], axis=-1, keepdims=True), mx_ref.shape)

    if bound is None:
        exact_max()
    else:
        safe = jnp.max(bound) <= SAFE_SHIFT

        @pl.when(safe)
        def _():
            mx_ref[...] = jnp.broadcast_to(bound, mx_ref.shape)

        @pl.when(jnp.logical_not(safe))
        def _():
            exact_max()

    acc_ref[...] = jnp.zeros(acc_ref.shape, F32)

    def pass2(kb, masked):
        s = scores(kb, masked)
        m = mx_ref[...]
        p = jnp.concatenate([jnp.exp2(s[:, c * LANES:(c + 1) * LANES] - m) for c in range(s.shape[1] // LANES)],
                            axis=-1).astype(BF16)
        v = values(kb)
        v1 = jnp.concatenate([v, jnp.ones(v.shape, BF16)], axis=-1)
        acc_ref[...] += _dot(p, v1)

    _for_tiles(lo, hi, lambda kb: pass2(kb, mask_body))
    for kb in last:
        pass2(kb, True)
    return acc_ref[:, 0:LANES] / acc_ref[:, LANES:2 * LANES]


def _diff_lambda(lw, lam_init):
    a = jnp.sum(lw[0:1] * lw[1:2], axis=-1, keepdims=True)
    b = jnp.sum(lw[2:3] * lw[3:4], axis=-1, keepdims=True)
    return jnp.exp(a) - jnp.exp(b) + lam_init


def _diff_finalize(o2, tq, lam, sub_w, lam_init):
    od = o2[:tq] - lam * o2[tq:]
    return _rms(od, sub_w) * (1.0 - lam_init)


def _diff_flash_kernel(q_ref, k_ref, v_ref, lamw_ref, sub_ref, o_ref, q2_ref, mx_ref, acc_ref, kmax_ref, *,
                       tq, tk, lam_init):
    i = pl.program_id(2)

    def key_tile(kb):
        return k_ref[0, :, pl.ds(pl.multiple_of(kb * tk, tk), tk)].astype(BF16)

    @pl.when(i == 0)
    def _():
        def body(kb, m):
            kf = key_tile(kb).astype(F32)
            return jnp.maximum(m, jnp.sum(kf * kf, axis=0, keepdims=True))

        m = lax.fori_loop(0, k_ref.shape[2] // tk, body, jnp.zeros((1, tk), F32))
        kmax_ref[...] = jnp.broadcast_to(jnp.max(m, axis=-1, keepdims=True), kmax_ref.shape)

    q = q_ref[0] * (DK_A ** -0.5 * LOG2E)
    lane = _iota((1, LANES), 1)
    q2_ref[0:tq] = jnp.where(lane < DK_A, q, 0.0).astype(BF16)
    q2_ref[tq:2 * tq] = jnp.where(lane >= DK_A, q, 0.0).astype(BF16)
    bound = _score_bound(q2_ref[...], kmax_ref[0:1])

    def scores(kb, masked):
        s = _dot(q2_ref[...], key_tile(kb))
        if masked:
            r = _iota((2 * tq, tk), 0)
            qp = i * tq + jnp.where(r >= tq, r - tq, r)
            s = jnp.where(kb * tk + _iota((2 * tq, tk), 1) <= qp, s, MASK_NEG)
        return s

    def values(kb):
        return v_ref[0, pl.ds(pl.multiple_of(kb * tk, tk), tk), :].astype(BF16)

    per = tq // tk
    o2 = _two_pass_attention(mx_ref, acc_ref, 0, i * per, [i * per + u for u in range(per)], scores, values,
                             bound=bound)
    o_ref[0] = _diff_finalize(o2, tq, _diff_lambda(lamw_ref[...], lam_init), sub_ref[...], lam_init)


def _diff_attn_prompt(qa, kat, va, lam_w, sub_w, lam_init, tq, tk):
    n, t, _ = qa.shape
    kern = functools.partial(_diff_flash_kernel, tq=tq, tk=tk, lam_init=lam_init)
    return pl.pallas_call(
        kern,
        grid=(n, H_A, t // tq),
        in_specs=[pl.BlockSpec((1, tq, LANES), lambda b, h, i: (b, i, h)),
                  pl.BlockSpec((1, LANES, t), lambda b, h, i: (b, h, 0)),
                  pl.BlockSpec((1, t, LANES), lambda b, h, i: (b, 0, h)),
                  _full_spec(lam_w.shape), _full_spec(sub_w.shape)],
        out_specs=pl.BlockSpec((1, tq, LANES), lambda b, h, i: (b, i, h)),
        out_shape=jax.ShapeDtypeStruct((n, t, H_A * DV_A), F32),
        scratch_shapes=[pltpu.VMEM((2 * tq, LANES), BF16), pltpu.VMEM((2 * tq, LANES), F32),
                        pltpu.VMEM((2 * tq, 2 * LANES), F32), pltpu.VMEM((8, LANES), F32)],
        compiler_params=_cparams(("parallel", "parallel", "arbitrary")),
        name="diff_attn_prompt",
    )(qa, kat, va, lam_w, sub_w)


def _sel_onehot(rows, first_block):
    blk = _iota((rows, LANES), 0) // SEL_BLOCK + first_block
    return jnp.where(blk == _iota((rows, LANES), 1), 1.0, 0.0).astype(BF16)


def _nsa_flash_kernel(*refs, tq, tk, use_sel):
    if use_sel:
        q_ref, selm_ref, k_ref, v_ref, o_ref, q2_ref, mx_ref, acc_ref, kmax_ref = refs
    else:
        q_ref, k_ref, v_ref, o_ref, q2_ref, mx_ref, acc_ref, kmax_ref = refs
    i = pl.program_id(2)

    @pl.when(i == 0)
    def _():
        kmax_ref[...] = jnp.broadcast_to(_key_norm_bound(k_ref, k_ref.shape[1], tk), kmax_ref.shape)

    q = q_ref[0] * (DH_B ** -0.5 * LOG2E)
    for r in range(R_B):
        q2_ref[r * tq:(r + 1) * tq, 0:LANES] = q[:, r * LANES:(r + 1) * LANES].astype(BF16)
        if use_sel:
            q2_ref[r * tq:(r + 1) * tq, LANES:2 * LANES] = selm_ref[0, 0]
    bound = _score_bound(q2_ref[:, 0:LANES], kmax_ref[0:1])

    def scores(kb, masked):
        k = k_ref[0, pl.ds(pl.multiple_of(kb * tk, tk), tk), :].astype(BF16)
        if use_sel:
            k = jnp.concatenate([k, _sel_onehot(tk, kb * (tk // SEL_BLOCK))], axis=-1)
        s = _dot_nt(q2_ref[...], k)
        if masked:
            r = _iota((R_B * tq, tk), 0)
            qp = i * tq + jnp.where(r >= tq, r - tq, r)
            kp = kb * tk + _iota((R_B * tq, tk), 1)
            ok = kp <= qp
            if not use_sel:
                ok = ok & (kp > qp - WINDOW)
            s = jnp.where(ok, s, MASK_NEG)
        return s

    def values(kb):
        return v_ref[0, pl.ds(pl.multiple_of(kb * tk, tk), tk), :].astype(BF16)

    per = tq // tk
    lo = 0 if use_sel else jnp.maximum(i * per - WINDOW // tk, 0)
    o = _two_pass_attention(mx_ref, acc_ref, lo, i * per, [i * per + u for u in range(per)], scores, values,
                            mask_body=not use_sel, bound=bound)
    for r in range(R_B):
        o_ref[0, :, r * LANES:(r + 1) * LANES] = o[r * tq:(r + 1) * tq]


def _nsa_attn_prompt(qb, kv, selm, tq, tk):
    n, t, _ = qb.shape
    use_sel = selm is not None
    kd = 2 * LANES if use_sel else LANES
    kern = functools.partial(_nsa_flash_kernel, tq=tq, tk=tk, use_sel=use_sel)
    in_specs = [pl.BlockSpec((1, tq, R_B * LANES), lambda b, g, i: (b, i, g))]
    args = [qb]
    if use_sel:
        in_specs.append(pl.BlockSpec((1, 1, tq, LANES), lambda b, g, i: (b, g, i, 0)))
        args.append(selm)
    in_specs += [pl.BlockSpec((1, t, LANES), lambda b, g, i: (b, 0, g)),
                 pl.BlockSpec((1, t, LANES), lambda b, g, i: (b, 0, G_B + g))]
    args += [kv, kv]
    return pl.pallas_call(
        kern,
        grid=(n, G_B, t // tq),
        in_specs=in_specs,
        out_specs=pl.BlockSpec((1, tq, R_B * LANES), lambda b, g, i: (b, i, g)),
        out_shape=jax.ShapeDtypeStruct((n, t, H_B * DH_B), F32),
        scratch_shapes=[pltpu.VMEM((R_B * tq, kd), BF16), pltpu.VMEM((R_B * tq, LANES), F32),
                        pltpu.VMEM((R_B * tq, 2 * LANES), F32), pltpu.VMEM((8, LANES), F32)],
        compiler_params=_cparams(("parallel", "parallel", "arbitrary")),
        name="nsa_sel_prompt" if use_sel else "nsa_win_prompt",
    )(*args)


def _compress_core(load, wc_ref, pe_ref, w2_ref, wi, nchunk):
    rows = G_B * nchunk
    acc = jnp.zeros((rows, 2 * LANES), F32)
    pew = jnp.zeros((16, 2 * LANES), F32)
    for u in range(CMP_STRIDE // 2):
        lhs = jnp.concatenate([load(2 * u), load(2 * u + 1)], axis=-1).astype(BF16)
        w = wc_ref[wi, u]
        acc = acc + _dot(lhs, w)
        pew = pew + _dot(pe_ref[wi, u], w)
    first = acc[:, :LANES]
    second = pltpu.roll(acc[:, LANES:], rows - 1, 0)
    hid = first + second + pew[0:1, :LANES] + pew[8:9, LANES:]
    hid = hid * jax.nn.sigmoid(hid)
    return _dot(hid.astype(BF16), w2_ref[wi])


def _compress_prompt_kernel(*refs, nchunk):
    x_refs = refs[:G_B]
    wc_ref, pe_ref, w2_ref, o_ref = refs[G_B:]

    def load(tok):
        return jnp.concatenate([xr[0, pl.ds(tok, nchunk, stride=CMP_STRIDE), :] for xr in x_refs], axis=0)

    out = _compress_core(load, wc_ref, pe_ref, w2_ref, 0, nchunk)
    for g in range(G_B):
        o_ref[0, 0, g] = out[g * nchunk:(g + 1) * nchunk]


def _compress_sample_kernel(pt_ref, *refs, n_pages, spb):
    pg_refs = refs[:spb * n_pages]
    wc_ref, pe_ref, w2_ref, o_ref = refs[spb * n_pages:]
    cps = PAGE_SIZE // CMP_STRIDE
    nchunk = n_pages * cps
    for kv in range(2):
        def load(tok):
            return jnp.concatenate(
                [pg[0, pl.ds(tok * SLOTS + kv * G_B + g, cps, stride=CMP_STRIDE * SLOTS), :]
                 for smp in range(spb) for g in range(G_B) for pg in pg_refs[smp * n_pages:(smp + 1) * n_pages]],
                axis=0)

        out = _compress_core(load, wc_ref, pe_ref, w2_ref, kv, spb * nchunk)
        for smp in range(spb):
            for g in range(G_B):
                seg = smp * G_B + g
                o_ref[smp, kv, g] = out[seg * nchunk:(seg + 1) * nchunk]


def _compress_weights(cmp_pe, cmp_w1, cmp_w2):
    w1 = cmp_w1.reshape(2, CMP_LEN, DH_B, DH_B)
    wab = jnp.concatenate([w1[:, :CMP_STRIDE], w1[:, CMP_STRIDE:]], axis=-1)
    wc = wab.reshape(2, CMP_STRIDE // 2, 2 * DH_B, 2 * DH_B).astype(BF16)
    pa = cmp_pe[:, :CMP_STRIDE].reshape(2, CMP_STRIDE // 2, 1, 2 * DH_B)
    pb = cmp_pe[:, CMP_STRIDE:].reshape(2, CMP_STRIDE // 2, 1, 2 * DH_B)
    z = jnp.zeros((2, CMP_STRIDE // 2, 7, 2 * DH_B), F32)
    pe = jnp.concatenate([pa, z, pb, z], axis=2).astype(BF16)
    return wc, pe, cmp_w2.astype(BF16)


def _compress_prompt(cmp_kv, wc, pe, w2):
    n, t, _ = cmp_kv.shape
    nchunk = t // CMP_STRIDE
    kern = functools.partial(_compress_prompt_kernel, nchunk=nchunk)
    return pl.pallas_call(
        kern,
        grid=(n, 2),
        in_specs=[pl.BlockSpec((1, t, DH_B), functools.partial(lambda b, kv, g: (b, 0, kv * G_B + g), g=g))
                  for g in range(G_B)] + [
                  pl.BlockSpec((1,) + wc.shape[1:], lambda b, kv: (kv, 0, 0, 0)),
                  pl.BlockSpec((1,) + pe.shape[1:], lambda b, kv: (kv, 0, 0, 0)),
                  pl.BlockSpec((1, DH_B, DH_B), lambda b, kv: (kv, 0, 0))],
        out_specs=pl.BlockSpec((1, 1, G_B, nchunk, DH_B), lambda b, kv: (b, kv, 0, 0, 0)),
        out_shape=jax.ShapeDtypeStruct((n, 2, G_B, nchunk, DH_B), F32),
        compiler_params=_cparams(("parallel", "parallel")),
        name="compress_prompt",
    )(*([cmp_kv] * G_B), wc, pe, w2)


def _compress_sample(cache, page_table, wc, pe, w2):
    n, n_pages = page_table.shape
    nchunk = n_pages * (PAGE_SIZE // CMP_STRIDE)
    spb = math.gcd(n, 2)
    kern = functools.partial(_compress_sample_kernel, n_pages=n_pages, spb=spb)
    page_specs = [pl.BlockSpec((1, PAGE_SIZE * SLOTS, LANES),
                               functools.partial(lambda b, pt, smp, j: (pt[b * spb + smp, j], 0, 0), smp=smp, j=j))
                  for smp in range(spb) for j in range(n_pages)]
    grid_spec = pltpu.PrefetchScalarGridSpec(
        num_scalar_prefetch=1,
        grid=(n // spb,),
        in_specs=page_specs + [
            pl.BlockSpec(wc.shape, lambda b, pt: (0, 0, 0, 0)),
            pl.BlockSpec(pe.shape, lambda b, pt: (0, 0, 0, 0)),
            pl.BlockSpec(w2.shape, lambda b, pt: (0, 0, 0))],
        out_specs=pl.BlockSpec((spb, 2, G_B, nchunk, DH_B), lambda b, pt: (b, 0, 0, 0, 0)),
    )
    return pl.pallas_call(
        kern,
        grid_spec=grid_spec,
        out_shape=jax.ShapeDtypeStruct((n, 2, G_B, nchunk, DH_B), F32),
        compiler_params=_cparams(("parallel",)),
        name="compress_sample",
    )(page_table, *([cache] * (spb * n_pages)), wc, pe, w2)


def _cmp_topk_kernel(q_ref, kc_ref, vc_ref, ocmp_ref, selm_ref, p_scr, v_scr, *,
                     nb, rq, nck, n_cmp, n_sel, nsp, pos_base):
    tq = nb * rq
    j = pl.program_id(2)
    scale = DH_B ** -0.5
    kidx = _iota((1, nck), 1)
    qpos_c = pos_base + j * rq + _iota((rq, 1), 0)
    valid = (kidx < n_cmp) & (CMP_STRIDE * kidx + (CMP_LEN - 1) <= qpos_c)
    for b in range(nb):
        kcc = kc_ref[b, 0, 0].astype(BF16)
        vcc = vc_ref[b, 0, 0].astype(BF16)
        q = q_ref[b] * scale
        q2 = jnp.concatenate([q[:, r * LANES:(r + 1) * LANES] for r in range(R_B)], axis=0).astype(BF16)
        valid2 = jnp.concatenate([valid] * R_B, axis=0)
        s = jnp.where(valid2, _dot_nt(q2, kcc), MASK_NEG)
        m = jnp.max(s, axis=-1, keepdims=True)
        p = jnp.where(valid2, jnp.exp(s - m), 0.0)
        p = p / jnp.maximum(jnp.sum(p, axis=-1, keepdims=True), 1e-30)
        o = _dot(p.astype(BF16), vcc)
        psum = jnp.zeros((rq, nck), F32)
        for r in range(R_B):
            ocmp_ref[b, :, r * LANES:(r + 1) * LANES] = o[r * rq:(r + 1) * rq]
            psum = psum + p[r * rq:(r + 1) * rq]
        p_scr[b * rq:(b + 1) * rq, :] = psum
    psum = p_scr[...]
    p_hi = psum.astype(BF16)
    p_lo = (psum - p_hi.astype(F32)).astype(BF16)
    srow = _iota((LANES, nck), 0)
    kcol = _iota((LANES, nck), 1)
    cov = (CMP_STRIDE * kcol < SEL_BLOCK * srow + SEL_BLOCK) & (CMP_STRIDE * kcol + CMP_LEN > SEL_BLOCK * srow)
    cov = cov & (srow < n_sel) & (kcol < n_cmp)
    cov_t = jnp.where(cov, 1.0, 0.0).astype(BF16)
    imp_t = _dot_nt(cov_t, p_hi) + _dot_nt(cov_t, p_lo)
    qpos_l = pos_base + j * rq + _iota((1, tq), 1) % rq
    sb = _iota((nsp, 1), 0)
    valid_b = (sb < n_sel) & (sb * SEL_BLOCK <= qpos_l)
    forced = (sb == 0) | (sb == qpos_l // SEL_BLOCK)
    vals = jnp.where(forced, jnp.inf, jnp.where(valid_b, imp_t[:nsp], -jnp.inf))
    v_scr[...] = vals

    def body(jj, cnt):
        vj = v_scr[pl.ds(jj, 1), :]
        tie = jnp.where(sb > jj, 1.0, 0.0)
        return cnt + jnp.where(vj > vals, 1.0, jnp.where(vj == vals, tie, 0.0))

    last_pos = pos_base + (j + 1) * rq - 1
    n_rank = jnp.minimum(last_pos // SEL_BLOCK + 1, nsp)
    cnt = lax.fori_loop(0, n_rank, body, jnp.zeros((nsp, tq), F32))
    keep = valid_b & (cnt < float(min(TOP_N, n_sel)))
    selm_t = jnp.where(keep, 0.0, -SEL_NEG)
    if nsp < LANES:
        selm_t = jnp.concatenate([selm_t, jnp.zeros((LANES - nsp, tq), F32)], axis=0)
    selm = selm_t.T.astype(BF16)
    for b in range(nb):
        selm_ref[b, 0] = selm[b * rq:(b + 1) * rq]


def _cmp_topk(qb, ccmp, *, nb, rq, n_cmp, n_sel, pos_base):
    n, r, _ = qb.shape
    nck = ccmp.shape[3]
    nsp = -(-n_sel // 8) * 8
    tq = nb * rq
    kern = functools.partial(_cmp_topk_kernel, nb=nb, rq=rq, nck=nck, n_cmp=n_cmp, n_sel=n_sel, nsp=nsp,
                             pos_base=pos_base)
    return pl.pallas_call(
        kern,
        grid=(n // nb, G_B, r // rq),
        in_specs=[pl.BlockSpec((nb, rq, R_B * LANES), lambda a, g, j: (a, j, g)),
                  pl.BlockSpec((nb, 1, 1, nck, DH_B), lambda a, g, j: (a, 0, g, 0, 0)),
                  pl.BlockSpec((nb, 1, 1, nck, DH_B), lambda a, g, j: (a, 1, g, 0, 0))],
        out_specs=[pl.BlockSpec((nb, rq, R_B * LANES), lambda a, g, j: (a, j, g)),
                   pl.BlockSpec((nb, 1, rq, LANES), lambda a, g, j: (a, g, j, 0))],
        out_shape=[jax.ShapeDtypeStruct((n, r, H_B * DH_B), F32),
                   jax.ShapeDtypeStruct((n, G_B, r, LANES), BF16)],
        scratch_shapes=[pltpu.VMEM((tq, nck), F32), pltpu.VMEM((nsp, tq), F32)],
        compiler_params=_cparams(("parallel", "parallel", "parallel")),
        name="cmp_topk",
    )(qb, ccmp, ccmp)


def _attend_scores(pieces):
    m = None
    for s, _ in pieces:
        ms = jnp.max(s, axis=-1, keepdims=True)
        m = ms if m is None else jnp.maximum(m, ms)
    l = jnp.zeros_like(m)
    acc = jnp.zeros((m.shape[0], LANES), F32)
    for s, v in pieces:
        p = jnp.exp(s - m)
        l = l + jnp.sum(p, axis=-1, keepdims=True)
        acc = acc + _dot(p.astype(BF16), v)
    return acc / l


def _attend_pieces(q2, pieces):
    scored = []
    for k, v, mask in pieces:
        s = _dot_nt(q2, k)
        scored.append((s if mask is None else jnp.where(mask, s, MASK_NEG), v))
    return _attend_scores(scored)


def _pad_rows(x, rows):
    return jnp.concatenate([x, jnp.zeros((rows - x.shape[0], x.shape[1]), x.dtype)], axis=0)


def _new_token_mask(m_rows, s):
    r = _iota((m_rows, LANES), 0) % s
    return _iota((m_rows, LANES), 1) <= r


def _diff_sample_kernel(pt_ref, *refs, n_pages, s, lam_init):
    kp_refs = refs[:n_pages]
    vp_refs = refs[n_pages:2 * n_pages]
    q_ref, kn_ref, vn_ref, lamw_ref, sub_ref, o_ref = refs[2 * n_pages:]
    lane = _iota((1, LANES), 1)
    lam = _diff_lambda(lamw_ref[...], lam_init)
    new_mask = _new_token_mask(2 * s, s)
    for h in range(H_A):
        cols = slice(h * LANES, (h + 1) * LANES)
        q = q_ref[0, :, cols] * (DK_A ** -0.5)
        q2 = jnp.concatenate([jnp.where(lane < DK_A, q, 0.0), jnp.where(lane >= DK_A, q, 0.0)], axis=0).astype(BF16)
        kt_past = jnp.concatenate([r[0, cols, :].astype(BF16) for r in kp_refs], axis=1)
        s_past = _dot(q2, kt_past)
        vpast = jnp.concatenate([_slot_rows(r, h, PAGE_SIZE).astype(BF16) for r in vp_refs], axis=0)
        knew = _pad_rows(kn_ref[0, :, cols], LANES).astype(BF16)
        vnew = _pad_rows(vn_ref[0, :, cols], LANES).astype(BF16)
        s_new = jnp.where(new_mask, _dot_nt(q2, knew), MASK_NEG)
        o2 = _attend_scores([(s_past, vpast), (s_new, vnew)])
        o_ref[0, :, cols] = _diff_finalize(o2, s, lam, sub_ref[...], lam_init)


def _page_specs(n_pages, rows, width):
    return [pl.BlockSpec((1, rows, width), functools.partial(lambda b, pt, j: (pt[b, j], 0, 0), j=j))
            for j in range(n_pages)]


def _slot_rows(ref, slot, n_tok):
    return ref[0, pl.ds(slot, n_tok, stride=SLOTS), :]


def _diff_attn_sample(qa, ka, va, cache_k, cache_v, page_table, lam_w, sub_w, lam_init):
    n, s, _ = qa.shape
    n_pages = page_table.shape[1]
    kern = functools.partial(_diff_sample_kernel, n_pages=n_pages, s=s, lam_init=lam_init)
    row_spec = pl.BlockSpec((1, s, 512), lambda b, pt: (b, 0, 0))
    grid_spec = pltpu.PrefetchScalarGridSpec(
        num_scalar_prefetch=1,
        grid=(n,),
        in_specs=_page_specs(n_pages, 512, PAGE_SIZE) + _page_specs(n_pages, PAGE_SIZE * SLOTS, LANES)
        + [row_spec, row_spec, row_spec,
           pl.BlockSpec(lam_w.shape, lambda b, pt: (0, 0)), pl.BlockSpec(sub_w.shape, lambda b, pt: (0, 0))],
        out_specs=row_spec,
    )
    return pl.pallas_call(
        kern,
        grid_spec=grid_spec,
        out_shape=jax.ShapeDtypeStruct((n, s, 512), F32),
        compiler_params=_cparams(("parallel",)),
        name="diff_attn_sample",
    )(page_table, *([cache_k] * n_pages), *([cache_v] * n_pages), qa, ka, va, lam_w, sub_w)


def _sel_sample_kernel(pt_ref, *refs, n_pages, s):
    pg_refs = refs[:n_pages]
    q_ref, selm_ref, new_ref, o_ref = refs[n_pages:]
    past = n_pages * PAGE_SIZE
    new_mask = _new_token_mask(R_B * s, s)
    oh_past = _sel_onehot(past, 0)
    new_blk = past // SEL_BLOCK
    oh_new = jnp.where((_iota((LANES, LANES), 1) == new_blk) & (_iota((LANES, LANES), 0) < s), 1.0, 0.0)
    oh_new = oh_new.astype(BF16)
    scale = DH_B ** -0.5
    for g in range(G_B):
        kcols = slice(g * LANES, (g + 1) * LANES)
        vcols = slice((G_B + g) * LANES, (G_B + g + 1) * LANES)
        selm = selm_ref[0, g]
        q2 = jnp.concatenate(
            [jnp.concatenate([(q_ref[0, :, (g * R_B + r) * LANES:(g * R_B + r + 1) * LANES] * scale).astype(BF16),
                              selm], axis=-1) for r in range(R_B)], axis=0)
        kpast = jnp.concatenate([_slot_rows(r, g, PAGE_SIZE).astype(BF16) for r in pg_refs], axis=0)
        vpast = jnp.concatenate([_slot_rows(r, G_B + g, PAGE_SIZE).astype(BF16) for r in pg_refs], axis=0)
        kpast = jnp.concatenate([kpast, oh_past], axis=-1)
        knew = jnp.concatenate([_pad_rows(new_ref[0, :, kcols], LANES).astype(BF16), oh_new], axis=-1)
        vnew = _pad_rows(new_ref[0, :, vcols], LANES).astype(BF16)
        o2 = _attend_pieces(q2, [(kpast, vpast, None), (knew, vnew, new_mask)])
        for r in range(R_B):
            o_ref[0, :, (g * R_B + r) * LANES:(g * R_B + r + 1) * LANES] = o2[r * s:(r + 1) * s]


def _sel_attn_sample(qb, selm, sel_new, cache, page_table):
    n, s, _ = qb.shape
    n_pages = page_table.shape[1]
    assert s <= SEL_BLOCK and (n_pages * PAGE_SIZE) % SEL_BLOCK == 0
    kern = functools.partial(_sel_sample_kernel, n_pages=n_pages, s=s)
    row_spec = pl.BlockSpec((1, s, 512), lambda b, pt: (b, 0, 0))
    grid_spec = pltpu.PrefetchScalarGridSpec(
        num_scalar_prefetch=1,
        grid=(n,),
        in_specs=_page_specs(n_pages, PAGE_SIZE * SLOTS, LANES)
        + [row_spec, pl.BlockSpec((1, G_B, s, LANES), lambda b, pt: (b, 0, 0, 0)), row_spec],
        out_specs=row_spec,
    )
    return pl.pallas_call(
        kern,
        grid_spec=grid_spec,
        out_shape=jax.ShapeDtypeStruct((n, s, 512), F32),
        compiler_params=_cparams(("parallel",)),
        name="nsa_sel_sample",
    )(page_table, *([cache] * n_pages), qb, selm, sel_new)


def _win_sample_kernel(q_ref, buf_ref, new_ref, o_ref, wout_ref, *, s, wb, spb):
    new_mask = _new_token_mask(R_B * s, s)
    r = _iota((R_B * s, wb), 0) % s
    buf_mask = _iota((R_B * s, wb), 1) > r + (wb - WINDOW)
    scale = DH_B ** -0.5
    keep = (wb - s) * SLOTS
    for b in range(spb):
        for g in range(G_B):
            kcols = slice(g * LANES, (g + 1) * LANES)
            vcols = slice((G_B + g) * LANES, (G_B + g + 1) * LANES)
            q2 = jnp.concatenate(
                [(q_ref[b, :, (g * R_B + r_) * LANES:(g * R_B + r_ + 1) * LANES] * scale).astype(BF16)
                 for r_ in range(R_B)], axis=0)
            kbuf = buf_ref[b, pl.ds(g, wb, stride=SLOTS), :].astype(BF16)
            vbuf = buf_ref[b, pl.ds(G_B + g, wb, stride=SLOTS), :].astype(BF16)
            knew = _pad_rows(new_ref[b, :, kcols], LANES).astype(BF16)
            vnew = _pad_rows(new_ref[b, :, vcols], LANES).astype(BF16)
            o2 = _attend_pieces(q2, [(kbuf, vbuf, buf_mask), (knew, vnew, new_mask)])
            for r_ in range(R_B):
                o_ref[b, :, (g * R_B + r_) * LANES:(g * R_B + r_ + 1) * LANES] = o2[r_ * s:(r_ + 1) * s]
        wout_ref[b, 0:keep] = buf_ref[b, s * SLOTS:wb * SLOTS]
        for slot in range(SLOTS):
            wout_ref[b, pl.ds(keep + slot, s, stride=SLOTS), :] = new_ref[b, :, slot * LANES:(slot + 1) * LANES]


def _win_attn_sample(qb, win_buf, win_new, past):
    n, s, _ = qb.shape
    wb = win_buf.shape[1] // SLOTS
    assert past >= wb and wb % 8 == 0 and s % 8 == 0
    spb = math.gcd(n, 4)
    kern = functools.partial(_win_sample_kernel, s=s, wb=wb, spb=spb)
    row_spec = pl.BlockSpec((spb, s, 512), lambda b: (b, 0, 0))
    buf_spec = pl.BlockSpec((spb, wb * SLOTS, LANES), lambda b: (b, 0, 0))
    return pl.pallas_call(
        kern,
        grid=(n // spb,),
        in_specs=[row_spec, buf_spec, row_spec],
        out_specs=[row_spec, buf_spec],
        out_shape=[jax.ShapeDtypeStruct((n, s, 512), F32), jax.ShapeDtypeStruct((n, wb * SLOTS, LANES), F32)],
        compiler_params=_cparams(("parallel",)),
        name="nsa_win_sample",
    )(qb, win_buf, win_new)


def _mlp_value(y, sh, sc, g, nw, w1_ref, w2_ref, fw, final):
    h = _norm_mod(y, nw, sh, sc).reshape(-1, D_MODEL).astype(BF16)
    a = jnp.maximum(_dot(h, w1_ref[...]), 0.0)
    out = _dot((a * a).astype(BF16), w2_ref[...])
    y2 = y + g * out.reshape(y.shape)
    return _rms(y2, fw) if final else y2


def _mlp_specs(til, mlp):
    nw, w1, w2, fw, _ = mlp
    specs = [til.mod_spec(3), til.mod_spec(4), til.mod_spec(5), _full_spec((1, D_MODEL)),
             _const_spec(w1.shape), _const_spec(w2.shape), _full_spec((1, D_MODEL))]
    return specs, [nw, w1, w2, fw]


def _even_out_kernel(y_ref, g1_ref, oa_ref, oc_ref, os_ref, ow_ref, gt_ref, w_ref,
                     sh_ref, sc_ref, g2_ref, nw_ref, w1_ref, w2_ref, fw_ref, o_ref, *, final):
    gt = gt_ref[...]
    parts = [oa_ref[...].astype(BF16)]
    for hb in range(H_B):
        sl = slice(hb * LANES, (hb + 1) * LANES)
        ob = (gt[:, 3 * hb:3 * hb + 1] * oc_ref[:, sl] + gt[:, 3 * hb + 1:3 * hb + 2] * os_ref[:, sl]
              + gt[:, 3 * hb + 2:3 * hb + 3] * ow_ref[:, sl])
        parts.append(ob.astype(BF16))
    out = _dot(jnp.concatenate(parts, axis=-1), w_ref[...])
    y = y_ref[...]
    y1 = y + g1_ref[0] * out.reshape(y.shape)
    o_ref[...] = _mlp_value(y1, sh_ref[0], sc_ref[0], g2_ref[0], nw_ref[...], w1_ref, w2_ref, fw_ref[...], final)


def _even_out(til, y, mods, oa, oc, os_, ow, gt, w_out, mlp):
    mlp_specs, mlp_args = _mlp_specs(til, mlp)
    return pl.pallas_call(
        functools.partial(_even_out_kernel, final=mlp[4]),
        grid=til.grid,
        in_specs=[til.x_spec(D_MODEL), til.mod_spec(2), til.flat_spec(512), til.flat_spec(512),
                  til.flat_spec(512), til.flat_spec(512), til.flat_spec(LANES), _const_spec(w_out.shape)] + mlp_specs,
        out_specs=til.x_spec(D_MODEL),
        out_shape=jax.ShapeDtypeStruct(y.shape, F32),
        compiler_params=_cparams(("parallel", "parallel")),
        name="even_out_mlp",
    )(y, mods, oa, oc, os_, ow, gt, w_out, mods, mods, mods, *mlp_args)


def _gla_inproj_kernel(x_ref, sh_ref, sc_ref, nw_ref, w_ref, wgl_ref, wgate_ref, bgate_ref,
                       q_ref, k_ref, v_ref, r_ref, la_ref):
    h = _norm_mod(x_ref[...], nw_ref[...], sh_ref[0], sc_ref[0]).reshape(-1, D_MODEL).astype(BF16)
    nk = H_C * DK_C
    nv = H_C * DV_C
    q_ref[...] = _dot(h, w_ref[:, 0:nk]) * (DK_C ** -0.5)
    k_ref[...] = _dot(h, w_ref[:, nk:2 * nk])
    v_ref[...] = _dot(h, w_ref[:, 2 * nk:2 * nk + nv])
    r_ref[...] = _dot(h, w_ref[:, 2 * nk + nv:2 * nk + 2 * nv])
    gl = _dot(h, wgl_ref[...])
    x = _dot(gl.astype(BF16), wgate_ref[...]) + bgate_ref[...]
    log_sig = jnp.minimum(x, 0.0) - jnp.log1p(jnp.exp(-jnp.abs(x)))
    la_ref[...] = log_sig / GATE_TAU


def _gla_inproj(til, x, mods, nw, w_main, w_gl, w_gate, b_gate):
    rows_total = til.nb * til.r
    widths = (H_C * DK_C, H_C * DK_C, H_C * DV_C, H_C * DV_C, H_C * DK_C)
    return pl.pallas_call(
        _gla_inproj_kernel,
        grid=til.grid,
        in_specs=[til.x_spec(D_MODEL), til.mod_spec(0), til.mod_spec(1), _full_spec((1, D_MODEL)),
                  _full_spec(w_main.shape), _full_spec(w_gl.shape), _full_spec(w_gate.shape),
                  _full_spec(b_gate.shape)],
        out_specs=[til.flat_spec(c) for c in widths],
        out_shape=[jax.ShapeDtypeStruct((rows_total, c), F32) for c in widths],
        compiler_params=_cparams(("parallel", "parallel")),
        name="gla_inproj",
    )(x, mods, mods, nw, w_main, w_gl, w_gate, b_gate)


def _cumsum_rows(g):
    c = g.shape[0]
    row = _iota((c, 1), 0)
    b = g
    shift = 1
    while shift < c:
        b = b + jnp.where(row >= shift, pltpu.roll(b, shift, 0), 0.0)
        shift *= 2
    return b


def _gla_prep(q, k, g, c, sub, n_real):
    b = _cumsum_rows(g)
    qe = (q * jnp.exp(b)).astype(BF16)
    lane_c = _iota((sub, c), 1)
    row_s = _iota((sub, c), 0)
    att_rows = []
    for blk in range(c // sub):
        lo = blk * sub
        qi, ki, bi = q[lo:lo + sub], k[lo:lo + sub], b[lo:lo + sub]
        diag = jnp.zeros((sub, c), F32)
        for jj in range(min(sub, max(n_real - lo, 0))):
            e = jnp.exp(jnp.minimum(bi - bi[jj:jj + 1], 0.0))
            col = jnp.sum(qi * ki[jj:jj + 1] * e, axis=-1, keepdims=True)
            diag = jnp.where(lane_c == lo + jj, col, diag)
        att = jnp.where(lane_c - lo <= row_s, diag, 0.0)
        if blk > 0:
            bs = b[lo - 1:lo]
            q_in = qi * jnp.exp(bi - bs)
            k_out = k * jnp.exp(jnp.minimum(bs - b, 0.0))
            att = jnp.where(lane_c < lo, _dot_nt(q_in.astype(BF16), k_out.astype(BF16)), att)
        att_rows.append(att)
    att = att_rows[0] if len(att_rows) == 1 else jnp.concatenate(att_rows, axis=0)
    bl = b[c - 1:c]
    kd = (k * jnp.exp(bl - b)).astype(BF16)
    eye = _iota((DK_C, DK_C), 0) == _iota((DK_C, DK_C), 1)
    decay = jnp.sum(jnp.where(eye, jnp.exp(bl), 0.0), axis=-1, keepdims=True)
    return qe, att, kd, decay


def _gla_prep_bounded(q, k, g, c, sub, n_real):
    del sub, n_real
    b = _cumsum_rows(g)
    r = b[0:1]
    bl = b[c - 1:c]
    q_in = q * jnp.exp(b - r)
    k_out = k * jnp.exp(r - b)
    att = jnp.where(_iota((c, c), 0) >= _iota((c, c), 1), _dot_nt(q_in.astype(BF16), k_out.astype(BF16)), 0.0)
    qe = (q_in * jnp.exp(r)).astype(BF16)
    kd = (k_out * jnp.exp(bl - r)).astype(BF16)
    eye = _iota((DK_C, DK_C), 0) == _iota((DK_C, DK_C), 1)
    decay = jnp.sum(jnp.where(eye, jnp.exp(bl), 0.0), axis=-1, keepdims=True)
    return qe, att, kd, decay


def _gla_apply(state, prep, v):
    qe, att, kd, decay = prep
    o = _dot(qe, state.astype(BF16)) + _dot(att, v)
    return o, decay * state + _dot_tn(kd, v.astype(BF16))


def _gla_rec_kernel(*refs, tt, c, sub, hp, has_s0):
    if has_s0:
        q_ref, k_ref, v_ref, g_ref, s0_ref, o_ref, sfin_ref, s_ref = refs
    else:
        q_ref, k_ref, v_ref, g_ref, o_ref, sfin_ref, s_ref = refs
    t = pl.program_id(2)

    @pl.when(t == 0)
    def _():
        s_ref[...] = s0_ref[0] if has_s0 else jnp.zeros(s_ref.shape, F32)

    def kcols(hh):
        return slice(hh * DK_C, (hh + 1) * DK_C)

    def vcols(hh):
        return slice(hh * DV_C, (hh + 1) * DV_C)

    def run(prep_fn):
        if tt < c:
            pad = lambda x: _pad_rows(x, c)
            for hh in range(hp):
                prep = prep_fn(pad(q_ref[0, :, kcols(hh)]), pad(k_ref[0, :, kcols(hh)]),
                               pad(g_ref[0, :, kcols(hh)]), c, sub, tt)
                o, s_ref[hh] = _gla_apply(s_ref[hh], prep, pad(v_ref[0, :, vcols(hh)]))
                o_ref[0, :, vcols(hh)] = o[:tt]
            return
        per_trip = 2 if (tt // c) % 2 == 0 else 1

        def body(ci, carry):
            for hh in range(hp):
                rows = [pl.ds(pl.multiple_of((ci * per_trip + u) * c, c), c) for u in range(per_trip)]
                preps = [prep_fn(q_ref[0, r, kcols(hh)], k_ref[0, r, kcols(hh)], g_ref[0, r, kcols(hh)],
                                 c, sub, c) for r in rows]
                state = s_ref[hh]
                for r, prep in zip(rows, preps):
                    o_ref[0, r, vcols(hh)], state = _gla_apply(state, prep, v_ref[0, r, vcols(hh)])
                s_ref[hh] = state
            return carry
        lax.fori_loop(0, tt // c // per_trip, body, 0)

    g_all = g_ref[0]
    if tt > c:
        chunk_decay = -jnp.sum(g_all.reshape(tt // c, c, g_all.shape[-1]), axis=1)
    else:
        chunk_decay = -jnp.sum(g_all, axis=0, keepdims=True)
    bounded = jnp.max(chunk_decay) <= GLA_SAFE_DECAY

    @pl.when(bounded)
    def _():
        run(_gla_prep_bounded)

    @pl.when(jnp.logical_not(bounded))
    def _():
        run(_gla_prep)

    @pl.when(t == pl.num_programs(2) - 1)
    def _():
        sfin_ref[0] = s_ref[...]


def _gla_recurrence(q, k, v, g, s0, tt, c, sub, hp):
    n, t, _ = q.shape
    kern = functools.partial(_gla_rec_kernel, tt=tt, c=c, sub=sub, hp=hp, has_s0=s0 is not None)
    kspec = pl.BlockSpec((1, tt, hp * DK_C), lambda b, h, i: (b, i, h))
    vspec = pl.BlockSpec((1, tt, hp * DV_C), lambda b, h, i: (b, i, h))
    sspec = pl.BlockSpec((1, hp, DK_C, DV_C), lambda b, h, i: (b, h, 0, 0))
    in_specs = [kspec, kspec, vspec, kspec]
    args = [q, k, v, g]
    if s0 is not None:
        in_specs.append(sspec)
        args.append(s0)
    return pl.pallas_call(
        kern,
        grid=(n, H_C // hp, t // tt),
        in_specs=in_specs,
        out_specs=[vspec, sspec],
        out_shape=[jax.ShapeDtypeStruct((n, t, H_C * DV_C), F32),
                   jax.ShapeDtypeStruct((n, H_C, DK_C, DV_C), F32)],
        scratch_shapes=[pltpu.VMEM((hp, DK_C, DV_C), F32)],
        compiler_params=_cparams(("parallel", "parallel", "arbitrary")),
        name="gla_recurrence",
    )(*args)


def _gla_out_kernel(y_ref, g1_ref, o_ref_in, r_ref, gnw_ref, w_ref,
                    sh_ref, sc_ref, g2_ref, nw_ref, w1_ref, w2_ref, fw_ref, out_ref, *, final):
    parts = []
    for h in range(H_C):
        sl = slice(h * DV_C, (h + 1) * DV_C)
        r = r_ref[:, sl]
        parts.append((_rms(o_ref_in[:, sl], gnw_ref[...]) * (r * jax.nn.sigmoid(r))).astype(BF16))
    out = _dot(jnp.concatenate(parts, axis=-1), w_ref[...])
    y = y_ref[...]
    y1 = y + g1_ref[0] * out.reshape(y.shape)
    out_ref[...] = _mlp_value(y1, sh_ref[0], sc_ref[0], g2_ref[0], nw_ref[...], w1_ref, w2_ref, fw_ref[...], final)


def _gla_out(til, y, mods, o, r, gnw, w_out, mlp):
    mlp_specs, mlp_args = _mlp_specs(til, mlp)
    return pl.pallas_call(
        functools.partial(_gla_out_kernel, final=mlp[4]),
        grid=til.grid,
        in_specs=[til.x_spec(D_MODEL), til.mod_spec(2), til.flat_spec(H_C * DV_C), til.flat_spec(H_C * DV_C),
                  _full_spec(gnw.shape), _const_spec(w_out.shape)] + mlp_specs,
        out_specs=til.x_spec(D_MODEL),
        out_shape=jax.ShapeDtypeStruct(y.shape, F32),
        compiler_params=_cparams(("parallel", "parallel")),
        name="gla_out_mlp",
    )(y, mods, o, r, gnw, w_out, mods, mods, mods, *mlp_args)


def _rope_tables(pos, d):
    inv = ROPE_THETA ** (-jnp.arange(0, d, 2, dtype=F32) / d)
    ang = pos.astype(F32)[:, None] * inv[None, :]
    cos, sin = jnp.cos(ang), jnp.sin(ang)
    rep = LANES // d
    c = jnp.tile(jnp.concatenate([cos, cos], axis=-1), (1, rep))
    s = jnp.tile(jnp.concatenate([-sin, sin], axis=-1), (1, rep))
    return c, s


def _rope_tables_t(pos, d):
    inv = ROPE_THETA ** (-jnp.arange(0, d, 2, dtype=F32) / d)
    ang = inv[:, None] * pos.astype(F32)[None, :]
    return jnp.cos(ang), jnp.sin(ang)


def _mods_for(mod_l, lo, hi):
    nb = hi - lo
    return mod_l[lo:hi].reshape(nb, 6, D_MODEL).transpose(1, 0, 2).reshape(6, nb, 1, D_MODEL)


def _prompt_tile_rows(t):
    return math.gcd(t, 256)


def _even_layer(yp, ys, mods_p, mods_s, caches, page_table, wts, lam_init, til_p, til_s, mlp):
    (c_dk, c_dv, c_cmp, c_sel, win_buf) = caches
    n, t, _ = yp.shape
    ns, s, _ = ys.shape
    n_pages = page_table.shape[1]
    past = n_pages * PAGE_SIZE
    w_in = wts["w_in"]
    n_main = w_in.shape[1] - 3 * H_B
    w_main = w_in[:, :n_main].astype(BF16)
    w_gate = jnp.pad(w_in[:, n_main:], ((0, 0), (0, LANES - 3 * H_B))).astype(BF16)
    nw1 = wts["norm1"].reshape(1, D_MODEL)
    lam_w = wts["lam_w"]
    sub_w = wts["sub_w"].reshape(1, DV_A)
    wc, pe, w2c = _compress_weights(wts["cmp_pe"], wts["cmp_w1"], wts["cmp_w2"])
    w_out = wts["w_out"].astype(BF16)

    tabs_p = _rope_tables(jnp.arange(t), DK_A) + _rope_tables(jnp.arange(t), DH_B)
    kt_tabs = _rope_tables_t(jnp.arange(t), DK_A)
    qa, kat, va, qb, cmp_kv, sel_kv, win_kv, gt = _even_inproj(til_p, yp, mods_p, nw1, w_main, w_gate, tabs_p, kt_tabs)
    r3 = lambda a: a.reshape(n, t, a.shape[-1])
    tq = _prompt_tile_rows(t)
    tq_attn = math.gcd(t, 2 * tq)
    oa = _diff_attn_prompt(r3(qa), kat, r3(va), lam_w, sub_w, lam_init, tq_attn, tq)
    ccmp = _compress_prompt(r3(cmp_kv), wc, pe, w2c)
    n_cmp = (t - CMP_LEN) // CMP_STRIDE + 1
    n_sel = -(-t // SEL_BLOCK)
    o_cmp, selm = _cmp_topk(r3(qb), ccmp, nb=1, rq=tq, n_cmp=n_cmp, n_sel=n_sel, pos_base=0)
    o_sel = _nsa_attn_prompt(r3(qb), r3(sel_kv), selm, tq_attn, tq)
    o_win = _nsa_attn_prompt(r3(qb), r3(win_kv), None, tq_attn, tq)
    f2 = lambda a: a.reshape(n * t, a.shape[-1])
    yp = _even_out(til_p, yp, mods_p, f2(oa), f2(o_cmp), f2(o_sel), f2(o_win), gt, w_out, mlp)
    wl = min(WINDOW, t)
    ka_state = jnp.transpose(kat.reshape(n, H_A, 2, DK_A, t), (0, 4, 1, 2, 3))
    st_p = (ka_state, r3(va).reshape(n, t, H_A, DV_A),
            r3(cmp_kv).reshape(n, t, 2, G_B, DH_B), r3(sel_kv).reshape(n, t, 2, G_B, DH_B),
            r3(win_kv)[:, t - wl:].reshape(n, wl, 2, G_B, DH_B))

    pos_s = past + jnp.arange(s)
    tabs_s = tuple(jnp.tile(x, (til_s.b, 1)) for x in _rope_tables(pos_s, DK_A) + _rope_tables(pos_s, DH_B))
    qa, ka, va, qb, cmp_kv, sel_kv, win_kv, gt = _even_inproj(til_s, ys, mods_s, nw1, w_main, w_gate, tabs_s)
    r3 = lambda a: a.reshape(ns, s, a.shape[-1])
    slot_cache = lambda c: c.reshape(c.shape[0], PAGE_SIZE * SLOTS, LANES)
    c_dkt = jnp.transpose(c_dk, (0, 2, 3, 4, 1)).reshape(c_dk.shape[0], H_A * 2 * DK_A, PAGE_SIZE)
    oa = _diff_attn_sample(r3(qa), r3(ka), r3(va), c_dkt, slot_cache(c_dv), page_table, lam_w, sub_w, lam_init)
    ccmp = _compress_sample(slot_cache(c_cmp), page_table, wc, pe, w2c)
    total = past + s
    n_cmp = (total - CMP_LEN) // CMP_STRIDE + 1
    n_sel = -(-total // SEL_BLOCK)
    assert n_cmp <= ccmp.shape[3] and n_sel <= LANES
    nb = math.gcd(ns, LANES // s)
    o_cmp, selm = _cmp_topk(r3(qb), ccmp, nb=nb, rq=s, n_cmp=n_cmp, n_sel=n_sel, pos_base=past)
    o_sel = _sel_attn_sample(r3(qb), selm, r3(sel_kv), slot_cache(c_sel), page_table)
    wb = win_buf.shape[1]
    o_win, win_out = _win_attn_sample(r3(qb), win_buf.reshape(ns, wb * SLOTS, LANES), r3(win_kv), past)
    f2 = lambda a: a.reshape(ns * s, a.shape[-1])
    ys = _even_out(til_s, ys, mods_s, f2(oa), f2(o_cmp), f2(o_sel), f2(o_win), gt, w_out, mlp)
    st_s = (r3(ka).reshape(ns, s, H_A, 2, DK_A), r3(va).reshape(ns, s, H_A, DV_A),
            r3(cmp_kv).reshape(ns, s, 2, G_B, DH_B), r3(sel_kv).reshape(ns, s, 2, G_B, DH_B),
            win_out.reshape(ns, wb, 2, G_B, DH_B))
    return yp, ys, st_p, st_s


def _odd_layer(yp, ys, mods_p, mods_s, s0, wts, til_p, til_s, mlp):
    w_in = wts["w_in"]
    n_main = w_in.shape[1] - GATE_RANK
    w_main = w_in[:, :n_main].astype(BF16)
    w_gl = jnp.pad(w_in[:, n_main:], ((0, 0), (0, LANES - GATE_RANK))).astype(BF16)
    w_gate = jnp.pad(wts["w_gate"], ((0, LANES - GATE_RANK), (0, 0))).astype(BF16)
    b_gate = wts["b_gate"].reshape(1, -1)
    nw1 = wts["norm1"].reshape(1, D_MODEL)
    gnw = wts["gnorm"].reshape(1, DV_C)
    w_out = wts["w_out"].astype(BF16)
    outs = []
    for y, mods, til, state in ((yp, mods_p, til_p, None), (ys, mods_s, til_s, s0)):
        n, t, _ = y.shape
        q, k, v, r, la = _gla_inproj(til, y, mods, nw1, w_main, w_gl, w_gate, b_gate)
        r3 = lambda a: a.reshape(n, t, a.shape[-1])
        c = math.gcd(t, GLA_CHUNK)
        if c >= GLA_SUB:
            tt, cc, sub = math.gcd(t, 8 * c), c, GLA_SUB_LONG
        else:
            tt, cc, sub = t, GLA_SUB, GLA_SUB
        o, s_fin = _gla_recurrence(r3(q), r3(k), r3(v), r3(la), state, tt, cc, sub, H_C)
        y = _gla_out(til, y, mods, o.reshape(n * t, -1), r, gnw, w_out, mlp)
        outs.append((y, s_fin))
    return outs[0][0], outs[1][0], outs[0][1], outs[1][1]


def kernel(x_prompt, x_sample, c_prompt, c_sample, cache_diff_k, cache_diff_v, cache_cmp_kv, cache_sel_kv,
           state_win_kv, state_gla, page_table, norm1_w, norm2_w, ada_w, ada_b, even_w_in, even_w_out,
           diff_lambda_w, diff_subln_w, cmp_pe, cmp_w1, cmp_w2, gla_w_in, gla_w_gate, gla_b_gate, gla_norm_w,
           gla_w_out, mlp_w1, mlp_w2, final_norm_w):
    depth = ada_w.shape[0]
    n, t, _ = x_prompt.shape
    ns, s, _ = x_sample.shape
    til_p = _Tiling(n, t, 1, _prompt_tile_rows(t))
    til_s = _Tiling(ns, s, math.gcd(ns, 256 // s), s)

    pad = (-(n + ns)) % 8
    c_all = jnp.concatenate([c_prompt, c_sample, jnp.zeros((pad, D_MODEL), F32)], axis=0)
    mod = _adaln(c_all, ada_w, ada_b)

    yp, ys = x_prompt, x_sample
    st_p = [[] for _ in range(6)]
    st_s = [[] for _ in range(6)]
    fw = final_norm_w.reshape(1, D_MODEL)
    for l in range(depth):
        mods_p = _mods_for(mod[l], 0, n)
        mods_s = _mods_for(mod[l], n, n + ns)
        mlp = (norm2_w[l].reshape(1, D_MODEL), mlp_w1[l].astype(BF16), mlp_w2[l].astype(BF16), fw, l == depth - 1)
        if l % 2 == 0:
            e = l // 2
            lam_init = 0.8 - 0.6 * math.exp(-0.3 * l)
            wts = dict(w_in=even_w_in[e], w_out=even_w_out[e], lam_w=diff_lambda_w[e], sub_w=diff_subln_w[e],
                       cmp_pe=cmp_pe[e], cmp_w1=cmp_w1[e], cmp_w2=cmp_w2[e], norm1=norm1_w[l])
            caches = (cache_diff_k[e], cache_diff_v[e], cache_cmp_kv[e], cache_sel_kv[e], state_win_kv[e])
            yp, ys, sp, ss = _even_layer(yp, ys, mods_p, mods_s, caches, page_table, wts, lam_init, til_p, til_s, mlp)
            for i in range(5):
                st_p[i].append(sp[i])
                st_s[i].append(ss[i])
        else:
            o = l // 2
            wts = dict(w_in=gla_w_in[o], w_gate=gla_w_gate[o], b_gate=gla_b_gate[o], gnorm=gla_norm_w[o],
                       w_out=gla_w_out[o], norm1=norm1_w[l])
            yp, ys, gp, gs = _odd_layer(yp, ys, mods_p, mods_s, state_gla[o], wts, til_p, til_s, mlp)
            st_p[5].append(gp)
            st_s[5].append(gs)
    outs_p = [jnp.stack(x, axis=0) for x in st_p]
    outs_s = [jnp.stack(x, axis=0) for x in st_s]
    return (yp, ys, *outs_p, *outs_s)
```

```python
import functools
import math

import jax
import jax.numpy as jnp
from jax import lax
from jax.experimental import pallas as pl
from jax.experimental.pallas import tpu as pltpu

F32 = jnp.float32
BF16 = jnp.bfloat16

D_MODEL = 1024
PAGE_SIZE = 128
H_A = 4
DK_A = 64
DV_A = 128
H_B = 4
G_B = 2
R_B = 2
DH_B = 128
CMP_LEN = 32
CMP_STRIDE = 16
SEL_BLOCK = 64
TOP_N = 16
WINDOW = 512
H_C = 4
DK_C = 128
DV_C = 256
GATE_RANK = 16
GATE_TAU = 16.0
GLA_CHUNK = 64
GLA_SUB = 16
GLA_SUB_LONG = 16
GLA_SAFE_DECAY = 80.0
D_FF = 4 * D_MODEL
ROPE_THETA = 10000.0
EPS = 1e-6

LANES = 128
SLOTS = 4
MASK_NEG = -1e30
LOG2E = 1.4426950408889634
SAFE_SHIFT = 56.0
SEL_NEG = 32768.0
VMEM_LIMIT_MB = 56


def _cparams(sem, vmem_mb=VMEM_LIMIT_MB):
    return pltpu.CompilerParams(dimension_semantics=sem, vmem_limit_bytes=vmem_mb * 1024 * 1024)


def _dot(a, b):
    return jnp.dot(a, b, preferred_element_type=F32)


def _dot_nt(a, b):
    return lax.dot_general(a, b, (((1,), (1,)), ((), ())), preferred_element_type=F32)


def _dot_tn(a, b):
    return lax.dot_general(a, b, (((0,), (0,)), ((), ())), preferred_element_type=F32)


def _iota(shape, dim):
    return lax.broadcasted_iota(jnp.int32, shape, dim)


def _rms(x, w):
    ms = jnp.mean(x * x, axis=-1, keepdims=True)
    return x * lax.rsqrt(ms + EPS) * w


def _norm_mod(x, nw, shift, scale):
    return _rms(x, nw) * (1.0 + scale) + shift


class _Tiling:
    def __init__(self, nb, r, b, rt):
        assert nb % b == 0 and r % rt == 0 and (b == 1 or rt == r)
        self.nb, self.r, self.b, self.rt = nb, r, b, rt
        self.grid = (nb // b, r // rt)
        self.rows = b * rt
        self.nrb = r // rt

    def x_spec(self, d):
        return pl.BlockSpec((self.b, self.rt, d), lambda i, j: (i, j, 0))

    def mod_spec(self, k):
        return pl.BlockSpec((1, self.b, 1, D_MODEL), lambda i, j: (k, i, 0, 0))

    def flat_spec(self, c):
        nrb = self.nrb
        return pl.BlockSpec((self.rows, c), lambda i, j: (i * nrb + j, 0))

    def tab_spec(self):
        return pl.BlockSpec((self.rows, LANES), lambda i, j: (j, 0))


def _full_spec(shape):
    nd = len(shape)
    return pl.BlockSpec(shape, lambda *_: (0,) * nd)


def _const_spec(shape):
    nd = len(shape)
    return pl.BlockSpec(shape, lambda *_: (0,) * nd, pipeline_mode=pl.Buffered(1))


def _adaln_kernel(c_ref, w_ref, b_ref, o_ref):
    c = c_ref[...]
    a = (c * jax.nn.sigmoid(c)).astype(BF16)
    o_ref[0] = _dot(a, w_ref[0].astype(BF16)) + b_ref[0]


def _adaln(c_all, ada_w, ada_b):
    depth, d, n6 = ada_w.shape
    rows = c_all.shape[0]
    tn = 1536
    return pl.pallas_call(
        _adaln_kernel,
        grid=(depth, n6 // tn),
        in_specs=[pl.BlockSpec((rows, d), lambda l, j: (0, 0)),
                  pl.BlockSpec((1, d, tn), lambda l, j: (l, 0, j)),
                  pl.BlockSpec((1, 1, tn), lambda l, j: (l, 0, j))],
        out_specs=pl.BlockSpec((1, rows, tn), lambda l, j: (l, 0, j)),
        out_shape=jax.ShapeDtypeStruct((depth, rows, n6), F32),
        compiler_params=_cparams(("parallel", "parallel")),
        name="adaln",
    )(c_all, ada_w, ada_b.reshape(depth, 1, n6))


def _swap_half(x, half):
    if 2 * half == LANES:
        return pltpu.roll(x, half, 1)
    lane = _iota((1, LANES), 1)
    lo = (lane % (2 * half)) < half
    return jnp.where(lo, pltpu.roll(x, LANES - half, 1), pltpu.roll(x, half, 1))


def _even_inproj_kernel(*refs, k_feature_major):
    if k_feature_major:
        (x_ref, sh_ref, sc_ref, nw_ref, w_ref, wg_ref, c64_ref, s64_ref, c128_ref, s128_ref, wkt_ref, ct_ref, st_ref,
         qa_ref, ka_ref, va_ref, qb_ref, cmp_ref, sel_ref, win_ref, gt_ref) = refs
    else:
        (x_ref, sh_ref, sc_ref, nw_ref, w_ref, wg_ref, c64_ref, s64_ref, c128_ref, s128_ref,
         qa_ref, ka_ref, va_ref, qb_ref, cmp_ref, sel_ref, win_ref, gt_ref) = refs
    h = _norm_mod(x_ref[...], nw_ref[...], sh_ref[0], sc_ref[0])
    h = h.reshape(-1, D_MODEL).astype(BF16)
    c64, s64, c128, s128 = c64_ref[...], s64_ref[...], c128_ref[...], s128_ref[...]

    def rope64(p):
        return p * c64 + _swap_half(p, DK_A // 2) * s64

    def rope128(p):
        return p * c128 + _swap_half(p, DH_B // 2) * s128

    def project(ref, off, ropes):
        p = _dot(h, w_ref[:, off:off + 4 * LANES])
        for j, rope in enumerate(ropes):
            sl = slice(j * LANES, (j + 1) * LANES)
            ref[:, sl] = p[:, sl] if rope is None else rope(p[:, sl])

    project(qa_ref, 0, [rope64] * 4)
    if k_feature_major:
        kt = _dot_nt(wkt_ref[...], h)
        ct, st = ct_ref[...], st_ref[...]
        half = DK_A // 2
        for grp in range(H_A * 2):
            x1 = kt[grp * DK_A:grp * DK_A + half]
            x2 = kt[grp * DK_A + half:(grp + 1) * DK_A]
            ka_ref[0, grp * DK_A:grp * DK_A + half, :] = x1 * ct - x2 * st
            ka_ref[0, grp * DK_A + half:(grp + 1) * DK_A, :] = x2 * ct + x1 * st
    else:
        project(ka_ref, 512, [rope64] * 4)
    project(va_ref, 1024, [None] * 4)
    project(qb_ref, 1536, [rope128] * 4)
    for t, ref in enumerate((cmp_ref, sel_ref, win_ref)):
        project(ref, 2048 + t * 512, [rope128, rope128, None, None])
    gt_ref[...] = jax.nn.sigmoid(_dot(h, wg_ref[...]))


def _even_inproj(til, x, mods, nw, w_main, w_gate, tabs, kt_tabs=None):
    rows_total = til.nb * til.r
    widths = (512, 512, 512, 512, 512, 512, 512, LANES)
    in_specs = [til.x_spec(D_MODEL), til.mod_spec(0), til.mod_spec(1), _full_spec((1, D_MODEL)),
                _full_spec(w_main.shape), _full_spec(w_gate.shape)] + [til.tab_spec()] * 4
    args = [x, mods, mods, nw, w_main, w_gate, *tabs]
    out_specs = [til.flat_spec(c) for c in widths]
    out_shape = [jax.ShapeDtypeStruct((rows_total, c), F32) for c in widths]
    if kt_tabs is not None:
        assert til.b == 1
        w_kt = jnp.transpose(w_main[:, 512:1024])
        in_specs += [_full_spec(w_kt.shape)] + [pl.BlockSpec((DK_A // 2, til.rt), lambda i, j: (0, j))] * 2
        args += [w_kt, *kt_tabs]
        out_specs[1] = pl.BlockSpec((1, 512, til.rt), lambda i, j: (i, 0, j))
        out_shape[1] = jax.ShapeDtypeStruct((til.nb, 512, til.r), F32)
    return pl.pallas_call(
        functools.partial(_even_inproj_kernel, k_feature_major=kt_tabs is not None),
        grid=til.grid,
        in_specs=in_specs,
        out_specs=out_specs,
        out_shape=out_shape,
        compiler_params=_cparams(("parallel", "parallel")),
        name="even_inproj",
    )(*args)


def _for_tiles(lo, hi, fn):
    n = hi - lo

    def pair(j, carry):
        fn(lo + 2 * j)
        fn(lo + 2 * j + 1)
        return carry

    lax.fori_loop(0, n // 2, pair, 0)

    @pl.when(n % 2 == 1)
    def _():
        fn(hi - 1)


def _row_sumsq(x):
    xf = x.astype(F32)
    return _dot((xf * xf).astype(BF16), jnp.ones((LANES, LANES), BF16))


def _key_norm_bound(k_ref, n_keys, tile):
    def body(kb, m):
        k = k_ref[0, pl.ds(pl.multiple_of(kb * tile, tile), tile), :].astype(BF16)
        return jnp.maximum(m, _row_sumsq(k))

    m = lax.fori_loop(0, n_keys // tile, body, jnp.zeros((tile, LANES), F32))
    return jnp.max(m, axis=0, keepdims=True)


def _score_bound(q2, kmax2):
    qmax2 = jnp.max(_row_sumsq(q2), axis=0, keepdims=True)
    return jnp.sqrt(qmax2 * kmax2) * 1.05


def _two_pass_attention(mx_ref, acc_ref, lo, hi, last, scores, values, mask_body=False, bound=None):
    def lane_max(s):
        m = s[:, 0:LANES]
        for c in range(1, s.shape[1] // LANES):
            m = jnp.maximum(m, s[:, c * LANES:(c + 1) * LANES])
        return m

    def pass1(kb, masked):
        mx_ref[...] = jnp.maximum(mx_ref[...], lane_max(scores(kb, masked)))

    def exact_max():
        mx_ref[...] = jnp.full(mx_ref.shape, MASK_NEG, F32)
        _for_tiles(lo, hi, lambda kb: pass1(kb, mask_body))
        for kb in last:
            pass1(kb, True)
        mx_ref[...] = jnp.broadcast_to(jnp.max(mx_ref[...], axis=-1, keepdims=True), mx_ref.shape)

    if bound is None:
        exact_max()
    else:
        safe = jnp.max(bound) <= SAFE_SHIFT

        @pl.when(safe)
        def _():
            mx_ref[...] = jnp.broadcast_to(bound, mx_ref.shape)

        @pl.when(jnp.logical_not(safe))
        def _():
            exact_max()

    acc_ref[...] = jnp.zeros(acc_ref.shape, F32)

    def pass2(kb, masked):
        s = scores(kb, masked)
        m = mx_ref[...]
        p = jnp.concatenate([jnp.exp2(s[:, c * LANES:(c + 1) * LANES] - m) for c in range(s.shape[1] // LANES)],
                            axis=-1).astype(BF16)
        v = values(kb)
        v1 = jnp.concatenate([v, jnp.ones(v.shape, BF16)], axis=-1)
        acc_ref[...] += _dot(p, v1)

    _for_tiles(lo, hi, lambda kb: pass2(kb, mask_body))
    for kb in last:
        pass2(kb, True)
    return acc_ref[:, 0:LANES] / acc_ref[:, LANES:2 * LANES]


def _diff_lambda(lw, lam_init):
    a = jnp.sum(lw[0:1] * lw[1:2], axis=-1, keepdims=True)
    b = jnp.sum(lw[2:3] * lw[3:4], axis=-1, keepdims=True)
    return jnp.exp(a) - jnp.exp(b) + lam_init


def _diff_finalize(o2, tq, lam, sub_w, lam_init):
    od = o2[:tq] - lam * o2[tq:]
    return _rms(od, sub_w) * (1.0 - lam_init)


def _diff_flash_kernel(q_ref, k_ref, v_ref, lamw_ref, sub_ref, o_ref, q2_ref, mx_ref, acc_ref, kmax_ref, *,
                       tq, tk, lam_init):
    i = pl.program_id(2)

    def key_tile(kb):
        return k_ref[0, :, pl.ds(pl.multiple_of(kb * tk, tk), tk)].astype(BF16)

    @pl.when(i == 0)
    def _():
        def body(kb, m):
            kf = key_tile(kb).astype(F32)
            return jnp.maximum(m, jnp.sum(kf * kf, axis=0, keepdims=True))

        m = lax.fori_loop(0, k_ref.shape[2] // tk, body, jnp.zeros((1, tk), F32))
        kmax_ref[...] = jnp.broadcast_to(jnp.max(m, axis=-1, keepdims=True), kmax_ref.shape)

    q = q_ref[0] * (DK_A ** -0.5 * LOG2E)
    lane = _iota((1, LANES), 1)
    q2_ref[0:tq] = jnp.where(lane < DK_A, q, 0.0).astype(BF16)
    q2_ref[tq:2 * tq] = jnp.where(lane >= DK_A, q, 0.0).astype(BF16)
    bound = _score_bound(q2_ref[...], kmax_ref[0:1])

    def scores(kb, masked):
        s = _dot(q2_ref[...], key_tile(kb))
        if masked:
            r = _iota((2 * tq, tk), 0)
            qp = i * tq + jnp.where(r >= tq, r - tq, r)
            s = jnp.where(kb * tk + _iota((2 * tq, tk), 1) <= qp, s, MASK_NEG)
        return s

    def values(kb):
        return v_ref[0, pl.ds(pl.multiple_of(kb * tk, tk), tk), :].astype(BF16)

    per = tq // tk
    o2 = _two_pass_attention(mx_ref, acc_ref, 0, i * per, [i * per + u for u in range(per)], scores, values,
                             bound=bound)
    o_ref[0] = _diff_finalize(o2, tq, _diff_lambda(lamw_ref[...], lam_init), sub_ref[...], lam_init)


def _diff_attn_prompt(qa, kat, va, lam_w, sub_w, lam_init, tq, tk):
    n, t, _ = qa.shape
    kern = functools.partial(_diff_flash_kernel, tq=tq, tk=tk, lam_init=lam_init)
    return pl.pallas_call(
        kern,
        grid=(n, H_A, t // tq),
        in_specs=[pl.BlockSpec((1, tq, LANES), lambda b, h, i: (b, i, h)),
                  pl.BlockSpec((1, LANES, t), lambda b, h, i: (b, h, 0)),
                  pl.BlockSpec((1, t, LANES), lambda b, h, i: (b, 0, h)),
                  _full_spec(lam_w.shape), _full_spec(sub_w.shape)],
        out_specs=pl.BlockSpec((1, tq, LANES), lambda b, h, i: (b, i, h)),
        out_shape=jax.ShapeDtypeStruct((n, t, H_A * DV_A), F32),
        scratch_shapes=[pltpu.VMEM((2 * tq, LANES), BF16), pltpu.VMEM((2 * tq, LANES), F32),
                        pltpu.VMEM((2 * tq, 2 * LANES), F32), pltpu.VMEM((8, LANES), F32)],
        compiler_params=_cparams(("parallel", "parallel", "arbitrary")),
        name="diff_attn_prompt",
    )(qa, kat, va, lam_w, sub_w)


def _sel_onehot(rows, first_block):
    blk = _iota((rows, LANES), 0) // SEL_BLOCK + first_block
    return jnp.where(blk == _iota((rows, LANES), 1), 1.0, 0.0).astype(BF16)


def _nsa_flash_kernel(*refs, tq, tk, use_sel):
    if use_sel:
        q_ref, selm_ref, k_ref, v_ref, o_ref, q2_ref, mx_ref, acc_ref, kmax_ref = refs
    else:
        q_ref, k_ref, v_ref, o_ref, q2_ref, mx_ref, acc_ref, kmax_ref = refs
    i = pl.program_id(2)

    @pl.when(i == 0)
    def _():
        kmax_ref[...] = jnp.broadcast_to(_key_norm_bound(k_ref, k_ref.shape[1], tk), kmax_ref.shape)

    q = q_ref[0] * (DH_B ** -0.5 * LOG2E)
    for r in range(R_B):
        q2_ref[r * tq:(r + 1) * tq, 0:LANES] = q[:, r * LANES:(r + 1) * LANES].astype(BF16)
        if use_sel:
            q2_ref[r * tq:(r + 1) * tq, LANES:2 * LANES] = selm_ref[0, 0]
    bound = _score_bound(q2_ref[:, 0:LANES], kmax_ref[0:1])

    def scores(kb, masked):
        k = k_ref[0, pl.ds(pl.multiple_of(kb * tk, tk), tk), :].astype(BF16)
        if use_sel:
            k = jnp.concatenate([k, _sel_onehot(tk, kb * (tk // SEL_BLOCK))], axis=-1)
        s = _dot_nt(q2_ref[...], k)
        if masked:
            r = _iota((R_B * tq, tk), 0)
            qp = i * tq + jnp.where(r >= tq, r - tq, r)
            kp = kb * tk + _iota((R_B * tq, tk), 1)
            ok = kp <= qp
            if not use_sel:
                ok = ok & (kp > qp - WINDOW)
            s = jnp.where(ok, s, MASK_NEG)
        return s

    def values(kb):
        return v_ref[0, pl.ds(pl.multiple_of(kb * tk, tk), tk), :].astype(BF16)

    per = tq // tk
    lo = 0 if use_sel else jnp.maximum(i * per - WINDOW // tk, 0)
    o = _two_pass_attention(mx_ref, acc_ref, lo, i * per, [i * per + u for u in range(per)], scores, values,
                            mask_body=not use_sel, bound=bound)
    for r in range(R_B):
        o_ref[0, :, r * LANES:(r + 1) * LANES] = o[r * tq:(r + 1) * tq]


def _nsa_attn_prompt(qb, kv, selm, tq, tk):
    n, t, _ = qb.shape
    use_sel = selm is not None
    kd = 2 * LANES if use_sel else LANES
    kern = functools.partial(_nsa_flash_kernel, tq=tq, tk=tk, use_sel=use_sel)
    in_specs = [pl.BlockSpec((1, tq, R_B * LANES), lambda b, g, i: (b, i, g))]
    args = [qb]
    if use_sel:
        in_specs.append(pl.BlockSpec((1, 1, tq, LANES), lambda b, g, i: (b, g, i, 0)))
        args.append(selm)
    in_specs += [pl.BlockSpec((1, t, LANES), lambda b, g, i: (b, 0, g)),
                 pl.BlockSpec((1, t, LANES), lambda b, g, i: (b, 0, G_B + g))]
    args += [kv, kv]
    return pl.pallas_call(
        kern,
        grid=(n, G_B, t // tq),
        in_specs=in_specs,
        out_specs=pl.BlockSpec((1, tq, R_B * LANES), lambda b, g, i: (b, i, g)),
        out_shape=jax.ShapeDtypeStruct((n, t, H_B * DH_B), F32),
        scratch_shapes=[pltpu.VMEM((R_B * tq, kd), BF16), pltpu.VMEM((R_B * tq, LANES), F32),
                        pltpu.VMEM((R_B * tq, 2 * LANES), F32), pltpu.VMEM((8, LANES), F32)],
        compiler_params=_cparams(("parallel", "parallel", "arbitrary")),
        name="nsa_sel_prompt" if use_sel else "nsa_win_prompt",
    )(*args)


def _compress_core(load, wc_ref, pe_ref, w2_ref, wi, nchunk):
    rows = G_B * nchunk
    acc = jnp.zeros((rows, 2 * LANES), F32)
    pew = jnp.zeros((16, 2 * LANES), F32)
    for u in range(CMP_STRIDE // 2):
        lhs = jnp.concatenate([load(2 * u), load(2 * u + 1)], axis=-1).astype(BF16)
        w = wc_ref[wi, u]
        acc = acc + _dot(lhs, w)
        pew = pew + _dot(pe_ref[wi, u], w)
    first = acc[:, :LANES]
    second = pltpu.roll(acc[:, LANES:], rows - 1, 0)
    hid = first + second + pew[0:1, :LANES] + pew[8:9, LANES:]
    hid = hid * jax.nn.sigmoid(hid)
    return _dot(hid.astype(BF16), w2_ref[wi])


def _compress_prompt_kernel(*refs, nchunk):
    x_refs = refs[:G_B]
    wc_ref, pe_ref, w2_ref, o_ref = refs[G_B:]

    def load(tok):
        return jnp.concatenate([xr[0, pl.ds(tok, nchunk, stride=CMP_STRIDE), :] for xr in x_refs], axis=0)

    out = _compress_core(load, wc_ref, pe_ref, w2_ref, 0, nchunk)
    for g in range(G_B):
        o_ref[0, 0, g] = out[g * nchunk:(g + 1) * nchunk]


def _compress_sample_kernel(pt_ref, *refs, n_pages, spb):
    pg_refs = refs[:spb * n_pages]
    wc_ref, pe_ref, w2_ref, o_ref = refs[spb * n_pages:]
    cps = PAGE_SIZE // CMP_STRIDE
    nchunk = n_pages * cps
    for kv in range(2):
        by_tok = [pltpu.einshape("ctd->tcd", pg[0, pl.ds(kv * G_B + g, PAGE_SIZE, stride=SLOTS), :]
                                 .reshape(cps, CMP_STRIDE, LANES))
                  for smp in range(spb) for g in range(G_B) for pg in pg_refs[smp * n_pages:(smp + 1) * n_pages]]

        def load(tok):
            return jnp.concatenate([x[tok] for x in by_tok], axis=0)

        out = _compress_core(load, wc_ref, pe_ref, w2_ref, kv, spb * nchunk)
        for smp in range(spb):
            for g in range(G_B):
                seg = smp * G_B + g
                o_ref[smp, kv, g] = out[seg * nchunk:(seg + 1) * nchunk]


def _compress_weights(cmp_pe, cmp_w1, cmp_w2):
    w1 = cmp_w1.reshape(2, CMP_LEN, DH_B, DH_B)
    wab = jnp.concatenate([w1[:, :CMP_STRIDE], w1[:, CMP_STRIDE:]], axis=-1)
    wc = wab.reshape(2, CMP_STRIDE // 2, 2 * DH_B, 2 * DH_B).astype(BF16)
    pa = cmp_pe[:, :CMP_STRIDE].reshape(2, CMP_STRIDE // 2, 1, 2 * DH_B)
    pb = cmp_pe[:, CMP_STRIDE:].reshape(2, CMP_STRIDE // 2, 1, 2 * DH_B)
    z = jnp.zeros((2, CMP_STRIDE // 2, 7, 2 * DH_B), F32)
    pe = jnp.concatenate([pa, z, pb, z], axis=2).astype(BF16)
    return wc, pe, cmp_w2.astype(BF16)


def _compress_prompt(cmp_kv, wc, pe, w2):
    n, t, _ = cmp_kv.shape
    nchunk = t // CMP_STRIDE
    kern = functools.partial(_compress_prompt_kernel, nchunk=nchunk)
    return pl.pallas_call(
        kern,
        grid=(n, 2),
        in_specs=[pl.BlockSpec((1, t, DH_B), functools.partial(lambda b, kv, g: (b, 0, kv * G_B + g), g=g))
                  for g in range(G_B)] + [
                  pl.BlockSpec((1,) + wc.shape[1:], lambda b, kv: (kv, 0, 0, 0)),
                  pl.BlockSpec((1,) + pe.shape[1:], lambda b, kv: (kv, 0, 0, 0)),
                  pl.BlockSpec((1, DH_B, DH_B), lambda b, kv: (kv, 0, 0))],
        out_specs=pl.BlockSpec((1, 1, G_B, nchunk, DH_B), lambda b, kv: (b, kv, 0, 0, 0)),
        out_shape=jax.ShapeDtypeStruct((n, 2, G_B, nchunk, DH_B), F32),
        compiler_params=_cparams(("parallel", "parallel")),
        name="compress_prompt",
    )(*([cmp_kv] * G_B), wc, pe, w2)


def _compress_sample(cache, page_table, wc, pe, w2):
    n, n_pages = page_table.shape
    nchunk = n_pages * (PAGE_SIZE // CMP_STRIDE)
    spb = math.gcd(n, 2)
    kern = functools.partial(_compress_sample_kernel, n_pages=n_pages, spb=spb)
    page_specs = [pl.BlockSpec((1, PAGE_SIZE * SLOTS, LANES),
                               functools.partial(lambda b, pt, smp, j: (pt[b * spb + smp, j], 0, 0), smp=smp, j=j))
                  for smp in range(spb) for j in range(n_pages)]
    grid_spec = pltpu.PrefetchScalarGridSpec(
        num_scalar_prefetch=1,
        grid=(n // spb,),
        in_specs=page_specs + [
            pl.BlockSpec(wc.shape, lambda b, pt: (0, 0, 0, 0)),
            pl.BlockSpec(pe.shape, lambda b, pt: (0, 0, 0, 0)),
            pl.BlockSpec(w2.shape, lambda b, pt: (0, 0, 0))],
        out_specs=pl.BlockSpec((spb, 2, G_B, nchunk, DH_B), lambda b, pt: (b, 0, 0, 0, 0)),
    )
    return pl.pallas_call(
        kern,
        grid_spec=grid_spec,
        out_shape=jax.ShapeDtypeStruct((n, 2, G_B, nchunk, DH_B), F32),
        compiler_params=_cparams(("parallel",)),
        name="compress_sample",
    )(page_table, *([cache] * (spb * n_pages)), wc, pe, w2)


def _cmp_topk_kernel(q_ref, kc_ref, vc_ref, ocmp_ref, selm_ref, p_scr, v_scr, *,
                     nb, rq, nck, n_cmp, n_sel, nsp, pos_base):
    tq = nb * rq
    j = pl.program_id(2)
    scale = DH_B ** -0.5
    kidx = _iota((1, nck), 1)
    qpos_c = pos_base + j * rq + _iota((rq, 1), 0)
    valid = (kidx < n_cmp) & (CMP_STRIDE * kidx + (CMP_LEN - 1) <= qpos_c)
    for b in range(nb):
        kcc = kc_ref[b, 0, 0].astype(BF16)
        vcc = vc_ref[b, 0, 0].astype(BF16)
        q = q_ref[b] * scale
        q2 = jnp.concatenate([q[:, r * LANES:(r + 1) * LANES] for r in range(R_B)], axis=0).astype(BF16)
        valid2 = jnp.concatenate([valid] * R_B, axis=0)
        s = jnp.where(valid2, _dot_nt(q2, kcc), MASK_NEG)
        m = jnp.max(s, axis=-1, keepdims=True)
        p = jnp.where(valid2, jnp.exp(s - m), 0.0)
        p = p / jnp.maximum(jnp.sum(p, axis=-1, keepdims=True), 1e-30)
        o = _dot(p.astype(BF16), vcc)
        psum = jnp.zeros((rq, nck), F32)
        for r in range(R_B):
            ocmp_ref[b, :, r * LANES:(r + 1) * LANES] = o[r * rq:(r + 1) * rq]
            psum = psum + p[r * rq:(r + 1) * rq]
        p_scr[b * rq:(b + 1) * rq, :] = psum
    psum = p_scr[...]
    p_hi = psum.astype(BF16)
    p_lo = (psum - p_hi.astype(F32)).astype(BF16)
    srow = _iota((LANES, nck), 0)
    kcol = _iota((LANES, nck), 1)
    cov = (CMP_STRIDE * kcol < SEL_BLOCK * srow + SEL_BLOCK) & (CMP_STRIDE * kcol + CMP_LEN > SEL_BLOCK * srow)
    cov = cov & (srow < n_sel) & (kcol < n_cmp)
    cov_t = jnp.where(cov, 1.0, 0.0).astype(BF16)
    imp_t = _dot_nt(cov_t, p_hi) + _dot_nt(cov_t, p_lo)
    qpos_l = pos_base + j * rq + _iota((1, tq), 1) % rq
    sb = _iota((nsp, 1), 0)
    valid_b = (sb < n_sel) & (sb * SEL_BLOCK <= qpos_l)
    forced = (sb == 0) | (sb == qpos_l // SEL_BLOCK)
    vals = jnp.where(forced, jnp.inf, jnp.where(valid_b, imp_t[:nsp], -jnp.inf))
    v_scr[...] = vals

    def body(jj, cnt):
        vj = v_scr[pl.ds(jj, 1), :]
        tie = jnp.where(sb > jj, 1.0, 0.0)
        return cnt + jnp.where(vj > vals, 1.0, jnp.where(vj == vals, tie, 0.0))

    last_pos = pos_base + (j + 1) * rq - 1
    n_rank = jnp.minimum(last_pos // SEL_BLOCK + 1, nsp)
    cnt = lax.fori_loop(0, n_rank, body, jnp.zeros((nsp, tq), F32))
    keep = valid_b & (cnt < float(min(TOP_N, n_sel)))
    selm_t = jnp.where(keep, 0.0, -SEL_NEG)
    if nsp < LANES:
        selm_t = jnp.concatenate([selm_t, jnp.zeros((LANES - nsp, tq), F32)], axis=0)
    selm = selm_t.T.astype(BF16)
    for b in range(nb):
        selm_ref[b, 0] = selm[b * rq:(b + 1) * rq]


def _cmp_topk(qb, ccmp, *, nb, rq, n_cmp, n_sel, pos_base):
    n, r, _ = qb.shape
    nck = ccmp.shape[3]
    nsp = -(-n_sel // 8) * 8
    tq = nb * rq
    kern = functools.partial(_cmp_topk_kernel, nb=nb, rq=rq, nck=nck, n_cmp=n_cmp, n_sel=n_sel, nsp=nsp,
                             pos_base=pos_base)
    return pl.pallas_call(
        kern,
        grid=(n // nb, G_B, r // rq),
        in_specs=[pl.BlockSpec((nb, rq, R_B * LANES), lambda a, g, j: (a, j, g)),
                  pl.BlockSpec((nb, 1, 1, nck, DH_B), lambda a, g, j: (a, 0, g, 0, 0)),
                  pl.BlockSpec((nb, 1, 1, nck, DH_B), lambda a, g, j: (a, 1, g, 0, 0))],
        out_specs=[pl.BlockSpec((nb, rq, R_B * LANES), lambda a, g, j: (a, j, g)),
                   pl.BlockSpec((nb, 1, rq, LANES), lambda a, g, j: (a, g, j, 0))],
        out_shape=[jax.ShapeDtypeStruct((n, r, H_B * DH_B), F32),
                   jax.ShapeDtypeStruct((n, G_B, r, LANES), BF16)],
        scratch_shapes=[pltpu.VMEM((tq, nck), F32), pltpu.VMEM((nsp, tq), F32)],
        compiler_params=_cparams(("parallel", "parallel", "parallel")),
        name="cmp_topk",
    )(qb, ccmp, ccmp)


def _attend_scores(pieces):
    m = None
    for s, _ in pieces:
        ms = jnp.max(s, axis=-1, keepdims=True)
        m = ms if m is None else jnp.maximum(m, ms)
    l = jnp.zeros_like(m)
    acc = jnp.zeros((m.shape[0], LANES), F32)
    for s, v in pieces:
        p = jnp.exp(s - m)
        l = l + jnp.sum(p, axis=-1, keepdims=True)
        acc = acc + _dot(p.astype(BF16), v)
    return acc / l


def _attend_pieces(q2, pieces):
    scored = []
    for k, v, mask in pieces:
        s = _dot_nt(q2, k)
        scored.append((s if mask is None else jnp.where(mask, s, MASK_NEG), v))
    return _attend_scores(scored)


def _pad_rows(x, rows):
    return jnp.concatenate([x, jnp.zeros((rows - x.shape[0], x.shape[1]), x.dtype)], axis=0)


def _new_token_mask(m_rows, s):
    r = _iota((m_rows, LANES), 0) % s
    return _iota((m_rows, LANES), 1) <= r


def _diff_sample_kernel(pt_ref, *refs, n_pages, s, lam_init):
    kp_refs = refs[:n_pages]
    vp_refs = refs[n_pages:2 * n_pages]
    q_ref, kn_ref, vn_ref, lamw_ref, sub_ref, o_ref = refs[2 * n_pages:]
    lane = _iota((1, LANES), 1)
    lam = _diff_lambda(lamw_ref[...], lam_init)
    new_mask = _new_token_mask(2 * s, s)
    for h in range(H_A):
        cols = slice(h * LANES, (h + 1) * LANES)
        q = q_ref[0, :, cols] * (DK_A ** -0.5)
        q2 = jnp.concatenate([jnp.where(lane < DK_A, q, 0.0), jnp.where(lane >= DK_A, q, 0.0)], axis=0).astype(BF16)
        kt_past = jnp.concatenate([r[0, cols, :].astype(BF16) for r in kp_refs], axis=1)
        s_past = _dot(q2, kt_past)
        vpast = jnp.concatenate([_slot_rows(r, h, PAGE_SIZE).astype(BF16) for r in vp_refs], axis=0)
        knew = _pad_rows(kn_ref[0, :, cols], LANES).astype(BF16)
        vnew = _pad_rows(vn_ref[0, :, cols], LANES).astype(BF16)
        s_new = jnp.where(new_mask, _dot_nt(q2, knew), MASK_NEG)
        o2 = _attend_scores([(s_past, vpast), (s_new, vnew)])
        o_ref[0, :, cols] = _diff_finalize(o2, s, lam, sub_ref[...], lam_init)


def _page_specs(n_pages, rows, width):
    return [pl.BlockSpec((1, rows, width), functools.partial(lambda b, pt, j: (pt[b, j], 0, 0), j=j))
            for j in range(n_pages)]


def _slot_rows(ref, slot, n_tok):
    return ref[0, pl.ds(slot, n_tok, stride=SLOTS), :]


def _diff_attn_sample(qa, ka, va, cache_k, cache_v, page_table, lam_w, sub_w, lam_init):
    n, s, _ = qa.shape
    n_pages = page_table.shape[1]
    kern = functools.partial(_diff_sample_kernel, n_pages=n_pages, s=s, lam_init=lam_init)
    row_spec = pl.BlockSpec((1, s, 512), lambda b, pt: (b, 0, 0))
    grid_spec = pltpu.PrefetchScalarGridSpec(
        num_scalar_prefetch=1,
        grid=(n,),
        in_specs=_page_specs(n_pages, 512, PAGE_SIZE) + _page_specs(n_pages, PAGE_SIZE * SLOTS, LANES)
        + [row_spec, row_spec, row_spec,
           pl.BlockSpec(lam_w.shape, lambda b, pt: (0, 0)), pl.BlockSpec(sub_w.shape, lambda b, pt: (0, 0))],
        out_specs=row_spec,
    )
    return pl.pallas_call(
        kern,
        grid_spec=grid_spec,
        out_shape=jax.ShapeDtypeStruct((n, s, 512), F32),
        compiler_params=_cparams(("parallel",)),
        name="diff_attn_sample",
    )(page_table, *([cache_k] * n_pages), *([cache_v] * n_pages), qa, ka, va, lam_w, sub_w)


def _sel_sample_kernel(pt_ref, *refs, n_pages, s):
    pg_refs = refs[:n_pages]
    q_ref, selm_ref, new_ref, o_ref = refs[n_pages:]
    past = n_pages * PAGE_SIZE
    new_mask = _new_token_mask(R_B * s, s)
    oh_past = _sel_onehot(past, 0)
    new_blk = past // SEL_BLOCK
    oh_new = jnp.where((_iota((LANES, LANES), 1) == new_blk) & (_iota((LANES, LANES), 0) < s), 1.0, 0.0)
    oh_new = oh_new.astype(BF16)
    scale = DH_B ** -0.5
    for g in range(G_B):
        kcols = slice(g * LANES, (g + 1) * LANES)
        vcols = slice((G_B + g) * LANES, (G_B + g + 1) * LANES)
        selm = selm_ref[0, g]
        q2 = jnp.concatenate(
            [jnp.concatenate([(q_ref[0, :, (g * R_B + r) * LANES:(g * R_B + r + 1) * LANES] * scale).astype(BF16),
                              selm], axis=-1) for r in range(R_B)], axis=0)
        kpast = jnp.concatenate([_slot_rows(r, g, PAGE_SIZE).astype(BF16) for r in pg_refs], axis=0)
        vpast = jnp.concatenate([_slot_rows(r, G_B + g, PAGE_SIZE).astype(BF16) for r in pg_refs], axis=0)
        kpast = jnp.concatenate([kpast, oh_past], axis=-1)
        knew = jnp.concatenate([_pad_rows(new_ref[0, :, kcols], LANES).astype(BF16), oh_new], axis=-1)
        vnew = _pad_rows(new_ref[0, :, vcols], LANES).astype(BF16)
        o2 = _attend_pieces(q2, [(kpast, vpast, None), (knew, vnew, new_mask)])
        for r in range(R_B):
            o_ref[0, :, (g * R_B + r) * LANES:(g * R_B + r + 1) * LANES] = o2[r * s:(r + 1) * s]


def _sel_attn_sample(qb, selm, sel_new, cache, page_table):
    n, s, _ = qb.shape
    n_pages = page_table.shape[1]
    assert s <= SEL_BLOCK and (n_pages * PAGE_SIZE) % SEL_BLOCK == 0
    kern = functools.partial(_sel_sample_kernel, n_pages=n_pages, s=s)
    row_spec = pl.BlockSpec((1, s, 512), lambda b, pt: (b, 0, 0))
    grid_spec = pltpu.PrefetchScalarGridSpec(
        num_scalar_prefetch=1,
        grid=(n,),
        in_specs=_page_specs(n_pages, PAGE_SIZE * SLOTS, LANES)
        + [row_spec, pl.BlockSpec((1, G_B, s, LANES), lambda b, pt: (b, 0, 0, 0)), row_spec],
        out_specs=row_spec,
    )
    return pl.pallas_call(
        kern,
        grid_spec=grid_spec,
        out_shape=jax.ShapeDtypeStruct((n, s, 512), F32),
        compiler_params=_cparams(("parallel",)),
        name="nsa_sel_sample",
    )(page_table, *([cache] * n_pages), qb, selm, sel_new)


def _win_sample_kernel(q_ref, buf_ref, new_ref, o_ref, wout_ref, *, s, wb, spb):
    new_mask = _new_token_mask(R_B * s, s)
    r = _iota((R_B * s, wb), 0) % s
    buf_mask = _iota((R_B * s, wb), 1) > r + (wb - WINDOW)
    scale = DH_B ** -0.5
    keep = (wb - s) * SLOTS
    for b in range(spb):
        for g in range(G_B):
            kcols = slice(g * LANES, (g + 1) * LANES)
            vcols = slice((G_B + g) * LANES, (G_B + g + 1) * LANES)
            q2 = jnp.concatenate(
                [(q_ref[b, :, (g * R_B + r_) * LANES:(g * R_B + r_ + 1) * LANES] * scale).astype(BF16)
                 for r_ in range(R_B)], axis=0)
            kbuf = buf_ref[b, pl.ds(g, wb, stride=SLOTS), :].astype(BF16)
            vbuf = buf_ref[b, pl.ds(G_B + g, wb, stride=SLOTS), :].astype(BF16)
            knew = _pad_rows(new_ref[b, :, kcols], LANES).astype(BF16)
            vnew = _pad_rows(new_ref[b, :, vcols], LANES).astype(BF16)
            o2 = _attend_pieces(q2, [(kbuf, vbuf, buf_mask), (knew, vnew, new_mask)])
            for r_ in range(R_B):
                o_ref[b, :, (g * R_B + r_) * LANES:(g * R_B + r_ + 1) * LANES] = o2[r_ * s:(r_ + 1) * s]
        wout_ref[b, 0:keep] = buf_ref[b, s * SLOTS:wb * SLOTS]
        for slot in range(SLOTS):
            wout_ref[b, pl.ds(keep + slot, s, stride=SLOTS), :] = new_ref[b, :, slot * LANES:(slot + 1) * LANES]


def _win_attn_sample(qb, win_buf, win_new, past):
    n, s, _ = qb.shape
    wb = win_buf.shape[1] // SLOTS
    assert past >= wb and wb % 8 == 0 and s % 8 == 0
    spb = math.gcd(n, 4)
    kern = functools.partial(_win_sample_kernel, s=s, wb=wb, spb=spb)
    row_spec = pl.BlockSpec((spb, s, 512), lambda b: (b, 0, 0))
    buf_spec = pl.BlockSpec((spb, wb * SLOTS, LANES), lambda b: (b, 0, 0))
    return pl.pallas_call(
        kern,
        grid=(n // spb,),
        in_specs=[row_spec, buf_spec, row_spec],
        out_specs=[row_spec, buf_spec],
        out_shape=[jax.ShapeDtypeStruct((n, s, 512), F32), jax.ShapeDtypeStruct((n, wb * SLOTS, LANES), F32)],
        compiler_params=_cparams(("parallel",)),
        name="nsa_win_sample",
    )(qb, win_buf, win_new)


def _mlp_value(y, sh, sc, g, nw, w1_ref, w2_ref, fw, final):
    h = _norm_mod(y, nw, sh, sc).reshape(-1, D_MODEL).astype(BF16)
    a = jnp.maximum(_dot(h, w1_ref[...]), 0.0)
    out = _dot((a * a).astype(BF16), w2_ref[...])
    y2 = y + g * out.reshape(y.shape)
    return _rms(y2, fw) if final else y2


def _mlp_specs(til, mlp):
    nw, w1, w2, fw, _ = mlp
    specs = [til.mod_spec(3), til.mod_spec(4), til.mod_spec(5), _full_spec((1, D_MODEL)),
             _const_spec(w1.shape), _const_spec(w2.shape), _full_spec((1, D_MODEL))]
    return specs, [nw, w1, w2, fw]


def _even_out_kernel(y_ref, g1_ref, oa_ref, oc_ref, os_ref, ow_ref, gt_ref, w_ref,
                     sh_ref, sc_ref, g2_ref, nw_ref, w1_ref, w2_ref, fw_ref, o_ref, *, final):
    gt = gt_ref[...]
    parts = [oa_ref[...].astype(BF16)]
    for hb in range(H_B):
        sl = slice(hb * LANES, (hb + 1) * LANES)
        ob = (gt[:, 3 * hb:3 * hb + 1] * oc_ref[:, sl] + gt[:, 3 * hb + 1:3 * hb + 2] * os_ref[:, sl]
              + gt[:, 3 * hb + 2:3 * hb + 3] * ow_ref[:, sl])
        parts.append(ob.astype(BF16))
    out = _dot(jnp.concatenate(parts, axis=-1), w_ref[...])
    y = y_ref[...]
    y1 = y + g1_ref[0] * out.reshape(y.shape)
    o_ref[...] = _mlp_value(y1, sh_ref[0], sc_ref[0], g2_ref[0], nw_ref[...], w1_ref, w2_ref, fw_ref[...], final)


def _even_out(til, y, mods, oa, oc, os_, ow, gt, w_out, mlp):
    mlp_specs, mlp_args = _mlp_specs(til, mlp)
    return pl.pallas_call(
        functools.partial(_even_out_kernel, final=mlp[4]),
        grid=til.grid,
        in_specs=[til.x_spec(D_MODEL), til.mod_spec(2), til.flat_spec(512), til.flat_spec(512),
                  til.flat_spec(512), til.flat_spec(512), til.flat_spec(LANES), _const_spec(w_out.shape)] + mlp_specs,
        out_specs=til.x_spec(D_MODEL),
        out_shape=jax.ShapeDtypeStruct(y.shape, F32),
        compiler_params=_cparams(("parallel", "parallel")),
        name="even_out_mlp",
    )(y, mods, oa, oc, os_, ow, gt, w_out, mods, mods, mods, *mlp_args)


def _gla_inproj_kernel(x_ref, sh_ref, sc_ref, nw_ref, w_ref, wgl_ref, wgate_ref, bgate_ref,
                       q_ref, k_ref, v_ref, r_ref, la_ref):
    h = _norm_mod(x_ref[...], nw_ref[...], sh_ref[0], sc_ref[0]).reshape(-1, D_MODEL).astype(BF16)
    nk = H_C * DK_C
    nv = H_C * DV_C
    q_ref[...] = _dot(h, w_ref[:, 0:nk]) * (DK_C ** -0.5)
    k_ref[...] = _dot(h, w_ref[:, nk:2 * nk])
    v_ref[...] = _dot(h, w_ref[:, 2 * nk:2 * nk + nv])
    r_ref[...] = _dot(h, w_ref[:, 2 * nk + nv:2 * nk + 2 * nv])
    gl = _dot(h, wgl_ref[...])
    x = _dot(gl.astype(BF16), wgate_ref[...]) + bgate_ref[...]
    log_sig = jnp.minimum(x, 0.0) - jnp.log1p(jnp.exp(-jnp.abs(x)))
    la_ref[...] = log_sig / GATE_TAU


def _gla_inproj(til, x, mods, nw, w_main, w_gl, w_gate, b_gate):
    rows_total = til.nb * til.r
    widths = (H_C * DK_C, H_C * DK_C, H_C * DV_C, H_C * DV_C, H_C * DK_C)
    return pl.pallas_call(
        _gla_inproj_kernel,
        grid=til.grid,
        in_specs=[til.x_spec(D_MODEL), til.mod_spec(0), til.mod_spec(1), _full_spec((1, D_MODEL)),
                  _full_spec(w_main.shape), _full_spec(w_gl.shape), _full_spec(w_gate.shape),
                  _full_spec(b_gate.shape)],
        out_specs=[til.flat_spec(c) for c in widths],
        out_shape=[jax.ShapeDtypeStruct((rows_total, c), F32) for c in widths],
        compiler_params=_cparams(("parallel", "parallel")),
        name="gla_inproj",
    )(x, mods, mods, nw, w_main, w_gl, w_gate, b_gate)


def _cumsum_rows(g):
    c = g.shape[0]
    row = _iota((c, 1), 0)
    b = g
    shift = 1
    while shift < c:
        b = b + jnp.where(row >= shift, pltpu.roll(b, shift, 0), 0.0)
        shift *= 2
    return b


def _gla_prep(q, k, g, c, sub, n_real):
    b = _cumsum_rows(g)
    qe = (q * jnp.exp(b)).astype(BF16)
    lane_c = _iota((sub, c), 1)
    row_s = _iota((sub, c), 0)
    att_rows = []
    for blk in range(c // sub):
        lo = blk * sub
        qi, ki, bi = q[lo:lo + sub], k[lo:lo + sub], b[lo:lo + sub]
        diag = jnp.zeros((sub, c), F32)
        for jj in range(min(sub, max(n_real - lo, 0))):
            e = jnp.exp(jnp.minimum(bi - bi[jj:jj + 1], 0.0))
            col = jnp.sum(qi * ki[jj:jj + 1] * e, axis=-1, keepdims=True)
            diag = jnp.where(lane_c == lo + jj, col, diag)
        att = jnp.where(lane_c - lo <= row_s, diag, 0.0)
        if blk > 0:
            bs = b[lo - 1:lo]
            q_in = qi * jnp.exp(bi - bs)
            k_out = k * jnp.exp(jnp.minimum(bs - b, 0.0))
            att = jnp.where(lane_c < lo, _dot_nt(q_in.astype(BF16), k_out.astype(BF16)), att)
        att_rows.append(att)
    att = att_rows[0] if len(att_rows) == 1 else jnp.concatenate(att_rows, axis=0)
    bl = b[c - 1:c]
    kd = (k * jnp.exp(bl - b)).astype(BF16)
    eye = _iota((DK_C, DK_C), 0) == _iota((DK_C, DK_C), 1)
    decay = jnp.sum(jnp.where(eye, jnp.exp(bl), 0.0), axis=-1, keepdims=True)
    return qe, att, kd, decay


def _gla_prep_bounded(q, k, g, c, sub, n_real):
    del sub, n_real
    b = _cumsum_rows(g)
    r = b[0:1]
    bl = b[c - 1:c]
    q_in = q * jnp.exp(b - r)
    k_out = k * jnp.exp(r - b)
    att = jnp.where(_iota((c, c), 0) >= _iota((c, c), 1), _dot_nt(q_in.astype(BF16), k_out.astype(BF16)), 0.0)
    qe = (q_in * jnp.exp(r)).astype(BF16)
    kd = (k_out * jnp.exp(bl - r)).astype(BF16)
    eye = _iota((DK_C, DK_C), 0) == _iota((DK_C, DK_C), 1)
    decay = jnp.sum(jnp.where(eye, jnp.exp(bl), 0.0), axis=-1, keepdims=True)
    return qe, att, kd, decay


def _gla_apply(state, prep, v):
    qe, att, kd, decay = prep
    o = _dot(qe, state.astype(BF16)) + _dot(att, v)
    return o, decay * state + _dot_tn(kd, v.astype(BF16))


def _gla_rec_kernel(*refs, tt, c, sub, hp, nseq, has_s0):
    if has_s0:
        q_ref, k_ref, v_ref, g_ref, s0_ref, o_ref, sfin_ref, s_ref = refs
    else:
        q_ref, k_ref, v_ref, g_ref, o_ref, sfin_ref, s_ref = refs
    t = pl.program_id(2)

    @pl.when(t == 0)
    def _():
        for bi in range(nseq):
            for hh in range(hp):
                s_ref[bi * hp + hh] = s0_ref[bi, hh] if has_s0 else jnp.zeros((DK_C, DV_C), F32)

    def kcols(hh):
        return slice(hh * DK_C, (hh + 1) * DK_C)

    def vcols(hh):
        return slice(hh * DV_C, (hh + 1) * DV_C)

    def run(prep_fn):
        if tt < c:
            pad = lambda x: _pad_rows(x, c)
            for bi in range(nseq):
                for hh in range(hp):
                    prep = prep_fn(pad(q_ref[bi, :, kcols(hh)]), pad(k_ref[bi, :, kcols(hh)]),
                                   pad(g_ref[bi, :, kcols(hh)]), c, sub, tt)
                    si = bi * hp + hh
                    o, s_ref[si] = _gla_apply(s_ref[si], prep, pad(v_ref[bi, :, vcols(hh)]))
                    o_ref[bi, :, vcols(hh)] = o[:tt]
            return
        assert nseq == 1
        per_trip = 2 if (tt // c) % 2 == 0 else 1

        def body(ci, carry):
            for hh in range(hp):
                rows = [pl.ds(pl.multiple_of((ci * per_trip + u) * c, c), c) for u in range(per_trip)]
                preps = [prep_fn(q_ref[0, r, kcols(hh)], k_ref[0, r, kcols(hh)], g_ref[0, r, kcols(hh)],
                                 c, sub, c) for r in rows]
                state = s_ref[hh]
                for r, prep in zip(rows, preps):
                    o_ref[0, r, vcols(hh)], state = _gla_apply(state, prep, v_ref[0, r, vcols(hh)])
                s_ref[hh] = state
            return carry
        lax.fori_loop(0, tt // c // per_trip, body, 0)

    if tt > c:
        g_all = g_ref[0]
        chunk_decay = -jnp.sum(g_all.reshape(tt // c, c, g_all.shape[-1]), axis=1)
    else:
        chunk_decay = -jnp.sum(g_ref[...], axis=1)
    bounded = jnp.max(chunk_decay) <= GLA_SAFE_DECAY

    @pl.when(bounded)
    def _():
        run(_gla_prep_bounded)

    @pl.when(jnp.logical_not(bounded))
    def _():
        run(_gla_prep)

    @pl.when(t == pl.num_programs(2) - 1)
    def _():
        for bi in range(nseq):
            for hh in range(hp):
                sfin_ref[bi, hh] = s_ref[bi * hp + hh]


def _gla_recurrence(q, k, v, g, s0, tt, c, sub, hp):
    n, t, _ = q.shape
    nseq = math.gcd(n, 2) if tt < c else 1
    kern = functools.partial(_gla_rec_kernel, tt=tt, c=c, sub=sub, hp=hp, nseq=nseq, has_s0=s0 is not None)
    kspec = pl.BlockSpec((nseq, tt, hp * DK_C), lambda b, h, i: (b, i, h))
    vspec = pl.BlockSpec((nseq, tt, hp * DV_C), lambda b, h, i: (b, i, h))
    sspec = pl.BlockSpec((nseq, hp, DK_C, DV_C), lambda b, h, i: (b, h, 0, 0))
    in_specs = [kspec, kspec, vspec, kspec]
    args = [q, k, v, g]
    if s0 is not None:
        in_specs.append(sspec)
        args.append(s0)
    return pl.pallas_call(
        kern,
        grid=(n // nseq, H_C // hp, t // tt),
        in_specs=in_specs,
        out_specs=[vspec, sspec],
        out_shape=[jax.ShapeDtypeStruct((n, t, H_C * DV_C), F32),
                   jax.ShapeDtypeStruct((n, H_C, DK_C, DV_C), F32)],
        scratch_shapes=[pltpu.VMEM((nseq * hp, DK_C, DV_C), F32)],
        compiler_params=_cparams(("parallel", "parallel", "arbitrary")),
        name="gla_recurrence",
    )(*args)


def _gla_out_kernel(y_ref, g1_ref, o_ref_in, r_ref, gnw_ref, w_ref,
                    sh_ref, sc_ref, g2_ref, nw_ref, w1_ref, w2_ref, fw_ref, out_ref, *, final):
    parts = []
    for h in range(H_C):
        sl = slice(h * DV_C, (h + 1) * DV_C)
        r = r_ref[:, sl]
        parts.append((_rms(o_ref_in[:, sl], gnw_ref[...]) * (r * jax.nn.sigmoid(r))).astype(BF16))
    out = _dot(jnp.concatenate(parts, axis=-1), w_ref[...])
    y = y_ref[...]
    y1 = y + g1_ref[0] * out.reshape(y.shape)
    out_ref[...] = _mlp_value(y1, sh_ref[0], sc_ref[0], g2_ref[0], nw_ref[...], w1_ref, w2_ref, fw_ref[...], final)


def _gla_out(til, y, mods, o, r, gnw, w_out, mlp):
    mlp_specs, mlp_args = _mlp_specs(til, mlp)
    return pl.pallas_call(
        functools.partial(_gla_out_kernel, final=mlp[4]),
        grid=til.grid,
        in_specs=[til.x_spec(D_MODEL), til.mod_spec(2), til.flat_spec(H_C * DV_C), til.flat_spec(H_C * DV_C),
                  _full_spec(gnw.shape), _const_spec(w_out.shape)] + mlp_specs,
        out_specs=til.x_spec(D_MODEL),
        out_shape=jax.ShapeDtypeStruct(y.shape, F32),
        compiler_params=_cparams(("parallel", "parallel")),
        name="gla_out_mlp",
    )(y, mods, o, r, gnw, w_out, mods, mods, mods, *mlp_args)


def _rope_tables(pos, d):
    inv = ROPE_THETA ** (-jnp.arange(0, d, 2, dtype=F32) / d)
    ang = pos.astype(F32)[:, None] * inv[None, :]
    cos, sin = jnp.cos(ang), jnp.sin(ang)
    rep = LANES // d
    c = jnp.tile(jnp.concatenate([cos, cos], axis=-1), (1, rep))
    s = jnp.tile(jnp.concatenate([-sin, sin], axis=-1), (1, rep))
    return c, s


def _rope_tables_t(pos, d):
    inv = ROPE_THETA ** (-jnp.arange(0, d, 2, dtype=F32) / d)
    ang = inv[:, None] * pos.astype(F32)[None, :]
    return jnp.cos(ang), jnp.sin(ang)


def _mods_for(mod_l, lo, hi):
    nb = hi - lo
    return mod_l[lo:hi].reshape(nb, 6, D_MODEL).transpose(1, 0, 2).reshape(6, nb, 1, D_MODEL)


def _prompt_tile_rows(t):
    return math.gcd(t, 256)


def _even_layer(yp, ys, mods_p, mods_s, caches, page_table, wts, lam_init, til_p, til_s, mlp):
    (c_dk, c_dv, c_cmp, c_sel, win_buf) = caches
    n, t, _ = yp.shape
    ns, s, _ = ys.shape
    n_pages = page_table.shape[1]
    past = n_pages * PAGE_SIZE
    w_in = wts["w_in"]
    n_main = w_in.shape[1] - 3 * H_B
    w_main = w_in[:, :n_main].astype(BF16)
    w_gate = jnp.pad(w_in[:, n_main:], ((0, 0), (0, LANES - 3 * H_B))).astype(BF16)
    nw1 = wts["norm1"].reshape(1, D_MODEL)
    lam_w = wts["lam_w"]
    sub_w = wts["sub_w"].reshape(1, DV_A)
    wc, pe, w2c = _compress_weights(wts["cmp_pe"], wts["cmp_w1"], wts["cmp_w2"])
    w_out = wts["w_out"].astype(BF16)

    tabs_p = _rope_tables(jnp.arange(t), DK_A) + _rope_tables(jnp.arange(t), DH_B)
    kt_tabs = _rope_tables_t(jnp.arange(t), DK_A)
    qa, kat, va, qb, cmp_kv, sel_kv, win_kv, gt = _even_inproj(til_p, yp, mods_p, nw1, w_main, w_gate, tabs_p, kt_tabs)
    r3 = lambda a: a.reshape(n, t, a.shape[-1])
    tq = _prompt_tile_rows(t)
    tq_attn = math.gcd(t, 2 * tq)
    oa = _diff_attn_prompt(r3(qa), kat, r3(va), lam_w, sub_w, lam_init, tq_attn, tq)
    ccmp = _compress_prompt(r3(cmp_kv), wc, pe, w2c)
    n_cmp = (t - CMP_LEN) // CMP_STRIDE + 1
    n_sel = -(-t // SEL_BLOCK)
    o_cmp, selm = _cmp_topk(r3(qb), ccmp, nb=1, rq=tq, n_cmp=n_cmp, n_sel=n_sel, pos_base=0)
    o_sel = _nsa_attn_prompt(r3(qb), r3(sel_kv), selm, tq_attn, tq)
    o_win = _nsa_attn_prompt(r3(qb), r3(win_kv), None, tq_attn, tq)
    f2 = lambda a: a.reshape(n * t, a.shape[-1])
    yp = _even_out(til_p, yp, mods_p, f2(oa), f2(o_cmp), f2(o_sel), f2(o_win), gt, w_out, mlp)
    wl = min(WINDOW, t)
    ka_state = jnp.transpose(kat.reshape(n, H_A, 2, DK_A, t), (0, 4, 1, 2, 3))
    st_p = (ka_state, r3(va).reshape(n, t, H_A, DV_A),
            r3(cmp_kv).reshape(n, t, 2, G_B, DH_B), r3(sel_kv).reshape(n, t, 2, G_B, DH_B),
            r3(win_kv)[:, t - wl:].reshape(n, wl, 2, G_B, DH_B))

    pos_s = past + jnp.arange(s)
    tabs_s = tuple(jnp.tile(x, (til_s.b, 1)) for x in _rope_tables(pos_s, DK_A) + _rope_tables(pos_s, DH_B))
    qa, ka, va, qb, cmp_kv, sel_kv, win_kv, gt = _even_inproj(til_s, ys, mods_s, nw1, w_main, w_gate, tabs_s)
    r3 = lambda a: a.reshape(ns, s, a.shape[-1])
    slot_cache = lambda c: c.reshape(c.shape[0], PAGE_SIZE * SLOTS, LANES)
    c_dkt = jnp.transpose(c_dk, (0, 2, 3, 4, 1)).reshape(c_dk.shape[0], H_A * 2 * DK_A, PAGE_SIZE)
    oa = _diff_attn_sample(r3(qa), r3(ka), r3(va), c_dkt, slot_cache(c_dv), page_table, lam_w, sub_w, lam_init)
    ccmp = _compress_sample(slot_cache(c_cmp), page_table, wc, pe, w2c)
    total = past + s
    n_cmp = (total - CMP_LEN) // CMP_STRIDE + 1
    n_sel = -(-total // SEL_BLOCK)
    assert n_cmp <= ccmp.shape[3] and n_sel <= LANES
    nb = math.gcd(ns, LANES // s)
    o_cmp, selm = _cmp_topk(r3(qb), ccmp, nb=nb, rq=s, n_cmp=n_cmp, n_sel=n_sel, pos_base=past)
    o_sel = _sel_attn_sample(r3(qb), selm, r3(sel_kv), slot_cache(c_sel), page_table)
    wb = win_buf.shape[1]
    o_win, win_out = _win_attn_sample(r3(qb), win_buf.reshape(ns, wb * SLOTS, LANES), r3(win_kv), past)
    f2 = lambda a: a.reshape(ns * s, a.shape[-1])
    ys = _even_out(til_s, ys, mods_s, f2(oa), f2(o_cmp), f2(o_sel), f2(o_win), gt, w_out, mlp)
    st_s = (r3(ka).reshape(ns, s, H_A, 2, DK_A), r3(va).reshape(ns, s, H_A, DV_A),
            r3(cmp_kv).reshape(ns, s, 2, G_B, DH_B), r3(sel_kv).reshape(ns, s, 2, G_B, DH_B),
            win_out.reshape(ns, wb, 2, G_B, DH_B))
    return yp, ys, st_p, st_s


def _odd_layer(yp, ys, mods_p, mods_s, s0, wts, til_p, til_s, mlp):
    w_in = wts["w_in"]
    n_main = w_in.shape[1] - GATE_RANK
    w_main = w_in[:, :n_main].astype(BF16)
    w_gl = jnp.pad(w_in[:, n_main:], ((0, 0), (0, LANES - GATE_RANK))).astype(BF16)
    w_gate = jnp.pad(wts["w_gate"], ((0, LANES - GATE_RANK), (0, 0))).astype(BF16)
    b_gate = wts["b_gate"].reshape(1, -1)
    nw1 = wts["norm1"].reshape(1, D_MODEL)
    gnw = wts["gnorm"].reshape(1, DV_C)
    w_out = wts["w_out"].astype(BF16)
    outs = []
    for y, mods, til, state in ((yp, mods_p, til_p, None), (ys, mods_s, til_s, s0)):
        n, t, _ = y.shape
        q, k, v, r, la = _gla_inproj(til, y, mods, nw1, w_main, w_gl, w_gate, b_gate)
        r3 = lambda a: a.reshape(n, t, a.shape[-1])
        c = math.gcd(t, GLA_CHUNK)
        if c >= GLA_SUB:
            tt, cc, sub = math.gcd(t, 8 * c), c, GLA_SUB_LONG
        else:
            tt, cc, sub = t, GLA_SUB, GLA_SUB
        o, s_fin = _gla_recurrence(r3(q), r3(k), r3(v), r3(la), state, tt, cc, sub, H_C)
        y = _gla_out(til, y, mods, o.reshape(n * t, -1), r, gnw, w_out, mlp)
        outs.append((y, s_fin))
    return outs[0][0], outs[1][0], outs[0][1], outs[1][1]


def kernel(x_prompt, x_sample, c_prompt, c_sample, cache_diff_k, cache_diff_v, cache_cmp_kv, cache_sel_kv,
           state_win_kv, state_gla, page_table, norm1_w, norm2_w, ada_w, ada_b, even_w_in, even_w_out,
           diff_lambda_w, diff_subln_w, cmp_pe, cmp_w1, cmp_w2, gla_w_in, gla_w_gate, gla_b_gate, gla_norm_w,
           gla_w_out, mlp_w1, mlp_w2, final_norm_w):
    depth = ada_w.shape[0]
    n, t, _ = x_prompt.shape
    ns, s, _ = x_sample.shape
    til_p = _Tiling(n, t, 1, _prompt_tile_rows(t))
    til_s = _Tiling(ns, s, math.gcd(ns, 256 // s), s)

    pad = (-(n + ns)) % 8
    c_all = jnp.concatenate([c_prompt, c_sample, jnp.zeros((pad, D_MODEL), F32)], axis=0)
    mod = _adaln(c_all, ada_w, ada_b)

    yp, ys = x_prompt, x_sample
    st_p = [[] for _ in range(6)]
    st_s = [[] for _ in range(6)]
    fw = final_norm_w.reshape(1, D_MODEL)
    for l in range(depth):
        mods_p = _mods_for(mod[l], 0, n)
        mods_s = _mods_for(mod[l], n, n + ns)
        mlp = (norm2_w[l].reshape(1, D_MODEL), mlp_w1[l].astype(BF16), mlp_w2[l].astype(BF16), fw, l == depth - 1)
        if l % 2 == 0:
            e = l // 2
            lam_init = 0.8 - 0.6 * math.exp(-0.3 * l)
            wts = dict(w_in=even_w_in[e], w_out=even_w_out[e], lam_w=diff_lambda_w[e], sub_w=diff_subln_w[e],
                       cmp_pe=cmp_pe[e], cmp_w1=cmp_w1[e], cmp_w2=cmp_w2[e], norm1=norm1_w[l])
            caches = (cache_diff_k[e], cache_diff_v[e], cache_cmp_kv[e], cache_sel_kv[e], state_win_kv[e])
            yp, ys, sp, ss = _even_layer(yp, ys, mods_p, mods_s, caches, page_table, wts, lam_init, til_p, til_s, mlp)
            for i in range(5):
                st_p[i].append(sp[i])
                st_s[i].append(ss[i])
        else:
            o = l // 2
            wts = dict(w_in=gla_w_in[o], w_gate=gla_w_gate[o], b_gate=gla_b_gate[o], gnorm=gla_norm_w[o],
                       w_out=gla_w_out[o], norm1=norm1_w[l])
            yp, ys, gp, gs = _odd_layer(yp, ys, mods_p, mods_s, state_gla[o], wts, til_p, til_s, mlp)
            st_p[5].append(gp)
            st_s[5].append(gs)
    outs_p = [jnp.stack(x, axis=0) for x in st_p]
    outs_s = [jnp.stack(x, axis=0) for x in st_s]
    return (yp, ys, *outs_p, *outs_s)
```

```python
import functools
import math

import jax
import jax.numpy as jnp
from jax import lax
from jax.experimental import pallas as pl
from jax.experimental.pallas import tpu as pltpu

F32 = jnp.float32
BF16 = jnp.bfloat16

D_MODEL = 1024
PAGE_SIZE = 128
H_A = 4
DK_A = 64
DV_A = 128
H_B = 4
G_B = 2
R_B = 2
DH_B = 128
CMP_LEN = 32
CMP_STRIDE = 16
SEL_BLOCK = 64
TOP_N = 16
WINDOW = 512
H_C = 4
DK_C = 128
DV_C = 256
GATE_RANK = 16
GATE_TAU = 16.0
GLA_CHUNK = 64
GLA_SUB = 16
GLA_SUB_LONG = 16
GLA_SAFE_DECAY = 80.0
D_FF = 4 * D_MODEL
ROPE_THETA = 10000.0
EPS = 1e-6

LANES = 128
SLOTS = 4
MASK_NEG = -1e30
LOG2E = 1.4426950408889634
SAFE_SHIFT = 56.0
SEL_NEG = 32768.0
VMEM_LIMIT_MB = 56


def _cparams(sem, vmem_mb=VMEM_LIMIT_MB):
    return pltpu.CompilerParams(dimension_semantics=sem, vmem_limit_bytes=vmem_mb * 1024 * 1024)


def _dot(a, b):
    return jnp.dot(a, b, preferred_element_type=F32)


def _dot_nt(a, b):
    return lax.dot_general(a, b, (((1,), (1,)), ((), ())), preferred_element_type=F32)


def _dot_tn(a, b):
    return lax.dot_general(a, b, (((0,), (0,)), ((), ())), preferred_element_type=F32)


def _iota(shape, dim):
    return lax.broadcasted_iota(jnp.int32, shape, dim)


def _rms(x, w):
    ms = jnp.mean(x * x, axis=-1, keepdims=True)
    return x * lax.rsqrt(ms + EPS) * w


def _norm_mod(x, nw, shift, scale):
    return _rms(x, nw) * (1.0 + scale) + shift


class _Tiling:
    def __init__(self, nb, r, b, rt):
        assert nb % b == 0 and r % rt == 0 and (b == 1 or rt == r)
        self.nb, self.r, self.b, self.rt = nb, r, b, rt
        self.grid = (nb // b, r // rt)
        self.rows = b * rt
        self.nrb = r // rt

    def x_spec(self, d):
        return pl.BlockSpec((self.b, self.rt, d), lambda i, j: (i, j, 0))

    def mod_spec(self, k):
        return pl.BlockSpec((1, self.b, 1, D_MODEL), lambda i, j: (k, i, 0, 0))

    def flat_spec(self, c):
        nrb = self.nrb
        return pl.BlockSpec((self.rows, c), lambda i, j: (i * nrb + j, 0))

    def tab_spec(self):
        return pl.BlockSpec((self.rows, LANES), lambda i, j: (j, 0))


def _full_spec(shape):
    nd = len(shape)
    return pl.BlockSpec(shape, lambda *_: (0,) * nd)


def _const_spec(shape):
    nd = len(shape)
    return pl.BlockSpec(shape, lambda *_: (0,) * nd, pipeline_mode=pl.Buffered(1))


def _adaln_kernel(c_ref, w_ref, b_ref, o_ref):
    c = c_ref[...]
    a = (c * jax.nn.sigmoid(c)).astype(BF16)
    o_ref[0] = _dot(a, w_ref[0].astype(BF16)) + b_ref[0]


def _adaln(c_all, ada_w, ada_b):
    depth, d, n6 = ada_w.shape
    rows = c_all.shape[0]
    tn = 1536
    return pl.pallas_call(
        _adaln_kernel,
        grid=(depth, n6 // tn),
        in_specs=[pl.BlockSpec((rows, d), lambda l, j: (0, 0)),
                  pl.BlockSpec((1, d, tn), lambda l, j: (l, 0, j)),
                  pl.BlockSpec((1, 1, tn), lambda l, j: (l, 0, j))],
        out_specs=pl.BlockSpec((1, rows, tn), lambda l, j: (l, 0, j)),
        out_shape=jax.ShapeDtypeStruct((depth, rows, n6), F32),
        compiler_params=_cparams(("parallel", "parallel")),
        name="adaln",
    )(c_all, ada_w, ada_b.reshape(depth, 1, n6))


def _swap_half(x, half):
    if 2 * half == LANES:
        return pltpu.roll(x, half, 1)
    lane = _iota((1, LANES), 1)
    lo = (lane % (2 * half)) < half
    return jnp.where(lo, pltpu.roll(x, LANES - half, 1), pltpu.roll(x, half, 1))


def _even_inproj_kernel(*refs, k_feature_major):
    if k_feature_major:
        (x_ref, sh_ref, sc_ref, nw_ref, w_ref, wg_ref, c64_ref, s64_ref, c128_ref, s128_ref, wkt_ref, ct_ref, st_ref,
         qa_ref, ka_ref, va_ref, qb_ref, cmp_ref, sel_ref, win_ref, gt_ref) = refs
    else:
        (x_ref, sh_ref, sc_ref, nw_ref, w_ref, wg_ref, c64_ref, s64_ref, c128_ref, s128_ref,
         qa_ref, ka_ref, va_ref, qb_ref, cmp_ref, sel_ref, win_ref, gt_ref) = refs
    h = _norm_mod(x_ref[...], nw_ref[...], sh_ref[0], sc_ref[0])
    h = h.reshape(-1, D_MODEL).astype(BF16)
    c64, s64, c128, s128 = c64_ref[...], s64_ref[...], c128_ref[...], s128_ref[...]

    def rope64(p):
        return p * c64 + _swap_half(p, DK_A // 2) * s64

    def rope128(p):
        return p * c128 + _swap_half(p, DH_B // 2) * s128

    def project(ref, off, ropes):
        p = _dot(h, w_ref[:, off:off + 4 * LANES])
        for j, rope in enumerate(ropes):
            sl = slice(j * LANES, (j + 1) * LANES)
            ref[:, sl] = p[:, sl] if rope is None else rope(p[:, sl])

    project(qa_ref, 0, [rope64] * 4)
    if k_feature_major:
        kt = _dot_nt(wkt_ref[...], h)
        ct, st = ct_ref[...], st_ref[...]
        half = DK_A // 2
        for grp in range(H_A * 2):
            x1 = kt[grp * DK_A:grp * DK_A + half]
            x2 = kt[grp * DK_A + half:(grp + 1) * DK_A]
            ka_ref[0, grp * DK_A:grp * DK_A + half, :] = x1 * ct - x2 * st
            ka_ref[0, grp * DK_A + half:(grp + 1) * DK_A, :] = x2 * ct + x1 * st
    else:
        project(ka_ref, 512, [rope64] * 4)
    project(va_ref, 1024, [None] * 4)
    project(qb_ref, 1536, [rope128] * 4)
    for t, ref in enumerate((cmp_ref, sel_ref, win_ref)):
        project(ref, 2048 + t * 512, [rope128, rope128, None, None])
    gt_ref[...] = jax.nn.sigmoid(_dot(h, wg_ref[...]))


def _even_inproj(til, x, mods, nw, w_main, w_gate, tabs, kt_tabs=None):
    rows_total = til.nb * til.r
    widths = (512, 512, 512, 512, 512, 512, 512, LANES)
    in_specs = [til.x_spec(D_MODEL), til.mod_spec(0), til.mod_spec(1), _full_spec((1, D_MODEL)),
                _const_spec(w_main.shape), _const_spec(w_gate.shape)] + [til.tab_spec()] * 4
    args = [x, mods, mods, nw, w_main, w_gate, *tabs]
    out_specs = [til.flat_spec(c) for c in widths]
    out_shape = [jax.ShapeDtypeStruct((rows_total, c), F32) for c in widths]
    if kt_tabs is not None:
        assert til.b == 1
        w_kt = jnp.transpose(w_main[:, 512:1024])
        in_specs += [_const_spec(w_kt.shape)] + [pl.BlockSpec((DK_A // 2, til.rt), lambda i, j: (0, j))] * 2
        args += [w_kt, *kt_tabs]
        out_specs[1] = pl.BlockSpec((1, 512, til.rt), lambda i, j: (i, 0, j))
        out_shape[1] = jax.ShapeDtypeStruct((til.nb, 512, til.r), F32)
    return pl.pallas_call(
        functools.partial(_even_inproj_kernel, k_feature_major=kt_tabs is not None),
        grid=til.grid,
        in_specs=in_specs,
        out_specs=out_specs,
        out_shape=out_shape,
        compiler_params=_cparams(("parallel", "parallel")),
        name="even_inproj",
    )(*args)


def _for_tiles(lo, hi, fn):
    n = hi - lo

    def pair(j, carry):
        fn(lo + 2 * j)
        fn(lo + 2 * j + 1)
        return carry

    lax.fori_loop(0, n // 2, pair, 0)

    @pl.when(n % 2 == 1)
    def _():
        fn(hi - 1)


def _row_sumsq(x):
    xf = x.astype(F32)
    return _dot((xf * xf).astype(BF16), jnp.ones((LANES, LANES), BF16))


def _key_norm_bound(k_ref, n_keys, tile):
    def body(kb, m):
        k = k_ref[0, pl.ds(pl.multiple_of(kb * tile, tile), tile), :].astype(BF16)
        return jnp.maximum(m, _row_sumsq(k))

    m = lax.fori_loop(0, n_keys // tile, body, jnp.zeros((tile, LANES), F32))
    return jnp.max(m, axis=0, keepdims=True)


def _score_bound(q2, kmax2):
    qmax2 = jnp.max(_row_sumsq(q2), axis=0, keepdims=True)
    return jnp.sqrt(qmax2 * kmax2) * 1.05


def _two_pass_attention(mx_ref, acc_ref, lo, hi, last, scores, values, mask_body=False, bound=None):
    def lane_max(s):
        m = s[:, 0:LANES]
        for c in range(1, s.shape[1] // LANES):
            m = jnp.maximum(m, s[:, c * LANES:(c + 1) * LANES])
        return m

    def pass1(kb, masked):
        mx_ref[...] = jnp.maximum(mx_ref[...], lane_max(scores(kb, masked)))

    def exact_max():
        mx_ref[...] = jnp.full(mx_ref.shape, MASK_NEG, F32)
        _for_tiles(lo, hi, lambda kb: pass1(kb, mask_body))
        for kb in last:
            pass1(kb, True)
        mx_ref[...] = jnp.broadcast_to(jnp.max(mx_ref[...], axis=-1, keepdims=True), mx_ref.shape)

    if bound is None:
        exact_max()
    else:
        safe = jnp.max(bound) <= SAFE_SHIFT

        @pl.when(safe)
        def _():
            mx_ref[...] = jnp.broadcast_to(bound, mx_ref.shape)

        @pl.when(jnp.logical_not(safe))
        def _():
            exact_max()

    acc_ref[...] = jnp.zeros(acc_ref.shape, F32)

    def pass2(kb, masked):
        s = scores(kb, masked)
        m = mx_ref[...]
        p = jnp.concatenate([jnp.exp2(s[:, c * LANES:(c + 1) * LANES] - m) for c in range(s.shape[1] // LANES)],
                            axis=-1).astype(BF16)
        v = values(kb)
        v1 = jnp.concatenate([v, jnp.ones(v.shape, BF16)], axis=-1)
        acc_ref[...] += _dot(p, v1)

    _for_tiles(lo, hi, lambda kb: pass2(kb, mask_body))
    for kb in last:
        pass2(kb, True)
    return acc_ref[:, 0:LANES] / acc_ref[:, LANES:2 * LANES]


def _diff_lambda(lw, lam_init):
    a = jnp.sum(lw[0:1] * lw[1:2], axis=-1, keepdims=True)
    b = jnp.sum(lw[2:3] * lw[3:4], axis=-1, keepdims=True)
    return jnp.exp(a) - jnp.exp(b) + lam_init


def _diff_finalize(o2, tq, lam, sub_w, lam_init):
    od = o2[:tq] - lam * o2[tq:]
    return _rms(od, sub_w) * (1.0 - lam_init)


def _diff_flash_kernel(q_ref, k_ref, v_ref, lamw_ref, sub_ref, o_ref, q2_ref, mx_ref, acc_ref, kmax_ref, *,
                       tq, tk, lam_init):
    i = pl.program_id(2)

    def key_tile(kb):
        return k_ref[0, :, pl.ds(pl.multiple_of(kb * tk, tk), tk)].astype(BF16)

    @pl.when(i == 0)
    def _():
        def body(kb, m):
            kf = key_tile(kb).astype(F32)
            return jnp.maximum(m, jnp.sum(kf * kf, axis=0, keepdims=True))

        m = lax.fori_loop(0, k_ref.shape[2] // tk, body, jnp.zeros((1, tk), F32))
        kmax_ref[...] = jnp.broadcast_to(jnp.max(m, axis=-1, keepdims=True), kmax_ref.shape)

    q = q_ref[0] * (DK_A ** -0.5 * LOG2E)
    lane = _iota((1, LANES), 1)
    q2_ref[0:tq] = jnp.where(lane < DK_A, q, 0.0).astype(BF16)
    q2_ref[tq:2 * tq] = jnp.where(lane >= DK_A, q, 0.0).astype(BF16)
    bound = _score_bound(q2_ref[...], kmax_ref[0:1])

    def scores(kb, masked):
        s = _dot(q2_ref[...], key_tile(kb))
        if masked:
            r = _iota((2 * tq, tk), 0)
            qp = i * tq + jnp.where(r >= tq, r - tq, r)
            s = jnp.where(kb * tk + _iota((2 * tq, tk), 1) <= qp, s, MASK_NEG)
        return s

    def values(kb):
        return v_ref[0, pl.ds(pl.multiple_of(kb * tk, tk), tk), :].astype(BF16)

    per = tq // tk
    o2 = _two_pass_attention(mx_ref, acc_ref, 0, i * per, [i * per + u for u in range(per)], scores, values,
                             bound=bound)
    o_ref[0] = _diff_finalize(o2, tq, _diff_lambda(lamw_ref[...], lam_init), sub_ref[...], lam_init)


def _diff_attn_prompt(qa, kat, va, lam_w, sub_w, lam_init, tq, tk):
    n, t, _ = qa.shape
    kern = functools.partial(_diff_flash_kernel, tq=tq, tk=tk, lam_init=lam_init)
    return pl.pallas_call(
        kern,
        grid=(n, H_A, t // tq),
        in_specs=[pl.BlockSpec((1, tq, LANES), lambda b, h, i: (b, i, h)),
                  pl.BlockSpec((1, LANES, t), lambda b, h, i: (b, h, 0)),
                  pl.BlockSpec((1, t, LANES), lambda b, h, i: (b, 0, h)),
                  _full_spec(lam_w.shape), _full_spec(sub_w.shape)],
        out_specs=pl.BlockSpec((1, tq, LANES), lambda b, h, i: (b, i, h)),
        out_shape=jax.ShapeDtypeStruct((n, t, H_A * DV_A), F32),
        scratch_shapes=[pltpu.VMEM((2 * tq, LANES), BF16), pltpu.VMEM((2 * tq, LANES), F32),
                        pltpu.VMEM((2 * tq, 2 * LANES), F32), pltpu.VMEM((8, LANES), F32)],
        compiler_params=_cparams(("parallel", "parallel", "arbitrary")),
        name="diff_attn_prompt",
    )(qa, kat, va, lam_w, sub_w)


def _sel_onehot(rows, first_block):
    blk = _iota((rows, LANES), 0) // SEL_BLOCK + first_block
    return jnp.where(blk == _iota((rows, LANES), 1), 1.0, 0.0).astype(BF16)


def _nsa_flash_kernel(*refs, tq, tk, use_sel):
    if use_sel:
        q_ref, selm_ref, k_ref, v_ref, o_ref, q2_ref, mx_ref, acc_ref, kmax_ref = refs
    else:
        q_ref, k_ref, v_ref, o_ref, q2_ref, mx_ref, acc_ref, kmax_ref = refs
    i = pl.program_id(2)

    @pl.when(i == 0)
    def _():
        kmax_ref[...] = jnp.broadcast_to(_key_norm_bound(k_ref, k_ref.shape[1], tk), kmax_ref.shape)

    q = q_ref[0] * (DH_B ** -0.5 * LOG2E)
    for r in range(R_B):
        q2_ref[r * tq:(r + 1) * tq, 0:LANES] = q[:, r * LANES:(r + 1) * LANES].astype(BF16)
        if use_sel:
            q2_ref[r * tq:(r + 1) * tq, LANES:2 * LANES] = selm_ref[0, 0]
    bound = _score_bound(q2_ref[:, 0:LANES], kmax_ref[0:1])

    def scores(kb, masked):
        k = k_ref[0, pl.ds(pl.multiple_of(kb * tk, tk), tk), :].astype(BF16)
        if use_sel:
            k = jnp.concatenate([k, _sel_onehot(tk, kb * (tk // SEL_BLOCK))], axis=-1)
        s = _dot_nt(q2_ref[...], k)
        if masked:
            r = _iota((R_B * tq, tk), 0)
            qp = i * tq + jnp.where(r >= tq, r - tq, r)
            kp = kb * tk + _iota((R_B * tq, tk), 1)
            ok = kp <= qp
            if not use_sel:
                ok = ok & (kp > qp - WINDOW)
            s = jnp.where(ok, s, MASK_NEG)
        return s

    def values(kb):
        return v_ref[0, pl.ds(pl.multiple_of(kb * tk, tk), tk), :].astype(BF16)

    per = tq // tk
    lo = 0 if use_sel else jnp.maximum(i * per - WINDOW // tk, 0)
    o = _two_pass_attention(mx_ref, acc_ref, lo, i * per, [i * per + u for u in range(per)], scores, values,
                            mask_body=not use_sel, bound=bound)
    for r in range(R_B):
        o_ref[0, :, r * LANES:(r + 1) * LANES] = o[r * tq:(r + 1) * tq]


def _nsa_attn_prompt(qb, kv, selm, tq, tk):
    n, t, _ = qb.shape
    use_sel = selm is not None
    kd = 2 * LANES if use_sel else LANES
    kern = functools.partial(_nsa_flash_kernel, tq=tq, tk=tk, use_sel=use_sel)
    in_specs = [pl.BlockSpec((1, tq, R_B * LANES), lambda b, g, i: (b, i, g))]
    args = [qb]
    if use_sel:
        in_specs.append(pl.BlockSpec((1, 1, tq, LANES), lambda b, g, i: (b, g, i, 0)))
        args.append(selm)
    in_specs += [pl.BlockSpec((1, t, LANES), lambda b, g, i: (b, 0, g)),
                 pl.BlockSpec((1, t, LANES), lambda b, g, i: (b, 0, G_B + g))]
    args += [kv, kv]
    return pl.pallas_call(
        kern,
        grid=(n, G_B, t // tq),
        in_specs=in_specs,
        out_specs=pl.BlockSpec((1, tq, R_B * LANES), lambda b, g, i: (b, i, g)),
        out_shape=jax.ShapeDtypeStruct((n, t, H_B * DH_B), F32),
        scratch_shapes=[pltpu.VMEM((R_B * tq, kd), BF16), pltpu.VMEM((R_B * tq, LANES), F32),
                        pltpu.VMEM((R_B * tq, 2 * LANES), F32), pltpu.VMEM((8, LANES), F32)],
        compiler_params=_cparams(("parallel", "parallel", "arbitrary")),
        name="nsa_sel_prompt" if use_sel else "nsa_win_prompt",
    )(*args)


def _compress_core(load, wc_ref, pe_ref, w2_ref, wi, nchunk):
    rows = G_B * nchunk
    acc = jnp.zeros((rows, 2 * LANES), F32)
    pew = jnp.zeros((16, 2 * LANES), F32)
    for u in range(CMP_STRIDE // 2):
        lhs = jnp.concatenate([load(2 * u), load(2 * u + 1)], axis=-1).astype(BF16)
        w = wc_ref[wi, u]
        acc = acc + _dot(lhs, w)
        pew = pew + _dot(pe_ref[wi, u], w)
    first = acc[:, :LANES]
    second = pltpu.roll(acc[:, LANES:], rows - 1, 0)
    hid = first + second + pew[0:1, :LANES] + pew[8:9, LANES:]
    hid = hid * jax.nn.sigmoid(hid)
    return _dot(hid.astype(BF16), w2_ref[wi])


def _compress_prompt_kernel(*refs, nchunk):
    x_refs = refs[:G_B]
    wc_ref, pe_ref, w2_ref, o_ref = refs[G_B:]

    def load(tok):
        return jnp.concatenate([xr[0, pl.ds(tok, nchunk, stride=CMP_STRIDE), :] for xr in x_refs], axis=0)

    out = _compress_core(load, wc_ref, pe_ref, w2_ref, 0, nchunk)
    for g in range(G_B):
        o_ref[0, 0, g] = out[g * nchunk:(g + 1) * nchunk]


def _compress_sample_kernel(pt_ref, *refs, n_pages, spb):
    pg_refs = refs[:spb * n_pages]
    wc_ref, pe_ref, w2_ref, o_ref = refs[spb * n_pages:]
    cps = PAGE_SIZE // CMP_STRIDE
    nchunk = n_pages * cps
    for kv in range(2):
        by_tok = [jnp.swapaxes(pg[0, pl.ds(kv * G_B + g, PAGE_SIZE, stride=SLOTS), :]
                               .reshape(cps, CMP_STRIDE, LANES), 0, 1)
                  for smp in range(spb) for g in range(G_B) for pg in pg_refs[smp * n_pages:(smp + 1) * n_pages]]

        def load(tok):
            return jnp.concatenate([x[tok] for x in by_tok], axis=0)

        out = _compress_core(load, wc_ref, pe_ref, w2_ref, kv, spb * nchunk)
        for smp in range(spb):
            for g in range(G_B):
                seg = smp * G_B + g
                o_ref[smp, kv, g] = out[seg * nchunk:(seg + 1) * nchunk]


def _compress_weights(cmp_pe, cmp_w1, cmp_w2):
    w1 = cmp_w1.reshape(2, CMP_LEN, DH_B, DH_B)
    wab = jnp.concatenate([w1[:, :CMP_STRIDE], w1[:, CMP_STRIDE:]], axis=-1)
    wc = wab.reshape(2, CMP_STRIDE // 2, 2 * DH_B, 2 * DH_B).astype(BF16)
    pa = cmp_pe[:, :CMP_STRIDE].reshape(2, CMP_STRIDE // 2, 1, 2 * DH_B)
    pb = cmp_pe[:, CMP_STRIDE:].reshape(2, CMP_STRIDE // 2, 1, 2 * DH_B)
    z = jnp.zeros((2, CMP_STRIDE // 2, 7, 2 * DH_B), F32)
    pe = jnp.concatenate([pa, z, pb, z], axis=2).astype(BF16)
    return wc, pe, cmp_w2.astype(BF16)


def _compress_prompt(cmp_kv, wc, pe, w2):
    n, t, _ = cmp_kv.shape
    nchunk = t // CMP_STRIDE
    kern = functools.partial(_compress_prompt_kernel, nchunk=nchunk)
    return pl.pallas_call(
        kern,
        grid=(n, 2),
        in_specs=[pl.BlockSpec((1, t, DH_B), functools.partial(lambda b, kv, g: (b, 0, kv * G_B + g), g=g))
                  for g in range(G_B)] + [
                  pl.BlockSpec((1,) + wc.shape[1:], lambda b, kv: (kv, 0, 0, 0)),
                  pl.BlockSpec((1,) + pe.shape[1:], lambda b, kv: (kv, 0, 0, 0)),
                  pl.BlockSpec((1, DH_B, DH_B), lambda b, kv: (kv, 0, 0))],
        out_specs=pl.BlockSpec((1, 1, G_B, nchunk, DH_B), lambda b, kv: (b, kv, 0, 0, 0)),
        out_shape=jax.ShapeDtypeStruct((n, 2, G_B, nchunk, DH_B), F32),
        compiler_params=_cparams(("parallel", "parallel")),
        name="compress_prompt",
    )(*([cmp_kv] * G_B), wc, pe, w2)


def _compress_sample(cache, page_table, wc, pe, w2):
    n, n_pages = page_table.shape
    nchunk = n_pages * (PAGE_SIZE // CMP_STRIDE)
    spb = math.gcd(n, 2)
    kern = functools.partial(_compress_sample_kernel, n_pages=n_pages, spb=spb)
    page_specs = [pl.BlockSpec((1, PAGE_SIZE * SLOTS, LANES),
                               functools.partial(lambda b, pt, smp, j: (pt[b * spb + smp, j], 0, 0), smp=smp, j=j))
                  for smp in range(spb) for j in range(n_pages)]
    grid_spec = pltpu.PrefetchScalarGridSpec(
        num_scalar_prefetch=1,
        grid=(n // spb,),
        in_specs=page_specs + [
            pl.BlockSpec(wc.shape, lambda b, pt: (0, 0, 0, 0)),
            pl.BlockSpec(pe.shape, lambda b, pt: (0, 0, 0, 0)),
            pl.BlockSpec(w2.shape, lambda b, pt: (0, 0, 0))],
        out_specs=pl.BlockSpec((spb, 2, G_B, nchunk, DH_B), lambda b, pt: (b, 0, 0, 0, 0)),
    )
    return pl.pallas_call(
        kern,
        grid_spec=grid_spec,
        out_shape=jax.ShapeDtypeStruct((n, 2, G_B, nchunk, DH_B), F32),
        compiler_params=_cparams(("parallel",)),
        name="compress_sample",
    )(page_table, *([cache] * (spb * n_pages)), wc, pe, w2)


def _cmp_topk_kernel(q_ref, kc_ref, vc_ref, ocmp_ref, selm_ref, p_scr, v_scr, *,
                     nb, rq, nck, n_cmp, n_sel, nsp, pos_base):
    tq = nb * rq
    j = pl.program_id(2)
    scale = DH_B ** -0.5
    kidx = _iota((1, nck), 1)
    qpos_c = pos_base + j * rq + _iota((rq, 1), 0)
    valid = (kidx < n_cmp) & (CMP_STRIDE * kidx + (CMP_LEN - 1) <= qpos_c)
    for b in range(nb):
        kcc = kc_ref[b, 0, 0].astype(BF16)
        vcc = vc_ref[b, 0, 0].astype(BF16)
        q = q_ref[b] * scale
        q2 = jnp.concatenate([q[:, r * LANES:(r + 1) * LANES] for r in range(R_B)], axis=0).astype(BF16)
        valid2 = jnp.concatenate([valid] * R_B, axis=0)
        s = jnp.where(valid2, _dot_nt(q2, kcc), MASK_NEG)
        m = jnp.max(s, axis=-1, keepdims=True)
        p = jnp.where(valid2, jnp.exp(s - m), 0.0)
        p = p / jnp.maximum(jnp.sum(p, axis=-1, keepdims=True), 1e-30)
        o = _dot(p.astype(BF16), vcc)
        psum = jnp.zeros((rq, nck), F32)
        for r in range(R_B):
            ocmp_ref[b, :, r * LANES:(r + 1) * LANES] = o[r * rq:(r + 1) * rq]
            psum = psum + p[r * rq:(r + 1) * rq]
        p_scr[b * rq:(b + 1) * rq, :] = psum
    psum = p_scr[...]
    p_hi = psum.astype(BF16)
    p_lo = (psum - p_hi.astype(F32)).astype(BF16)
    srow = _iota((LANES, nck), 0)
    kcol = _iota((LANES, nck), 1)
    cov = (CMP_STRIDE * kcol < SEL_BLOCK * srow + SEL_BLOCK) & (CMP_STRIDE * kcol + CMP_LEN > SEL_BLOCK * srow)
    cov = cov & (srow < n_sel) & (kcol < n_cmp)
    cov_t = jnp.where(cov, 1.0, 0.0).astype(BF16)
    imp_t = _dot_nt(cov_t, p_hi) + _dot_nt(cov_t, p_lo)
    qpos_l = pos_base + j * rq + _iota((1, tq), 1) % rq
    sb = _iota((nsp, 1), 0)
    valid_b = (sb < n_sel) & (sb * SEL_BLOCK <= qpos_l)
    forced = (sb == 0) | (sb == qpos_l // SEL_BLOCK)
    vals = jnp.where(forced, jnp.inf, jnp.where(valid_b, imp_t[:nsp], -jnp.inf))
    v_scr[...] = vals

    def body(jj, cnt):
        vj = v_scr[pl.ds(jj, 1), :]
        tie = jnp.where(sb > jj, 1.0, 0.0)
        return cnt + jnp.where(vj > vals, 1.0, jnp.where(vj == vals, tie, 0.0))

    last_pos = pos_base + (j + 1) * rq - 1
    n_rank = jnp.minimum(last_pos // SEL_BLOCK + 1, nsp)
    cnt = lax.fori_loop(0, n_rank, body, jnp.zeros((nsp, tq), F32))
    keep = valid_b & (cnt < float(min(TOP_N, n_sel)))
    selm_t = jnp.where(keep, 0.0, -SEL_NEG)
    if nsp < LANES:
        selm_t = jnp.concatenate([selm_t, jnp.zeros((LANES - nsp, tq), F32)], axis=0)
    selm = selm_t.T.astype(BF16)
    for b in range(nb):
        selm_ref[b, 0] = selm[b * rq:(b + 1) * rq]


def _cmp_topk(qb, ccmp, *, nb, rq, n_cmp, n_sel, pos_base):
    n, r, _ = qb.shape
    nck = ccmp.shape[3]
    nsp = -(-n_sel // 8) * 8
    tq = nb * rq
    kern = functools.partial(_cmp_topk_kernel, nb=nb, rq=rq, nck=nck, n_cmp=n_cmp, n_sel=n_sel, nsp=nsp,
                             pos_base=pos_base)
    return pl.pallas_call(
        kern,
        grid=(n // nb, G_B, r // rq),
        in_specs=[pl.BlockSpec((nb, rq, R_B * LANES), lambda a, g, j: (a, j, g)),
                  pl.BlockSpec((nb, 1, 1, nck, DH_B), lambda a, g, j: (a, 0, g, 0, 0)),
                  pl.BlockSpec((nb, 1, 1, nck, DH_B), lambda a, g, j: (a, 1, g, 0, 0))],
        out_specs=[pl.BlockSpec((nb, rq, R_B * LANES), lambda a, g, j: (a, j, g)),
                   pl.BlockSpec((nb, 1, rq, LANES), lambda a, g, j: (a, g, j, 0))],
        out_shape=[jax.ShapeDtypeStruct((n, r, H_B * DH_B), F32),
                   jax.ShapeDtypeStruct((n, G_B, r, LANES), BF16)],
        scratch_shapes=[pltpu.VMEM((tq, nck), F32), pltpu.VMEM((nsp, tq), F32)],
        compiler_params=_cparams(("parallel", "parallel", "parallel")),
        name="cmp_topk",
    )(qb, ccmp, ccmp)


def _attend_scores(pieces):
    m = None
    for s, _ in pieces:
        ms = jnp.max(s, axis=-1, keepdims=True)
        m = ms if m is None else jnp.maximum(m, ms)
    l = jnp.zeros_like(m)
    acc = jnp.zeros((m.shape[0], LANES), F32)
    for s, v in pieces:
        p = jnp.exp(s - m)
        l = l + jnp.sum(p, axis=-1, keepdims=True)
        acc = acc + _dot(p.astype(BF16), v)
    return acc / l


def _attend_pieces(q2, pieces):
    scored = []
    for k, v, mask in pieces:
        s = _dot_nt(q2, k)
        scored.append((s if mask is None else jnp.where(mask, s, MASK_NEG), v))
    return _attend_scores(scored)


def _pad_rows(x, rows):
    return jnp.concatenate([x, jnp.zeros((rows - x.shape[0], x.shape[1]), x.dtype)], axis=0)


def _new_token_mask(m_rows, s):
    r = _iota((m_rows, LANES), 0) % s
    return _iota((m_rows, LANES), 1) <= r


def _diff_sample_kernel(pt_ref, *refs, n_pages, s, lam_init):
    kp_refs = refs[:n_pages]
    vp_refs = refs[n_pages:2 * n_pages]
    q_ref, kn_ref, vn_ref, lamw_ref, sub_ref, o_ref = refs[2 * n_pages:]
    lane = _iota((1, LANES), 1)
    lam = _diff_lambda(lamw_ref[...], lam_init)
    new_mask = _new_token_mask(2 * s, s)
    for h in range(H_A):
        cols = slice(h * LANES, (h + 1) * LANES)
        q = q_ref[0, :, cols] * (DK_A ** -0.5)
        q2 = jnp.concatenate([jnp.where(lane < DK_A, q, 0.0), jnp.where(lane >= DK_A, q, 0.0)], axis=0).astype(BF16)
        kt_past = jnp.concatenate([r[0, cols, :].astype(BF16) for r in kp_refs], axis=1)
        s_past = _dot(q2, kt_past)
        vpast = jnp.concatenate([_slot_rows(r, h, PAGE_SIZE).astype(BF16) for r in vp_refs], axis=0)
        knew = _pad_rows(kn_ref[0, :, cols], LANES).astype(BF16)
        vnew = _pad_rows(vn_ref[0, :, cols], LANES).astype(BF16)
        s_new = jnp.where(new_mask, _dot_nt(q2, knew), MASK_NEG)
        o2 = _attend_scores([(s_past, vpast), (s_new, vnew)])
        o_ref[0, :, cols] = _diff_finalize(o2, s, lam, sub_ref[...], lam_init)


def _page_specs(n_pages, rows, width):
    return [pl.BlockSpec((1, rows, width), functools.partial(lambda b, pt, j: (pt[b, j], 0, 0), j=j))
            for j in range(n_pages)]


def _slot_rows(ref, slot, n_tok):
    return ref[0, pl.ds(slot, n_tok, stride=SLOTS), :]


def _diff_attn_sample(qa, ka, va, cache_k, cache_v, page_table, lam_w, sub_w, lam_init):
    n, s, _ = qa.shape
    n_pages = page_table.shape[1]
    kern = functools.partial(_diff_sample_kernel, n_pages=n_pages, s=s, lam_init=lam_init)
    row_spec = pl.BlockSpec((1, s, 512), lambda b, pt: (b, 0, 0))
    grid_spec = pltpu.PrefetchScalarGridSpec(
        num_scalar_prefetch=1,
        grid=(n,),
        in_specs=_page_specs(n_pages, 512, PAGE_SIZE) + _page_specs(n_pages, PAGE_SIZE * SLOTS, LANES)
        + [row_spec, row_spec, row_spec,
           pl.BlockSpec(lam_w.shape, lambda b, pt: (0, 0)), pl.BlockSpec(sub_w.shape, lambda b, pt: (0, 0))],
        out_specs=row_spec,
    )
    return pl.pallas_call(
        kern,
        grid_spec=grid_spec,
        out_shape=jax.ShapeDtypeStruct((n, s, 512), F32),
        compiler_params=_cparams(("parallel",)),
        name="diff_attn_sample",
    )(page_table, *([cache_k] * n_pages), *([cache_v] * n_pages), qa, ka, va, lam_w, sub_w)


def _sel_sample_kernel(pt_ref, *refs, n_pages, s, spb):
    pg_refs = refs[:spb * n_pages]
    q_ref, selm_ref, new_ref, o_ref = refs[spb * n_pages:]
    past = n_pages * PAGE_SIZE
    new_mask = _new_token_mask(R_B * s, s)
    oh_past = _sel_onehot(past, 0)
    new_blk = past // SEL_BLOCK
    oh_new = jnp.where((_iota((LANES, LANES), 1) == new_blk) & (_iota((LANES, LANES), 0) < s), 1.0, 0.0)
    oh_new = oh_new.astype(BF16)
    scale = DH_B ** -0.5
    for b in range(spb):
        pages = pg_refs[b * n_pages:(b + 1) * n_pages]
        for g in range(G_B):
            kcols = slice(g * LANES, (g + 1) * LANES)
            vcols = slice((G_B + g) * LANES, (G_B + g + 1) * LANES)
            selm = selm_ref[b, g]
            q2 = jnp.concatenate(
                [jnp.concatenate([(q_ref[b, :, (g * R_B + r) * LANES:(g * R_B + r + 1) * LANES] * scale).astype(BF16),
                                  selm], axis=-1) for r in range(R_B)], axis=0)
            kpast = jnp.concatenate([_slot_rows(r, g, PAGE_SIZE).astype(BF16) for r in pages], axis=0)
            vpast = jnp.concatenate([_slot_rows(r, G_B + g, PAGE_SIZE).astype(BF16) for r in pages], axis=0)
            kpast = jnp.concatenate([kpast, oh_past], axis=-1)
            knew = jnp.concatenate([_pad_rows(new_ref[b, :, kcols], LANES).astype(BF16), oh_new], axis=-1)
            vnew = _pad_rows(new_ref[b, :, vcols], LANES).astype(BF16)
            o2 = _attend_pieces(q2, [(kpast, vpast, None), (knew, vnew, new_mask)])
            for r in range(R_B):
                o_ref[b, :, (g * R_B + r) * LANES:(g * R_B + r + 1) * LANES] = o2[r * s:(r + 1) * s]


def _sel_attn_sample(qb, selm, sel_new, cache, page_table):
    n, s, _ = qb.shape
    n_pages = page_table.shape[1]
    assert s <= SEL_BLOCK and (n_pages * PAGE_SIZE) % SEL_BLOCK == 0
    spb = math.gcd(n, 2)
    kern = functools.partial(_sel_sample_kernel, n_pages=n_pages, s=s, spb=spb)
    row_spec = pl.BlockSpec((spb, s, 512), lambda b, pt: (b, 0, 0))
    page_specs = [pl.BlockSpec((1, PAGE_SIZE * SLOTS, LANES),
                               functools.partial(lambda b, pt, smp, j: (pt[b * spb + smp, j], 0, 0), smp=smp, j=j))
                  for smp in range(spb) for j in range(n_pages)]
    grid_spec = pltpu.PrefetchScalarGridSpec(
        num_scalar_prefetch=1,
        grid=(n // spb,),
        in_specs=page_specs
        + [row_spec, pl.BlockSpec((spb, G_B, s, LANES), lambda b, pt: (b, 0, 0, 0)), row_spec],
        out_specs=row_spec,
    )
    return pl.pallas_call(
        kern,
        grid_spec=grid_spec,
        out_shape=jax.ShapeDtypeStruct((n, s, 512), F32),
        compiler_params=_cparams(("parallel",)),
        name="nsa_sel_sample",
    )(page_table, *([cache] * (spb * n_pages)), qb, selm, sel_new)


def _win_sample_kernel(q_ref, buf_ref, new_ref, o_ref, wout_ref, *, s, wb, spb):
    new_mask = _new_token_mask(R_B * s, s)
    r = _iota((R_B * s, wb), 0) % s
    buf_mask = _iota((R_B * s, wb), 1) > r + (wb - WINDOW)
    scale = DH_B ** -0.5
    keep = (wb - s) * SLOTS
    for b in range(spb):
        for g in range(G_B):
            kcols = slice(g * LANES, (g + 1) * LANES)
            vcols = slice((G_B + g) * LANES, (G_B + g + 1) * LANES)
            q2 = jnp.concatenate(
                [(q_ref[b, :, (g * R_B + r_) * LANES:(g * R_B + r_ + 1) * LANES] * scale).astype(BF16)
                 for r_ in range(R_B)], axis=0)
            kbuf = buf_ref[b, pl.ds(g, wb, stride=SLOTS), :].astype(BF16)
            vbuf = buf_ref[b, pl.ds(G_B + g, wb, stride=SLOTS), :].astype(BF16)
            knew = _pad_rows(new_ref[b, :, kcols], LANES).astype(BF16)
            vnew = _pad_rows(new_ref[b, :, vcols], LANES).astype(BF16)
            o2 = _attend_pieces(q2, [(kbuf, vbuf, buf_mask), (knew, vnew, new_mask)])
            for r_ in range(R_B):
                o_ref[b, :, (g * R_B + r_) * LANES:(g * R_B + r_ + 1) * LANES] = o2[r_ * s:(r_ + 1) * s]
        wout_ref[b, 0:keep] = buf_ref[b, s * SLOTS:wb * SLOTS]
        for slot in range(SLOTS):
            wout_ref[b, pl.ds(keep + slot, s, stride=SLOTS), :] = new_ref[b, :, slot * LANES:(slot + 1) * LANES]


def _win_attn_sample(qb, win_buf, win_new, past):
    n, s, _ = qb.shape
    wb = win_buf.shape[1] // SLOTS
    assert past >= wb and wb % 8 == 0 and s % 8 == 0
    spb = math.gcd(n, 4)
    kern = functools.partial(_win_sample_kernel, s=s, wb=wb, spb=spb)
    row_spec = pl.BlockSpec((spb, s, 512), lambda b: (b, 0, 0))
    buf_spec = pl.BlockSpec((spb, wb * SLOTS, LANES), lambda b: (b, 0, 0))
    return pl.pallas_call(
        kern,
        grid=(n // spb,),
        in_specs=[row_spec, buf_spec, row_spec],
        out_specs=[row_spec, buf_spec],
        out_shape=[jax.ShapeDtypeStruct((n, s, 512), F32), jax.ShapeDtypeStruct((n, wb * SLOTS, LANES), F32)],
        compiler_params=_cparams(("parallel",)),
        name="nsa_win_sample",
    )(qb, win_buf, win_new)


def _mlp_value(y, sh, sc, g, nw, w1_ref, w2_ref, fw, final):
    h = _norm_mod(y, nw, sh, sc).reshape(-1, D_MODEL).astype(BF16)
    a = jnp.maximum(_dot(h, w1_ref[...]), 0.0)
    out = _dot((a * a).astype(BF16), w2_ref[...])
    y2 = y + g * out.reshape(y.shape)
    return _rms(y2, fw) if final else y2


def _mlp_specs(til, mlp):
    nw, w1, w2, fw, _ = mlp
    specs = [til.mod_spec(3), til.mod_spec(4), til.mod_spec(5), _full_spec((1, D_MODEL)),
             _const_spec(w1.shape), _const_spec(w2.shape), _full_spec((1, D_MODEL))]
    return specs, [nw, w1, w2, fw]


def _even_out_kernel(y_ref, g1_ref, oa_ref, oc_ref, os_ref, ow_ref, gt_ref, w_ref,
                     sh_ref, sc_ref, g2_ref, nw_ref, w1_ref, w2_ref, fw_ref, o_ref, *, final):
    gt = gt_ref[...]
    parts = [oa_ref[...].astype(BF16)]
    for hb in range(H_B):
        sl = slice(hb * LANES, (hb + 1) * LANES)
        ob = (gt[:, 3 * hb:3 * hb + 1] * oc_ref[:, sl] + gt[:, 3 * hb + 1:3 * hb + 2] * os_ref[:, sl]
              + gt[:, 3 * hb + 2:3 * hb + 3] * ow_ref[:, sl])
        parts.append(ob.astype(BF16))
    out = _dot(jnp.concatenate(parts, axis=-1), w_ref[...])
    y = y_ref[...]
    y1 = y + g1_ref[0] * out.reshape(y.shape)
    o_ref[...] = _mlp_value(y1, sh_ref[0], sc_ref[0], g2_ref[0], nw_ref[...], w1_ref, w2_ref, fw_ref[...], final)


def _even_out(til, y, mods, oa, oc, os_, ow, gt, w_out, mlp):
    mlp_specs, mlp_args = _mlp_specs(til, mlp)
    return pl.pallas_call(
        functools.partial(_even_out_kernel, final=mlp[4]),
        grid=til.grid,
        in_specs=[til.x_spec(D_MODEL), til.mod_spec(2), til.flat_spec(512), til.flat_spec(512),
                  til.flat_spec(512), til.flat_spec(512), til.flat_spec(LANES), _const_spec(w_out.shape)] + mlp_specs,
        out_specs=til.x_spec(D_MODEL),
        out_shape=jax.ShapeDtypeStruct(y.shape, F32),
        compiler_params=_cparams(("parallel", "parallel")),
        name="even_out_mlp",
    )(y, mods, oa, oc, os_, ow, gt, w_out, mods, mods, mods, *mlp_args)


def _gla_inproj_kernel(x_ref, sh_ref, sc_ref, nw_ref, w_ref, wgl_ref, wgate_ref, bgate_ref,
                       q_ref, k_ref, v_ref, r_ref, la_ref):
    h = _norm_mod(x_ref[...], nw_ref[...], sh_ref[0], sc_ref[0]).reshape(-1, D_MODEL).astype(BF16)
    nk = H_C * DK_C
    nv = H_C * DV_C
    q_ref[...] = _dot(h, w_ref[:, 0:nk]) * (DK_C ** -0.5)
    k_ref[...] = _dot(h, w_ref[:, nk:2 * nk])
    v_ref[...] = _dot(h, w_ref[:, 2 * nk:2 * nk + nv])
    r_ref[...] = _dot(h, w_ref[:, 2 * nk + nv:2 * nk + 2 * nv])
    gl = _dot(h, wgl_ref[...])
    x = _dot(gl.astype(BF16), wgate_ref[...]) + bgate_ref[...]
    log_sig = jnp.minimum(x, 0.0) - jnp.log1p(jnp.exp(-jnp.abs(x)))
    la_ref[...] = log_sig / GATE_TAU


def _gla_inproj(til, x, mods, nw, w_main, w_gl, w_gate, b_gate):
    rows_total = til.nb * til.r
    widths = (H_C * DK_C, H_C * DK_C, H_C * DV_C, H_C * DV_C, H_C * DK_C)
    return pl.pallas_call(
        _gla_inproj_kernel,
        grid=til.grid,
        in_specs=[til.x_spec(D_MODEL), til.mod_spec(0), til.mod_spec(1), _full_spec((1, D_MODEL)),
                  _const_spec(w_main.shape), _const_spec(w_gl.shape), _const_spec(w_gate.shape),
                  _full_spec(b_gate.shape)],
        out_specs=[til.flat_spec(c) for c in widths],
        out_shape=[jax.ShapeDtypeStruct((rows_total, c), F32) for c in widths],
        compiler_params=_cparams(("parallel", "parallel")),
        name="gla_inproj",
    )(x, mods, mods, nw, w_main, w_gl, w_gate, b_gate)


def _cumsum_rows(g):
    c = g.shape[0]
    row = _iota((c, 1), 0)
    b = g
    shift = 1
    while shift < c:
        b = b + jnp.where(row >= shift, pltpu.roll(b, shift, 0), 0.0)
        shift *= 2
    return b


def _gla_prep(q, k, g, c, sub, n_real):
    b = _cumsum_rows(g)
    qe = (q * jnp.exp(b)).astype(BF16)
    lane_c = _iota((sub, c), 1)
    row_s = _iota((sub, c), 0)
    att_rows = []
    for blk in range(c // sub):
        lo = blk * sub
        qi, ki, bi = q[lo:lo + sub], k[lo:lo + sub], b[lo:lo + sub]
        diag = jnp.zeros((sub, c), F32)
        for jj in range(min(sub, max(n_real - lo, 0))):
            e = jnp.exp(jnp.minimum(bi - bi[jj:jj + 1], 0.0))
            col = jnp.sum(qi * ki[jj:jj + 1] * e, axis=-1, keepdims=True)
            diag = jnp.where(lane_c == lo + jj, col, diag)
        att = jnp.where(lane_c - lo <= row_s, diag, 0.0)
        if blk > 0:
            bs = b[lo - 1:lo]
            q_in = qi * jnp.exp(bi - bs)
            k_out = k * jnp.exp(jnp.minimum(bs - b, 0.0))
            att = jnp.where(lane_c < lo, _dot_nt(q_in.astype(BF16), k_out.astype(BF16)), att)
        att_rows.append(att)
    att = att_rows[0] if len(att_rows) == 1 else jnp.concatenate(att_rows, axis=0)
    bl = b[c - 1:c]
    kd = (k * jnp.exp(bl - b)).astype(BF16)
    eye = _iota((DK_C, DK_C), 0) == _iota((DK_C, DK_C), 1)
    decay = jnp.sum(jnp.where(eye, jnp.exp(bl), 0.0), axis=-1, keepdims=True)
    return qe, att, kd, decay


def _gla_prep_bounded(q, k, g, c, sub, n_real):
    del sub, n_real
    b = _cumsum_rows(g)
    r = b[0:1]
    bl = b[c - 1:c]
    q_in = q * jnp.exp(b - r)
    k_out = k * jnp.exp(r - b)
    att = jnp.where(_iota((c, c), 0) >= _iota((c, c), 1), _dot_nt(q_in.astype(BF16), k_out.astype(BF16)), 0.0)
    qe = (q_in * jnp.exp(r)).astype(BF16)
    kd = (k_out * jnp.exp(bl - r)).astype(BF16)
    eye = _iota((DK_C, DK_C), 0) == _iota((DK_C, DK_C), 1)
    decay = jnp.sum(jnp.where(eye, jnp.exp(bl), 0.0), axis=-1, keepdims=True)
    return qe, att, kd, decay


def _gla_apply(state, prep, v):
    qe, att, kd, decay = prep
    o = _dot(qe, state.astype(BF16)) + _dot(att, v)
    return o, decay * state + _dot_tn(kd, v.astype(BF16))


def _gla_rec_kernel(*refs, tt, c, sub, hp, nseq, has_s0):
    if has_s0:
        q_ref, k_ref, v_ref, g_ref, s0_ref, o_ref, sfin_ref, s_ref = refs
    else:
        q_ref, k_ref, v_ref, g_ref, o_ref, sfin_ref, s_ref = refs
    t = pl.program_id(2)

    @pl.when(t == 0)
    def _():
        for bi in range(nseq):
            for hh in range(hp):
                s_ref[bi * hp + hh] = s0_ref[bi, hh] if has_s0 else jnp.zeros((DK_C, DV_C), F32)

    def kcols(hh):
        return slice(hh * DK_C, (hh + 1) * DK_C)

    def vcols(hh):
        return slice(hh * DV_C, (hh + 1) * DV_C)

    def run(prep_fn):
        if tt < c:
            pad = lambda x: _pad_rows(x, c)
            for bi in range(nseq):
                for hh in range(hp):
                    prep = prep_fn(pad(q_ref[bi, :, kcols(hh)]), pad(k_ref[bi, :, kcols(hh)]),
                                   pad(g_ref[bi, :, kcols(hh)]), c, sub, tt)
                    si = bi * hp + hh
                    o, s_ref[si] = _gla_apply(s_ref[si], prep, pad(v_ref[bi, :, vcols(hh)]))
                    o_ref[bi, :, vcols(hh)] = o[:tt]
            return
        assert nseq == 1
        per_trip = 2 if (tt // c) % 2 == 0 else 1

        def body(ci, carry):
            for hh in range(hp):
                rows = [pl.ds(pl.multiple_of((ci * per_trip + u) * c, c), c) for u in range(per_trip)]
                preps = [prep_fn(q_ref[0, r, kcols(hh)], k_ref[0, r, kcols(hh)], g_ref[0, r, kcols(hh)],
                                 c, sub, c) for r in rows]
                state = s_ref[hh]
                for r, prep in zip(rows, preps):
                    o_ref[0, r, vcols(hh)], state = _gla_apply(state, prep, v_ref[0, r, vcols(hh)])
                s_ref[hh] = state
            return carry
        lax.fori_loop(0, tt // c // per_trip, body, 0)

    if tt > c:
        g_all = g_ref[0]
        chunk_decay = -jnp.sum(g_all.reshape(tt // c, c, g_all.shape[-1]), axis=1)
    else:
        chunk_decay = -jnp.sum(g_ref[...], axis=1)
    bounded = jnp.max(chunk_decay) <= GLA_SAFE_DECAY

    @pl.when(bounded)
    def _():
        run(_gla_prep_bounded)

    @pl.when(jnp.logical_not(bounded))
    def _():
        run(_gla_prep)

    @pl.when(t == pl.num_programs(2) - 1)
    def _():
        for bi in range(nseq):
            for hh in range(hp):
                sfin_ref[bi, hh] = s_ref[bi * hp + hh]


def _gla_recurrence(q, k, v, g, s0, tt, c, sub, hp):
    n, t, _ = q.shape
    nseq = math.gcd(n, 2) if tt < c else 1
    kern = functools.partial(_gla_rec_kernel, tt=tt, c=c, sub=sub, hp=hp, nseq=nseq, has_s0=s0 is not None)
    kspec = pl.BlockSpec((nseq, tt, hp * DK_C), lambda b, h, i: (b, i, h))
    vspec = pl.BlockSpec((nseq, tt, hp * DV_C), lambda b, h, i: (b, i, h))
    sspec = pl.BlockSpec((nseq, hp, DK_C, DV_C), lambda b, h, i: (b, h, 0, 0))
    in_specs = [kspec, kspec, vspec, kspec]
    args = [q, k, v, g]
    if s0 is not None:
        in_specs.append(sspec)
        args.append(s0)
    return pl.pallas_call(
        kern,
        grid=(n // nseq, H_C // hp, t // tt),
        in_specs=in_specs,
        out_specs=[vspec, sspec],
        out_shape=[jax.ShapeDtypeStruct((n, t, H_C * DV_C), F32),
                   jax.ShapeDtypeStruct((n, H_C, DK_C, DV_C), F32)],
        scratch_shapes=[pltpu.VMEM((nseq * hp, DK_C, DV_C), F32)],
        compiler_params=_cparams(("parallel", "parallel", "arbitrary")),
        name="gla_recurrence",
    )(*args)


def _gla_out_kernel(y_ref, g1_ref, o_ref_in, r_ref, gnw_ref, w_ref,
                    sh_ref, sc_ref, g2_ref, nw_ref, w1_ref, w2_ref, fw_ref, out_ref, *, final):
    parts = []
    for h in range(H_C):
        sl = slice(h * DV_C, (h + 1) * DV_C)
        r = r_ref[:, sl]
        parts.append((_rms(o_ref_in[:, sl], gnw_ref[...]) * (r * jax.nn.sigmoid(r))).astype(BF16))
    out = _dot(jnp.concatenate(parts, axis=-1), w_ref[...])
    y = y_ref[...]
    y1 = y + g1_ref[0] * out.reshape(y.shape)
    out_ref[...] = _mlp_value(y1, sh_ref[0], sc_ref[0], g2_ref[0], nw_ref[...], w1_ref, w2_ref, fw_ref[...], final)


def _gla_out(til, y, mods, o, r, gnw, w_out, mlp):
    mlp_specs, mlp_args = _mlp_specs(til, mlp)
    return pl.pallas_call(
        functools.partial(_gla_out_kernel, final=mlp[4]),
        grid=til.grid,
        in_specs=[til.x_spec(D_MODEL), til.mod_spec(2), til.flat_spec(H_C * DV_C), til.flat_spec(H_C * DV_C),
                  _full_spec(gnw.shape), _const_spec(w_out.shape)] + mlp_specs,
        out_specs=til.x_spec(D_MODEL),
        out_shape=jax.ShapeDtypeStruct(y.shape, F32),
        compiler_params=_cparams(("parallel", "parallel")),
        name="gla_out_mlp",
    )(y, mods, o, r, gnw, w_out, mods, mods, mods, *mlp_args)


def _rope_tables(pos, d):
    inv = ROPE_THETA ** (-jnp.arange(0, d, 2, dtype=F32) / d)
    ang = pos.astype(F32)[:, None] * inv[None, :]
    cos, sin = jnp.cos(ang), jnp.sin(ang)
    rep = LANES // d
    c = jnp.tile(jnp.concatenate([cos, cos], axis=-1), (1, rep))
    s = jnp.tile(jnp.concatenate([-sin, sin], axis=-1), (1, rep))
    return c, s


def _rope_tables_t(pos, d):
    inv = ROPE_THETA ** (-jnp.arange(0, d, 2, dtype=F32) / d)
    ang = inv[:, None] * pos.astype(F32)[None, :]
    return jnp.cos(ang), jnp.sin(ang)


def _mods_for(mod_l, lo, hi):
    nb = hi - lo
    return mod_l[lo:hi].reshape(nb, 6, D_MODEL).transpose(1, 0, 2).reshape(6, nb, 1, D_MODEL)


def _prompt_tile_rows(t):
    return math.gcd(t, 256)


def _even_layer(yp, ys, mods_p, mods_s, caches, page_table, wts, lam_init, til_p, til_s, mlp):
    (c_dk, c_dv, c_cmp, c_sel, win_buf) = caches
    n, t, _ = yp.shape
    ns, s, _ = ys.shape
    n_pages = page_table.shape[1]
    past = n_pages * PAGE_SIZE
    w_in = wts["w_in"]
    n_main = w_in.shape[1] - 3 * H_B
    w_main = w_in[:, :n_main].astype(BF16)
    w_gate = jnp.pad(w_in[:, n_main:], ((0, 0), (0, LANES - 3 * H_B))).astype(BF16)
    nw1 = wts["norm1"].reshape(1, D_MODEL)
    lam_w = wts["lam_w"]
    sub_w = wts["sub_w"].reshape(1, DV_A)
    wc, pe, w2c = _compress_weights(wts["cmp_pe"], wts["cmp_w1"], wts["cmp_w2"])
    w_out = wts["w_out"].astype(BF16)

    tabs_p = _rope_tables(jnp.arange(t), DK_A) + _rope_tables(jnp.arange(t), DH_B)
    kt_tabs = _rope_tables_t(jnp.arange(t), DK_A)
    qa, kat, va, qb, cmp_kv, sel_kv, win_kv, gt = _even_inproj(til_p, yp, mods_p, nw1, w_main, w_gate, tabs_p, kt_tabs)
    r3 = lambda a: a.reshape(n, t, a.shape[-1])
    tq = _prompt_tile_rows(t)
    tq_attn = math.gcd(t, 2 * tq)
    oa = _diff_attn_prompt(r3(qa), kat, r3(va), lam_w, sub_w, lam_init, tq_attn, tq)
    ccmp = _compress_prompt(r3(cmp_kv), wc, pe, w2c)
    n_cmp = (t - CMP_LEN) // CMP_STRIDE + 1
    n_sel = -(-t // SEL_BLOCK)
    o_cmp, selm = _cmp_topk(r3(qb), ccmp, nb=1, rq=tq, n_cmp=n_cmp, n_sel=n_sel, pos_base=0)
    o_sel = _nsa_attn_prompt(r3(qb), r3(sel_kv), selm, tq_attn, tq)
    o_win = _nsa_attn_prompt(r3(qb), r3(win_kv), None, tq_attn, tq)
    f2 = lambda a: a.reshape(n * t, a.shape[-1])
    yp = _even_out(til_p, yp, mods_p, f2(oa), f2(o_cmp), f2(o_sel), f2(o_win), gt, w_out, mlp)
    wl = min(WINDOW, t)
    ka_state = jnp.transpose(kat.reshape(n, H_A, 2, DK_A, t), (0, 4, 1, 2, 3))
    st_p = (ka_state, r3(va).reshape(n, t, H_A, DV_A),
            r3(cmp_kv).reshape(n, t, 2, G_B, DH_B), r3(sel_kv).reshape(n, t, 2, G_B, DH_B),
            r3(win_kv)[:, t - wl:].reshape(n, wl, 2, G_B, DH_B))

    pos_s = past + jnp.arange(s)
    tabs_s = tuple(jnp.tile(x, (til_s.b, 1)) for x in _rope_tables(pos_s, DK_A) + _rope_tables(pos_s, DH_B))
    qa, ka, va, qb, cmp_kv, sel_kv, win_kv, gt = _even_inproj(til_s, ys, mods_s, nw1, w_main, w_gate, tabs_s)
    r3 = lambda a: a.reshape(ns, s, a.shape[-1])
    slot_cache = lambda c: c.reshape(c.shape[0], PAGE_SIZE * SLOTS, LANES)
    c_dkt = jnp.transpose(c_dk, (0, 2, 3, 4, 1)).reshape(c_dk.shape[0], H_A * 2 * DK_A, PAGE_SIZE)
    oa = _diff_attn_sample(r3(qa), r3(ka), r3(va), c_dkt, slot_cache(c_dv), page_table, lam_w, sub_w, lam_init)
    ccmp = _compress_sample(slot_cache(c_cmp), page_table, wc, pe, w2c)
    total = past + s
    n_cmp = (total - CMP_LEN) // CMP_STRIDE + 1
    n_sel = -(-total // SEL_BLOCK)
    assert n_cmp <= ccmp.shape[3] and n_sel <= LANES
    nb = math.gcd(ns, LANES // s)
    o_cmp, selm = _cmp_topk(r3(qb), ccmp, nb=nb, rq=s, n_cmp=n_cmp, n_sel=n_sel, pos_base=past)
    o_sel = _sel_attn_sample(r3(qb), selm, r3(sel_kv), slot_cache(c_sel), page_table)
    wb = win_buf.shape[1]
    o_win, win_out = _win_attn_sample(r3(qb), win_buf.reshape(ns, wb * SLOTS, LANES), r3(win_kv), past)
    f2 = lambda a: a.reshape(ns * s, a.shape[-1])
    ys = _even_out(til_s, ys, mods_s, f2(oa), f2(o_cmp), f2(o_sel), f2(o_win), gt, w_out, mlp)
    st_s = (r3(ka).reshape(ns, s, H_A, 2, DK_A), r3(va).reshape(ns, s, H_A, DV_A),
            r3(cmp_kv).reshape(ns, s, 2, G_B, DH_B), r3(sel_kv).reshape(ns, s, 2, G_B, DH_B),
            win_out.reshape(ns, wb, 2, G_B, DH_B))
    return yp, ys, st_p, st_s


def _odd_layer(yp, ys, mods_p, mods_s, s0, wts, til_p, til_s, mlp):
    w_in = wts["w_in"]
    n_main = w_in.shape[1] - GATE_RANK
    w_main = w_in[:, :n_main].astype(BF16)
    w_gl = jnp.pad(w_in[:, n_main:], ((0, 0), (0, LANES - GATE_RANK))).astype(BF16)
    w_gate = jnp.pad(wts["w_gate"], ((0, LANES - GATE_RANK), (0, 0))).astype(BF16)
    b_gate = wts["b_gate"].reshape(1, -1)
    nw1 = wts["norm1"].reshape(1, D_MODEL)
    gnw = wts["gnorm"].reshape(1, DV_C)
    w_out = wts["w_out"].astype(BF16)
    outs = []
    for y, mods, til, state in ((yp, mods_p, til_p, None), (ys, mods_s, til_s, s0)):
        n, t, _ = y.shape
        q, k, v, r, la = _gla_inproj(til, y, mods, nw1, w_main, w_gl, w_gate, b_gate)
        r3 = lambda a: a.reshape(n, t, a.shape[-1])
        c = math.gcd(t, GLA_CHUNK)
        if c >= GLA_SUB:
            tt, cc, sub = math.gcd(t, 8 * c), c, GLA_SUB_LONG
        else:
            tt, cc, sub = t, GLA_SUB, GLA_SUB
        o, s_fin = _gla_recurrence(r3(q), r3(k), r3(v), r3(la), state, tt, cc, sub, H_C)
        y = _gla_out(til, y, mods, o.reshape(n * t, -1), r, gnw, w_out, mlp)
        outs.append((y, s_fin))
    return outs[0][0], outs[1][0], outs[0][1], outs[1][1]


def kernel(x_prompt, x_sample, c_prompt, c_sample, cache_diff_k, cache_diff_v, cache_cmp_kv, cache_sel_kv,
           state_win_kv, state_gla, page_table, norm1_w, norm2_w, ada_w, ada_b, even_w_in, even_w_out,
           diff_lambda_w, diff_subln_w, cmp_pe, cmp_w1, cmp_w2, gla_w_in, gla_w_gate, gla_b_gate, gla_norm_w,
           gla_w_out, mlp_w1, mlp_w2, final_norm_w):
    depth = ada_w.shape[0]
    n, t, _ = x_prompt.shape
    ns, s, _ = x_sample.shape
    til_p = _Tiling(n, t, 1, _prompt_tile_rows(t))
    til_s = _Tiling(ns, s, math.gcd(ns, 256 // s), s)

    pad = (-(n + ns)) % 8
    c_all = jnp.concatenate([c_prompt, c_sample, jnp.zeros((pad, D_MODEL), F32)], axis=0)
    mod = _adaln(c_all, ada_w, ada_b)

    yp, ys = x_prompt, x_sample
    st_p = [[] for _ in range(6)]
    st_s = [[] for _ in range(6)]
    fw = final_norm_w.reshape(1, D_MODEL)
    for l in range(depth):
        mods_p = _mods_for(mod[l], 0, n)
        mods_s = _mods_for(mod[l], n, n + ns)
        mlp = (norm2_w[l].reshape(1, D_MODEL), mlp_w1[l].astype(BF16), mlp_w2[l].astype(BF16), fw, l == depth - 1)
        if l % 2 == 0:
            e = l // 2
            lam_init = 0.8 - 0.6 * math.exp(-0.3 * l)
            wts = dict(w_in=even_w_in[e], w_out=even_w_out[e], lam_w=diff_lambda_w[e], sub_w=diff_subln_w[e],
                       cmp_pe=cmp_pe[e], cmp_w1=cmp_w1[e], cmp_w2=cmp_w2[e], norm1=norm1_w[l])
            caches = (cache_diff_k[e], cache_diff_v[e], cache_cmp_kv[e], cache_sel_kv[e], state_win_kv[e])
            yp, ys, sp, ss = _even_layer(yp, ys, mods_p, mods_s, caches, page_table, wts, lam_init, til_p, til_s, mlp)
            for i in range(5):
                st_p[i].append(sp[i])
                st_s[i].append(ss[i])
        else:
            o = l // 2
            wts = dict(w_in=gla_w_in[o], w_gate=gla_w_gate[o], b_gate=gla_b_gate[o], gnorm=gla_norm_w[o],
                       w_out=gla_w_out[o], norm1=norm1_w[l])
            yp, ys, gp, gs = _odd_layer(yp, ys, mods_p, mods_s, state_gla[o], wts, til_p, til_s, mlp)
            st_p[5].append(gp)
            st_s[5].append(gs)
    outs_p = [jnp.stack(x, axis=0) for x in st_p]
    outs_s = [jnp.stack(x, axis=0) for x in st_s]
    return (yp, ys, *outs_p, *outs_s)
```

```python
import functools
import math

import jax
import jax.numpy as jnp
from jax import lax
from jax.experimental import pallas as pl
from jax.experimental.pallas import tpu as pltpu

F32 = jnp.float32
BF16 = jnp.bfloat16

D_MODEL = 1024
PAGE_SIZE = 128
H_A = 4
DK_A = 64
DV_A = 128
H_B = 4
G_B = 2
R_B = 2
DH_B = 128
CMP_LEN = 32
CMP_STRIDE = 16
SEL_BLOCK = 64
TOP_N = 16
WINDOW = 512
H_C = 4
DK_C = 128
DV_C = 256
GATE_RANK = 16
GATE_TAU = 16.0
GLA_CHUNK = 64
GLA_SUB = 16
GLA_SUB_LONG = 16
GLA_SAFE_DECAY = 80.0
D_FF = 4 * D_MODEL
ROPE_THETA = 10000.0
EPS = 1e-6

LANES = 128
SLOTS = 4
MASK_NEG = -1e30
LOG2E = 1.4426950408889634
SAFE_SHIFT = 56.0
SEL_NEG = 32768.0
VMEM_LIMIT_MB = 56


def _cparams(sem, vmem_mb=VMEM_LIMIT_MB):
    return pltpu.CompilerParams(dimension_semantics=sem, vmem_limit_bytes=vmem_mb * 1024 * 1024)


def _dot(a, b):
    return jnp.dot(a, b, preferred_element_type=F32)


def _dot_nt(a, b):
    return lax.dot_general(a, b, (((1,), (1,)), ((), ())), preferred_element_type=F32)


def _dot_tn(a, b):
    return lax.dot_general(a, b, (((0,), (0,)), ((), ())), preferred_element_type=F32)


def _iota(shape, dim):
    return lax.broadcasted_iota(jnp.int32, shape, dim)


def _rms(x, w):
    ms = jnp.mean(x * x, axis=-1, keepdims=True)
    return x * lax.rsqrt(ms + EPS) * w


def _norm_mod(x, nw, shift, scale):
    return _rms(x, nw) * (1.0 + scale) + shift


class _Tiling:
    def __init__(self, nb, r, b, rt):
        assert nb % b == 0 and r % rt == 0 and (b == 1 or rt == r)
        self.nb, self.r, self.b, self.rt = nb, r, b, rt
        self.grid = (nb // b, r // rt)
        self.rows = b * rt
        self.nrb = r // rt

    def x_spec(self, d):
        return pl.BlockSpec((self.b, self.rt, d), lambda i, j: (i, j, 0))

    def mod_spec(self, k):
        return pl.BlockSpec((1, self.b, 1, D_MODEL), lambda i, j: (k, i, 0, 0))

    def flat_spec(self, c):
        nrb = self.nrb
        return pl.BlockSpec((self.rows, c), lambda i, j: (i * nrb + j, 0))

    def tab_spec(self):
        return pl.BlockSpec((self.rows, LANES), lambda i, j: (j, 0))


def _full_spec(shape):
    nd = len(shape)
    return pl.BlockSpec(shape, lambda *_: (0,) * nd)


def _const_spec(shape):
    nd = len(shape)
    return pl.BlockSpec(shape, lambda *_: (0,) * nd, pipeline_mode=pl.Buffered(1))


def _adaln_kernel(c_ref, w_ref, b_ref, o_ref):
    c = c_ref[...]
    a = (c * jax.nn.sigmoid(c)).astype(BF16)
    o_ref[0] = _dot(a, w_ref[0].astype(BF16)) + b_ref[0]


def _adaln(c_all, ada_w, ada_b):
    depth, d, n6 = ada_w.shape
    rows = c_all.shape[0]
    tn = 1536
    return pl.pallas_call(
        _adaln_kernel,
        grid=(depth, n6 // tn),
        in_specs=[pl.BlockSpec((rows, d), lambda l, j: (0, 0)),
                  pl.BlockSpec((1, d, tn), lambda l, j: (l, 0, j)),
                  pl.BlockSpec((1, 1, tn), lambda l, j: (l, 0, j))],
        out_specs=pl.BlockSpec((1, rows, tn), lambda l, j: (l, 0, j)),
        out_shape=jax.ShapeDtypeStruct((depth, rows, n6), F32),
        compiler_params=_cparams(("parallel", "parallel")),
        name="adaln",
    )(c_all, ada_w, ada_b.reshape(depth, 1, n6))


def _swap_half(x, half):
    if 2 * half == LANES:
        return pltpu.roll(x, half, 1)
    lane = _iota((1, LANES), 1)
    lo = (lane % (2 * half)) < half
    return jnp.where(lo, pltpu.roll(x, LANES - half, 1), pltpu.roll(x, half, 1))


def _even_inproj_kernel(*refs, k_feature_major):
    if k_feature_major:
        (x_ref, sh_ref, sc_ref, nw_ref, w_ref, wg_ref, c64_ref, s64_ref, c128_ref, s128_ref, wkt_ref, ct_ref, st_ref,
         qa_ref, ka_ref, va_ref, qb_ref, cmp_ref, sel_ref, win_ref, gt_ref) = refs
    else:
        (x_ref, sh_ref, sc_ref, nw_ref, w_ref, wg_ref, c64_ref, s64_ref, c128_ref, s128_ref,
         qa_ref, ka_ref, va_ref, qb_ref, cmp_ref, sel_ref, win_ref, gt_ref) = refs
    h = _norm_mod(x_ref[...], nw_ref[...], sh_ref[0], sc_ref[0])
    h = h.reshape(-1, D_MODEL).astype(BF16)
    c64, s64, c128, s128 = c64_ref[...], s64_ref[...], c128_ref[...], s128_ref[...]

    def rope64(p):
        return p * c64 + _swap_half(p, DK_A // 2) * s64

    def rope128(p):
        return p * c128 + _swap_half(p, DH_B // 2) * s128

    def project(ref, off, ropes):
        p = _dot(h, w_ref[:, off:off + 4 * LANES])
        for j, rope in enumerate(ropes):
            sl = slice(j * LANES, (j + 1) * LANES)
            ref[:, sl] = p[:, sl] if rope is None else rope(p[:, sl])

    project(qa_ref, 0, [rope64] * 4)
    if k_feature_major:
        kt = _dot_nt(wkt_ref[...], h)
        ct, st = ct_ref[...], st_ref[...]
        half = DK_A // 2
        for grp in range(H_A * 2):
            x1 = kt[grp * DK_A:grp * DK_A + half]
            x2 = kt[grp * DK_A + half:(grp + 1) * DK_A]
            ka_ref[0, grp * DK_A:grp * DK_A + half, :] = x1 * ct - x2 * st
            ka_ref[0, grp * DK_A + half:(grp + 1) * DK_A, :] = x2 * ct + x1 * st
    else:
        project(ka_ref, 512, [rope64] * 4)
    project(va_ref, 1024, [None] * 4)
    project(qb_ref, 1536, [rope128] * 4)
    for t, ref in enumerate((cmp_ref, sel_ref, win_ref)):
        project(ref, 2048 + t * 512, [rope128, rope128, None, None])
    gt_ref[...] = jax.nn.sigmoid(_dot(h, wg_ref[...]))


def _even_inproj(til, x, mods, nw, w_main, w_gate, tabs, kt_tabs=None):
    rows_total = til.nb * til.r
    widths = (512, 512, 512, 512, 512, 512, 512, LANES)
    in_specs = [til.x_spec(D_MODEL), til.mod_spec(0), til.mod_spec(1), _full_spec((1, D_MODEL)),
                _const_spec(w_main.shape), _const_spec(w_gate.shape)] + [til.tab_spec()] * 4
    args = [x, mods, mods, nw, w_main, w_gate, *tabs]
    out_specs = [til.flat_spec(c) for c in widths]
    out_shape = [jax.ShapeDtypeStruct((rows_total, c), F32) for c in widths]
    if kt_tabs is not None:
        assert til.b == 1
        w_kt = jnp.transpose(w_main[:, 512:1024])
        in_specs += [_const_spec(w_kt.shape)] + [pl.BlockSpec((DK_A // 2, til.rt), lambda i, j: (0, j))] * 2
        args += [w_kt, *kt_tabs]
        out_specs[1] = pl.BlockSpec((1, 512, til.rt), lambda i, j: (i, 0, j))
        out_shape[1] = jax.ShapeDtypeStruct((til.nb, 512, til.r), F32)
    return pl.pallas_call(
        functools.partial(_even_inproj_kernel, k_feature_major=kt_tabs is not None),
        grid=til.grid,
        in_specs=in_specs,
        out_specs=out_specs,
        out_shape=out_shape,
        compiler_params=_cparams(("parallel", "parallel")),
        name="even_inproj",
    )(*args)


def _for_tiles(lo, hi, fn):
    n = hi - lo

    def pair(j, carry):
        fn(lo + 2 * j)
        fn(lo + 2 * j + 1)
        return carry

    lax.fori_loop(0, n // 2, pair, 0)

    @pl.when(n % 2 == 1)
    def _():
        fn(hi - 1)


def _row_sumsq(x):
    xf = x.astype(F32)
    return _dot((xf * xf).astype(BF16), jnp.ones((LANES, LANES), BF16))


def _key_norm_bound(k_ref, n_keys, tile):
    def body(kb, m):
        k = k_ref[0, pl.ds(pl.multiple_of(kb * tile, tile), tile), :].astype(BF16)
        return jnp.maximum(m, _row_sumsq(k))

    m = lax.fori_loop(0, n_keys // tile, body, jnp.zeros((tile, LANES), F32))
    return jnp.max(m, axis=0, keepdims=True)


def _score_bound(q2, kmax2):
    qmax2 = jnp.max(_row_sumsq(q2), axis=0, keepdims=True)
    return jnp.sqrt(qmax2 * kmax2) * 1.05


def _two_pass_attention(mx_ref, acc_ref, lo, hi, last, scores, values, mask_body=False, bound=None):
    def lane_max(s):
        m = s[:, 0:LANES]
        for c in range(1, s.shape[1] // LANES):
            m = jnp.maximum(m, s[:, c * LANES:(c + 1) * LANES])
        return m

    def pass1(kb, masked):
        mx_ref[...] = jnp.maximum(mx_ref[...], lane_max(scores(kb, masked)))

    def exact_max():
        mx_ref[...] = jnp.full(mx_ref.shape, MASK_NEG, F32)
        _for_tiles(lo, hi, lambda kb: pass1(kb, mask_body))
        for kb in last:
            pass1(kb, True)
        mx_ref[...] = jnp.broadcast_to(jnp.max(mx_ref[...], axis=-1, keepdims=True), mx_ref.shape)

    if bound is None:
        exact_max()
    else:
        safe = jnp.max(bound) <= SAFE_SHIFT

        @pl.when(safe)
        def _():
            mx_ref[...] = jnp.broadcast_to(bound, mx_ref.shape)

        @pl.when(jnp.logical_not(safe))
        def _():
            exact_max()

    acc_ref[...] = jnp.zeros(acc_ref.shape, F32)

    def pass2(kb, masked):
        s = scores(kb, masked)
        m = mx_ref[...]
        p = jnp.concatenate([jnp.exp2(s[:, c * LANES:(c + 1) * LANES] - m) for c in range(s.shape[1] // LANES)],
                            axis=-1).astype(BF16)
        v = values(kb)
        v1 = jnp.concatenate([v, jnp.ones(v.shape, BF16)], axis=-1)
        acc_ref[...] += _dot(p, v1)

    _for_tiles(lo, hi, lambda kb: pass2(kb, mask_body))
    for kb in last:
        pass2(kb, True)
    return acc_ref[:, 0:LANES] / acc_ref[:, LANES:2 * LANES]


def _diff_lambda(lw, lam_init):
    a = jnp.sum(lw[0:1] * lw[1:2], axis=-1, keepdims=True)
    b = jnp.sum(lw[2:3] * lw[3:4], axis=-1, keepdims=True)
    return jnp.exp(a) - jnp.exp(b) + lam_init


def _diff_finalize(o2, tq, lam, sub_w, lam_init):
    od = o2[:tq] - lam * o2[tq:]
    return _rms(od, sub_w) * (1.0 - lam_init)


def _diff_flash_kernel(q_ref, k_ref, v_ref, lamw_ref, sub_ref, o_ref, q2_ref, mx_ref, acc_ref, kmax_ref, *,
                       tq, tk, lam_init):
    i = pl.program_id(2)

    def key_tile(kb):
        return k_ref[0, :, pl.ds(pl.multiple_of(kb * tk, tk), tk)].astype(BF16)

    @pl.when(i == 0)
    def _():
        def body(kb, m):
            kf = key_tile(kb).astype(F32)
            return jnp.maximum(m, jnp.sum(kf * kf, axis=0, keepdims=True))

        m = lax.fori_loop(0, k_ref.shape[2] // tk, body, jnp.zeros((1, tk), F32))
        kmax_ref[...] = jnp.broadcast_to(jnp.max(m, axis=-1, keepdims=True), kmax_ref.shape)

    q = q_ref[0] * (DK_A ** -0.5 * LOG2E)
    lane = _iota((1, LANES), 1)
    q2_ref[0:tq] = jnp.where(lane < DK_A, q, 0.0).astype(BF16)
    q2_ref[tq:2 * tq] = jnp.where(lane >= DK_A, q, 0.0).astype(BF16)
    bound = _score_bound(q2_ref[...], kmax_ref[0:1])

    def scores(kb, masked):
        s = _dot(q2_ref[...], key_tile(kb))
        if masked:
            r = _iota((2 * tq, tk), 0)
            qp = i * tq + jnp.where(r >= tq, r - tq, r)
            s = jnp.where(kb * tk + _iota((2 * tq, tk), 1) <= qp, s, MASK_NEG)
        return s

    def values(kb):
        return v_ref[0, pl.ds(pl.multiple_of(kb * tk, tk), tk), :].astype(BF16)

    per = tq // tk
    o2 = _two_pass_attention(mx_ref, acc_ref, 0, i * per, [i * per + u for u in range(per)], scores, values,
                             bound=bound)
    o_ref[0] = _diff_finalize(o2, tq, _diff_lambda(lamw_ref[...], lam_init), sub_ref[...], lam_init)


def _diff_attn_prompt(qa, kat, va, lam_w, sub_w, lam_init, tq, tk):
    n, t, _ = qa.shape
    kern = functools.partial(_diff_flash_kernel, tq=tq, tk=tk, lam_init=lam_init)
    return pl.pallas_call(
        kern,
        grid=(n, H_A, t // tq),
        in_specs=[pl.BlockSpec((1, tq, LANES), lambda b, h, i: (b, i, h)),
                  pl.BlockSpec((1, LANES, t), lambda b, h, i: (b, h, 0)),
                  pl.BlockSpec((1, t, LANES), lambda b, h, i: (b, 0, h)),
                  _full_spec(lam_w.shape), _full_spec(sub_w.shape)],
        out_specs=pl.BlockSpec((1, tq, LANES), lambda b, h, i: (b, i, h)),
        out_shape=jax.ShapeDtypeStruct((n, t, H_A * DV_A), F32),
        scratch_shapes=[pltpu.VMEM((2 * tq, LANES), BF16), pltpu.VMEM((2 * tq, LANES), F32),
                        pltpu.VMEM((2 * tq, 2 * LANES), F32), pltpu.VMEM((8, LANES), F32)],
        compiler_params=_cparams(("parallel", "parallel", "arbitrary")),
        name="diff_attn_prompt",
    )(qa, kat, va, lam_w, sub_w)


def _sel_onehot(rows, first_block):
    blk = _iota((rows, LANES), 0) // SEL_BLOCK + first_block
    return jnp.where(blk == _iota((rows, LANES), 1), 1.0, 0.0).astype(BF16)


def _nsa_flash_kernel(*refs, tq, tk, use_sel):
    if use_sel:
        q_ref, selm_ref, k_ref, v_ref, o_ref, q2_ref, mx_ref, acc_ref, kmax_ref = refs
    else:
        q_ref, k_ref, v_ref, o_ref, q2_ref, mx_ref, acc_ref, kmax_ref = refs
    i = pl.program_id(2)

    @pl.when(i == 0)
    def _():
        kmax_ref[...] = jnp.broadcast_to(_key_norm_bound(k_ref, k_ref.shape[1], tk), kmax_ref.shape)

    q = q_ref[0] * (DH_B ** -0.5 * LOG2E)
    for r in range(R_B):
        q2_ref[r * tq:(r + 1) * tq, 0:LANES] = q[:, r * LANES:(r + 1) * LANES].astype(BF16)
        if use_sel:
            q2_ref[r * tq:(r + 1) * tq, LANES:2 * LANES] = selm_ref[0, 0]
    bound = _score_bound(q2_ref[:, 0:LANES], kmax_ref[0:1])

    def scores(kb, masked):
        k = k_ref[0, pl.ds(pl.multiple_of(kb * tk, tk), tk), :].astype(BF16)
        if use_sel:
            k = jnp.concatenate([k, _sel_onehot(tk, kb * (tk // SEL_BLOCK))], axis=-1)
        s = _dot_nt(q2_ref[...], k)
        if masked:
            r = _iota((R_B * tq, tk), 0)
            qp = i * tq + jnp.where(r >= tq, r - tq, r)
            kp = kb * tk + _iota((R_B * tq, tk), 1)
            ok = kp <= qp
            if not use_sel:
                ok = ok & (kp > qp - WINDOW)
            s = jnp.where(ok, s, MASK_NEG)
        return s

    def values(kb):
        return v_ref[0, pl.ds(pl.multiple_of(kb * tk, tk), tk), :].astype(BF16)

    per = tq // tk
    lo = 0 if use_sel else jnp.maximum(i * per - WINDOW // tk, 0)
    o = _two_pass_attention(mx_ref, acc_ref, lo, i * per, [i * per + u for u in range(per)], scores, values,
                            mask_body=not use_sel, bound=bound)
    for r in range(R_B):
        o_ref[0, :, r * LANES:(r + 1) * LANES] = o[r * tq:(r + 1) * tq]


def _nsa_attn_prompt(qb, kv, selm, tq, tk):
    n, t, _ = qb.shape
    use_sel = selm is not None
    kd = 2 * LANES if use_sel else LANES
    kern = functools.partial(_nsa_flash_kernel, tq=tq, tk=tk, use_sel=use_sel)
    in_specs = [pl.BlockSpec((1, tq, R_B * LANES), lambda b, g, i: (b, i, g))]
    args = [qb]
    if use_sel:
        in_specs.append(pl.BlockSpec((1, 1, tq, LANES), lambda b, g, i: (b, g, i, 0)))
        args.append(selm)
    in_specs += [pl.BlockSpec((1, t, LANES), lambda b, g, i: (b, 0, g)),
                 pl.BlockSpec((1, t, LANES), lambda b, g, i: (b, 0, G_B + g))]
    args += [kv, kv]
    return pl.pallas_call(
        kern,
        grid=(n, G_B, t // tq),
        in_specs=in_specs,
        out_specs=pl.BlockSpec((1, tq, R_B * LANES), lambda b, g, i: (b, i, g)),
        out_shape=jax.ShapeDtypeStruct((n, t, H_B * DH_B), F32),
        scratch_shapes=[pltpu.VMEM((R_B * tq, kd), BF16), pltpu.VMEM((R_B * tq, LANES), F32),
                        pltpu.VMEM((R_B * tq, 2 * LANES), F32), pltpu.VMEM((8, LANES), F32)],
        compiler_params=_cparams(("parallel", "parallel", "arbitrary")),
        name="nsa_sel_prompt" if use_sel else "nsa_win_prompt",
    )(*args)


def _compress_core(load, wc_ref, pe_ref, w2_ref, wi, nchunk):
    rows = G_B * nchunk
    acc = jnp.zeros((rows, 2 * LANES), F32)
    pew = jnp.zeros((16, 2 * LANES), F32)
    for u in range(CMP_STRIDE // 2):
        lhs = jnp.concatenate([load(2 * u), load(2 * u + 1)], axis=-1).astype(BF16)
        w = wc_ref[wi, u]
        acc = acc + _dot(lhs, w)
        pew = pew + _dot(pe_ref[wi, u], w)
    first = acc[:, :LANES]
    second = pltpu.roll(acc[:, LANES:], rows - 1, 0)
    hid = first + second + pew[0:1, :LANES] + pew[8:9, LANES:]
    hid = hid * jax.nn.sigmoid(hid)
    return _dot(hid.astype(BF16), w2_ref[wi])


def _compress_prompt_kernel(*refs, nchunk):
    x_refs = refs[:G_B]
    wc_ref, pe_ref, w2_ref, o_ref = refs[G_B:]

    def load(tok):
        return jnp.concatenate([xr[0, pl.ds(tok, nchunk, stride=CMP_STRIDE), :] for xr in x_refs], axis=0)

    out = _compress_core(load, wc_ref, pe_ref, w2_ref, 0, nchunk)
    for g in range(G_B):
        o_ref[0, 0, g] = out[g * nchunk:(g + 1) * nchunk]


def _compress_sample_kernel(pt_ref, *refs, n_pages, spb):
    pg_refs = refs[:spb * n_pages]
    wc_ref, pe_ref, w2_ref, o_ref = refs[spb * n_pages:]
    cps = PAGE_SIZE // CMP_STRIDE
    nchunk = n_pages * cps
    for kv in range(2):
        by_tok = [jnp.swapaxes(pg[0, pl.ds(kv * G_B + g, PAGE_SIZE, stride=SLOTS), :]
                               .reshape(cps, CMP_STRIDE, LANES), 0, 1)
                  for smp in range(spb) for g in range(G_B) for pg in pg_refs[smp * n_pages:(smp + 1) * n_pages]]

        def load(tok):
            return jnp.concatenate([x[tok] for x in by_tok], axis=0)

        out = _compress_core(load, wc_ref, pe_ref, w2_ref, kv, spb * nchunk)
        for smp in range(spb):
            for g in range(G_B):
                seg = smp * G_B + g
                o_ref[smp, kv, g] = out[seg * nchunk:(seg + 1) * nchunk]


def _compress_weights(cmp_pe, cmp_w1, cmp_w2):
    w1 = cmp_w1.reshape(2, CMP_LEN, DH_B, DH_B)
    wab = jnp.concatenate([w1[:, :CMP_STRIDE], w1[:, CMP_STRIDE:]], axis=-1)
    wc = wab.reshape(2, CMP_STRIDE // 2, 2 * DH_B, 2 * DH_B).astype(BF16)
    pa = cmp_pe[:, :CMP_STRIDE].reshape(2, CMP_STRIDE // 2, 1, 2 * DH_B)
    pb = cmp_pe[:, CMP_STRIDE:].reshape(2, CMP_STRIDE // 2, 1, 2 * DH_B)
    z = jnp.zeros((2, CMP_STRIDE // 2, 7, 2 * DH_B), F32)
    pe = jnp.concatenate([pa, z, pb, z], axis=2).astype(BF16)
    return wc, pe, cmp_w2.astype(BF16)


def _compress_prompt(cmp_kv, wc, pe, w2):
    n, t, _ = cmp_kv.shape
    nchunk = t // CMP_STRIDE
    kern = functools.partial(_compress_prompt_kernel, nchunk=nchunk)
    return pl.pallas_call(
        kern,
        grid=(n, 2),
        in_specs=[pl.BlockSpec((1, t, DH_B), functools.partial(lambda b, kv, g: (b, 0, kv * G_B + g), g=g))
                  for g in range(G_B)] + [
                  pl.BlockSpec((1,) + wc.shape[1:], lambda b, kv: (kv, 0, 0, 0)),
                  pl.BlockSpec((1,) + pe.shape[1:], lambda b, kv: (kv, 0, 0, 0)),
                  pl.BlockSpec((1, DH_B, DH_B), lambda b, kv: (kv, 0, 0))],
        out_specs=pl.BlockSpec((1, 1, G_B, nchunk, DH_B), lambda b, kv: (b, kv, 0, 0, 0)),
        out_shape=jax.ShapeDtypeStruct((n, 2, G_B, nchunk, DH_B), F32),
        compiler_params=_cparams(("parallel", "parallel")),
        name="compress_prompt",
    )(*([cmp_kv] * G_B), wc, pe, w2)


def _compress_sample(cache, page_table, wc, pe, w2):
    n, n_pages = page_table.shape
    nchunk = n_pages * (PAGE_SIZE // CMP_STRIDE)
    spb = math.gcd(n, 2)
    kern = functools.partial(_compress_sample_kernel, n_pages=n_pages, spb=spb)
    page_specs = _page_specs(n_pages, PAGE_SIZE * SLOTS, LANES, spb)
    grid_spec = pltpu.PrefetchScalarGridSpec(
        num_scalar_prefetch=1,
        grid=(n // spb,),
        in_specs=page_specs + [
            pl.BlockSpec(wc.shape, lambda b, pt: (0, 0, 0, 0)),
            pl.BlockSpec(pe.shape, lambda b, pt: (0, 0, 0, 0)),
            pl.BlockSpec(w2.shape, lambda b, pt: (0, 0, 0))],
        out_specs=pl.BlockSpec((spb, 2, G_B, nchunk, DH_B), lambda b, pt: (b, 0, 0, 0, 0)),
    )
    return pl.pallas_call(
        kern,
        grid_spec=grid_spec,
        out_shape=jax.ShapeDtypeStruct((n, 2, G_B, nchunk, DH_B), F32),
        compiler_params=_cparams(("parallel",)),
        name="compress_sample",
    )(page_table, *([cache] * (spb * n_pages)), wc, pe, w2)


def _cmp_topk_kernel(q_ref, kc_ref, vc_ref, ocmp_ref, selm_ref, p_scr, v_scr, *,
                     nb, rq, nck, n_cmp, n_sel, nsp, pos_base):
    tq = nb * rq
    j = pl.program_id(2)
    scale = DH_B ** -0.5
    kidx = _iota((1, nck), 1)
    qpos_c = pos_base + j * rq + _iota((rq, 1), 0)
    valid = (kidx < n_cmp) & (CMP_STRIDE * kidx + (CMP_LEN - 1) <= qpos_c)
    for b in range(nb):
        kcc = kc_ref[b, 0, 0].astype(BF16)
        vcc = vc_ref[b, 0, 0].astype(BF16)
        q = q_ref[b] * scale
        q2 = jnp.concatenate([q[:, r * LANES:(r + 1) * LANES] for r in range(R_B)], axis=0).astype(BF16)
        valid2 = jnp.concatenate([valid] * R_B, axis=0)
        s = jnp.where(valid2, _dot_nt(q2, kcc), MASK_NEG)
        m = jnp.max(s, axis=-1, keepdims=True)
        p = jnp.where(valid2, jnp.exp(s - m), 0.0)
        p = p / jnp.maximum(jnp.sum(p, axis=-1, keepdims=True), 1e-30)
        o = _dot(p.astype(BF16), vcc)
        psum = jnp.zeros((rq, nck), F32)
        for r in range(R_B):
            ocmp_ref[b, :, r * LANES:(r + 1) * LANES] = o[r * rq:(r + 1) * rq]
            psum = psum + p[r * rq:(r + 1) * rq]
        p_scr[b * rq:(b + 1) * rq, :] = psum
    psum = p_scr[...]
    p_hi = psum.astype(BF16)
    p_lo = (psum - p_hi.astype(F32)).astype(BF16)
    srow = _iota((LANES, nck), 0)
    kcol = _iota((LANES, nck), 1)
    cov = (CMP_STRIDE * kcol < SEL_BLOCK * srow + SEL_BLOCK) & (CMP_STRIDE * kcol + CMP_LEN > SEL_BLOCK * srow)
    cov = cov & (srow < n_sel) & (kcol < n_cmp)
    cov_t = jnp.where(cov, 1.0, 0.0).astype(BF16)
    imp_t = _dot_nt(cov_t, p_hi) + _dot_nt(cov_t, p_lo)
    qpos_l = pos_base + j * rq + _iota((1, tq), 1) % rq
    sb = _iota((nsp, 1), 0)
    valid_b = (sb < n_sel) & (sb * SEL_BLOCK <= qpos_l)
    forced = (sb == 0) | (sb == qpos_l // SEL_BLOCK)
    vals = jnp.where(forced, jnp.inf, jnp.where(valid_b, imp_t[:nsp], -jnp.inf))
    v_scr[...] = vals

    def body(jj, cnt):
        vj = v_scr[pl.ds(jj, 1), :]
        tie = jnp.where(sb > jj, 1.0, 0.0)
        return cnt + jnp.where(vj > vals, 1.0, jnp.where(vj == vals, tie, 0.0))

    last_pos = pos_base + (j + 1) * rq - 1
    n_rank = jnp.minimum(last_pos // SEL_BLOCK + 1, nsp)
    cnt = lax.fori_loop(0, n_rank, body, jnp.zeros((nsp, tq), F32))
    keep = valid_b & (cnt < float(min(TOP_N, n_sel)))
    selm_t = jnp.where(keep, 0.0, -SEL_NEG)
    if nsp < LANES:
        selm_t = jnp.concatenate([selm_t, jnp.zeros((LANES - nsp, tq), F32)], axis=0)
    selm = selm_t.T.astype(BF16)
    for b in range(nb):
        selm_ref[b, 0] = selm[b * rq:(b + 1) * rq]


def _cmp_topk(qb, ccmp, *, nb, rq, n_cmp, n_sel, pos_base):
    n, r, _ = qb.shape
    nck = ccmp.shape[3]
    nsp = -(-n_sel // 8) * 8
    tq = nb * rq
    kern = functools.partial(_cmp_topk_kernel, nb=nb, rq=rq, nck=nck, n_cmp=n_cmp, n_sel=n_sel, nsp=nsp,
                             pos_base=pos_base)
    return pl.pallas_call(
        kern,
        grid=(n // nb, G_B, r // rq),
        in_specs=[pl.BlockSpec((nb, rq, R_B * LANES), lambda a, g, j: (a, j, g)),
                  pl.BlockSpec((nb, 1, 1, nck, DH_B), lambda a, g, j: (a, 0, g, 0, 0)),
                  pl.BlockSpec((nb, 1, 1, nck, DH_B), lambda a, g, j: (a, 1, g, 0, 0))],
        out_specs=[pl.BlockSpec((nb, rq, R_B * LANES), lambda a, g, j: (a, j, g)),
                   pl.BlockSpec((nb, 1, rq, LANES), lambda a, g, j: (a, g, j, 0))],
        out_shape=[jax.ShapeDtypeStruct((n, r, H_B * DH_B), F32),
                   jax.ShapeDtypeStruct((n, G_B, r, LANES), BF16)],
        scratch_shapes=[pltpu.VMEM((tq, nck), F32), pltpu.VMEM((nsp, tq), F32)],
        compiler_params=_cparams(("parallel", "parallel", "parallel")),
        name="cmp_topk",
    )(qb, ccmp, ccmp)


def _attend_scores(pieces):
    m = None
    for s, _ in pieces:
        ms = jnp.max(s, axis=-1, keepdims=True)
        m = ms if m is None else jnp.maximum(m, ms)
    l = jnp.zeros_like(m)
    acc = jnp.zeros((m.shape[0], LANES), F32)
    for s, v in pieces:
        p = jnp.exp(s - m)
        l = l + jnp.sum(p, axis=-1, keepdims=True)
        acc = acc + _dot(p.astype(BF16), v)
    return acc / l


def _attend_pieces(q2, pieces):
    scored = []
    for k, v, mask in pieces:
        s = _dot_nt(q2, k)
        scored.append((s if mask is None else jnp.where(mask, s, MASK_NEG), v))
    return _attend_scores(scored)


def _pad_rows(x, rows):
    return jnp.concatenate([x, jnp.zeros((rows - x.shape[0], x.shape[1]), x.dtype)], axis=0)


def _new_token_mask(m_rows, s):
    r = _iota((m_rows, LANES), 0) % s
    return _iota((m_rows, LANES), 1) <= r


def _diff_sample_kernel(pt_ref, *refs, n_pages, s, spb, lam_init):
    kp_refs = refs[:spb * n_pages]
    vp_refs = refs[spb * n_pages:2 * spb * n_pages]
    q_ref, kn_ref, vn_ref, lamw_ref, sub_ref, o_ref = refs[2 * spb * n_pages:]
    lane = _iota((1, LANES), 1)
    lam = _diff_lambda(lamw_ref[...], lam_init)
    new_mask = _new_token_mask(2 * s, s)
    for b in range(spb):
        kpages = kp_refs[b * n_pages:(b + 1) * n_pages]
        vpages = vp_refs[b * n_pages:(b + 1) * n_pages]
        for h in range(H_A):
            cols = slice(h * LANES, (h + 1) * LANES)
            q = q_ref[b, :, cols] * (DK_A ** -0.5)
            q2 = jnp.concatenate([jnp.where(lane < DK_A, q, 0.0), jnp.where(lane >= DK_A, q, 0.0)],
                                 axis=0).astype(BF16)
            kt_past = jnp.concatenate([r[0, cols, :].astype(BF16) for r in kpages], axis=1)
            s_past = _dot(q2, kt_past)
            vpast = jnp.concatenate([_slot_rows(r, h, PAGE_SIZE).astype(BF16) for r in vpages], axis=0)
            knew = _pad_rows(kn_ref[b, :, cols], LANES).astype(BF16)
            vnew = _pad_rows(vn_ref[b, :, cols], LANES).astype(BF16)
            s_new = jnp.where(new_mask, _dot_nt(q2, knew), MASK_NEG)
            o2 = _attend_scores([(s_past, vpast), (s_new, vnew)])
            o_ref[b, :, cols] = _diff_finalize(o2, s, lam, sub_ref[...], lam_init)


def _page_specs(n_pages, rows, width, spb=1):
    return [pl.BlockSpec((1, rows, width),
                         functools.partial(lambda b, pt, smp, j: (pt[b * spb + smp, j], 0, 0), smp=smp, j=j))
            for smp in range(spb) for j in range(n_pages)]


def _slot_rows(ref, slot, n_tok):
    return ref[0, pl.ds(slot, n_tok, stride=SLOTS), :]


def _diff_attn_sample(qa, ka, va, cache_k, cache_v, page_table, lam_w, sub_w, lam_init):
    n, s, _ = qa.shape
    n_pages = page_table.shape[1]
    spb = math.gcd(n, 2)
    kern = functools.partial(_diff_sample_kernel, n_pages=n_pages, s=s, spb=spb, lam_init=lam_init)
    row_spec = pl.BlockSpec((spb, s, 512), lambda b, pt: (b, 0, 0))
    grid_spec = pltpu.PrefetchScalarGridSpec(
        num_scalar_prefetch=1,
        grid=(n // spb,),
        in_specs=_page_specs(n_pages, 512, PAGE_SIZE, spb) + _page_specs(n_pages, PAGE_SIZE * SLOTS, LANES, spb)
        + [row_spec, row_spec, row_spec,
           pl.BlockSpec(lam_w.shape, lambda b, pt: (0, 0)), pl.BlockSpec(sub_w.shape, lambda b, pt: (0, 0))],
        out_specs=row_spec,
    )
    return pl.pallas_call(
        kern,
        grid_spec=grid_spec,
        out_shape=jax.ShapeDtypeStruct((n, s, 512), F32),
        compiler_params=_cparams(("parallel",)),
        name="diff_attn_sample",
    )(page_table, *([cache_k] * (spb * n_pages)), *([cache_v] * (spb * n_pages)), qa, ka, va, lam_w, sub_w)


def _sel_sample_kernel(pt_ref, *refs, n_pages, s, spb):
    pg_refs = refs[:spb * n_pages]
    q_ref, selm_ref, new_ref, o_ref = refs[spb * n_pages:]
    past = n_pages * PAGE_SIZE
    new_mask = _new_token_mask(R_B * s, s)
    oh_past = _sel_onehot(past, 0)
    new_blk = past // SEL_BLOCK
    oh_new = jnp.where((_iota((LANES, LANES), 1) == new_blk) & (_iota((LANES, LANES), 0) < s), 1.0, 0.0)
    oh_new = oh_new.astype(BF16)
    scale = DH_B ** -0.5
    for b in range(spb):
        pages = pg_refs[b * n_pages:(b + 1) * n_pages]
        for g in range(G_B):
            kcols = slice(g * LANES, (g + 1) * LANES)
            vcols = slice((G_B + g) * LANES, (G_B + g + 1) * LANES)
            selm = selm_ref[b, g]
            q2 = jnp.concatenate(
                [jnp.concatenate([(q_ref[b, :, (g * R_B + r) * LANES:(g * R_B + r + 1) * LANES] * scale).astype(BF16),
                                  selm], axis=-1) for r in range(R_B)], axis=0)
            kpast = jnp.concatenate([_slot_rows(r, g, PAGE_SIZE).astype(BF16) for r in pages], axis=0)
            vpast = jnp.concatenate([_slot_rows(r, G_B + g, PAGE_SIZE).astype(BF16) for r in pages], axis=0)
            kpast = jnp.concatenate([kpast, oh_past], axis=-1)
            knew = jnp.concatenate([_pad_rows(new_ref[b, :, kcols], LANES).astype(BF16), oh_new], axis=-1)
            vnew = _pad_rows(new_ref[b, :, vcols], LANES).astype(BF16)
            o2 = _attend_pieces(q2, [(kpast, vpast, None), (knew, vnew, new_mask)])
            for r in range(R_B):
                o_ref[b, :, (g * R_B + r) * LANES:(g * R_B + r + 1) * LANES] = o2[r * s:(r + 1) * s]


def _sel_attn_sample(qb, selm, sel_new, cache, page_table):
    n, s, _ = qb.shape
    n_pages = page_table.shape[1]
    assert s <= SEL_BLOCK and (n_pages * PAGE_SIZE) % SEL_BLOCK == 0
    spb = math.gcd(n, 2)
    kern = functools.partial(_sel_sample_kernel, n_pages=n_pages, s=s, spb=spb)
    row_spec = pl.BlockSpec((spb, s, 512), lambda b, pt: (b, 0, 0))
    page_specs = _page_specs(n_pages, PAGE_SIZE * SLOTS, LANES, spb)
    grid_spec = pltpu.PrefetchScalarGridSpec(
        num_scalar_prefetch=1,
        grid=(n // spb,),
        in_specs=page_specs
        + [row_spec, pl.BlockSpec((spb, G_B, s, LANES), lambda b, pt: (b, 0, 0, 0)), row_spec],
        out_specs=row_spec,
    )
    return pl.pallas_call(
        kern,
        grid_spec=grid_spec,
        out_shape=jax.ShapeDtypeStruct((n, s, 512), F32),
        compiler_params=_cparams(("parallel",)),
        name="nsa_sel_sample",
    )(page_table, *([cache] * (spb * n_pages)), qb, selm, sel_new)


def _win_sample_kernel(q_ref, buf_ref, new_ref, o_ref, wout_ref, *, s, wb, spb):
    new_mask = _new_token_mask(R_B * s, s)
    r = _iota((R_B * s, wb), 0) % s
    buf_mask = _iota((R_B * s, wb), 1) > r + (wb - WINDOW)
    scale = DH_B ** -0.5
    keep = (wb - s) * SLOTS
    for b in range(spb):
        for g in range(G_B):
            kcols = slice(g * LANES, (g + 1) * LANES)
            vcols = slice((G_B + g) * LANES, (G_B + g + 1) * LANES)
            q2 = jnp.concatenate(
                [(q_ref[b, :, (g * R_B + r_) * LANES:(g * R_B + r_ + 1) * LANES] * scale).astype(BF16)
                 for r_ in range(R_B)], axis=0)
            kbuf = buf_ref[b, pl.ds(g, wb, stride=SLOTS), :].astype(BF16)
            vbuf = buf_ref[b, pl.ds(G_B + g, wb, stride=SLOTS), :].astype(BF16)
            knew = _pad_rows(new_ref[b, :, kcols], LANES).astype(BF16)
            vnew = _pad_rows(new_ref[b, :, vcols], LANES).astype(BF16)
            o2 = _attend_pieces(q2, [(kbuf, vbuf, buf_mask), (knew, vnew, new_mask)])
            for r_ in range(R_B):
                o_ref[b, :, (g * R_B + r_) * LANES:(g * R_B + r_ + 1) * LANES] = o2[r_ * s:(r_ + 1) * s]
        wout_ref[b, 0:keep] = buf_ref[b, s * SLOTS:wb * SLOTS]
        for slot in range(SLOTS):
            wout_ref[b, pl.ds(keep + slot, s, stride=SLOTS), :] = new_ref[b, :, slot * LANES:(slot + 1) * LANES]


def _win_attn_sample(qb, win_buf, win_new, past):
    n, s, _ = qb.shape
    wb = win_buf.shape[1] // SLOTS
    assert past >= wb and wb % 8 == 0 and s % 8 == 0
    spb = math.gcd(n, 4)
    kern = functools.partial(_win_sample_kernel, s=s, wb=wb, spb=spb)
    row_spec = pl.BlockSpec((spb, s, 512), lambda b: (b, 0, 0))
    buf_spec = pl.BlockSpec((spb, wb * SLOTS, LANES), lambda b: (b, 0, 0))
    return pl.pallas_call(
        kern,
        grid=(n // spb,),
        in_specs=[row_spec, buf_spec, row_spec],
        out_specs=[row_spec, buf_spec],
        out_shape=[jax.ShapeDtypeStruct((n, s, 512), F32), jax.ShapeDtypeStruct((n, wb * SLOTS, LANES), F32)],
        compiler_params=_cparams(("parallel",)),
        name="nsa_win_sample",
    )(qb, win_buf, win_new)


def _mlp_value(y, sh, sc, g, nw, w1_ref, w2_ref, fw, final):
    h = _norm_mod(y, nw, sh, sc).reshape(-1, D_MODEL).astype(BF16)
    a = jnp.maximum(_dot(h, w1_ref[...]), 0.0)
    out = _dot((a * a).astype(BF16), w2_ref[...])
    y2 = y + g * out.reshape(y.shape)
    return _rms(y2, fw) if final else y2


def _mlp_specs(til, mlp):
    nw, w1, w2, fw, _ = mlp
    specs = [til.mod_spec(3), til.mod_spec(4), til.mod_spec(5), _full_spec((1, D_MODEL)),
             _const_spec(w1.shape), _const_spec(w2.shape), _full_spec((1, D_MODEL))]
    return specs, [nw, w1, w2, fw]


def _even_out_kernel(y_ref, g1_ref, oa_ref, oc_ref, os_ref, ow_ref, gt_ref, w_ref,
                     sh_ref, sc_ref, g2_ref, nw_ref, w1_ref, w2_ref, fw_ref, o_ref, *, final):
    gt = gt_ref[...]
    parts = [oa_ref[...].astype(BF16)]
    for hb in range(H_B):
        sl = slice(hb * LANES, (hb + 1) * LANES)
        ob = (gt[:, 3 * hb:3 * hb + 1] * oc_ref[:, sl] + gt[:, 3 * hb + 1:3 * hb + 2] * os_ref[:, sl]
              + gt[:, 3 * hb + 2:3 * hb + 3] * ow_ref[:, sl])
        parts.append(ob.astype(BF16))
    out = _dot(jnp.concatenate(parts, axis=-1), w_ref[...])
    y = y_ref[...]
    y1 = y + g1_ref[0] * out.reshape(y.shape)
    o_ref[...] = _mlp_value(y1, sh_ref[0], sc_ref[0], g2_ref[0], nw_ref[...], w1_ref, w2_ref, fw_ref[...], final)


def _even_out(til, y, mods, oa, oc, os_, ow, gt, w_out, mlp):
    mlp_specs, mlp_args = _mlp_specs(til, mlp)
    return pl.pallas_call(
        functools.partial(_even_out_kernel, final=mlp[4]),
        grid=til.grid,
        in_specs=[til.x_spec(D_MODEL), til.mod_spec(2), til.flat_spec(512), til.flat_spec(512),
                  til.flat_spec(512), til.flat_spec(512), til.flat_spec(LANES), _const_spec(w_out.shape)] + mlp_specs,
        out_specs=til.x_spec(D_MODEL),
        out_shape=jax.ShapeDtypeStruct(y.shape, F32),
        compiler_params=_cparams(("parallel", "parallel")),
        name="even_out_mlp",
    )(y, mods, oa, oc, os_, ow, gt, w_out, mods, mods, mods, *mlp_args)


def _gla_inproj_kernel(x_ref, sh_ref, sc_ref, nw_ref, w_ref, wgl_ref, wgate_ref, bgate_ref,
                       q_ref, k_ref, v_ref, r_ref, la_ref):
    h = _norm_mod(x_ref[...], nw_ref[...], sh_ref[0], sc_ref[0]).reshape(-1, D_MODEL).astype(BF16)
    nk = H_C * DK_C
    nv = H_C * DV_C
    q_ref[...] = _dot(h, w_ref[:, 0:nk]) * (DK_C ** -0.5)
    k_ref[...] = _dot(h, w_ref[:, nk:2 * nk])
    v_ref[...] = _dot(h, w_ref[:, 2 * nk:2 * nk + nv])
    r_ref[...] = _dot(h, w_ref[:, 2 * nk + nv:2 * nk + 2 * nv])
    gl = _dot(h, wgl_ref[...])
    x = _dot(gl.astype(BF16), wgate_ref[...]) + bgate_ref[...]
    log_sig = jnp.minimum(x, 0.0) - jnp.log1p(jnp.exp(-jnp.abs(x)))
    la_ref[...] = log_sig / GATE_TAU


def _gla_inproj(til, x, mods, nw, w_main, w_gl, w_gate, b_gate):
    rows_total = til.nb * til.r
    widths = (H_C * DK_C, H_C * DK_C, H_C * DV_C, H_C * DV_C, H_C * DK_C)
    return pl.pallas_call(
        _gla_inproj_kernel,
        grid=til.grid,
        in_specs=[til.x_spec(D_MODEL), til.mod_spec(0), til.mod_spec(1), _full_spec((1, D_MODEL)),
                  _const_spec(w_main.shape), _const_spec(w_gl.shape), _const_spec(w_gate.shape),
                  _full_spec(b_gate.shape)],
        out_specs=[til.flat_spec(c) for c in widths],
        out_shape=[jax.ShapeDtypeStruct((rows_total, c), F32) for c in widths],
        compiler_params=_cparams(("parallel", "parallel")),
        name="gla_inproj",
    )(x, mods, mods, nw, w_main, w_gl, w_gate, b_gate)


def _cumsum_rows(g):
    c = g.shape[0]
    row = _iota((c, 1), 0)
    b = g
    shift = 1
    while shift < c:
        b = b + jnp.where(row >= shift, pltpu.roll(b, shift, 0), 0.0)
        shift *= 2
    return b


def _gla_prep(q, k, g, c, sub, n_real):
    b = _cumsum_rows(g)
    qe = (q * jnp.exp(b)).astype(BF16)
    lane_c = _iota((sub, c), 1)
    row_s = _iota((sub, c), 0)
    att_rows = []
    for blk in range(c // sub):
        lo = blk * sub
        qi, ki, bi = q[lo:lo + sub], k[lo:lo + sub], b[lo:lo + sub]
        diag = jnp.zeros((sub, c), F32)
        for jj in range(min(sub, max(n_real - lo, 0))):
            e = jnp.exp(jnp.minimum(bi - bi[jj:jj + 1], 0.0))
            col = jnp.sum(qi * ki[jj:jj + 1] * e, axis=-1, keepdims=True)
            diag = jnp.where(lane_c == lo + jj, col, diag)
        att = jnp.where(lane_c - lo <= row_s, diag, 0.0)
        if blk > 0:
            bs = b[lo - 1:lo]
            q_in = qi * jnp.exp(bi - bs)
            k_out = k * jnp.exp(jnp.minimum(bs - b, 0.0))
            att = jnp.where(lane_c < lo, _dot_nt(q_in.astype(BF16), k_out.astype(BF16)), att)
        att_rows.append(att)
    att = att_rows[0] if len(att_rows) == 1 else jnp.concatenate(att_rows, axis=0)
    bl = b[c - 1:c]
    kd = (k * jnp.exp(bl - b)).astype(BF16)
    eye = _iota((DK_C, DK_C), 0) == _iota((DK_C, DK_C), 1)
    decay = jnp.sum(jnp.where(eye, jnp.exp(bl), 0.0), axis=-1, keepdims=True)
    return qe, att, kd, decay


def _gla_prep_bounded(q, k, g, c, sub, n_real):
    del sub, n_real
    b = _cumsum_rows(g)
    r = b[0:1]
    bl = b[c - 1:c]
    q_in = q * jnp.exp(b - r)
    k_out = k * jnp.exp(r - b)
    att = jnp.where(_iota((c, c), 0) >= _iota((c, c), 1), _dot_nt(q_in.astype(BF16), k_out.astype(BF16)), 0.0)
    qe = (q_in * jnp.exp(r)).astype(BF16)
    kd = (k_out * jnp.exp(bl - r)).astype(BF16)
    eye = _iota((DK_C, DK_C), 0) == _iota((DK_C, DK_C), 1)
    decay = jnp.sum(jnp.where(eye, jnp.exp(bl), 0.0), axis=-1, keepdims=True)
    return qe, att, kd, decay


def _gla_apply(state, prep, v):
    qe, att, kd, decay = prep
    o = _dot(qe, state.astype(BF16)) + _dot(att, v)
    return o, decay * state + _dot_tn(kd, v.astype(BF16))


def _gla_rec_kernel(*refs, tt, c, sub, hp, nseq, has_s0):
    if has_s0:
        q_ref, k_ref, v_ref, g_ref, s0_ref, o_ref, sfin_ref, s_ref = refs
    else:
        q_ref, k_ref, v_ref, g_ref, o_ref, sfin_ref, s_ref = refs
    t = pl.program_id(2)

    @pl.when(t == 0)
    def _():
        for bi in range(nseq):
            for hh in range(hp):
                s_ref[bi * hp + hh] = s0_ref[bi, hh] if has_s0 else jnp.zeros((DK_C, DV_C), F32)

    def kcols(hh):
        return slice(hh * DK_C, (hh + 1) * DK_C)

    def vcols(hh):
        return slice(hh * DV_C, (hh + 1) * DV_C)

    def run(prep_fn):
        if tt < c:
            pad = lambda x: _pad_rows(x, c)
            for bi in range(nseq):
                for hh in range(hp):
                    prep = prep_fn(pad(q_ref[bi, :, kcols(hh)]), pad(k_ref[bi, :, kcols(hh)]),
                                   pad(g_ref[bi, :, kcols(hh)]), c, sub, tt)
                    si = bi * hp + hh
                    o, s_ref[si] = _gla_apply(s_ref[si], prep, pad(v_ref[bi, :, vcols(hh)]))
                    o_ref[bi, :, vcols(hh)] = o[:tt]
            return
        assert nseq == 1
        per_trip = 2 if (tt // c) % 2 == 0 else 1

        def body(ci, carry):
            for hh in range(hp):
                rows = [pl.ds(pl.multiple_of((ci * per_trip + u) * c, c), c) for u in range(per_trip)]
                preps = [prep_fn(q_ref[0, r, kcols(hh)], k_ref[0, r, kcols(hh)], g_ref[0, r, kcols(hh)],
                                 c, sub, c) for r in rows]
                state = s_ref[hh]
                for r, prep in zip(rows, preps):
                    o_ref[0, r, vcols(hh)], state = _gla_apply(state, prep, v_ref[0, r, vcols(hh)])
                s_ref[hh] = state
            return carry
        lax.fori_loop(0, tt // c // per_trip, body, 0)

    if tt > c:
        g_all = g_ref[0]
        chunk_decay = -jnp.sum(g_all.reshape(tt // c, c, g_all.shape[-1]), axis=1)
    else:
        chunk_decay = -jnp.sum(g_ref[...], axis=1)
    bounded = jnp.max(chunk_decay) <= GLA_SAFE_DECAY

    @pl.when(bounded)
    def _():
        run(_gla_prep_bounded)

    @pl.when(jnp.logical_not(bounded))
    def _():
        run(_gla_prep)

    @pl.when(t == pl.num_programs(2) - 1)
    def _():
        for bi in range(nseq):
            for hh in range(hp):
                sfin_ref[bi, hh] = s_ref[bi * hp + hh]


def _gla_recurrence(q, k, v, g, s0, tt, c, sub, hp):
    n, t, _ = q.shape
    nseq = math.gcd(n, 2) if tt < c else 1
    kern = functools.partial(_gla_rec_kernel, tt=tt, c=c, sub=sub, hp=hp, nseq=nseq, has_s0=s0 is not None)
    kspec = pl.BlockSpec((nseq, tt, hp * DK_C), lambda b, h, i: (b, i, h))
    vspec = pl.BlockSpec((nseq, tt, hp * DV_C), lambda b, h, i: (b, i, h))
    sspec = pl.BlockSpec((nseq, hp, DK_C, DV_C), lambda b, h, i: (b, h, 0, 0))
    in_specs = [kspec, kspec, vspec, kspec]
    args = [q, k, v, g]
    if s0 is not None:
        in_specs.append(sspec)
        args.append(s0)
    return pl.pallas_call(
        kern,
        grid=(n // nseq, H_C // hp, t // tt),
        in_specs=in_specs,
        out_specs=[vspec, sspec],
        out_shape=[jax.ShapeDtypeStruct((n, t, H_C * DV_C), F32),
                   jax.ShapeDtypeStruct((n, H_C, DK_C, DV_C), F32)],
        scratch_shapes=[pltpu.VMEM((nseq * hp, DK_C, DV_C), F32)],
        compiler_params=_cparams(("parallel", "parallel", "arbitrary")),
        name="gla_recurrence",
    )(*args)


def _gla_out_kernel(y_ref, g1_ref, o_ref_in, r_ref, gnw_ref, w_ref,
                    sh_ref, sc_ref, g2_ref, nw_ref, w1_ref, w2_ref, fw_ref, out_ref, *, final):
    parts = []
    for h in range(H_C):
        sl = slice(h * DV_C, (h + 1) * DV_C)
        r = r_ref[:, sl]
        parts.append((_rms(o_ref_in[:, sl], gnw_ref[...]) * (r * jax.nn.sigmoid(r))).astype(BF16))
    out = _dot(jnp.concatenate(parts, axis=-1), w_ref[...])
    y = y_ref[...]
    y1 = y + g1_ref[0] * out.reshape(y.shape)
    out_ref[...] = _mlp_value(y1, sh_ref[0], sc_ref[0], g2_ref[0], nw_ref[...], w1_ref, w2_ref, fw_ref[...], final)


def _gla_out(til, y, mods, o, r, gnw, w_out, mlp):
    mlp_specs, mlp_args = _mlp_specs(til, mlp)
    return pl.pallas_call(
        functools.partial(_gla_out_kernel, final=mlp[4]),
        grid=til.grid,
        in_specs=[til.x_spec(D_MODEL), til.mod_spec(2), til.flat_spec(H_C * DV_C), til.flat_spec(H_C * DV_C),
                  _full_spec(gnw.shape), _const_spec(w_out.shape)] + mlp_specs,
        out_specs=til.x_spec(D_MODEL),
        out_shape=jax.ShapeDtypeStruct(y.shape, F32),
        compiler_params=_cparams(("parallel", "parallel")),
        name="gla_out_mlp",
    )(y, mods, o, r, gnw, w_out, mods, mods, mods, *mlp_args)


def _rope_tables(pos, d):
    inv = ROPE_THETA ** (-jnp.arange(0, d, 2, dtype=F32) / d)
    ang = pos.astype(F32)[:, None] * inv[None, :]
    cos, sin = jnp.cos(ang), jnp.sin(ang)
    rep = LANES // d
    c = jnp.tile(jnp.concatenate([cos, cos], axis=-1), (1, rep))
    s = jnp.tile(jnp.concatenate([-sin, sin], axis=-1), (1, rep))
    return c, s


def _rope_tables_t(pos, d):
    inv = ROPE_THETA ** (-jnp.arange(0, d, 2, dtype=F32) / d)
    ang = inv[:, None] * pos.astype(F32)[None, :]
    return jnp.cos(ang), jnp.sin(ang)


def _mods_for(mod_l, lo, hi):
    nb = hi - lo
    return mod_l[lo:hi].reshape(nb, 6, D_MODEL).transpose(1, 0, 2).reshape(6, nb, 1, D_MODEL)


def _prompt_tile_rows(t):
    return math.gcd(t, 256)


def _even_layer(yp, ys, mods_p, mods_s, caches, page_table, wts, lam_init, til_p, til_s, mlp):
    (c_dk, c_dv, c_cmp, c_sel, win_buf) = caches
    n, t, _ = yp.shape
    ns, s, _ = ys.shape
    n_pages = page_table.shape[1]
    past = n_pages * PAGE_SIZE
    w_in = wts["w_in"]
    n_main = w_in.shape[1] - 3 * H_B
    w_main = w_in[:, :n_main].astype(BF16)
    w_gate = jnp.pad(w_in[:, n_main:], ((0, 0), (0, LANES - 3 * H_B))).astype(BF16)
    nw1 = wts["norm1"].reshape(1, D_MODEL)
    lam_w = wts["lam_w"]
    sub_w = wts["sub_w"].reshape(1, DV_A)
    wc, pe, w2c = _compress_weights(wts["cmp_pe"], wts["cmp_w1"], wts["cmp_w2"])
    w_out = wts["w_out"].astype(BF16)

    tabs_p = _rope_tables(jnp.arange(t), DK_A) + _rope_tables(jnp.arange(t), DH_B)
    kt_tabs = _rope_tables_t(jnp.arange(t), DK_A)
    qa, kat, va, qb, cmp_kv, sel_kv, win_kv, gt = _even_inproj(til_p, yp, mods_p, nw1, w_main, w_gate, tabs_p, kt_tabs)
    r3 = lambda a: a.reshape(n, t, a.shape[-1])
    tq = _prompt_tile_rows(t)
    tq_attn = math.gcd(t, 2 * tq)
    oa = _diff_attn_prompt(r3(qa), kat, r3(va), lam_w, sub_w, lam_init, tq_attn, tq)
    ccmp = _compress_prompt(r3(cmp_kv), wc, pe, w2c)
    n_cmp = (t - CMP_LEN) // CMP_STRIDE + 1
    n_sel = -(-t // SEL_BLOCK)
    o_cmp, selm = _cmp_topk(r3(qb), ccmp, nb=1, rq=tq, n_cmp=n_cmp, n_sel=n_sel, pos_base=0)
    o_sel = _nsa_attn_prompt(r3(qb), r3(sel_kv), selm, tq_attn, tq)
    o_win = _nsa_attn_prompt(r3(qb), r3(win_kv), None, tq_attn, tq)
    f2 = lambda a: a.reshape(n * t, a.shape[-1])
    yp = _even_out(til_p, yp, mods_p, f2(oa), f2(o_cmp), f2(o_sel), f2(o_win), gt, w_out, mlp)
    wl = min(WINDOW, t)
    ka_state = jnp.transpose(kat.reshape(n, H_A, 2, DK_A, t), (0, 4, 1, 2, 3))
    st_p = (ka_state, r3(va).reshape(n, t, H_A, DV_A),
            r3(cmp_kv).reshape(n, t, 2, G_B, DH_B), r3(sel_kv).reshape(n, t, 2, G_B, DH_B),
            r3(win_kv)[:, t - wl:].reshape(n, wl, 2, G_B, DH_B))

    pos_s = past + jnp.arange(s)
    tabs_s = tuple(jnp.tile(x, (til_s.b, 1)) for x in _rope_tables(pos_s, DK_A) + _rope_tables(pos_s, DH_B))
    qa, ka, va, qb, cmp_kv, sel_kv, win_kv, gt = _even_inproj(til_s, ys, mods_s, nw1, w_main, w_gate, tabs_s)
    r3 = lambda a: a.reshape(ns, s, a.shape[-1])
    slot_cache = lambda c: c.reshape(c.shape[0], PAGE_SIZE * SLOTS, LANES)
    c_dkt = jnp.transpose(c_dk, (0, 2, 3, 4, 1)).reshape(c_dk.shape[0], H_A * 2 * DK_A, PAGE_SIZE)
    oa = _diff_attn_sample(r3(qa), r3(ka), r3(va), c_dkt, slot_cache(c_dv), page_table, lam_w, sub_w, lam_init)
    ccmp = _compress_sample(slot_cache(c_cmp), page_table, wc, pe, w2c)
    total = past + s
    n_cmp = (total - CMP_LEN) // CMP_STRIDE + 1
    n_sel = -(-total // SEL_BLOCK)
    assert n_cmp <= ccmp.shape[3] and n_sel <= LANES
    nb = math.gcd(ns, LANES // s)
    o_cmp, selm = _cmp_topk(r3(qb), ccmp, nb=nb, rq=s, n_cmp=n_cmp, n_sel=n_sel, pos_base=past)
    o_sel = _sel_attn_sample(r3(qb), selm, r3(sel_kv), slot_cache(c_sel), page_table)
    wb = win_buf.shape[1]
    o_win, win_out = _win_attn_sample(r3(qb), win_buf.reshape(ns, wb * SLOTS, LANES), r3(win_kv), past)
    f2 = lambda a: a.reshape(ns * s, a.shape[-1])
    ys = _even_out(til_s, ys, mods_s, f2(oa), f2(o_cmp), f2(o_sel), f2(o_win), gt, w_out, mlp)
    st_s = (r3(ka).reshape(ns, s, H_A, 2, DK_A), r3(va).reshape(ns, s, H_A, DV_A),
            r3(cmp_kv).reshape(ns, s, 2, G_B, DH_B), r3(sel_kv).reshape(ns, s, 2, G_B, DH_B),
            win_out.reshape(ns, wb, 2, G_B, DH_B))
    return yp, ys, st_p, st_s


def _odd_layer(yp, ys, mods_p, mods_s, s0, wts, til_p, til_s, mlp):
    w_in = wts["w_in"]
    n_main = w_in.shape[1] - GATE_RANK
    w_main = w_in[:, :n_main].astype(BF16)
    w_gl = jnp.pad(w_in[:, n_main:], ((0, 0), (0, LANES - GATE_RANK))).astype(BF16)
    w_gate = jnp.pad(wts["w_gate"], ((0, LANES - GATE_RANK), (0, 0))).astype(BF16)
    b_gate = wts["b_gate"].reshape(1, -1)
    nw1 = wts["norm1"].reshape(1, D_MODEL)
    gnw = wts["gnorm"].reshape(1, DV_C)
    w_out = wts["w_out"].astype(BF16)
    outs = []
    for y, mods, til, state in ((yp, mods_p, til_p, None), (ys, mods_s, til_s, s0)):
        n, t, _ = y.shape
        q, k, v, r, la = _gla_inproj(til, y, mods, nw1, w_main, w_gl, w_gate, b_gate)
        r3 = lambda a: a.reshape(n, t, a.shape[-1])
        c = math.gcd(t, GLA_CHUNK)
        if c >= GLA_SUB:
            tt, cc, sub = math.gcd(t, 8 * c), c, GLA_SUB_LONG
        else:
            tt, cc, sub = t, GLA_SUB, GLA_SUB
        o, s_fin = _gla_recurrence(r3(q), r3(k), r3(v), r3(la), state, tt, cc, sub, H_C)
        y = _gla_out(til, y, mods, o.reshape(n * t, -1), r, gnw, w_out, mlp)
        outs.append((y, s_fin))
    return outs[0][0], outs[1][0], outs[0][1], outs[1][1]


def kernel(x_prompt, x_sample, c_prompt, c_sample, cache_diff_k, cache_diff_v, cache_cmp_kv, cache_sel_kv,
           state_win_kv, state_gla, page_table, norm1_w, norm2_w, ada_w, ada_b, even_w_in, even_w_out,
           diff_lambda_w, diff_subln_w, cmp_pe, cmp_w1, cmp_w2, gla_w_in, gla_w_gate, gla_b_gate, gla_norm_w,
           gla_w_out, mlp_w1, mlp_w2, final_norm_w):
    depth = ada_w.shape[0]
    n, t, _ = x_prompt.shape
    ns, s, _ = x_sample.shape
    til_p = _Tiling(n, t, 1, _prompt_tile_rows(t))
    til_s = _Tiling(ns, s, math.gcd(ns, 256 // s), s)

    pad = (-(n + ns)) % 8
    c_all = jnp.concatenate([c_prompt, c_sample, jnp.zeros((pad, D_MODEL), F32)], axis=0)
    mod = _adaln(c_all, ada_w, ada_b)

    yp, ys = x_prompt, x_sample
    st_p = [[] for _ in range(6)]
    st_s = [[] for _ in range(6)]
    fw = final_norm_w.reshape(1, D_MODEL)
    for l in range(depth):
        mods_p = _mods_for(mod[l], 0, n)
        mods_s = _mods_for(mod[l], n, n + ns)
        mlp = (norm2_w[l].reshape(1, D_MODEL), mlp_w1[l].astype(BF16), mlp_w2[l].astype(BF16), fw, l == depth - 1)
        if l % 2 == 0:
            e = l // 2
            lam_init = 0.8 - 0.6 * math.exp(-0.3 * l)
            wts = dict(w_in=even_w_in[e], w_out=even_w_out[e], lam_w=diff_lambda_w[e], sub_w=diff_subln_w[e],
                       cmp_pe=cmp_pe[e], cmp_w1=cmp_w1[e], cmp_w2=cmp_w2[e], norm1=norm1_w[l])
            caches = (cache_diff_k[e], cache_diff_v[e], cache_cmp_kv[e], cache_sel_kv[e], state_win_kv[e])
            yp, ys, sp, ss = _even_layer(yp, ys, mods_p, mods_s, caches, page_table, wts, lam_init, til_p, til_s, mlp)
            for i in range(5):
                st_p[i].append(sp[i])
                st_s[i].append(ss[i])
        else:
            o = l // 2
            wts = dict(w_in=gla_w_in[o], w_gate=gla_w_gate[o], b_gate=gla_b_gate[o], gnorm=gla_norm_w[o],
                       w_out=gla_w_out[o], norm1=norm1_w[l])
            yp, ys, gp, gs = _odd_layer(yp, ys, mods_p, mods_s, state_gla[o], wts, til_p, til_s, mlp)
            st_p[5].append(gp)
            st_s[5].append(gs)
    outs_p = [jnp.stack(x, axis=0) for x in st_p]
    outs_s = [jnp.stack(x, axis=0) for x in st_s]
    return (yp, ys, *outs_p, *outs_s)
```

```python
import functools
import math

import jax
import jax.numpy as jnp
from jax import lax
from jax.experimental import pallas as pl
from jax.experimental.pallas import tpu as pltpu

F32 = jnp.float32
BF16 = jnp.bfloat16

D_MODEL = 1024
PAGE_SIZE = 128
H_A = 4
DK_A = 64
DV_A = 128
H_B = 4
G_B = 2
R_B = 2
DH_B = 128
CMP_LEN = 32
CMP_STRIDE = 16
SEL_BLOCK = 64
TOP_N = 16
WINDOW = 512
H_C = 4
DK_C = 128
DV_C = 256
GATE_RANK = 16
GATE_TAU = 16.0
GLA_CHUNK = 64
GLA_SUB = 16
GLA_SAFE_DECAY = 80.0
D_FF = 4 * D_MODEL
ROPE_THETA = 10000.0
EPS = 1e-6

LANES = 128
GROUP_W = 512
assert GROUP_W == H_A * 2 * DK_A == H_A * DV_A == H_B * DH_B == 2 * G_B * DH_B
SLOTS = GROUP_W // LANES
MASK_NEG = -1e30
LOG2E = 1.4426950408889634
SAFE_SHIFT = 56.0
SEL_NEG = 32768.0
VMEM_LIMIT_MB = 56


def _cparams(sem, vmem_mb=VMEM_LIMIT_MB):
    return pltpu.CompilerParams(dimension_semantics=sem, vmem_limit_bytes=vmem_mb * 1024 * 1024)


def _dot(a, b):
    return jnp.dot(a, b, preferred_element_type=F32)


def _dot_nt(a, b):
    return lax.dot_general(a, b, (((1,), (1,)), ((), ())), preferred_element_type=F32)


def _dot_tn(a, b):
    return lax.dot_general(a, b, (((0,), (0,)), ((), ())), preferred_element_type=F32)


def _iota(shape, dim):
    return lax.broadcasted_iota(jnp.int32, shape, dim)


def _rms(x, w):
    ms = jnp.mean(x * x, axis=-1, keepdims=True)
    return x * lax.rsqrt(ms + EPS) * w


def _norm_mod(x, nw, shift, scale):
    return _rms(x, nw) * (1.0 + scale) + shift


class _Tiling:
    def __init__(self, nb, r, b, rt):
        assert nb % b == 0 and r % rt == 0 and (b == 1 or rt == r)
        self.nb, self.r, self.b, self.rt = nb, r, b, rt
        self.grid = (nb // b, r // rt)
        self.rows = b * rt
        self.nrb = r // rt

    def x_spec(self, d):
        return pl.BlockSpec((self.b, self.rt, d), lambda i, j: (i, j, 0))

    def mod_spec(self, k):
        return pl.BlockSpec((1, self.b, 1, D_MODEL), lambda i, j: (k, i, 0, 0))

    def flat_spec(self, c):
        nrb = self.nrb
        return pl.BlockSpec((self.rows, c), lambda i, j: (i * nrb + j, 0))

    def tab_spec(self):
        return pl.BlockSpec((self.rows, LANES), lambda i, j: (j, 0))


def _full_spec(shape):
    nd = len(shape)
    return pl.BlockSpec(shape, lambda *_: (0,) * nd)


def _const_spec(shape):
    nd = len(shape)
    return pl.BlockSpec(shape, lambda *_: (0,) * nd, pipeline_mode=pl.Buffered(1))


def _adaln_kernel(c_ref, w_ref, b_ref, o_ref):
    c = c_ref[...]
    a = (c * jax.nn.sigmoid(c)).astype(BF16)
    o_ref[0] = _dot(a, w_ref[0].astype(BF16)) + b_ref[0]


def _adaln(c_all, ada_w, ada_b):
    depth, d, n6 = ada_w.shape
    rows = c_all.shape[0]
    tn = math.gcd(n6, 12 * LANES)
    return pl.pallas_call(
        _adaln_kernel,
        grid=(depth, n6 // tn),
        in_specs=[pl.BlockSpec((rows, d), lambda l, j: (0, 0)),
                  pl.BlockSpec((1, d, tn), lambda l, j: (l, 0, j)),
                  pl.BlockSpec((1, 1, tn), lambda l, j: (l, 0, j))],
        out_specs=pl.BlockSpec((1, rows, tn), lambda l, j: (l, 0, j)),
        out_shape=jax.ShapeDtypeStruct((depth, rows, n6), F32),
        compiler_params=_cparams(("parallel", "parallel")),
        name="adaln",
    )(c_all, ada_w, ada_b.reshape(depth, 1, n6))


def _swap_half(x, half):
    if 2 * half == LANES:
        return pltpu.roll(x, half, 1)
    lane = _iota((1, LANES), 1)
    lo = (lane % (2 * half)) < half
    return jnp.where(lo, pltpu.roll(x, LANES - half, 1), pltpu.roll(x, half, 1))


def _even_inproj_kernel(*refs, k_feature_major):
    if k_feature_major:
        (x_ref, sh_ref, sc_ref, nw_ref, w_ref, wg_ref, c64_ref, s64_ref, c128_ref, s128_ref, wkt_ref, ct_ref, st_ref,
         qa_ref, ka_ref, va_ref, qb_ref, cmp_ref, sel_ref, win_ref, gt_ref) = refs
    else:
        (x_ref, sh_ref, sc_ref, nw_ref, w_ref, wg_ref, c64_ref, s64_ref, c128_ref, s128_ref,
         qa_ref, ka_ref, va_ref, qb_ref, cmp_ref, sel_ref, win_ref, gt_ref) = refs
    h = _norm_mod(x_ref[...], nw_ref[...], sh_ref[0], sc_ref[0])
    h = h.reshape(-1, D_MODEL).astype(BF16)
    c64, s64, c128, s128 = c64_ref[...], s64_ref[...], c128_ref[...], s128_ref[...]

    def rope64(p):
        return p * c64 + _swap_half(p, DK_A // 2) * s64

    def rope128(p):
        return p * c128 + _swap_half(p, DH_B // 2) * s128

    def project(ref, off, ropes):
        p = _dot(h, w_ref[:, off:off + GROUP_W])
        for j, rope in enumerate(ropes):
            sl = slice(j * LANES, (j + 1) * LANES)
            ref[:, sl] = p[:, sl] if rope is None else rope(p[:, sl])

    project(qa_ref, 0, [rope64] * 4)
    if k_feature_major:
        kt = _dot_nt(wkt_ref[...], h)
        ct, st = ct_ref[...], st_ref[...]
        half = DK_A // 2
        for grp in range(H_A * 2):
            x1 = kt[grp * DK_A:grp * DK_A + half]
            x2 = kt[grp * DK_A + half:(grp + 1) * DK_A]
            ka_ref[0, grp * DK_A:grp * DK_A + half, :] = x1 * ct - x2 * st
            ka_ref[0, grp * DK_A + half:(grp + 1) * DK_A, :] = x2 * ct + x1 * st
    else:
        project(ka_ref, GROUP_W, [rope64] * 4)
    project(va_ref, 2 * GROUP_W, [None] * 4)
    project(qb_ref, 3 * GROUP_W, [rope128] * 4)
    for t, ref in enumerate((cmp_ref, sel_ref, win_ref)):
        project(ref, (4 + t) * GROUP_W, [rope128, rope128, None, None])
    gt_ref[...] = jax.nn.sigmoid(_dot(h, wg_ref[...]))


def _even_inproj(til, x, mods, nw, w_main, w_gate, tabs, kt_tabs=None):
    rows_total = til.nb * til.r
    widths = (GROUP_W,) * 7 + (LANES,)
    in_specs = [til.x_spec(D_MODEL), til.mod_spec(0), til.mod_spec(1), _full_spec((1, D_MODEL)),
                _const_spec(w_main.shape), _const_spec(w_gate.shape)] + [til.tab_spec()] * 4
    args = [x, mods, mods, nw, w_main, w_gate, *tabs]
    out_specs = [til.flat_spec(c) for c in widths]
    out_shape = [jax.ShapeDtypeStruct((rows_total, c), F32) for c in widths]
    if kt_tabs is not None:
        assert til.b == 1
        w_kt = jnp.transpose(w_main[:, GROUP_W:2 * GROUP_W])
        in_specs += [_const_spec(w_kt.shape)] + [pl.BlockSpec((DK_A // 2, til.rt), lambda i, j: (0, j))] * 2
        args += [w_kt, *kt_tabs]
        out_specs[1] = pl.BlockSpec((1, GROUP_W, til.rt), lambda i, j: (i, 0, j))
        out_shape[1] = jax.ShapeDtypeStruct((til.nb, GROUP_W, til.r), F32)
    return pl.pallas_call(
        functools.partial(_even_inproj_kernel, k_feature_major=kt_tabs is not None),
        grid=til.grid,
        in_specs=in_specs,
        out_specs=out_specs,
        out_shape=out_shape,
        compiler_params=_cparams(("parallel", "parallel")),
        name="even_inproj",
    )(*args)


def _for_tiles(lo, hi, fn):
    n = hi - lo

    def pair(j, carry):
        fn(lo + 2 * j)
        fn(lo + 2 * j + 1)
        return carry

    lax.fori_loop(0, n // 2, pair, 0)

    @pl.when(n % 2 == 1)
    def _():
        fn(hi - 1)


def _tile_distance(stack, tq, tk):
    r = _iota((stack * tq, tk), 0) % tq
    return r - _iota((stack * tq, tk), 1)


def _row_sumsq(x):
    xf = x.astype(F32)
    return _dot((xf * xf).astype(BF16), jnp.ones((LANES, LANES), BF16))


def _key_norm_bound(k_ref, n_keys, tile):
    def body(kb, m):
        k = k_ref[0, pl.ds(pl.multiple_of(kb * tile, tile), tile), :].astype(BF16)
        return jnp.maximum(m, _row_sumsq(k))

    m = lax.fori_loop(0, n_keys // tile, body, jnp.zeros((tile, LANES), F32))
    return jnp.max(m, axis=0, keepdims=True)


def _score_bound(q2, kmax2):
    qmax2 = jnp.max(_row_sumsq(q2), axis=0, keepdims=True)
    return jnp.sqrt(qmax2 * kmax2) * 1.05


def _two_pass_attention(mx_ref, acc_ref, lo, hi, last, scores, values, mask_body=False, bound=None):
    def lane_max(s):
        m = s[:, 0:LANES]
        for c in range(1, s.shape[1] // LANES):
            m = jnp.maximum(m, s[:, c * LANES:(c + 1) * LANES])
        return m

    def pass1(kb, masked):
        mx_ref[...] = jnp.maximum(mx_ref[...], lane_max(scores(kb, masked)))

    def exact_max():
        mx_ref[...] = jnp.full(mx_ref.shape, MASK_NEG, F32)
        _for_tiles(lo, hi, lambda kb: pass1(kb, mask_body))
        for kb in last:
            pass1(kb, True)
        mx_ref[...] = jnp.broadcast_to(jnp.max(mx_ref[...], axis=-1, keepdims=True), mx_ref.shape)

    if bound is None:
        exact_max()
    else:
        safe = jnp.max(bound) <= SAFE_SHIFT

        @pl.when(safe)
        def _():
            mx_ref[...] = jnp.broadcast_to(bound, mx_ref.shape)

        @pl.when(jnp.logical_not(safe))
        def _():
            exact_max()

    acc_ref[...] = jnp.zeros(acc_ref.shape, F32)

    def pass2(kb, masked):
        s = scores(kb, masked)
        m = mx_ref[...]
        p = jnp.concatenate([jnp.exp2(s[:, c * LANES:(c + 1) * LANES] - m) for c in range(s.shape[1] // LANES)],
                            axis=-1).astype(BF16)
        v = values(kb)
        v1 = jnp.concatenate([v, jnp.ones(v.shape, BF16)], axis=-1)
        acc_ref[...] += _dot(p, v1)

    _for_tiles(lo, hi, lambda kb: pass2(kb, mask_body))
    for kb in last:
        pass2(kb, True)
    return acc_ref[:, 0:LANES] / acc_ref[:, LANES:2 * LANES]


def _diff_lambda(lw, lam_init):
    a = jnp.sum(lw[0:1] * lw[1:2], axis=-1, keepdims=True)
    b = jnp.sum(lw[2:3] * lw[3:4], axis=-1, keepdims=True)
    return jnp.exp(a) - jnp.exp(b) + lam_init


def _diff_finalize(o2, tq, lam, sub_w, lam_init):
    od = o2[:tq] - lam * o2[tq:]
    return _rms(od, sub_w) * (1.0 - lam_init)


def _diff_flash_kernel(q_ref, k_ref, v_ref, lamw_ref, sub_ref, o_ref, q2_ref, mx_ref, acc_ref, kmax_ref, *,
                       tq, tk, lam_init):
    i = pl.program_id(2)

    def key_tile(kb):
        return k_ref[0, :, pl.ds(pl.multiple_of(kb * tk, tk), tk)].astype(BF16)

    @pl.when(i == 0)
    def _():
        def body(kb, m):
            kf = key_tile(kb).astype(F32)
            return jnp.maximum(m, jnp.sum(kf * kf, axis=0, keepdims=True))

        m = lax.fori_loop(0, k_ref.shape[2] // tk, body, jnp.zeros((1, tk), F32))
        kmax_ref[...] = jnp.broadcast_to(jnp.max(m, axis=-1, keepdims=True), kmax_ref.shape)

    q = q_ref[0] * (DK_A ** -0.5 * LOG2E)
    lane = _iota((1, LANES), 1)
    q2_ref[0:tq] = jnp.where(lane < DK_A, q, 0.0).astype(BF16)
    q2_ref[tq:2 * tq] = jnp.where(lane >= DK_A, q, 0.0).astype(BF16)
    bound = _score_bound(q2_ref[...], kmax_ref[0:1])
    rel = _tile_distance(2, tq, tk)

    def scores(kb, masked):
        s = _dot(q2_ref[...], key_tile(kb))
        if masked:
            dist = rel + (i * tq - kb * tk)
            s = jnp.where(dist >= 0, s, MASK_NEG)
        return s

    def values(kb):
        return v_ref[0, pl.ds(pl.multiple_of(kb * tk, tk), tk), :].astype(BF16)

    per = tq // tk
    o2 = _two_pass_attention(mx_ref, acc_ref, 0, i * per, [i * per + u for u in range(per)], scores, values,
                             bound=bound)
    o_ref[0] = _diff_finalize(o2, tq, _diff_lambda(lamw_ref[...], lam_init), sub_ref[...], lam_init)


def _diff_attn_prompt(qa, kat, va, lam_w, sub_w, lam_init, tq, tk):
    n, t, _ = qa.shape
    kern = functools.partial(_diff_flash_kernel, tq=tq, tk=tk, lam_init=lam_init)
    return pl.pallas_call(
        kern,
        grid=(n, H_A, t // tq),
        in_specs=[pl.BlockSpec((1, tq, LANES), lambda b, h, i: (b, i, h)),
                  pl.BlockSpec((1, LANES, t), lambda b, h, i: (b, h, 0)),
                  pl.BlockSpec((1, t, LANES), lambda b, h, i: (b, 0, h)),
                  _full_spec(lam_w.shape), _full_spec(sub_w.shape)],
        out_specs=pl.BlockSpec((1, tq, LANES), lambda b, h, i: (b, i, h)),
        out_shape=jax.ShapeDtypeStruct((n, t, H_A * DV_A), F32),
        scratch_shapes=[pltpu.VMEM((2 * tq, LANES), BF16), pltpu.VMEM((2 * tq, LANES), F32),
                        pltpu.VMEM((2 * tq, 2 * LANES), F32), pltpu.VMEM((8, LANES), F32)],
        compiler_params=_cparams(("parallel", "parallel", "arbitrary")),
        name="diff_attn_prompt",
    )(qa, kat, va, lam_w, sub_w)


def _sel_onehot(rows, first_block):
    blk = _iota((rows, LANES), 0) // SEL_BLOCK + first_block
    return jnp.where(blk == _iota((rows, LANES), 1), 1.0, 0.0).astype(BF16)


def _nsa_flash_kernel(*refs, tq, tk, use_sel):
    if use_sel:
        q_ref, selm_ref, k_ref, v_ref, o_ref, q2_ref, mx_ref, acc_ref, kmax_ref = refs
    else:
        q_ref, k_ref, v_ref, o_ref, q2_ref, mx_ref, acc_ref, kmax_ref = refs
    i = pl.program_id(2)

    @pl.when(i == 0)
    def _():
        kmax_ref[...] = jnp.broadcast_to(_key_norm_bound(k_ref, k_ref.shape[1], tk), kmax_ref.shape)

    q = q_ref[0] * (DH_B ** -0.5 * LOG2E)
    for r in range(R_B):
        q2_ref[r * tq:(r + 1) * tq, 0:LANES] = q[:, r * LANES:(r + 1) * LANES].astype(BF16)
        if use_sel:
            q2_ref[r * tq:(r + 1) * tq, LANES:2 * LANES] = selm_ref[0, 0]
    bound = _score_bound(q2_ref[:, 0:LANES], kmax_ref[0:1])
    rel = _tile_distance(R_B, tq, tk)

    def scores(kb, masked):
        k = k_ref[0, pl.ds(pl.multiple_of(kb * tk, tk), tk), :].astype(BF16)
        if use_sel:
            k = jnp.concatenate([k, _sel_onehot(tk, kb * (tk // SEL_BLOCK))], axis=-1)
        s = _dot_nt(q2_ref[...], k)
        if masked:
            dist = rel + (i * tq - kb * tk)
            if use_sel:
                ok = dist >= 0
            else:
                ok = lax.bitcast_convert_type(dist, jnp.uint32) < jnp.uint32(WINDOW)
            s = jnp.where(ok, s, MASK_NEG)
        return s

    def values(kb):
        return v_ref[0, pl.ds(pl.multiple_of(kb * tk, tk), tk), :].astype(BF16)

    per = tq // tk
    lo = 0 if use_sel else jnp.maximum(i * per - WINDOW // tk, 0)
    o = _two_pass_attention(mx_ref, acc_ref, lo, i * per, [i * per + u for u in range(per)], scores, values,
                            mask_body=not use_sel, bound=bound)
    for r in range(R_B):
        o_ref[0, :, r * LANES:(r + 1) * LANES] = o[r * tq:(r + 1) * tq]


def _nsa_attn_prompt(qb, kv, selm, tq, tk):
    n, t, _ = qb.shape
    use_sel = selm is not None
    kd = 2 * LANES if use_sel else LANES
    kern = functools.partial(_nsa_flash_kernel, tq=tq, tk=tk, use_sel=use_sel)
    in_specs = [pl.BlockSpec((1, tq, R_B * LANES), lambda b, g, i: (b, i, g))]
    args = [qb]
    if use_sel:
        in_specs.append(pl.BlockSpec((1, 1, tq, LANES), lambda b, g, i: (b, g, i, 0)))
        args.append(selm)
    in_specs += [pl.BlockSpec((1, t, LANES), lambda b, g, i: (b, 0, g)),
                 pl.BlockSpec((1, t, LANES), lambda b, g, i: (b, 0, G_B + g))]
    args += [kv, kv]
    return pl.pallas_call(
        kern,
        grid=(n, G_B, t // tq),
        in_specs=in_specs,
        out_specs=pl.BlockSpec((1, tq, R_B * LANES), lambda b, g, i: (b, i, g)),
        out_shape=jax.ShapeDtypeStruct((n, t, H_B * DH_B), F32),
        scratch_shapes=[pltpu.VMEM((R_B * tq, kd), BF16), pltpu.VMEM((R_B * tq, LANES), F32),
                        pltpu.VMEM((R_B * tq, 2 * LANES), F32), pltpu.VMEM((8, LANES), F32)],
        compiler_params=_cparams(("parallel", "parallel", "arbitrary")),
        name="nsa_sel_prompt" if use_sel else "nsa_win_prompt",
    )(*args)


def _compress_core(load, wc_ref, pe_ref, w2_ref, wi, nchunk):
    rows = G_B * nchunk
    acc = jnp.zeros((rows, 2 * LANES), F32)
    pew = jnp.zeros((16, 2 * LANES), F32)
    for u in range(CMP_STRIDE // 2):
        lhs = jnp.concatenate([load(2 * u), load(2 * u + 1)], axis=-1).astype(BF16)
        w = wc_ref[wi, u]
        acc = acc + _dot(lhs, w)
        pew = pew + _dot(pe_ref[wi, u], w)
    first = acc[:, :LANES]
    second = pltpu.roll(acc[:, LANES:], rows - 1, 0)
    hid = first + second + pew[0:1, :LANES] + pew[8:9, LANES:]
    hid = hid * jax.nn.sigmoid(hid)
    return _dot(hid.astype(BF16), w2_ref[wi])


def _compress_prompt_kernel(*refs, nchunk):
    x_refs = refs[:G_B]
    wc_ref, pe_ref, w2_ref, o_ref = refs[G_B:]

    def load(tok):
        return jnp.concatenate([xr[0, pl.ds(tok, nchunk, stride=CMP_STRIDE), :] for xr in x_refs], axis=0)

    out = _compress_core(load, wc_ref, pe_ref, w2_ref, 0, nchunk)
    for g in range(G_B):
        o_ref[0, 0, g] = out[g * nchunk:(g + 1) * nchunk]


def _compress_sample_kernel(pt_ref, *refs, n_pages, spb):
    pg_refs = refs[:spb * n_pages]
    wc_ref, pe_ref, w2_ref, o_ref = refs[spb * n_pages:]
    cps = PAGE_SIZE // CMP_STRIDE
    nchunk = n_pages * cps
    for kv in range(2):
        by_tok = [jnp.swapaxes(pg[0, pl.ds(kv * G_B + g, PAGE_SIZE, stride=SLOTS), :]
                               .reshape(cps, CMP_STRIDE, LANES), 0, 1)
                  for smp in range(spb) for g in range(G_B) for pg in pg_refs[smp * n_pages:(smp + 1) * n_pages]]

        def load(tok):
            return jnp.concatenate([x[tok] for x in by_tok], axis=0)

        out = _compress_core(load, wc_ref, pe_ref, w2_ref, kv, spb * nchunk)
        for smp in range(spb):
            for g in range(G_B):
                seg = smp * G_B + g
                o_ref[smp, kv, g] = out[seg * nchunk:(seg + 1) * nchunk]


def _compress_weights(cmp_pe, cmp_w1, cmp_w2):
    w1 = cmp_w1.reshape(2, CMP_LEN, DH_B, DH_B)
    wab = jnp.concatenate([w1[:, :CMP_STRIDE], w1[:, CMP_STRIDE:]], axis=-1)
    wc = wab.reshape(2, CMP_STRIDE // 2, 2 * DH_B, 2 * DH_B).astype(BF16)
    pa = cmp_pe[:, :CMP_STRIDE].reshape(2, CMP_STRIDE // 2, 1, 2 * DH_B)
    pb = cmp_pe[:, CMP_STRIDE:].reshape(2, CMP_STRIDE // 2, 1, 2 * DH_B)
    z = jnp.zeros((2, CMP_STRIDE // 2, 7, 2 * DH_B), F32)
    pe = jnp.concatenate([pa, z, pb, z], axis=2).astype(BF16)
    return wc, pe, cmp_w2.astype(BF16)


def _compress_prompt(cmp_kv, wc, pe, w2):
    n, t, _ = cmp_kv.shape
    nchunk = t // CMP_STRIDE
    kern = functools.partial(_compress_prompt_kernel, nchunk=nchunk)
    return pl.pallas_call(
        kern,
        grid=(n, 2),
        in_specs=[pl.BlockSpec((1, t, DH_B), functools.partial(lambda b, kv, g: (b, 0, kv * G_B + g), g=g))
                  for g in range(G_B)] + [
                  pl.BlockSpec((1,) + wc.shape[1:], lambda b, kv: (kv, 0, 0, 0)),
                  pl.BlockSpec((1,) + pe.shape[1:], lambda b, kv: (kv, 0, 0, 0)),
                  pl.BlockSpec((1, DH_B, DH_B), lambda b, kv: (kv, 0, 0))],
        out_specs=pl.BlockSpec((1, 1, G_B, nchunk, DH_B), lambda b, kv: (b, kv, 0, 0, 0)),
        out_shape=jax.ShapeDtypeStruct((n, 2, G_B, nchunk, DH_B), F32),
        compiler_params=_cparams(("parallel", "parallel")),
        name="compress_prompt",
    )(*([cmp_kv] * G_B), wc, pe, w2)


def _compress_sample(cache, page_table, wc, pe, w2):
    n, n_pages = page_table.shape
    nchunk = n_pages * (PAGE_SIZE // CMP_STRIDE)
    spb = math.gcd(n, 2)
    kern = functools.partial(_compress_sample_kernel, n_pages=n_pages, spb=spb)
    page_specs = _page_specs(n_pages, PAGE_SIZE * SLOTS, LANES, spb)
    grid_spec = pltpu.PrefetchScalarGridSpec(
        num_scalar_prefetch=1,
        grid=(n // spb,),
        in_specs=page_specs + [
            pl.BlockSpec(wc.shape, lambda b, pt: (0, 0, 0, 0)),
            pl.BlockSpec(pe.shape, lambda b, pt: (0, 0, 0, 0)),
            pl.BlockSpec(w2.shape, lambda b, pt: (0, 0, 0))],
        out_specs=pl.BlockSpec((spb, 2, G_B, nchunk, DH_B), lambda b, pt: (b, 0, 0, 0, 0)),
    )
    return pl.pallas_call(
        kern,
        grid_spec=grid_spec,
        out_shape=jax.ShapeDtypeStruct((n, 2, G_B, nchunk, DH_B), F32),
        compiler_params=_cparams(("parallel",)),
        name="compress_sample",
    )(page_table, *([cache] * (spb * n_pages)), wc, pe, w2)


def _cmp_topk_kernel(q_ref, kc_ref, vc_ref, ocmp_ref, selm_ref, p_scr, v_scr, *,
                     nb, rq, nck, n_cmp, n_sel, nsp, pos_base):
    tq = nb * rq
    j = pl.program_id(2)
    scale = DH_B ** -0.5
    kidx = _iota((1, nck), 1)
    qpos_c = pos_base + j * rq + _iota((rq, 1), 0)
    valid = (kidx < n_cmp) & (CMP_STRIDE * kidx + (CMP_LEN - 1) <= qpos_c)
    for b in range(nb):
        kcc = kc_ref[b, 0, 0].astype(BF16)
        vcc = vc_ref[b, 0, 0].astype(BF16)
        q = q_ref[b] * scale
        q2 = jnp.concatenate([q[:, r * LANES:(r + 1) * LANES] for r in range(R_B)], axis=0).astype(BF16)
        valid2 = jnp.concatenate([valid] * R_B, axis=0)
        s = jnp.where(valid2, _dot_nt(q2, kcc), MASK_NEG)
        m = jnp.max(s, axis=-1, keepdims=True)
        p = jnp.where(valid2, jnp.exp(s - m), 0.0)
        p = p / jnp.maximum(jnp.sum(p, axis=-1, keepdims=True), 1e-30)
        o = _dot(p.astype(BF16), vcc)
        psum = jnp.zeros((rq, nck), F32)
        for r in range(R_B):
            ocmp_ref[b, :, r * LANES:(r + 1) * LANES] = o[r * rq:(r + 1) * rq]
            psum = psum + p[r * rq:(r + 1) * rq]
        p_scr[b * rq:(b + 1) * rq, :] = psum
    psum = p_scr[...]
    p_hi = psum.astype(BF16)
    p_lo = (psum - p_hi.astype(F32)).astype(BF16)
    srow = _iota((LANES, nck), 0)
    kcol = _iota((LANES, nck), 1)
    cov = (CMP_STRIDE * kcol < SEL_BLOCK * srow + SEL_BLOCK) & (CMP_STRIDE * kcol + CMP_LEN > SEL_BLOCK * srow)
    cov = cov & (srow < n_sel) & (kcol < n_cmp)
    cov_t = jnp.where(cov, 1.0, 0.0).astype(BF16)
    imp_t = _dot_nt(cov_t, p_hi) + _dot_nt(cov_t, p_lo)
    qpos_l = pos_base + j * rq + _iota((1, tq), 1) % rq
    sb = _iota((nsp, 1), 0)
    valid_b = (sb < n_sel) & (sb * SEL_BLOCK <= qpos_l)
    forced = (sb == 0) | (sb == qpos_l // SEL_BLOCK)
    vals = jnp.where(forced, jnp.inf, jnp.where(valid_b, imp_t[:nsp], -jnp.inf))
    v_scr[...] = vals

    def body(jj, cnt):
        vj = v_scr[pl.ds(jj, 1), :]
        tie = jnp.where(sb > jj, 1.0, 0.0)
        return cnt + jnp.where(vj > vals, 1.0, jnp.where(vj == vals, tie, 0.0))

    last_pos = pos_base + (j + 1) * rq - 1
    n_rank = jnp.minimum(last_pos // SEL_BLOCK + 1, nsp)
    cnt = lax.fori_loop(0, n_rank, body, jnp.zeros((nsp, tq), F32))
    keep = valid_b & (cnt < float(min(TOP_N, n_sel)))
    selm_t = jnp.where(keep, 0.0, -SEL_NEG)
    if nsp < LANES:
        selm_t = jnp.concatenate([selm_t, jnp.zeros((LANES - nsp, tq), F32)], axis=0)
    selm = selm_t.T.astype(BF16)
    for b in range(nb):
        selm_ref[b, 0] = selm[b * rq:(b + 1) * rq]


def _cmp_topk(qb, ccmp, *, nb, rq, n_cmp, n_sel, pos_base):
    n, r, _ = qb.shape
    nck = ccmp.shape[3]
    nsp = -(-n_sel // 8) * 8
    tq = nb * rq
    kern = functools.partial(_cmp_topk_kernel, nb=nb, rq=rq, nck=nck, n_cmp=n_cmp, n_sel=n_sel, nsp=nsp,
                             pos_base=pos_base)
    return pl.pallas_call(
        kern,
        grid=(n // nb, G_B, r // rq),
        in_specs=[pl.BlockSpec((nb, rq, R_B * LANES), lambda a, g, j: (a, j, g)),
                  pl.BlockSpec((nb, 1, 1, nck, DH_B), lambda a, g, j: (a, 0, g, 0, 0)),
                  pl.BlockSpec((nb, 1, 1, nck, DH_B), lambda a, g, j: (a, 1, g, 0, 0))],
        out_specs=[pl.BlockSpec((nb, rq, R_B * LANES), lambda a, g, j: (a, j, g)),
                   pl.BlockSpec((nb, 1, rq, LANES), lambda a, g, j: (a, g, j, 0))],
        out_shape=[jax.ShapeDtypeStruct((n, r, H_B * DH_B), F32),
                   jax.ShapeDtypeStruct((n, G_B, r, LANES), BF16)],
        scratch_shapes=[pltpu.VMEM((tq, nck), F32), pltpu.VMEM((nsp, tq), F32)],
        compiler_params=_cparams(("parallel", "parallel", "parallel")),
        name="cmp_topk",
    )(qb, ccmp, ccmp)


def _attend_scores(pieces):
    m = None
    for s, _ in pieces:
        ms = jnp.max(s, axis=-1, keepdims=True)
        m = ms if m is None else jnp.maximum(m, ms)
    l = jnp.zeros_like(m)
    acc = jnp.zeros((m.shape[0], LANES), F32)
    for s, v in pieces:
        p = jnp.exp(s - m)
        l = l + jnp.sum(p, axis=-1, keepdims=True)
        acc = acc + _dot(p.astype(BF16), v)
    return acc / l


def _attend_pieces(q2, pieces):
    scored = []
    for k, v, mask in pieces:
        s = _dot_nt(q2, k)
        scored.append((s if mask is None else jnp.where(mask, s, MASK_NEG), v))
    return _attend_scores(scored)


def _pad_rows(x, rows):
    return jnp.concatenate([x, jnp.zeros((rows - x.shape[0], x.shape[1]), x.dtype)], axis=0)


def _new_token_mask(m_rows, s):
    r = _iota((m_rows, LANES), 0) % s
    return _iota((m_rows, LANES), 1) <= r


def _diff_sample_kernel(pt_ref, *refs, n_pages, s, spb, lam_init):
    kp_refs = refs[:spb * n_pages]
    vp_refs = refs[spb * n_pages:2 * spb * n_pages]
    q_ref, kn_ref, vn_ref, lamw_ref, sub_ref, o_ref = refs[2 * spb * n_pages:]
    lane = _iota((1, LANES), 1)
    lam = _diff_lambda(lamw_ref[...], lam_init)
    new_mask = _new_token_mask(2 * s, s)
    for b in range(spb):
        kpages = kp_refs[b * n_pages:(b + 1) * n_pages]
        vpages = vp_refs[b * n_pages:(b + 1) * n_pages]
        for h in range(H_A):
            cols = slice(h * LANES, (h + 1) * LANES)
            q = q_ref[b, :, cols] * (DK_A ** -0.5)
            q2 = jnp.concatenate([jnp.where(lane < DK_A, q, 0.0), jnp.where(lane >= DK_A, q, 0.0)],
                                 axis=0).astype(BF16)
            kt_past = jnp.concatenate([r[0, cols, :].astype(BF16) for r in kpages], axis=1)
            s_past = _dot(q2, kt_past)
            vpast = jnp.concatenate([_slot_rows(r, h, PAGE_SIZE).astype(BF16) for r in vpages], axis=0)
            knew = _pad_rows(kn_ref[b, :, cols], LANES).astype(BF16)
            vnew = _pad_rows(vn_ref[b, :, cols], LANES).astype(BF16)
            s_new = jnp.where(new_mask, _dot_nt(q2, knew), MASK_NEG)
            o2 = _attend_scores([(s_past, vpast), (s_new, vnew)])
            o_ref[b, :, cols] = _diff_finalize(o2, s, lam, sub_ref[...], lam_init)


def _page_specs(n_pages, rows, width, spb=1):
    return [pl.BlockSpec((1, rows, width),
                         functools.partial(lambda b, pt, smp, j: (pt[b * spb + smp, j], 0, 0), smp=smp, j=j))
            for smp in range(spb) for j in range(n_pages)]


def _slot_rows(ref, slot, n_tok):
    return ref[0, pl.ds(slot, n_tok, stride=SLOTS), :]


def _diff_attn_sample(qa, ka, va, cache_k, cache_v, page_table, lam_w, sub_w, lam_init):
    n, s, _ = qa.shape
    n_pages = page_table.shape[1]
    spb = math.gcd(n, 2)
    kern = functools.partial(_diff_sample_kernel, n_pages=n_pages, s=s, spb=spb, lam_init=lam_init)
    row_spec = pl.BlockSpec((spb, s, GROUP_W), lambda b, pt: (b, 0, 0))
    grid_spec = pltpu.PrefetchScalarGridSpec(
        num_scalar_prefetch=1,
        grid=(n // spb,),
        in_specs=_page_specs(n_pages, 512, PAGE_SIZE, spb) + _page_specs(n_pages, PAGE_SIZE * SLOTS, LANES, spb)
        + [row_spec, row_spec, row_spec,
           pl.BlockSpec(lam_w.shape, lambda b, pt: (0, 0)), pl.BlockSpec(sub_w.shape, lambda b, pt: (0, 0))],
        out_specs=row_spec,
    )
    return pl.pallas_call(
        kern,
        grid_spec=grid_spec,
        out_shape=jax.ShapeDtypeStruct((n, s, GROUP_W), F32),
        compiler_params=_cparams(("parallel",)),
        name="diff_attn_sample",
    )(page_table, *([cache_k] * (spb * n_pages)), *([cache_v] * (spb * n_pages)), qa, ka, va, lam_w, sub_w)


def _sel_sample_kernel(pt_ref, *refs, n_pages, s, spb):
    pg_refs = refs[:spb * n_pages]
    q_ref, selm_ref, new_ref, o_ref = refs[spb * n_pages:]
    past = n_pages * PAGE_SIZE
    new_mask = _new_token_mask(R_B * s, s)
    oh_past = _sel_onehot(past, 0)
    new_blk = past // SEL_BLOCK
    oh_new = jnp.where((_iota((LANES, LANES), 1) == new_blk) & (_iota((LANES, LANES), 0) < s), 1.0, 0.0)
    oh_new = oh_new.astype(BF16)
    scale = DH_B ** -0.5
    for b in range(spb):
        pages = pg_refs[b * n_pages:(b + 1) * n_pages]
        for g in range(G_B):
            kcols = slice(g * LANES, (g + 1) * LANES)
            vcols = slice((G_B + g) * LANES, (G_B + g + 1) * LANES)
            selm = selm_ref[b, g]
            q2 = jnp.concatenate(
                [jnp.concatenate([(q_ref[b, :, (g * R_B + r) * LANES:(g * R_B + r + 1) * LANES] * scale).astype(BF16),
                                  selm], axis=-1) for r in range(R_B)], axis=0)
            kpast = jnp.concatenate([_slot_rows(r, g, PAGE_SIZE).astype(BF16) for r in pages], axis=0)
            vpast = jnp.concatenate([_slot_rows(r, G_B + g, PAGE_SIZE).astype(BF16) for r in pages], axis=0)
            kpast = jnp.concatenate([kpast, oh_past], axis=-1)
            knew = jnp.concatenate([_pad_rows(new_ref[b, :, kcols], LANES).astype(BF16), oh_new], axis=-1)
            vnew = _pad_rows(new_ref[b, :, vcols], LANES).astype(BF16)
            o2 = _attend_pieces(q2, [(kpast, vpast, None), (knew, vnew, new_mask)])
            for r in range(R_B):
                o_ref[b, :, (g * R_B + r) * LANES:(g * R_B + r + 1) * LANES] = o2[r * s:(r + 1) * s]


def _sel_attn_sample(qb, selm, sel_new, cache, page_table):
    n, s, _ = qb.shape
    n_pages = page_table.shape[1]
    assert s <= SEL_BLOCK and (n_pages * PAGE_SIZE) % SEL_BLOCK == 0
    spb = math.gcd(n, 2)
    kern = functools.partial(_sel_sample_kernel, n_pages=n_pages, s=s, spb=spb)
    row_spec = pl.BlockSpec((spb, s, GROUP_W), lambda b, pt: (b, 0, 0))
    page_specs = _page_specs(n_pages, PAGE_SIZE * SLOTS, LANES, spb)
    grid_spec = pltpu.PrefetchScalarGridSpec(
        num_scalar_prefetch=1,
        grid=(n // spb,),
        in_specs=page_specs
        + [row_spec, pl.BlockSpec((spb, G_B, s, LANES), lambda b, pt: (b, 0, 0, 0)), row_spec],
        out_specs=row_spec,
    )
    return pl.pallas_call(
        kern,
        grid_spec=grid_spec,
        out_shape=jax.ShapeDtypeStruct((n, s, GROUP_W), F32),
        compiler_params=_cparams(("parallel",)),
        name="nsa_sel_sample",
    )(page_table, *([cache] * (spb * n_pages)), qb, selm, sel_new)


def _win_sample_kernel(q_ref, buf_ref, new_ref, o_ref, wout_ref, *, s, wb, spb):
    new_mask = _new_token_mask(R_B * s, s)
    r = _iota((R_B * s, wb), 0) % s
    buf_mask = _iota((R_B * s, wb), 1) > r + (wb - WINDOW)
    scale = DH_B ** -0.5
    keep = (wb - s) * SLOTS
    for b in range(spb):
        for g in range(G_B):
            kcols = slice(g * LANES, (g + 1) * LANES)
            vcols = slice((G_B + g) * LANES, (G_B + g + 1) * LANES)
            q2 = jnp.concatenate(
                [(q_ref[b, :, (g * R_B + r_) * LANES:(g * R_B + r_ + 1) * LANES] * scale).astype(BF16)
                 for r_ in range(R_B)], axis=0)
            kbuf = buf_ref[b, pl.ds(g, wb, stride=SLOTS), :].astype(BF16)
            vbuf = buf_ref[b, pl.ds(G_B + g, wb, stride=SLOTS), :].astype(BF16)
            knew = _pad_rows(new_ref[b, :, kcols], LANES).astype(BF16)
            vnew = _pad_rows(new_ref[b, :, vcols], LANES).astype(BF16)
            o2 = _attend_pieces(q2, [(kbuf, vbuf, buf_mask), (knew, vnew, new_mask)])
            for r_ in range(R_B):
                o_ref[b, :, (g * R_B + r_) * LANES:(g * R_B + r_ + 1) * LANES] = o2[r_ * s:(r_ + 1) * s]
        wout_ref[b, 0:keep] = buf_ref[b, s * SLOTS:wb * SLOTS]
        for slot in range(SLOTS):
            wout_ref[b, pl.ds(keep + slot, s, stride=SLOTS), :] = new_ref[b, :, slot * LANES:(slot + 1) * LANES]


def _win_attn_sample(qb, win_buf, win_new, past):
    n, s, _ = qb.shape
    wb = win_buf.shape[1] // SLOTS
    assert past >= wb and wb % 8 == 0 and s % 8 == 0
    spb = math.gcd(n, 4)
    kern = functools.partial(_win_sample_kernel, s=s, wb=wb, spb=spb)
    row_spec = pl.BlockSpec((spb, s, GROUP_W), lambda b: (b, 0, 0))
    buf_spec = pl.BlockSpec((spb, wb * SLOTS, LANES), lambda b: (b, 0, 0))
    return pl.pallas_call(
        kern,
        grid=(n // spb,),
        in_specs=[row_spec, buf_spec, row_spec],
        out_specs=[row_spec, buf_spec],
        out_shape=[jax.ShapeDtypeStruct((n, s, GROUP_W), F32), jax.ShapeDtypeStruct((n, wb * SLOTS, LANES), F32)],
        compiler_params=_cparams(("parallel",)),
        name="nsa_win_sample",
    )(qb, win_buf, win_new)


def _mlp_value(y, sh, sc, g, nw, w1_ref, w2_ref, fw, final):
    h = _norm_mod(y, nw, sh, sc).reshape(-1, D_MODEL).astype(BF16)
    a = jnp.maximum(_dot(h, w1_ref[...]), 0.0)
    out = _dot((a * a).astype(BF16), w2_ref[...])
    y2 = y + g * out.reshape(y.shape)
    return _rms(y2, fw) if final else y2


def _mlp_specs(til, mlp):
    nw, w1, w2, fw, _ = mlp
    specs = [til.mod_spec(3), til.mod_spec(4), til.mod_spec(5), _full_spec((1, D_MODEL)),
             _const_spec(w1.shape), _const_spec(w2.shape), _full_spec((1, D_MODEL))]
    return specs, [nw, w1, w2, fw]


def _even_out_kernel(y_ref, g1_ref, oa_ref, oc_ref, os_ref, ow_ref, gt_ref, w_ref,
                     sh_ref, sc_ref, g2_ref, nw_ref, w1_ref, w2_ref, fw_ref, o_ref, *, final):
    gt = gt_ref[...]
    parts = [oa_ref[...].astype(BF16)]
    for hb in range(H_B):
        sl = slice(hb * LANES, (hb + 1) * LANES)
        ob = (gt[:, 3 * hb:3 * hb + 1] * oc_ref[:, sl] + gt[:, 3 * hb + 1:3 * hb + 2] * os_ref[:, sl]
              + gt[:, 3 * hb + 2:3 * hb + 3] * ow_ref[:, sl])
        parts.append(ob.astype(BF16))
    out = _dot(jnp.concatenate(parts, axis=-1), w_ref[...])
    y = y_ref[...]
    y1 = y + g1_ref[0] * out.reshape(y.shape)
    o_ref[...] = _mlp_value(y1, sh_ref[0], sc_ref[0], g2_ref[0], nw_ref[...], w1_ref, w2_ref, fw_ref[...], final)


def _even_out(til, y, mods, oa, oc, os_, ow, gt, w_out, mlp):
    mlp_specs, mlp_args = _mlp_specs(til, mlp)
    return pl.pallas_call(
        functools.partial(_even_out_kernel, final=mlp[4]),
        grid=til.grid,
        in_specs=[til.x_spec(D_MODEL), til.mod_spec(2), til.flat_spec(GROUP_W), til.flat_spec(GROUP_W),
                  til.flat_spec(GROUP_W), til.flat_spec(GROUP_W), til.flat_spec(LANES), _const_spec(w_out.shape)] + mlp_specs,
        out_specs=til.x_spec(D_MODEL),
        out_shape=jax.ShapeDtypeStruct(y.shape, F32),
        compiler_params=_cparams(("parallel", "parallel")),
        name="even_out_mlp",
    )(y, mods, oa, oc, os_, ow, gt, w_out, mods, mods, mods, *mlp_args)


def _gla_inproj_kernel(x_ref, sh_ref, sc_ref, nw_ref, w_ref, wgl_ref, wgate_ref, bgate_ref,
                       q_ref, k_ref, v_ref, r_ref, la_ref):
    h = _norm_mod(x_ref[...], nw_ref[...], sh_ref[0], sc_ref[0]).reshape(-1, D_MODEL).astype(BF16)
    nk = H_C * DK_C
    nv = H_C * DV_C
    q_ref[...] = _dot(h, w_ref[:, 0:nk]) * (DK_C ** -0.5)
    k_ref[...] = _dot(h, w_ref[:, nk:2 * nk])
    v_ref[...] = _dot(h, w_ref[:, 2 * nk:2 * nk + nv])
    r_ref[...] = _dot(h, w_ref[:, 2 * nk + nv:2 * nk + 2 * nv])
    gl = _dot(h, wgl_ref[...])
    x = _dot(gl.astype(BF16), wgate_ref[...]) + bgate_ref[...]
    log_sig = jnp.minimum(x, 0.0) - jnp.log1p(jnp.exp(-jnp.abs(x)))
    la_ref[...] = log_sig / GATE_TAU


def _gla_inproj(til, x, mods, nw, w_main, w_gl, w_gate, b_gate):
    rows_total = til.nb * til.r
    widths = (H_C * DK_C, H_C * DK_C, H_C * DV_C, H_C * DV_C, H_C * DK_C)
    return pl.pallas_call(
        _gla_inproj_kernel,
        grid=til.grid,
        in_specs=[til.x_spec(D_MODEL), til.mod_spec(0), til.mod_spec(1), _full_spec((1, D_MODEL)),
                  _const_spec(w_main.shape), _const_spec(w_gl.shape), _const_spec(w_gate.shape),
                  _full_spec(b_gate.shape)],
        out_specs=[til.flat_spec(c) for c in widths],
        out_shape=[jax.ShapeDtypeStruct((rows_total, c), F32) for c in widths],
        compiler_params=_cparams(("parallel", "parallel")),
        name="gla_inproj",
    )(x, mods, mods, nw, w_main, w_gl, w_gate, b_gate)


def _cumsum_rows(g):
    c = g.shape[0]
    row = _iota((c, 1), 0)
    b = g
    shift = 1
    while shift < c:
        b = b + jnp.where(row >= shift, pltpu.roll(b, shift, 0), 0.0)
        shift *= 2
    return b


def _gla_prep(q, k, g, c, sub, n_real):
    b = _cumsum_rows(g)
    qe = (q * jnp.exp(b)).astype(BF16)
    lane_c = _iota((sub, c), 1)
    row_s = _iota((sub, c), 0)
    att_rows = []
    for blk in range(c // sub):
        lo = blk * sub
        qi, ki, bi = q[lo:lo + sub], k[lo:lo + sub], b[lo:lo + sub]
        diag = jnp.zeros((sub, c), F32)
        for jj in range(min(sub, max(n_real - lo, 0))):
            e = jnp.exp(jnp.minimum(bi - bi[jj:jj + 1], 0.0))
            col = jnp.sum(qi * ki[jj:jj + 1] * e, axis=-1, keepdims=True)
            diag = jnp.where(lane_c == lo + jj, col, diag)
        att = jnp.where(lane_c - lo <= row_s, diag, 0.0)
        if blk > 0:
            bs = b[lo - 1:lo]
            q_in = qi * jnp.exp(bi - bs)
            k_out = k * jnp.exp(jnp.minimum(bs - b, 0.0))
            att = jnp.where(lane_c < lo, _dot_nt(q_in.astype(BF16), k_out.astype(BF16)), att)
        att_rows.append(att)
    att = att_rows[0] if len(att_rows) == 1 else jnp.concatenate(att_rows, axis=0)
    bl = b[c - 1:c]
    kd = (k * jnp.exp(bl - b)).astype(BF16)
    eye = _iota((DK_C, DK_C), 0) == _iota((DK_C, DK_C), 1)
    decay = jnp.sum(jnp.where(eye, jnp.exp(bl), 0.0), axis=-1, keepdims=True)
    return qe, att, kd, decay


def _gla_prep_bounded(q, k, g, c, sub, n_real):
    del sub, n_real
    b = _cumsum_rows(g)
    r = b[0:1]
    bl = b[c - 1:c]
    q_in = q * jnp.exp(b - r)
    k_out = k * jnp.exp(r - b)
    att = jnp.where(_iota((c, c), 0) >= _iota((c, c), 1), _dot_nt(q_in.astype(BF16), k_out.astype(BF16)), 0.0)
    qe = (q_in * jnp.exp(r)).astype(BF16)
    kd = (k_out * jnp.exp(bl - r)).astype(BF16)
    eye = _iota((DK_C, DK_C), 0) == _iota((DK_C, DK_C), 1)
    decay = jnp.sum(jnp.where(eye, jnp.exp(bl), 0.0), axis=-1, keepdims=True)
    return qe, att, kd, decay


def _gla_apply(state, prep, v):
    qe, att, kd, decay = prep
    o = _dot(qe, state.astype(BF16)) + _dot(att, v)
    return o, decay * state + _dot_tn(kd, v.astype(BF16))


def _gla_rec_kernel(*refs, tt, c, sub, hp, nseq, has_s0):
    if has_s0:
        q_ref, k_ref, v_ref, g_ref, s0_ref, o_ref, sfin_ref, s_ref = refs
    else:
        q_ref, k_ref, v_ref, g_ref, o_ref, sfin_ref, s_ref = refs
    t = pl.program_id(2)

    @pl.when(t == 0)
    def _():
        for bi in range(nseq):
            for hh in range(hp):
                s_ref[bi * hp + hh] = s0_ref[bi, hh] if has_s0 else jnp.zeros((DK_C, DV_C), F32)

    def kcols(hh):
        return slice(hh * DK_C, (hh + 1) * DK_C)

    def vcols(hh):
        return slice(hh * DV_C, (hh + 1) * DV_C)

    def run(prep_fn):
        if tt < c:
            pad = lambda x: _pad_rows(x, c)
            for bi in range(nseq):
                for hh in range(hp):
                    prep = prep_fn(pad(q_ref[bi, :, kcols(hh)]), pad(k_ref[bi, :, kcols(hh)]),
                                   pad(g_ref[bi, :, kcols(hh)]), c, sub, tt)
                    si = bi * hp + hh
                    o, s_ref[si] = _gla_apply(s_ref[si], prep, pad(v_ref[bi, :, vcols(hh)]))
                    o_ref[bi, :, vcols(hh)] = o[:tt]
            return
        assert nseq == 1
        per_trip = 2 if (tt // c) % 2 == 0 else 1

        def body(ci, carry):
            for hh in range(hp):
                rows = [pl.ds(pl.multiple_of((ci * per_trip + u) * c, c), c) for u in range(per_trip)]
                preps = [prep_fn(q_ref[0, r, kcols(hh)], k_ref[0, r, kcols(hh)], g_ref[0, r, kcols(hh)],
                                 c, sub, c) for r in rows]
                state = s_ref[hh]
                for r, prep in zip(rows, preps):
                    o_ref[0, r, vcols(hh)], state = _gla_apply(state, prep, v_ref[0, r, vcols(hh)])
                s_ref[hh] = state
            return carry
        lax.fori_loop(0, tt // c // per_trip, body, 0)

    if tt > c:
        g_all = g_ref[0]
        chunk_decay = -jnp.sum(g_all.reshape(tt // c, c, g_all.shape[-1]), axis=1)
    else:
        chunk_decay = -jnp.sum(g_ref[...], axis=1)
    bounded = jnp.max(chunk_decay) <= GLA_SAFE_DECAY

    @pl.when(bounded)
    def _():
        run(_gla_prep_bounded)

    @pl.when(jnp.logical_not(bounded))
    def _():
        run(_gla_prep)

    @pl.when(t == pl.num_programs(2) - 1)
    def _():
        for bi in range(nseq):
            for hh in range(hp):
                sfin_ref[bi, hh] = s_ref[bi * hp + hh]


def _gla_recurrence(q, k, v, g, s0, tt, c, sub, hp):
    n, t, _ = q.shape
    nseq = math.gcd(n, 2) if tt < c else 1
    kern = functools.partial(_gla_rec_kernel, tt=tt, c=c, sub=sub, hp=hp, nseq=nseq, has_s0=s0 is not None)
    kspec = pl.BlockSpec((nseq, tt, hp * DK_C), lambda b, h, i: (b, i, h))
    vspec = pl.BlockSpec((nseq, tt, hp * DV_C), lambda b, h, i: (b, i, h))
    sspec = pl.BlockSpec((nseq, hp, DK_C, DV_C), lambda b, h, i: (b, h, 0, 0))
    in_specs = [kspec, kspec, vspec, kspec]
    args = [q, k, v, g]
    if s0 is not None:
        in_specs.append(sspec)
        args.append(s0)
    return pl.pallas_call(
        kern,
        grid=(n // nseq, H_C // hp, t // tt),
        in_specs=in_specs,
        out_specs=[vspec, sspec],
        out_shape=[jax.ShapeDtypeStruct((n, t, H_C * DV_C), F32),
                   jax.ShapeDtypeStruct((n, H_C, DK_C, DV_C), F32)],
        scratch_shapes=[pltpu.VMEM((nseq * hp, DK_C, DV_C), F32)],
        compiler_params=_cparams(("parallel", "parallel", "arbitrary")),
        name="gla_recurrence",
    )(*args)


def _gla_out_kernel(y_ref, g1_ref, o_ref_in, r_ref, gnw_ref, w_ref,
                    sh_ref, sc_ref, g2_ref, nw_ref, w1_ref, w2_ref, fw_ref, out_ref, *, final):
    parts = []
    for h in range(H_C):
        sl = slice(h * DV_C, (h + 1) * DV_C)
        r = r_ref[:, sl]
        parts.append((_rms(o_ref_in[:, sl], gnw_ref[...]) * (r * jax.nn.sigmoid(r))).astype(BF16))
    out = _dot(jnp.concatenate(parts, axis=-1), w_ref[...])
    y = y_ref[...]
    y1 = y + g1_ref[0] * out.reshape(y.shape)
    out_ref[...] = _mlp_value(y1, sh_ref[0], sc_ref[0], g2_ref[0], nw_ref[...], w1_ref, w2_ref, fw_ref[...], final)


def _gla_out(til, y, mods, o, r, gnw, w_out, mlp):
    mlp_specs, mlp_args = _mlp_specs(til, mlp)
    return pl.pallas_call(
        functools.partial(_gla_out_kernel, final=mlp[4]),
        grid=til.grid,
        in_specs=[til.x_spec(D_MODEL), til.mod_spec(2), til.flat_spec(H_C * DV_C), til.flat_spec(H_C * DV_C),
                  _full_spec(gnw.shape), _const_spec(w_out.shape)] + mlp_specs,
        out_specs=til.x_spec(D_MODEL),
        out_shape=jax.ShapeDtypeStruct(y.shape, F32),
        compiler_params=_cparams(("parallel", "parallel")),
        name="gla_out_mlp",
    )(y, mods, o, r, gnw, w_out, mods, mods, mods, *mlp_args)


def _rope_tables(pos, d):
    inv = ROPE_THETA ** (-jnp.arange(0, d, 2, dtype=F32) / d)
    ang = pos.astype(F32)[:, None] * inv[None, :]
    cos, sin = jnp.cos(ang), jnp.sin(ang)
    rep = LANES // d
    c = jnp.tile(jnp.concatenate([cos, cos], axis=-1), (1, rep))
    s = jnp.tile(jnp.concatenate([-sin, sin], axis=-1), (1, rep))
    return c, s


def _rope_tables_t(pos, d):
    inv = ROPE_THETA ** (-jnp.arange(0, d, 2, dtype=F32) / d)
    ang = inv[:, None] * pos.astype(F32)[None, :]
    return jnp.cos(ang), jnp.sin(ang)


def _mods_for(mod_l, lo, hi):
    nb = hi - lo
    return mod_l[lo:hi].reshape(nb, 6, D_MODEL).transpose(1, 0, 2).reshape(6, nb, 1, D_MODEL)


def _prompt_tile_rows(t):
    return math.gcd(t, 256)


def _even_layer(yp, ys, mods_p, mods_s, caches, page_table, wts, lam_init, til_p, til_s, mlp):
    (c_dk, c_dv, c_cmp, c_sel, win_buf) = caches
    n, t, _ = yp.shape
    ns, s, _ = ys.shape
    n_pages = page_table.shape[1]
    past = n_pages * PAGE_SIZE
    w_in = wts["w_in"]
    n_main = w_in.shape[1] - 3 * H_B
    w_main = w_in[:, :n_main].astype(BF16)
    w_gate = jnp.pad(w_in[:, n_main:], ((0, 0), (0, LANES - 3 * H_B))).astype(BF16)
    nw1 = wts["norm1"].reshape(1, D_MODEL)
    lam_w = wts["lam_w"]
    sub_w = wts["sub_w"].reshape(1, DV_A)
    wc, pe, w2c = _compress_weights(wts["cmp_pe"], wts["cmp_w1"], wts["cmp_w2"])
    w_out = wts["w_out"].astype(BF16)

    tabs_p = _rope_tables(jnp.arange(t), DK_A) + _rope_tables(jnp.arange(t), DH_B)
    kt_tabs = _rope_tables_t(jnp.arange(t), DK_A)
    qa, kat, va, qb, cmp_kv, sel_kv, win_kv, gt = _even_inproj(til_p, yp, mods_p, nw1, w_main, w_gate, tabs_p, kt_tabs)
    r3 = lambda a: a.reshape(n, t, a.shape[-1])
    tq = _prompt_tile_rows(t)
    tq_attn = math.gcd(t, 2 * tq)
    oa = _diff_attn_prompt(r3(qa), kat, r3(va), lam_w, sub_w, lam_init, tq_attn, tq)
    ccmp = _compress_prompt(r3(cmp_kv), wc, pe, w2c)
    n_cmp = (t - CMP_LEN) // CMP_STRIDE + 1
    n_sel = -(-t // SEL_BLOCK)
    o_cmp, selm = _cmp_topk(r3(qb), ccmp, nb=1, rq=tq, n_cmp=n_cmp, n_sel=n_sel, pos_base=0)
    o_sel = _nsa_attn_prompt(r3(qb), r3(sel_kv), selm, tq_attn, tq)
    o_win = _nsa_attn_prompt(r3(qb), r3(win_kv), None, tq_attn, tq)
    f2 = lambda a: a.reshape(n * t, a.shape[-1])
    yp = _even_out(til_p, yp, mods_p, f2(oa), f2(o_cmp), f2(o_sel), f2(o_win), gt, w_out, mlp)
    wl = min(WINDOW, t)
    ka_state = jnp.transpose(kat.reshape(n, H_A, 2, DK_A, t), (0, 4, 1, 2, 3))
    st_p = (ka_state, r3(va).reshape(n, t, H_A, DV_A),
            r3(cmp_kv).reshape(n, t, 2, G_B, DH_B), r3(sel_kv).reshape(n, t, 2, G_B, DH_B),
            r3(win_kv)[:, t - wl:].reshape(n, wl, 2, G_B, DH_B))

    pos_s = past + jnp.arange(s)
    tabs_s = tuple(jnp.tile(x, (til_s.b, 1)) for x in _rope_tables(pos_s, DK_A) + _rope_tables(pos_s, DH_B))
    qa, ka, va, qb, cmp_kv, sel_kv, win_kv, gt = _even_inproj(til_s, ys, mods_s, nw1, w_main, w_gate, tabs_s)
    r3 = lambda a: a.reshape(ns, s, a.shape[-1])
    slot_cache = lambda c: c.reshape(c.shape[0], PAGE_SIZE * SLOTS, LANES)
    c_dkt = jnp.transpose(c_dk, (0, 2, 3, 4, 1)).reshape(c_dk.shape[0], H_A * 2 * DK_A, PAGE_SIZE)
    oa = _diff_attn_sample(r3(qa), r3(ka), r3(va), c_dkt, slot_cache(c_dv), page_table, lam_w, sub_w, lam_init)
    ccmp = _compress_sample(slot_cache(c_cmp), page_table, wc, pe, w2c)
    total = past + s
    n_cmp = (total - CMP_LEN) // CMP_STRIDE + 1
    n_sel = -(-total // SEL_BLOCK)
    assert n_cmp <= ccmp.shape[3] and n_sel <= LANES
    nb = math.gcd(ns, LANES // s)
    o_cmp, selm = _cmp_topk(r3(qb), ccmp, nb=nb, rq=s, n_cmp=n_cmp, n_sel=n_sel, pos_base=past)
    o_sel = _sel_attn_sample(r3(qb), selm, r3(sel_kv), slot_cache(c_sel), page_table)
    wb = win_buf.shape[1]
    o_win, win_out = _win_attn_sample(r3(qb), win_buf.reshape(ns, wb * SLOTS, LANES), r3(win_kv), past)
    f2 = lambda a: a.reshape(ns * s, a.shape[-1])
    ys = _even_out(til_s, ys, mods_s, f2(oa), f2(o_cmp), f2(o_sel), f2(o_win), gt, w_out, mlp)
    st_s = (r3(ka).reshape(ns, s, H_A, 2, DK_A), r3(va).reshape(ns, s, H_A, DV_A),
            r3(cmp_kv).reshape(ns, s, 2, G_B, DH_B), r3(sel_kv).reshape(ns, s, 2, G_B, DH_B),
            win_out.reshape(ns, wb, 2, G_B, DH_B))
    return yp, ys, st_p, st_s


def _odd_layer(yp, ys, mods_p, mods_s, s0, wts, til_p, til_s, mlp):
    w_in = wts["w_in"]
    n_main = w_in.shape[1] - GATE_RANK
    w_main = w_in[:, :n_main].astype(BF16)
    w_gl = jnp.pad(w_in[:, n_main:], ((0, 0), (0, LANES - GATE_RANK))).astype(BF16)
    w_gate = jnp.pad(wts["w_gate"], ((0, LANES - GATE_RANK), (0, 0))).astype(BF16)
    b_gate = wts["b_gate"].reshape(1, -1)
    nw1 = wts["norm1"].reshape(1, D_MODEL)
    gnw = wts["gnorm"].reshape(1, DV_C)
    w_out = wts["w_out"].astype(BF16)
    outs = []
    for y, mods, til, state in ((yp, mods_p, til_p, None), (ys, mods_s, til_s, s0)):
        n, t, _ = y.shape
        q, k, v, r, la = _gla_inproj(til, y, mods, nw1, w_main, w_gl, w_gate, b_gate)
        r3 = lambda a: a.reshape(n, t, a.shape[-1])
        c = math.gcd(t, GLA_CHUNK)
        if c >= GLA_SUB:
            tt, cc, sub = math.gcd(t, 8 * c), c, GLA_SUB
        else:
            tt, cc, sub = t, GLA_SUB, GLA_SUB
        o, s_fin = _gla_recurrence(r3(q), r3(k), r3(v), r3(la), state, tt, cc, sub, H_C)
        y = _gla_out(til, y, mods, o.reshape(n * t, -1), r, gnw, w_out, mlp)
        outs.append((y, s_fin))
    return outs[0][0], outs[1][0], outs[0][1], outs[1][1]


def kernel(x_prompt, x_sample, c_prompt, c_sample, cache_diff_k, cache_diff_v, cache_cmp_kv, cache_sel_kv,
           state_win_kv, state_gla, page_table, norm1_w, norm2_w, ada_w, ada_b, even_w_in, even_w_out,
           diff_lambda_w, diff_subln_w, cmp_pe, cmp_w1, cmp_w2, gla_w_in, gla_w_gate, gla_b_gate, gla_norm_w,
           gla_w_out, mlp_w1, mlp_w2, final_norm_w):
    depth = ada_w.shape[0]
    n, t, _ = x_prompt.shape
    ns, s, _ = x_sample.shape
    til_p = _Tiling(n, t, 1, _prompt_tile_rows(t))
    til_s = _Tiling(ns, s, math.gcd(ns, 256 // s), s)

    pad = (-(n + ns)) % 8
    c_all = jnp.concatenate([c_prompt, c_sample, jnp.zeros((pad, D_MODEL), F32)], axis=0)
    mod = _adaln(c_all, ada_w, ada_b)

    yp, ys = x_prompt, x_sample
    st_p = [[] for _ in range(6)]
    st_s = [[] for _ in range(6)]
    fw = final_norm_w.reshape(1, D_MODEL)
    for l in range(depth):
        mods_p = _mods_for(mod[l], 0, n)
        mods_s = _mods_for(mod[l], n, n + ns)
        mlp = (norm2_w[l].reshape(1, D_MODEL), mlp_w1[l].astype(BF16), mlp_w2[l].astype(BF16), fw, l == depth - 1)
        if l % 2 == 0:
            e = l // 2
            lam_init = 0.8 - 0.6 * math.exp(-0.3 * l)
            wts = dict(w_in=even_w_in[e], w_out=even_w_out[e], lam_w=diff_lambda_w[e], sub_w=diff_subln_w[e],
                       cmp_pe=cmp_pe[e], cmp_w1=cmp_w1[e], cmp_w2=cmp_w2[e], norm1=norm1_w[l])
            caches = (cache_diff_k[e], cache_diff_v[e], cache_cmp_kv[e], cache_sel_kv[e], state_win_kv[e])
            yp, ys, sp, ss = _even_layer(yp, ys, mods_p, mods_s, caches, page_table, wts, lam_init, til_p, til_s, mlp)
            for i in range(5):
                st_p[i].append(sp[i])
                st_s[i].append(ss[i])
        else:
            o = l // 2
            wts = dict(w_in=gla_w_in[o], w_gate=gla_w_gate[o], b_gate=gla_b_gate[o], gnorm=gla_norm_w[o],
                       w_out=gla_w_out[o], norm1=norm1_w[l])
            yp, ys, gp, gs = _odd_layer(yp, ys, mods_p, mods_s, state_gla[o], wts, til_p, til_s, mlp)
            st_p[5].append(gp)
            st_s[5].append(gs)
    outs_p = [jnp.stack(x, axis=0) for x in st_p]
    outs_s = [jnp.stack(x, axis=0) for x in st_s]
    return (yp, ys, *outs_p, *outs_s)
```

```python
import functools
import math

import jax
import jax.numpy as jnp
from jax import lax
from jax.experimental import pallas as pl
from jax.experimental.pallas import tpu as pltpu

F32 = jnp.float32
BF16 = jnp.bfloat16

D_MODEL = 1024
PAGE_SIZE = 128
H_A = 4
DK_A = 64
DV_A = 128
H_B = 4
G_B = 2
R_B = 2
DH_B = 128
CMP_LEN = 32
CMP_STRIDE = 16
SEL_BLOCK = 64
TOP_N = 16
WINDOW = 512
H_C = 4
DK_C = 128
DV_C = 256
GATE_RANK = 16
GATE_TAU = 16.0
GLA_CHUNK = 64
GLA_SUB = 16
GLA_SAFE_DECAY = 80.0
D_FF = 4 * D_MODEL
ROPE_THETA = 10000.0
EPS = 1e-6

LANES = 128
GROUP_W = 512
assert GROUP_W == H_A * 2 * DK_A == H_A * DV_A == H_B * DH_B == 2 * G_B * DH_B
SLOTS = GROUP_W // LANES
MASK_NEG = -1e30
LOG2E = 1.4426950408889634
SAFE_SHIFT = 56.0
SEL_NEG = 32768.0
VMEM_LIMIT_MB = 56


def _cparams(sem, vmem_mb=VMEM_LIMIT_MB):
    return pltpu.CompilerParams(dimension_semantics=sem, vmem_limit_bytes=vmem_mb * 1024 * 1024)


def _dot(a, b):
    return jnp.dot(a, b, preferred_element_type=F32)


def _dot_nt(a, b):
    return lax.dot_general(a, b, (((1,), (1,)), ((), ())), preferred_element_type=F32)


def _dot_tn(a, b):
    return lax.dot_general(a, b, (((0,), (0,)), ((), ())), preferred_element_type=F32)


def _iota(shape, dim):
    return lax.broadcasted_iota(jnp.int32, shape, dim)


def _rms(x, w):
    ms = jnp.mean(x * x, axis=-1, keepdims=True)
    return x * lax.rsqrt(ms + EPS) * w


def _norm_mod(x, nw, shift, scale):
    return _rms(x, nw) * (1.0 + scale) + shift


class _Tiling:
    def __init__(self, nb, r, b, rt):
        assert nb % b == 0 and r % rt == 0 and (b == 1 or rt == r)
        self.nb, self.r, self.b, self.rt = nb, r, b, rt
        self.grid = (nb // b, r // rt)
        self.rows = b * rt
        self.nrb = r // rt

    def x_spec(self, d):
        return pl.BlockSpec((self.b, self.rt, d), lambda i, j: (i, j, 0))

    def mod_spec(self, k):
        return pl.BlockSpec((1, self.b, 1, D_MODEL), lambda i, j: (k, i, 0, 0))

    def flat_spec(self, c):
        nrb = self.nrb
        return pl.BlockSpec((self.rows, c), lambda i, j: (i * nrb + j, 0))

    def tab_spec(self):
        return pl.BlockSpec((self.rows, LANES), lambda i, j: (j, 0))


def _full_spec(shape):
    nd = len(shape)
    return pl.BlockSpec(shape, lambda *_: (0,) * nd)


def _const_spec(shape):
    nd = len(shape)
    return pl.BlockSpec(shape, lambda *_: (0,) * nd, pipeline_mode=pl.Buffered(1))


def _adaln_kernel(c_ref, w_ref, b_ref, o_ref):
    c = c_ref[...]
    a = (c * jax.nn.sigmoid(c)).astype(BF16)
    o_ref[0] = _dot(a, w_ref[0].astype(BF16)) + b_ref[0]


def _adaln(c_all, ada_w, ada_b):
    depth, d, n6 = ada_w.shape
    rows = c_all.shape[0]
    tn = math.gcd(n6, 12 * LANES)
    return pl.pallas_call(
        _adaln_kernel,
        grid=(depth, n6 // tn),
        in_specs=[pl.BlockSpec((rows, d), lambda l, j: (0, 0)),
                  pl.BlockSpec((1, d, tn), lambda l, j: (l, 0, j)),
                  pl.BlockSpec((1, 1, tn), lambda l, j: (l, 0, j))],
        out_specs=pl.BlockSpec((1, rows, tn), lambda l, j: (l, 0, j)),
        out_shape=jax.ShapeDtypeStruct((depth, rows, n6), F32),
        compiler_params=_cparams(("parallel", "parallel")),
        name="adaln",
    )(c_all, ada_w, ada_b.reshape(depth, 1, n6))


def _swap_half(x, half):
    if 2 * half == LANES:
        return pltpu.roll(x, half, 1)
    lane = _iota((1, LANES), 1)
    lo = (lane % (2 * half)) < half
    return jnp.where(lo, pltpu.roll(x, LANES - half, 1), pltpu.roll(x, half, 1))


def _even_inproj_kernel(*refs, k_feature_major):
    if k_feature_major:
        (x_ref, sh_ref, sc_ref, nw_ref, w_ref, wg_ref, c64_ref, s64_ref, c128_ref, s128_ref, wkt_ref, ct_ref, st_ref,
         qa_ref, ka_ref, va_ref, qb_ref, cmp_ref, sel_ref, win_ref, gt_ref) = refs
    else:
        (x_ref, sh_ref, sc_ref, nw_ref, w_ref, wg_ref, c64_ref, s64_ref, c128_ref, s128_ref,
         qa_ref, ka_ref, va_ref, qb_ref, cmp_ref, sel_ref, win_ref, gt_ref) = refs
    h = _norm_mod(x_ref[...], nw_ref[...], sh_ref[0], sc_ref[0])
    h = h.reshape(-1, D_MODEL).astype(BF16)
    c64, s64, c128, s128 = c64_ref[...], s64_ref[...], c128_ref[...], s128_ref[...]

    def rope64(p):
        return p * c64 + _swap_half(p, DK_A // 2) * s64

    def rope128(p):
        return p * c128 + _swap_half(p, DH_B // 2) * s128

    def project(ref, off, ropes):
        p = _dot(h, w_ref[:, off:off + GROUP_W])
        for j, rope in enumerate(ropes):
            sl = slice(j * LANES, (j + 1) * LANES)
            ref[:, sl] = p[:, sl] if rope is None else rope(p[:, sl])

    project(qa_ref, 0, [rope64] * 4)
    if k_feature_major:
        kt = _dot_nt(wkt_ref[...], h)
        ct, st = ct_ref[...], st_ref[...]
        half = DK_A // 2
        for grp in range(H_A * 2):
            x1 = kt[grp * DK_A:grp * DK_A + half]
            x2 = kt[grp * DK_A + half:(grp + 1) * DK_A]
            ka_ref[0, grp * DK_A:grp * DK_A + half, :] = x1 * ct - x2 * st
            ka_ref[0, grp * DK_A + half:(grp + 1) * DK_A, :] = x2 * ct + x1 * st
    else:
        project(ka_ref, GROUP_W, [rope64] * 4)
    project(va_ref, 2 * GROUP_W, [None] * 4)
    project(qb_ref, 3 * GROUP_W, [rope128] * 4)
    for t, ref in enumerate((cmp_ref, sel_ref, win_ref)):
        project(ref, (4 + t) * GROUP_W, [rope128, rope128, None, None])
    gt_ref[...] = jax.nn.sigmoid(_dot(h, wg_ref[...]))


def _even_inproj(til, x, mods, nw, w_main, w_gate, tabs, kt_tabs=None):
    rows_total = til.nb * til.r
    widths = (GROUP_W,) * 7 + (LANES,)
    in_specs = [til.x_spec(D_MODEL), til.mod_spec(0), til.mod_spec(1), _full_spec((1, D_MODEL)),
                _const_spec(w_main.shape), _const_spec(w_gate.shape)] + [til.tab_spec()] * 4
    args = [x, mods, mods, nw, w_main, w_gate, *tabs]
    out_specs = [til.flat_spec(c) for c in widths]
    out_shape = [jax.ShapeDtypeStruct((rows_total, c), F32) for c in widths]
    if kt_tabs is not None:
        assert til.b == 1
        w_kt = jnp.transpose(w_main[:, GROUP_W:2 * GROUP_W])
        in_specs += [_const_spec(w_kt.shape)] + [pl.BlockSpec((DK_A // 2, til.rt), lambda i, j: (0, j))] * 2
        args += [w_kt, *kt_tabs]
        out_specs[1] = pl.BlockSpec((1, GROUP_W, til.rt), lambda i, j: (i, 0, j))
        out_shape[1] = jax.ShapeDtypeStruct((til.nb, GROUP_W, til.r), F32)
    return pl.pallas_call(
        functools.partial(_even_inproj_kernel, k_feature_major=kt_tabs is not None),
        grid=til.grid,
        in_specs=in_specs,
        out_specs=out_specs,
        out_shape=out_shape,
        compiler_params=_cparams(("parallel", "parallel")),
        name="even_inproj",
    )(*args)


def _for_tiles(lo, hi, fn):
    n = hi - lo

    def pair(j, carry):
        fn(lo + 2 * j)
        fn(lo + 2 * j + 1)
        return carry

    lax.fori_loop(0, n // 2, pair, 0)

    @pl.when(n % 2 == 1)
    def _():
        fn(hi - 1)


def _row_sumsq(x):
    xf = x.astype(F32)
    return _dot((xf * xf).astype(BF16), jnp.ones((LANES, LANES), BF16))


def _key_norm_bound(k_ref, n_keys, tile):
    def body(kb, m):
        k = k_ref[0, pl.ds(pl.multiple_of(kb * tile, tile), tile), :].astype(BF16)
        return jnp.maximum(m, _row_sumsq(k))

    m = lax.fori_loop(0, n_keys // tile, body, jnp.zeros((tile, LANES), F32))
    return jnp.max(m, axis=0, keepdims=True)


def _score_bound(q2, kmax2):
    qmax2 = jnp.max(_row_sumsq(q2), axis=0, keepdims=True)
    return jnp.sqrt(qmax2 * kmax2) * 1.05


def _two_pass_attention(mx_ref, acc_ref, lo, hi, last, scores, values, mask_body=False, bound=None):
    def lane_max(s):
        m = s[:, 0:LANES]
        for c in range(1, s.shape[1] // LANES):
            m = jnp.maximum(m, s[:, c * LANES:(c + 1) * LANES])
        return m

    def pass1(kb, masked):
        mx_ref[...] = jnp.maximum(mx_ref[...], lane_max(scores(kb, masked)))

    def exact_max():
        mx_ref[...] = jnp.full(mx_ref.shape, MASK_NEG, F32)
        _for_tiles(lo, hi, lambda kb: pass1(kb, mask_body))
        for kb in last:
            pass1(kb, True)
        mx_ref[...] = jnp.broadcast_to(jnp.max(mx_ref[...], axis=-1, keepdims=True), mx_ref.shape)

    if bound is None:
        exact_max()
    else:
        safe = jnp.max(bound) <= SAFE_SHIFT

        @pl.when(safe)
        def _():
            mx_ref[...] = jnp.broadcast_to(bound, mx_ref.shape)

        @pl.when(jnp.logical_not(safe))
        def _():
            exact_max()

    acc_ref[...] = jnp.zeros(acc_ref.shape, F32)

    def pass2(kb, masked):
        s = scores(kb, masked)
        m = mx_ref[...]
        p = jnp.concatenate([jnp.exp2(s[:, c * LANES:(c + 1) * LANES] - m) for c in range(s.shape[1] // LANES)],
                            axis=-1).astype(BF16)
        v = values(kb)
        v1 = jnp.concatenate([v, jnp.ones(v.shape, BF16)], axis=-1)
        acc_ref[...] += _dot(p, v1)

    _for_tiles(lo, hi, lambda kb: pass2(kb, mask_body))
    for kb in last:
        pass2(kb, True)
    return acc_ref[:, 0:LANES] / acc_ref[:, LANES:2 * LANES]


def _diff_lambda(lw, lam_init):
    a = jnp.sum(lw[0:1] * lw[1:2], axis=-1, keepdims=True)
    b = jnp.sum(lw[2:3] * lw[3:4], axis=-1, keepdims=True)
    return jnp.exp(a) - jnp.exp(b) + lam_init


def _diff_finalize(o2, tq, lam, sub_w, lam_init):
    od = o2[:tq] - lam * o2[tq:]
    return _rms(od, sub_w) * (1.0 - lam_init)


def _diff_flash_kernel(q_ref, k_ref, v_ref, lamw_ref, sub_ref, o_ref, q2_ref, mx_ref, acc_ref, kmax_ref, *,
                       tq, tk, lam_init):
    i = pl.program_id(2)

    def key_tile(kb):
        return k_ref[0, :, pl.ds(pl.multiple_of(kb * tk, tk), tk)].astype(BF16)

    @pl.when(i == 0)
    def _():
        def body(kb, m):
            kf = key_tile(kb).astype(F32)
            return jnp.maximum(m, jnp.sum(kf * kf, axis=0, keepdims=True))

        m = lax.fori_loop(0, k_ref.shape[2] // tk, body, jnp.zeros((1, tk), F32))
        kmax_ref[...] = jnp.broadcast_to(jnp.max(m, axis=-1, keepdims=True), kmax_ref.shape)

    q = q_ref[0] * (DK_A ** -0.5 * LOG2E)
    lane = _iota((1, LANES), 1)
    q2_ref[0:tq] = jnp.where(lane < DK_A, q, 0.0).astype(BF16)
    q2_ref[tq:2 * tq] = jnp.where(lane >= DK_A, q, 0.0).astype(BF16)
    bound = _score_bound(q2_ref[...], kmax_ref[0:1])

    def scores(kb, masked):
        s = _dot(q2_ref[...], key_tile(kb))
        if masked:
            r = _iota((2 * tq, tk), 0)
            qp = i * tq + jnp.where(r >= tq, r - tq, r)
            s = jnp.where(kb * tk + _iota((2 * tq, tk), 1) <= qp, s, MASK_NEG)
        return s

    def values(kb):
        return v_ref[0, pl.ds(pl.multiple_of(kb * tk, tk), tk), :].astype(BF16)

    per = tq // tk
    o2 = _two_pass_attention(mx_ref, acc_ref, 0, i * per, [i * per + u for u in range(per)], scores, values,
                             bound=bound)
    o_ref[0] = _diff_finalize(o2, tq, _diff_lambda(lamw_ref[...], lam_init), sub_ref[...], lam_init)


def _diff_attn_prompt(qa, kat, va, lam_w, sub_w, lam_init, tq, tk):
    n, t, _ = qa.shape
    kern = functools.partial(_diff_flash_kernel, tq=tq, tk=tk, lam_init=lam_init)
    return pl.pallas_call(
        kern,
        grid=(n, H_A, t // tq),
        in_specs=[pl.BlockSpec((1, tq, LANES), lambda b, h, i: (b, i, h)),
                  pl.BlockSpec((1, LANES, t), lambda b, h, i: (b, h, 0)),
                  pl.BlockSpec((1, t, LANES), lambda b, h, i: (b, 0, h)),
                  _full_spec(lam_w.shape), _full_spec(sub_w.shape)],
        out_specs=pl.BlockSpec((1, tq, LANES), lambda b, h, i: (b, i, h)),
        out_shape=jax.ShapeDtypeStruct((n, t, H_A * DV_A), F32),
        scratch_shapes=[pltpu.VMEM((2 * tq, LANES), BF16), pltpu.VMEM((2 * tq, LANES), F32),
                        pltpu.VMEM((2 * tq, 2 * LANES), F32), pltpu.VMEM((8, LANES), F32)],
        compiler_params=_cparams(("parallel", "parallel", "arbitrary")),
        name="diff_attn_prompt",
    )(qa, kat, va, lam_w, sub_w)


def _sel_onehot(rows, first_block):
    blk = _iota((rows, LANES), 0) // SEL_BLOCK + first_block
    return jnp.where(blk == _iota((rows, LANES), 1), 1.0, 0.0).astype(BF16)


def _nsa_flash_kernel(*refs, tq, tk, use_sel):
    if use_sel:
        q_ref, selm_ref, k_ref, v_ref, o_ref, q2_ref, mx_ref, acc_ref, kmax_ref = refs
    else:
        q_ref, k_ref, v_ref, o_ref, q2_ref, mx_ref, acc_ref, kmax_ref = refs
    i = pl.program_id(2)

    @pl.when(i == 0)
    def _():
        kmax_ref[...] = jnp.broadcast_to(_key_norm_bound(k_ref, k_ref.shape[1], tk), kmax_ref.shape)

    q = q_ref[0] * (DH_B ** -0.5 * LOG2E)
    for r in range(R_B):
        q2_ref[r * tq:(r + 1) * tq, 0:LANES] = q[:, r * LANES:(r + 1) * LANES].astype(BF16)
        if use_sel:
            q2_ref[r * tq:(r + 1) * tq, LANES:2 * LANES] = selm_ref[0, 0]
    bound = _score_bound(q2_ref[:, 0:LANES], kmax_ref[0:1])

    def scores(kb, masked):
        k = k_ref[0, pl.ds(pl.multiple_of(kb * tk, tk), tk), :].astype(BF16)
        if use_sel:
            k = jnp.concatenate([k, _sel_onehot(tk, kb * (tk // SEL_BLOCK))], axis=-1)
        s = _dot_nt(q2_ref[...], k)
        if masked:
            r = _iota((R_B * tq, tk), 0)
            qp = i * tq + jnp.where(r >= tq, r - tq, r)
            kp = kb * tk + _iota((R_B * tq, tk), 1)
            ok = kp <= qp
            if not use_sel:
                ok = ok & (kp > qp - WINDOW)
            s = jnp.where(ok, s, MASK_NEG)
        return s

    def values(kb):
        return v_ref[0, pl.ds(pl.multiple_of(kb * tk, tk), tk), :].astype(BF16)

    per = tq // tk
    lo = 0 if use_sel else jnp.maximum(i * per - WINDOW // tk, 0)
    o = _two_pass_attention(mx_ref, acc_ref, lo, i * per, [i * per + u for u in range(per)], scores, values,
                            mask_body=not use_sel, bound=bound)
    for r in range(R_B):
        o_ref[0, :, r * LANES:(r + 1) * LANES] = o[r * tq:(r + 1) * tq]


def _nsa_attn_prompt(qb, kv, selm, tq, tk):
    n, t, _ = qb.shape
    use_sel = selm is not None
    kd = 2 * LANES if use_sel else LANES
    kern = functools.partial(_nsa_flash_kernel, tq=tq, tk=tk, use_sel=use_sel)
    in_specs = [pl.BlockSpec((1, tq, R_B * LANES), lambda b, g, i: (b, i, g))]
    args = [qb]
    if use_sel:
        in_specs.append(pl.BlockSpec((1, 1, tq, LANES), lambda b, g, i: (b, g, i, 0)))
        args.append(selm)
    in_specs += [pl.BlockSpec((1, t, LANES), lambda b, g, i: (b, 0, g)),
                 pl.BlockSpec((1, t, LANES), lambda b, g, i: (b, 0, G_B + g))]
    args += [kv, kv]
    return pl.pallas_call(
        kern,
        grid=(n, G_B, t // tq),
        in_specs=in_specs,
        out_specs=pl.BlockSpec((1, tq, R_B * LANES), lambda b, g, i: (b, i, g)),
        out_shape=jax.ShapeDtypeStruct((n, t, H_B * DH_B), F32),
        scratch_shapes=[pltpu.VMEM((R_B * tq, kd), BF16), pltpu.VMEM((R_B * tq, LANES), F32),
                        pltpu.VMEM((R_B * tq, 2 * LANES), F32), pltpu.VMEM((8, LANES), F32)],
        compiler_params=_cparams(("parallel", "parallel", "arbitrary")),
        name="nsa_sel_prompt" if use_sel else "nsa_win_prompt",
    )(*args)


def _compress_core(load, wc_ref, pe_ref, w2_ref, wi, nchunk):
    rows = G_B * nchunk
    acc = jnp.zeros((rows, 2 * LANES), F32)
    pew = jnp.zeros((16, 2 * LANES), F32)
    for u in range(CMP_STRIDE // 2):
        lhs = jnp.concatenate([load(2 * u), load(2 * u + 1)], axis=-1).astype(BF16)
        w = wc_ref[wi, u]
        acc = acc + _dot(lhs, w)
        pew = pew + _dot(pe_ref[wi, u], w)
    first = acc[:, :LANES]
    second = pltpu.roll(acc[:, LANES:], rows - 1, 0)
    hid = first + second + pew[0:1, :LANES] + pew[8:9, LANES:]
    hid = hid * jax.nn.sigmoid(hid)
    return _dot(hid.astype(BF16), w2_ref[wi])


def _compress_prompt_kernel(*refs, nchunk):
    x_refs = refs[:G_B]
    wc_ref, pe_ref, w2_ref, o_ref = refs[G_B:]

    def load(tok):
        return jnp.concatenate([xr[0, pl.ds(tok, nchunk, stride=CMP_STRIDE), :] for xr in x_refs], axis=0)

    out = _compress_core(load, wc_ref, pe_ref, w2_ref, 0, nchunk)
    for g in range(G_B):
        o_ref[0, 0, g] = out[g * nchunk:(g + 1) * nchunk]


def _compress_sample_kernel(pt_ref, *refs, n_pages, spb):
    pg_refs = refs[:spb * n_pages]
    wc_ref, pe_ref, w2_ref, o_ref = refs[spb * n_pages:]
    cps = PAGE_SIZE // CMP_STRIDE
    nchunk = n_pages * cps
    for kv in range(2):
        by_tok = [jnp.swapaxes(pg[0, pl.ds(kv * G_B + g, PAGE_SIZE, stride=SLOTS), :]
                               .reshape(cps, CMP_STRIDE, LANES), 0, 1)
                  for smp in range(spb) for g in range(G_B) for pg in pg_refs[smp * n_pages:(smp + 1) * n_pages]]

        def load(tok):
            return jnp.concatenate([x[tok] for x in by_tok], axis=0)

        out = _compress_core(load, wc_ref, pe_ref, w2_ref, kv, spb * nchunk)
        for smp in range(spb):
            for g in range(G_B):
                seg = smp * G_B + g
                o_ref[smp, kv, g] = out[seg * nchunk:(seg + 1) * nchunk]


def _compress_weights(cmp_pe, cmp_w1, cmp_w2):
    w1 = cmp_w1.reshape(2, CMP_LEN, DH_B, DH_B)
    wab = jnp.concatenate([w1[:, :CMP_STRIDE], w1[:, CMP_STRIDE:]], axis=-1)
    wc = wab.reshape(2, CMP_STRIDE // 2, 2 * DH_B, 2 * DH_B).astype(BF16)
    pa = cmp_pe[:, :CMP_STRIDE].reshape(2, CMP_STRIDE // 2, 1, 2 * DH_B)
    pb = cmp_pe[:, CMP_STRIDE:].reshape(2, CMP_STRIDE // 2, 1, 2 * DH_B)
    z = jnp.zeros((2, CMP_STRIDE // 2, 7, 2 * DH_B), F32)
    pe = jnp.concatenate([pa, z, pb, z], axis=2).astype(BF16)
    return wc, pe, cmp_w2.astype(BF16)


def _compress_prompt(cmp_kv, wc, pe, w2):
    n, t, _ = cmp_kv.shape
    nchunk = t // CMP_STRIDE
    kern = functools.partial(_compress_prompt_kernel, nchunk=nchunk)
    return pl.pallas_call(
        kern,
        grid=(n, 2),
        in_specs=[pl.BlockSpec((1, t, DH_B), functools.partial(lambda b, kv, g: (b, 0, kv * G_B + g), g=g))
                  for g in range(G_B)] + [
                  pl.BlockSpec((1,) + wc.shape[1:], lambda b, kv: (kv, 0, 0, 0)),
                  pl.BlockSpec((1,) + pe.shape[1:], lambda b, kv: (kv, 0, 0, 0)),
                  pl.BlockSpec((1, DH_B, DH_B), lambda b, kv: (kv, 0, 0))],
        out_specs=pl.BlockSpec((1, 1, G_B, nchunk, DH_B), lambda b, kv: (b, kv, 0, 0, 0)),
        out_shape=jax.ShapeDtypeStruct((n, 2, G_B, nchunk, DH_B), F32),
        compiler_params=_cparams(("parallel", "parallel")),
        name="compress_prompt",
    )(*([cmp_kv] * G_B), wc, pe, w2)


def _compress_sample(cache, page_table, wc, pe, w2):
    n, n_pages = page_table.shape
    nchunk = n_pages * (PAGE_SIZE // CMP_STRIDE)
    spb = math.gcd(n, 2)
    kern = functools.partial(_compress_sample_kernel, n_pages=n_pages, spb=spb)
    page_specs = _page_specs(n_pages, PAGE_SIZE * SLOTS, LANES, spb)
    grid_spec = pltpu.PrefetchScalarGridSpec(
        num_scalar_prefetch=1,
        grid=(n // spb,),
        in_specs=page_specs + [
            pl.BlockSpec(wc.shape, lambda b, pt: (0, 0, 0, 0)),
            pl.BlockSpec(pe.shape, lambda b, pt: (0, 0, 0, 0)),
            pl.BlockSpec(w2.shape, lambda b, pt: (0, 0, 0))],
        out_specs=pl.BlockSpec((spb, 2, G_B, nchunk, DH_B), lambda b, pt: (b, 0, 0, 0, 0)),
    )
    return pl.pallas_call(
        kern,
        grid_spec=grid_spec,
        out_shape=jax.ShapeDtypeStruct((n, 2, G_B, nchunk, DH_B), F32),
        compiler_params=_cparams(("parallel",)),
        name="compress_sample",
    )(page_table, *([cache] * (spb * n_pages)), wc, pe, w2)


def _cmp_topk_kernel(q_ref, kc_ref, vc_ref, ocmp_ref, selm_ref, p_scr, v_scr, *,
                     nb, rq, nck, n_cmp, n_sel, nsp, pos_base):
    tq = nb * rq
    j = pl.program_id(2)
    scale = DH_B ** -0.5
    kidx = _iota((1, nck), 1)
    qpos_c = pos_base + j * rq + _iota((rq, 1), 0)
    valid = (kidx < n_cmp) & (CMP_STRIDE * kidx + (CMP_LEN - 1) <= qpos_c)
    for b in range(nb):
        kcc = kc_ref[b, 0, 0].astype(BF16)
        vcc = vc_ref[b, 0, 0].astype(BF16)
        q = q_ref[b] * scale
        q2 = jnp.concatenate([q[:, r * LANES:(r + 1) * LANES] for r in range(R_B)], axis=0).astype(BF16)
        valid2 = jnp.concatenate([valid] * R_B, axis=0)
        s = jnp.where(valid2, _dot_nt(q2, kcc), MASK_NEG)
        m = jnp.max(s, axis=-1, keepdims=True)
        p = jnp.where(valid2, jnp.exp(s - m), 0.0)
        p = p / jnp.maximum(jnp.sum(p, axis=-1, keepdims=True), 1e-30)
        o = _dot(p.astype(BF16), vcc)
        psum = jnp.zeros((rq, nck), F32)
        for r in range(R_B):
            ocmp_ref[b, :, r * LANES:(r + 1) * LANES] = o[r * rq:(r + 1) * rq]
            psum = psum + p[r * rq:(r + 1) * rq]
        p_scr[b * rq:(b + 1) * rq, :] = psum
    psum = p_scr[...]
    p_hi = psum.astype(BF16)
    p_lo = (psum - p_hi.astype(F32)).astype(BF16)
    srow = _iota((LANES, nck), 0)
    kcol = _iota((LANES, nck), 1)
    cov = (CMP_STRIDE * kcol < SEL_BLOCK * srow + SEL_BLOCK) & (CMP_STRIDE * kcol + CMP_LEN > SEL_BLOCK * srow)
    cov = cov & (srow < n_sel) & (kcol < n_cmp)
    cov_t = jnp.where(cov, 1.0, 0.0).astype(BF16)
    imp_t = _dot_nt(cov_t, p_hi) + _dot_nt(cov_t, p_lo)
    qpos_l = pos_base + j * rq + _iota((1, tq), 1) % rq
    sb = _iota((nsp, 1), 0)
    valid_b = (sb < n_sel) & (sb * SEL_BLOCK <= qpos_l)
    forced = (sb == 0) | (sb == qpos_l // SEL_BLOCK)
    vals = jnp.where(forced, jnp.inf, jnp.where(valid_b, imp_t[:nsp], -jnp.inf))
    v_scr[...] = vals

    def body(jj, cnt):
        vj = v_scr[pl.ds(jj, 1), :]
        tie = jnp.where(sb > jj, 1.0, 0.0)
        return cnt + jnp.where(vj > vals, 1.0, jnp.where(vj == vals, tie, 0.0))

    last_pos = pos_base + (j + 1) * rq - 1
    n_rank = jnp.minimum(last_pos // SEL_BLOCK + 1, nsp)
    cnt = lax.fori_loop(0, n_rank, body, jnp.zeros((nsp, tq), F32))
    keep = valid_b & (cnt < float(min(TOP_N, n_sel)))
    selm_t = jnp.where(keep, 0.0, -SEL_NEG)
    if nsp < LANES:
        selm_t = jnp.concatenate([selm_t, jnp.zeros((LANES - nsp, tq), F32)], axis=0)
    selm = selm_t.T.astype(BF16)
    for b in range(nb):
        selm_ref[b, 0] = selm[b * rq:(b + 1) * rq]


def _cmp_topk(qb, ccmp, *, nb, rq, n_cmp, n_sel, pos_base):
    n, r, _ = qb.shape
    nck = ccmp.shape[3]
    nsp = -(-n_sel // 8) * 8
    tq = nb * rq
    kern = functools.partial(_cmp_topk_kernel, nb=nb, rq=rq, nck=nck, n_cmp=n_cmp, n_sel=n_sel, nsp=nsp,
                             pos_base=pos_base)
    return pl.pallas_call(
        kern,
        grid=(n // nb, G_B, r // rq),
        in_specs=[pl.BlockSpec((nb, rq, R_B * LANES), lambda a, g, j: (a, j, g)),
                  pl.BlockSpec((nb, 1, 1, nck, DH_B), lambda a, g, j: (a, 0, g, 0, 0)),
                  pl.BlockSpec((nb, 1, 1, nck, DH_B), lambda a, g, j: (a, 1, g, 0, 0))],
        out_specs=[pl.BlockSpec((nb, rq, R_B * LANES), lambda a, g, j: (a, j, g)),
                   pl.BlockSpec((nb, 1, rq, LANES), lambda a, g, j: (a, g, j, 0))],
        out_shape=[jax.ShapeDtypeStruct((n, r, H_B * DH_B), F32),
                   jax.ShapeDtypeStruct((n, G_B, r, LANES), BF16)],
        scratch_shapes=[pltpu.VMEM((tq, nck), F32), pltpu.VMEM((nsp, tq), F32)],
        compiler_params=_cparams(("parallel", "parallel", "parallel")),
        name="cmp_topk",
    )(qb, ccmp, ccmp)


def _attend_scores(pieces):
    m = None
    for s, _ in pieces:
        ms = jnp.max(s, axis=-1, keepdims=True)
        m = ms if m is None else jnp.maximum(m, ms)
    l = jnp.zeros_like(m)
    acc = jnp.zeros((m.shape[0], LANES), F32)
    for s, v in pieces:
        p = jnp.exp(s - m)
        l = l + jnp.sum(p, axis=-1, keepdims=True)
        acc = acc + _dot(p.astype(BF16), v)
    return acc / l


def _attend_pieces(q2, pieces):
    scored = []
    for k, v, mask in pieces:
        s = _dot_nt(q2, k)
        scored.append((s if mask is None else jnp.where(mask, s, MASK_NEG), v))
    return _attend_scores(scored)


def _pad_rows(x, rows):
    return jnp.concatenate([x, jnp.zeros((rows - x.shape[0], x.shape[1]), x.dtype)], axis=0)


def _new_token_mask(m_rows, s):
    r = _iota((m_rows, LANES), 0) % s
    return _iota((m_rows, LANES), 1) <= r


def _diff_sample_kernel(pt_ref, *refs, n_pages, s, spb, lam_init):
    kp_refs = refs[:spb * n_pages]
    vp_refs = refs[spb * n_pages:2 * spb * n_pages]
    q_ref, kn_ref, vn_ref, lamw_ref, sub_ref, o_ref = refs[2 * spb * n_pages:]
    lane = _iota((1, LANES), 1)
    lam = _diff_lambda(lamw_ref[...], lam_init)
    new_mask = _new_token_mask(2 * s, s)
    for b in range(spb):
        kpages = kp_refs[b * n_pages:(b + 1) * n_pages]
        vpages = vp_refs[b * n_pages:(b + 1) * n_pages]
        for h in range(H_A):
            cols = slice(h * LANES, (h + 1) * LANES)
            q = q_ref[b, :, cols] * (DK_A ** -0.5)
            q2 = jnp.concatenate([jnp.where(lane < DK_A, q, 0.0), jnp.where(lane >= DK_A, q, 0.0)],
                                 axis=0).astype(BF16)
            kt_past = jnp.concatenate([r[0, cols, :].astype(BF16) for r in kpages], axis=1)
            s_past = _dot(q2, kt_past)
            vpast = jnp.concatenate([_slot_rows(r, h, PAGE_SIZE).astype(BF16) for r in vpages], axis=0)
            knew = _pad_rows(kn_ref[b, :, cols], LANES).astype(BF16)
            vnew = _pad_rows(vn_ref[b, :, cols], LANES).astype(BF16)
            s_new = jnp.where(new_mask, _dot_nt(q2, knew), MASK_NEG)
            o2 = _attend_scores([(s_past, vpast), (s_new, vnew)])
            o_ref[b, :, cols] = _diff_finalize(o2, s, lam, sub_ref[...], lam_init)


def _page_specs(n_pages, rows, width, spb=1):
    return [pl.BlockSpec((1, rows, width),
                         functools.partial(lambda b, pt, smp, j: (pt[b * spb + smp, j], 0, 0), smp=smp, j=j))
            for smp in range(spb) for j in range(n_pages)]


def _slot_rows(ref, slot, n_tok):
    return ref[0, pl.ds(slot, n_tok, stride=SLOTS), :]


def _diff_attn_sample(qa, ka, va, cache_k, cache_v, page_table, lam_w, sub_w, lam_init):
    n, s, _ = qa.shape
    n_pages = page_table.shape[1]
    spb = math.gcd(n, 2)
    kern = functools.partial(_diff_sample_kernel, n_pages=n_pages, s=s, spb=spb, lam_init=lam_init)
    row_spec = pl.BlockSpec((spb, s, GROUP_W), lambda b, pt: (b, 0, 0))
    grid_spec = pltpu.PrefetchScalarGridSpec(
        num_scalar_prefetch=1,
        grid=(n // spb,),
        in_specs=_page_specs(n_pages, 512, PAGE_SIZE, spb) + _page_specs(n_pages, PAGE_SIZE * SLOTS, LANES, spb)
        + [row_spec, row_spec, row_spec,
           pl.BlockSpec(lam_w.shape, lambda b, pt: (0, 0)), pl.BlockSpec(sub_w.shape, lambda b, pt: (0, 0))],
        out_specs=row_spec,
    )
    return pl.pallas_call(
        kern,
        grid_spec=grid_spec,
        out_shape=jax.ShapeDtypeStruct((n, s, GROUP_W), F32),
        compiler_params=_cparams(("parallel",)),
        name="diff_attn_sample",
    )(page_table, *([cache_k] * (spb * n_pages)), *([cache_v] * (spb * n_pages)), qa, ka, va, lam_w, sub_w)


def _sel_sample_kernel(pt_ref, *refs, n_pages, s, spb):
    pg_refs = refs[:spb * n_pages]
    q_ref, selm_ref, new_ref, o_ref = refs[spb * n_pages:]
    past = n_pages * PAGE_SIZE
    new_mask = _new_token_mask(R_B * s, s)
    oh_past = _sel_onehot(past, 0)
    new_blk = past // SEL_BLOCK
    oh_new = jnp.where((_iota((LANES, LANES), 1) == new_blk) & (_iota((LANES, LANES), 0) < s), 1.0, 0.0)
    oh_new = oh_new.astype(BF16)
    scale = DH_B ** -0.5
    for b in range(spb):
        pages = pg_refs[b * n_pages:(b + 1) * n_pages]
        for g in range(G_B):
            kcols = slice(g * LANES, (g + 1) * LANES)
            vcols = slice((G_B + g) * LANES, (G_B + g + 1) * LANES)
            selm = selm_ref[b, g]
            q2 = jnp.concatenate(
                [jnp.concatenate([(q_ref[b, :, (g * R_B + r) * LANES:(g * R_B + r + 1) * LANES] * scale).astype(BF16),
                                  selm], axis=-1) for r in range(R_B)], axis=0)
            kpast = jnp.concatenate([_slot_rows(r, g, PAGE_SIZE).astype(BF16) for r in pages], axis=0)
            vpast = jnp.concatenate([_slot_rows(r, G_B + g, PAGE_SIZE).astype(BF16) for r in pages], axis=0)
            kpast = jnp.concatenate([kpast, oh_past], axis=-1)
            knew = jnp.concatenate([_pad_rows(new_ref[b, :, kcols], LANES).astype(BF16), oh_new], axis=-1)
            vnew = _pad_rows(new_ref[b, :, vcols], LANES).astype(BF16)
            o2 = _attend_pieces(q2, [(kpast, vpast, None), (knew, vnew, new_mask)])
            for r in range(R_B):
                o_ref[b, :, (g * R_B + r) * LANES:(g * R_B + r + 1) * LANES] = o2[r * s:(r + 1) * s]


def _sel_attn_sample(qb, selm, sel_new, cache, page_table):
    n, s, _ = qb.shape
    n_pages = page_table.shape[1]
    assert s <= SEL_BLOCK and (n_pages * PAGE_SIZE) % SEL_BLOCK == 0
    spb = math.gcd(n, 2)
    kern = functools.partial(_sel_sample_kernel, n_pages=n_pages, s=s, spb=spb)
    row_spec = pl.BlockSpec((spb, s, GROUP_W), lambda b, pt: (b, 0, 0))
    page_specs = _page_specs(n_pages, PAGE_SIZE * SLOTS, LANES, spb)
    grid_spec = pltpu.PrefetchScalarGridSpec(
        num_scalar_prefetch=1,
        grid=(n // spb,),
        in_specs=page_specs
        + [row_spec, pl.BlockSpec((spb, G_B, s, LANES), lambda b, pt: (b, 0, 0, 0)), row_spec],
        out_specs=row_spec,
    )
    return pl.pallas_call(
        kern,
        grid_spec=grid_spec,
        out_shape=jax.ShapeDtypeStruct((n, s, GROUP_W), F32),
        compiler_params=_cparams(("parallel",)),
        name="nsa_sel_sample",
    )(page_table, *([cache] * (spb * n_pages)), qb, selm, sel_new)


def _win_sample_kernel(q_ref, buf_ref, new_ref, o_ref, wout_ref, *, s, wb, spb):
    new_mask = _new_token_mask(R_B * s, s)
    r = _iota((R_B * s, wb), 0) % s
    buf_mask = _iota((R_B * s, wb), 1) > r + (wb - WINDOW)
    scale = DH_B ** -0.5
    keep = (wb - s) * SLOTS
    for b in range(spb):
        for g in range(G_B):
            kcols = slice(g * LANES, (g + 1) * LANES)
            vcols = slice((G_B + g) * LANES, (G_B + g + 1) * LANES)
            q2 = jnp.concatenate(
                [(q_ref[b, :, (g * R_B + r_) * LANES:(g * R_B + r_ + 1) * LANES] * scale).astype(BF16)
                 for r_ in range(R_B)], axis=0)
            kbuf = buf_ref[b, pl.ds(g, wb, stride=SLOTS), :].astype(BF16)
            vbuf = buf_ref[b, pl.ds(G_B + g, wb, stride=SLOTS), :].astype(BF16)
            knew = _pad_rows(new_ref[b, :, kcols], LANES).astype(BF16)
            vnew = _pad_rows(new_ref[b, :, vcols], LANES).astype(BF16)
            o2 = _attend_pieces(q2, [(kbuf, vbuf, buf_mask), (knew, vnew, new_mask)])
            for r_ in range(R_B):
                o_ref[b, :, (g * R_B + r_) * LANES:(g * R_B + r_ + 1) * LANES] = o2[r_ * s:(r_ + 1) * s]
        wout_ref[b, 0:keep] = buf_ref[b, s * SLOTS:wb * SLOTS]
        for slot in range(SLOTS):
            wout_ref[b, pl.ds(keep + slot, s, stride=SLOTS), :] = new_ref[b, :, slot * LANES:(slot + 1) * LANES]


def _win_attn_sample(qb, win_buf, win_new, past):
    n, s, _ = qb.shape
    wb = win_buf.shape[1] // SLOTS
    assert past >= wb and wb % 8 == 0 and s % 8 == 0
    spb = math.gcd(n, 4)
    kern = functools.partial(_win_sample_kernel, s=s, wb=wb, spb=spb)
    row_spec = pl.BlockSpec((spb, s, GROUP_W), lambda b: (b, 0, 0))
    buf_spec = pl.BlockSpec((spb, wb * SLOTS, LANES), lambda b: (b, 0, 0))
    return pl.pallas_call(
        kern,
        grid=(n // spb,),
        in_specs=[row_spec, buf_spec, row_spec],
        out_specs=[row_spec, buf_spec],
        out_shape=[jax.ShapeDtypeStruct((n, s, GROUP_W), F32), jax.ShapeDtypeStruct((n, wb * SLOTS, LANES), F32)],
        compiler_params=_cparams(("parallel",)),
        name="nsa_win_sample",
    )(qb, win_buf, win_new)


def _mlp_value(y, sh, sc, g, nw, w1_ref, w2_ref, fw, final):
    h = _norm_mod(y, nw, sh, sc).reshape(-1, D_MODEL).astype(BF16)
    a = jnp.maximum(_dot(h, w1_ref[...]), 0.0)
    out = _dot((a * a).astype(BF16), w2_ref[...])
    y2 = y + g * out.reshape(y.shape)
    return _rms(y2, fw) if final else y2


def _mlp_specs(til, mlp):
    nw, w1, w2, fw, _ = mlp
    specs = [til.mod_spec(3), til.mod_spec(4), til.mod_spec(5), _full_spec((1, D_MODEL)),
             _const_spec(w1.shape), _const_spec(w2.shape), _full_spec((1, D_MODEL))]
    return specs, [nw, w1, w2, fw]


def _even_out_kernel(y_ref, g1_ref, oa_ref, oc_ref, os_ref, ow_ref, gt_ref, w_ref,
                     sh_ref, sc_ref, g2_ref, nw_ref, w1_ref, w2_ref, fw_ref, o_ref, *, final):
    gt = gt_ref[...]
    parts = [oa_ref[...].astype(BF16)]
    for hb in range(H_B):
        sl = slice(hb * LANES, (hb + 1) * LANES)
        ob = (gt[:, 3 * hb:3 * hb + 1] * oc_ref[:, sl] + gt[:, 3 * hb + 1:3 * hb + 2] * os_ref[:, sl]
              + gt[:, 3 * hb + 2:3 * hb + 3] * ow_ref[:, sl])
        parts.append(ob.astype(BF16))
    out = _dot(jnp.concatenate(parts, axis=-1), w_ref[...])
    y = y_ref[...]
    y1 = y + g1_ref[0] * out.reshape(y.shape)
    o_ref[...] = _mlp_value(y1, sh_ref[0], sc_ref[0], g2_ref[0], nw_ref[...], w1_ref, w2_ref, fw_ref[...], final)


def _even_out(til, y, mods, oa, oc, os_, ow, gt, w_out, mlp):
    mlp_specs, mlp_args = _mlp_specs(til, mlp)
    return pl.pallas_call(
        functools.partial(_even_out_kernel, final=mlp[4]),
        grid=til.grid,
        in_specs=[til.x_spec(D_MODEL), til.mod_spec(2), til.flat_spec(GROUP_W), til.flat_spec(GROUP_W),
                  til.flat_spec(GROUP_W), til.flat_spec(GROUP_W), til.flat_spec(LANES), _const_spec(w_out.shape)] + mlp_specs,
        out_specs=til.x_spec(D_MODEL),
        out_shape=jax.ShapeDtypeStruct(y.shape, F32),
        compiler_params=_cparams(("parallel", "parallel")),
        name="even_out_mlp",
    )(y, mods, oa, oc, os_, ow, gt, w_out, mods, mods, mods, *mlp_args)


def _gla_inproj_kernel(x_ref, sh_ref, sc_ref, nw_ref, w_ref, wgl_ref, wgate_ref, bgate_ref,
                       q_ref, k_ref, v_ref, r_ref, la_ref):
    h = _norm_mod(x_ref[...], nw_ref[...], sh_ref[0], sc_ref[0]).reshape(-1, D_MODEL).astype(BF16)
    nk = H_C * DK_C
    nv = H_C * DV_C
    q_ref[...] = _dot(h, w_ref[:, 0:nk]) * (DK_C ** -0.5)
    k_ref[...] = _dot(h, w_ref[:, nk:2 * nk])
    v_ref[...] = _dot(h, w_ref[:, 2 * nk:2 * nk + nv])
    r_ref[...] = _dot(h, w_ref[:, 2 * nk + nv:2 * nk + 2 * nv])
    gl = _dot(h, wgl_ref[...])
    x = _dot(gl.astype(BF16), wgate_ref[...]) + bgate_ref[...]
    log_sig = jnp.minimum(x, 0.0) - jnp.log1p(jnp.exp(-jnp.abs(x)))
    la_ref[...] = log_sig / GATE_TAU


def _gla_inproj(til, x, mods, nw, w_main, w_gl, w_gate, b_gate):
    rows_total = til.nb * til.r
    widths = (H_C * DK_C, H_C * DK_C, H_C * DV_C, H_C * DV_C, H_C * DK_C)
    return pl.pallas_call(
        _gla_inproj_kernel,
        grid=til.grid,
        in_specs=[til.x_spec(D_MODEL), til.mod_spec(0), til.mod_spec(1), _full_spec((1, D_MODEL)),
                  _const_spec(w_main.shape), _const_spec(w_gl.shape), _const_spec(w_gate.shape),
                  _full_spec(b_gate.shape)],
        out_specs=[til.flat_spec(c) for c in widths],
        out_shape=[jax.ShapeDtypeStruct((rows_total, c), F32) for c in widths],
        compiler_params=_cparams(("parallel", "parallel")),
        name="gla_inproj",
    )(x, mods, mods, nw, w_main, w_gl, w_gate, b_gate)


def _cumsum_rows(g):
    c = g.shape[0]
    row = _iota((c, 1), 0)
    b = g
    shift = 1
    while shift < c:
        b = b + jnp.where(row >= shift, pltpu.roll(b, shift, 0), 0.0)
        shift *= 2
    return b


def _gla_prep(q, k, g, c, sub, n_real):
    b = _cumsum_rows(g)
    qe = (q * jnp.exp(b)).astype(BF16)
    lane_c = _iota((sub, c), 1)
    row_s = _iota((sub, c), 0)
    att_rows = []
    for blk in range(c // sub):
        lo = blk * sub
        qi, ki, bi = q[lo:lo + sub], k[lo:lo + sub], b[lo:lo + sub]
        diag = jnp.zeros((sub, c), F32)
        for jj in range(min(sub, max(n_real - lo, 0))):
            e = jnp.exp(jnp.minimum(bi - bi[jj:jj + 1], 0.0))
            col = jnp.sum(qi * ki[jj:jj + 1] * e, axis=-1, keepdims=True)
            diag = jnp.where(lane_c == lo + jj, col, diag)
        att = jnp.where(lane_c - lo <= row_s, diag, 0.0)
        if blk > 0:
            bs = b[lo - 1:lo]
            q_in = qi * jnp.exp(bi - bs)
            k_out = k * jnp.exp(jnp.minimum(bs - b, 0.0))
            att = jnp.where(lane_c < lo, _dot_nt(q_in.astype(BF16), k_out.astype(BF16)), att)
        att_rows.append(att)
    att = att_rows[0] if len(att_rows) == 1 else jnp.concatenate(att_rows, axis=0)
    bl = b[c - 1:c]
    kd = (k * jnp.exp(bl - b)).astype(BF16)
    eye = _iota((DK_C, DK_C), 0) == _iota((DK_C, DK_C), 1)
    decay = jnp.sum(jnp.where(eye, jnp.exp(bl), 0.0), axis=-1, keepdims=True)
    return qe, att, kd, decay


def _gla_prep_bounded(q, k, g, c, sub, n_real):
    del sub, n_real
    b = _cumsum_rows(g)
    r = b[0:1]
    bl = b[c - 1:c]
    q_in = q * jnp.exp(b - r)
    k_out = k * jnp.exp(r - b)
    att = jnp.where(_iota((c, c), 0) >= _iota((c, c), 1), _dot_nt(q_in.astype(BF16), k_out.astype(BF16)), 0.0)
    qe = (q_in * jnp.exp(r)).astype(BF16)
    kd = (k_out * jnp.exp(bl - r)).astype(BF16)
    eye = _iota((DK_C, DK_C), 0) == _iota((DK_C, DK_C), 1)
    decay = jnp.sum(jnp.where(eye, jnp.exp(bl), 0.0), axis=-1, keepdims=True)
    return qe, att, kd, decay


def _gla_apply(state, prep, v):
    qe, att, kd, decay = prep
    o = _dot(qe, state.astype(BF16)) + _dot(att, v)
    return o, decay * state + _dot_tn(kd, v.astype(BF16))


def _gla_rec_kernel(*refs, tt, c, sub, hp, nseq, has_s0):
    if has_s0:
        q_ref, k_ref, v_ref, g_ref, s0_ref, o_ref, sfin_ref, s_ref = refs
    else:
        q_ref, k_ref, v_ref, g_ref, o_ref, sfin_ref, s_ref = refs
    t = pl.program_id(2)

    @pl.when(t == 0)
    def _():
        for bi in range(nseq):
            for hh in range(hp):
                s_ref[bi * hp + hh] = s0_ref[bi, hh] if has_s0 else jnp.zeros((DK_C, DV_C), F32)

    def kcols(hh):
        return slice(hh * DK_C, (hh + 1) * DK_C)

    def vcols(hh):
        return slice(hh * DV_C, (hh + 1) * DV_C)

    def run(prep_fn):
        if tt < c:
            pad = lambda x: _pad_rows(x, c)
            for bi in range(nseq):
                for hh in range(hp):
                    prep = prep_fn(pad(q_ref[bi, :, kcols(hh)]), pad(k_ref[bi, :, kcols(hh)]),
                                   pad(g_ref[bi, :, kcols(hh)]), c, sub, tt)
                    si = bi * hp + hh
                    o, s_ref[si] = _gla_apply(s_ref[si], prep, pad(v_ref[bi, :, vcols(hh)]))
                    o_ref[bi, :, vcols(hh)] = o[:tt]
            return
        assert nseq == 1
        per_trip = 2 if (tt // c) % 2 == 0 else 1

        def body(ci, carry):
            for hh in range(hp):
                rows = [pl.ds(pl.multiple_of((ci * per_trip + u) * c, c), c) for u in range(per_trip)]
                preps = [prep_fn(q_ref[0, r, kcols(hh)], k_ref[0, r, kcols(hh)], g_ref[0, r, kcols(hh)],
                                 c, sub, c) for r in rows]
                state = s_ref[hh]
                for r, prep in zip(rows, preps):
                    o_ref[0, r, vcols(hh)], state = _gla_apply(state, prep, v_ref[0, r, vcols(hh)])
                s_ref[hh] = state
            return carry
        lax.fori_loop(0, tt // c // per_trip, body, 0)

    if tt > c:
        g_all = g_ref[0]
        chunk_decay = -jnp.sum(g_all.reshape(tt // c, c, g_all.shape[-1]), axis=1)
    else:
        chunk_decay = -jnp.sum(g_ref[...], axis=1)
    bounded = jnp.max(chunk_decay) <= GLA_SAFE_DECAY

    @pl.when(bounded)
    def _():
        run(_gla_prep_bounded)

    @pl.when(jnp.logical_not(bounded))
    def _():
        run(_gla_prep)

    @pl.when(t == pl.num_programs(2) - 1)
    def _():
        for bi in range(nseq):
            for hh in range(hp):
                sfin_ref[bi, hh] = s_ref[bi * hp + hh]


def _gla_recurrence(q, k, v, g, s0, tt, c, sub, hp):
    n, t, _ = q.shape
    nseq = math.gcd(n, 2) if tt < c else 1
    kern = functools.partial(_gla_rec_kernel, tt=tt, c=c, sub=sub, hp=hp, nseq=nseq, has_s0=s0 is not None)
    kspec = pl.BlockSpec((nseq, tt, hp * DK_C), lambda b, h, i: (b, i, h))
    vspec = pl.BlockSpec((nseq, tt, hp * DV_C), lambda b, h, i: (b, i, h))
    sspec = pl.BlockSpec((nseq, hp, DK_C, DV_C), lambda b, h, i: (b, h, 0, 0))
    in_specs = [kspec, kspec, vspec, kspec]
    args = [q, k, v, g]
    if s0 is not None:
        in_specs.append(sspec)
        args.append(s0)
    return pl.pallas_call(
        kern,
        grid=(n // nseq, H_C // hp, t // tt),
        in_specs=in_specs,
        out_specs=[vspec, sspec],
        out_shape=[jax.ShapeDtypeStruct((n, t, H_C * DV_C), F32),
                   jax.ShapeDtypeStruct((n, H_C, DK_C, DV_C), F32)],
        scratch_shapes=[pltpu.VMEM((nseq * hp, DK_C, DV_C), F32)],
        compiler_params=_cparams(("parallel", "parallel", "arbitrary")),
        name="gla_recurrence",
    )(*args)


def _gla_out_kernel(y_ref, g1_ref, o_ref_in, r_ref, gnw_ref, w_ref,
                    sh_ref, sc_ref, g2_ref, nw_ref, w1_ref, w2_ref, fw_ref, out_ref, *, final):
    parts = []
    for h in range(H_C):
        sl = slice(h * DV_C, (h + 1) * DV_C)
        r = r_ref[:, sl]
        parts.append((_rms(o_ref_in[:, sl], gnw_ref[...]) * (r * jax.nn.sigmoid(r))).astype(BF16))
    out = _dot(jnp.concatenate(parts, axis=-1), w_ref[...])
    y = y_ref[...]
    y1 = y + g1_ref[0] * out.reshape(y.shape)
    out_ref[...] = _mlp_value(y1, sh_ref[0], sc_ref[0], g2_ref[0], nw_ref[...], w1_ref, w2_ref, fw_ref[...], final)


def _gla_out(til, y, mods, o, r, gnw, w_out, mlp):
    mlp_specs, mlp_args = _mlp_specs(til, mlp)
    return pl.pallas_call(
        functools.partial(_gla_out_kernel, final=mlp[4]),
        grid=til.grid,
        in_specs=[til.x_spec(D_MODEL), til.mod_spec(2), til.flat_spec(H_C * DV_C), til.flat_spec(H_C * DV_C),
                  _full_spec(gnw.shape), _const_spec(w_out.shape)] + mlp_specs,
        out_specs=til.x_spec(D_MODEL),
        out_shape=jax.ShapeDtypeStruct(y.shape, F32),
        compiler_params=_cparams(("parallel", "parallel")),
        name="gla_out_mlp",
    )(y, mods, o, r, gnw, w_out, mods, mods, mods, *mlp_args)


def _rope_tables(pos, d):
    inv = ROPE_THETA ** (-jnp.arange(0, d, 2, dtype=F32) / d)
    ang = pos.astype(F32)[:, None] * inv[None, :]
    cos, sin = jnp.cos(ang), jnp.sin(ang)
    rep = LANES // d
    c = jnp.tile(jnp.concatenate([cos, cos], axis=-1), (1, rep))
    s = jnp.tile(jnp.concatenate([-sin, sin], axis=-1), (1, rep))
    return c, s


def _rope_tables_t(pos, d):
    inv = ROPE_THETA ** (-jnp.arange(0, d, 2, dtype=F32) / d)
    ang = inv[:, None] * pos.astype(F32)[None, :]
    return jnp.cos(ang), jnp.sin(ang)


def _mods_for(mod_l, lo, hi):
    nb = hi - lo
    return mod_l[lo:hi].reshape(nb, 6, D_MODEL).transpose(1, 0, 2).reshape(6, nb, 1, D_MODEL)


def _prompt_tile_rows(t):
    return math.gcd(t, 256)


def _even_layer(yp, ys, mods_p, mods_s, caches, page_table, wts, lam_init, til_p, til_s, mlp):
    (c_dk, c_dv, c_cmp, c_sel, win_buf) = caches
    n, t, _ = yp.shape
    ns, s, _ = ys.shape
    n_pages = page_table.shape[1]
    past = n_pages * PAGE_SIZE
    w_in = wts["w_in"]
    n_main = w_in.shape[1] - 3 * H_B
    w_main = w_in[:, :n_main].astype(BF16)
    w_gate = jnp.pad(w_in[:, n_main:], ((0, 0), (0, LANES - 3 * H_B))).astype(BF16)
    nw1 = wts["norm1"].reshape(1, D_MODEL)
    lam_w = wts["lam_w"]
    sub_w = wts["sub_w"].reshape(1, DV_A)
    wc, pe, w2c = _compress_weights(wts["cmp_pe"], wts["cmp_w1"], wts["cmp_w2"])
    w_out = wts["w_out"].astype(BF16)

    tabs_p = _rope_tables(jnp.arange(t), DK_A) + _rope_tables(jnp.arange(t), DH_B)
    kt_tabs = _rope_tables_t(jnp.arange(t), DK_A)
    qa, kat, va, qb, cmp_kv, sel_kv, win_kv, gt = _even_inproj(til_p, yp, mods_p, nw1, w_main, w_gate, tabs_p, kt_tabs)
    r3 = lambda a: a.reshape(n, t, a.shape[-1])
    tq = _prompt_tile_rows(t)
    tq_attn = math.gcd(t, 2 * tq)
    oa = _diff_attn_prompt(r3(qa), kat, r3(va), lam_w, sub_w, lam_init, tq_attn, tq)
    ccmp = _compress_prompt(r3(cmp_kv), wc, pe, w2c)
    n_cmp = (t - CMP_LEN) // CMP_STRIDE + 1
    n_sel = -(-t // SEL_BLOCK)
    o_cmp, selm = _cmp_topk(r3(qb), ccmp, nb=1, rq=tq, n_cmp=n_cmp, n_sel=n_sel, pos_base=0)
    o_sel = _nsa_attn_prompt(r3(qb), r3(sel_kv), selm, tq_attn, tq)
    o_win = _nsa_attn_prompt(r3(qb), r3(win_kv), None, tq_attn, tq)
    f2 = lambda a: a.reshape(n * t, a.shape[-1])
    yp = _even_out(til_p, yp, mods_p, f2(oa), f2(o_cmp), f2(o_sel), f2(o_win), gt, w_out, mlp)
    wl = min(WINDOW, t)
    ka_state = jnp.transpose(kat.reshape(n, H_A, 2, DK_A, t), (0, 4, 1, 2, 3))
    st_p = (ka_state, r3(va).reshape(n, t, H_A, DV_A),
            r3(cmp_kv).reshape(n, t, 2, G_B, DH_B), r3(sel_kv).reshape(n, t, 2, G_B, DH_B),
            r3(win_kv)[:, t - wl:].reshape(n, wl, 2, G_B, DH_B))

    pos_s = past + jnp.arange(s)
    tabs_s = tuple(jnp.tile(x, (til_s.b, 1)) for x in _rope_tables(pos_s, DK_A) + _rope_tables(pos_s, DH_B))
    qa, ka, va, qb, cmp_kv, sel_kv, win_kv, gt = _even_inproj(til_s, ys, mods_s, nw1, w_main, w_gate, tabs_s)
    r3 = lambda a: a.reshape(ns, s, a.shape[-1])
    slot_cache = lambda c: c.reshape(c.shape[0], PAGE_SIZE * SLOTS, LANES)
    c_dkt = jnp.transpose(c_dk, (0, 2, 3, 4, 1)).reshape(c_dk.shape[0], H_A * 2 * DK_A, PAGE_SIZE)
    oa = _diff_attn_sample(r3(qa), r3(ka), r3(va), c_dkt, slot_cache(c_dv), page_table, lam_w, sub_w, lam_init)
    ccmp = _compress_sample(slot_cache(c_cmp), page_table, wc, pe, w2c)
    total = past + s
    n_cmp = (total - CMP_LEN) // CMP_STRIDE + 1
    n_sel = -(-total // SEL_BLOCK)
    assert n_cmp <= ccmp.shape[3] and n_sel <= LANES
    nb = math.gcd(ns, LANES // s)
    o_cmp, selm = _cmp_topk(r3(qb), ccmp, nb=nb, rq=s, n_cmp=n_cmp, n_sel=n_sel, pos_base=past)
    o_sel = _sel_attn_sample(r3(qb), selm, r3(sel_kv), slot_cache(c_sel), page_table)
    wb = win_buf.shape[1]
    o_win, win_out = _win_attn_sample(r3(qb), win_buf.reshape(ns, wb * SLOTS, LANES), r3(win_kv), past)
    f2 = lambda a: a.reshape(ns * s, a.shape[-1])
    ys = _even_out(til_s, ys, mods_s, f2(oa), f2(o_cmp), f2(o_sel), f2(o_win), gt, w_out, mlp)
    st_s = (r3(ka).reshape(ns, s, H_A, 2, DK_A), r3(va).reshape(ns, s, H_A, DV_A),
            r3(cmp_kv).reshape(ns, s, 2, G_B, DH_B), r3(sel_kv).reshape(ns, s, 2, G_B, DH_B),
            win_out.reshape(ns, wb, 2, G_B, DH_B))
    return yp, ys, st_p, st_s


def _odd_layer(yp, ys, mods_p, mods_s, s0, wts, til_p, til_s, mlp):
    w_in = wts["w_in"]
    n_main = w_in.shape[1] - GATE_RANK
    w_main = w_in[:, :n_main].astype(BF16)
    w_gl = jnp.pad(w_in[:, n_main:], ((0, 0), (0, LANES - GATE_RANK))).astype(BF16)
    w_gate = jnp.pad(wts["w_gate"], ((0, LANES - GATE_RANK), (0, 0))).astype(BF16)
    b_gate = wts["b_gate"].reshape(1, -1)
    nw1 = wts["norm1"].reshape(1, D_MODEL)
    gnw = wts["gnorm"].reshape(1, DV_C)
    w_out = wts["w_out"].astype(BF16)
    outs = []
    for y, mods, til, state in ((yp, mods_p, til_p, None), (ys, mods_s, til_s, s0)):
        n, t, _ = y.shape
        q, k, v, r, la = _gla_inproj(til, y, mods, nw1, w_main, w_gl, w_gate, b_gate)
        r3 = lambda a: a.reshape(n, t, a.shape[-1])
        c = math.gcd(t, GLA_CHUNK)
        if c >= GLA_SUB:
            tt, cc, sub = math.gcd(t, 8 * c), c, GLA_SUB
        else:
            tt, cc, sub = t, GLA_SUB, GLA_SUB
        o, s_fin = _gla_recurrence(r3(q), r3(k), r3(v), r3(la), state, tt, cc, sub, H_C)
        y = _gla_out(til, y, mods, o.reshape(n * t, -1), r, gnw, w_out, mlp)
        outs.append((y, s_fin))
    return outs[0][0], outs[1][0], outs[0][1], outs[1][1]


def kernel(x_prompt, x_sample, c_prompt, c_sample, cache_diff_k, cache_diff_v, cache_cmp_kv, cache_sel_kv,
           state_win_kv, state_gla, page_table, norm1_w, norm2_w, ada_w, ada_b, even_w_in, even_w_out,
           diff_lambda_w, diff_subln_w, cmp_pe, cmp_w1, cmp_w2, gla_w_in, gla_w_gate, gla_b_gate, gla_norm_w,
           gla_w_out, mlp_w1, mlp_w2, final_norm_w):
    depth = ada_w.shape[0]
    n, t, _ = x_prompt.shape
    ns, s, _ = x_sample.shape
    til_p = _Tiling(n, t, 1, _prompt_tile_rows(t))
    til_s = _Tiling(ns, s, math.gcd(ns, 256 // s), s)

    pad = (-(n + ns)) % 8
    c_all = jnp.concatenate([c_prompt, c_sample, jnp.zeros((pad, D_MODEL), F32)], axis=0)
    mod = _adaln(c_all, ada_w, ada_b)

    yp, ys = x_prompt, x_sample
    st_p = [[] for _ in range(6)]
    st_s = [[] for _ in range(6)]
    fw = final_norm_w.reshape(1, D_MODEL)
    for l in range(depth):
        mods_p = _mods_for(mod[l], 0, n)
        mods_s = _mods_for(mod[l], n, n + ns)
        mlp = (norm2_w[l].reshape(1, D_MODEL), mlp_w1[l].astype(BF16), mlp_w2[l].astype(BF16), fw, l == depth - 1)
        if l % 2 == 0:
            e = l // 2
            lam_init = 0.8 - 0.6 * math.exp(-0.3 * l)
            wts = dict(w_in=even_w_in[e], w_out=even_w_out[e], lam_w=diff_lambda_w[e], sub_w=diff_subln_w[e],
                       cmp_pe=cmp_pe[e], cmp_w1=cmp_w1[e], cmp_w2=cmp_w2[e], norm1=norm1_w[l])
            caches = (cache_diff_k[e], cache_diff_v[e], cache_cmp_kv[e], cache_sel_kv[e], state_win_kv[e])
            yp, ys, sp, ss = _even_layer(yp, ys, mods_p, mods_s, caches, page_table, wts, lam_init, til_p, til_s, mlp)
            for i in range(5):
                st_p[i].append(sp[i])
                st_s[i].append(ss[i])
        else:
            o = l // 2
            wts = dict(w_in=gla_w_in[o], w_gate=gla_w_gate[o], b_gate=gla_b_gate[o], gnorm=gla_norm_w[o],
                       w_out=gla_w_out[o], norm1=norm1_w[l])
            yp, ys, gp, gs = _odd_layer(yp, ys, mods_p, mods_s, state_gla[o], wts, til_p, til_s, mlp)
            st_p[5].append(gp)
            st_s[5].append(gs)
    outs_p = [jnp.stack(x, axis=0) for x in st_p]
    outs_s = [jnp.stack(x, axis=0) for x in st_s]
    return (yp, ys, *outs_p, *outs_s)
```

```python
import functools
import math

import jax
import jax.numpy as jnp
from jax import lax
from jax.experimental import pallas as pl
from jax.experimental.pallas import tpu as pltpu

F32 = jnp.float32
BF16 = jnp.bfloat16

D_MODEL = 1024
PAGE_SIZE = 128
H_A = 4
DK_A = 64
DV_A = 128
H_B = 4
G_B = 2
R_B = 2
DH_B = 128
CMP_LEN = 32
CMP_STRIDE = 16
SEL_BLOCK = 64
TOP_N = 16
WINDOW = 512
H_C = 4
DK_C = 128
DV_C = 256
GATE_RANK = 16
GATE_TAU = 16.0
GLA_CHUNK = 64
GLA_SUB = 16
GLA_SAFE_DECAY = 80.0
D_FF = 4 * D_MODEL
ROPE_THETA = 10000.0
EPS = 1e-6

LANES = 128
GROUP_W = 512
assert GROUP_W == H_A * 2 * DK_A == H_A * DV_A == H_B * DH_B == 2 * G_B * DH_B
SLOTS = GROUP_W // LANES
MASK_NEG = -1e30
LOG2E = 1.4426950408889634
SAFE_SHIFT = 56.0
SEL_NEG = 32768.0
VMEM_LIMIT_MB = 56


def _cparams(sem, vmem_mb=VMEM_LIMIT_MB):
    return pltpu.CompilerParams(dimension_semantics=sem, vmem_limit_bytes=vmem_mb * 1024 * 1024)


def _dot(a, b):
    return jnp.dot(a, b, preferred_element_type=F32)


def _dot_nt(a, b):
    return lax.dot_general(a, b, (((1,), (1,)), ((), ())), preferred_element_type=F32)


def _dot_tn(a, b):
    return lax.dot_general(a, b, (((0,), (0,)), ((), ())), preferred_element_type=F32)


def _iota(shape, dim):
    return lax.broadcasted_iota(jnp.int32, shape, dim)


def _rms(x, w):
    ms = jnp.mean(x * x, axis=-1, keepdims=True)
    return x * lax.rsqrt(ms + EPS) * w


def _norm_mod(x, nw, shift, scale):
    return _rms(x, nw) * (1.0 + scale) + shift


class _Tiling:
    def __init__(self, nb, r, b, rt):
        assert nb % b == 0 and r % rt == 0 and (b == 1 or rt == r)
        self.nb, self.r, self.b, self.rt = nb, r, b, rt
        self.grid = (nb // b, r // rt)
        self.rows = b * rt
        self.nrb = r // rt

    def x_spec(self, d):
        return pl.BlockSpec((self.b, self.rt, d), lambda i, j: (i, j, 0))

    def mod_spec(self, k):
        return pl.BlockSpec((1, self.b, 1, D_MODEL), lambda i, j: (k, i, 0, 0))

    def flat_spec(self, c):
        nrb = self.nrb
        return pl.BlockSpec((self.rows, c), lambda i, j: (i * nrb + j, 0))

    def tab_spec(self):
        return pl.BlockSpec((self.rows, LANES), lambda i, j: (j, 0))


def _full_spec(shape):
    nd = len(shape)
    return pl.BlockSpec(shape, lambda *_: (0,) * nd)


def _const_spec(shape):
    nd = len(shape)
    return pl.BlockSpec(shape, lambda *_: (0,) * nd, pipeline_mode=pl.Buffered(1))


def _adaln_kernel(c_ref, w_ref, b_ref, o_ref):
    c = c_ref[...]
    a = (c * jax.nn.sigmoid(c)).astype(BF16)
    o_ref[0] = _dot(a, w_ref[0].astype(BF16)) + b_ref[0]


def _adaln(c_all, ada_w, ada_b):
    depth, d, n6 = ada_w.shape
    rows = c_all.shape[0]
    tn = math.gcd(n6, 12 * LANES)
    return pl.pallas_call(
        _adaln_kernel,
        grid=(depth, n6 // tn),
        in_specs=[pl.BlockSpec((rows, d), lambda l, j: (0, 0)),
                  pl.BlockSpec((1, d, tn), lambda l, j: (l, 0, j)),
                  pl.BlockSpec((1, 1, tn), lambda l, j: (l, 0, j))],
        out_specs=pl.BlockSpec((1, rows, tn), lambda l, j: (l, 0, j)),
        out_shape=jax.ShapeDtypeStruct((depth, rows, n6), F32),
        compiler_params=_cparams(("parallel", "parallel")),
        name="adaln",
    )(c_all, ada_w, ada_b.reshape(depth, 1, n6))


def _swap_half(x, half):
    if 2 * half == LANES:
        return pltpu.roll(x, half, 1)
    lane = _iota((1, LANES), 1)
    lo = (lane % (2 * half)) < half
    return jnp.where(lo, pltpu.roll(x, LANES - half, 1), pltpu.roll(x, half, 1))


def _even_inproj_kernel(*refs, k_feature_major):
    if k_feature_major:
        (x_ref, sh_ref, sc_ref, nw_ref, w_ref, wg_ref, c64_ref, s64_ref, c128_ref, s128_ref, wkt_ref, ct_ref, st_ref,
         qa_ref, ka_ref, va_ref, qb_ref, cmp_ref, sel_ref, win_ref, gt_ref) = refs
    else:
        (x_ref, sh_ref, sc_ref, nw_ref, w_ref, wg_ref, c64_ref, s64_ref, c128_ref, s128_ref,
         qa_ref, ka_ref, va_ref, qb_ref, cmp_ref, sel_ref, win_ref, gt_ref) = refs
    h = _norm_mod(x_ref[...], nw_ref[...], sh_ref[0], sc_ref[0])
    h = h.reshape(-1, D_MODEL).astype(BF16)
    c64, s64, c128, s128 = c64_ref[...], s64_ref[...], c128_ref[...], s128_ref[...]

    def rope64(p):
        return p * c64 + _swap_half(p, DK_A // 2) * s64

    def rope128(p):
        return p * c128 + _swap_half(p, DH_B // 2) * s128

    def project(ref, off, ropes):
        p = _dot(h, w_ref[:, off:off + GROUP_W])
        for j, rope in enumerate(ropes):
            sl = slice(j * LANES, (j + 1) * LANES)
            ref[:, sl] = p[:, sl] if rope is None else rope(p[:, sl])

    project(qa_ref, 0, [rope64] * 4)
    if k_feature_major:
        kt = _dot_nt(wkt_ref[...], h)
        ct, st = ct_ref[...], st_ref[...]
        half = DK_A // 2
        for grp in range(H_A * 2):
            x1 = kt[grp * DK_A:grp * DK_A + half]
            x2 = kt[grp * DK_A + half:(grp + 1) * DK_A]
            ka_ref[0, grp * DK_A:grp * DK_A + half, :] = x1 * ct - x2 * st
            ka_ref[0, grp * DK_A + half:(grp + 1) * DK_A, :] = x2 * ct + x1 * st
    else:
        project(ka_ref, GROUP_W, [rope64] * 4)
    project(va_ref, 2 * GROUP_W, [None] * 4)
    project(qb_ref, 3 * GROUP_W, [rope128] * 4)
    for t, ref in enumerate((cmp_ref, sel_ref, win_ref)):
        project(ref, (4 + t) * GROUP_W, [rope128, rope128, None, None])
    gt_ref[...] = jax.nn.sigmoid(_dot(h, wg_ref[...]))


def _even_inproj(til, x, mods, nw, w_main, w_gate, tabs, kt_tabs=None):
    rows_total = til.nb * til.r
    widths = (GROUP_W,) * 7 + (LANES,)
    in_specs = [til.x_spec(D_MODEL), til.mod_spec(0), til.mod_spec(1), _full_spec((1, D_MODEL)),
                _const_spec(w_main.shape), _const_spec(w_gate.shape)] + [til.tab_spec()] * 4
    args = [x, mods, mods, nw, w_main, w_gate, *tabs]
    out_specs = [til.flat_spec(c) for c in widths]
    out_shape = [jax.ShapeDtypeStruct((rows_total, c), F32) for c in widths]
    if kt_tabs is not None:
        assert til.b == 1
        w_kt = jnp.transpose(w_main[:, GROUP_W:2 * GROUP_W])
        in_specs += [_const_spec(w_kt.shape)] + [pl.BlockSpec((DK_A // 2, til.rt), lambda i, j: (0, j))] * 2
        args += [w_kt, *kt_tabs]
        out_specs[1] = pl.BlockSpec((1, GROUP_W, til.rt), lambda i, j: (i, 0, j))
        out_shape[1] = jax.ShapeDtypeStruct((til.nb, GROUP_W, til.r), F32)
    return pl.pallas_call(
        functools.partial(_even_inproj_kernel, k_feature_major=kt_tabs is not None),
        grid=til.grid,
        in_specs=in_specs,
        out_specs=out_specs,
        out_shape=out_shape,
        compiler_params=_cparams(("parallel", "parallel")),
        name="even_inproj",
    )(*args)


def _for_tiles(lo, hi, fn):
    n = hi - lo

    def pair(j, carry):
        fn(lo + 2 * j)
        fn(lo + 2 * j + 1)
        return carry

    lax.fori_loop(0, n // 2, pair, 0)

    @pl.when(n % 2 == 1)
    def _():
        fn(hi - 1)


def _row_sumsq(x):
    xf = x.astype(F32)
    return _dot((xf * xf).astype(BF16), jnp.ones((LANES, LANES), BF16))


def _key_norm_bound(k_ref, n_keys, tile):
    def body(kb, m):
        k = k_ref[0, pl.ds(pl.multiple_of(kb * tile, tile), tile), :].astype(BF16)
        return jnp.maximum(m, _row_sumsq(k))

    m = lax.fori_loop(0, n_keys // tile, body, jnp.zeros((tile, LANES), F32))
    return jnp.max(m, axis=0, keepdims=True)


def _score_bound(q2, kmax2):
    qmax2 = jnp.max(_row_sumsq(q2), axis=0, keepdims=True)
    return jnp.sqrt(qmax2 * kmax2) * 1.05


def _two_pass_attention(mx_ref, acc_ref, lo, hi, last, scores, values, mask_body=False, bound=None):
    def lane_max(s):
        m = s[:, 0:LANES]
        for c in range(1, s.shape[1] // LANES):
            m = jnp.maximum(m, s[:, c * LANES:(c + 1) * LANES])
        return m

    def pass1(kb, masked):
        mx_ref[...] = jnp.maximum(mx_ref[...], lane_max(scores(kb, masked)))

    def exact_max():
        mx_ref[...] = jnp.full(mx_ref.shape, MASK_NEG, F32)
        _for_tiles(lo, hi, lambda kb: pass1(kb, mask_body))
        for kb in last:
            pass1(kb, True)
        mx_ref[...] = jnp.broadcast_to(jnp.max(mx_ref[...], axis=-1, keepdims=True), mx_ref.shape)

    if bound is None:
        exact_max()
    else:
        safe = jnp.max(bound) <= SAFE_SHIFT

        @pl.when(safe)
        def _():
            mx_ref[...] = jnp.broadcast_to(bound, mx_ref.shape)

        @pl.when(jnp.logical_not(safe))
        def _():
            exact_max()

    acc_ref[...] = jnp.zeros(acc_ref.shape, F32)

    def pass2(kb, masked):
        s = scores(kb, masked)
        m = mx_ref[...]
        p = jnp.concatenate([jnp.exp2(s[:, c * LANES:(c + 1) * LANES] - m) for c in range(s.shape[1] // LANES)],
                            axis=-1).astype(BF16)
        v = values(kb)
        v1 = jnp.concatenate([v, jnp.ones(v.shape, BF16)], axis=-1)
        acc_ref[...] += _dot(p, v1)

    _for_tiles(lo, hi, lambda kb: pass2(kb, mask_body))
    for kb in last:
        pass2(kb, True)
    return acc_ref[:, 0:LANES] / acc_ref[:, LANES:2 * LANES]


def _diff_lambda(lw, lam_init):
    a = jnp.sum(lw[0:1] * lw[1:2], axis=-1, keepdims=True)
    b = jnp.sum(lw[2:3] * lw[3:4], axis=-1, keepdims=True)
    return jnp.exp(a) - jnp.exp(b) + lam_init


def _diff_finalize(o2, tq, lam, sub_w, lam_init):
    od = o2[:tq] - lam * o2[tq:]
    return _rms(od, sub_w) * (1.0 - lam_init)


def _diff_flash_kernel(q_ref, k_ref, v_ref, lamw_ref, sub_ref, o_ref, q2_ref, mx_ref, acc_ref, kmax_ref, *,
                       tq, tk, lam_init):
    i = pl.program_id(2)

    def key_tile(kb):
        return k_ref[0, :, pl.ds(pl.multiple_of(kb * tk, tk), tk)].astype(BF16)

    @pl.when(i == 0)
    def _():
        def body(kb, m):
            kf = key_tile(kb).astype(F32)
            return jnp.maximum(m, jnp.sum(kf * kf, axis=0, keepdims=True))

        m = lax.fori_loop(0, k_ref.shape[2] // tk, body, jnp.zeros((1, tk), F32))
        kmax_ref[...] = jnp.broadcast_to(jnp.max(m, axis=-1, keepdims=True), kmax_ref.shape)

    q = q_ref[0] * (DK_A ** -0.5 * LOG2E)
    lane = _iota((1, LANES), 1)
    q2_ref[0:tq] = jnp.where(lane < DK_A, q, 0.0).astype(BF16)
    q2_ref[tq:2 * tq] = jnp.where(lane >= DK_A, q, 0.0).astype(BF16)
    bound = _score_bound(q2_ref[...], kmax_ref[0:1])

    def scores(kb, masked):
        s = _dot(q2_ref[...], key_tile(kb))
        if masked:
            r = _iota((2 * tq, tk), 0)
            qp = i * tq + jnp.where(r >= tq, r - tq, r)
            s = jnp.where(kb * tk + _iota((2 * tq, tk), 1) <= qp, s, MASK_NEG)
        return s

    def values(kb):
        return v_ref[0, pl.ds(pl.multiple_of(kb * tk, tk), tk), :].astype(BF16)

    per = tq // tk
    o2 = _two_pass_attention(mx_ref, acc_ref, 0, i * per, [i * per + u for u in range(per)], scores, values,
                             bound=bound)
    o_ref[0] = _diff_finalize(o2, tq, _diff_lambda(lamw_ref[...], lam_init), sub_ref[...], lam_init)


def _diff_attn_prompt(qa, kat, va, lam_w, sub_w, lam_init, tq, tk):
    n, t, _ = qa.shape
    kern = functools.partial(_diff_flash_kernel, tq=tq, tk=tk, lam_init=lam_init)
    return pl.pallas_call(
        kern,
        grid=(n, H_A, t // tq),
        in_specs=[pl.BlockSpec((1, tq, LANES), lambda b, h, i: (b, i, h)),
                  pl.BlockSpec((1, LANES, t), lambda b, h, i: (b, h, 0)),
                  pl.BlockSpec((1, t, LANES), lambda b, h, i: (b, 0, h)),
                  _full_spec(lam_w.shape), _full_spec(sub_w.shape)],
        out_specs=pl.BlockSpec((1, tq, LANES), lambda b, h, i: (b, i, h)),
        out_shape=jax.ShapeDtypeStruct((n, t, H_A * DV_A), F32),
        scratch_shapes=[pltpu.VMEM((2 * tq, LANES), BF16), pltpu.VMEM((2 * tq, LANES), F32),
                        pltpu.VMEM((2 * tq, 2 * LANES), F32), pltpu.VMEM((8, LANES), F32)],
        compiler_params=_cparams(("parallel", "parallel", "arbitrary")),
        name="diff_attn_prompt",
    )(qa, kat, va, lam_w, sub_w)


def _sel_onehot(rows, first_block):
    blk = _iota((rows, LANES), 0) // SEL_BLOCK + first_block
    return jnp.where(blk == _iota((rows, LANES), 1), 1.0, 0.0).astype(BF16)


def _nsa_flash_kernel(*refs, tq, tk, use_sel):
    if use_sel:
        q_ref, selm_ref, k_ref, v_ref, o_ref, q2_ref, mx_ref, acc_ref, kmax_ref = refs
    else:
        q_ref, k_ref, v_ref, o_ref, q2_ref, mx_ref, acc_ref, kmax_ref = refs
    i = pl.program_id(2)

    @pl.when(i == 0)
    def _():
        kmax_ref[...] = jnp.broadcast_to(_key_norm_bound(k_ref, k_ref.shape[1], tk), kmax_ref.shape)

    q = q_ref[0] * (DH_B ** -0.5 * LOG2E)
    for r in range(R_B):
        q2_ref[r * tq:(r + 1) * tq, 0:LANES] = q[:, r * LANES:(r + 1) * LANES].astype(BF16)
        if use_sel:
            q2_ref[r * tq:(r + 1) * tq, LANES:2 * LANES] = selm_ref[0, 0]
    bound = _score_bound(q2_ref[:, 0:LANES], kmax_ref[0:1])

    def scores(kb, masked):
        k = k_ref[0, pl.ds(pl.multiple_of(kb * tk, tk), tk), :].astype(BF16)
        if use_sel:
            k = jnp.concatenate([k, _sel_onehot(tk, kb * (tk // SEL_BLOCK))], axis=-1)
        s = _dot_nt(q2_ref[...], k)
        if masked:
            r = _iota((R_B * tq, tk), 0)
            qp = i * tq + jnp.where(r >= tq, r - tq, r)
            kp = kb * tk + _iota((R_B * tq, tk), 1)
            ok = kp <= qp
            if not use_sel:
                ok = ok & (kp > qp - WINDOW)
            s = jnp.where(ok, s, MASK_NEG)
        return s

    def values(kb):
        return v_ref[0, pl.ds(pl.multiple_of(kb * tk, tk), tk), :].astype(BF16)

    per = tq // tk
    lo = 0 if use_sel else jnp.maximum(i * per - WINDOW // tk, 0)
    o = _two_pass_attention(mx_ref, acc_ref, lo, i * per, [i * per + u for u in range(per)], scores, values,
                            mask_body=not use_sel, bound=bound)
    for r in range(R_B):
        o_ref[0, :, r * LANES:(r + 1) * LANES] = o[r * tq:(r + 1) * tq]


def _nsa_attn_prompt(qb, kv, selm, tq, tk):
    n, t, _ = qb.shape
    use_sel = selm is not None
    kd = 2 * LANES if use_sel else LANES
    kern = functools.partial(_nsa_flash_kernel, tq=tq, tk=tk, use_sel=use_sel)
    in_specs = [pl.BlockSpec((1, tq, R_B * LANES), lambda b, g, i: (b, i, g))]
    args = [qb]
    if use_sel:
        in_specs.append(pl.BlockSpec((1, 1, tq, LANES), lambda b, g, i: (b, g, i, 0)))
        args.append(selm)
    in_specs += [pl.BlockSpec((1, t, LANES), lambda b, g, i: (b, 0, g)),
                 pl.BlockSpec((1, t, LANES), lambda b, g, i: (b, 0, G_B + g))]
    args += [kv, kv]
    return pl.pallas_call(
        kern,
        grid=(n, G_B, t // tq),
        in_specs=in_specs,
        out_specs=pl.BlockSpec((1, tq, R_B * LANES), lambda b, g, i: (b, i, g)),
        out_shape=jax.ShapeDtypeStruct((n, t, H_B * DH_B), F32),
        scratch_shapes=[pltpu.VMEM((R_B * tq, kd), BF16), pltpu.VMEM((R_B * tq, LANES), F32),
                        pltpu.VMEM((R_B * tq, 2 * LANES), F32), pltpu.VMEM((8, LANES), F32)],
        compiler_params=_cparams(("parallel", "parallel", "arbitrary")),
        name="nsa_sel_prompt" if use_sel else "nsa_win_prompt",
    )(*args)


def _compress_core(load, wc_ref, pe_ref, w2_ref, wi, nchunk):
    rows = G_B * nchunk
    acc = jnp.zeros((rows, 2 * LANES), F32)
    pew = jnp.zeros((16, 2 * LANES), F32)
    for u in range(CMP_STRIDE // 2):
        lhs = jnp.concatenate([load(2 * u), load(2 * u + 1)], axis=-1).astype(BF16)
        w = wc_ref[wi, u]
        acc = acc + _dot(lhs, w)
        pew = pew + _dot(pe_ref[wi, u], w)
    first = acc[:, :LANES]
    second = pltpu.roll(acc[:, LANES:], rows - 1, 0)
    hid = first + second + pew[0:1, :LANES] + pew[8:9, LANES:]
    hid = hid * jax.nn.sigmoid(hid)
    return _dot(hid.astype(BF16), w2_ref[wi])


def _compress_prompt_kernel(*refs, nchunk):
    x_refs = refs[:G_B]
    wc_ref, pe_ref, w2_ref, o_ref = refs[G_B:]

    def load(tok):
        return jnp.concatenate([xr[0, pl.ds(tok, nchunk, stride=CMP_STRIDE), :] for xr in x_refs], axis=0)

    out = _compress_core(load, wc_ref, pe_ref, w2_ref, 0, nchunk)
    for g in range(G_B):
        o_ref[0, 0, g] = out[g * nchunk:(g + 1) * nchunk]


def _compress_sample_kernel(pt_ref, *refs, n_pages, spb):
    pg_refs = refs[:spb * n_pages]
    wc_ref, pe_ref, w2_ref, o_ref = refs[spb * n_pages:]
    cps = PAGE_SIZE // CMP_STRIDE
    nchunk = n_pages * cps
    for kv in range(2):
        by_tok = [jnp.swapaxes(pg[0, pl.ds(kv * G_B + g, PAGE_SIZE, stride=SLOTS), :]
                               .reshape(cps, CMP_STRIDE, LANES), 0, 1)
                  for smp in range(spb) for g in range(G_B) for pg in pg_refs[smp * n_pages:(smp + 1) * n_pages]]

        def load(tok):
            return jnp.concatenate([x[tok] for x in by_tok], axis=0)

        out = _compress_core(load, wc_ref, pe_ref, w2_ref, kv, spb * nchunk)
        for smp in range(spb):
            for g in range(G_B):
                seg = smp * G_B + g
                o_ref[smp, kv, g] = out[seg * nchunk:(seg + 1) * nchunk]


def _compress_weights(cmp_pe, cmp_w1, cmp_w2):
    w1 = cmp_w1.reshape(2, CMP_LEN, DH_B, DH_B)
    wab = jnp.concatenate([w1[:, :CMP_STRIDE], w1[:, CMP_STRIDE:]], axis=-1)
    wc = wab.reshape(2, CMP_STRIDE // 2, 2 * DH_B, 2 * DH_B).astype(BF16)
    pa = cmp_pe[:, :CMP_STRIDE].reshape(2, CMP_STRIDE // 2, 1, 2 * DH_B)
    pb = cmp_pe[:, CMP_STRIDE:].reshape(2, CMP_STRIDE // 2, 1, 2 * DH_B)
    z = jnp.zeros((2, CMP_STRIDE // 2, 7, 2 * DH_B), F32)
    pe = jnp.concatenate([pa, z, pb, z], axis=2).astype(BF16)
    return wc, pe, cmp_w2.astype(BF16)


def _compress_prompt(cmp_kv, wc, pe, w2):
    n, t, _ = cmp_kv.shape
    nchunk = t // CMP_STRIDE
    kern = functools.partial(_compress_prompt_kernel, nchunk=nchunk)
    return pl.pallas_call(
        kern,
        grid=(n, 2),
        in_specs=[pl.BlockSpec((1, t, DH_B), functools.partial(lambda b, kv, g: (b, 0, kv * G_B + g), g=g))
                  for g in range(G_B)] + [
                  pl.BlockSpec((1,) + wc.shape[1:], lambda b, kv: (kv, 0, 0, 0)),
                  pl.BlockSpec((1,) + pe.shape[1:], lambda b, kv: (kv, 0, 0, 0)),
                  pl.BlockSpec((1, DH_B, DH_B), lambda b, kv: (kv, 0, 0))],
        out_specs=pl.BlockSpec((1, 1, G_B, nchunk, DH_B), lambda b, kv: (b, kv, 0, 0, 0)),
        out_shape=jax.ShapeDtypeStruct((n, 2, G_B, nchunk, DH_B), F32),
        compiler_params=_cparams(("parallel", "parallel")),
        name="compress_prompt",
    )(*([cmp_kv] * G_B), wc, pe, w2)


def _compress_sample(cache, page_table, wc, pe, w2):
    n, n_pages = page_table.shape
    nchunk = n_pages * (PAGE_SIZE // CMP_STRIDE)
    spb = math.gcd(n, 2)
    kern = functools.partial(_compress_sample_kernel, n_pages=n_pages, spb=spb)
    page_specs = _page_specs(n_pages, PAGE_SIZE * SLOTS, LANES, spb)
    grid_spec = pltpu.PrefetchScalarGridSpec(
        num_scalar_prefetch=1,
        grid=(n // spb,),
        in_specs=page_specs + [
            pl.BlockSpec(wc.shape, lambda b, pt: (0, 0, 0, 0)),
            pl.BlockSpec(pe.shape, lambda b, pt: (0, 0, 0, 0)),
            pl.BlockSpec(w2.shape, lambda b, pt: (0, 0, 0))],
        out_specs=pl.BlockSpec((spb, 2, G_B, nchunk, DH_B), lambda b, pt: (b, 0, 0, 0, 0)),
    )
    return pl.pallas_call(
        kern,
        grid_spec=grid_spec,
        out_shape=jax.ShapeDtypeStruct((n, 2, G_B, nchunk, DH_B), F32),
        compiler_params=_cparams(("parallel",)),
        name="compress_sample",
    )(page_table, *([cache] * (spb * n_pages)), wc, pe, w2)


def _cmp_topk_kernel(q_ref, kc_ref, vc_ref, ocmp_ref, selm_ref, p_scr, v_scr, *,
                     nb, rq, nck, n_cmp, n_sel, nsp, pos_base):
    tq = nb * rq
    j = pl.program_id(2)
    scale = DH_B ** -0.5
    kidx = _iota((1, nck), 1)
    qpos_c = pos_base + j * rq + _iota((rq, 1), 0)
    valid = (kidx < n_cmp) & (CMP_STRIDE * kidx + (CMP_LEN - 1) <= qpos_c)
    for b in range(nb):
        kcc = kc_ref[b, 0, 0].astype(BF16)
        vcc = vc_ref[b, 0, 0].astype(BF16)
        q = q_ref[b] * scale
        q2 = jnp.concatenate([q[:, r * LANES:(r + 1) * LANES] for r in range(R_B)], axis=0).astype(BF16)
        valid2 = jnp.concatenate([valid] * R_B, axis=0)
        s = jnp.where(valid2, _dot_nt(q2, kcc), MASK_NEG)
        m = jnp.max(s, axis=-1, keepdims=True)
        p = jnp.where(valid2, jnp.exp(s - m), 0.0)
        p = p / jnp.maximum(jnp.sum(p, axis=-1, keepdims=True), 1e-30)
        o = _dot(p.astype(BF16), vcc)
        psum = jnp.zeros((rq, nck), F32)
        for r in range(R_B):
            ocmp_ref[b, :, r * LANES:(r + 1) * LANES] = o[r * rq:(r + 1) * rq]
            psum = psum + p[r * rq:(r + 1) * rq]
        p_scr[b * rq:(b + 1) * rq, :] = psum
    psum = p_scr[...]
    p_hi = psum.astype(BF16)
    p_lo = (psum - p_hi.astype(F32)).astype(BF16)
    srow = _iota((LANES, nck), 0)
    kcol = _iota((LANES, nck), 1)
    cov = (CMP_STRIDE * kcol < SEL_BLOCK * srow + SEL_BLOCK) & (CMP_STRIDE * kcol + CMP_LEN > SEL_BLOCK * srow)
    cov = cov & (srow < n_sel) & (kcol < n_cmp)
    cov_t = jnp.where(cov, 1.0, 0.0).astype(BF16)
    imp_t = _dot_nt(cov_t, p_hi) + _dot_nt(cov_t, p_lo)
    qpos_l = pos_base + j * rq + _iota((1, tq), 1) % rq
    sb = _iota((nsp, 1), 0)
    valid_b = (sb < n_sel) & (sb * SEL_BLOCK <= qpos_l)
    forced = (sb == 0) | (sb == qpos_l // SEL_BLOCK)
    vals = jnp.where(forced, jnp.inf, jnp.where(valid_b, imp_t[:nsp], -jnp.inf))
    v_scr[...] = vals

    def body(jj, cnt):
        vj = v_scr[pl.ds(jj, 1), :]
        tie = jnp.where(sb > jj, 1.0, 0.0)
        return cnt + jnp.where(vj > vals, 1.0, jnp.where(vj == vals, tie, 0.0))

    last_pos = pos_base + (j + 1) * rq - 1
    n_rank = jnp.minimum(last_pos // SEL_BLOCK + 1, nsp)
    cnt = lax.fori_loop(0, n_rank, body, jnp.zeros((nsp, tq), F32))
    keep = valid_b & (cnt < float(min(TOP_N, n_sel)))
    selm_t = jnp.where(keep, 0.0, -SEL_NEG)
    if nsp < LANES:
        selm_t = jnp.concatenate([selm_t, jnp.zeros((LANES - nsp, tq), F32)], axis=0)
    selm = selm_t.T.astype(BF16)
    for b in range(nb):
        selm_ref[b, 0] = selm[b * rq:(b + 1) * rq]


def _cmp_topk(qb, ccmp, *, nb, rq, n_cmp, n_sel, pos_base):
    n, r, _ = qb.shape
    nck = ccmp.shape[3]
    nsp = -(-n_sel // 8) * 8
    tq = nb * rq
    kern = functools.partial(_cmp_topk_kernel, nb=nb, rq=rq, nck=nck, n_cmp=n_cmp, n_sel=n_sel, nsp=nsp,
                             pos_base=pos_base)
    return pl.pallas_call(
        kern,
        grid=(n // nb, G_B, r // rq),
        in_specs=[pl.BlockSpec((nb, rq, R_B * LANES), lambda a, g, j: (a, j, g)),
                  pl.BlockSpec((nb, 1, 1, nck, DH_B), lambda a, g, j: (a, 0, g, 0, 0)),
                  pl.BlockSpec((nb, 1, 1, nck, DH_B), lambda a, g, j: (a, 1, g, 0, 0))],
        out_specs=[pl.BlockSpec((nb, rq, R_B * LANES), lambda a, g, j: (a, j, g)),
                   pl.BlockSpec((nb, 1, rq, LANES), lambda a, g, j: (a, g, j, 0))],
        out_shape=[jax.ShapeDtypeStruct((n, r, H_B * DH_B), F32),
                   jax.ShapeDtypeStruct((n, G_B, r, LANES), BF16)],
        scratch_shapes=[pltpu.VMEM((tq, nck), F32), pltpu.VMEM((nsp, tq), F32)],
        compiler_params=_cparams(("parallel", "parallel", "parallel")),
        name="cmp_topk",
    )(qb, ccmp, ccmp)


def _attend_scores(pieces):
    m = None
    for s, _ in pieces:
        ms = jnp.max(s, axis=-1, keepdims=True)
        m = ms if m is None else jnp.maximum(m, ms)
    l = jnp.zeros_like(m)
    acc = jnp.zeros((m.shape[0], LANES), F32)
    for s, v in pieces:
        p = jnp.exp(s - m)
        l = l + jnp.sum(p, axis=-1, keepdims=True)
        acc = acc + _dot(p.astype(BF16), v)
    return acc / l


def _attend_pieces(q2, pieces):
    scored = []
    for k, v, mask in pieces:
        s = _dot_nt(q2, k)
        scored.append((s if mask is None else jnp.where(mask, s, MASK_NEG), v))
    return _attend_scores(scored)


def _pad_rows(x, rows):
    return jnp.concatenate([x, jnp.zeros((rows - x.shape[0], x.shape[1]), x.dtype)], axis=0)


def _new_token_mask(m_rows, s):
    r = _iota((m_rows, LANES), 0) % s
    return _iota((m_rows, LANES), 1) <= r


def _diff_sample_kernel(pt_ref, *refs, n_pages, s, spb, lam_init):
    kp_refs = refs[:spb * n_pages]
    vp_refs = refs[spb * n_pages:2 * spb * n_pages]
    q_ref, kn_ref, vn_ref, lamw_ref, sub_ref, o_ref = refs[2 * spb * n_pages:]
    lane = _iota((1, LANES), 1)
    lam = _diff_lambda(lamw_ref[...], lam_init)
    new_mask = _new_token_mask(2 * s, s)
    for b in range(spb):
        kpages = kp_refs[b * n_pages:(b + 1) * n_pages]
        vpages = vp_refs[b * n_pages:(b + 1) * n_pages]
        for h in range(H_A):
            cols = slice(h * LANES, (h + 1) * LANES)
            q = q_ref[b, :, cols] * (DK_A ** -0.5)
            q2 = jnp.concatenate([jnp.where(lane < DK_A, q, 0.0), jnp.where(lane >= DK_A, q, 0.0)],
                                 axis=0).astype(BF16)
            kt_past = jnp.concatenate([r[0, cols, :].astype(BF16) for r in kpages], axis=1)
            s_past = _dot(q2, kt_past)
            vpast = jnp.concatenate([_slot_rows(r, h, PAGE_SIZE).astype(BF16) for r in vpages], axis=0)
            knew = _pad_rows(kn_ref[b, :, cols], LANES).astype(BF16)
            vnew = _pad_rows(vn_ref[b, :, cols], LANES).astype(BF16)
            s_new = jnp.where(new_mask, _dot_nt(q2, knew), MASK_NEG)
            o2 = _attend_scores([(s_past, vpast), (s_new, vnew)])
            o_ref[b, :, cols] = _diff_finalize(o2, s, lam, sub_ref[...], lam_init)


def _page_specs(n_pages, rows, width, spb=1):
    return [pl.BlockSpec((1, rows, width),
                         functools.partial(lambda b, pt, smp, j: (pt[b * spb + smp, j], 0, 0), smp=smp, j=j))
            for smp in range(spb) for j in range(n_pages)]


def _slot_rows(ref, slot, n_tok):
    return ref[0, pl.ds(slot, n_tok, stride=SLOTS), :]


def _diff_attn_sample(qa, ka, va, cache_k, cache_v, page_table, lam_w, sub_w, lam_init):
    n, s, _ = qa.shape
    n_pages = page_table.shape[1]
    spb = math.gcd(n, 2)
    kern = functools.partial(_diff_sample_kernel, n_pages=n_pages, s=s, spb=spb, lam_init=lam_init)
    row_spec = pl.BlockSpec((spb, s, GROUP_W), lambda b, pt: (b, 0, 0))
    grid_spec = pltpu.PrefetchScalarGridSpec(
        num_scalar_prefetch=1,
        grid=(n // spb,),
        in_specs=_page_specs(n_pages, 512, PAGE_SIZE, spb) + _page_specs(n_pages, PAGE_SIZE * SLOTS, LANES, spb)
        + [row_spec, row_spec, row_spec,
           pl.BlockSpec(lam_w.shape, lambda b, pt: (0, 0)), pl.BlockSpec(sub_w.shape, lambda b, pt: (0, 0))],
        out_specs=row_spec,
    )
    return pl.pallas_call(
        kern,
        grid_spec=grid_spec,
        out_shape=jax.ShapeDtypeStruct((n, s, GROUP_W), F32),
        compiler_params=_cparams(("parallel",)),
        name="diff_attn_sample",
    )(page_table, *([cache_k] * (spb * n_pages)), *([cache_v] * (spb * n_pages)), qa, ka, va, lam_w, sub_w)


def _sel_sample_kernel(pt_ref, *refs, n_pages, s, spb):
    pg_refs = refs[:spb * n_pages]
    q_ref, selm_ref, new_ref, o_ref = refs[spb * n_pages:]
    past = n_pages * PAGE_SIZE
    new_mask = _new_token_mask(R_B * s, s)
    oh_past = _sel_onehot(past, 0)
    new_blk = past // SEL_BLOCK
    oh_new = jnp.where((_iota((LANES, LANES), 1) == new_blk) & (_iota((LANES, LANES), 0) < s), 1.0, 0.0)
    oh_new = oh_new.astype(BF16)
    scale = DH_B ** -0.5
    for b in range(spb):
        pages = pg_refs[b * n_pages:(b + 1) * n_pages]
        for g in range(G_B):
            kcols = slice(g * LANES, (g + 1) * LANES)
            vcols = slice((G_B + g) * LANES, (G_B + g + 1) * LANES)
            selm = selm_ref[b, g]
            q2 = jnp.concatenate(
                [jnp.concatenate([(q_ref[b, :, (g * R_B + r) * LANES:(g * R_B + r + 1) * LANES] * scale).astype(BF16),
                                  selm], axis=-1) for r in range(R_B)], axis=0)
            kpast = jnp.concatenate([_slot_rows(r, g, PAGE_SIZE).astype(BF16) for r in pages], axis=0)
            vpast = jnp.concatenate([_slot_rows(r, G_B + g, PAGE_SIZE).astype(BF16) for r in pages], axis=0)
            kpast = jnp.concatenate([kpast, oh_past], axis=-1)
            knew = jnp.concatenate([_pad_rows(new_ref[b, :, kcols], LANES).astype(BF16), oh_new], axis=-1)
            vnew = _pad_rows(new_ref[b, :, vcols], LANES).astype(BF16)
            o2 = _attend_pieces(q2, [(kpast, vpast, None), (knew, vnew, new_mask)])
            for r in range(R_B):
                o_ref[b, :, (g * R_B + r) * LANES:(g * R_B + r + 1) * LANES] = o2[r * s:(r + 1) * s]


def _sel_attn_sample(qb, selm, sel_new, cache, page_table):
    n, s, _ = qb.shape
    n_pages = page_table.shape[1]
    assert s <= SEL_BLOCK and (n_pages * PAGE_SIZE) % SEL_BLOCK == 0
    spb = math.gcd(n, 2)
    kern = functools.partial(_sel_sample_kernel, n_pages=n_pages, s=s, spb=spb)
    row_spec = pl.BlockSpec((spb, s, GROUP_W), lambda b, pt: (b, 0, 0))
    page_specs = _page_specs(n_pages, PAGE_SIZE * SLOTS, LANES, spb)
    grid_spec = pltpu.PrefetchScalarGridSpec(
        num_scalar_prefetch=1,
        grid=(n // spb,),
        in_specs=page_specs
        + [row_spec, pl.BlockSpec((spb, G_B, s, LANES), lambda b, pt: (b, 0, 0, 0)), row_spec],
        out_specs=row_spec,
    )
    return pl.pallas_call(
        kern,
        grid_spec=grid_spec,
        out_shape=jax.ShapeDtypeStruct((n, s, GROUP_W), F32),
        compiler_params=_cparams(("parallel",)),
        name="nsa_sel_sample",
    )(page_table, *([cache] * (spb * n_pages)), qb, selm, sel_new)


def _win_sample_kernel(q_ref, buf_ref, new_ref, o_ref, wout_ref, *, s, wb, spb):
    new_mask = _new_token_mask(R_B * s, s)
    r = _iota((R_B * s, wb), 0) % s
    buf_mask = _iota((R_B * s, wb), 1) > r + (wb - WINDOW)
    scale = DH_B ** -0.5
    keep = (wb - s) * SLOTS
    for b in range(spb):
        for g in range(G_B):
            kcols = slice(g * LANES, (g + 1) * LANES)
            vcols = slice((G_B + g) * LANES, (G_B + g + 1) * LANES)
            q2 = jnp.concatenate(
                [(q_ref[b, :, (g * R_B + r_) * LANES:(g * R_B + r_ + 1) * LANES] * scale).astype(BF16)
                 for r_ in range(R_B)], axis=0)
            kbuf = buf_ref[b, pl.ds(g, wb, stride=SLOTS), :].astype(BF16)
            vbuf = buf_ref[b, pl.ds(G_B + g, wb, stride=SLOTS), :].astype(BF16)
            knew = _pad_rows(new_ref[b, :, kcols], LANES).astype(BF16)
            vnew = _pad_rows(new_ref[b, :, vcols], LANES).astype(BF16)
            o2 = _attend_pieces(q2, [(kbuf, vbuf, buf_mask), (knew, vnew, new_mask)])
            for r_ in range(R_B):
                o_ref[b, :, (g * R_B + r_) * LANES:(g * R_B + r_ + 1) * LANES] = o2[r_ * s:(r_ + 1) * s]
        wout_ref[b, 0:keep] = buf_ref[b, s * SLOTS:wb * SLOTS]
        for slot in range(SLOTS):
            wout_ref[b, pl.ds(keep + slot, s, stride=SLOTS), :] = new_ref[b, :, slot * LANES:(slot + 1) * LANES]


def _win_attn_sample(qb, win_buf, win_new, past):
    n, s, _ = qb.shape
    wb = win_buf.shape[1] // SLOTS
    assert past >= wb and wb % 8 == 0 and s % 8 == 0
    spb = math.gcd(n, 4)
    kern = functools.partial(_win_sample_kernel, s=s, wb=wb, spb=spb)
    row_spec = pl.BlockSpec((spb, s, GROUP_W), lambda b: (b, 0, 0))
    buf_spec = pl.BlockSpec((spb, wb * SLOTS, LANES), lambda b: (b, 0, 0))
    return pl.pallas_call(
        kern,
        grid=(n // spb,),
        in_specs=[row_spec, buf_spec, row_spec],
        out_specs=[row_spec, buf_spec],
        out_shape=[jax.ShapeDtypeStruct((n, s, GROUP_W), F32), jax.ShapeDtypeStruct((n, wb * SLOTS, LANES), F32)],
        compiler_params=_cparams(("parallel",)),
        name="nsa_win_sample",
    )(qb, win_buf, win_new)


def _mlp_value(y, sh, sc, g, nw, w1_ref, w2_ref, fw, final):
    h = _norm_mod(y, nw, sh, sc).reshape(-1, D_MODEL).astype(BF16)
    a = jnp.maximum(_dot(h, w1_ref[...]), 0.0)
    out = _dot((a * a).astype(BF16), w2_ref[...])
    y2 = y + g * out.reshape(y.shape)
    return _rms(y2, fw) if final else y2


def _mlp_specs(til, mlp):
    nw, w1, w2, fw, _ = mlp
    specs = [til.mod_spec(3), til.mod_spec(4), til.mod_spec(5), _full_spec((1, D_MODEL)),
             _const_spec(w1.shape), _const_spec(w2.shape), _full_spec((1, D_MODEL))]
    return specs, [nw, w1, w2, fw]


def _even_out_kernel(y_ref, g1_ref, oa_ref, oc_ref, os_ref, ow_ref, gt_ref, w_ref,
                     sh_ref, sc_ref, g2_ref, nw_ref, w1_ref, w2_ref, fw_ref, o_ref, *, final):
    gt = gt_ref[...]
    parts = [oa_ref[...].astype(BF16)]
    for hb in range(H_B):
        sl = slice(hb * LANES, (hb + 1) * LANES)
        ob = (gt[:, 3 * hb:3 * hb + 1] * oc_ref[:, sl] + gt[:, 3 * hb + 1:3 * hb + 2] * os_ref[:, sl]
              + gt[:, 3 * hb + 2:3 * hb + 3] * ow_ref[:, sl])
        parts.append(ob.astype(BF16))
    out = _dot(jnp.concatenate(parts, axis=-1), w_ref[...])
    y = y_ref[...]
    y1 = y + g1_ref[0] * out.reshape(y.shape)
    o_ref[...] = _mlp_value(y1, sh_ref[0], sc_ref[0], g2_ref[0], nw_ref[...], w1_ref, w2_ref, fw_ref[...], final)


def _even_out(til, y, mods, oa, oc, os_, ow, gt, w_out, mlp):
    mlp_specs, mlp_args = _mlp_specs(til, mlp)
    return pl.pallas_call(
        functools.partial(_even_out_kernel, final=mlp[4]),
        grid=til.grid,
        in_specs=[til.x_spec(D_MODEL), til.mod_spec(2), til.flat_spec(GROUP_W), til.flat_spec(GROUP_W),
                  til.flat_spec(GROUP_W), til.flat_spec(GROUP_W), til.flat_spec(LANES), _const_spec(w_out.shape)] + mlp_specs,
        out_specs=til.x_spec(D_MODEL),
        out_shape=jax.ShapeDtypeStruct(y.shape, F32),
        compiler_params=_cparams(("parallel", "parallel")),
        name="even_out_mlp",
    )(y, mods, oa, oc, os_, ow, gt, w_out, mods, mods, mods, *mlp_args)


def _gla_inproj_kernel(x_ref, sh_ref, sc_ref, nw_ref, w_ref, wgl_ref, wgate_ref, bgate_ref,
                       q_ref, k_ref, v_ref, r_ref, la_ref):
    h = _norm_mod(x_ref[...], nw_ref[...], sh_ref[0], sc_ref[0]).reshape(-1, D_MODEL).astype(BF16)
    nk = H_C * DK_C
    nv = H_C * DV_C
    q_ref[...] = _dot(h, w_ref[:, 0:nk]) * (DK_C ** -0.5)
    k_ref[...] = _dot(h, w_ref[:, nk:2 * nk])
    v_ref[...] = _dot(h, w_ref[:, 2 * nk:2 * nk + nv])
    r_ref[...] = _dot(h, w_ref[:, 2 * nk + nv:2 * nk + 2 * nv])
    gl = _dot(h, wgl_ref[...])
    x = _dot(gl.astype(BF16), wgate_ref[...]) + bgate_ref[...]
    log_sig = jnp.minimum(x, 0.0) - jnp.log1p(jnp.exp(-jnp.abs(x)))
    la_ref[...] = log_sig / GATE_TAU


def _gla_inproj(til, x, mods, nw, w_main, w_gl, w_gate, b_gate):
    rows_total = til.nb * til.r
    widths = (H_C * DK_C, H_C * DK_C, H_C * DV_C, H_C * DV_C, H_C * DK_C)
    return pl.pallas_call(
        _gla_inproj_kernel,
        grid=til.grid,
        in_specs=[til.x_spec(D_MODEL), til.mod_spec(0), til.mod_spec(1), _full_spec((1, D_MODEL)),
                  _const_spec(w_main.shape), _const_spec(w_gl.shape), _const_spec(w_gate.shape),
                  _full_spec(b_gate.shape)],
        out_specs=[til.flat_spec(c) for c in widths],
        out_shape=[jax.ShapeDtypeStruct((rows_total, c), F32) for c in widths],
        compiler_params=_cparams(("parallel", "parallel")),
        name="gla_inproj",
    )(x, mods, mods, nw, w_main, w_gl, w_gate, b_gate)


def _cumsum_rows(g):
    c = g.shape[0]
    row = _iota((c, 1), 0)
    b = g
    shift = 1
    while shift < c:
        b = b + jnp.where(row >= shift, pltpu.roll(b, shift, 0), 0.0)
        shift *= 2
    return b


def _gla_prep(q, k, g, c, sub, n_real):
    b = _cumsum_rows(g)
    qe = (q * jnp.exp(b)).astype(BF16)
    lane_c = _iota((sub, c), 1)
    row_s = _iota((sub, c), 0)
    att_rows = []
    for blk in range(c // sub):
        lo = blk * sub
        qi, ki, bi = q[lo:lo + sub], k[lo:lo + sub], b[lo:lo + sub]
        diag = jnp.zeros((sub, c), F32)
        for jj in range(min(sub, max(n_real - lo, 0))):
            e = jnp.exp(jnp.minimum(bi - bi[jj:jj + 1], 0.0))
            col = jnp.sum(qi * ki[jj:jj + 1] * e, axis=-1, keepdims=True)
            diag = jnp.where(lane_c == lo + jj, col, diag)
        att = jnp.where(lane_c - lo <= row_s, diag, 0.0)
        if blk > 0:
            bs = b[lo - 1:lo]
            q_in = qi * jnp.exp(bi - bs)
            k_out = k * jnp.exp(jnp.minimum(bs - b, 0.0))
            att = jnp.where(lane_c < lo, _dot_nt(q_in.astype(BF16), k_out.astype(BF16)), att)
        att_rows.append(att)
    att = att_rows[0] if len(att_rows) == 1 else jnp.concatenate(att_rows, axis=0)
    bl = b[c - 1:c]
    kd = (k * jnp.exp(bl - b)).astype(BF16)
    eye = _iota((DK_C, DK_C), 0) == _iota((DK_C, DK_C), 1)
    decay = jnp.sum(jnp.where(eye, jnp.exp(bl), 0.0), axis=-1, keepdims=True)
    return qe, att, kd, decay


def _gla_prep_bounded(q, k, g, c, sub, n_real):
    del sub, n_real
    b = _cumsum_rows(g)
    r = b[0:1]
    bl = b[c - 1:c]
    q_in = q * jnp.exp(b - r)
    k_out = k * jnp.exp(r - b)
    att = jnp.where(_iota((c, c), 0) >= _iota((c, c), 1), _dot_nt(q_in.astype(BF16), k_out.astype(BF16)), 0.0)
    qe = (q_in * jnp.exp(r)).astype(BF16)
    kd = (k_out * jnp.exp(bl - r)).astype(BF16)
    eye = _iota((DK_C, DK_C), 0) == _iota((DK_C, DK_C), 1)
    decay = jnp.sum(jnp.where(eye, jnp.exp(bl), 0.0), axis=-1, keepdims=True)
    return qe, att, kd, decay


def _gla_apply(state, prep, v):
    qe, att, kd, decay = prep
    o = _dot(qe, state.astype(BF16)) + _dot(att, v)
    return o, decay * state + _dot_tn(kd, v.astype(BF16))


def _gla_rec_kernel(*refs, tt, c, sub, hp, nseq, has_s0):
    if has_s0:
        q_ref, k_ref, v_ref, g_ref, s0_ref, o_ref, sfin_ref, s_ref = refs
    else:
        q_ref, k_ref, v_ref, g_ref, o_ref, sfin_ref, s_ref = refs
    t = pl.program_id(2)

    @pl.when(t == 0)
    def _():
        for bi in range(nseq):
            for hh in range(hp):
                s_ref[bi * hp + hh] = s0_ref[bi, hh] if has_s0 else jnp.zeros((DK_C, DV_C), F32)

    def kcols(hh):
        return slice(hh * DK_C, (hh + 1) * DK_C)

    def vcols(hh):
        return slice(hh * DV_C, (hh + 1) * DV_C)

    def run(prep_fn):
        if tt < c:
            pad = lambda x: _pad_rows(x, c)
            for bi in range(nseq):
                for hh in range(hp):
                    prep = prep_fn(pad(q_ref[bi, :, kcols(hh)]), pad(k_ref[bi, :, kcols(hh)]),
                                   pad(g_ref[bi, :, kcols(hh)]), c, sub, tt)
                    si = bi * hp + hh
                    o, s_ref[si] = _gla_apply(s_ref[si], prep, pad(v_ref[bi, :, vcols(hh)]))
                    o_ref[bi, :, vcols(hh)] = o[:tt]
            return
        assert nseq == 1
        per_trip = math.gcd(tt // c, 4)

        def body(ci, carry):
            for hh in range(hp):
                rows = [pl.ds(pl.multiple_of((ci * per_trip + u) * c, c), c) for u in range(per_trip)]
                preps = [prep_fn(q_ref[0, r, kcols(hh)], k_ref[0, r, kcols(hh)], g_ref[0, r, kcols(hh)],
                                 c, sub, c) for r in rows]
                state = s_ref[hh]
                for r, prep in zip(rows, preps):
                    o_ref[0, r, vcols(hh)], state = _gla_apply(state, prep, v_ref[0, r, vcols(hh)])
                s_ref[hh] = state
            return carry
        lax.fori_loop(0, tt // c // per_trip, body, 0)

    if tt > c:
        g_all = g_ref[0]
        chunk_decay = -jnp.sum(g_all.reshape(tt // c, c, g_all.shape[-1]), axis=1)
    else:
        chunk_decay = -jnp.sum(g_ref[...], axis=1)
    bounded = jnp.max(chunk_decay) <= GLA_SAFE_DECAY

    @pl.when(bounded)
    def _():
        run(_gla_prep_bounded)

    @pl.when(jnp.logical_not(bounded))
    def _():
        run(_gla_prep)

    @pl.when(t == pl.num_programs(2) - 1)
    def _():
        for bi in range(nseq):
            for hh in range(hp):
                sfin_ref[bi, hh] = s_ref[bi * hp + hh]


def _gla_recurrence(q, k, v, g, s0, tt, c, sub, hp):
    n, t, _ = q.shape
    nseq = math.gcd(n, 2) if tt < c else 1
    kern = functools.partial(_gla_rec_kernel, tt=tt, c=c, sub=sub, hp=hp, nseq=nseq, has_s0=s0 is not None)
    kspec = pl.BlockSpec((nseq, tt, hp * DK_C), lambda b, h, i: (b, i, h))
    vspec = pl.BlockSpec((nseq, tt, hp * DV_C), lambda b, h, i: (b, i, h))
    sspec = pl.BlockSpec((nseq, hp, DK_C, DV_C), lambda b, h, i: (b, h, 0, 0))
    in_specs = [kspec, kspec, vspec, kspec]
    args = [q, k, v, g]
    if s0 is not None:
        in_specs.append(sspec)
        args.append(s0)
    return pl.pallas_call(
        kern,
        grid=(n // nseq, H_C // hp, t // tt),
        in_specs=in_specs,
        out_specs=[vspec, sspec],
        out_shape=[jax.ShapeDtypeStruct((n, t, H_C * DV_C), F32),
                   jax.ShapeDtypeStruct((n, H_C, DK_C, DV_C), F32)],
        scratch_shapes=[pltpu.VMEM((nseq * hp, DK_C, DV_C), F32)],
        compiler_params=_cparams(("parallel", "parallel", "arbitrary")),
        name="gla_recurrence",
    )(*args)


def _gla_out_kernel(y_ref, g1_ref, o_ref_in, r_ref, gnw_ref, w_ref,
                    sh_ref, sc_ref, g2_ref, nw_ref, w1_ref, w2_ref, fw_ref, out_ref, *, final):
    parts = []
    for h in range(H_C):
        sl = slice(h * DV_C, (h + 1) * DV_C)
        r = r_ref[:, sl]
        parts.append((_rms(o_ref_in[:, sl], gnw_ref[...]) * (r * jax.nn.sigmoid(r))).astype(BF16))
    out = _dot(jnp.concatenate(parts, axis=-1), w_ref[...])
    y = y_ref[...]
    y1 = y + g1_ref[0] * out.reshape(y.shape)
    out_ref[...] = _mlp_value(y1, sh_ref[0], sc_ref[0], g2_ref[0], nw_ref[...], w1_ref, w2_ref, fw_ref[...], final)


def _gla_out(til, y, mods, o, r, gnw, w_out, mlp):
    mlp_specs, mlp_args = _mlp_specs(til, mlp)
    return pl.pallas_call(
        functools.partial(_gla_out_kernel, final=mlp[4]),
        grid=til.grid,
        in_specs=[til.x_spec(D_MODEL), til.mod_spec(2), til.flat_spec(H_C * DV_C), til.flat_spec(H_C * DV_C),
                  _full_spec(gnw.shape), _const_spec(w_out.shape)] + mlp_specs,
        out_specs=til.x_spec(D_MODEL),
        out_shape=jax.ShapeDtypeStruct(y.shape, F32),
        compiler_params=_cparams(("parallel", "parallel")),
        name="gla_out_mlp",
    )(y, mods, o, r, gnw, w_out, mods, mods, mods, *mlp_args)


def _rope_tables(pos, d):
    inv = ROPE_THETA ** (-jnp.arange(0, d, 2, dtype=F32) / d)
    ang = pos.astype(F32)[:, None] * inv[None, :]
    cos, sin = jnp.cos(ang), jnp.sin(ang)
    rep = LANES // d
    c = jnp.tile(jnp.concatenate([cos, cos], axis=-1), (1, rep))
    s = jnp.tile(jnp.concatenate([-sin, sin], axis=-1), (1, rep))
    return c, s


def _rope_tables_t(pos, d):
    inv = ROPE_THETA ** (-jnp.arange(0, d, 2, dtype=F32) / d)
    ang = inv[:, None] * pos.astype(F32)[None, :]
    return jnp.cos(ang), jnp.sin(ang)


def _mods_for(mod_l, lo, hi):
    nb = hi - lo
    return mod_l[lo:hi].reshape(nb, 6, D_MODEL).transpose(1, 0, 2).reshape(6, nb, 1, D_MODEL)


def _prompt_tile_rows(t):
    return math.gcd(t, 256)


def _even_layer(yp, ys, mods_p, mods_s, caches, page_table, wts, lam_init, til_p, til_s, mlp):
    (c_dk, c_dv, c_cmp, c_sel, win_buf) = caches
    n, t, _ = yp.shape
    ns, s, _ = ys.shape
    n_pages = page_table.shape[1]
    past = n_pages * PAGE_SIZE
    w_in = wts["w_in"]
    n_main = w_in.shape[1] - 3 * H_B
    w_main = w_in[:, :n_main].astype(BF16)
    w_gate = jnp.pad(w_in[:, n_main:], ((0, 0), (0, LANES - 3 * H_B))).astype(BF16)
    nw1 = wts["norm1"].reshape(1, D_MODEL)
    lam_w = wts["lam_w"]
    sub_w = wts["sub_w"].reshape(1, DV_A)
    wc, pe, w2c = _compress_weights(wts["cmp_pe"], wts["cmp_w1"], wts["cmp_w2"])
    w_out = wts["w_out"].astype(BF16)

    tabs_p = _rope_tables(jnp.arange(t), DK_A) + _rope_tables(jnp.arange(t), DH_B)
    kt_tabs = _rope_tables_t(jnp.arange(t), DK_A)
    qa, kat, va, qb, cmp_kv, sel_kv, win_kv, gt = _even_inproj(til_p, yp, mods_p, nw1, w_main, w_gate, tabs_p, kt_tabs)
    r3 = lambda a: a.reshape(n, t, a.shape[-1])
    tq = _prompt_tile_rows(t)
    tq_attn = math.gcd(t, 2 * tq)
    oa = _diff_attn_prompt(r3(qa), kat, r3(va), lam_w, sub_w, lam_init, tq_attn, tq)
    ccmp = _compress_prompt(r3(cmp_kv), wc, pe, w2c)
    n_cmp = (t - CMP_LEN) // CMP_STRIDE + 1
    n_sel = -(-t // SEL_BLOCK)
    o_cmp, selm = _cmp_topk(r3(qb), ccmp, nb=1, rq=tq, n_cmp=n_cmp, n_sel=n_sel, pos_base=0)
    o_sel = _nsa_attn_prompt(r3(qb), r3(sel_kv), selm, tq_attn, tq)
    o_win = _nsa_attn_prompt(r3(qb), r3(win_kv), None, tq_attn, tq)
    f2 = lambda a: a.reshape(n * t, a.shape[-1])
    yp = _even_out(til_p, yp, mods_p, f2(oa), f2(o_cmp), f2(o_sel), f2(o_win), gt, w_out, mlp)
    wl = min(WINDOW, t)
    ka_state = jnp.transpose(kat.reshape(n, H_A, 2, DK_A, t), (0, 4, 1, 2, 3))
    st_p = (ka_state, r3(va).reshape(n, t, H_A, DV_A),
            r3(cmp_kv).reshape(n, t, 2, G_B, DH_B), r3(sel_kv).reshape(n, t, 2, G_B, DH_B),
            r3(win_kv)[:, t - wl:].reshape(n, wl, 2, G_B, DH_B))

    pos_s = past + jnp.arange(s)
    tabs_s = tuple(jnp.tile(x, (til_s.b, 1)) for x in _rope_tables(pos_s, DK_A) + _rope_tables(pos_s, DH_B))
    qa, ka, va, qb, cmp_kv, sel_kv, win_kv, gt = _even_inproj(til_s, ys, mods_s, nw1, w_main, w_gate, tabs_s)
    r3 = lambda a: a.reshape(ns, s, a.shape[-1])
    slot_cache = lambda c: c.reshape(c.shape[0], PAGE_SIZE * SLOTS, LANES)
    c_dkt = jnp.transpose(c_dk, (0, 2, 3, 4, 1)).reshape(c_dk.shape[0], H_A * 2 * DK_A, PAGE_SIZE)
    oa = _diff_attn_sample(r3(qa), r3(ka), r3(va), c_dkt, slot_cache(c_dv), page_table, lam_w, sub_w, lam_init)
    ccmp = _compress_sample(slot_cache(c_cmp), page_table, wc, pe, w2c)
    total = past + s
    n_cmp = (total - CMP_LEN) // CMP_STRIDE + 1
    n_sel = -(-total // SEL_BLOCK)
    assert n_cmp <= ccmp.shape[3] and n_sel <= LANES
    nb = math.gcd(ns, LANES // s)
    o_cmp, selm = _cmp_topk(r3(qb), ccmp, nb=nb, rq=s, n_cmp=n_cmp, n_sel=n_sel, pos_base=past)
    o_sel = _sel_attn_sample(r3(qb), selm, r3(sel_kv), slot_cache(c_sel), page_table)
    wb = win_buf.shape[1]
    o_win, win_out = _win_attn_sample(r3(qb), win_buf.reshape(ns, wb * SLOTS, LANES), r3(win_kv), past)
    f2 = lambda a: a.reshape(ns * s, a.shape[-1])
    ys = _even_out(til_s, ys, mods_s, f2(oa), f2(o_cmp), f2(o_sel), f2(o_win), gt, w_out, mlp)
    st_s = (r3(ka).reshape(ns, s, H_A, 2, DK_A), r3(va).reshape(ns, s, H_A, DV_A),
            r3(cmp_kv).reshape(ns, s, 2, G_B, DH_B), r3(sel_kv).reshape(ns, s, 2, G_B, DH_B),
            win_out.reshape(ns, wb, 2, G_B, DH_B))
    return yp, ys, st_p, st_s


def _odd_layer(yp, ys, mods_p, mods_s, s0, wts, til_p, til_s, mlp):
    w_in = wts["w_in"]
    n_main = w_in.shape[1] - GATE_RANK
    w_main = w_in[:, :n_main].astype(BF16)
    w_gl = jnp.pad(w_in[:, n_main:], ((0, 0), (0, LANES - GATE_RANK))).astype(BF16)
    w_gate = jnp.pad(wts["w_gate"], ((0, LANES - GATE_RANK), (0, 0))).astype(BF16)
    b_gate = wts["b_gate"].reshape(1, -1)
    nw1 = wts["norm1"].reshape(1, D_MODEL)
    gnw = wts["gnorm"].reshape(1, DV_C)
    w_out = wts["w_out"].astype(BF16)
    outs = []
    for y, mods, til, state in ((yp, mods_p, til_p, None), (ys, mods_s, til_s, s0)):
        n, t, _ = y.shape
        q, k, v, r, la = _gla_inproj(til, y, mods, nw1, w_main, w_gl, w_gate, b_gate)
        r3 = lambda a: a.reshape(n, t, a.shape[-1])
        c = math.gcd(t, GLA_CHUNK)
        if c >= GLA_SUB:
            tt, cc, sub = math.gcd(t, 8 * c), c, GLA_SUB
        else:
            tt, cc, sub = t, GLA_SUB, GLA_SUB
        o, s_fin = _gla_recurrence(r3(q), r3(k), r3(v), r3(la), state, tt, cc, sub, H_C)
        y = _gla_out(til, y, mods, o.reshape(n * t, -1), r, gnw, w_out, mlp)
        outs.append((y, s_fin))
    return outs[0][0], outs[1][0], outs[0][1], outs[1][1]


def kernel(x_prompt, x_sample, c_prompt, c_sample, cache_diff_k, cache_diff_v, cache_cmp_kv, cache_sel_kv,
           state_win_kv, state_gla, page_table, norm1_w, norm2_w, ada_w, ada_b, even_w_in, even_w_out,
           diff_lambda_w, diff_subln_w, cmp_pe, cmp_w1, cmp_w2, gla_w_in, gla_w_gate, gla_b_gate, gla_norm_w,
           gla_w_out, mlp_w1, mlp_w2, final_norm_w):
    depth = ada_w.shape[0]
    n, t, _ = x_prompt.shape
    ns, s, _ = x_sample.shape
    til_p = _Tiling(n, t, 1, _prompt_tile_rows(t))
    til_s = _Tiling(ns, s, math.gcd(ns, 256 // s), s)

    pad = (-(n + ns)) % 8
    c_all = jnp.concatenate([c_prompt, c_sample, jnp.zeros((pad, D_MODEL), F32)], axis=0)
    mod = _adaln(c_all, ada_w, ada_b)

    yp, ys = x_prompt, x_sample
    st_p = [[] for _ in range(6)]
    st_s = [[] for _ in range(6)]
    fw = final_norm_w.reshape(1, D_MODEL)
    for l in range(depth):
        mods_p = _mods_for(mod[l], 0, n)
        mods_s = _mods_for(mod[l], n, n + ns)
        mlp = (norm2_w[l].reshape(1, D_MODEL), mlp_w1[l].astype(BF16), mlp_w2[l].astype(BF16), fw, l == depth - 1)
        if l % 2 == 0:
            e = l // 2
            lam_init = 0.8 - 0.6 * math.exp(-0.3 * l)
            wts = dict(w_in=even_w_in[e], w_out=even_w_out[e], lam_w=diff_lambda_w[e], sub_w=diff_subln_w[e],
                       cmp_pe=cmp_pe[e], cmp_w1=cmp_w1[e], cmp_w2=cmp_w2[e], norm1=norm1_w[l])
            caches = (cache_diff_k[e], cache_diff_v[e], cache_cmp_kv[e], cache_sel_kv[e], state_win_kv[e])
            yp, ys, sp, ss = _even_layer(yp, ys, mods_p, mods_s, caches, page_table, wts, lam_init, til_p, til_s, mlp)
            for i in range(5):
                st_p[i].append(sp[i])
                st_s[i].append(ss[i])
        else:
            o = l // 2
            wts = dict(w_in=gla_w_in[o], w_gate=gla_w_gate[o], b_gate=gla_b_gate[o], gnorm=gla_norm_w[o],
                       w_out=gla_w_out[o], norm1=norm1_w[l])
            yp, ys, gp, gs = _odd_layer(yp, ys, mods_p, mods_s, state_gla[o], wts, til_p, til_s, mlp)
            st_p[5].append(gp)
            st_s[5].append(gs)
    outs_p = [jnp.stack(x, axis=0) for x in st_p]
    outs_s = [jnp.stack(x, axis=0) for x in st_s]
    return (yp, ys, *outs_p, *outs_s)
```
